```python
import jax
import jax.numpy as jnp
from jax import lax
import numpy as np

D_MODEL = 1024
BATCH = 8
SEQ = 4096
DEPTH = 2

GRID_W = 64
CTX_LEN = 256
HEAD_DIM = 64
NA_HEADS = 8
NA_WIDTH = NA_HEADS * HEAD_DIM
NA_WIN_ROWS = 8
NA_WIN_COLS = 16
NA_QBLOCK_COLS = 16
NA_KBLOCK_COLS = 2 * NA_WIN_COLS
LRU_WIDTH = D_MODEL // 2
LRU_BLOCKS = 8
LRU_BLOCK = LRU_WIDTH // LRU_BLOCKS
LRU_CONV_W = 4
LRU_C = 8.0
MIX_WIDTH = NA_WIDTH + LRU_WIDTH
MIX_IN_WIDTH = 3 * NA_WIDTH + 2 * LRU_WIDTH
FNET_GROUPS = 4
N_EXPERTS = 16
N_EXPERT_GROUPS = 4
EXPERTS_PER_GROUP = N_EXPERTS // N_EXPERT_GROUPS
TOP_K = 2
D_FF_EXPERT = 512
N_EVEN_LAYERS = (DEPTH + 1) // 2
N_ODD_LAYERS = DEPTH // 2
RMS_EPS = 1e-6
MASK_VALUE = -1e30

kernel_name = "hybrid_na_rglru_fnet_grouped_moe"


def rmsnorm(x, gain):
    xf = x.astype(jnp.float32)
    y = xf * lax.rsqrt(jnp.mean(xf * xf, axis=-1, keepdims=True) + RMS_EPS)
    return (y * gain.astype(jnp.float32)).astype(x.dtype)


def modulate(h, shift, scale):
    return h * (1 + scale) + shift


def na_tables(rows):
    kr = min(NA_WIN_ROWS, rows)
    r = np.arange(rows)
    row_start = np.clip(r - kr // 2, 0, rows - kr)
    key_rows = row_start[:, None] + np.arange(kr)
    ncb = GRID_W // NA_QBLOCK_COLS
    c0 = np.arange(ncb) * NA_QBLOCK_COLS
    kb = np.clip(c0 - NA_WIN_COLS // 2, 0, GRID_W - NA_KBLOCK_COLS)
    key_cols = kb[:, None] + np.arange(NA_KBLOCK_COLS)
    q_cols = c0[:, None] + np.arange(NA_QBLOCK_COLS)
    col_start = np.clip(q_cols - NA_WIN_COLS // 2, 0, GRID_W - NA_WIN_COLS)
    idx = (key_rows[:, None, :, None] * GRID_W + key_cols[None, :, None, :]).reshape(rows, ncb, kr * NA_KBLOCK_COLS)
    kc = key_cols[:, None, :]
    valid = (kc >= col_start[..., None]) & (kc < col_start[..., None] + NA_WIN_COLS)
    dr = key_rows - r[:, None] + (NA_WIN_ROWS - 1)
    dc = np.clip(kc - q_cols[..., None], 1 - NA_WIN_COLS, NA_WIN_COLS - 1) + (NA_WIN_COLS - 1)
    return idx.astype(np.int32), valid, dr, dc, kr


def neighbourhood_attention(q, k, v, k_ctx, v_ctx, rpb):
    B, N, H, dh = q.shape
    rows = N // GRID_W
    ncb = GRID_W // NA_QBLOCK_COLS
    idx, valid, dr, dc, kr = na_tables(rows)
    n_keys = kr * NA_KBLOCK_COLS
    bias = rpb.astype(jnp.float32)[:, dr[:, None, None, :, None], dc[None, :, :, None, :]]
    bias = jnp.where(valid[None, None, :, :, None, :], bias, MASK_VALUE)
    bias = bias.transpose(1, 0, 2, 3, 4, 5).reshape(rows, H, ncb, NA_QBLOCK_COLS, n_keys)
    q_rows = q.reshape(B, rows, ncb, NA_QBLOCK_COLS, H, dh).transpose(1, 0, 2, 3, 4, 5)

    def row_block(args):
        q_r, idx_r, bias_r = args
        k_g = jnp.take(k, idx_r, axis=1)
        v_g = jnp.take(v, idx_r, axis=1)
        s_win = jnp.einsum("bcqhd,bckhd->bhcqk", q_r, k_g, preferred_element_type=jnp.float32) + bias_r[None]
        s_ctx = jnp.einsum("bcqhd,bjhd->bhcqj", q_r, k_ctx, preferred_element_type=jnp.float32)
        p = jax.nn.softmax(jnp.concatenate([s_win, s_ctx], axis=-1), axis=-1).astype(v.dtype)
        return (jnp.einsum("bhcqk,bckhd->bcqhd", p[..., :n_keys], v_g)
                + jnp.einsum("bhcqj,bjhd->bcqhd", p[..., n_keys:], v_ctx))

    o = lax.map(row_block, (q_rows, jnp.asarray(idx), bias))
    return o.transpose(1, 0, 2, 3, 4, 5).reshape(B, N, H * dh)


def centred_depthwise_conv(x, w, b):
    T = x.shape[1]
    lo = LRU_CONV_W // 2
    xp = jnp.pad(x, ((0, 0), (lo, LRU_CONV_W - 1 - lo), (0, 0)))
    y = b
    for tap in range(LRU_CONV_W):
        y = y + w[tap] * xp[:, tap:tap + T]
    return y


def rglru_coeffs(xc, w_r, b_r, w_i, b_i, lam):
    B, T, W = xc.shape
    xb = xc.reshape(B, T, LRU_BLOCKS, LRU_BLOCK)
    r = jax.nn.sigmoid(jnp.einsum("bthi,hij->bthj", xb, w_r.astype(jnp.float32)).reshape(B, T, W) + b_r.astype(jnp.float32))
    i = jax.nn.sigmoid(jnp.einsum("bthi,hij->bthj", xb, w_i.astype(jnp.float32)).reshape(B, T, W) + b_i.astype(jnp.float32))
    log_a = -LRU_C * r * jax.nn.softplus(-lam.astype(jnp.float32))
    return jnp.exp(log_a), jnp.sqrt(-jnp.expm1(2.0 * log_a)) * (i * xc)


def linear_scan(a, b):
    def combine(left, right):
        return left[0] * right[0], right[0] * left[1] + right[1]
    return lax.associative_scan(combine, (a, b), axis=1)


def bidirectional_rglru(x_lat, x_ctx, w_r, b_r, w_i, b_i, lam):
    x_lat = x_lat.astype(jnp.float32)
    x_ctx = x_ctx.astype(jnp.float32)
    outs = []
    for d in range(2):
        a_c, b_c = rglru_coeffs(x_ctx, w_r[d], b_r[d], w_i[d], b_i[d], lam[d])
        a_l, b_l = rglru_coeffs(x_lat, w_r[d], b_r[d], w_i[d], b_i[d], lam[d])
        if d == 1:
            a_c, b_c, a_l, b_l = (jnp.flip(t, axis=1) for t in (a_c, b_c, a_l, b_l))
        h0 = linear_scan(a_c, b_c)[1][:, -1]
        a_prefix, h_l = linear_scan(a_l, b_l)
        h = h_l + a_prefix * h0[:, None, :]
        if d == 1:
            h = jnp.flip(h, axis=1)
        outs.append(h)
    return outs[0] + outs[1]


def grouped_moe(h, router_w, router_bias, w_gate, w_up, w_down):
    B, T, D = h.shape
    scores = jax.nn.sigmoid(jnp.einsum("btd,de->bte", h, router_w, preferred_element_type=jnp.float32))
    sel = scores + router_bias.astype(jnp.float32)
    grp = sel.reshape(B, T, N_EXPERT_GROUPS, EXPERTS_PER_GROUP)
    grp_score = lax.top_k(grp, TOP_K)[0].sum(-1)
    best = jnp.argmax(grp_score, axis=-1)
    in_group = (jnp.arange(N_EXPERTS) // EXPERTS_PER_GROUP)[None, None, :] == best[..., None]
    _, idx = lax.top_k(jnp.where(in_group, sel, -jnp.inf), TOP_K)
    w = jnp.take_along_axis(scores, idx, axis=-1)
    w = w / jnp.sum(w, axis=-1, keepdims=True)
    gates = jnp.sum(jax.nn.one_hot(idx, N_EXPERTS, dtype=jnp.float32) * w[..., None], axis=-2).astype(h.dtype)
    out = jnp.zeros_like(h)
    for e in range(N_EXPERTS):
        y = (jax.nn.silu(h @ w_gate[e]) * (h @ w_up[e])) @ w_down[e]
        out = out + gates[..., e:e + 1] * y
    return out


def setup_inputs(seed: int = 0) -> dict:
    key = jax.random.key(seed)
    keys = iter(jax.random.split(key, 32))
    D = D_MODEL

    def nrm(shape, scale):
        return jax.random.normal(next(keys), shape, jnp.float32) * scale

    x = nrm((BATCH, SEQ, D), 1.0)
    c = nrm((BATCH, D), 1.0)
    ctx = nrm((BATCH, CTX_LEN, D), 1.0)
    c_ctx = nrm((D,), 1.0)
    ada_w = nrm((DEPTH, D, 6 * D), 0.5 * D ** -0.5)
    ada_b = nrm((DEPTH, 6 * D), 0.02)
    norm_mix = 1.0 + nrm((DEPTH, D), 0.02)
    norm_ffn = 1.0 + nrm((DEPTH, D), 0.02)
    mix_w_in = nrm((N_EVEN_LAYERS, D, MIX_IN_WIDTH), D ** -0.5)
    mix_w_out = nrm((N_EVEN_LAYERS, MIX_WIDTH, D), MIX_WIDTH ** -0.5)
    na_q_norm = 1.0 + nrm((N_EVEN_LAYERS, HEAD_DIM), 0.02)
    na_k_norm = 1.0 + nrm((N_EVEN_LAYERS, HEAD_DIM), 0.02)
    na_rpb = nrm((N_EVEN_LAYERS, NA_HEADS, 2 * NA_WIN_ROWS - 1, 2 * NA_WIN_COLS - 1), 0.5)
    lru_conv_w = nrm((N_EVEN_LAYERS, LRU_CONV_W, LRU_WIDTH), LRU_CONV_W ** -0.5)
    lru_conv_b = nrm((N_EVEN_LAYERS, LRU_WIDTH), 0.02)
    lru_gate_r_w = nrm((N_EVEN_LAYERS, 2, LRU_BLOCKS, LRU_BLOCK, LRU_BLOCK), LRU_BLOCK ** -0.5)
    lru_gate_r_b = nrm((N_EVEN_LAYERS, 2, LRU_WIDTH), 0.02)
    lru_gate_i_w = nrm((N_EVEN_LAYERS, 2, LRU_BLOCKS, LRU_BLOCK, LRU_BLOCK), LRU_BLOCK ** -0.5)
    lru_gate_i_b = nrm((N_EVEN_LAYERS, 2, LRU_WIDTH), 0.02)
    a_pow_c = jax.random.uniform(next(keys), (N_EVEN_LAYERS, 2, LRU_WIDTH), jnp.float32, 0.9, 0.999)
    a0 = a_pow_c ** (1.0 / LRU_C)
    lru_lambda = jnp.log(a0) - jnp.log1p(-a0)
    fnet_w_out = nrm((N_ODD_LAYERS, D, D), D ** -0.5)
    router_w = nrm((D, N_EXPERTS), D ** -0.5)
    router_bias = nrm((N_EXPERTS,), 0.01)
    moe_w_gate = nrm((DEPTH, N_EXPERTS, D, D_FF_EXPERT), D ** -0.5)
    moe_w_up = nrm((DEPTH, N_EXPERTS, D, D_FF_EXPERT), D ** -0.5)
    moe_w_down = nrm((DEPTH, N_EXPERTS, D_FF_EXPERT, D), D_FF_EXPERT ** -0.5)
    return {"x": x, "c": c, "ctx": ctx, "c_ctx": c_ctx, "ada_w": ada_w, "ada_b": ada_b,
            "norm_mix": norm_mix, "norm_ffn": norm_ffn, "mix_w_in": mix_w_in, "mix_w_out": mix_w_out,
            "na_q_norm": na_q_norm, "na_k_norm": na_k_norm, "na_rpb": na_rpb,
            "lru_conv_w": lru_conv_w, "lru_conv_b": lru_conv_b,
            "lru_gate_r_w": lru_gate_r_w, "lru_gate_r_b": lru_gate_r_b,
            "lru_gate_i_w": lru_gate_i_w, "lru_gate_i_b": lru_gate_i_b, "lru_lambda": lru_lambda,
            "fnet_w_out": fnet_w_out, "router_w": router_w, "router_bias": router_bias,
            "moe_w_gate": moe_w_gate, "moe_w_up": moe_w_up, "moe_w_down": moe_w_down}


def reference(x, c, ctx, c_ctx, ada_w, ada_b, norm_mix, norm_ffn, mix_w_in, mix_w_out,
              na_q_norm, na_k_norm, na_rpb, lru_conv_w, lru_conv_b,
              lru_gate_r_w, lru_gate_r_b, lru_gate_i_w, lru_gate_i_b, lru_lambda,
              fnet_w_out, router_w, router_bias, moe_w_gate, moe_w_up, moe_w_down):
    B, N, D = x.shape
    silu_c = jax.nn.silu(c)
    silu_c_ctx = jax.nn.silu(c_ctx)

    def heads(t):
        return t.reshape(t.shape[0], t.shape[1], NA_HEADS, HEAD_DIM)

    for layer in range(DEPTH):
        li = layer // 2
        mod = silu_c @ ada_w[layer] + ada_b[layer]
        shift1, scale1, gate1, shift2, scale2, gate2 = jnp.split(mod[:, None, :], 6, axis=-1)
        h = modulate(rmsnorm(x, norm_mix[layer]), shift1, scale1)
        if layer % 2 == 0:
            mod_ctx = silu_c_ctx @ ada_w[layer, :, :2 * D] + ada_b[layer, :2 * D]
            h_ctx = modulate(rmsnorm(ctx, norm_mix[layer]), mod_ctx[:D], mod_ctx[D:])
            w_in = mix_w_in[li]
            q, k, v, xb, gb = jnp.split(h @ w_in, [NA_WIDTH, 2 * NA_WIDTH, 3 * NA_WIDTH, 3 * NA_WIDTH + LRU_WIDTH], axis=-1)
            k_c, v_c, xb_c = jnp.split(h_ctx @ w_in[:, NA_WIDTH:3 * NA_WIDTH + LRU_WIDTH], [NA_WIDTH, 2 * NA_WIDTH], axis=-1)
            q = rmsnorm(heads(q), na_q_norm[li]) * HEAD_DIM ** -0.5
            k = rmsnorm(heads(k), na_k_norm[li])
            k_c = rmsnorm(heads(k_c), na_k_norm[li])
            attn = neighbourhood_attention(q, k, heads(v), k_c, heads(v_c), na_rpb[li])
            xc_lat = centred_depthwise_conv(xb, lru_conv_w[li], lru_conv_b[li])
            xc_ctx = centred_depthwise_conv(xb_c, lru_conv_w[li], lru_conv_b[li])
            y = bidirectional_rglru(xc_lat, xc_ctx, lru_gate_r_w[li], lru_gate_r_b[li],
                                    lru_gate_i_w[li], lru_gate_i_b[li], lru_lambda[li])
            lru = jax.nn.gelu(gb) * y.astype(gb.dtype)
            mix = jnp.concatenate([attn, lru], axis=-1) @ mix_w_out[li]
        else:
            hg = h.astype(jnp.float32).reshape(B, N, FNET_GROUPS, D // FNET_GROUPS)
            f = jnp.fft.fft2(hg, axes=(1, 3), norm="ortho").real.reshape(B, N, D).astype(x.dtype)
            mix = f @ fnet_w_out[li]
        x = x + gate1 * mix
        h2 = modulate(rmsnorm(x, norm_ffn[layer]), shift2, scale2)
        x = x + gate2 * grouped_moe(h2, router_w, router_bias, moe_w_gate[layer], moe_w_up[layer], moe_w_down[layer])
    return x
```

```python
import functools

import numpy as np
import jax
import jax.numpy as jnp
from jax import lax
from jax.experimental import pallas as pl
from jax.experimental.pallas import tpu as pltpu

F32 = jnp.float32
BF16 = jnp.bfloat16
HIGHEST = lax.Precision.HIGHEST

D_MODEL = 1024
GRID_W = 64
HEAD_DIM = 64
NA_HEADS = 8
NA_WIDTH = NA_HEADS * HEAD_DIM
NA_WIN_ROWS = 8
NA_WIN_COLS = 16
LRU_WIDTH = 512
LRU_BLOCK = 64
LRU_C = 8.0
FNET_GROUPS = 4
N_EXPERTS = 16
EXPERTS_PER_GROUP = 4
N_EXPERT_GROUPS = 4
D_FF_EXPERT = 512
RMS_EPS = 1e-6
MASK_VALUE = -1e30

V7X_VMEM_LIMIT_BYTES = 56 * 1024 * 1024
LANES = 128
SUBLANES = 8

NA_QROWS = 4
NA_KROWS = NA_QROWS + NA_WIN_ROWS - 1
NA_QBLK = NA_QROWS * GRID_W
NA_KBLK = NA_KROWS * GRID_W

LRU_CHUNK = LANES
LRU_TROWS = 512


def _sigmoid(x):
    return 1.0 / (1.0 + jnp.exp(-x))


def _cparams(sem, vmem=V7X_VMEM_LIMIT_BYTES):
    return pltpu.CompilerParams(dimension_semantics=sem, vmem_limit_bytes=vmem)


def _mod_kernel(c_ref, w_ref, b_ref, o_ref):
    c = c_ref[...]
    s = c * _sigmoid(c)
    o_ref[0] = jnp.dot(s, w_ref[0], precision=HIGHEST, preferred_element_type=F32) + b_ref[0]


def _modulation(c_rows, ada_w, ada_b):
    depth, d, n6 = ada_w.shape
    r = c_rows.shape[0]
    tn = 1536
    return pl.pallas_call(
        _mod_kernel,
        out_shape=jax.ShapeDtypeStruct((depth, r, n6), F32),
        grid=(depth, n6 // tn),
        in_specs=[pl.BlockSpec((r, d), lambda l, j: (0, 0)),
                  pl.BlockSpec((1, d, tn), lambda l, j: (l, 0, j)),
                  pl.BlockSpec((1, 1, tn), lambda l, j: (l, 0, j))],
        out_specs=pl.BlockSpec((1, r, tn), lambda l, j: (l, 0, j)),
        compiler_params=_cparams(("arbitrary", "arbitrary")),
        name="adaln_mod",
    )(c_rows, ada_w, ada_b.reshape(depth, 1, n6))


def _norm_modulate(x, gain, shift, scale):
    ms = jnp.mean(x * x, axis=-1, keepdims=True)
    y = x * lax.rsqrt(ms + RMS_EPS) * gain
    return y * (1.0 + scale) + shift


def _inproj_kernel(x_ref, shift_ref, scale_ref, gain_ref, w_ref, ind_ref, qg_ref, kg_ref, *out_refs, segs):
    h = _norm_modulate(x_ref[...], gain_ref[...], shift_ref[0], scale_ref[0]).astype(BF16)
    for s, (kind, o_ref) in enumerate(zip(segs, out_refs)):
        z = jnp.dot(h, w_ref[:, s * NA_WIDTH:(s + 1) * NA_WIDTH], preferred_element_type=F32)
        if kind in ("q", "k"):
            ms = jnp.dot((z * z).astype(BF16), ind_ref[...], preferred_element_type=F32) * (1.0 / HEAD_DIM)
            g = qg_ref[...] if kind == "q" else kg_ref[...]
            z = z * lax.rsqrt(ms + RMS_EPS) * g
        o_ref[...] = z.astype(o_ref.dtype)


def _inproj(x2d, shift, scale, gain, w, ind, qg, kg, segs, tokens_per_batch, tm):
    t, d = x2d.shape
    tpb = tokens_per_batch // tm
    dt = {"q": BF16, "k": BF16, "v": BF16, "x": F32, "g": F32}
    full = lambda i: (0, 0)
    return pl.pallas_call(
        functools.partial(_inproj_kernel, segs=segs),
        out_shape=[jax.ShapeDtypeStruct((t, NA_WIDTH), dt[k]) for k in segs],
        grid=(t // tm,),
        in_specs=[pl.BlockSpec((tm, d), lambda i: (i, 0)),
                  pl.BlockSpec((1, 1, d), lambda i: (i // tpb, 0, 0)),
                  pl.BlockSpec((1, 1, d), lambda i: (i // tpb, 0, 0)),
                  pl.BlockSpec((1, d), full),
                  pl.BlockSpec(w.shape, full),
                  pl.BlockSpec(ind.shape, full),
                  pl.BlockSpec((1, NA_WIDTH), full),
                  pl.BlockSpec((1, NA_WIDTH), full)],
        out_specs=[pl.BlockSpec((tm, NA_WIDTH), lambda i: (i, 0)) for _ in segs],
        compiler_params=_cparams(("arbitrary",)),
        name="inproj_" + "".join(segs),
    )(x2d, shift, scale, gain, w, ind, qg, kg)


def _na_bias_tables(rpb, rows):
    kr = NA_WIN_ROWS
    rb_count = rows // NA_QROWS
    cq = np.arange(GRID_W)
    ck = np.arange(GRID_W)
    col_start = np.clip(cq - NA_WIN_COLS // 2, 0, GRID_W - NA_WIN_COLS)
    valid_c = (ck[None, :] >= col_start[:, None]) & (ck[None, :] < col_start[:, None] + NA_WIN_COLS)
    dc = np.clip(ck[None, :] - cq[:, None], 1 - NA_WIN_COLS, NA_WIN_COLS - 1) + (NA_WIN_COLS - 1)
    tables = []
    for rb in (0, 1, rb_count - 1):
        r = rb * NA_QROWS + np.arange(NA_QROWS)
        ks = int(np.clip(rb * NA_QROWS - kr // 2, 0, rows - NA_KROWS))
        key_r = ks + np.arange(NA_KROWS)
        row_start = np.clip(r - kr // 2, 0, rows - kr)
        valid_r = (key_r[None, :] >= row_start[:, None]) & (key_r[None, :] < row_start[:, None] + kr)
        dr = np.clip(key_r[None, :] - r[:, None] + (NA_WIN_ROWS - 1), 0, 2 * NA_WIN_ROWS - 2)
        b = jnp.take(rpb.astype(F32), jnp.asarray(dr.reshape(-1)), axis=1)
        b = jnp.take(b, jnp.asarray(dc.reshape(-1)), axis=2)
        b = b.reshape(NA_HEADS, NA_QROWS, NA_KROWS, GRID_W, GRID_W)
        valid = valid_r[:, :, None, None] & valid_c[None, None, :, :]
        b = jnp.where(jnp.asarray(valid)[None], b, MASK_VALUE)
        b = b.transpose(0, 1, 3, 2, 4).reshape(NA_HEADS, NA_QBLK, NA_KBLK)
        tables.append(b)
    return jnp.stack(tables)


def _attn_kernel(q_ref, k_ref, v_ref, kc_ref, vc_ref, bias_ref, o_ref, *, rows):
    rb = pl.program_id(1)
    ks = jnp.clip(rb * NA_QROWS - NA_WIN_ROWS // 2, 0, rows - NA_KROWS)
    kstart = pl.multiple_of(ks * GRID_W, GRID_W)
    nt = (((1,), (1,)), ((), ()))
    for h in range(NA_HEADS):
        hs = slice(h * HEAD_DIM, (h + 1) * HEAD_DIM)
        qh = q_ref[:, hs]
        kh = k_ref[pl.ds(kstart, NA_KBLK), hs]
        vh = v_ref[pl.ds(kstart, NA_KBLK), hs]
        s_w = lax.dot_general(qh, kh, nt, preferred_element_type=F32) + bias_ref[0, h]
        s_c = lax.dot_general(qh, kc_ref[:, hs], nt, preferred_element_type=F32)
        m = jnp.maximum(jnp.max(s_w, axis=-1, keepdims=True), jnp.max(s_c, axis=-1, keepdims=True))
        p_w = jnp.exp(s_w - m)
        p_c = jnp.exp(s_c - m)
        l = jnp.sum(p_w, axis=-1, keepdims=True) + jnp.sum(p_c, axis=-1, keepdims=True)
        o = (jnp.dot(p_w.astype(BF16), vh, preferred_element_type=F32)
             + jnp.dot(p_c.astype(BF16), vc_ref[:, hs], preferred_element_type=F32))
        o_ref[:, hs] = (o / l).astype(o_ref.dtype)


def _attention(q, k, v, kc, vc, bias, batch, n, ctx_len):
    rows = n // GRID_W
    rbc = rows // NA_QROWS

    def bias_idx(b, rb):
        return (jnp.where(rb == 0, 0, jnp.where(rb == rbc - 1, 2, 1)), 0, 0, 0)

    return pl.pallas_call(
        functools.partial(_attn_kernel, rows=rows),
        out_shape=jax.ShapeDtypeStruct((batch * n, NA_WIDTH), BF16),
        grid=(batch, rbc),
        in_specs=[pl.BlockSpec((NA_QBLK, NA_WIDTH), lambda b, rb: (b * rbc + rb, 0)),
                  pl.BlockSpec((n, NA_WIDTH), lambda b, rb: (b, 0)),
                  pl.BlockSpec((n, NA_WIDTH), lambda b, rb: (b, 0)),
                  pl.BlockSpec((ctx_len, NA_WIDTH), lambda b, rb: (b, 0)),
                  pl.BlockSpec((ctx_len, NA_WIDTH), lambda b, rb: (b, 0)),
                  pl.BlockSpec((1, NA_HEADS, NA_QBLK, NA_KBLK), bias_idx)],
        out_specs=pl.BlockSpec((NA_QBLK, NA_WIDTH), lambda b, rb: (b * rbc + rb, 0)),
        compiler_params=_cparams(("arbitrary", "arbitrary")),
        name="na_attention",
    )(q, k, v, kc, vc, bias)


def _scan_pitch(n):
    p = -(-n // SUBLANES)
    while p % 8 != 4:
        p += 1
    return p


def _lru_coeff_tile(xc, z, bias, sp, d):
    c = LRU_CHUNK
    r = _sigmoid(z[:, (2 * d) * c:(2 * d + 1) * c] + bias[:, (2 * d) * c:(2 * d + 1) * c])
    i = _sigmoid(z[:, (2 * d + 1) * c:(2 * d + 2) * c] + bias[:, (2 * d + 1) * c:(2 * d + 2) * c])
    log_a = -LRU_C * r * sp[d:d + 1, :]
    a = jnp.exp(log_a)
    one_minus_a2 = -jnp.tanh(log_a) * (a * a + 1.0)
    return a, jnp.sqrt(one_minus_a2) * (i * xc)


def _conv_tile(win, w, b, rows):
    acc = b + w[0:1, :] * win[6:6 + rows, :]
    acc = acc + w[1:2, :] * win[7:7 + rows, :]
    acc = acc + w[2:3, :] * win[8:8 + rows, :]
    return acc + w[3:4, :] * win[9:9 + rows, :]


def _chunk_totals(a_ref, b_ref, pitch, reverse):
    def body(j, carry):
        p, h = carry
        jj = pitch - 1 - j if reverse else j
        a = a_ref[pl.ds(jj, SUBLANES, stride=pitch), :]
        b = b_ref[pl.ds(jj, SUBLANES, stride=pitch), :]
        return a * p, a * h + b
    init = (jnp.ones((SUBLANES, LRU_CHUNK), F32), jnp.zeros((SUBLANES, LRU_CHUNK), F32))
    return lax.fori_loop(0, pitch, body, init)


def _chunk_starts(p_end, h_end, h0, reverse):
    row = lax.broadcasted_iota(jnp.int32, (SUBLANES, LRU_CHUNK), 0)
    starts = jnp.zeros((SUBLANES, LRU_CHUNK), F32)
    state = h0
    order = range(SUBLANES - 1, -1, -1) if reverse else range(SUBLANES)
    for s in order:
        starts = jnp.where(row == s, state, starts)
        state = p_end[s:s + 1, :] * state + h_end[s:s + 1, :]
    return starts, state


def _scan_write(a_ref, b_ref, h_ref, starts, pitch, reverse):
    def body(j, h):
        jj = pitch - 1 - j if reverse else j
        a = a_ref[pl.ds(jj, SUBLANES, stride=pitch), :]
        b = b_ref[pl.ds(jj, SUBLANES, stride=pitch), :]
        h = a * h + b
        h_ref[pl.ds(jj, SUBLANES, stride=pitch), :] = h
        return h
    lax.fori_loop(0, pitch, body, starts)


def _lru_kernel(x_ref, g_ref, xc_ref, cw_ref, cb_ref, w_ref, gb_ref, lam_ref, o_ref,
                xpad, a0, b0, a1, b1, h0s, h1s, ca0, cb0, ca1, cb1, *, n, ctx_len):
    pitch = _scan_pitch(n)
    cpitch = _scan_pitch(ctx_len)
    cw = cw_ref[...]
    cb = cb_ref[...]
    gbias = gb_ref[0]
    lam = lam_ref[...]
    sp = jnp.maximum(-lam, 0.0) + jnp.log1p(jnp.exp(-jnp.abs(lam)))
    wcat = w_ref[0]
    zeros8 = jnp.zeros((SUBLANES, LRU_CHUNK), F32)

    def fill_coeffs(src_rows, total, length, trows, a_refs, b_refs):
        for d in range(2):
            a_refs[d][pl.ds(length, total - length), :] = jnp.ones((total - length, LRU_CHUNK), F32)
            b_refs[d][pl.ds(length, total - length), :] = jnp.zeros((total - length, LRU_CHUNK), F32)
        xpad[pl.ds(0, SUBLANES), :] = zeros8
        xpad[pl.ds(SUBLANES + length, SUBLANES), :] = zeros8
        xpad[pl.ds(SUBLANES, length), :] = src_rows

        def tile(t, carry):
            t0 = pl.multiple_of(t * trows, SUBLANES)
            win = xpad[pl.ds(t0, trows + 2 * SUBLANES), :]
            xc = _conv_tile(win, cw, cb, trows)
            z = jnp.dot(xc.astype(BF16), wcat, preferred_element_type=F32)
            for d in range(2):
                a, b = _lru_coeff_tile(xc, z, gbias, sp, d)
                a_refs[d][pl.ds(t0, trows), :] = a
                b_refs[d][pl.ds(t0, trows), :] = b
            return carry
        lax.fori_loop(0, length // trows, tile, 0)

    fill_coeffs(xc_ref[...], SUBLANES * cpitch, ctx_len, ctx_len, (ca0, ca1), (cb0, cb1))
    h_init = []
    for d, (ar, br) in enumerate(((ca0, cb0), (ca1, cb1))):
        p_end, h_end = _chunk_totals(ar, br, cpitch, reverse=(d == 1))
        _, final = _chunk_starts(p_end, h_end, jnp.zeros((1, LRU_CHUNK), F32), reverse=(d == 1))
        h_init.append(final)

    fill_coeffs(x_ref[...], SUBLANES * pitch, n, LRU_TROWS, (a0, a1), (b0, b1))
    for d, (ar, br, hr) in enumerate(((a0, b0, h0s), (a1, b1, h1s))):
        p_end, h_end = _chunk_totals(ar, br, pitch, reverse=(d == 1))
        starts, _ = _chunk_starts(p_end, h_end, h_init[d], reverse=(d == 1))
        _scan_write(ar, br, hr, starts, pitch, reverse=(d == 1))

    def out_tile(t, carry):
        t0 = pl.multiple_of(t * LRU_TROWS, SUBLANES)
        y = h0s[pl.ds(t0, LRU_TROWS), :] + h1s[pl.ds(t0, LRU_TROWS), :]
        g = g_ref[pl.ds(t0, LRU_TROWS), :]
        gelu = 0.5 * g * (1.0 + jnp.tanh(0.7978845608028654 * (g + 0.044715 * (g * g * g))))
        o_ref[pl.ds(t0, LRU_TROWS), :] = (gelu * y).astype(o_ref.dtype)
        return carry
    lax.fori_loop(0, n // LRU_TROWS, out_tile, 0)


def _lru(xb, gb, xb_ctx, conv_w, conv_b, wcat, gbias, lam, batch, n, ctx_len):
    nch = LRU_WIDTH // LRU_CHUNK
    pitch = _scan_pitch(n)
    cpitch = _scan_pitch(ctx_len)
    big = pltpu.VMEM((SUBLANES * pitch, LRU_CHUNK), F32)
    small = pltpu.VMEM((SUBLANES * cpitch, LRU_CHUNK), F32)
    return pl.pallas_call(
        functools.partial(_lru_kernel, n=n, ctx_len=ctx_len),
        out_shape=jax.ShapeDtypeStruct((batch * n, LRU_WIDTH), BF16),
        grid=(batch, nch),
        in_specs=[pl.BlockSpec((n, LRU_CHUNK), lambda b, c: (b, c)),
                  pl.BlockSpec((n, LRU_CHUNK), lambda b, c: (b, c)),
                  pl.BlockSpec((ctx_len, LRU_CHUNK), lambda b, c: (b, c)),
                  pl.BlockSpec((4, LRU_CHUNK), lambda b, c: (0, c)),
                  pl.BlockSpec((1, LRU_CHUNK), lambda b, c: (0, c)),
                  pl.BlockSpec((1, LRU_CHUNK, 4 * LRU_CHUNK), lambda b, c: (c, 0, 0)),
                  pl.BlockSpec((1, 1, 4 * LRU_CHUNK), lambda b, c: (c, 0, 0)),
                  pl.BlockSpec((2, LRU_CHUNK), lambda b, c: (0, c))],
        out_specs=pl.BlockSpec((n, LRU_CHUNK), lambda b, c: (b, c)),
        scratch_shapes=[pltpu.VMEM((n + 2 * SUBLANES, LRU_CHUNK), F32),
                        big, big, big, big, big, big, small, small, small, small],
        compiler_params=_cparams(("arbitrary", "arbitrary")),
        name="rglru",
    )(xb, gb, xb_ctx, conv_w, conv_b, wcat, gbias, lam)


def _lru_gate_weights(w_r, b_r, w_i, b_i):
    nch = LRU_WIDTH // LRU_CHUNK
    bpc = LRU_CHUNK // LRU_BLOCK

    def dense(w):
        wc = w.reshape(nch, bpc, LRU_BLOCK, LRU_BLOCK)
        eye = jnp.eye(bpc, dtype=w.dtype)
        return jnp.einsum("cbij,bd->cbidj", wc, eye).reshape(nch, LRU_CHUNK, LRU_CHUNK)

    wcat = jnp.concatenate([dense(w_r[0]), dense(w_i[0]), dense(w_r[1]), dense(w_i[1])], axis=-1).astype(BF16)
    chunk = lambda v: v.reshape(nch, 1, LRU_CHUNK)
    gbias = jnp.concatenate([chunk(b_r[0]), chunk(b_i[0]), chunk(b_r[1]), chunk(b_i[1])], axis=-1).astype(F32)
    return wcat, gbias


def _route(s, sel, gates_ref):
    srow = [s[e:e + 1, :] for e in range(N_EXPERTS)]
    lrow = [sel[e:e + 1, :] for e in range(N_EXPERTS)]
    gscore = []
    for g in range(N_EXPERT_GROUPS):
        a = lrow[g * EXPERTS_PER_GROUP:(g + 1) * EXPERTS_PER_GROUP]
        best = a[0] + a[1]
        for i, j in ((0, 2), (0, 3), (1, 2), (1, 3), (2, 3)):
            best = jnp.maximum(best, a[i] + a[j])
        gscore.append(best)
    bg = jnp.zeros_like(gscore[0], dtype=jnp.int32)
    bv = gscore[0]
    for g in range(1, N_EXPERT_GROUPS):
        upd = gscore[g] > bv
        bg = jnp.where(upd, g, bg)
        bv = jnp.where(upd, gscore[g], bv)

    def pick(rows_):
        out = []
        for j in range(EXPERTS_PER_GROUP):
            v = rows_[j]
            for g in range(1, N_EXPERT_GROUPS):
                v = jnp.where(bg == g, rows_[g * EXPERTS_PER_GROUP + j], v)
            out.append(v)
        return out
    cand = pick(lrow)
    cs = pick(srow)
    i1 = jnp.zeros_like(bg)
    v1 = cand[0]
    w1 = cs[0]
    for j in range(1, EXPERTS_PER_GROUP):
        upd = cand[j] > v1
        i1 = jnp.where(upd, j, i1)
        v1 = jnp.where(upd, cand[j], v1)
        w1 = jnp.where(upd, cs[j], w1)
    i2 = jnp.full_like(bg, -1)
    v2 = jnp.full_like(v1, -jnp.inf)
    w2 = jnp.zeros_like(w1)
    for j in range(EXPERTS_PER_GROUP):
        upd = (i1 != j) & (cand[j] > v2)
        i2 = jnp.where(upd, j, i2)
        v2 = jnp.where(upd, cand[j], v2)
        w2 = jnp.where(upd, cs[j], w2)
    den = w1 + w2
    g1 = w1 / den
    g2 = w2 / den
    for e in range(N_EXPERTS):
        ge, je = divmod(e, EXPERTS_PER_GROUP)
        val = jnp.where(i1 == je, g1, 0.0) + jnp.where(i2 == je, g2, 0.0)
        gates_ref[e:e + 1, :] = jnp.where(bg == ge, val, 0.0)


def _post_kernel(a_ref, b_ref, w_ref, x_ref, gate_ref, shift_ref, scale_ref, gain_ref, rw_ref, rb_ref,
                 x1_ref, h2_ref, gates_ref, gscr):
    half = a_ref.shape[1]
    mix = (jnp.dot(a_ref[...], w_ref[:half, :], preferred_element_type=F32)
           + jnp.dot(b_ref[...], w_ref[half:, :], preferred_element_type=F32))
    x1 = x_ref[...] + gate_ref[0] * mix
    x1_ref[...] = x1
    h2 = _norm_modulate(x1, gain_ref[...], shift_ref[0], scale_ref[0])
    h2_ref[...] = h2.astype(h2_ref.dtype)
    logits = lax.dot_general(rw_ref[...], h2, (((1,), (1,)), ((), ())), precision=HIGHEST,
                             preferred_element_type=F32)
    s = _sigmoid(logits)
    gscr[...] = jnp.zeros(gscr.shape, F32)
    _route(s, s + rb_ref[...], gscr)
    gates_ref[...] = gscr[...].T


def _post_mixer(a, a_col, b, b_col, w, x2d, gate1, shift2, scale2, gain, rw_t, rbias, tokens_per_batch, tm):
    t, d = x2d.shape
    tpb = tokens_per_batch // tm
    half = d // 2
    full = lambda i: (0, 0)
    per_b = lambda i: (i // tpb, 0, 0)
    return pl.pallas_call(
        _post_kernel,
        out_shape=[jax.ShapeDtypeStruct((t, d), F32), jax.ShapeDtypeStruct((t, d), BF16),
                   jax.ShapeDtypeStruct((t, LANES), F32)],
        grid=(t // tm,),
        in_specs=[pl.BlockSpec((tm, half), lambda i: (i, a_col)),
                  pl.BlockSpec((tm, half), lambda i: (i, b_col)),
                  pl.BlockSpec(w.shape, full),
                  pl.BlockSpec((tm, d), lambda i: (i, 0)),
                  pl.BlockSpec((1, 1, d), per_b),
                  pl.BlockSpec((1, 1, d), per_b),
                  pl.BlockSpec((1, 1, d), per_b),
                  pl.BlockSpec((1, d), full),
                  pl.BlockSpec(rw_t.shape, full),
                  pl.BlockSpec(rbias.shape, full)],
        out_specs=[pl.BlockSpec((tm, d), lambda i: (i, 0)),
                   pl.BlockSpec((tm, d), lambda i: (i, 0)),
                   pl.BlockSpec((tm, LANES), lambda i: (i, 0))],
        scratch_shapes=[pltpu.VMEM((LANES, tm), F32)],
        compiler_params=_cparams(("arbitrary",)),
        name="post_mixer",
    )(a, b, w, x2d, gate1, shift2, scale2, gain, rw_t, rbias)


def _moe_kernel(h_ref, gates_ref, wg_ref, wu_ref, wd_ref, x1_ref, gate2_ref, o_ref, acc):
    e = pl.program_id(1)

    @pl.when(e == 0)
    def _():
        acc[...] = jnp.zeros(acc.shape, F32)

    h = h_ref[...]
    g = jnp.dot(h, wg_ref[0], preferred_element_type=F32)
    u = jnp.dot(h, wu_ref[0], preferred_element_type=F32)
    lane = lax.broadcasted_iota(jnp.int32, gates_ref.shape, 1)
    gcol = jnp.sum(jnp.where(lane == e, gates_ref[...], 0.0), axis=1, keepdims=True)
    act = (g * _sigmoid(g)) * u * gcol
    acc[...] += jnp.dot(act.astype(BF16), wd_ref[0], preferred_element_type=F32)

    @pl.when(e == N_EXPERTS - 1)
    def _():
        o_ref[...] = x1_ref[...] + gate2_ref[0] * acc[...]


def _moe(h2, gates, wg, wu, wd, x1, gate2, tokens_per_batch, tm):
    t, d = x1.shape
    tpb = tokens_per_batch // tm
    return pl.pallas_call(
        _moe_kernel,
        out_shape=jax.ShapeDtypeStruct((t, d), F32),
        grid=(t // tm, N_EXPERTS),
        in_specs=[pl.BlockSpec((tm, d), lambda i, e: (i, 0)),
                  pl.BlockSpec((tm, LANES), lambda i, e: (i, 0)),
                  pl.BlockSpec((1, d, D_FF_EXPERT), lambda i, e: (e, 0, 0)),
                  pl.BlockSpec((1, d, D_FF_EXPERT), lambda i, e: (e, 0, 0)),
                  pl.BlockSpec((1, D_FF_EXPERT, d), lambda i, e: (e, 0, 0)),
                  pl.BlockSpec((tm, d), lambda i, e: (i, 0)),
                  pl.BlockSpec((1, 1, d), lambda i, e: (i // tpb, 0, 0))],
        out_specs=pl.BlockSpec((tm, d), lambda i, e: (i, 0)),
        scratch_shapes=[pltpu.VMEM((tm, d), F32)],
        compiler_params=_cparams(("arbitrary", "arbitrary")),
        name="moe_dense",
    )(h2, gates, wg, wu, wd, x1, gate2)


def _fnet_chan_kernel(x_ref, shift_ref, scale_ref, gain_ref, cs_ref, y1_ref, y2_ref):
    h = _norm_modulate(x_ref[...], gain_ref[...], shift_ref[0], scale_ref[0]).astype(BF16)
    gw = D_MODEL // FNET_GROUPS
    for g in range(FNET_GROUPS):
        y = jnp.dot(h[:, g * gw:(g + 1) * gw], cs_ref[...], preferred_element_type=F32)
        y1_ref[:, g * gw:(g + 1) * gw] = y[:, :gw].astype(y1_ref.dtype)
        y2_ref[:, g * gw:(g + 1) * gw] = y[:, gw:].astype(y2_ref.dtype)


def _fnet_channel(x2d, shift, scale, gain, cs, tokens_per_batch, tm):
    t, d = x2d.shape
    tpb = tokens_per_batch // tm
    full = lambda i: (0, 0)
    return pl.pallas_call(
        _fnet_chan_kernel,
        out_shape=[jax.ShapeDtypeStruct((t, d), BF16), jax.ShapeDtypeStruct((t, d), BF16)],
        grid=(t // tm,),
        in_specs=[pl.BlockSpec((tm, d), lambda i: (i, 0)),
                  pl.BlockSpec((1, 1, d), lambda i: (i // tpb, 0, 0)),
                  pl.BlockSpec((1, 1, d), lambda i: (i // tpb, 0, 0)),
                  pl.BlockSpec((1, d), full),
                  pl.BlockSpec(cs.shape, full)],
        out_specs=[pl.BlockSpec((tm, d), lambda i: (i, 0)), pl.BlockSpec((tm, d), lambda i: (i, 0))],
        compiler_params=_cparams(("arbitrary",)),
        name="fnet_channel",
    )(x2d, shift, scale, gain, cs)


def _fnet_pos_kernel(c_ref, s_ref, y1_ref, y2_ref, o_ref):
    o = (jnp.dot(c_ref[...], y1_ref[...], preferred_element_type=F32)
         - jnp.dot(s_ref[...], y2_ref[...], preferred_element_type=F32))
    o_ref[...] = o.astype(o_ref.dtype)


def _fnet_position(cn, sn, y1, y2, batch, n, tm):
    t, d = y1.shape
    halves = 2
    dh = d // halves
    mt = n // tm
    return pl.pallas_call(
        _fnet_pos_kernel,
        out_shape=jax.ShapeDtypeStruct((t, d), BF16),
        grid=(batch, halves, mt),
        in_specs=[pl.BlockSpec((tm, n), lambda b, c, m: (m, 0)),
                  pl.BlockSpec((tm, n), lambda b, c, m: (m, 0)),
                  pl.BlockSpec((n, dh), lambda b, c, m: (b, c)),
                  pl.BlockSpec((n, dh), lambda b, c, m: (b, c))],
        out_specs=pl.BlockSpec((tm, dh), lambda b, c, m: (b * mt + m, c)),
        compiler_params=_cparams(("arbitrary", "arbitrary", "arbitrary")),
        name="fnet_position",
    )(cn, sn, y1, y2)


def _dft_tables(n):
    gw = D_MODEL // FNET_GROUPS
    j = np.arange(gw)
    ang = 2.0 * np.pi * ((j[:, None] * j[None, :]) % gw) / gw
    cs = np.concatenate([np.cos(ang), np.sin(ang)], axis=1) / np.sqrt(gw)
    ra = n // GRID_W
    k = np.arange(n)
    a = np.arange(ra)
    b = np.arange(GRID_W)
    ang_a = 2.0 * np.pi * ((a[:, None] * GRID_W * k[None, :]) % n) / n
    ang_b = 2.0 * np.pi * ((b[:, None] * k[None, :]) % n) / n
    scale = 1.0 / np.sqrt(n)
    ca, sa = jnp.asarray(np.cos(ang_a), F32), jnp.asarray(np.sin(ang_a), F32)
    cb, sb = jnp.asarray(np.cos(ang_b) * scale, F32), jnp.asarray(np.sin(ang_b) * scale, F32)
    cn = (ca[:, None, :] * cb[None, :, :] - sa[:, None, :] * sb[None, :, :]).reshape(n, n).astype(BF16)
    sn = (sa[:, None, :] * cb[None, :, :] + ca[:, None, :] * sb[None, :, :]).reshape(n, n).astype(BF16)
    return jnp.asarray(cs, BF16), cn, sn


def kernel(x, c, ctx, c_ctx, ada_w, ada_b, norm_mix, norm_ffn, mix_w_in, mix_w_out, na_q_norm, na_k_norm, na_rpb,
           lru_conv_w, lru_conv_b, lru_gate_r_w, lru_gate_r_b, lru_gate_i_w, lru_gate_i_b, lru_lambda,
           fnet_w_out, router_w, router_bias, moe_w_gate, moe_w_up, moe_w_down):
    batch, n, d = x.shape
    ctx_len = ctx.shape[1]
    depth = ada_w.shape[0]
    rows = n // GRID_W
    assert d == D_MODEL and n % (GRID_W * NA_QROWS) == 0 and rows >= 4 * NA_QROWS
    t = batch * n
    tm = 512
    tm_moe = 1024

    r_pad = -(-(batch + 1) // SUBLANES) * SUBLANES
    c_rows = jnp.concatenate([c, c_ctx[None, :], jnp.zeros((r_pad - batch - 1, d), c.dtype)], axis=0)
    mod = _modulation(c_rows, ada_w, ada_b)

    def mod_slices(layer):
        m = mod[layer, :batch].reshape(batch, 1, 6, d)
        return [m[:, :, i, :] for i in range(6)]

    rw_t = router_w.T.astype(F32)
    rbias = router_bias.reshape(N_EXPERTS, 1).astype(F32)
    x2d = x.reshape(t, d)
    ctx2d = ctx.reshape(batch * ctx_len, d)

    for layer in range(depth):
        li = layer // 2
        shift1, scale1, gate1, shift2, scale2, gate2 = mod_slices(layer)
        gain_mix = norm_mix[layer].reshape(1, d)
        gain_ffn = norm_ffn[layer].reshape(1, d)
        if layer % 2 == 0:
            w_in = mix_w_in[li].astype(BF16)
            ind = jnp.asarray(np.kron(np.eye(NA_HEADS), np.ones((HEAD_DIM, HEAD_DIM))), BF16)
            qg = (jnp.tile(na_q_norm[li], NA_HEADS) * HEAD_DIM ** -0.5).reshape(1, NA_WIDTH).astype(F32)
            kg = jnp.tile(na_k_norm[li], NA_HEADS).reshape(1, NA_WIDTH).astype(F32)
            q, k, v, xb, gb = _inproj(x2d, shift1, scale1, gain_mix, w_in, ind, qg, kg,
                                      ("q", "k", "v", "x", "g"), n, tm)
            mctx = mod[layer, batch, :2 * d]
            shift_c = jnp.broadcast_to(mctx[:d], (batch, 1, d))
            scale_c = jnp.broadcast_to(mctx[d:], (batch, 1, d))
            k_c, v_c, xb_c = _inproj(ctx2d, shift_c, scale_c, gain_mix, w_in[:, NA_WIDTH:4 * NA_WIDTH], ind, qg, kg,
                                     ("k", "v", "x"), ctx_len, ctx_len)
            bias = _na_bias_tables(na_rpb[li], rows)
            attn = _attention(q, k, v, k_c, v_c, bias, batch, n, ctx_len)
            wcat, gbias = _lru_gate_weights(lru_gate_r_w[li], lru_gate_r_b[li], lru_gate_i_w[li], lru_gate_i_b[li])
            lru = _lru(xb, gb, xb_c, lru_conv_w[li].astype(F32), lru_conv_b[li].reshape(1, LRU_WIDTH).astype(F32),
                       wcat, gbias, lru_lambda[li].astype(F32), batch, n, ctx_len)
            mix_a, col_a, mix_b, col_b, w_out = attn, 0, lru, 0, mix_w_out[li].astype(BF16)
        else:
            cs, cn, sn = _dft_tables(n)
            y1, y2 = _fnet_channel(x2d, shift1, scale1, gain_mix, cs, n, tm)
            f = _fnet_position(cn, sn, y1, y2, batch, n, tm)
            mix_a, col_a, mix_b, col_b, w_out = f, 0, f, 1, fnet_w_out[li].astype(BF16)
        x1, h2, gates = _post_mixer(mix_a, col_a, mix_b, col_b, w_out, x2d, gate1, shift2, scale2, gain_ffn, rw_t, rbias, n, tm)
        x2d = _moe(h2, gates, moe_w_gate[layer].astype(BF16), moe_w_up[layer].astype(BF16),
                   moe_w_down[layer].astype(BF16), x1, gate2, n, tm_moe)
    return x2d.reshape(batch, n, d)
```

```python
import functools

import numpy as np
import jax
import jax.numpy as jnp
from jax import lax
from jax.experimental import pallas as pl
from jax.experimental.pallas import tpu as pltpu

F32 = jnp.float32
BF16 = jnp.bfloat16
HIGHEST = lax.Precision.HIGHEST

D_MODEL = 1024
GRID_W = 64
HEAD_DIM = 64
NA_HEADS = 8
NA_WIDTH = NA_HEADS * HEAD_DIM
NA_WIN_ROWS = 8
NA_WIN_COLS = 16
LRU_WIDTH = 512
LRU_BLOCK = 64
LRU_C = 8.0
FNET_GROUPS = 4
N_EXPERTS = 16
EXPERTS_PER_GROUP = 4
N_EXPERT_GROUPS = 4
D_FF_EXPERT = 512
RMS_EPS = 1e-6
MASK_VALUE = -1e30

V7X_VMEM_LIMIT_BYTES = 56 * 1024 * 1024
LANES = 128
SUBLANES = 8

NA_QROWS = 4
NA_KROWS = NA_QROWS + NA_WIN_ROWS - 1
NA_QBLK = NA_QROWS * GRID_W
NA_KBLK = NA_KROWS * GRID_W

LRU_CHUNK = LANES
LRU_TROWS = 512

ROUTE_GID_ROW = EXPERTS_PER_GROUP
MOE_TILE = 512
ROW_ALIGN = 16
MOE_CROWS = MOE_TILE + N_EXPERT_GROUPS * ROW_ALIGN
MOE_SEG_BITS = (MOE_TILE // ROW_ALIGN).bit_length()
MOE_TAIL_BITS = (MOE_TILE // ROW_ALIGN - 1).bit_length()


def _sigmoid(x):
    return 1.0 / (1.0 + jnp.exp(-x))


def _cparams(sem, vmem=V7X_VMEM_LIMIT_BYTES):
    return pltpu.CompilerParams(dimension_semantics=sem, vmem_limit_bytes=vmem)


def _mod_kernel(c_ref, w_ref, b_ref, o_ref):
    c = c_ref[...]
    s = c * _sigmoid(c)
    o_ref[0] = jnp.dot(s, w_ref[0], precision=HIGHEST, preferred_element_type=F32) + b_ref[0]


def _modulation(c_rows, ada_w, ada_b):
    depth, d, n6 = ada_w.shape
    r = c_rows.shape[0]
    tn = 1536
    return pl.pallas_call(
        _mod_kernel,
        out_shape=jax.ShapeDtypeStruct((depth, r, n6), F32),
        grid=(depth, n6 // tn),
        in_specs=[pl.BlockSpec((r, d), lambda l, j: (0, 0)),
                  pl.BlockSpec((1, d, tn), lambda l, j: (l, 0, j)),
                  pl.BlockSpec((1, 1, tn), lambda l, j: (l, 0, j))],
        out_specs=pl.BlockSpec((1, r, tn), lambda l, j: (l, 0, j)),
        compiler_params=_cparams(("arbitrary", "arbitrary")),
        name="adaln_mod",
    )(c_rows, ada_w, ada_b.reshape(depth, 1, n6))


def _norm_modulate(x, gain, shift, scale):
    ms = jnp.mean(x * x, axis=-1, keepdims=True)
    y = x * lax.rsqrt(ms + RMS_EPS) * gain
    return y * (1.0 + scale) + shift


def _inproj_kernel(x_ref, shift_ref, scale_ref, gain_ref, w_ref, ind_ref, qg_ref, kg_ref, *out_refs, segs):
    h = _norm_modulate(x_ref[...], gain_ref[...], shift_ref[0], scale_ref[0]).astype(BF16)
    for s, (kind, o_ref) in enumerate(zip(segs, out_refs)):
        z = jnp.dot(h, w_ref[:, s * NA_WIDTH:(s + 1) * NA_WIDTH], preferred_element_type=F32)
        if kind in ("q", "k"):
            ms = jnp.dot((z * z).astype(BF16), ind_ref[...], preferred_element_type=F32) * (1.0 / HEAD_DIM)
            g = qg_ref[...] if kind == "q" else kg_ref[...]
            z = z * lax.rsqrt(ms + RMS_EPS) * g
        o_ref[...] = z.astype(o_ref.dtype)


def _inproj(x2d, shift, scale, gain, w, ind, qg, kg, segs, tokens_per_batch, tm):
    t, d = x2d.shape
    tpb = tokens_per_batch // tm
    dt = {"q": BF16, "k": BF16, "v": BF16, "x": F32, "g": F32}
    full = lambda i: (0, 0)
    return pl.pallas_call(
        functools.partial(_inproj_kernel, segs=segs),
        out_shape=[jax.ShapeDtypeStruct((t, NA_WIDTH), dt[k]) for k in segs],
        grid=(t // tm,),
        in_specs=[pl.BlockSpec((tm, d), lambda i: (i, 0)),
                  pl.BlockSpec((1, 1, d), lambda i: (i // tpb, 0, 0)),
                  pl.BlockSpec((1, 1, d), lambda i: (i // tpb, 0, 0)),
                  pl.BlockSpec((1, d), full),
                  pl.BlockSpec(w.shape, full),
                  pl.BlockSpec(ind.shape, full),
                  pl.BlockSpec((1, NA_WIDTH), full),
                  pl.BlockSpec((1, NA_WIDTH), full)],
        out_specs=[pl.BlockSpec((tm, NA_WIDTH), lambda i: (i, 0)) for _ in segs],
        compiler_params=_cparams(("arbitrary",)),
        name="inproj_" + "".join(segs),
    )(x2d, shift, scale, gain, w, ind, qg, kg)


def _na_bias_tables(rpb, rows):
    kr = NA_WIN_ROWS
    rb_count = rows // NA_QROWS
    cq = np.arange(GRID_W)
    ck = np.arange(GRID_W)
    col_start = np.clip(cq - NA_WIN_COLS // 2, 0, GRID_W - NA_WIN_COLS)
    valid_c = (ck[None, :] >= col_start[:, None]) & (ck[None, :] < col_start[:, None] + NA_WIN_COLS)
    dc = np.clip(ck[None, :] - cq[:, None], 1 - NA_WIN_COLS, NA_WIN_COLS - 1) + (NA_WIN_COLS - 1)
    tables = []
    for rb in (0, 1, rb_count - 1):
        r = rb * NA_QROWS + np.arange(NA_QROWS)
        ks = int(np.clip(rb * NA_QROWS - kr // 2, 0, rows - NA_KROWS))
        key_r = ks + np.arange(NA_KROWS)
        row_start = np.clip(r - kr // 2, 0, rows - kr)
        valid_r = (key_r[None, :] >= row_start[:, None]) & (key_r[None, :] < row_start[:, None] + kr)
        dr = np.clip(key_r[None, :] - r[:, None] + (NA_WIN_ROWS - 1), 0, 2 * NA_WIN_ROWS - 2)
        b = jnp.take(rpb.astype(F32), jnp.asarray(dr.reshape(-1)), axis=1)
        b = jnp.take(b, jnp.asarray(dc.reshape(-1)), axis=2)
        b = b.reshape(NA_HEADS, NA_QROWS, NA_KROWS, GRID_W, GRID_W)
        valid = valid_r[:, :, None, None] & valid_c[None, None, :, :]
        b = jnp.where(jnp.asarray(valid)[None], b, MASK_VALUE)
        b = b.transpose(0, 1, 3, 2, 4).reshape(NA_HEADS, NA_QBLK, NA_KBLK)
        tables.append(b)
    return jnp.stack(tables)


def _attn_kernel(q_ref, k_ref, v_ref, kc_ref, vc_ref, bias_ref, o_ref, *, rows):
    rb = pl.program_id(1)
    ks = jnp.clip(rb * NA_QROWS - NA_WIN_ROWS // 2, 0, rows - NA_KROWS)
    kstart = pl.multiple_of(ks * GRID_W, GRID_W)
    nt = (((1,), (1,)), ((), ()))
    for h in range(NA_HEADS):
        hs = slice(h * HEAD_DIM, (h + 1) * HEAD_DIM)
        qh = q_ref[:, hs]
        kh = k_ref[pl.ds(kstart, NA_KBLK), hs]
        vh = v_ref[pl.ds(kstart, NA_KBLK), hs]
        s_w = lax.dot_general(qh, kh, nt, preferred_element_type=F32) + bias_ref[0, h]
        s_c = lax.dot_general(qh, kc_ref[:, hs], nt, preferred_element_type=F32)
        m = jnp.maximum(jnp.max(s_w, axis=-1, keepdims=True), jnp.max(s_c, axis=-1, keepdims=True))
        p_w = jnp.exp(s_w - m)
        p_c = jnp.exp(s_c - m)
        l = jnp.sum(p_w, axis=-1, keepdims=True) + jnp.sum(p_c, axis=-1, keepdims=True)
        o = (jnp.dot(p_w.astype(BF16), vh, preferred_element_type=F32)
             + jnp.dot(p_c.astype(BF16), vc_ref[:, hs], preferred_element_type=F32))
        o_ref[:, hs] = (o / l).astype(o_ref.dtype)


def _attention(q, k, v, kc, vc, bias, batch, n, ctx_len):
    rows = n // GRID_W
    rbc = rows // NA_QROWS

    def bias_idx(b, rb):
        return (jnp.where(rb == 0, 0, jnp.where(rb == rbc - 1, 2, 1)), 0, 0, 0)

    return pl.pallas_call(
        functools.partial(_attn_kernel, rows=rows),
        out_shape=jax.ShapeDtypeStruct((batch * n, NA_WIDTH), BF16),
        grid=(batch, rbc),
        in_specs=[pl.BlockSpec((NA_QBLK, NA_WIDTH), lambda b, rb: (b * rbc + rb, 0)),
                  pl.BlockSpec((n, NA_WIDTH), lambda b, rb: (b, 0)),
                  pl.BlockSpec((n, NA_WIDTH), lambda b, rb: (b, 0)),
                  pl.BlockSpec((ctx_len, NA_WIDTH), lambda b, rb: (b, 0)),
                  pl.BlockSpec((ctx_len, NA_WIDTH), lambda b, rb: (b, 0)),
                  pl.BlockSpec((1, NA_HEADS, NA_QBLK, NA_KBLK), bias_idx)],
        out_specs=pl.BlockSpec((NA_QBLK, NA_WIDTH), lambda b, rb: (b * rbc + rb, 0)),
        compiler_params=_cparams(("arbitrary", "arbitrary")),
        name="na_attention",
    )(q, k, v, kc, vc, bias)


def _scan_pitch(n):
    p = -(-n // SUBLANES)
    while p % 8 != 4:
        p += 1
    return p


def _lru_coeff_tile(xc, z, bias, sp, d):
    c = LRU_CHUNK
    r = _sigmoid(z[:, (2 * d) * c:(2 * d + 1) * c] + bias[:, (2 * d) * c:(2 * d + 1) * c])
    i = _sigmoid(z[:, (2 * d + 1) * c:(2 * d + 2) * c] + bias[:, (2 * d + 1) * c:(2 * d + 2) * c])
    log_a = -LRU_C * r * sp[d:d + 1, :]
    a = jnp.exp(log_a)
    one_minus_a2 = -jnp.tanh(log_a) * (a * a + 1.0)
    return a, jnp.sqrt(one_minus_a2) * (i * xc)


def _conv_tile(win, w, b, rows):
    acc = b + w[0:1, :] * win[6:6 + rows, :]
    acc = acc + w[1:2, :] * win[7:7 + rows, :]
    acc = acc + w[2:3, :] * win[8:8 + rows, :]
    return acc + w[3:4, :] * win[9:9 + rows, :]


def _chunk_totals(a_ref, b_ref, pitch, reverse):
    def body(j, carry):
        p, h = carry
        jj = pitch - 1 - j if reverse else j
        a = a_ref[pl.ds(jj, SUBLANES, stride=pitch), :]
        b = b_ref[pl.ds(jj, SUBLANES, stride=pitch), :]
        return a * p, a * h + b
    init = (jnp.ones((SUBLANES, LRU_CHUNK), F32), jnp.zeros((SUBLANES, LRU_CHUNK), F32))
    return lax.fori_loop(0, pitch, body, init)


def _chunk_starts(p_end, h_end, h0, reverse):
    row = lax.broadcasted_iota(jnp.int32, (SUBLANES, LRU_CHUNK), 0)
    starts = jnp.zeros((SUBLANES, LRU_CHUNK), F32)
    state = h0
    order = range(SUBLANES - 1, -1, -1) if reverse else range(SUBLANES)
    for s in order:
        starts = jnp.where(row == s, state, starts)
        state = p_end[s:s + 1, :] * state + h_end[s:s + 1, :]
    return starts, state


def _scan_write(a_ref, b_ref, h_ref, starts, pitch, reverse):
    def body(j, h):
        jj = pitch - 1 - j if reverse else j
        a = a_ref[pl.ds(jj, SUBLANES, stride=pitch), :]
        b = b_ref[pl.ds(jj, SUBLANES, stride=pitch), :]
        h = a * h + b
        h_ref[pl.ds(jj, SUBLANES, stride=pitch), :] = h
        return h
    lax.fori_loop(0, pitch, body, starts)


def _lru_kernel(x_ref, g_ref, xc_ref, cw_ref, cb_ref, w_ref, gb_ref, lam_ref, o_ref,
                xpad, a0, b0, a1, b1, h0s, h1s, ca0, cb0, ca1, cb1, *, n, ctx_len):
    pitch = _scan_pitch(n)
    cpitch = _scan_pitch(ctx_len)
    cw = cw_ref[...]
    cb = cb_ref[...]
    gbias = gb_ref[0]
    lam = lam_ref[...]
    sp = jnp.maximum(-lam, 0.0) + jnp.log1p(jnp.exp(-jnp.abs(lam)))
    wcat = w_ref[0]
    zeros8 = jnp.zeros((SUBLANES, LRU_CHUNK), F32)

    def fill_coeffs(src_rows, total, length, trows, a_refs, b_refs):
        for d in range(2):
            a_refs[d][pl.ds(length, total - length), :] = jnp.ones((total - length, LRU_CHUNK), F32)
            b_refs[d][pl.ds(length, total - length), :] = jnp.zeros((total - length, LRU_CHUNK), F32)
        xpad[pl.ds(0, SUBLANES), :] = zeros8
        xpad[pl.ds(SUBLANES + length, SUBLANES), :] = zeros8
        xpad[pl.ds(SUBLANES, length), :] = src_rows

        def tile(t, carry):
            t0 = pl.multiple_of(t * trows, SUBLANES)
            win = xpad[pl.ds(t0, trows + 2 * SUBLANES), :]
            xc = _conv_tile(win, cw, cb, trows)
            z = jnp.dot(xc.astype(BF16), wcat, preferred_element_type=F32)
            for d in range(2):
                a, b = _lru_coeff_tile(xc, z, gbias, sp, d)
                a_refs[d][pl.ds(t0, trows), :] = a
                b_refs[d][pl.ds(t0, trows), :] = b
            return carry
        lax.fori_loop(0, length // trows, tile, 0)

    fill_coeffs(xc_ref[...], SUBLANES * cpitch, ctx_len, ctx_len, (ca0, ca1), (cb0, cb1))
    h_init = []
    for d, (ar, br) in enumerate(((ca0, cb0), (ca1, cb1))):
        p_end, h_end = _chunk_totals(ar, br, cpitch, reverse=(d == 1))
        _, final = _chunk_starts(p_end, h_end, jnp.zeros((1, LRU_CHUNK), F32), reverse=(d == 1))
        h_init.append(final)

    fill_coeffs(x_ref[...], SUBLANES * pitch, n, LRU_TROWS, (a0, a1), (b0, b1))
    for d, (ar, br, hr) in enumerate(((a0, b0, h0s), (a1, b1, h1s))):
        p_end, h_end = _chunk_totals(ar, br, pitch, reverse=(d == 1))
        starts, _ = _chunk_starts(p_end, h_end, h_init[d], reverse=(d == 1))
        _scan_write(ar, br, hr, starts, pitch, reverse=(d == 1))

    def out_tile(t, carry):
        t0 = pl.multiple_of(t * LRU_TROWS, SUBLANES)
        y = h0s[pl.ds(t0, LRU_TROWS), :] + h1s[pl.ds(t0, LRU_TROWS), :]
        g = g_ref[pl.ds(t0, LRU_TROWS), :]
        gelu = 0.5 * g * (1.0 + jnp.tanh(0.7978845608028654 * (g + 0.044715 * (g * g * g))))
        o_ref[pl.ds(t0, LRU_TROWS), :] = (gelu * y).astype(o_ref.dtype)
        return carry
    lax.fori_loop(0, n // LRU_TROWS, out_tile, 0)


def _lru(xb, gb, xb_ctx, conv_w, conv_b, wcat, gbias, lam, batch, n, ctx_len):
    nch = LRU_WIDTH // LRU_CHUNK
    pitch = _scan_pitch(n)
    cpitch = _scan_pitch(ctx_len)
    big = pltpu.VMEM((SUBLANES * pitch, LRU_CHUNK), F32)
    small = pltpu.VMEM((SUBLANES * cpitch, LRU_CHUNK), F32)
    return pl.pallas_call(
        functools.partial(_lru_kernel, n=n, ctx_len=ctx_len),
        out_shape=jax.ShapeDtypeStruct((batch * n, LRU_WIDTH), BF16),
        grid=(batch, nch),
        in_specs=[pl.BlockSpec((n, LRU_CHUNK), lambda b, c: (b, c)),
                  pl.BlockSpec((n, LRU_CHUNK), lambda b, c: (b, c)),
                  pl.BlockSpec((ctx_len, LRU_CHUNK), lambda b, c: (b, c)),
                  pl.BlockSpec((4, LRU_CHUNK), lambda b, c: (0, c)),
                  pl.BlockSpec((1, LRU_CHUNK), lambda b, c: (0, c)),
                  pl.BlockSpec((1, LRU_CHUNK, 4 * LRU_CHUNK), lambda b, c: (c, 0, 0)),
                  pl.BlockSpec((1, 1, 4 * LRU_CHUNK), lambda b, c: (c, 0, 0)),
                  pl.BlockSpec((2, LRU_CHUNK), lambda b, c: (0, c))],
        out_specs=pl.BlockSpec((n, LRU_CHUNK), lambda b, c: (b, c)),
        scratch_shapes=[pltpu.VMEM((n + 2 * SUBLANES, LRU_CHUNK), F32),
                        big, big, big, big, big, big, small, small, small, small],
        compiler_params=_cparams(("arbitrary", "arbitrary")),
        name="rglru",
    )(xb, gb, xb_ctx, conv_w, conv_b, wcat, gbias, lam)


def _lru_gate_weights(w_r, b_r, w_i, b_i):
    nch = LRU_WIDTH // LRU_CHUNK
    bpc = LRU_CHUNK // LRU_BLOCK

    def dense(w):
        wc = w.reshape(nch, bpc, LRU_BLOCK, LRU_BLOCK)
        eye = jnp.eye(bpc, dtype=w.dtype)
        return jnp.einsum("cbij,bd->cbidj", wc, eye).reshape(nch, LRU_CHUNK, LRU_CHUNK)

    wcat = jnp.concatenate([dense(w_r[0]), dense(w_i[0]), dense(w_r[1]), dense(w_i[1])], axis=-1).astype(BF16)
    chunk = lambda v: v.reshape(nch, 1, LRU_CHUNK)
    gbias = jnp.concatenate([chunk(b_r[0]), chunk(b_i[0]), chunk(b_r[1]), chunk(b_i[1])], axis=-1).astype(F32)
    return wcat, gbias


def _route(s, sel, route_ref):
    srow = [s[e:e + 1, :] for e in range(N_EXPERTS)]
    lrow = [sel[e:e + 1, :] for e in range(N_EXPERTS)]
    gscore = []
    for g in range(N_EXPERT_GROUPS):
        a = lrow[g * EXPERTS_PER_GROUP:(g + 1) * EXPERTS_PER_GROUP]
        best = a[0] + a[1]
        for i, j in ((0, 2), (0, 3), (1, 2), (1, 3), (2, 3)):
            best = jnp.maximum(best, a[i] + a[j])
        gscore.append(best)
    bg = jnp.zeros_like(gscore[0], dtype=jnp.int32)
    bv = gscore[0]
    for g in range(1, N_EXPERT_GROUPS):
        upd = gscore[g] > bv
        bg = jnp.where(upd, g, bg)
        bv = jnp.where(upd, gscore[g], bv)

    def pick(rows_):
        out = []
        for j in range(EXPERTS_PER_GROUP):
            v = rows_[j]
            for g in range(1, N_EXPERT_GROUPS):
                v = jnp.where(bg == g, rows_[g * EXPERTS_PER_GROUP + j], v)
            out.append(v)
        return out
    cand = pick(lrow)
    cs = pick(srow)
    i1 = jnp.zeros_like(bg)
    v1 = cand[0]
    w1 = cs[0]
    for j in range(1, EXPERTS_PER_GROUP):
        upd = cand[j] > v1
        i1 = jnp.where(upd, j, i1)
        v1 = jnp.where(upd, cand[j], v1)
        w1 = jnp.where(upd, cs[j], w1)
    i2 = jnp.full_like(bg, -1)
    v2 = jnp.full_like(v1, -jnp.inf)
    w2 = jnp.zeros_like(w1)
    for j in range(EXPERTS_PER_GROUP):
        upd = (i1 != j) & (cand[j] > v2)
        i2 = jnp.where(upd, j, i2)
        v2 = jnp.where(upd, cand[j], v2)
        w2 = jnp.where(upd, cs[j], w2)
    den = w1 + w2
    g1 = w1 / den
    g2 = w2 / den
    for j in range(EXPERTS_PER_GROUP):
        route_ref[j:j + 1, :] = jnp.where(i1 == j, g1, 0.0) + jnp.where(i2 == j, g2, 0.0)
    route_ref[ROUTE_GID_ROW:ROUTE_GID_ROW + 1, :] = bg.astype(F32)
    pad = SUBLANES - ROUTE_GID_ROW - 1
    route_ref[ROUTE_GID_ROW + 1:, :] = jnp.zeros((pad, bg.shape[1]), F32)


def _post_kernel(a_ref, b_ref, w_ref, x_ref, gate_ref, shift_ref, scale_ref, gain_ref, rw_ref, rb_ref,
                 x1_ref, h2_ref, route_ref):
    half = a_ref.shape[1]
    mix = (jnp.dot(a_ref[...], w_ref[:half, :], preferred_element_type=F32)
           + jnp.dot(b_ref[...], w_ref[half:, :], preferred_element_type=F32))
    x1 = x_ref[...] + gate_ref[0] * mix
    x1_ref[...] = x1
    h2 = _norm_modulate(x1, gain_ref[...], shift_ref[0], scale_ref[0])
    h2_ref[...] = h2.astype(h2_ref.dtype)
    logits = lax.dot_general(rw_ref[...], h2, (((1,), (1,)), ((), ())), precision=HIGHEST,
                             preferred_element_type=F32)
    s = _sigmoid(logits)
    _route(s, s + rb_ref[...], route_ref)


def _post_mixer(a, a_col, b, b_col, w, x2d, gate1, shift2, scale2, gain, rw_t, rbias, tokens_per_batch, tm):
    t, d = x2d.shape
    tpb = tokens_per_batch // tm
    half = d // 2
    full = lambda i: (0, 0)
    per_b = lambda i: (i // tpb, 0, 0)
    return pl.pallas_call(
        _post_kernel,
        out_shape=[jax.ShapeDtypeStruct((t, d), F32), jax.ShapeDtypeStruct((t, d), BF16),
                   jax.ShapeDtypeStruct((SUBLANES, t), F32)],
        grid=(t // tm,),
        in_specs=[pl.BlockSpec((tm, half), lambda i: (i, a_col)),
                  pl.BlockSpec((tm, half), lambda i: (i, b_col)),
                  pl.BlockSpec(w.shape, full),
                  pl.BlockSpec((tm, d), lambda i: (i, 0)),
                  pl.BlockSpec((1, 1, d), per_b),
                  pl.BlockSpec((1, 1, d), per_b),
                  pl.BlockSpec((1, 1, d), per_b),
                  pl.BlockSpec((1, d), full),
                  pl.BlockSpec(rw_t.shape, full),
                  pl.BlockSpec(rbias.shape, full)],
        out_specs=[pl.BlockSpec((tm, d), lambda i: (i, 0)),
                   pl.BlockSpec((tm, d), lambda i: (i, 0)),
                   pl.BlockSpec((SUBLANES, tm), lambda i: (0, i))],
        compiler_params=_cparams(("arbitrary",)),
        name="post_mixer",
    )(a, b, w, x2d, gate1, shift2, scale2, gain, rw_t, rbias)


def _moe_layout(t):
    nt = t // MOE_TILE
    grid = -(-(t + N_EXPERT_GROUPS * (ROW_ALIGN - 1) * nt) // MOE_TILE) + N_EXPERT_GROUPS
    return nt, grid


def _moe_tables(gid, t):
    nt, grid = _moe_layout(t)
    ng = N_EXPERT_GROUPS
    per_tile = MOE_TILE // ROW_ALIGN
    onehot = (gid.reshape(nt, MOE_TILE, 1) == jnp.arange(ng, dtype=jnp.int32)).astype(jnp.int32)
    cnt = onehot.sum(axis=1)
    seg = (cnt + ROW_ALIGN - 1) // ROW_ALIGN
    src = jnp.cumsum(seg, axis=1) - seg
    fill = seg.sum(axis=0)
    ntile = (fill + per_tile - 1) // per_tile
    cum = jnp.cumsum(ntile)
    base = (cum - ntile) * per_tile
    dst = jnp.cumsum(seg, axis=0) - seg + base[None, :]
    seg_tab = jnp.concatenate([seg, src, dst], axis=1).reshape(-1).astype(jnp.int32)
    tail = (-fill) % per_tile
    tail_tab = jnp.concatenate([tail, fill + base, cum[-1:]]).astype(jnp.int32)
    i = jnp.arange(grid, dtype=jnp.int32)
    valid = i < cum[-1]
    ie = jnp.minimum(i, cum[-1] - 1)
    g_of = jnp.sum((ie[:, None] >= cum[None, :]).astype(jnp.int32), axis=1)
    return seg_tab, tail_tab, g_of.astype(jnp.int32), valid.astype(jnp.int32)


def _segment_copies(tab_ref, base, bits, make_copy):
    ng = N_EXPERT_GROUPS
    out = []
    for g in range(ng):
        n = tab_ref[base + g]
        src = tab_ref[base + ng + g]
        dst = tab_ref[base + 2 * ng + g]
        for k in range(bits - 1, -1, -1):
            done = (n >> (k + 1)) << (k + 1)
            rows = ROW_ALIGN << k
            s0 = pl.multiple_of((src + done) * ROW_ALIGN, ROW_ALIGN)
            d0 = pl.multiple_of((dst + done) * ROW_ALIGN, ROW_ALIGN)
            out.append((((n >> k) & 1) == 1, make_copy(s0, d0, rows)))
    return out


def _run_copies(pairs):
    for cond, copies in pairs:
        @pl.when(cond)
        def _():
            for c in copies:
                c.start()
    for cond, copies in pairs:
        @pl.when(cond)
        def _():
            for c in copies:
                c.wait()


def _dispatch_kernel(seg_ref, tail_ref, h_ref, route_ref, tri_ref, slot_ref, hs_ref, gs_ref,
                     cbuf, gbuf, zh, zg, sem, *, nt):
    i = pl.program_id(0)
    tm = h_ref.shape[0]
    ng = N_EXPERT_GROUPS
    route = route_ref[...]
    gid = route[ROUTE_GID_ROW:ROUTE_GID_ROW + 1, :]
    grp = lax.broadcasted_iota(jnp.int32, (SUBLANES, tm), 0).astype(F32)
    onehot = jnp.where(grp == gid, 1.0, 0.0)
    rank = jnp.dot(onehot.astype(BF16), tri_ref[...], preferred_element_type=F32)
    slot = jnp.zeros((1, tm), F32)
    for g in range(ng):
        start = (seg_ref[i * 3 * ng + ng + g] * ROW_ALIGN).astype(F32)
        slot = slot + onehot[g:g + 1, :] * (rank[g:g + 1, :] - 1.0 + start)
    slot_ref[...] = jnp.broadcast_to(slot, (SUBLANES, tm))
    perm = jnp.where(lax.broadcasted_iota(jnp.int32, (MOE_CROWS, tm), 0).astype(F32) == slot, 1.0, 0.0)
    cbuf[...] = jnp.dot(perm.astype(BF16), h_ref[...], preferred_element_type=F32).astype(cbuf.dtype)
    route_pad = jnp.concatenate([route, jnp.zeros((LANES - SUBLANES, tm), F32)], axis=0)
    gbuf[...] = lax.dot_general(perm, route_pad, (((1,), (1,)), ((), ())), precision=HIGHEST,
                                preferred_element_type=F32)

    def seg_copy(s0, d0, rows):
        return (pltpu.make_async_copy(cbuf.at[pl.ds(s0, rows)], hs_ref.at[pl.ds(d0, rows)], sem.at[0]),
                pltpu.make_async_copy(gbuf.at[pl.ds(s0, rows)], gs_ref.at[pl.ds(d0, rows)], sem.at[1]))
    _run_copies(_segment_copies(seg_ref, i * 3 * ng, MOE_SEG_BITS, seg_copy))

    @pl.when(i == pl.num_programs(0) - 1)
    def _():
        zh[...] = jnp.zeros(zh.shape, zh.dtype)
        zg[...] = jnp.zeros(zg.shape, zg.dtype)

        def zero_copy(d0, rows):
            return (pltpu.make_async_copy(zh.at[pl.ds(0, rows)], hs_ref.at[pl.ds(d0, rows)], sem.at[0]),
                    pltpu.make_async_copy(zg.at[pl.ds(0, rows)], gs_ref.at[pl.ds(d0, rows)], sem.at[1]))
        pairs = []
        for g in range(ng):
            n = tail_ref[g]
            dst = tail_ref[ng + g]
            for k in range(MOE_TAIL_BITS - 1, -1, -1):
                done = (n >> (k + 1)) << (k + 1)
                d0 = pl.multiple_of((dst + done) * ROW_ALIGN, ROW_ALIGN)
                pairs.append((((n >> k) & 1) == 1, zero_copy(d0, ROW_ALIGN << k)))
        used = tail_ref[2 * ng]
        total = hs_ref.shape[0] // MOE_TILE
        for j in range(total - nt):
            d0 = pl.multiple_of(jnp.minimum(used + j, total - 1) * MOE_TILE, MOE_TILE)
            pairs.append((used + j < total, zero_copy(d0, MOE_TILE)))
        _run_copies(pairs)


def _dispatch(seg_tab, tail_tab, h2, route, tri):
    t, d = h2.shape
    nt, grid = _moe_layout(t)
    rows = grid * MOE_TILE
    grid_spec = pltpu.PrefetchScalarGridSpec(
        num_scalar_prefetch=2,
        grid=(nt,),
        in_specs=[pl.BlockSpec((MOE_TILE, d), lambda i, *_: (i, 0)),
                  pl.BlockSpec((SUBLANES, MOE_TILE), lambda i, *_: (0, i)),
                  pl.BlockSpec((MOE_TILE, MOE_TILE), lambda i, *_: (0, 0))],
        out_specs=[pl.BlockSpec((SUBLANES, MOE_TILE), lambda i, *_: (0, i)),
                   pl.BlockSpec(memory_space=pl.ANY),
                   pl.BlockSpec(memory_space=pl.ANY)],
        scratch_shapes=[pltpu.VMEM((MOE_CROWS, d), BF16), pltpu.VMEM((MOE_CROWS, LANES), F32),
                        pltpu.VMEM((MOE_TILE, d), BF16), pltpu.VMEM((MOE_TILE, LANES), F32),
                        pltpu.SemaphoreType.DMA((2,))])
    return pl.pallas_call(
        functools.partial(_dispatch_kernel, nt=nt),
        out_shape=[jax.ShapeDtypeStruct((SUBLANES, t), F32),
                   jax.ShapeDtypeStruct((rows, d), BF16),
                   jax.ShapeDtypeStruct((rows, LANES), F32)],
        grid_spec=grid_spec,
        compiler_params=_cparams(("arbitrary",)),
        name="moe_dispatch",
    )(seg_tab, tail_tab, h2, route, tri)


def _ffn_kernel(grp_ref, valid_ref, h_ref, g_ref, wg_ref, wu_ref, wd_ref, y_ref):
    @pl.when(valid_ref[pl.program_id(0)] == 0)
    def _():
        y_ref[...] = jnp.zeros(y_ref.shape, y_ref.dtype)

    @pl.when(valid_ref[pl.program_id(0)] == 1)
    def _():
        h = h_ref[...]
        gates = g_ref[...]
        y = jnp.zeros(y_ref.shape, F32)
        for j in range(EXPERTS_PER_GROUP):
            a = jnp.dot(h, wg_ref[j], preferred_element_type=F32)
            u = jnp.dot(h, wu_ref[j], preferred_element_type=F32)
            act = (a * _sigmoid(a)) * u * gates[:, j:j + 1]
            y = y + jnp.dot(act.astype(BF16), wd_ref[j], preferred_element_type=F32)
        y_ref[...] = y.astype(y_ref.dtype)


def _ffn(grp, valid, hs, gs, wg, wu, wd):
    rows, d = hs.shape
    epg = EXPERTS_PER_GROUP
    grid_spec = pltpu.PrefetchScalarGridSpec(
        num_scalar_prefetch=2,
        grid=(rows // MOE_TILE,),
        in_specs=[pl.BlockSpec((MOE_TILE, d), lambda i, grp, valid: (i, 0)),
                  pl.BlockSpec((MOE_TILE, LANES), lambda i, grp, valid: (i, 0)),
                  pl.BlockSpec((epg, d, D_FF_EXPERT), lambda i, grp, valid: (grp[i], 0, 0)),
                  pl.BlockSpec((epg, d, D_FF_EXPERT), lambda i, grp, valid: (grp[i], 0, 0)),
                  pl.BlockSpec((epg, D_FF_EXPERT, d), lambda i, grp, valid: (grp[i], 0, 0))],
        out_specs=pl.BlockSpec((MOE_TILE, d), lambda i, grp, valid: (i, 0)))
    return pl.pallas_call(
        _ffn_kernel,
        out_shape=jax.ShapeDtypeStruct((rows, d), BF16),
        grid_spec=grid_spec,
        compiler_params=_cparams(("arbitrary",)),
        name="moe_ffn",
    )(grp, valid, hs, gs, wg, wu, wd)


def _combine_kernel(seg_ref, x1_ref, slot_ref, gate2_ref, ys_ref, o_ref, ybuf, sem):
    i = pl.program_id(0)
    tm = x1_ref.shape[0]
    ybuf[...] = jnp.zeros(ybuf.shape, ybuf.dtype)

    def seg_copy(s0, d0, rows):
        return (pltpu.make_async_copy(ys_ref.at[pl.ds(d0, rows)], ybuf.at[pl.ds(s0, rows)], sem.at[0]),)
    _run_copies(_segment_copies(seg_ref, i * 3 * N_EXPERT_GROUPS, MOE_SEG_BITS, seg_copy))
    slot = slot_ref[0:1, :]
    perm = jnp.where(lax.broadcasted_iota(jnp.int32, (MOE_CROWS, tm), 0).astype(F32) == slot, 1.0, 0.0)
    y = lax.dot_general(perm.astype(BF16), ybuf[...], (((0,), (0,)), ((), ())), preferred_element_type=F32)
    o_ref[...] = x1_ref[...] + gate2_ref[0] * y


def _combine(seg_tab, x1, slot, gate2, ys, tokens_per_batch):
    t, d = x1.shape
    tpb = tokens_per_batch // MOE_TILE
    grid_spec = pltpu.PrefetchScalarGridSpec(
        num_scalar_prefetch=1,
        grid=(t // MOE_TILE,),
        in_specs=[pl.BlockSpec((MOE_TILE, d), lambda i, *_: (i, 0)),
                  pl.BlockSpec((SUBLANES, MOE_TILE), lambda i, *_: (0, i)),
                  pl.BlockSpec((1, 1, d), lambda i, *_: (i // tpb, 0, 0)),
                  pl.BlockSpec(memory_space=pl.ANY)],
        out_specs=pl.BlockSpec((MOE_TILE, d), lambda i, *_: (i, 0)),
        scratch_shapes=[pltpu.VMEM((MOE_CROWS, d), BF16), pltpu.SemaphoreType.DMA((1,))])
    return pl.pallas_call(
        _combine_kernel,
        out_shape=jax.ShapeDtypeStruct((t, d), F32),
        grid_spec=grid_spec,
        compiler_params=_cparams(("arbitrary",)),
        name="moe_combine",
    )(seg_tab, x1, slot, gate2, ys)


def _grouped_moe(h2, route, x1, gate2, wg, wu, wd, tri, tokens_per_batch):
    t = h2.shape[0]
    gid = route[ROUTE_GID_ROW].astype(jnp.int32)
    seg_tab, tail_tab, grp, valid = _moe_tables(gid, t)
    slot, hs, gs = _dispatch(seg_tab, tail_tab, h2, route, tri)
    ys = _ffn(grp, valid, hs, gs, wg, wu, wd)
    return _combine(seg_tab, x1, slot, gate2, ys, tokens_per_batch)


def _fnet_chan_kernel(x_ref, shift_ref, scale_ref, gain_ref, cs_ref, y1_ref, y2_ref):
    h = _norm_modulate(x_ref[...], gain_ref[...], shift_ref[0], scale_ref[0]).astype(BF16)
    gw = D_MODEL // FNET_GROUPS
    for g in range(FNET_GROUPS):
        y = jnp.dot(h[:, g * gw:(g + 1) * gw], cs_ref[...], preferred_element_type=F32)
        y1_ref[:, g * gw:(g + 1) * gw] = y[:, :gw].astype(y1_ref.dtype)
        y2_ref[:, g * gw:(g + 1) * gw] = y[:, gw:].astype(y2_ref.dtype)


def _fnet_channel(x2d, shift, scale, gain, cs, tokens_per_batch, tm):
    t, d = x2d.shape
    tpb = tokens_per_batch // tm
    full = lambda i: (0, 0)
    return pl.pallas_call(
        _fnet_chan_kernel,
        out_shape=[jax.ShapeDtypeStruct((t, d), BF16), jax.ShapeDtypeStruct((t, d), BF16)],
        grid=(t // tm,),
        in_specs=[pl.BlockSpec((tm, d), lambda i: (i, 0)),
                  pl.BlockSpec((1, 1, d), lambda i: (i // tpb, 0, 0)),
                  pl.BlockSpec((1, 1, d), lambda i: (i // tpb, 0, 0)),
                  pl.BlockSpec((1, d), full),
                  pl.BlockSpec(cs.shape, full)],
        out_specs=[pl.BlockSpec((tm, d), lambda i: (i, 0)), pl.BlockSpec((tm, d), lambda i: (i, 0))],
        compiler_params=_cparams(("arbitrary",)),
        name="fnet_channel",
    )(x2d, shift, scale, gain, cs)


def _fnet_pos_kernel(c_ref, s_ref, y1_ref, y2_ref, o_ref):
    o = (jnp.dot(c_ref[...], y1_ref[...], preferred_element_type=F32)
         - jnp.dot(s_ref[...], y2_ref[...], preferred_element_type=F32))
    o_ref[...] = o.astype(o_ref.dtype)


def _fnet_position(cn, sn, y1, y2, batch, n, tm):
    t, d = y1.shape
    halves = 2
    dh = d // halves
    mt = n // tm
    return pl.pallas_call(
        _fnet_pos_kernel,
        out_shape=jax.ShapeDtypeStruct((t, d), BF16),
        grid=(batch, halves, mt),
        in_specs=[pl.BlockSpec((tm, n), lambda b, c, m: (m, 0)),
                  pl.BlockSpec((tm, n), lambda b, c, m: (m, 0)),
                  pl.BlockSpec((n, dh), lambda b, c, m: (b, c)),
                  pl.BlockSpec((n, dh), lambda b, c, m: (b, c))],
        out_specs=pl.BlockSpec((tm, dh), lambda b, c, m: (b * mt + m, c)),
        compiler_params=_cparams(("arbitrary", "arbitrary", "arbitrary")),
        name="fnet_position",
    )(cn, sn, y1, y2)


def _dft_tables(n):
    gw = D_MODEL // FNET_GROUPS
    j = np.arange(gw)
    ang = 2.0 * np.pi * ((j[:, None] * j[None, :]) % gw) / gw
    cs = np.concatenate([np.cos(ang), np.sin(ang)], axis=1) / np.sqrt(gw)
    ra = n // GRID_W
    k = np.arange(n)
    a = np.arange(ra)
    b = np.arange(GRID_W)
    ang_a = 2.0 * np.pi * ((a[:, None] * GRID_W * k[None, :]) % n) / n
    ang_b = 2.0 * np.pi * ((b[:, None] * k[None, :]) % n) / n
    scale = 1.0 / np.sqrt(n)
    ca, sa = jnp.asarray(np.cos(ang_a), F32), jnp.asarray(np.sin(ang_a), F32)
    cb, sb = jnp.asarray(np.cos(ang_b) * scale, F32), jnp.asarray(np.sin(ang_b) * scale, F32)
    cn = (ca[:, None, :] * cb[None, :, :] - sa[:, None, :] * sb[None, :, :]).reshape(n, n).astype(BF16)
    sn = (sa[:, None, :] * cb[None, :, :] + ca[:, None, :] * sb[None, :, :]).reshape(n, n).astype(BF16)
    return jnp.asarray(cs, BF16), cn, sn


def kernel(x, c, ctx, c_ctx, ada_w, ada_b, norm_mix, norm_ffn, mix_w_in, mix_w_out, na_q_norm, na_k_norm, na_rpb,
           lru_conv_w, lru_conv_b, lru_gate_r_w, lru_gate_r_b, lru_gate_i_w, lru_gate_i_b, lru_lambda,
           fnet_w_out, router_w, router_bias, moe_w_gate, moe_w_up, moe_w_down):
    batch, n, d = x.shape
    ctx_len = ctx.shape[1]
    depth = ada_w.shape[0]
    rows = n // GRID_W
    assert d == D_MODEL and n % (GRID_W * NA_QROWS) == 0 and rows >= 4 * NA_QROWS
    assert n % MOE_TILE == 0
    t = batch * n
    tm = 512
    tri = jnp.asarray(np.triu(np.ones((MOE_TILE, MOE_TILE))), BF16)

    r_pad = -(-(batch + 1) // SUBLANES) * SUBLANES
    c_rows = jnp.concatenate([c, c_ctx[None, :], jnp.zeros((r_pad - batch - 1, d), c.dtype)], axis=0)
    mod = _modulation(c_rows, ada_w, ada_b)

    def mod_slices(layer):
        m = mod[layer, :batch].reshape(batch, 1, 6, d)
        return [m[:, :, i, :] for i in range(6)]

    rw_t = router_w.T.astype(F32)
    rbias = router_bias.reshape(N_EXPERTS, 1).astype(F32)
    x2d = x.reshape(t, d)
    ctx2d = ctx.reshape(batch * ctx_len, d)

    for layer in range(depth):
        li = layer // 2
        shift1, scale1, gate1, shift2, scale2, gate2 = mod_slices(layer)
        gain_mix = norm_mix[layer].reshape(1, d)
        gain_ffn = norm_ffn[layer].reshape(1, d)
        if layer % 2 == 0:
            w_in = mix_w_in[li].astype(BF16)
            ind = jnp.asarray(np.kron(np.eye(NA_HEADS), np.ones((HEAD_DIM, HEAD_DIM))), BF16)
            qg = (jnp.tile(na_q_norm[li], NA_HEADS) * HEAD_DIM ** -0.5).reshape(1, NA_WIDTH).astype(F32)
            kg = jnp.tile(na_k_norm[li], NA_HEADS).reshape(1, NA_WIDTH).astype(F32)
            q, k, v, xb, gb = _inproj(x2d, shift1, scale1, gain_mix, w_in, ind, qg, kg,
                                      ("q", "k", "v", "x", "g"), n, tm)
            mctx = mod[layer, batch, :2 * d]
            shift_c = jnp.broadcast_to(mctx[:d], (batch, 1, d))
            scale_c = jnp.broadcast_to(mctx[d:], (batch, 1, d))
            k_c, v_c, xb_c = _inproj(ctx2d, shift_c, scale_c, gain_mix, w_in[:, NA_WIDTH:4 * NA_WIDTH], ind, qg, kg,
                                     ("k", "v", "x"), ctx_len, ctx_len)
            bias = _na_bias_tables(na_rpb[li], rows)
            attn = _attention(q, k, v, k_c, v_c, bias, batch, n, ctx_len)
            wcat, gbias = _lru_gate_weights(lru_gate_r_w[li], lru_gate_r_b[li], lru_gate_i_w[li], lru_gate_i_b[li])
            lru = _lru(xb, gb, xb_c, lru_conv_w[li].astype(F32), lru_conv_b[li].reshape(1, LRU_WIDTH).astype(F32),
                       wcat, gbias, lru_lambda[li].astype(F32), batch, n, ctx_len)
            mix_a, col_a, mix_b, col_b, w_out = attn, 0, lru, 0, mix_w_out[li].astype(BF16)
        else:
            cs, cn, sn = _dft_tables(n)
            y1, y2 = _fnet_channel(x2d, shift1, scale1, gain_mix, cs, n, tm)
            f = _fnet_position(cn, sn, y1, y2, batch, n, tm)
            mix_a, col_a, mix_b, col_b, w_out = f, 0, f, 1, fnet_w_out[li].astype(BF16)
        x1, h2, route = _post_mixer(mix_a, col_a, mix_b, col_b, w_out, x2d, gate1, shift2, scale2, gain_ffn,
                                    rw_t, rbias, n, tm)
        x2d = _grouped_moe(h2, route, x1, gate2, moe_w_gate[layer].astype(BF16), moe_w_up[layer].astype(BF16),
                           moe_w_down[layer].astype(BF16), tri, n)
    return x2d.reshape(batch, n, d)
```

```python
import functools

import numpy as np
import jax
import jax.numpy as jnp
from jax import lax
from jax.experimental import pallas as pl
from jax.experimental.pallas import tpu as pltpu

F32 = jnp.float32
BF16 = jnp.bfloat16
HIGHEST = lax.Precision.HIGHEST

D_MODEL = 1024
GRID_W = 64
HEAD_DIM = 64
NA_HEADS = 8
NA_WIDTH = NA_HEADS * HEAD_DIM
NA_WIN_ROWS = 8
NA_WIN_COLS = 16
LRU_WIDTH = 512
LRU_BLOCK = 64
LRU_C = 8.0
FNET_GROUPS = 4
N_EXPERTS = 16
EXPERTS_PER_GROUP = 4
N_EXPERT_GROUPS = 4
D_FF_EXPERT = 512
RMS_EPS = 1e-6
MASK_VALUE = -1e30

V7X_VMEM_LIMIT_BYTES = 56 * 1024 * 1024
LANES = 128
SUBLANES = 8

NA_QROWS = 4
NA_KROWS = NA_QROWS + NA_WIN_ROWS - 1
NA_QBLK = NA_QROWS * GRID_W
NA_KBLK = NA_KROWS * GRID_W

LRU_CHUNK = LANES
LRU_TROWS = 512

ROUTE_GID_ROW = EXPERTS_PER_GROUP
MOE_TILE = 512
ROW_ALIGN = 16
MOE_CROWS = MOE_TILE + N_EXPERT_GROUPS * ROW_ALIGN
MOE_SEG_BITS = (MOE_TILE // ROW_ALIGN).bit_length()
MOE_TAIL_BITS = (MOE_TILE // ROW_ALIGN - 1).bit_length()


def _sigmoid(x):
    return 1.0 / (1.0 + jnp.exp(-x))


def _sigmoid_tanh(x):
    return 0.5 + 0.5 * jnp.tanh(0.5 * x)


def _cparams(sem, vmem=V7X_VMEM_LIMIT_BYTES):
    return pltpu.CompilerParams(dimension_semantics=sem, vmem_limit_bytes=vmem)


def _mod_kernel(c_ref, w_ref, b_ref, o_ref):
    c = c_ref[...]
    s = c * _sigmoid(c)
    o_ref[0] = jnp.dot(s, w_ref[0], precision=HIGHEST, preferred_element_type=F32) + b_ref[0]


def _modulation(c_rows, ada_w, ada_b):
    depth, d, n6 = ada_w.shape
    r = c_rows.shape[0]
    tn = 1536
    return pl.pallas_call(
        _mod_kernel,
        out_shape=jax.ShapeDtypeStruct((depth, r, n6), F32),
        grid=(depth, n6 // tn),
        in_specs=[pl.BlockSpec((r, d), lambda l, j: (0, 0)),
                  pl.BlockSpec((1, d, tn), lambda l, j: (l, 0, j)),
                  pl.BlockSpec((1, 1, tn), lambda l, j: (l, 0, j))],
        out_specs=pl.BlockSpec((1, r, tn), lambda l, j: (l, 0, j)),
        compiler_params=_cparams(("arbitrary", "arbitrary")),
        name="adaln_mod",
    )(c_rows, ada_w, ada_b.reshape(depth, 1, n6))


def _norm_modulate(x, gain, shift, scale):
    ms = jnp.mean(x * x, axis=-1, keepdims=True)
    y = x * lax.rsqrt(ms + RMS_EPS) * gain
    return y * (1.0 + scale) + shift


def _inproj_kernel(x_ref, shift_ref, scale_ref, gain_ref, w_ref, ind_ref, qg_ref, kg_ref, *out_refs, segs):
    h = _norm_modulate(x_ref[...], gain_ref[...], shift_ref[0], scale_ref[0]).astype(BF16)
    for s, (kind, o_ref) in enumerate(zip(segs, out_refs)):
        z = jnp.dot(h, w_ref[:, s * NA_WIDTH:(s + 1) * NA_WIDTH], preferred_element_type=F32)
        if kind in ("q", "k"):
            ms = jnp.dot((z * z).astype(BF16), ind_ref[...], preferred_element_type=F32) * (1.0 / HEAD_DIM)
            g = qg_ref[...] if kind == "q" else kg_ref[...]
            z = z * lax.rsqrt(ms + RMS_EPS) * g
        if kind in ("x", "g"):
            for c in range(LRU_WIDTH // LRU_CHUNK):
                o_ref[c] = z[:, c * LRU_CHUNK:(c + 1) * LRU_CHUNK].astype(o_ref.dtype)
        else:
            o_ref[...] = z.astype(o_ref.dtype)


def _inproj(x2d, shift, scale, gain, w, ind, qg, kg, segs, tokens_per_batch, tm):
    t, d = x2d.shape
    tpb = tokens_per_batch // tm
    dt = {"q": BF16, "k": BF16, "v": BF16, "x": F32, "g": F32}
    full = lambda i: (0, 0)
    nch = LRU_WIDTH // LRU_CHUNK

    def out_shape(kind):
        shape = (nch, t, LRU_CHUNK) if kind in ("x", "g") else (t, NA_WIDTH)
        return jax.ShapeDtypeStruct(shape, dt[kind])

    def out_spec(kind):
        if kind in ("x", "g"):
            return pl.BlockSpec((nch, tm, LRU_CHUNK), lambda i: (0, i, 0))
        return pl.BlockSpec((tm, NA_WIDTH), lambda i: (i, 0))

    return pl.pallas_call(
        functools.partial(_inproj_kernel, segs=segs),
        out_shape=[out_shape(k) for k in segs],
        grid=(t // tm,),
        in_specs=[pl.BlockSpec((tm, d), lambda i: (i, 0)),
                  pl.BlockSpec((1, 1, d), lambda i: (i // tpb, 0, 0)),
                  pl.BlockSpec((1, 1, d), lambda i: (i // tpb, 0, 0)),
                  pl.BlockSpec((1, d), full),
                  pl.BlockSpec(w.shape, full),
                  pl.BlockSpec(ind.shape, full),
                  pl.BlockSpec((1, NA_WIDTH), full),
                  pl.BlockSpec((1, NA_WIDTH), full)],
        out_specs=[out_spec(k) for k in segs],
        compiler_params=_cparams(("arbitrary",)),
        name="inproj_" + "".join(segs),
    )(x2d, shift, scale, gain, w, ind, qg, kg)


def _na_bias_tables(rpb, rows):
    kr = NA_WIN_ROWS
    rb_count = rows // NA_QROWS
    cq = np.arange(GRID_W)
    ck = np.arange(GRID_W)
    col_start = np.clip(cq - NA_WIN_COLS // 2, 0, GRID_W - NA_WIN_COLS)
    valid_c = (ck[None, :] >= col_start[:, None]) & (ck[None, :] < col_start[:, None] + NA_WIN_COLS)
    dc = np.clip(ck[None, :] - cq[:, None], 1 - NA_WIN_COLS, NA_WIN_COLS - 1) + (NA_WIN_COLS - 1)
    n_dr, n_dc = 2 * NA_WIN_ROWS - 1, 2 * NA_WIN_COLS - 1
    sel_c = (dc[:, :, None] == np.arange(n_dc)) & valid_c[:, :, None]
    sel_r, mask = [], []
    for rb in (0, 1, rb_count - 1):
        r = rb * NA_QROWS + np.arange(NA_QROWS)
        ks = int(np.clip(rb * NA_QROWS - kr // 2, 0, rows - NA_KROWS))
        key_r = ks + np.arange(NA_KROWS)
        row_start = np.clip(r - kr // 2, 0, rows - kr)
        valid_r = (key_r[None, :] >= row_start[:, None]) & (key_r[None, :] < row_start[:, None] + kr)
        dr = np.clip(key_r[None, :] - r[:, None] + (NA_WIN_ROWS - 1), 0, n_dr - 1)
        sel_r.append((dr[:, :, None] == np.arange(n_dr)) & valid_r[:, :, None])
        valid = valid_r[:, None, :, None] & valid_c[None, :, None, :]
        mask.append(np.where(valid, 0.0, MASK_VALUE).reshape(NA_QBLK, NA_KBLK))
    sel_r = jnp.asarray(np.stack(sel_r), F32)
    b = jnp.einsum("hrc,tijr,qkc->thiqjk", rpb.astype(F32), sel_r, jnp.asarray(sel_c, F32), precision=HIGHEST)
    return b.reshape(3, NA_HEADS, NA_QBLK, NA_KBLK) + jnp.asarray(np.stack(mask), F32)[:, None]


def _attn_kernel(q_ref, k_ref, v_ref, kc_ref, vc_ref, bias_ref, o_ref, *, rows):
    rb = pl.program_id(1)
    ks = jnp.clip(rb * NA_QROWS - NA_WIN_ROWS // 2, 0, rows - NA_KROWS)
    kstart = pl.multiple_of(ks * GRID_W, GRID_W)
    nt = (((1,), (1,)), ((), ()))
    for h in range(NA_HEADS):
        hs = slice(h * HEAD_DIM, (h + 1) * HEAD_DIM)
        qh = q_ref[:, hs]
        kh = k_ref[pl.ds(kstart, NA_KBLK), hs]
        vh = v_ref[pl.ds(kstart, NA_KBLK), hs]
        s_w = lax.dot_general(qh, kh, nt, preferred_element_type=F32) + bias_ref[0, h]
        s_c = lax.dot_general(qh, kc_ref[:, hs], nt, preferred_element_type=F32)
        m = jnp.maximum(jnp.max(s_w, axis=-1, keepdims=True), jnp.max(s_c, axis=-1, keepdims=True))
        p_w = jnp.exp(s_w - m)
        p_c = jnp.exp(s_c - m)
        l = jnp.sum(p_w, axis=-1, keepdims=True) + jnp.sum(p_c, axis=-1, keepdims=True)
        o = (jnp.dot(p_w.astype(BF16), vh, preferred_element_type=F32)
             + jnp.dot(p_c.astype(BF16), vc_ref[:, hs], preferred_element_type=F32))
        o_ref[:, hs] = (o / l).astype(o_ref.dtype)


def _attention(q, k, v, kc, vc, bias, batch, n, ctx_len):
    rows = n // GRID_W
    rbc = rows // NA_QROWS

    def bias_idx(b, rb):
        return (jnp.where(rb == 0, 0, jnp.where(rb == rbc - 1, 2, 1)), 0, 0, 0)

    return pl.pallas_call(
        functools.partial(_attn_kernel, rows=rows),
        out_shape=jax.ShapeDtypeStruct((batch * n, NA_WIDTH), BF16),
        grid=(batch, rbc),
        in_specs=[pl.BlockSpec((NA_QBLK, NA_WIDTH), lambda b, rb: (b * rbc + rb, 0)),
                  pl.BlockSpec((n, NA_WIDTH), lambda b, rb: (b, 0)),
                  pl.BlockSpec((n, NA_WIDTH), lambda b, rb: (b, 0)),
                  pl.BlockSpec((ctx_len, NA_WIDTH), lambda b, rb: (b, 0)),
                  pl.BlockSpec((ctx_len, NA_WIDTH), lambda b, rb: (b, 0)),
                  pl.BlockSpec((1, NA_HEADS, NA_QBLK, NA_KBLK), bias_idx)],
        out_specs=pl.BlockSpec((NA_QBLK, NA_WIDTH), lambda b, rb: (b * rbc + rb, 0)),
        compiler_params=_cparams(("arbitrary", "arbitrary")),
        name="na_attention",
    )(q, k, v, kc, vc, bias)


def _scan_pitch(n):
    p = -(-n // SUBLANES)
    while p % 8 != 4:
        p += 1
    return p


NEG_LOG2_E = -1.4426950408889634


def _lru_coeff_tile(half_xc, zh, half_bias, k, d):
    c = LRU_CHUNK
    t_r = jnp.tanh(zh[:, (2 * d) * c:(2 * d + 1) * c] + half_bias[:, (2 * d) * c:(2 * d + 1) * c])
    t_i = jnp.tanh(zh[:, (2 * d + 1) * c:(2 * d + 2) * c] + half_bias[:, (2 * d + 1) * c:(2 * d + 2) * c])
    neg_log_a = k[d:d + 1, :] * (1.0 + t_r)
    a = jnp.exp2(neg_log_a * NEG_LOG2_E)
    one_minus_a2 = jnp.tanh(neg_log_a) * (a * a + 1.0)
    root = jnp.where(one_minus_a2 > 0.0, one_minus_a2 * lax.rsqrt(one_minus_a2), 0.0)
    return a, root * (half_xc + half_xc * t_i)


def _conv_tile(xpad, t0, w, b, rows):
    acc = b + w[0:1, :] * xpad[pl.ds(t0 + SUBLANES - 2, rows), :]
    acc = acc + w[1:2, :] * xpad[pl.ds(t0 + SUBLANES - 1, rows), :]
    acc = acc + w[2:3, :] * xpad[pl.ds(t0 + SUBLANES, rows), :]
    return acc + w[3:4, :] * xpad[pl.ds(t0 + SUBLANES + 1, rows), :]


SCAN_UNROLL = 4


def _strided_rows(ref, j, pitch):
    return ref[pl.ds(j, SUBLANES, stride=pitch), :]


def _chunk_totals(af_ref, bf_ref, ab_ref, bb_ref, pitch):
    def body(j, carry):
        pf, hf, pb, hb = carry
        jb = pitch - 1 - j
        af = _strided_rows(af_ref, j, pitch)
        ab = _strided_rows(ab_ref, jb, pitch)
        return (af * pf, af * hf + _strided_rows(bf_ref, j, pitch),
                ab * pb, ab * hb + _strided_rows(bb_ref, jb, pitch))
    one = jnp.ones((SUBLANES, LRU_CHUNK), F32)
    zero = jnp.zeros((SUBLANES, LRU_CHUNK), F32)
    return lax.fori_loop(0, pitch, body, (one, zero, one, zero), unroll=SCAN_UNROLL)


def _chunk_starts(p_end, h_end, h0, reverse):
    row = lax.broadcasted_iota(jnp.int32, (SUBLANES, LRU_CHUNK), 0)
    starts = jnp.zeros((SUBLANES, LRU_CHUNK), F32)
    state = h0
    order = range(SUBLANES - 1, -1, -1) if reverse else range(SUBLANES)
    for s in order:
        starts = jnp.where(row == s, state, starts)
        state = p_end[s:s + 1, :] * state + h_end[s:s + 1, :]
    return starts, state


def _scan_write(af_ref, bf_ref, hf_ref, ab_ref, bb_ref, hb_ref, starts_f, starts_b, pitch):
    def body(j, carry):
        hf, hb = carry
        jb = pitch - 1 - j
        hf = _strided_rows(af_ref, j, pitch) * hf + _strided_rows(bf_ref, j, pitch)
        hb = _strided_rows(ab_ref, jb, pitch) * hb + _strided_rows(bb_ref, jb, pitch)
        hf_ref[pl.ds(j, SUBLANES, stride=pitch), :] = hf
        hb_ref[pl.ds(jb, SUBLANES, stride=pitch), :] = hb
        return hf, hb
    lax.fori_loop(0, pitch, body, (starts_f, starts_b), unroll=SCAN_UNROLL)


def _lru_kernel(x_ref, g_ref, xc_ref, cw_ref, cb_ref, w_ref, gb_ref, lam_ref, o_ref,
                xpad, a0, b0, a1, b1, h0s, h1s, ca0, cb0, ca1, cb1, *, n, ctx_len):
    pitch = _scan_pitch(n)
    cpitch = _scan_pitch(ctx_len)
    cw = cw_ref[...]
    cb = cb_ref[...]
    gbias = gb_ref[0]
    lam = lam_ref[...]
    sp = jnp.maximum(-lam, 0.0) + jnp.log1p(jnp.exp(-jnp.abs(lam)))
    k = (0.5 * LRU_C) * sp
    wcat = w_ref[0]
    zeros8 = jnp.zeros((SUBLANES, LRU_CHUNK), F32)

    def fill_coeffs(src_rows, total, length, trows, a_refs, b_refs):
        for d in range(2):
            a_refs[d][pl.ds(length, total - length), :] = jnp.ones((total - length, LRU_CHUNK), F32)
            b_refs[d][pl.ds(length, total - length), :] = jnp.zeros((total - length, LRU_CHUNK), F32)
        xpad[pl.ds(0, SUBLANES), :] = zeros8
        xpad[pl.ds(SUBLANES + length, SUBLANES), :] = zeros8
        xpad[pl.ds(SUBLANES, length), :] = src_rows

        def tile(t, carry):
            t0 = pl.multiple_of(t * trows, SUBLANES)
            xc = _conv_tile(xpad, t0, cw, cb, trows)
            zh = jnp.dot(xc.astype(BF16), wcat, preferred_element_type=F32)
            half_xc = 0.5 * xc
            for d in range(2):
                a, b = _lru_coeff_tile(half_xc, zh, gbias, k, d)
                a_refs[d][pl.ds(t0, trows), :] = a
                b_refs[d][pl.ds(t0, trows), :] = b
            return carry
        lax.fori_loop(0, length // trows, tile, 0)

    fill_coeffs(xc_ref[0], SUBLANES * cpitch, ctx_len, ctx_len, (ca0, ca1), (cb0, cb1))
    zero_state = jnp.zeros((1, LRU_CHUNK), F32)
    pf, hf, pb, hb = _chunk_totals(ca0, cb0, ca1, cb1, cpitch)
    _, init_f = _chunk_starts(pf, hf, zero_state, reverse=False)
    _, init_b = _chunk_starts(pb, hb, zero_state, reverse=True)

    fill_coeffs(x_ref[0], SUBLANES * pitch, n, LRU_TROWS, (a0, a1), (b0, b1))
    pf, hf, pb, hb = _chunk_totals(a0, b0, a1, b1, pitch)
    starts_f, _ = _chunk_starts(pf, hf, init_f, reverse=False)
    starts_b, _ = _chunk_starts(pb, hb, init_b, reverse=True)
    _scan_write(a0, b0, h0s, a1, b1, h1s, starts_f, starts_b, pitch)

    def out_tile(t, carry):
        t0 = pl.multiple_of(t * LRU_TROWS, SUBLANES)
        y = h0s[pl.ds(t0, LRU_TROWS), :] + h1s[pl.ds(t0, LRU_TROWS), :]
        g = g_ref[0, pl.ds(t0, LRU_TROWS), :]
        gelu = 0.5 * g * (1.0 + jnp.tanh(0.7978845608028654 * (g + 0.044715 * (g * g * g))))
        o_ref[0, pl.ds(t0, LRU_TROWS), :] = (gelu * y).astype(o_ref.dtype)
        return carry
    lax.fori_loop(0, n // LRU_TROWS, out_tile, 0)


def _lru(xb, gb, xb_ctx, conv_w, conv_b, wcat, gbias, lam, batch, n, ctx_len):
    nch = LRU_WIDTH // LRU_CHUNK
    pitch = _scan_pitch(n)
    cpitch = _scan_pitch(ctx_len)
    big = pltpu.VMEM((SUBLANES * pitch, LRU_CHUNK), F32)
    small = pltpu.VMEM((SUBLANES * cpitch, LRU_CHUNK), F32)
    return pl.pallas_call(
        functools.partial(_lru_kernel, n=n, ctx_len=ctx_len),
        out_shape=jax.ShapeDtypeStruct((nch, batch * n, LRU_CHUNK), BF16),
        grid=(batch, nch),
        in_specs=[pl.BlockSpec((1, n, LRU_CHUNK), lambda b, c: (c, b, 0)),
                  pl.BlockSpec((1, n, LRU_CHUNK), lambda b, c: (c, b, 0)),
                  pl.BlockSpec((1, ctx_len, LRU_CHUNK), lambda b, c: (c, b, 0)),
                  pl.BlockSpec((4, LRU_CHUNK), lambda b, c: (0, c)),
                  pl.BlockSpec((1, LRU_CHUNK), lambda b, c: (0, c)),
                  pl.BlockSpec((1, LRU_CHUNK, 4 * LRU_CHUNK), lambda b, c: (c, 0, 0)),
                  pl.BlockSpec((1, 1, 4 * LRU_CHUNK), lambda b, c: (c, 0, 0)),
                  pl.BlockSpec((2, LRU_CHUNK), lambda b, c: (0, c))],
        out_specs=pl.BlockSpec((1, n, LRU_CHUNK), lambda b, c: (c, b, 0)),
        scratch_shapes=[pltpu.VMEM((n + 2 * SUBLANES, LRU_CHUNK), F32),
                        big, big, big, big, big, big, small, small, small, small],
        compiler_params=_cparams(("arbitrary", "arbitrary")),
        name="rglru",
    )(xb, gb, xb_ctx, conv_w, conv_b, wcat, gbias, lam)


def _lru_gate_weights(w_r, b_r, w_i, b_i):
    nch = LRU_WIDTH // LRU_CHUNK
    bpc = LRU_CHUNK // LRU_BLOCK

    def dense(w):
        wc = w.reshape(nch, bpc, LRU_BLOCK, LRU_BLOCK)
        eye = jnp.eye(bpc, dtype=w.dtype)
        return jnp.einsum("cbij,bd->cbidj", wc, eye).reshape(nch, LRU_CHUNK, LRU_CHUNK)

    wcat = jnp.concatenate([dense(w_r[0]), dense(w_i[0]), dense(w_r[1]), dense(w_i[1])], axis=-1)
    chunk = lambda v: v.reshape(nch, 1, LRU_CHUNK)
    gbias = jnp.concatenate([chunk(b_r[0]), chunk(b_i[0]), chunk(b_r[1]), chunk(b_i[1])], axis=-1)
    return (0.5 * wcat).astype(BF16), (0.5 * gbias).astype(F32)


def _route(s, sel, route_ref):
    srow = [s[e:e + 1, :] for e in range(N_EXPERTS)]
    lrow = [sel[e:e + 1, :] for e in range(N_EXPERTS)]
    gscore = []
    for g in range(N_EXPERT_GROUPS):
        a = lrow[g * EXPERTS_PER_GROUP:(g + 1) * EXPERTS_PER_GROUP]
        best = a[0] + a[1]
        for i, j in ((0, 2), (0, 3), (1, 2), (1, 3), (2, 3)):
            best = jnp.maximum(best, a[i] + a[j])
        gscore.append(best)
    bg = jnp.zeros_like(gscore[0], dtype=jnp.int32)
    bv = gscore[0]
    for g in range(1, N_EXPERT_GROUPS):
        upd = gscore[g] > bv
        bg = jnp.where(upd, g, bg)
        bv = jnp.where(upd, gscore[g], bv)

    def pick(rows_):
        out = []
        for j in range(EXPERTS_PER_GROUP):
            v = rows_[j]
            for g in range(1, N_EXPERT_GROUPS):
                v = jnp.where(bg == g, rows_[g * EXPERTS_PER_GROUP + j], v)
            out.append(v)
        return out
    cand = pick(lrow)
    cs = pick(srow)
    i1 = jnp.zeros_like(bg)
    v1 = cand[0]
    w1 = cs[0]
    for j in range(1, EXPERTS_PER_GROUP):
        upd = cand[j] > v1
        i1 = jnp.where(upd, j, i1)
        v1 = jnp.where(upd, cand[j], v1)
        w1 = jnp.where(upd, cs[j], w1)
    i2 = jnp.full_like(bg, -1)
    v2 = jnp.full_like(v1, -jnp.inf)
    w2 = jnp.zeros_like(w1)
    for j in range(EXPERTS_PER_GROUP):
        upd = (i1 != j) & (cand[j] > v2)
        i2 = jnp.where(upd, j, i2)
        v2 = jnp.where(upd, cand[j], v2)
        w2 = jnp.where(upd, cs[j], w2)
    den = w1 + w2
    g1 = w1 / den
    g2 = w2 / den
    for j in range(EXPERTS_PER_GROUP):
        route_ref[j:j + 1, :] = jnp.where(i1 == j, g1, 0.0) + jnp.where(i2 == j, g2, 0.0)
    route_ref[ROUTE_GID_ROW:ROUTE_GID_ROW + 1, :] = bg.astype(F32)
    pad = SUBLANES - ROUTE_GID_ROW - 1
    route_ref[ROUTE_GID_ROW + 1:, :] = jnp.zeros((pad, bg.shape[1]), F32)


def _post_kernel(a_ref, b_ref, w_ref, x_ref, gate_ref, shift_ref, scale_ref, gain_ref, rw_ref, rb_ref,
                 x1_ref, h2_ref, route_ref):
    half = a_ref.shape[1]
    if len(b_ref.shape) == 3:
        b = jnp.concatenate([b_ref[c] for c in range(b_ref.shape[0])], axis=-1)
    else:
        b = b_ref[...]
    mix = (jnp.dot(a_ref[...], w_ref[:half, :], preferred_element_type=F32)
           + jnp.dot(b, w_ref[half:, :], preferred_element_type=F32))
    x1 = x_ref[...] + gate_ref[0] * mix
    x1_ref[...] = x1
    h2 = _norm_modulate(x1, gain_ref[...], shift_ref[0], scale_ref[0])
    h2_ref[...] = h2.astype(h2_ref.dtype)
    logits = lax.dot_general(rw_ref[...], h2, (((1,), (1,)), ((), ())), precision=HIGHEST,
                             preferred_element_type=F32)
    s = _sigmoid(logits)
    _route(s, s + rb_ref[...], route_ref)


def _post_mixer(a, a_col, b, b_col, w, x2d, gate1, shift2, scale2, gain, rw_t, rbias, tokens_per_batch, tm):
    t, d = x2d.shape
    tpb = tokens_per_batch // tm
    half = d // 2
    full = lambda i: (0, 0)
    per_b = lambda i: (i // tpb, 0, 0)
    if b.ndim == 3:
        b_spec = pl.BlockSpec((b.shape[0], tm, b.shape[2]), lambda i: (0, i, 0))
    else:
        b_spec = pl.BlockSpec((tm, half), lambda i: (i, b_col))
    return pl.pallas_call(
        _post_kernel,
        out_shape=[jax.ShapeDtypeStruct((t, d), F32), jax.ShapeDtypeStruct((t, d), BF16),
                   jax.ShapeDtypeStruct((SUBLANES, t), F32)],
        grid=(t // tm,),
        in_specs=[pl.BlockSpec((tm, half), lambda i: (i, a_col)),
                  b_spec,
                  pl.BlockSpec(w.shape, full),
                  pl.BlockSpec((tm, d), lambda i: (i, 0)),
                  pl.BlockSpec((1, 1, d), per_b),
                  pl.BlockSpec((1, 1, d), per_b),
                  pl.BlockSpec((1, 1, d), per_b),
                  pl.BlockSpec((1, d), full),
                  pl.BlockSpec(rw_t.shape, full),
                  pl.BlockSpec(rbias.shape, full)],
        out_specs=[pl.BlockSpec((tm, d), lambda i: (i, 0)),
                   pl.BlockSpec((tm, d), lambda i: (i, 0)),
                   pl.BlockSpec((SUBLANES, tm), lambda i: (0, i))],
        compiler_params=_cparams(("arbitrary",)),
        name="post_mixer",
    )(a, b, w, x2d, gate1, shift2, scale2, gain, rw_t, rbias)


def _moe_layout(t):
    nt = t // MOE_TILE
    grid = -(-(t + N_EXPERT_GROUPS * (ROW_ALIGN - 1) * nt) // MOE_TILE) + N_EXPERT_GROUPS
    return nt, grid


def _moe_tables(gid, t):
    nt, grid = _moe_layout(t)
    ng = N_EXPERT_GROUPS
    per_tile = MOE_TILE // ROW_ALIGN
    onehot = (gid.reshape(nt, MOE_TILE, 1) == jnp.arange(ng, dtype=jnp.int32)).astype(jnp.int32)
    cnt = onehot.sum(axis=1)
    seg = (cnt + ROW_ALIGN - 1) // ROW_ALIGN
    src = jnp.cumsum(seg, axis=1) - seg
    fill = seg.sum(axis=0)
    ntile = (fill + per_tile - 1) // per_tile
    cum = jnp.cumsum(ntile)
    base = (cum - ntile) * per_tile
    dst = jnp.cumsum(seg, axis=0) - seg + base[None, :]
    seg_tab = jnp.concatenate([seg, src, dst], axis=1).reshape(-1).astype(jnp.int32)
    tail = (-fill) % per_tile
    tail_tab = jnp.concatenate([tail, fill + base, cum[-1:]]).astype(jnp.int32)
    i = jnp.arange(grid, dtype=jnp.int32)
    valid = i < cum[-1]
    ie = jnp.minimum(i, cum[-1] - 1)
    g_of = jnp.sum((ie[:, None] >= cum[None, :]).astype(jnp.int32), axis=1)
    return seg_tab, tail_tab, g_of.astype(jnp.int32), valid.astype(jnp.int32)


def _segment_copies(tab_ref, tile, enable, make_copy):
    ng = N_EXPERT_GROUPS
    base = jnp.maximum(tile, 0) * (3 * ng)
    out = []
    for g in range(ng):
        n = tab_ref[base + g]
        src = tab_ref[base + ng + g]
        dst = tab_ref[base + 2 * ng + g]
        for k in range(MOE_SEG_BITS - 1, -1, -1):
            done = (n >> (k + 1)) << (k + 1)
            rows = ROW_ALIGN << k
            s0 = pl.multiple_of((src + done) * ROW_ALIGN, ROW_ALIGN)
            d0 = pl.multiple_of((dst + done) * ROW_ALIGN, ROW_ALIGN)
            out.append((enable & (((n >> k) & 1) == 1), make_copy(s0, d0, rows)))
    return out


def _start_copies(pairs):
    for cond, copies in pairs:
        @pl.when(cond)
        def _():
            for c in copies:
                c.start()


def _wait_copies(pairs):
    for cond, copies in pairs:
        @pl.when(cond)
        def _():
            for c in copies:
                c.wait()


def _split_bf16x3(x):
    hi = x.astype(BF16).astype(F32)
    r1 = x - hi
    mid = r1.astype(BF16).astype(F32)
    lo = (r1 - mid).astype(BF16).astype(F32)
    return hi, mid, lo


def _dispatch_kernel(seg_ref, tail_ref, h_ref, route_ref, tri_ref, slot_ref, hs_ref, gs_ref,
                     cbuf, gbuf, zh, zg, sem, *, nt):
    i = pl.program_id(0)
    tm = h_ref.shape[0]
    ng = N_EXPERT_GROUPS
    cur = i % 2

    def seg_copies(tile, enable, buf):
        def seg_copy(s0, d0, rows):
            return (pltpu.make_async_copy(cbuf.at[buf, pl.ds(s0, rows)], hs_ref.at[pl.ds(d0, rows)], sem.at[buf, 0]),
                    pltpu.make_async_copy(gbuf.at[buf, pl.ds(s0, rows)], gs_ref.at[pl.ds(d0, rows)], sem.at[buf, 1]))
        return _segment_copies(seg_ref, tile, enable, seg_copy)

    _wait_copies(seg_copies(i - 2, i >= 2, cur))

    route = route_ref[...]
    gid = route[ROUTE_GID_ROW:ROUTE_GID_ROW + 1, :]
    grp = lax.broadcasted_iota(jnp.int32, (SUBLANES, tm), 0).astype(F32)
    onehot = jnp.where(grp == gid, 1.0, 0.0)
    rank = jnp.dot(onehot.astype(BF16), tri_ref[...], preferred_element_type=F32)
    slot = jnp.zeros((1, tm), F32)
    for g in range(ng):
        start = (seg_ref[i * 3 * ng + ng + g] * ROW_ALIGN).astype(F32)
        slot = slot + onehot[g:g + 1, :] * (rank[g:g + 1, :] - 1.0 + start)
    slot_ref[...] = jnp.broadcast_to(slot, (SUBLANES, tm))
    perm = jnp.where(lax.broadcasted_iota(jnp.int32, (MOE_CROWS, tm), 0).astype(F32) == slot, 1.0, 0.0)
    perm = perm.astype(BF16)
    cbuf[cur] = jnp.dot(perm, h_ref[...], preferred_element_type=F32).astype(cbuf.dtype)
    parts = jnp.concatenate(list(_split_bf16x3(route)) + [jnp.zeros((LANES - 3 * SUBLANES, tm), F32)], axis=0)
    gbuf[cur] = lax.dot_general(perm, parts.astype(BF16), (((1,), (1,)), ((), ())), preferred_element_type=F32)
    _start_copies(seg_copies(i, i >= 0, cur))

    @pl.when(i == pl.num_programs(0) - 1)
    def _():
        _wait_copies(seg_copies(i - 1, i >= 1, 1 - cur))
        _wait_copies(seg_copies(i, i >= 0, cur))
        zh[...] = jnp.zeros(zh.shape, zh.dtype)
        zg[...] = jnp.zeros(zg.shape, zg.dtype)

        def zero_copy(d0, rows):
            return (pltpu.make_async_copy(zh.at[pl.ds(0, rows)], hs_ref.at[pl.ds(d0, rows)], sem.at[0, 0]),
                    pltpu.make_async_copy(zg.at[pl.ds(0, rows)], gs_ref.at[pl.ds(d0, rows)], sem.at[0, 1]))
        pairs = []
        for g in range(ng):
            n = tail_ref[g]
            dst = tail_ref[ng + g]
            for k in range(MOE_TAIL_BITS - 1, -1, -1):
                done = (n >> (k + 1)) << (k + 1)
                d0 = pl.multiple_of((dst + done) * ROW_ALIGN, ROW_ALIGN)
                pairs.append((((n >> k) & 1) == 1, zero_copy(d0, ROW_ALIGN << k)))
        used = tail_ref[2 * ng]
        total = hs_ref.shape[0] // MOE_TILE
        for j in range(total - nt):
            d0 = pl.multiple_of(jnp.minimum(used + j, total - 1) * MOE_TILE, MOE_TILE)
            pairs.append((used + j < total, zero_copy(d0, MOE_TILE)))
        _start_copies(pairs)
        _wait_copies(pairs)


def _dispatch(seg_tab, tail_tab, h2, route, tri):
    t, d = h2.shape
    nt, grid = _moe_layout(t)
    rows = grid * MOE_TILE
    grid_spec = pltpu.PrefetchScalarGridSpec(
        num_scalar_prefetch=2,
        grid=(nt,),
        in_specs=[pl.BlockSpec((MOE_TILE, d), lambda i, *_: (i, 0)),
                  pl.BlockSpec((SUBLANES, MOE_TILE), lambda i, *_: (0, i)),
                  pl.BlockSpec((MOE_TILE, MOE_TILE), lambda i, *_: (0, 0))],
        out_specs=[pl.BlockSpec((SUBLANES, MOE_TILE), lambda i, *_: (0, i)),
                   pl.BlockSpec(memory_space=pl.ANY),
                   pl.BlockSpec(memory_space=pl.ANY)],
        scratch_shapes=[pltpu.VMEM((2, MOE_CROWS, d), BF16), pltpu.VMEM((2, MOE_CROWS, LANES), F32),
                        pltpu.VMEM((MOE_TILE, d), BF16), pltpu.VMEM((MOE_TILE, LANES), F32),
                        pltpu.SemaphoreType.DMA((2, 2))])
    return pl.pallas_call(
        functools.partial(_dispatch_kernel, nt=nt),
        out_shape=[jax.ShapeDtypeStruct((SUBLANES, t), F32),
                   jax.ShapeDtypeStruct((rows, d), BF16),
                   jax.ShapeDtypeStruct((rows, LANES), F32)],
        grid_spec=grid_spec,
        compiler_params=_cparams(("arbitrary",)),
        name="moe_dispatch",
    )(seg_tab, tail_tab, h2, route, tri)


def _ffn_kernel(grp_ref, valid_ref, h_ref, g_ref, wg_ref, wu_ref, wd_ref, y_ref):
    @pl.when(valid_ref[pl.program_id(0)] == 0)
    def _():
        y_ref[...] = jnp.zeros(y_ref.shape, y_ref.dtype)

    @pl.when(valid_ref[pl.program_id(0)] == 1)
    def _():
        h = h_ref[...]
        gates = g_ref[...]
        y = jnp.zeros(y_ref.shape, F32)
        for j in range(EXPERTS_PER_GROUP):
            a = jnp.dot(h, wg_ref[j], preferred_element_type=F32)
            u = jnp.dot(h, wu_ref[j], preferred_element_type=F32)
            gate = (gates[:, j:j + 1] + gates[:, SUBLANES + j:SUBLANES + j + 1]
                    + gates[:, 2 * SUBLANES + j:2 * SUBLANES + j + 1])
            act = (a * _sigmoid_tanh(a)) * u * gate
            y = y + jnp.dot(act.astype(BF16), wd_ref[j], preferred_element_type=F32)
        y_ref[...] = y.astype(y_ref.dtype)


def _ffn(grp, valid, hs, gs, wg, wu, wd):
    rows, d = hs.shape
    epg = EXPERTS_PER_GROUP
    grid_spec = pltpu.PrefetchScalarGridSpec(
        num_scalar_prefetch=2,
        grid=(rows // MOE_TILE,),
        in_specs=[pl.BlockSpec((MOE_TILE, d), lambda i, grp, valid: (i, 0)),
                  pl.BlockSpec((MOE_TILE, LANES), lambda i, grp, valid: (i, 0)),
                  pl.BlockSpec((epg, d, D_FF_EXPERT), lambda i, grp, valid: (grp[i], 0, 0)),
                  pl.BlockSpec((epg, d, D_FF_EXPERT), lambda i, grp, valid: (grp[i], 0, 0)),
                  pl.BlockSpec((epg, D_FF_EXPERT, d), lambda i, grp, valid: (grp[i], 0, 0))],
        out_specs=pl.BlockSpec((MOE_TILE, d), lambda i, grp, valid: (i, 0)))
    return pl.pallas_call(
        _ffn_kernel,
        out_shape=jax.ShapeDtypeStruct((rows, d), BF16),
        grid_spec=grid_spec,
        compiler_params=_cparams(("arbitrary",)),
        name="moe_ffn",
    )(grp, valid, hs, gs, wg, wu, wd)


def _combine_kernel(seg_ref, x1_ref, slot_ref, gate2_ref, ys_ref, o_ref, ybuf, sem):
    i = pl.program_id(0)
    nt = pl.num_programs(0)
    tm = x1_ref.shape[0]
    cur = i % 2

    def seg_copies(tile, enable, buf):
        def seg_copy(s0, d0, rows):
            return (pltpu.make_async_copy(ys_ref.at[pl.ds(d0, rows)], ybuf.at[buf, pl.ds(s0, rows)], sem.at[buf]),)
        return _segment_copies(seg_ref, tile, enable, seg_copy)

    @pl.when(i == 0)
    def _():
        ybuf[...] = jnp.zeros(ybuf.shape, ybuf.dtype)
        _start_copies(seg_copies(i, i == 0, cur))

    nxt = jnp.minimum(i + 1, nt - 1)
    _start_copies(seg_copies(nxt, i + 1 < nt, 1 - cur))
    _wait_copies(seg_copies(i, i >= 0, cur))
    slot = slot_ref[0:1, :]
    perm = jnp.where(lax.broadcasted_iota(jnp.int32, (MOE_CROWS, tm), 0).astype(F32) == slot, 1.0, 0.0)
    y = lax.dot_general(perm.astype(BF16), ybuf[cur], (((0,), (0,)), ((), ())), preferred_element_type=F32)
    o_ref[...] = x1_ref[...] + gate2_ref[0] * y


def _combine(seg_tab, x1, slot, gate2, ys, tokens_per_batch):
    t, d = x1.shape
    tpb = tokens_per_batch // MOE_TILE
    grid_spec = pltpu.PrefetchScalarGridSpec(
        num_scalar_prefetch=1,
        grid=(t // MOE_TILE,),
        in_specs=[pl.BlockSpec((MOE_TILE, d), lambda i, *_: (i, 0)),
                  pl.BlockSpec((SUBLANES, MOE_TILE), lambda i, *_: (0, i)),
                  pl.BlockSpec((1, 1, d), lambda i, *_: (i // tpb, 0, 0)),
                  pl.BlockSpec(memory_space=pl.ANY)],
        out_specs=pl.BlockSpec((MOE_TILE, d), lambda i, *_: (i, 0)),
        scratch_shapes=[pltpu.VMEM((2, MOE_CROWS, d), BF16), pltpu.SemaphoreType.DMA((2,))])
    return pl.pallas_call(
        _combine_kernel,
        out_shape=jax.ShapeDtypeStruct((t, d), F32),
        grid_spec=grid_spec,
        compiler_params=_cparams(("arbitrary",)),
        name="moe_combine",
    )(seg_tab, x1, slot, gate2, ys)


def _grouped_moe(h2, route, x1, gate2, wg, wu, wd, tri, tokens_per_batch):
    t = h2.shape[0]
    gid = route[ROUTE_GID_ROW].astype(jnp.int32)
    seg_tab, tail_tab, grp, valid = _moe_tables(gid, t)
    slot, hs, gs = _dispatch(seg_tab, tail_tab, h2, route, tri)
    ys = _ffn(grp, valid, hs, gs, wg, wu, wd)
    return _combine(seg_tab, x1, slot, gate2, ys, tokens_per_batch)


def _fnet_chan_kernel(x_ref, shift_ref, scale_ref, gain_ref, cs_ref, y1_ref, y2_ref):
    h = _norm_modulate(x_ref[...], gain_ref[...], shift_ref[0], scale_ref[0]).astype(BF16)
    gw = D_MODEL // FNET_GROUPS
    for g in range(FNET_GROUPS):
        y = jnp.dot(h[:, g * gw:(g + 1) * gw], cs_ref[...], preferred_element_type=F32)
        y1_ref[:, g * gw:(g + 1) * gw] = y[:, :gw].astype(y1_ref.dtype)
        y2_ref[:, g * gw:(g + 1) * gw] = y[:, gw:].astype(y2_ref.dtype)


def _fnet_channel(x2d, shift, scale, gain, cs, tokens_per_batch, tm):
    t, d = x2d.shape
    tpb = tokens_per_batch // tm
    full = lambda i: (0, 0)
    return pl.pallas_call(
        _fnet_chan_kernel,
        out_shape=[jax.ShapeDtypeStruct((t, d), BF16), jax.ShapeDtypeStruct((t, d), BF16)],
        grid=(t // tm,),
        in_specs=[pl.BlockSpec((tm, d), lambda i: (i, 0)),
                  pl.BlockSpec((1, 1, d), lambda i: (i // tpb, 0, 0)),
                  pl.BlockSpec((1, 1, d), lambda i: (i // tpb, 0, 0)),
                  pl.BlockSpec((1, d), full),
                  pl.BlockSpec(cs.shape, full)],
        out_specs=[pl.BlockSpec((tm, d), lambda i: (i, 0)), pl.BlockSpec((tm, d), lambda i: (i, 0))],
        compiler_params=_cparams(("arbitrary",)),
        name="fnet_channel",
    )(x2d, shift, scale, gain, cs)


def _fnet_pos_kernel(c_ref, s_ref, y1_ref, y2_ref, o_ref):
    o = (jnp.dot(c_ref[...], y1_ref[...], preferred_element_type=F32)
         - jnp.dot(s_ref[...], y2_ref[...], preferred_element_type=F32))
    o_ref[...] = o.astype(o_ref.dtype)


def _fnet_position(cn, sn, y1, y2, batch, n, tm):
    t, d = y1.shape
    halves = 2
    dh = d // halves
    mt = n // tm
    return pl.pallas_call(
        _fnet_pos_kernel,
        out_shape=jax.ShapeDtypeStruct((t, d), BF16),
        grid=(batch, halves, mt),
        in_specs=[pl.BlockSpec((tm, n), lambda b, c, m: (m, 0)),
                  pl.BlockSpec((tm, n), lambda b, c, m: (m, 0)),
                  pl.BlockSpec((n, dh), lambda b, c, m: (b, c)),
                  pl.BlockSpec((n, dh), lambda b, c, m: (b, c))],
        out_specs=pl.BlockSpec((tm, dh), lambda b, c, m: (b * mt + m, c)),
        compiler_params=_cparams(("arbitrary", "arbitrary", "arbitrary")),
        name="fnet_position",
    )(cn, sn, y1, y2)


def _dft_tables(n):
    gw = D_MODEL // FNET_GROUPS
    j = np.arange(gw)
    ang = 2.0 * np.pi * ((j[:, None] * j[None, :]) % gw) / gw
    cs = np.concatenate([np.cos(ang), np.sin(ang)], axis=1) / np.sqrt(gw)
    ra = n // GRID_W
    k = np.arange(n)
    a = np.arange(ra)
    b = np.arange(GRID_W)
    ang_a = 2.0 * np.pi * ((a[:, None] * GRID_W * k[None, :]) % n) / n
    ang_b = 2.0 * np.pi * ((b[:, None] * k[None, :]) % n) / n
    scale = 1.0 / np.sqrt(n)
    ca, sa = jnp.asarray(np.cos(ang_a), F32), jnp.asarray(np.sin(ang_a), F32)
    cb, sb = jnp.asarray(np.cos(ang_b) * scale, F32), jnp.asarray(np.sin(ang_b) * scale, F32)
    cn = (ca[:, None, :] * cb[None, :, :] - sa[:, None, :] * sb[None, :, :]).reshape(n, n).astype(BF16)
    sn = (sa[:, None, :] * cb[None, :, :] + ca[:, None, :] * sb[None, :, :]).reshape(n, n).astype(BF16)
    return jnp.asarray(cs, BF16), cn, sn


def kernel(x, c, ctx, c_ctx, ada_w, ada_b, norm_mix, norm_ffn, mix_w_in, mix_w_out, na_q_norm, na_k_norm, na_rpb,
           lru_conv_w, lru_conv_b, lru_gate_r_w, lru_gate_r_b, lru_gate_i_w, lru_gate_i_b, lru_lambda,
           fnet_w_out, router_w, router_bias, moe_w_gate, moe_w_up, moe_w_down):
    batch, n, d = x.shape
    ctx_len = ctx.shape[1]
    depth = ada_w.shape[0]
    rows = n // GRID_W
    assert d == D_MODEL and n % (GRID_W * NA_QROWS) == 0 and rows >= 4 * NA_QROWS
    assert n % MOE_TILE == 0
    t = batch * n
    tm = 512
    tri = jnp.asarray(np.triu(np.ones((MOE_TILE, MOE_TILE))), BF16)

    r_pad = -(-(batch + 1) // SUBLANES) * SUBLANES
    c_rows = jnp.concatenate([c, c_ctx[None, :], jnp.zeros((r_pad - batch - 1, d), c.dtype)], axis=0)
    mod = _modulation(c_rows, ada_w, ada_b)

    def mod_slices(layer):
        m = mod[layer, :batch].reshape(batch, 1, 6, d)
        return [m[:, :, i, :] for i in range(6)]

    rw_t = router_w.T.astype(F32)
    rbias = router_bias.reshape(N_EXPERTS, 1).astype(F32)
    x2d = x.reshape(t, d)
    ctx2d = ctx.reshape(batch * ctx_len, d)

    for layer in range(depth):
        li = layer // 2
        shift1, scale1, gate1, shift2, scale2, gate2 = mod_slices(layer)
        gain_mix = norm_mix[layer].reshape(1, d)
        gain_ffn = norm_ffn[layer].reshape(1, d)
        if layer % 2 == 0:
            w_in = mix_w_in[li].astype(BF16)
            ind = jnp.asarray(np.kron(np.eye(NA_HEADS), np.ones((HEAD_DIM, HEAD_DIM))), BF16)
            qg = (jnp.tile(na_q_norm[li], NA_HEADS) * HEAD_DIM ** -0.5).reshape(1, NA_WIDTH).astype(F32)
            kg = jnp.tile(na_k_norm[li], NA_HEADS).reshape(1, NA_WIDTH).astype(F32)
            q, k, v, xb, gb = _inproj(x2d, shift1, scale1, gain_mix, w_in, ind, qg, kg,
                                      ("q", "k", "v", "x", "g"), n, tm)
            mctx = mod[layer, batch, :2 * d]
            shift_c = jnp.broadcast_to(mctx[:d], (batch, 1, d))
            scale_c = jnp.broadcast_to(mctx[d:], (batch, 1, d))
            k_c, v_c, xb_c = _inproj(ctx2d, shift_c, scale_c, gain_mix, w_in[:, NA_WIDTH:4 * NA_WIDTH], ind, qg, kg,
                                     ("k", "v", "x"), ctx_len, ctx_len)
            bias = _na_bias_tables(na_rpb[li], rows)
            attn = _attention(q, k, v, k_c, v_c, bias, batch, n, ctx_len)
            wcat, gbias = _lru_gate_weights(lru_gate_r_w[li], lru_gate_r_b[li], lru_gate_i_w[li], lru_gate_i_b[li])
            lru = _lru(xb, gb, xb_c, lru_conv_w[li].astype(F32), lru_conv_b[li].reshape(1, LRU_WIDTH).astype(F32),
                       wcat, gbias, lru_lambda[li].astype(F32), batch, n, ctx_len)
            mix_a, col_a, mix_b, col_b, w_out = attn, 0, lru, 0, mix_w_out[li].astype(BF16)
        else:
            cs, cn, sn = _dft_tables(n)
            y1, y2 = _fnet_channel(x2d, shift1, scale1, gain_mix, cs, n, tm)
            f = _fnet_position(cn, sn, y1, y2, batch, n, tm)
            mix_a, col_a, mix_b, col_b, w_out = f, 0, f, 1, fnet_w_out[li].astype(BF16)
        x1, h2, route = _post_mixer(mix_a, col_a, mix_b, col_b, w_out, x2d, gate1, shift2, scale2, gain_ffn,
                                    rw_t, rbias, n, tm)
        x2d = _grouped_moe(h2, route, x1, gate2, moe_w_gate[layer].astype(BF16), moe_w_up[layer].astype(BF16),
                           moe_w_down[layer].astype(BF16), tri, n)
    return x2d.reshape(batch, n, d)
```

```python
import functools

import numpy as np
import jax
import jax.numpy as jnp
from jax import lax
from jax.experimental import pallas as pl
from jax.experimental.pallas import tpu as pltpu

F32 = jnp.float32
BF16 = jnp.bfloat16
HIGHEST = lax.Precision.HIGHEST

D_MODEL = 1024
GRID_W = 64
HEAD_DIM = 64
NA_HEADS = 8
NA_WIDTH = NA_HEADS * HEAD_DIM
NA_WIN_ROWS = 8
NA_WIN_COLS = 16
LRU_WIDTH = 512
LRU_BLOCK = 64
LRU_C = 8.0
FNET_GROUPS = 4
N_EXPERTS = 16
EXPERTS_PER_GROUP = 4
N_EXPERT_GROUPS = 4
D_FF_EXPERT = 512
RMS_EPS = 1e-6
MASK_VALUE = -1e30

V7X_VMEM_LIMIT_BYTES = 56 * 1024 * 1024
LANES = 128
SUBLANES = 8

NA_QROWS = 4
NA_KROWS = NA_QROWS + NA_WIN_ROWS - 1
NA_QBLK = NA_QROWS * GRID_W
NA_KBLK = NA_KROWS * GRID_W

LRU_CHUNK = LANES
LRU_TROWS = 512

ROUTE_GID_ROW = EXPERTS_PER_GROUP
MOE_TILE = 512
ROW_ALIGN = 16
MOE_CROWS = MOE_TILE + N_EXPERT_GROUPS * ROW_ALIGN
MOE_SEG_BITS = (MOE_TILE // ROW_ALIGN).bit_length()
MOE_TAIL_BITS = (MOE_TILE // ROW_ALIGN - 1).bit_length()


def _sigmoid(x):
    return 1.0 / (1.0 + jnp.exp(-x))


def _sigmoid_tanh(x):
    return 0.5 + 0.5 * jnp.tanh(0.5 * x)


def _cparams(sem, vmem=V7X_VMEM_LIMIT_BYTES):
    return pltpu.CompilerParams(dimension_semantics=sem, vmem_limit_bytes=vmem)


def _mod_kernel(c_ref, w_ref, b_ref, o_ref):
    c = c_ref[...]
    s = c * _sigmoid(c)
    o_ref[0] = jnp.dot(s, w_ref[0], precision=HIGHEST, preferred_element_type=F32) + b_ref[0]


def _modulation(c_rows, ada_w, ada_b):
    depth, d, n6 = ada_w.shape
    r = c_rows.shape[0]
    tn = 1536
    return pl.pallas_call(
        _mod_kernel,
        out_shape=jax.ShapeDtypeStruct((depth, r, n6), F32),
        grid=(depth, n6 // tn),
        in_specs=[pl.BlockSpec((r, d), lambda l, j: (0, 0)),
                  pl.BlockSpec((1, d, tn), lambda l, j: (l, 0, j)),
                  pl.BlockSpec((1, 1, tn), lambda l, j: (l, 0, j))],
        out_specs=pl.BlockSpec((1, r, tn), lambda l, j: (l, 0, j)),
        compiler_params=_cparams(("arbitrary", "arbitrary")),
        name="adaln_mod",
    )(c_rows, ada_w, ada_b.reshape(depth, 1, n6))


def _norm_modulate(x, gain, shift, scale):
    ms = jnp.mean(x * x, axis=-1, keepdims=True)
    y = x * lax.rsqrt(ms + RMS_EPS) * gain
    return y * (1.0 + scale) + shift


def _inproj_kernel(x_ref, shift_ref, scale_ref, gain_ref, w_ref, ind_ref, qg_ref, kg_ref, *out_refs, segs):
    h = _norm_modulate(x_ref[...], gain_ref[...], shift_ref[0], scale_ref[0]).astype(BF16)
    for s, (kind, o_ref) in enumerate(zip(segs, out_refs)):
        z = jnp.dot(h, w_ref[:, s * NA_WIDTH:(s + 1) * NA_WIDTH], preferred_element_type=F32)
        if kind in ("q", "k"):
            ms = jnp.dot((z * z).astype(BF16), ind_ref[...], preferred_element_type=F32) * (1.0 / HEAD_DIM)
            g = qg_ref[...] if kind == "q" else kg_ref[...]
            z = z * lax.rsqrt(ms + RMS_EPS) * g
        if kind in ("x", "g"):
            for c in range(LRU_WIDTH // LRU_CHUNK):
                o_ref[c] = z[:, c * LRU_CHUNK:(c + 1) * LRU_CHUNK].astype(o_ref.dtype)
        else:
            o_ref[...] = z.astype(o_ref.dtype)


def _inproj(x2d, shift, scale, gain, w, ind, qg, kg, segs, tokens_per_batch, tm):
    t, d = x2d.shape
    tpb = tokens_per_batch // tm
    dt = {"q": BF16, "k": BF16, "v": BF16, "x": F32, "g": F32}
    full = lambda i: (0, 0)
    nch = LRU_WIDTH // LRU_CHUNK

    def out_shape(kind):
        shape = (nch, t, LRU_CHUNK) if kind in ("x", "g") else (t, NA_WIDTH)
        return jax.ShapeDtypeStruct(shape, dt[kind])

    def out_spec(kind):
        if kind in ("x", "g"):
            return pl.BlockSpec((nch, tm, LRU_CHUNK), lambda i: (0, i, 0))
        return pl.BlockSpec((tm, NA_WIDTH), lambda i: (i, 0))

    return pl.pallas_call(
        functools.partial(_inproj_kernel, segs=segs),
        out_shape=[out_shape(k) for k in segs],
        grid=(t // tm,),
        in_specs=[pl.BlockSpec((tm, d), lambda i: (i, 0)),
                  pl.BlockSpec((1, 1, d), lambda i: (i // tpb, 0, 0)),
                  pl.BlockSpec((1, 1, d), lambda i: (i // tpb, 0, 0)),
                  pl.BlockSpec((1, d), full),
                  pl.BlockSpec(w.shape, full),
                  pl.BlockSpec(ind.shape, full),
                  pl.BlockSpec((1, NA_WIDTH), full),
                  pl.BlockSpec((1, NA_WIDTH), full)],
        out_specs=[out_spec(k) for k in segs],
        compiler_params=_cparams(("arbitrary",)),
        name="inproj_" + "".join(segs),
    )(x2d, shift, scale, gain, w, ind, qg, kg)


def _na_bias_tables(rpb, rows):
    kr = NA_WIN_ROWS
    rb_count = rows // NA_QROWS
    cq = np.arange(GRID_W)
    ck = np.arange(GRID_W)
    col_start = np.clip(cq - NA_WIN_COLS // 2, 0, GRID_W - NA_WIN_COLS)
    valid_c = (ck[None, :] >= col_start[:, None]) & (ck[None, :] < col_start[:, None] + NA_WIN_COLS)
    dc = np.clip(ck[None, :] - cq[:, None], 1 - NA_WIN_COLS, NA_WIN_COLS - 1) + (NA_WIN_COLS - 1)
    n_dr, n_dc = 2 * NA_WIN_ROWS - 1, 2 * NA_WIN_COLS - 1
    sel_c = (dc[:, :, None] == np.arange(n_dc)) & valid_c[:, :, None]
    sel_r, mask = [], []
    for rb in (0, 1, rb_count - 1):
        r = rb * NA_QROWS + np.arange(NA_QROWS)
        ks = int(np.clip(rb * NA_QROWS - kr // 2, 0, rows - NA_KROWS))
        key_r = ks + np.arange(NA_KROWS)
        row_start = np.clip(r - kr // 2, 0, rows - kr)
        valid_r = (key_r[None, :] >= row_start[:, None]) & (key_r[None, :] < row_start[:, None] + kr)
        dr = np.clip(key_r[None, :] - r[:, None] + (NA_WIN_ROWS - 1), 0, n_dr - 1)
        sel_r.append((dr[:, :, None] == np.arange(n_dr)) & valid_r[:, :, None])
        valid = valid_r[:, None, :, None] & valid_c[None, :, None, :]
        mask.append(np.where(valid, 0.0, MASK_VALUE).reshape(NA_QBLK, NA_KBLK))
    sel_r = jnp.asarray(np.stack(sel_r), F32)
    b = jnp.einsum("hrc,tijr,qkc->thiqjk", rpb.astype(F32), sel_r, jnp.asarray(sel_c, F32), precision=HIGHEST)
    return b.reshape(3, NA_HEADS, NA_QBLK, NA_KBLK) + jnp.asarray(np.stack(mask), F32)[:, None]


def _attn_kernel(q_ref, k_ref, v_ref, kc_ref, vc_ref, bias_ref, o_ref, *, rows):
    rb = pl.program_id(1)
    ks = jnp.clip(rb * NA_QROWS - NA_WIN_ROWS // 2, 0, rows - NA_KROWS)
    kstart = pl.multiple_of(ks * GRID_W, GRID_W)
    nt = (((1,), (1,)), ((), ()))
    for h in range(NA_HEADS):
        hs = slice(h * HEAD_DIM, (h + 1) * HEAD_DIM)
        qh = q_ref[:, hs]
        kh = k_ref[pl.ds(kstart, NA_KBLK), hs]
        vh = v_ref[pl.ds(kstart, NA_KBLK), hs]
        s_w = lax.dot_general(qh, kh, nt, preferred_element_type=F32) + bias_ref[0, h]
        s_c = lax.dot_general(qh, kc_ref[:, hs], nt, preferred_element_type=F32)
        m = jnp.maximum(jnp.max(s_w, axis=-1, keepdims=True), jnp.max(s_c, axis=-1, keepdims=True))
        p_w = jnp.exp(s_w - m)
        p_c = jnp.exp(s_c - m)
        l = jnp.sum(p_w, axis=-1, keepdims=True) + jnp.sum(p_c, axis=-1, keepdims=True)
        o = (jnp.dot(p_w.astype(BF16), vh, preferred_element_type=F32)
             + jnp.dot(p_c.astype(BF16), vc_ref[:, hs], preferred_element_type=F32))
        o_ref[:, hs] = (o / l).astype(o_ref.dtype)


def _attention(q, k, v, kc, vc, bias, batch, n, ctx_len):
    rows = n // GRID_W
    rbc = rows // NA_QROWS

    def bias_idx(b, rb):
        return (jnp.where(rb == 0, 0, jnp.where(rb == rbc - 1, 2, 1)), 0, 0, 0)

    return pl.pallas_call(
        functools.partial(_attn_kernel, rows=rows),
        out_shape=jax.ShapeDtypeStruct((batch * n, NA_WIDTH), BF16),
        grid=(batch, rbc),
        in_specs=[pl.BlockSpec((NA_QBLK, NA_WIDTH), lambda b, rb: (b * rbc + rb, 0)),
                  pl.BlockSpec((n, NA_WIDTH), lambda b, rb: (b, 0)),
                  pl.BlockSpec((n, NA_WIDTH), lambda b, rb: (b, 0)),
                  pl.BlockSpec((ctx_len, NA_WIDTH), lambda b, rb: (b, 0)),
                  pl.BlockSpec((ctx_len, NA_WIDTH), lambda b, rb: (b, 0)),
                  pl.BlockSpec((1, NA_HEADS, NA_QBLK, NA_KBLK), bias_idx)],
        out_specs=pl.BlockSpec((NA_QBLK, NA_WIDTH), lambda b, rb: (b * rbc + rb, 0)),
        compiler_params=_cparams(("arbitrary", "arbitrary")),
        name="na_attention",
    )(q, k, v, kc, vc, bias)


def _scan_pitch(n):
    p = -(-n // SUBLANES)
    while p % 8 != 4:
        p += 1
    return p


NEG_LOG2_E = -1.4426950408889634


def _lru_coeff_tile(half_xc, zh, half_bias, k, d):
    c = LRU_CHUNK
    t_r = jnp.tanh(zh[:, (2 * d) * c:(2 * d + 1) * c] + half_bias[:, (2 * d) * c:(2 * d + 1) * c])
    t_i = jnp.tanh(zh[:, (2 * d + 1) * c:(2 * d + 2) * c] + half_bias[:, (2 * d + 1) * c:(2 * d + 2) * c])
    neg_log_a = k[d:d + 1, :] * (1.0 + t_r)
    a = jnp.exp2(neg_log_a * NEG_LOG2_E)
    one_minus_a2 = jnp.tanh(neg_log_a) * (a * a + 1.0)
    root = jnp.where(one_minus_a2 > 0.0, one_minus_a2 * lax.rsqrt(one_minus_a2), 0.0)
    return a, root * (half_xc + half_xc * t_i)


def _conv_tile(xpad, t0, w, b, rows):
    acc = b + w[0:1, :] * xpad[pl.ds(t0 + SUBLANES - 2, rows), :]
    acc = acc + w[1:2, :] * xpad[pl.ds(t0 + SUBLANES - 1, rows), :]
    acc = acc + w[2:3, :] * xpad[pl.ds(t0 + SUBLANES, rows), :]
    return acc + w[3:4, :] * xpad[pl.ds(t0 + SUBLANES + 1, rows), :]


SCAN_UNROLL = 4


def _strided_rows(ref, j, pitch):
    return ref[pl.ds(j, SUBLANES, stride=pitch), :]


def _chunk_totals(af_ref, bf_ref, ab_ref, bb_ref, pitch):
    def body(j, carry):
        pf, hf, pb, hb = carry
        jb = pitch - 1 - j
        af = _strided_rows(af_ref, j, pitch)
        ab = _strided_rows(ab_ref, jb, pitch)
        return (af * pf, af * hf + _strided_rows(bf_ref, j, pitch),
                ab * pb, ab * hb + _strided_rows(bb_ref, jb, pitch))
    one = jnp.ones((SUBLANES, LRU_CHUNK), F32)
    zero = jnp.zeros((SUBLANES, LRU_CHUNK), F32)
    return lax.fori_loop(0, pitch, body, (one, zero, one, zero), unroll=SCAN_UNROLL)


def _chunk_starts(p_end, h_end, h0, reverse):
    row = lax.broadcasted_iota(jnp.int32, (SUBLANES, LRU_CHUNK), 0)
    starts = jnp.zeros((SUBLANES, LRU_CHUNK), F32)
    state = h0
    order = range(SUBLANES - 1, -1, -1) if reverse else range(SUBLANES)
    for s in order:
        starts = jnp.where(row == s, state, starts)
        state = p_end[s:s + 1, :] * state + h_end[s:s + 1, :]
    return starts, state


def _scan_write(af_ref, bf_ref, hf_ref, ab_ref, bb_ref, hb_ref, starts_f, starts_b, pitch):
    def body(j, carry):
        hf, hb = carry
        jb = pitch - 1 - j
        hf = _strided_rows(af_ref, j, pitch) * hf + _strided_rows(bf_ref, j, pitch)
        hb = _strided_rows(ab_ref, jb, pitch) * hb + _strided_rows(bb_ref, jb, pitch)
        hf_ref[pl.ds(j, SUBLANES, stride=pitch), :] = hf
        hb_ref[pl.ds(jb, SUBLANES, stride=pitch), :] = hb
        return hf, hb
    lax.fori_loop(0, pitch, body, (starts_f, starts_b), unroll=SCAN_UNROLL)


def _lru_kernel(x_ref, g_ref, xc_ref, cw_ref, cb_ref, w_ref, gb_ref, lam_ref, o_ref,
                xpad, a0, b0, a1, b1, h0s, h1s, ca0, cb0, ca1, cb1, *, n, ctx_len):
    pitch = _scan_pitch(n)
    cpitch = _scan_pitch(ctx_len)
    cw = cw_ref[...]
    cb = cb_ref[...]
    gbias = gb_ref[0]
    lam = lam_ref[...]
    sp = jnp.maximum(-lam, 0.0) + jnp.log1p(jnp.exp(-jnp.abs(lam)))
    k = (0.5 * LRU_C) * sp
    wcat = w_ref[0]
    zeros8 = jnp.zeros((SUBLANES, LRU_CHUNK), F32)

    def fill_coeffs(src_rows, total, length, trows, a_refs, b_refs):
        for d in range(2):
            a_refs[d][pl.ds(length, total - length), :] = jnp.ones((total - length, LRU_CHUNK), F32)
            b_refs[d][pl.ds(length, total - length), :] = jnp.zeros((total - length, LRU_CHUNK), F32)
        xpad[pl.ds(0, SUBLANES), :] = zeros8
        xpad[pl.ds(SUBLANES + length, SUBLANES), :] = zeros8
        xpad[pl.ds(SUBLANES, length), :] = src_rows

        def tile(t, carry):
            t0 = pl.multiple_of(t * trows, SUBLANES)
            xc = _conv_tile(xpad, t0, cw, cb, trows)
            zh = jnp.dot(xc.astype(BF16), wcat, preferred_element_type=F32)
            half_xc = 0.5 * xc
            for d in range(2):
                a, b = _lru_coeff_tile(half_xc, zh, gbias, k, d)
                a_refs[d][pl.ds(t0, trows), :] = a
                b_refs[d][pl.ds(t0, trows), :] = b
            return carry
        lax.fori_loop(0, length // trows, tile, 0)

    fill_coeffs(xc_ref[0], SUBLANES * cpitch, ctx_len, ctx_len, (ca0, ca1), (cb0, cb1))
    zero_state = jnp.zeros((1, LRU_CHUNK), F32)
    pf, hf, pb, hb = _chunk_totals(ca0, cb0, ca1, cb1, cpitch)
    _, init_f = _chunk_starts(pf, hf, zero_state, reverse=False)
    _, init_b = _chunk_starts(pb, hb, zero_state, reverse=True)

    fill_coeffs(x_ref[0], SUBLANES * pitch, n, LRU_TROWS, (a0, a1), (b0, b1))
    pf, hf, pb, hb = _chunk_totals(a0, b0, a1, b1, pitch)
    starts_f, _ = _chunk_starts(pf, hf, init_f, reverse=False)
    starts_b, _ = _chunk_starts(pb, hb, init_b, reverse=True)
    _scan_write(a0, b0, h0s, a1, b1, h1s, starts_f, starts_b, pitch)

    def out_tile(t, carry):
        t0 = pl.multiple_of(t * LRU_TROWS, SUBLANES)
        y = h0s[pl.ds(t0, LRU_TROWS), :] + h1s[pl.ds(t0, LRU_TROWS), :]
        g = g_ref[0, pl.ds(t0, LRU_TROWS), :]
        gelu = 0.5 * g * (1.0 + jnp.tanh(0.7978845608028654 * (g + 0.044715 * (g * g * g))))
        o_ref[0, pl.ds(t0, LRU_TROWS), :] = (gelu * y).astype(o_ref.dtype)
        return carry
    lax.fori_loop(0, n // LRU_TROWS, out_tile, 0)


def _lru(xb, gb, xb_ctx, conv_w, conv_b, wcat, gbias, lam, batch, n, ctx_len):
    nch = LRU_WIDTH // LRU_CHUNK
    pitch = _scan_pitch(n)
    cpitch = _scan_pitch(ctx_len)
    big = pltpu.VMEM((SUBLANES * pitch, LRU_CHUNK), F32)
    small = pltpu.VMEM((SUBLANES * cpitch, LRU_CHUNK), F32)
    return pl.pallas_call(
        functools.partial(_lru_kernel, n=n, ctx_len=ctx_len),
        out_shape=jax.ShapeDtypeStruct((nch, batch * n, LRU_CHUNK), BF16),
        grid=(batch, nch),
        in_specs=[pl.BlockSpec((1, n, LRU_CHUNK), lambda b, c: (c, b, 0)),
                  pl.BlockSpec((1, n, LRU_CHUNK), lambda b, c: (c, b, 0)),
                  pl.BlockSpec((1, ctx_len, LRU_CHUNK), lambda b, c: (c, b, 0)),
                  pl.BlockSpec((4, LRU_CHUNK), lambda b, c: (0, c)),
                  pl.BlockSpec((1, LRU_CHUNK), lambda b, c: (0, c)),
                  pl.BlockSpec((1, LRU_CHUNK, 4 * LRU_CHUNK), lambda b, c: (c, 0, 0)),
                  pl.BlockSpec((1, 1, 4 * LRU_CHUNK), lambda b, c: (c, 0, 0)),
                  pl.BlockSpec((2, LRU_CHUNK), lambda b, c: (0, c))],
        out_specs=pl.BlockSpec((1, n, LRU_CHUNK), lambda b, c: (c, b, 0)),
        scratch_shapes=[pltpu.VMEM((n + 2 * SUBLANES, LRU_CHUNK), F32),
                        big, big, big, big, big, big, small, small, small, small],
        compiler_params=_cparams(("arbitrary", "arbitrary")),
        name="rglru",
    )(xb, gb, xb_ctx, conv_w, conv_b, wcat, gbias, lam)


def _lru_gate_weights(w_r, b_r, w_i, b_i):
    nch = LRU_WIDTH // LRU_CHUNK
    bpc = LRU_CHUNK // LRU_BLOCK

    def dense(w):
        wc = w.reshape(nch, bpc, LRU_BLOCK, LRU_BLOCK)
        eye = jnp.eye(bpc, dtype=w.dtype)
        return jnp.einsum("cbij,bd->cbidj", wc, eye).reshape(nch, LRU_CHUNK, LRU_CHUNK)

    wcat = jnp.concatenate([dense(w_r[0]), dense(w_i[0]), dense(w_r[1]), dense(w_i[1])], axis=-1)
    chunk = lambda v: v.reshape(nch, 1, LRU_CHUNK)
    gbias = jnp.concatenate([chunk(b_r[0]), chunk(b_i[0]), chunk(b_r[1]), chunk(b_i[1])], axis=-1)
    return (0.5 * wcat).astype(BF16), (0.5 * gbias).astype(F32)


def _route(s, sel, route_ref):
    srow = [s[e:e + 1, :] for e in range(N_EXPERTS)]
    lrow = [sel[e:e + 1, :] for e in range(N_EXPERTS)]
    gscore = []
    for g in range(N_EXPERT_GROUPS):
        a = lrow[g * EXPERTS_PER_GROUP:(g + 1) * EXPERTS_PER_GROUP]
        best = a[0] + a[1]
        for i, j in ((0, 2), (0, 3), (1, 2), (1, 3), (2, 3)):
            best = jnp.maximum(best, a[i] + a[j])
        gscore.append(best)
    bg = jnp.zeros_like(gscore[0], dtype=jnp.int32)
    bv = gscore[0]
    for g in range(1, N_EXPERT_GROUPS):
        upd = gscore[g] > bv
        bg = jnp.where(upd, g, bg)
        bv = jnp.where(upd, gscore[g], bv)

    def pick(rows_):
        out = []
        for j in range(EXPERTS_PER_GROUP):
            v = rows_[j]
            for g in range(1, N_EXPERT_GROUPS):
                v = jnp.where(bg == g, rows_[g * EXPERTS_PER_GROUP + j], v)
            out.append(v)
        return out
    cand = pick(lrow)
    cs = pick(srow)
    i1 = jnp.zeros_like(bg)
    v1 = cand[0]
    w1 = cs[0]
    for j in range(1, EXPERTS_PER_GROUP):
        upd = cand[j] > v1
        i1 = jnp.where(upd, j, i1)
        v1 = jnp.where(upd, cand[j], v1)
        w1 = jnp.where(upd, cs[j], w1)
    i2 = jnp.full_like(bg, -1)
    v2 = jnp.full_like(v1, -jnp.inf)
    w2 = jnp.zeros_like(w1)
    for j in range(EXPERTS_PER_GROUP):
        upd = (i1 != j) & (cand[j] > v2)
        i2 = jnp.where(upd, j, i2)
        v2 = jnp.where(upd, cand[j], v2)
        w2 = jnp.where(upd, cs[j], w2)
    den = w1 + w2
    g1 = w1 / den
    g2 = w2 / den
    for j in range(EXPERTS_PER_GROUP):
        route_ref[j:j + 1, :] = jnp.where(i1 == j, g1, 0.0) + jnp.where(i2 == j, g2, 0.0)
    route_ref[ROUTE_GID_ROW:ROUTE_GID_ROW + 1, :] = bg.astype(F32)
    pad = SUBLANES - ROUTE_GID_ROW - 1
    route_ref[ROUTE_GID_ROW + 1:, :] = jnp.zeros((pad, bg.shape[1]), F32)


POST_SUBTILE = 512


def _post_kernel(*refs, n_parts):
    parts = refs[:n_parts]
    (w_ref, x_ref, gate_ref, shift_ref, scale_ref, gain_ref, rw_ref, rb_ref,
     x1_ref, h2_ref, route_ref) = refs[n_parts:]
    for sub in range(x_ref.shape[0] // POST_SUBTILE):
        rows = pl.ds(sub * POST_SUBTILE, POST_SUBTILE)
        pieces = []
        for p in parts:
            pieces += [p[c, rows, :] for c in range(p.shape[0])] if len(p.shape) == 3 else [p[rows, :]]
        mixed = jnp.concatenate(pieces, axis=-1) if len(pieces) > 1 else pieces[0]
        mix = jnp.dot(mixed, w_ref[...], preferred_element_type=F32)
        x1 = x_ref[rows, :] + gate_ref[0] * mix
        x1_ref[rows, :] = x1
        h2 = _norm_modulate(x1, gain_ref[...], shift_ref[0], scale_ref[0])
        h_hi = h2.astype(BF16)
        h2_ref[rows, :] = h_hi
        h_lo = (h2 - h_hi.astype(F32)).astype(BF16)
        prod = (jnp.dot(h_hi, rw_ref[...], preferred_element_type=F32)
                + jnp.dot(h_lo, rw_ref[...], preferred_element_type=F32))
        logits = prod[:, :LANES] + prod[:, LANES:]
        s = _sigmoid(logits.T[:N_EXPERTS, :])
        _route(s, s + rb_ref[...], route_ref.at[:, rows])


def _post_mixer(parts, w, x2d, gate1, shift2, scale2, gain, rw_cat, rbias, tokens_per_batch, tm):
    t, d = x2d.shape
    tpb = tokens_per_batch // tm
    full = lambda i: (0, 0)
    per_b = lambda i: (i // tpb, 0, 0)

    def part_spec(p):
        if p.ndim == 3:
            return pl.BlockSpec((p.shape[0], tm, p.shape[2]), lambda i: (0, i, 0))
        return pl.BlockSpec((tm, p.shape[1]), lambda i: (i, 0))

    return pl.pallas_call(
        functools.partial(_post_kernel, n_parts=len(parts)),
        out_shape=[jax.ShapeDtypeStruct((t, d), F32), jax.ShapeDtypeStruct((t, d), BF16),
                   jax.ShapeDtypeStruct((SUBLANES, t), F32)],
        grid=(t // tm,),
        in_specs=[part_spec(p) for p in parts] + [
                  pl.BlockSpec(w.shape, full),
                  pl.BlockSpec((tm, d), lambda i: (i, 0)),
                  pl.BlockSpec((1, 1, d), per_b),
                  pl.BlockSpec((1, 1, d), per_b),
                  pl.BlockSpec((1, 1, d), per_b),
                  pl.BlockSpec((1, d), full),
                  pl.BlockSpec(rw_cat.shape, full),
                  pl.BlockSpec(rbias.shape, full)],
        out_specs=[pl.BlockSpec((tm, d), lambda i: (i, 0)),
                   pl.BlockSpec((tm, d), lambda i: (i, 0)),
                   pl.BlockSpec((SUBLANES, tm), lambda i: (0, i))],
        compiler_params=_cparams(("arbitrary",)),
        name="post_mixer",
    )(*parts, w, x2d, gate1, shift2, scale2, gain, rw_cat, rbias)


def _moe_layout(t):
    nt = t // MOE_TILE
    grid = -(-(t + N_EXPERT_GROUPS * (ROW_ALIGN - 1) * nt) // MOE_TILE) + N_EXPERT_GROUPS
    return nt, grid


def _moe_tables(gid, t):
    nt, grid = _moe_layout(t)
    ng = N_EXPERT_GROUPS
    per_tile = MOE_TILE // ROW_ALIGN
    onehot = (gid.reshape(nt, MOE_TILE, 1) == jnp.arange(ng, dtype=jnp.int32)).astype(jnp.int32)
    cnt = onehot.sum(axis=1)
    seg = (cnt + ROW_ALIGN - 1) // ROW_ALIGN
    src = jnp.cumsum(seg, axis=1) - seg
    fill = seg.sum(axis=0)
    ntile = (fill + per_tile - 1) // per_tile
    cum = jnp.cumsum(ntile)
    base = (cum - ntile) * per_tile
    dst = jnp.cumsum(seg, axis=0) - seg + base[None, :]
    seg_tab = jnp.concatenate([seg, src, dst], axis=1).reshape(-1).astype(jnp.int32)
    tail = (-fill) % per_tile
    tail_tab = jnp.concatenate([tail, fill + base, cum[-1:]]).astype(jnp.int32)
    i = jnp.arange(grid, dtype=jnp.int32)
    valid = i < cum[-1]
    ie = jnp.minimum(i, cum[-1] - 1)
    g_of = jnp.sum((ie[:, None] >= cum[None, :]).astype(jnp.int32), axis=1)
    return seg_tab, tail_tab, g_of.astype(jnp.int32), valid.astype(jnp.int32)


def _segment_copies(tab_ref, tile, enable, make_copy):
    ng = N_EXPERT_GROUPS
    base = jnp.maximum(tile, 0) * (3 * ng)
    out = []
    for g in range(ng):
        n = tab_ref[base + g]
        src = tab_ref[base + ng + g]
        dst = tab_ref[base + 2 * ng + g]
        for k in range(MOE_SEG_BITS - 1, -1, -1):
            done = (n >> (k + 1)) << (k + 1)
            rows = ROW_ALIGN << k
            s0 = pl.multiple_of((src + done) * ROW_ALIGN, ROW_ALIGN)
            d0 = pl.multiple_of((dst + done) * ROW_ALIGN, ROW_ALIGN)
            out.append((enable & (((n >> k) & 1) == 1), make_copy(s0, d0, rows)))
    return out


def _start_copies(pairs):
    for cond, copies in pairs:
        @pl.when(cond)
        def _():
            for c in copies:
                c.start()


def _wait_copies(pairs):
    for cond, copies in pairs:
        @pl.when(cond)
        def _():
            for c in copies:
                c.wait()


def _split_bf16x3(x):
    hi = x.astype(BF16).astype(F32)
    r1 = x - hi
    mid = r1.astype(BF16).astype(F32)
    lo = (r1 - mid).astype(BF16).astype(F32)
    return hi, mid, lo


def _dispatch_kernel(seg_ref, tail_ref, h_ref, route_ref, tri_ref, slot_ref, hs_ref, gs_ref,
                     cbuf, gbuf, zh, zg, sem, *, nt):
    i = pl.program_id(0)
    tm = h_ref.shape[0]
    ng = N_EXPERT_GROUPS
    cur = i % 2

    def seg_copies(tile, enable, buf):
        def seg_copy(s0, d0, rows):
            return (pltpu.make_async_copy(cbuf.at[buf, pl.ds(s0, rows)], hs_ref.at[pl.ds(d0, rows)], sem.at[buf, 0]),
                    pltpu.make_async_copy(gbuf.at[buf, pl.ds(s0, rows)], gs_ref.at[pl.ds(d0, rows)], sem.at[buf, 1]))
        return _segment_copies(seg_ref, tile, enable, seg_copy)

    _wait_copies(seg_copies(i - 2, i >= 2, cur))

    route = route_ref[...]
    gid = route[ROUTE_GID_ROW:ROUTE_GID_ROW + 1, :]
    grp = lax.broadcasted_iota(jnp.int32, (SUBLANES, tm), 0).astype(F32)
    onehot = jnp.where(grp == gid, 1.0, 0.0)
    rank = jnp.dot(onehot.astype(BF16), tri_ref[...], preferred_element_type=F32)
    slot = jnp.zeros((1, tm), F32)
    for g in range(ng):
        start = (seg_ref[i * 3 * ng + ng + g] * ROW_ALIGN).astype(F32)
        slot = slot + onehot[g:g + 1, :] * (rank[g:g + 1, :] - 1.0 + start)
    slot_ref[...] = jnp.broadcast_to(slot, (SUBLANES, tm))
    perm = jnp.where(lax.broadcasted_iota(jnp.int32, (MOE_CROWS, tm), 0).astype(F32) == slot, 1.0, 0.0)
    perm = perm.astype(BF16)
    cbuf[cur] = jnp.dot(perm, h_ref[...], preferred_element_type=F32).astype(cbuf.dtype)
    parts = jnp.concatenate(list(_split_bf16x3(route)) + [jnp.zeros((LANES - 3 * SUBLANES, tm), F32)], axis=0)
    gbuf[cur] = lax.dot_general(perm, parts.astype(BF16), (((1,), (1,)), ((), ())), preferred_element_type=F32)
    _start_copies(seg_copies(i, i >= 0, cur))

    @pl.when(i == pl.num_programs(0) - 1)
    def _():
        _wait_copies(seg_copies(i - 1, i >= 1, 1 - cur))
        _wait_copies(seg_copies(i, i >= 0, cur))
        zh[...] = jnp.zeros(zh.shape, zh.dtype)
        zg[...] = jnp.zeros(zg.shape, zg.dtype)

        def zero_copy(d0, rows):
            return (pltpu.make_async_copy(zh.at[pl.ds(0, rows)], hs_ref.at[pl.ds(d0, rows)], sem.at[0, 0]),
                    pltpu.make_async_copy(zg.at[pl.ds(0, rows)], gs_ref.at[pl.ds(d0, rows)], sem.at[0, 1]))
        pairs = []
        for g in range(ng):
            n = tail_ref[g]
            dst = tail_ref[ng + g]
            for k in range(MOE_TAIL_BITS - 1, -1, -1):
                done = (n >> (k + 1)) << (k + 1)
                d0 = pl.multiple_of((dst + done) * ROW_ALIGN, ROW_ALIGN)
                pairs.append((((n >> k) & 1) == 1, zero_copy(d0, ROW_ALIGN << k)))
        used = tail_ref[2 * ng]
        total = hs_ref.shape[0] // MOE_TILE
        for j in range(total - nt):
            d0 = pl.multiple_of(jnp.minimum(used + j, total - 1) * MOE_TILE, MOE_TILE)
            pairs.append((used + j < total, zero_copy(d0, MOE_TILE)))
        _start_copies(pairs)
        _wait_copies(pairs)


def _dispatch(seg_tab, tail_tab, h2, route, tri):
    t, d = h2.shape
    nt, grid = _moe_layout(t)
    rows = grid * MOE_TILE
    grid_spec = pltpu.PrefetchScalarGridSpec(
        num_scalar_prefetch=2,
        grid=(nt,),
        in_specs=[pl.BlockSpec((MOE_TILE, d), lambda i, *_: (i, 0)),
                  pl.BlockSpec((SUBLANES, MOE_TILE), lambda i, *_: (0, i)),
                  pl.BlockSpec((MOE_TILE, MOE_TILE), lambda i, *_: (0, 0))],
        out_specs=[pl.BlockSpec((SUBLANES, MOE_TILE), lambda i, *_: (0, i)),
                   pl.BlockSpec(memory_space=pl.ANY),
                   pl.BlockSpec(memory_space=pl.ANY)],
        scratch_shapes=[pltpu.VMEM((2, MOE_CROWS, d), BF16), pltpu.VMEM((2, MOE_CROWS, LANES), F32),
                        pltpu.VMEM((MOE_TILE, d), BF16), pltpu.VMEM((MOE_TILE, LANES), F32),
                        pltpu.SemaphoreType.DMA((2, 2))])
    return pl.pallas_call(
        functools.partial(_dispatch_kernel, nt=nt),
        out_shape=[jax.ShapeDtypeStruct((SUBLANES, t), F32),
                   jax.ShapeDtypeStruct((rows, d), BF16),
                   jax.ShapeDtypeStruct((rows, LANES), F32)],
        grid_spec=grid_spec,
        compiler_params=_cparams(("arbitrary",)),
        name="moe_dispatch",
    )(seg_tab, tail_tab, h2, route, tri)


def _ffn_kernel(grp_ref, valid_ref, h_ref, g_ref, wg_ref, wu_ref, wd_ref, y_ref):
    @pl.when(valid_ref[pl.program_id(0)] == 0)
    def _():
        y_ref[...] = jnp.zeros(y_ref.shape, y_ref.dtype)

    @pl.when(valid_ref[pl.program_id(0)] == 1)
    def _():
        h = h_ref[...]
        gates = g_ref[...]
        y = jnp.zeros(y_ref.shape, F32)
        for j in range(EXPERTS_PER_GROUP):
            a = jnp.dot(h, wg_ref[j], preferred_element_type=F32)
            u = jnp.dot(h, wu_ref[j], preferred_element_type=F32)
            gate = (gates[:, j:j + 1] + gates[:, SUBLANES + j:SUBLANES + j + 1]
                    + gates[:, 2 * SUBLANES + j:2 * SUBLANES + j + 1])
            act = (a * _sigmoid_tanh(a)) * u * gate
            y = y + jnp.dot(act.astype(BF16), wd_ref[j], preferred_element_type=F32)
        y_ref[...] = y.astype(y_ref.dtype)


def _ffn(grp, valid, hs, gs, wg, wu, wd):
    rows, d = hs.shape
    epg = EXPERTS_PER_GROUP
    grid_spec = pltpu.PrefetchScalarGridSpec(
        num_scalar_prefetch=2,
        grid=(rows // MOE_TILE,),
        in_specs=[pl.BlockSpec((MOE_TILE, d), lambda i, grp, valid: (i, 0)),
                  pl.BlockSpec((MOE_TILE, LANES), lambda i, grp, valid: (i, 0)),
                  pl.BlockSpec((epg, d, D_FF_EXPERT), lambda i, grp, valid: (grp[i], 0, 0)),
                  pl.BlockSpec((epg, d, D_FF_EXPERT), lambda i, grp, valid: (grp[i], 0, 0)),
                  pl.BlockSpec((epg, D_FF_EXPERT, d), lambda i, grp, valid: (grp[i], 0, 0))],
        out_specs=pl.BlockSpec((MOE_TILE, d), lambda i, grp, valid: (i, 0)))
    return pl.pallas_call(
        _ffn_kernel,
        out_shape=jax.ShapeDtypeStruct((rows, d), BF16),
        grid_spec=grid_spec,
        compiler_params=_cparams(("arbitrary",)),
        name="moe_ffn",
    )(grp, valid, hs, gs, wg, wu, wd)


def _combine_kernel(seg_ref, x1_ref, slot_ref, gate2_ref, ys_ref, o_ref, ybuf, sem):
    i = pl.program_id(0)
    nt = pl.num_programs(0)
    tm = x1_ref.shape[0]
    cur = i % 2

    def seg_copies(tile, enable, buf):
        def seg_copy(s0, d0, rows):
            return (pltpu.make_async_copy(ys_ref.at[pl.ds(d0, rows)], ybuf.at[buf, pl.ds(s0, rows)], sem.at[buf]),)
        return _segment_copies(seg_ref, tile, enable, seg_copy)

    @pl.when(i == 0)
    def _():
        ybuf[...] = jnp.zeros(ybuf.shape, ybuf.dtype)
        _start_copies(seg_copies(i, i == 0, cur))

    nxt = jnp.minimum(i + 1, nt - 1)
    _start_copies(seg_copies(nxt, i + 1 < nt, 1 - cur))
    _wait_copies(seg_copies(i, i >= 0, cur))
    slot = slot_ref[0:1, :]
    perm = jnp.where(lax.broadcasted_iota(jnp.int32, (MOE_CROWS, tm), 0).astype(F32) == slot, 1.0, 0.0)
    y = lax.dot_general(perm.astype(BF16), ybuf[cur], (((0,), (0,)), ((), ())), preferred_element_type=F32)
    o_ref[...] = x1_ref[...] + gate2_ref[0] * y


def _combine(seg_tab, x1, slot, gate2, ys, tokens_per_batch):
    t, d = x1.shape
    tpb = tokens_per_batch // MOE_TILE
    grid_spec = pltpu.PrefetchScalarGridSpec(
        num_scalar_prefetch=1,
        grid=(t // MOE_TILE,),
        in_specs=[pl.BlockSpec((MOE_TILE, d), lambda i, *_: (i, 0)),
                  pl.BlockSpec((SUBLANES, MOE_TILE), lambda i, *_: (0, i)),
                  pl.BlockSpec((1, 1, d), lambda i, *_: (i // tpb, 0, 0)),
                  pl.BlockSpec(memory_space=pl.ANY)],
        out_specs=pl.BlockSpec((MOE_TILE, d), lambda i, *_: (i, 0)),
        scratch_shapes=[pltpu.VMEM((2, MOE_CROWS, d), BF16), pltpu.SemaphoreType.DMA((2,))])
    return pl.pallas_call(
        _combine_kernel,
        out_shape=jax.ShapeDtypeStruct((t, d), F32),
        grid_spec=grid_spec,
        compiler_params=_cparams(("arbitrary",)),
        name="moe_combine",
    )(seg_tab, x1, slot, gate2, ys)


def _grouped_moe(h2, route, x1, gate2, wg, wu, wd, tri, tokens_per_batch):
    t = h2.shape[0]
    gid = route[ROUTE_GID_ROW].astype(jnp.int32)
    seg_tab, tail_tab, grp, valid = _moe_tables(gid, t)
    slot, hs, gs = _dispatch(seg_tab, tail_tab, h2, route, tri)
    ys = _ffn(grp, valid, hs, gs, wg, wu, wd)
    return _combine(seg_tab, x1, slot, gate2, ys, tokens_per_batch)


def _fnet_chan_kernel(x_ref, shift_ref, scale_ref, gain_ref, cs_ref, y1_ref, y2_ref):
    h = _norm_modulate(x_ref[...], gain_ref[...], shift_ref[0], scale_ref[0]).astype(BF16)
    gw = D_MODEL // FNET_GROUPS
    for g in range(FNET_GROUPS):
        y = jnp.dot(h[:, g * gw:(g + 1) * gw], cs_ref[...], preferred_element_type=F32)
        y1_ref[g] = y[:, :gw].astype(y1_ref.dtype)
        y2_ref[g] = y[:, gw:].astype(y2_ref.dtype)


def _fnet_channel(x2d, shift, scale, gain, cs, tokens_per_batch, tm):
    t, d = x2d.shape
    tpb = tokens_per_batch // tm
    gw = d // FNET_GROUPS
    full = lambda i: (0, 0)
    out = jax.ShapeDtypeStruct((FNET_GROUPS, t, gw), BF16)
    out_spec = pl.BlockSpec((FNET_GROUPS, tm, gw), lambda i: (0, i, 0))
    return pl.pallas_call(
        _fnet_chan_kernel,
        out_shape=[out, out],
        grid=(t // tm,),
        in_specs=[pl.BlockSpec((tm, d), lambda i: (i, 0)),
                  pl.BlockSpec((1, 1, d), lambda i: (i // tpb, 0, 0)),
                  pl.BlockSpec((1, 1, d), lambda i: (i // tpb, 0, 0)),
                  pl.BlockSpec((1, d), full),
                  pl.BlockSpec(cs.shape, full)],
        out_specs=[out_spec, out_spec],
        compiler_params=_cparams(("arbitrary",)),
        name="fnet_channel",
    )(x2d, shift, scale, gain, cs)


FFT_J = SUBLANES


def _fnet_pos_kernel(y1_ref, y2_ref, lr_ref, ls_ref, cs_ref, sn_ref, o_ref, z1, z2, a_re, a_im, *, n):
    r1 = n // GRID_W
    gw = y1_ref.shape[2]
    nrj = FFT_J * r1
    z1[...] = y1_ref[0].astype(F32)
    z2[...] = y2_ref[0].astype(F32)

    def stage_r(sb, carry):
        s0 = pl.multiple_of(sb * FFT_J, FFT_J)
        rhs1 = jnp.concatenate([z1[pl.ds(GRID_W * r + s0, FFT_J), :] for r in range(r1)], axis=0).astype(BF16)
        rhs2 = jnp.concatenate([z2[pl.ds(GRID_W * r + s0, FFT_J), :] for r in range(r1)], axis=0).astype(BF16)
        p = jnp.dot(lr_ref[...], rhs1, preferred_element_type=F32)
        q = jnp.dot(lr_ref[...], rhs2, preferred_element_type=F32)
        re = p[:nrj] - q[nrj:]
        nim = q[:nrj] + p[nrj:]
        cs = jnp.concatenate([cs_ref[sb]] * (gw // LANES), axis=1)
        sn = jnp.concatenate([sn_ref[sb]] * (gw // LANES), axis=1)
        tre = re * cs - nim * sn
        tnim = re * sn + nim * cs
        for c in range(r1):
            a_re[pl.ds(GRID_W * c + s0, FFT_J), :] = tre[FFT_J * c:FFT_J * (c + 1)]
            a_im[pl.ds(GRID_W * c + s0, FFT_J), :] = tnim[FFT_J * c:FFT_J * (c + 1)]
        return carry
    lax.fori_loop(0, GRID_W // FFT_J, stage_r, 0)

    cblk = FFT_J * GRID_W

    def stage_s(cb, carry):
        c0 = pl.multiple_of(cb * cblk, cblk)
        blk = jnp.concatenate([a_re[pl.ds(c0, cblk), :], a_im[pl.ds(c0, cblk), :]], axis=0).astype(BF16)
        o = jnp.dot(ls_ref[...], blk, preferred_element_type=F32)
        k0 = pl.multiple_of(cb * FFT_J, FFT_J)
        for d in range(GRID_W):
            z1[pl.ds(r1 * d + k0, FFT_J), :] = o[FFT_J * d:FFT_J * (d + 1)]
        return carry
    lax.fori_loop(0, r1 // FFT_J, stage_s, 0)
    o_ref[0] = z1[...].astype(o_ref.dtype)


def _fnet_position(y1, y2, lr, ls, tw_cos, tw_sin, batch, n):
    groups, t, gw = y1.shape
    full2 = lambda b, g: (0, 0)
    full3 = lambda b, g: (0, 0, 0)
    scratch = pltpu.VMEM((n, gw), F32)
    return pl.pallas_call(
        functools.partial(_fnet_pos_kernel, n=n),
        out_shape=jax.ShapeDtypeStruct((groups, t, gw), BF16),
        grid=(batch, groups),
        in_specs=[pl.BlockSpec((1, n, gw), lambda b, g: (g, b, 0)),
                  pl.BlockSpec((1, n, gw), lambda b, g: (g, b, 0)),
                  pl.BlockSpec(lr.shape, full2),
                  pl.BlockSpec(ls.shape, full2),
                  pl.BlockSpec(tw_cos.shape, full3),
                  pl.BlockSpec(tw_sin.shape, full3)],
        out_specs=pl.BlockSpec((1, n, gw), lambda b, g: (g, b, 0)),
        scratch_shapes=[scratch, scratch, scratch, scratch],
        compiler_params=_cparams(("arbitrary", "arbitrary")),
        name="fnet_position",
    )(y1, y2, lr, ls, tw_cos, tw_sin)


def _dft_tables(n):
    assert n % (GRID_W * FFT_J) == 0
    gw = D_MODEL // FNET_GROUPS
    j = np.arange(gw)
    ang = 2.0 * np.pi * ((j[:, None] * j[None, :]) % gw) / gw
    cs = np.concatenate([np.cos(ang), np.sin(ang)], axis=1) / np.sqrt(gw)
    r1 = n // GRID_W
    eye = np.eye(FFT_J)
    scale = float(n) ** -0.25
    a = np.arange(r1)
    ang_r = 2.0 * np.pi * ((a[:, None] * a[None, :]) % r1) / r1
    lr = np.concatenate([np.kron(np.cos(ang_r), eye), np.kron(np.sin(ang_r), eye)], axis=0) * scale
    s = np.arange(GRID_W)
    ang_s = 2.0 * np.pi * ((s[:, None] * s[None, :]) % GRID_W) / GRID_W
    m_c = np.einsum("ds,cC->dcCs", np.cos(ang_s), eye).reshape(FFT_J * GRID_W, FFT_J * GRID_W)
    m_s = np.einsum("ds,cC->dcCs", np.sin(ang_s), eye).reshape(FFT_J * GRID_W, FFT_J * GRID_W)
    ls = np.concatenate([m_c, -m_s], axis=1) * scale
    sb = np.arange(GRID_W // FFT_J)
    s_of = sb[:, None, None] * FFT_J + np.arange(FFT_J)[None, None, :]
    ang_t = 2.0 * np.pi * ((s_of * a[None, :, None]) % n) / n
    ang_t = ang_t.reshape(len(sb), r1 * FFT_J, 1)
    tw_cos = jnp.broadcast_to(jnp.asarray(np.cos(ang_t), F32), (len(sb), r1 * FFT_J, LANES))
    tw_sin = jnp.broadcast_to(jnp.asarray(np.sin(ang_t), F32), (len(sb), r1 * FFT_J, LANES))
    return jnp.asarray(cs, BF16), jnp.asarray(lr, BF16), jnp.asarray(ls, BF16), tw_cos, tw_sin


def kernel(x, c, ctx, c_ctx, ada_w, ada_b, norm_mix, norm_ffn, mix_w_in, mix_w_out, na_q_norm, na_k_norm, na_rpb,
           lru_conv_w, lru_conv_b, lru_gate_r_w, lru_gate_r_b, lru_gate_i_w, lru_gate_i_b, lru_lambda,
           fnet_w_out, router_w, router_bias, moe_w_gate, moe_w_up, moe_w_down):
    batch, n, d = x.shape
    ctx_len = ctx.shape[1]
    depth = ada_w.shape[0]
    rows = n // GRID_W
    assert d == D_MODEL and n % (GRID_W * NA_QROWS) == 0 and rows >= 4 * NA_QROWS
    assert n % MOE_TILE == 0
    t = batch * n
    tm = 512
    tri = jnp.asarray(np.triu(np.ones((MOE_TILE, MOE_TILE))), BF16)

    r_pad = -(-(batch + 1) // SUBLANES) * SUBLANES
    c_rows = jnp.concatenate([c, c_ctx[None, :], jnp.zeros((r_pad - batch - 1, d), c.dtype)], axis=0)
    mod = _modulation(c_rows, ada_w, ada_b)

    def mod_slices(layer):
        m = mod[layer, :batch].reshape(batch, 1, 6, d)
        return [m[:, :, i, :] for i in range(6)]

    rw = jnp.pad(router_w.astype(F32), ((0, 0), (0, LANES - N_EXPERTS)))
    rw_hi = rw.astype(BF16)
    rw_cat = jnp.concatenate([rw_hi, (rw - rw_hi.astype(F32)).astype(BF16)], axis=1)
    rbias = router_bias.reshape(N_EXPERTS, 1).astype(F32)
    x2d = x.reshape(t, d)
    ctx2d = ctx.reshape(batch * ctx_len, d)

    for layer in range(depth):
        li = layer // 2
        shift1, scale1, gate1, shift2, scale2, gate2 = mod_slices(layer)
        gain_mix = norm_mix[layer].reshape(1, d)
        gain_ffn = norm_ffn[layer].reshape(1, d)
        if layer % 2 == 0:
            w_in = mix_w_in[li].astype(BF16)
            ind = jnp.asarray(np.kron(np.eye(NA_HEADS), np.ones((HEAD_DIM, HEAD_DIM))), BF16)
            qg = (jnp.tile(na_q_norm[li], NA_HEADS) * HEAD_DIM ** -0.5).reshape(1, NA_WIDTH).astype(F32)
            kg = jnp.tile(na_k_norm[li], NA_HEADS).reshape(1, NA_WIDTH).astype(F32)
            q, k, v, xb, gb = _inproj(x2d, shift1, scale1, gain_mix, w_in, ind, qg, kg,
                                      ("q", "k", "v", "x", "g"), n, tm)
            mctx = mod[layer, batch, :2 * d]
            shift_c = jnp.broadcast_to(mctx[:d], (batch, 1, d))
            scale_c = jnp.broadcast_to(mctx[d:], (batch, 1, d))
            k_c, v_c, xb_c = _inproj(ctx2d, shift_c, scale_c, gain_mix, w_in[:, NA_WIDTH:4 * NA_WIDTH], ind, qg, kg,
                                     ("k", "v", "x"), ctx_len, ctx_len)
            bias = _na_bias_tables(na_rpb[li], rows)
            attn = _attention(q, k, v, k_c, v_c, bias, batch, n, ctx_len)
            wcat, gbias = _lru_gate_weights(lru_gate_r_w[li], lru_gate_r_b[li], lru_gate_i_w[li], lru_gate_i_b[li])
            lru = _lru(xb, gb, xb_c, lru_conv_w[li].astype(F32), lru_conv_b[li].reshape(1, LRU_WIDTH).astype(F32),
                       wcat, gbias, lru_lambda[li].astype(F32), batch, n, ctx_len)
            parts, w_out = [attn, lru], mix_w_out[li].astype(BF16)
        else:
            cs, lr, ls, tw_cos, tw_sin = _dft_tables(n)
            y1, y2 = _fnet_channel(x2d, shift1, scale1, gain_mix, cs, n, tm)
            parts, w_out = [_fnet_position(y1, y2, lr, ls, tw_cos, tw_sin, batch, n)], fnet_w_out[li].astype(BF16)
        x1, h2, route = _post_mixer(parts, w_out, x2d, gate1, shift2, scale2, gain_ffn, rw_cat, rbias, n,
                                    2 * POST_SUBTILE)
        x2d = _grouped_moe(h2, route, x1, gate2, moe_w_gate[layer].astype(BF16), moe_w_up[layer].astype(BF16),
                           moe_w_down[layer].astype(BF16), tri, n)
    return x2d.reshape(batch, n, d)
```

```python
import functools

import numpy as np
import jax
import jax.numpy as jnp
from jax import lax
from jax.experimental import pallas as pl
from jax.experimental.pallas import tpu as pltpu

F32 = jnp.float32
BF16 = jnp.bfloat16
HIGHEST = lax.Precision.HIGHEST

D_MODEL = 1024
GRID_W = 64
HEAD_DIM = 64
NA_HEADS = 8
NA_WIDTH = NA_HEADS * HEAD_DIM
NA_WIN_ROWS = 8
NA_WIN_COLS = 16
LRU_WIDTH = 512
LRU_BLOCK = 64
LRU_C = 8.0
FNET_GROUPS = 4
N_EXPERTS = 16
EXPERTS_PER_GROUP = 4
N_EXPERT_GROUPS = 4
D_FF_EXPERT = 512
RMS_EPS = 1e-6
MASK_VALUE = -1e30

V7X_VMEM_LIMIT_BYTES = 56 * 1024 * 1024
LANES = 128
SUBLANES = 8

NA_QROWS = 4
NA_KROWS = NA_QROWS + NA_WIN_ROWS - 1
NA_QBLK = NA_QROWS * GRID_W
NA_KBLK = NA_KROWS * GRID_W

LRU_CHUNK = LANES
LRU_TROWS = 512

ROUTE_GID_ROW = EXPERTS_PER_GROUP
MOE_TILE = 512
ROW_ALIGN = 16
MOE_CROWS = MOE_TILE + N_EXPERT_GROUPS * ROW_ALIGN
MOE_SEG_BITS = (MOE_TILE // ROW_ALIGN).bit_length()
MOE_TAIL_BITS = (MOE_TILE // ROW_ALIGN - 1).bit_length()


def _sigmoid(x):
    return 1.0 / (1.0 + jnp.exp(-x))


def _sigmoid_tanh(x):
    return 0.5 + 0.5 * jnp.tanh(0.5 * x)


def _cparams(sem, vmem=V7X_VMEM_LIMIT_BYTES):
    return pltpu.CompilerParams(dimension_semantics=sem, vmem_limit_bytes=vmem)


def _mod_kernel(c_ref, w_ref, b_ref, o_ref):
    c = c_ref[...]
    s = c * _sigmoid(c)
    o_ref[0] = jnp.dot(s, w_ref[0], precision=HIGHEST, preferred_element_type=F32) + b_ref[0]


def _modulation(c_rows, ada_w, ada_b):
    depth, d, n6 = ada_w.shape
    r = c_rows.shape[0]
    tn = 1536
    return pl.pallas_call(
        _mod_kernel,
        out_shape=jax.ShapeDtypeStruct((depth, r, n6), F32),
        grid=(depth, n6 // tn),
        in_specs=[pl.BlockSpec((r, d), lambda l, j: (0, 0)),
                  pl.BlockSpec((1, d, tn), lambda l, j: (l, 0, j)),
                  pl.BlockSpec((1, 1, tn), lambda l, j: (l, 0, j))],
        out_specs=pl.BlockSpec((1, r, tn), lambda l, j: (l, 0, j)),
        compiler_params=_cparams(("arbitrary", "arbitrary")),
        name="adaln_mod",
    )(c_rows, ada_w, ada_b.reshape(depth, 1, n6))


def _norm_modulate(x, gain, shift, scale):
    ms = jnp.mean(x * x, axis=-1, keepdims=True)
    y = x * lax.rsqrt(ms + RMS_EPS) * gain
    return y * (1.0 + scale) + shift


def _inproj_kernel(x_ref, shift_ref, scale_ref, gain_ref, w_ref, ind_ref, qg_ref, kg_ref, *out_refs, segs):
    h = _norm_modulate(x_ref[...], gain_ref[...], shift_ref[0], scale_ref[0]).astype(BF16)
    for s, (kind, o_ref) in enumerate(zip(segs, out_refs)):
        z = jnp.dot(h, w_ref[:, s * NA_WIDTH:(s + 1) * NA_WIDTH], preferred_element_type=F32)
        if kind in ("q", "k"):
            ms = jnp.dot((z * z).astype(BF16), ind_ref[...], preferred_element_type=F32) * (1.0 / HEAD_DIM)
            g = qg_ref[...] if kind == "q" else kg_ref[...]
            z = z * lax.rsqrt(ms + RMS_EPS) * g
        if kind in ("x", "g"):
            for c in range(LRU_WIDTH // LRU_CHUNK):
                o_ref[c] = z[:, c * LRU_CHUNK:(c + 1) * LRU_CHUNK].astype(o_ref.dtype)
        else:
            o_ref[...] = z.astype(o_ref.dtype)


def _inproj(x2d, shift, scale, gain, w, ind, qg, kg, segs, tokens_per_batch, tm):
    t, d = x2d.shape
    tpb = tokens_per_batch // tm
    dt = {"q": BF16, "k": BF16, "v": BF16, "x": F32, "g": F32}
    full = lambda i: (0, 0)
    nch = LRU_WIDTH // LRU_CHUNK

    def out_shape(kind):
        shape = (nch, t, LRU_CHUNK) if kind in ("x", "g") else (t, NA_WIDTH)
        return jax.ShapeDtypeStruct(shape, dt[kind])

    def out_spec(kind):
        if kind in ("x", "g"):
            return pl.BlockSpec((nch, tm, LRU_CHUNK), lambda i: (0, i, 0))
        return pl.BlockSpec((tm, NA_WIDTH), lambda i: (i, 0))

    return pl.pallas_call(
        functools.partial(_inproj_kernel, segs=segs),
        out_shape=[out_shape(k) for k in segs],
        grid=(t // tm,),
        in_specs=[pl.BlockSpec((tm, d), lambda i: (i, 0)),
                  pl.BlockSpec((1, 1, d), lambda i: (i // tpb, 0, 0)),
                  pl.BlockSpec((1, 1, d), lambda i: (i // tpb, 0, 0)),
                  pl.BlockSpec((1, d), full),
                  pl.BlockSpec(w.shape, full),
                  pl.BlockSpec(ind.shape, full),
                  pl.BlockSpec((1, NA_WIDTH), full),
                  pl.BlockSpec((1, NA_WIDTH), full)],
        out_specs=[out_spec(k) for k in segs],
        compiler_params=_cparams(("arbitrary",)),
        name="inproj_" + "".join(segs),
    )(x2d, shift, scale, gain, w, ind, qg, kg)


def _na_bias_tables(rpb, rows):
    kr = NA_WIN_ROWS
    rb_count = rows // NA_QROWS
    cq = np.arange(GRID_W)
    ck = np.arange(GRID_W)
    col_start = np.clip(cq - NA_WIN_COLS // 2, 0, GRID_W - NA_WIN_COLS)
    valid_c = (ck[None, :] >= col_start[:, None]) & (ck[None, :] < col_start[:, None] + NA_WIN_COLS)
    dc = np.clip(ck[None, :] - cq[:, None], 1 - NA_WIN_COLS, NA_WIN_COLS - 1) + (NA_WIN_COLS - 1)
    n_dr, n_dc = 2 * NA_WIN_ROWS - 1, 2 * NA_WIN_COLS - 1
    sel_c = (dc[:, :, None] == np.arange(n_dc)) & valid_c[:, :, None]
    blocks = jnp.einsum("hrc,qkc->hrqk", rpb.astype(F32), jnp.asarray(sel_c, F32), precision=HIGHEST)
    blocks = blocks + jnp.asarray(np.where(valid_c, 0.0, MASK_VALUE), F32)
    blocks = jnp.concatenate([blocks, jnp.full((NA_HEADS, 1, GRID_W, GRID_W), MASK_VALUE, F32)], axis=1)
    which = []
    for rb in (0, 1, rb_count - 1):
        r = rb * NA_QROWS + np.arange(NA_QROWS)
        ks = int(np.clip(rb * NA_QROWS - kr // 2, 0, rows - NA_KROWS))
        key_r = ks + np.arange(NA_KROWS)
        row_start = np.clip(r - kr // 2, 0, rows - kr)
        valid_r = (key_r[None, :] >= row_start[:, None]) & (key_r[None, :] < row_start[:, None] + kr)
        dr = np.clip(key_r[None, :] - r[:, None] + (NA_WIN_ROWS - 1), 0, n_dr - 1)
        which.append(np.where(valid_r, dr, n_dr))
    return _na_bias_assemble(blocks, which)


def _na_bias_kernel(blk_ref, o_ref, *, which):
    for t, table in enumerate(which):
        @pl.when(pl.program_id(0) == t)
        def _():
            for i in range(NA_QROWS):
                row = jnp.concatenate([blk_ref[0, int(table[i, j])] for j in range(NA_KROWS)], axis=1)
                o_ref[0, 0, i * GRID_W:(i + 1) * GRID_W, :] = row


def _na_bias_assemble(blocks, which):
    heads, nblk = blocks.shape[:2]
    return pl.pallas_call(
        functools.partial(_na_bias_kernel, which=which),
        out_shape=jax.ShapeDtypeStruct((len(which), heads, NA_QBLK, NA_KBLK), F32),
        grid=(len(which), heads),
        in_specs=[pl.BlockSpec((1, nblk, GRID_W, GRID_W), lambda t, h: (h, 0, 0, 0))],
        out_specs=pl.BlockSpec((1, 1, NA_QBLK, NA_KBLK), lambda t, h: (t, h, 0, 0)),
        compiler_params=_cparams(("arbitrary", "arbitrary")),
        name="na_bias",
    )(blocks)


def _attn_kernel(q_ref, k_ref, v_ref, kc_ref, vc_ref, bias_ref, o_ref, *, rows):
    rb = pl.program_id(1)
    ks = jnp.clip(rb * NA_QROWS - NA_WIN_ROWS // 2, 0, rows - NA_KROWS)
    kstart = pl.multiple_of(ks * GRID_W, GRID_W)
    last = rows // NA_QROWS - 1
    geom = jnp.where(rb == 0, 0, jnp.where(rb == last, 2, 1))
    nt = (((1,), (1,)), ((), ()))
    for h in range(NA_HEADS):
        hs = slice(h * HEAD_DIM, (h + 1) * HEAD_DIM)
        qh = q_ref[:, hs]
        kh = k_ref[pl.ds(kstart, NA_KBLK), hs]
        vh = v_ref[pl.ds(kstart, NA_KBLK), hs]
        s_w = lax.dot_general(qh, kh, nt, preferred_element_type=F32) + bias_ref[geom, h]
        s_c = lax.dot_general(qh, kc_ref[:, hs], nt, preferred_element_type=F32)
        m = jnp.maximum(jnp.max(s_w, axis=-1, keepdims=True), jnp.max(s_c, axis=-1, keepdims=True))
        p_w = jnp.exp(s_w - m)
        p_c = jnp.exp(s_c - m)
        l = jnp.sum(p_w, axis=-1, keepdims=True) + jnp.sum(p_c, axis=-1, keepdims=True)
        o = (jnp.dot(p_w.astype(BF16), vh, preferred_element_type=F32)
             + jnp.dot(p_c.astype(BF16), vc_ref[:, hs], preferred_element_type=F32))
        o_ref[:, hs] = (o / l).astype(o_ref.dtype)


def _attention(q, k, v, kc, vc, bias, batch, n, ctx_len):
    rows = n // GRID_W
    rbc = rows // NA_QROWS
    return pl.pallas_call(
        functools.partial(_attn_kernel, rows=rows),
        out_shape=jax.ShapeDtypeStruct((batch * n, NA_WIDTH), BF16),
        grid=(batch, rbc),
        in_specs=[pl.BlockSpec((NA_QBLK, NA_WIDTH), lambda b, rb: (b * rbc + rb, 0)),
                  pl.BlockSpec((n, NA_WIDTH), lambda b, rb: (b, 0)),
                  pl.BlockSpec((n, NA_WIDTH), lambda b, rb: (b, 0)),
                  pl.BlockSpec((ctx_len, NA_WIDTH), lambda b, rb: (b, 0)),
                  pl.BlockSpec((ctx_len, NA_WIDTH), lambda b, rb: (b, 0)),
                  pl.BlockSpec(bias.shape, lambda b, rb: (0, 0, 0, 0), pipeline_mode=pl.Buffered(1))],
        out_specs=pl.BlockSpec((NA_QBLK, NA_WIDTH), lambda b, rb: (b * rbc + rb, 0)),
        compiler_params=_cparams(("arbitrary", "arbitrary")),
        name="na_attention",
    )(q, k, v, kc, vc, bias)


def _scan_pitch(n):
    p = -(-n // SUBLANES)
    while p % 8 != 4:
        p += 1
    return p


NEG_LOG2_E = -1.4426950408889634


def _lru_coeff_tile(half_xc, zh, half_bias, k, d):
    c = LRU_CHUNK
    t_r = jnp.tanh(zh[:, (2 * d) * c:(2 * d + 1) * c] + half_bias[:, (2 * d) * c:(2 * d + 1) * c])
    t_i = jnp.tanh(zh[:, (2 * d + 1) * c:(2 * d + 2) * c] + half_bias[:, (2 * d + 1) * c:(2 * d + 2) * c])
    neg_log_a = k[d:d + 1, :] * (1.0 + t_r)
    a = jnp.exp2(neg_log_a * NEG_LOG2_E)
    one_minus_a2 = jnp.tanh(neg_log_a) * (a * a + 1.0)
    root = jnp.where(one_minus_a2 > 0.0, one_minus_a2 * lax.rsqrt(one_minus_a2), 0.0)
    return a, root * (half_xc + half_xc * t_i)


def _conv_tile(xpad, t0, w, b, rows):
    acc = b + w[0:1, :] * xpad[pl.ds(t0 + SUBLANES - 2, rows), :]
    acc = acc + w[1:2, :] * xpad[pl.ds(t0 + SUBLANES - 1, rows), :]
    acc = acc + w[2:3, :] * xpad[pl.ds(t0 + SUBLANES, rows), :]
    return acc + w[3:4, :] * xpad[pl.ds(t0 + SUBLANES + 1, rows), :]


SCAN_UNROLL = 4


def _strided_rows(ref, j, pitch):
    return ref[pl.ds(j, SUBLANES, stride=pitch), :]


def _chunk_totals(af_ref, bf_ref, ab_ref, bb_ref, pitch):
    def body(j, carry):
        pf, hf, pb, hb = carry
        jb = pitch - 1 - j
        af = _strided_rows(af_ref, j, pitch)
        ab = _strided_rows(ab_ref, jb, pitch)
        return (af * pf, af * hf + _strided_rows(bf_ref, j, pitch),
                ab * pb, ab * hb + _strided_rows(bb_ref, jb, pitch))
    one = jnp.ones((SUBLANES, LRU_CHUNK), F32)
    zero = jnp.zeros((SUBLANES, LRU_CHUNK), F32)
    return lax.fori_loop(0, pitch, body, (one, zero, one, zero), unroll=SCAN_UNROLL)


def _chunk_starts(p_end, h_end, h0, reverse):
    row = lax.broadcasted_iota(jnp.int32, (SUBLANES, LRU_CHUNK), 0)
    starts = jnp.zeros((SUBLANES, LRU_CHUNK), F32)
    state = h0
    order = range(SUBLANES - 1, -1, -1) if reverse else range(SUBLANES)
    for s in order:
        starts = jnp.where(row == s, state, starts)
        state = p_end[s:s + 1, :] * state + h_end[s:s + 1, :]
    return starts, state


def _scan_write(af_ref, bf_ref, hf_ref, ab_ref, bb_ref, hb_ref, starts_f, starts_b, pitch):
    def body(j, carry):
        hf, hb = carry
        jb = pitch - 1 - j
        hf = _strided_rows(af_ref, j, pitch) * hf + _strided_rows(bf_ref, j, pitch)
        hb = _strided_rows(ab_ref, jb, pitch) * hb + _strided_rows(bb_ref, jb, pitch)
        hf_ref[pl.ds(j, SUBLANES, stride=pitch), :] = hf
        hb_ref[pl.ds(jb, SUBLANES, stride=pitch), :] = hb
        return hf, hb
    lax.fori_loop(0, pitch, body, (starts_f, starts_b), unroll=SCAN_UNROLL)


def _lru_kernel(x_ref, g_ref, xc_ref, cw_ref, cb_ref, w_ref, gb_ref, lam_ref, o_ref,
                xpad, a0, b0, a1, b1, h0s, h1s, ca0, cb0, ca1, cb1, *, n, ctx_len):
    pitch = _scan_pitch(n)
    cpitch = _scan_pitch(ctx_len)
    cw = cw_ref[...]
    cb = cb_ref[...]
    gbias = gb_ref[0]
    lam = lam_ref[...]
    sp = jnp.maximum(-lam, 0.0) + jnp.log1p(jnp.exp(-jnp.abs(lam)))
    k = (0.5 * LRU_C) * sp
    wcat = w_ref[0]
    zeros8 = jnp.zeros((SUBLANES, LRU_CHUNK), F32)

    def fill_coeffs(src_rows, total, length, trows, a_refs, b_refs):
        for d in range(2):
            a_refs[d][pl.ds(length, total - length), :] = jnp.ones((total - length, LRU_CHUNK), F32)
            b_refs[d][pl.ds(length, total - length), :] = jnp.zeros((total - length, LRU_CHUNK), F32)
        xpad[pl.ds(0, SUBLANES), :] = zeros8
        xpad[pl.ds(SUBLANES + length, SUBLANES), :] = zeros8
        xpad[pl.ds(SUBLANES, length), :] = src_rows

        def tile(t, carry):
            t0 = pl.multiple_of(t * trows, SUBLANES)
            xc = _conv_tile(xpad, t0, cw, cb, trows)
            zh = jnp.dot(xc.astype(BF16), wcat, preferred_element_type=F32)
            half_xc = 0.5 * xc
            for d in range(2):
                a, b = _lru_coeff_tile(half_xc, zh, gbias, k, d)
                a_refs[d][pl.ds(t0, trows), :] = a
                b_refs[d][pl.ds(t0, trows), :] = b
            return carry
        lax.fori_loop(0, length // trows, tile, 0)

    fill_coeffs(xc_ref[0], SUBLANES * cpitch, ctx_len, ctx_len, (ca0, ca1), (cb0, cb1))
    zero_state = jnp.zeros((1, LRU_CHUNK), F32)
    pf, hf, pb, hb = _chunk_totals(ca0, cb0, ca1, cb1, cpitch)
    _, init_f = _chunk_starts(pf, hf, zero_state, reverse=False)
    _, init_b = _chunk_starts(pb, hb, zero_state, reverse=True)

    fill_coeffs(x_ref[0], SUBLANES * pitch, n, LRU_TROWS, (a0, a1), (b0, b1))
    pf, hf, pb, hb = _chunk_totals(a0, b0, a1, b1, pitch)
    starts_f, _ = _chunk_starts(pf, hf, init_f, reverse=False)
    starts_b, _ = _chunk_starts(pb, hb, init_b, reverse=True)
    _scan_write(a0, b0, h0s, a1, b1, h1s, starts_f, starts_b, pitch)

    def out_tile(t, carry):
        t0 = pl.multiple_of(t * LRU_TROWS, SUBLANES)
        y = h0s[pl.ds(t0, LRU_TROWS), :] + h1s[pl.ds(t0, LRU_TROWS), :]
        g = g_ref[0, pl.ds(t0, LRU_TROWS), :]
        gelu = 0.5 * g * (1.0 + jnp.tanh(0.7978845608028654 * (g + 0.044715 * (g * g * g))))
        o_ref[0, pl.ds(t0, LRU_TROWS), :] = (gelu * y).astype(o_ref.dtype)
        return carry
    lax.fori_loop(0, n // LRU_TROWS, out_tile, 0)


def _lru(xb, gb, xb_ctx, conv_w, conv_b, wcat, gbias, lam, batch, n, ctx_len):
    nch = LRU_WIDTH // LRU_CHUNK
    pitch = _scan_pitch(n)
    cpitch = _scan_pitch(ctx_len)
    big = pltpu.VMEM((SUBLANES * pitch, LRU_CHUNK), F32)
    small = pltpu.VMEM((SUBLANES * cpitch, LRU_CHUNK), F32)
    return pl.pallas_call(
        functools.partial(_lru_kernel, n=n, ctx_len=ctx_len),
        out_shape=jax.ShapeDtypeStruct((nch, batch * n, LRU_CHUNK), BF16),
        grid=(batch, nch),
        in_specs=[pl.BlockSpec((1, n, LRU_CHUNK), lambda b, c: (c, b, 0)),
                  pl.BlockSpec((1, n, LRU_CHUNK), lambda b, c: (c, b, 0)),
                  pl.BlockSpec((1, ctx_len, LRU_CHUNK), lambda b, c: (c, b, 0)),
                  pl.BlockSpec((4, LRU_CHUNK), lambda b, c: (0, c)),
                  pl.BlockSpec((1, LRU_CHUNK), lambda b, c: (0, c)),
                  pl.BlockSpec((1, LRU_CHUNK, 4 * LRU_CHUNK), lambda b, c: (c, 0, 0)),
                  pl.BlockSpec((1, 1, 4 * LRU_CHUNK), lambda b, c: (c, 0, 0)),
                  pl.BlockSpec((2, LRU_CHUNK), lambda b, c: (0, c))],
        out_specs=pl.BlockSpec((1, n, LRU_CHUNK), lambda b, c: (c, b, 0)),
        scratch_shapes=[pltpu.VMEM((n + 2 * SUBLANES, LRU_CHUNK), F32),
                        big, big, big, big, big, big, small, small, small, small],
        compiler_params=_cparams(("arbitrary", "arbitrary")),
        name="rglru",
    )(xb, gb, xb_ctx, conv_w, conv_b, wcat, gbias, lam)


def _lru_gate_weights(w_r, b_r, w_i, b_i):
    nch = LRU_WIDTH // LRU_CHUNK
    bpc = LRU_CHUNK // LRU_BLOCK

    def dense(w):
        wc = w.reshape(nch, bpc, LRU_BLOCK, LRU_BLOCK)
        eye = jnp.eye(bpc, dtype=w.dtype)
        return jnp.einsum("cbij,bd->cbidj", wc, eye).reshape(nch, LRU_CHUNK, LRU_CHUNK)

    wcat = jnp.concatenate([dense(w_r[0]), dense(w_i[0]), dense(w_r[1]), dense(w_i[1])], axis=-1)
    chunk = lambda v: v.reshape(nch, 1, LRU_CHUNK)
    gbias = jnp.concatenate([chunk(b_r[0]), chunk(b_i[0]), chunk(b_r[1]), chunk(b_i[1])], axis=-1)
    return (0.5 * wcat).astype(BF16), (0.5 * gbias).astype(F32)


def _route(s, sel, route_ref):
    srow = [s[e:e + 1, :] for e in range(N_EXPERTS)]
    lrow = [sel[e:e + 1, :] for e in range(N_EXPERTS)]
    gscore = []
    for g in range(N_EXPERT_GROUPS):
        a = lrow[g * EXPERTS_PER_GROUP:(g + 1) * EXPERTS_PER_GROUP]
        best = a[0] + a[1]
        for i, j in ((0, 2), (0, 3), (1, 2), (1, 3), (2, 3)):
            best = jnp.maximum(best, a[i] + a[j])
        gscore.append(best)
    bg = jnp.zeros_like(gscore[0], dtype=jnp.int32)
    bv = gscore[0]
    for g in range(1, N_EXPERT_GROUPS):
        upd = gscore[g] > bv
        bg = jnp.where(upd, g, bg)
        bv = jnp.where(upd, gscore[g], bv)

    def pick(rows_):
        out = []
        for j in range(EXPERTS_PER_GROUP):
            v = rows_[j]
            for g in range(1, N_EXPERT_GROUPS):
                v = jnp.where(bg == g, rows_[g * EXPERTS_PER_GROUP + j], v)
            out.append(v)
        return out
    cand = pick(lrow)
    cs = pick(srow)
    i1 = jnp.zeros_like(bg)
    v1 = cand[0]
    w1 = cs[0]
    for j in range(1, EXPERTS_PER_GROUP):
        upd = cand[j] > v1
        i1 = jnp.where(upd, j, i1)
        v1 = jnp.where(upd, cand[j], v1)
        w1 = jnp.where(upd, cs[j], w1)
    i2 = jnp.full_like(bg, -1)
    v2 = jnp.full_like(v1, -jnp.inf)
    w2 = jnp.zeros_like(w1)
    for j in range(EXPERTS_PER_GROUP):
        upd = (i1 != j) & (cand[j] > v2)
        i2 = jnp.where(upd, j, i2)
        v2 = jnp.where(upd, cand[j], v2)
        w2 = jnp.where(upd, cs[j], w2)
    den = w1 + w2
    g1 = w1 / den
    g2 = w2 / den
    for j in range(EXPERTS_PER_GROUP):
        route_ref[j:j + 1, :] = jnp.where(i1 == j, g1, 0.0) + jnp.where(i2 == j, g2, 0.0)
    route_ref[ROUTE_GID_ROW:ROUTE_GID_ROW + 1, :] = bg.astype(F32)
    pad = SUBLANES - ROUTE_GID_ROW - 1
    route_ref[ROUTE_GID_ROW + 1:, :] = jnp.zeros((pad, bg.shape[1]), F32)


POST_SUBTILE = 512


def _post_kernel(*refs, n_parts):
    parts = refs[:n_parts]
    (w_ref, x_ref, gate_ref, shift_ref, scale_ref, gain_ref, rw_ref, rb_ref,
     x1_ref, h2_ref, route_ref) = refs[n_parts:]
    for sub in range(x_ref.shape[0] // POST_SUBTILE):
        rows = pl.ds(sub * POST_SUBTILE, POST_SUBTILE)
        pieces = []
        for p in parts:
            pieces += [p[c, rows, :] for c in range(p.shape[0])] if len(p.shape) == 3 else [p[rows, :]]
        mixed = jnp.concatenate(pieces, axis=-1) if len(pieces) > 1 else pieces[0]
        mix = jnp.dot(mixed, w_ref[...], preferred_element_type=F32)
        x1 = x_ref[rows, :] + gate_ref[0] * mix
        x1_ref[rows, :] = x1
        h2 = _norm_modulate(x1, gain_ref[...], shift_ref[0], scale_ref[0])
        h_hi = h2.astype(BF16)
        h2_ref[rows, :] = h_hi
        h_lo = (h2 - h_hi.astype(F32)).astype(BF16)
        prod = (jnp.dot(h_hi, rw_ref[...], preferred_element_type=F32)
                + jnp.dot(h_lo, rw_ref[...], preferred_element_type=F32))
        logits = prod[:, :LANES] + prod[:, LANES:]
        s = _sigmoid(logits.T[:N_EXPERTS, :])
        _route(s, s + rb_ref[...], route_ref.at[:, rows])


def _post_mixer(parts, w, x2d, gate1, shift2, scale2, gain, rw_cat, rbias, tokens_per_batch, tm):
    t, d = x2d.shape
    tpb = tokens_per_batch // tm
    full = lambda i: (0, 0)
    per_b = lambda i: (i // tpb, 0, 0)

    def part_spec(p):
        if p.ndim == 3:
            return pl.BlockSpec((p.shape[0], tm, p.shape[2]), lambda i: (0, i, 0))
        return pl.BlockSpec((tm, p.shape[1]), lambda i: (i, 0))

    return pl.pallas_call(
        functools.partial(_post_kernel, n_parts=len(parts)),
        out_shape=[jax.ShapeDtypeStruct((t, d), F32), jax.ShapeDtypeStruct((t, d), BF16),
                   jax.ShapeDtypeStruct((SUBLANES, t), F32)],
        grid=(t // tm,),
        in_specs=[part_spec(p) for p in parts] + [
                  pl.BlockSpec(w.shape, full),
                  pl.BlockSpec((tm, d), lambda i: (i, 0)),
                  pl.BlockSpec((1, 1, d), per_b),
                  pl.BlockSpec((1, 1, d), per_b),
                  pl.BlockSpec((1, 1, d), per_b),
                  pl.BlockSpec((1, d), full),
                  pl.BlockSpec(rw_cat.shape, full),
                  pl.BlockSpec(rbias.shape, full)],
        out_specs=[pl.BlockSpec((tm, d), lambda i: (i, 0)),
                   pl.BlockSpec((tm, d), lambda i: (i, 0)),
                   pl.BlockSpec((SUBLANES, tm), lambda i: (0, i))],
        compiler_params=_cparams(("arbitrary",)),
        name="post_mixer",
    )(*parts, w, x2d, gate1, shift2, scale2, gain, rw_cat, rbias)


def _moe_layout(t):
    nt = t // MOE_TILE
    grid = -(-(t + N_EXPERT_GROUPS * (ROW_ALIGN - 1) * nt) // MOE_TILE) + N_EXPERT_GROUPS
    return nt, grid


def _moe_tables(gid, t):
    nt, grid = _moe_layout(t)
    ng = N_EXPERT_GROUPS
    per_tile = MOE_TILE // ROW_ALIGN
    onehot = (gid.reshape(nt, MOE_TILE, 1) == jnp.arange(ng, dtype=jnp.int32)).astype(jnp.int32)
    cnt = onehot.sum(axis=1)
    seg = (cnt + ROW_ALIGN - 1) // ROW_ALIGN
    src = jnp.cumsum(seg, axis=1) - seg
    fill = seg.sum(axis=0)
    ntile = (fill + per_tile - 1) // per_tile
    cum = jnp.cumsum(ntile)
    base = (cum - ntile) * per_tile
    dst = jnp.cumsum(seg, axis=0) - seg + base[None, :]
    seg_tab = jnp.concatenate([seg, src, dst], axis=1).reshape(-1).astype(jnp.int32)
    tail = (-fill) % per_tile
    tail_tab = jnp.concatenate([tail, fill + base, cum[-1:]]).astype(jnp.int32)
    i = jnp.arange(grid, dtype=jnp.int32)
    valid = i < cum[-1]
    ie = jnp.minimum(i, cum[-1] - 1)
    g_of = jnp.sum((ie[:, None] >= cum[None, :]).astype(jnp.int32), axis=1)
    return seg_tab, tail_tab, g_of.astype(jnp.int32), valid.astype(jnp.int32)


def _segment_copies(tab_ref, tile, enable, make_copy):
    ng = N_EXPERT_GROUPS
    base = jnp.maximum(tile, 0) * (3 * ng)
    out = []
    for g in range(ng):
        n = tab_ref[base + g]
        src = tab_ref[base + ng + g]
        dst = tab_ref[base + 2 * ng + g]
        for k in range(MOE_SEG_BITS - 1, -1, -1):
            done = (n >> (k + 1)) << (k + 1)
            rows = ROW_ALIGN << k
            s0 = pl.multiple_of((src + done) * ROW_ALIGN, ROW_ALIGN)
            d0 = pl.multiple_of((dst + done) * ROW_ALIGN, ROW_ALIGN)
            out.append((enable & (((n >> k) & 1) == 1), make_copy(s0, d0, rows)))
    return out


def _start_copies(pairs):
    for cond, copies in pairs:
        @pl.when(cond)
        def _():
            for c in copies:
                c.start()


def _wait_copies(pairs):
    for cond, copies in pairs:
        @pl.when(cond)
        def _():
            for c in copies:
                c.wait()


def _split_bf16x3(x):
    hi = x.astype(BF16).astype(F32)
    r1 = x - hi
    mid = r1.astype(BF16).astype(F32)
    lo = (r1 - mid).astype(BF16).astype(F32)
    return hi, mid, lo


def _dispatch_kernel(seg_ref, tail_ref, h_ref, route_ref, tri_ref, slot_ref, hs_ref, gs_ref,
                     cbuf, gbuf, zh, zg, sem, *, nt):
    i = pl.program_id(0)
    tm = h_ref.shape[0]
    ng = N_EXPERT_GROUPS
    cur = i % 2

    def seg_copies(tile, enable, buf):
        def seg_copy(s0, d0, rows):
            return (pltpu.make_async_copy(cbuf.at[buf, pl.ds(s0, rows)], hs_ref.at[pl.ds(d0, rows)], sem.at[buf, 0]),
                    pltpu.make_async_copy(gbuf.at[buf, pl.ds(s0, rows)], gs_ref.at[pl.ds(d0, rows)], sem.at[buf, 1]))
        return _segment_copies(seg_ref, tile, enable, seg_copy)

    _wait_copies(seg_copies(i - 2, i >= 2, cur))

    route = route_ref[...]
    gid = route[ROUTE_GID_ROW:ROUTE_GID_ROW + 1, :]
    grp = lax.broadcasted_iota(jnp.int32, (SUBLANES, tm), 0).astype(F32)
    onehot = jnp.where(grp == gid, 1.0, 0.0)
    rank = jnp.dot(onehot.astype(BF16), tri_ref[...], preferred_element_type=F32)
    slot = jnp.zeros((1, tm), F32)
    for g in range(ng):
        start = (seg_ref[i * 3 * ng + ng + g] * ROW_ALIGN).astype(F32)
        slot = slot + onehot[g:g + 1, :] * (rank[g:g + 1, :] - 1.0 + start)
    slot_ref[...] = jnp.broadcast_to(slot, (SUBLANES, tm))
    perm = jnp.where(lax.broadcasted_iota(jnp.int32, (MOE_CROWS, tm), 0).astype(F32) == slot, 1.0, 0.0)
    perm = perm.astype(BF16)
    cbuf[cur] = jnp.dot(perm, h_ref[...], preferred_element_type=F32).astype(cbuf.dtype)
    parts = jnp.concatenate(list(_split_bf16x3(route)) + [jnp.zeros((LANES - 3 * SUBLANES, tm), F32)], axis=0)
    gbuf[cur] = lax.dot_general(perm, parts.astype(BF16), (((1,), (1,)), ((), ())), preferred_element_type=F32)
    _start_copies(seg_copies(i, i >= 0, cur))

    @pl.when(i == pl.num_programs(0) - 1)
    def _():
        _wait_copies(seg_copies(i - 1, i >= 1, 1 - cur))
        _wait_copies(seg_copies(i, i >= 0, cur))
        zh[...] = jnp.zeros(zh.shape, zh.dtype)
        zg[...] = jnp.zeros(zg.shape, zg.dtype)

        def zero_copy(d0, rows):
            return (pltpu.make_async_copy(zh.at[pl.ds(0, rows)], hs_ref.at[pl.ds(d0, rows)], sem.at[0, 0]),
                    pltpu.make_async_copy(zg.at[pl.ds(0, rows)], gs_ref.at[pl.ds(d0, rows)], sem.at[0, 1]))
        pairs = []
        for g in range(ng):
            n = tail_ref[g]
            dst = tail_ref[ng + g]
            for k in range(MOE_TAIL_BITS - 1, -1, -1):
                done = (n >> (k + 1)) << (k + 1)
                d0 = pl.multiple_of((dst + done) * ROW_ALIGN, ROW_ALIGN)
                pairs.append((((n >> k) & 1) == 1, zero_copy(d0, ROW_ALIGN << k)))
        used = tail_ref[2 * ng]
        total = hs_ref.shape[0] // MOE_TILE
        for j in range(total - nt):
            d0 = pl.multiple_of(jnp.minimum(used + j, total - 1) * MOE_TILE, MOE_TILE)
            pairs.append((used + j < total, zero_copy(d0, MOE_TILE)))
        _start_copies(pairs)
        _wait_copies(pairs)


def _dispatch(seg_tab, tail_tab, h2, route, tri):
    t, d = h2.shape
    nt, grid = _moe_layout(t)
    rows = grid * MOE_TILE
    grid_spec = pltpu.PrefetchScalarGridSpec(
        num_scalar_prefetch=2,
        grid=(nt,),
        in_specs=[pl.BlockSpec((MOE_TILE, d), lambda i, *_: (i, 0)),
                  pl.BlockSpec((SUBLANES, MOE_TILE), lambda i, *_: (0, i)),
                  pl.BlockSpec((MOE_TILE, MOE_TILE), lambda i, *_: (0, 0))],
        out_specs=[pl.BlockSpec((SUBLANES, MOE_TILE), lambda i, *_: (0, i)),
                   pl.BlockSpec(memory_space=pl.ANY),
                   pl.BlockSpec(memory_space=pl.ANY)],
        scratch_shapes=[pltpu.VMEM((2, MOE_CROWS, d), BF16), pltpu.VMEM((2, MOE_CROWS, LANES), F32),
                        pltpu.VMEM((MOE_TILE, d), BF16), pltpu.VMEM((MOE_TILE, LANES), F32),
                        pltpu.SemaphoreType.DMA((2, 2))])
    return pl.pallas_call(
        functools.partial(_dispatch_kernel, nt=nt),
        out_shape=[jax.ShapeDtypeStruct((SUBLANES, t), F32),
                   jax.ShapeDtypeStruct((rows, d), BF16),
                   jax.ShapeDtypeStruct((rows, LANES), F32)],
        grid_spec=grid_spec,
        compiler_params=_cparams(("arbitrary",)),
        name="moe_dispatch",
    )(seg_tab, tail_tab, h2, route, tri)


def _ffn_kernel(grp_ref, valid_ref, h_ref, g_ref, wg32_ref, wu32_ref, wd32_ref, y_ref, wg_ref, wu_ref, wd_ref):
    i = pl.program_id(0)

    @pl.when((i == 0) | (grp_ref[i] != grp_ref[jnp.maximum(i - 1, 0)]))
    def _():
        for j in range(EXPERTS_PER_GROUP):
            wg_ref[j] = wg32_ref[j].astype(BF16)
            wu_ref[j] = wu32_ref[j].astype(BF16)
            wd_ref[j] = wd32_ref[j].astype(BF16)

    @pl.when(valid_ref[i] == 0)
    def _():
        y_ref[...] = jnp.zeros(y_ref.shape, y_ref.dtype)

    @pl.when(valid_ref[i] == 1)
    def _():
        h = h_ref[...]
        gates = g_ref[...]
        y = jnp.zeros(y_ref.shape, F32)
        for j in range(EXPERTS_PER_GROUP):
            a = jnp.dot(h, wg_ref[j], preferred_element_type=F32)
            u = jnp.dot(h, wu_ref[j], preferred_element_type=F32)
            gate = (gates[:, j:j + 1] + gates[:, SUBLANES + j:SUBLANES + j + 1]
                    + gates[:, 2 * SUBLANES + j:2 * SUBLANES + j + 1])
            act = (a * _sigmoid_tanh(a)) * u * gate
            y = y + jnp.dot(act.astype(BF16), wd_ref[j], preferred_element_type=F32)
        y_ref[...] = y.astype(y_ref.dtype)


def _ffn(grp, valid, hs, gs, wg, wu, wd, layer):
    rows, d = hs.shape
    epg = EXPERTS_PER_GROUP
    once = pl.Buffered(1)
    w_idx = lambda i, grp, valid: (layer, grp[i], 0, 0)
    grid_spec = pltpu.PrefetchScalarGridSpec(
        num_scalar_prefetch=2,
        grid=(rows // MOE_TILE,),
        in_specs=[pl.BlockSpec((MOE_TILE, d), lambda i, grp, valid: (i, 0)),
                  pl.BlockSpec((MOE_TILE, LANES), lambda i, grp, valid: (i, 0)),
                  pl.BlockSpec((None, epg, d, D_FF_EXPERT), w_idx, pipeline_mode=once),
                  pl.BlockSpec((None, epg, d, D_FF_EXPERT), w_idx, pipeline_mode=once),
                  pl.BlockSpec((None, epg, D_FF_EXPERT, d), w_idx, pipeline_mode=once)],
        out_specs=pl.BlockSpec((MOE_TILE, d), lambda i, grp, valid: (i, 0)),
        scratch_shapes=[pltpu.VMEM((epg, d, D_FF_EXPERT), BF16), pltpu.VMEM((epg, d, D_FF_EXPERT), BF16),
                        pltpu.VMEM((epg, D_FF_EXPERT, d), BF16)])
    return pl.pallas_call(
        _ffn_kernel,
        out_shape=jax.ShapeDtypeStruct((rows, d), BF16),
        grid_spec=grid_spec,
        compiler_params=_cparams(("arbitrary",)),
        name="moe_ffn",
    )(grp, valid, hs, gs, wg, wu, wd)


def _combine_kernel(seg_ref, x1_ref, slot_ref, gate2_ref, ys_ref, o_ref, ybuf, sem):
    i = pl.program_id(0)
    nt = pl.num_programs(0)
    tm = x1_ref.shape[0]
    cur = i % 2

    def seg_copies(tile, enable, buf):
        def seg_copy(s0, d0, rows):
            return (pltpu.make_async_copy(ys_ref.at[pl.ds(d0, rows)], ybuf.at[buf, pl.ds(s0, rows)], sem.at[buf]),)
        return _segment_copies(seg_ref, tile, enable, seg_copy)

    @pl.when(i == 0)
    def _():
        ybuf[...] = jnp.zeros(ybuf.shape, ybuf.dtype)
        _start_copies(seg_copies(i, i == 0, cur))

    nxt = jnp.minimum(i + 1, nt - 1)
    _start_copies(seg_copies(nxt, i + 1 < nt, 1 - cur))
    _wait_copies(seg_copies(i, i >= 0, cur))
    slot = slot_ref[0:1, :]
    perm = jnp.where(lax.broadcasted_iota(jnp.int32, (MOE_CROWS, tm), 0).astype(F32) == slot, 1.0, 0.0)
    y = lax.dot_general(perm.astype(BF16), ybuf[cur], (((0,), (0,)), ((), ())), preferred_element_type=F32)
    o_ref[...] = x1_ref[...] + gate2_ref[0] * y


def _combine(seg_tab, x1, slot, gate2, ys, tokens_per_batch):
    t, d = x1.shape
    tpb = tokens_per_batch // MOE_TILE
    grid_spec = pltpu.PrefetchScalarGridSpec(
        num_scalar_prefetch=1,
        grid=(t // MOE_TILE,),
        in_specs=[pl.BlockSpec((MOE_TILE, d), lambda i, *_: (i, 0)),
                  pl.BlockSpec((SUBLANES, MOE_TILE), lambda i, *_: (0, i)),
                  pl.BlockSpec((1, 1, d), lambda i, *_: (i // tpb, 0, 0)),
                  pl.BlockSpec(memory_space=pl.ANY)],
        out_specs=pl.BlockSpec((MOE_TILE, d), lambda i, *_: (i, 0)),
        scratch_shapes=[pltpu.VMEM((2, MOE_CROWS, d), BF16), pltpu.SemaphoreType.DMA((2,))])
    return pl.pallas_call(
        _combine_kernel,
        out_shape=jax.ShapeDtypeStruct((t, d), F32),
        grid_spec=grid_spec,
        compiler_params=_cparams(("arbitrary",)),
        name="moe_combine",
    )(seg_tab, x1, slot, gate2, ys)


def _grouped_moe(h2, route, x1, gate2, wg, wu, wd, layer, tri, tokens_per_batch):
    t = h2.shape[0]
    gid = route[ROUTE_GID_ROW].astype(jnp.int32)
    seg_tab, tail_tab, grp, valid = _moe_tables(gid, t)
    slot, hs, gs = _dispatch(seg_tab, tail_tab, h2, route, tri)
    ys = _ffn(grp, valid, hs, gs, wg, wu, wd, layer)
    return _combine(seg_tab, x1, slot, gate2, ys, tokens_per_batch)


def _fnet_chan_kernel(x_ref, shift_ref, scale_ref, gain_ref, cs_ref, y1_ref, y2_ref):
    h = _norm_modulate(x_ref[...], gain_ref[...], shift_ref[0], scale_ref[0]).astype(BF16)
    gw = D_MODEL // FNET_GROUPS
    for g in range(FNET_GROUPS):
        y = jnp.dot(h[:, g * gw:(g + 1) * gw], cs_ref[...], preferred_element_type=F32)
        y1_ref[g] = y[:, :gw].astype(y1_ref.dtype)
        y2_ref[g] = y[:, gw:].astype(y2_ref.dtype)


def _fnet_channel(x2d, shift, scale, gain, cs, tokens_per_batch, tm):
    t, d = x2d.shape
    tpb = tokens_per_batch // tm
    gw = d // FNET_GROUPS
    full = lambda i: (0, 0)
    out = jax.ShapeDtypeStruct((FNET_GROUPS, t, gw), BF16)
    out_spec = pl.BlockSpec((FNET_GROUPS, tm, gw), lambda i: (0, i, 0))
    return pl.pallas_call(
        _fnet_chan_kernel,
        out_shape=[out, out],
        grid=(t // tm,),
        in_specs=[pl.BlockSpec((tm, d), lambda i: (i, 0)),
                  pl.BlockSpec((1, 1, d), lambda i: (i // tpb, 0, 0)),
                  pl.BlockSpec((1, 1, d), lambda i: (i // tpb, 0, 0)),
                  pl.BlockSpec((1, d), full),
                  pl.BlockSpec(cs.shape, full)],
        out_specs=[out_spec, out_spec],
        compiler_params=_cparams(("arbitrary",)),
        name="fnet_channel",
    )(x2d, shift, scale, gain, cs)


FFT_J = SUBLANES


def _fnet_pos_kernel(y1_ref, y2_ref, lr_ref, ls_ref, cs_ref, sn_ref, o_ref, z1, z2, a_re, a_im, *, n):
    r1 = n // GRID_W
    gw = y1_ref.shape[2]
    nrj = FFT_J * r1
    z1[...] = y1_ref[0].astype(F32)
    z2[...] = y2_ref[0].astype(F32)

    def stage_r(sb, carry):
        s0 = pl.multiple_of(sb * FFT_J, FFT_J)
        rhs1 = jnp.concatenate([z1[pl.ds(GRID_W * r + s0, FFT_J), :] for r in range(r1)], axis=0).astype(BF16)
        rhs2 = jnp.concatenate([z2[pl.ds(GRID_W * r + s0, FFT_J), :] for r in range(r1)], axis=0).astype(BF16)
        p = jnp.dot(lr_ref[...], rhs1, preferred_element_type=F32)
        q = jnp.dot(lr_ref[...], rhs2, preferred_element_type=F32)
        re = p[:nrj] - q[nrj:]
        nim = q[:nrj] + p[nrj:]
        cs = jnp.concatenate([cs_ref[sb]] * (gw // LANES), axis=1)
        sn = jnp.concatenate([sn_ref[sb]] * (gw // LANES), axis=1)
        tre = re * cs - nim * sn
        tnim = re * sn + nim * cs
        for c in range(r1):
            a_re[pl.ds(GRID_W * c + s0, FFT_J), :] = tre[FFT_J * c:FFT_J * (c + 1)]
            a_im[pl.ds(GRID_W * c + s0, FFT_J), :] = tnim[FFT_J * c:FFT_J * (c + 1)]
        return carry
    lax.fori_loop(0, GRID_W // FFT_J, stage_r, 0)

    cblk = FFT_J * GRID_W

    def stage_s(cb, carry):
        c0 = pl.multiple_of(cb * cblk, cblk)
        blk = jnp.concatenate([a_re[pl.ds(c0, cblk), :], a_im[pl.ds(c0, cblk), :]], axis=0).astype(BF16)
        o = jnp.dot(ls_ref[...], blk, preferred_element_type=F32)
        k0 = pl.multiple_of(cb * FFT_J, FFT_J)
        for d in range(GRID_W):
            z1[pl.ds(r1 * d + k0, FFT_J), :] = o[FFT_J * d:FFT_J * (d + 1)]
        return carry
    lax.fori_loop(0, r1 // FFT_J, stage_s, 0)
    o_ref[0] = z1[...].astype(o_ref.dtype)


def _fnet_position(y1, y2, lr, ls, tw_cos, tw_sin, batch, n):
    groups, t, gw = y1.shape
    full2 = lambda b, g: (0, 0)
    full3 = lambda b, g: (0, 0, 0)
    scratch = pltpu.VMEM((n, gw), F32)
    return pl.pallas_call(
        functools.partial(_fnet_pos_kernel, n=n),
        out_shape=jax.ShapeDtypeStruct((groups, t, gw), BF16),
        grid=(batch, groups),
        in_specs=[pl.BlockSpec((1, n, gw), lambda b, g: (g, b, 0)),
                  pl.BlockSpec((1, n, gw), lambda b, g: (g, b, 0)),
                  pl.BlockSpec(lr.shape, full2),
                  pl.BlockSpec(ls.shape, full2),
                  pl.BlockSpec(tw_cos.shape, full3),
                  pl.BlockSpec(tw_sin.shape, full3)],
        out_specs=pl.BlockSpec((1, n, gw), lambda b, g: (g, b, 0)),
        scratch_shapes=[scratch, scratch, scratch, scratch],
        compiler_params=_cparams(("arbitrary", "arbitrary")),
        name="fnet_position",
    )(y1, y2, lr, ls, tw_cos, tw_sin)


def _dft_tables(n):
    assert n % (GRID_W * FFT_J) == 0
    gw = D_MODEL // FNET_GROUPS
    j = np.arange(gw)
    ang = 2.0 * np.pi * ((j[:, None] * j[None, :]) % gw) / gw
    cs = np.concatenate([np.cos(ang), np.sin(ang)], axis=1) / np.sqrt(gw)
    r1 = n // GRID_W
    eye = np.eye(FFT_J)
    scale = float(n) ** -0.25
    a = np.arange(r1)
    ang_r = 2.0 * np.pi * ((a[:, None] * a[None, :]) % r1) / r1
    lr = np.concatenate([np.kron(np.cos(ang_r), eye), np.kron(np.sin(ang_r), eye)], axis=0) * scale
    s = np.arange(GRID_W)
    ang_s = 2.0 * np.pi * ((s[:, None] * s[None, :]) % GRID_W) / GRID_W
    m_c = np.einsum("ds,cC->dcCs", np.cos(ang_s), eye).reshape(FFT_J * GRID_W, FFT_J * GRID_W)
    m_s = np.einsum("ds,cC->dcCs", np.sin(ang_s), eye).reshape(FFT_J * GRID_W, FFT_J * GRID_W)
    ls = np.concatenate([m_c, -m_s], axis=1) * scale
    sb = np.arange(GRID_W // FFT_J)
    s_of = sb[:, None, None] * FFT_J + np.arange(FFT_J)[None, None, :]
    ang_t = 2.0 * np.pi * ((s_of * a[None, :, None]) % n) / n
    ang_t = ang_t.reshape(len(sb), r1 * FFT_J, 1)
    tw_cos = jnp.broadcast_to(jnp.asarray(np.cos(ang_t), F32), (len(sb), r1 * FFT_J, LANES))
    tw_sin = jnp.broadcast_to(jnp.asarray(np.sin(ang_t), F32), (len(sb), r1 * FFT_J, LANES))
    return jnp.asarray(cs, BF16), jnp.asarray(lr, BF16), jnp.asarray(ls, BF16), tw_cos, tw_sin


def kernel(x, c, ctx, c_ctx, ada_w, ada_b, norm_mix, norm_ffn, mix_w_in, mix_w_out, na_q_norm, na_k_norm, na_rpb,
           lru_conv_w, lru_conv_b, lru_gate_r_w, lru_gate_r_b, lru_gate_i_w, lru_gate_i_b, lru_lambda,
           fnet_w_out, router_w, router_bias, moe_w_gate, moe_w_up, moe_w_down):
    batch, n, d = x.shape
    ctx_len = ctx.shape[1]
    depth = ada_w.shape[0]
    rows = n // GRID_W
    assert d == D_MODEL and n % (GRID_W * NA_QROWS) == 0 and rows >= 4 * NA_QROWS
    assert n % MOE_TILE == 0
    t = batch * n
    tm = 512
    tri = jnp.asarray(np.triu(np.ones((MOE_TILE, MOE_TILE))), BF16)

    r_pad = -(-(batch + 1) // SUBLANES) * SUBLANES
    c_rows = jnp.concatenate([c, c_ctx[None, :], jnp.zeros((r_pad - batch - 1, d), c.dtype)], axis=0)
    mod = _modulation(c_rows, ada_w, ada_b)

    def mod_slices(layer):
        m = mod[layer, :batch].reshape(batch, 1, 6, d)
        return [m[:, :, i, :] for i in range(6)]

    rw = jnp.pad(router_w.astype(F32), ((0, 0), (0, LANES - N_EXPERTS)))
    rw_hi = rw.astype(BF16)
    rw_cat = jnp.concatenate([rw_hi, (rw - rw_hi.astype(F32)).astype(BF16)], axis=1)
    rbias = router_bias.reshape(N_EXPERTS, 1).astype(F32)
    x2d = x.reshape(t, d)
    ctx2d = ctx.reshape(batch * ctx_len, d)

    for layer in range(depth):
        li = layer // 2
        shift1, scale1, gate1, shift2, scale2, gate2 = mod_slices(layer)
        gain_mix = norm_mix[layer].reshape(1, d)
        gain_ffn = norm_ffn[layer].reshape(1, d)
        if layer % 2 == 0:
            w_in = mix_w_in[li].astype(BF16)
            ind = jnp.asarray(np.kron(np.eye(NA_HEADS), np.ones((HEAD_DIM, HEAD_DIM))), BF16)
            qg = (jnp.tile(na_q_norm[li], NA_HEADS) * HEAD_DIM ** -0.5).reshape(1, NA_WIDTH).astype(F32)
            kg = jnp.tile(na_k_norm[li], NA_HEADS).reshape(1, NA_WIDTH).astype(F32)
            q, k, v, xb, gb = _inproj(x2d, shift1, scale1, gain_mix, w_in, ind, qg, kg,
                                      ("q", "k", "v", "x", "g"), n, 2 * tm)
            mctx = mod[layer, batch, :2 * d]
            shift_c = jnp.broadcast_to(mctx[:d], (batch, 1, d))
            scale_c = jnp.broadcast_to(mctx[d:], (batch, 1, d))
            k_c, v_c, xb_c = _inproj(ctx2d, shift_c, scale_c, gain_mix, w_in[:, NA_WIDTH:4 * NA_WIDTH], ind, qg, kg,
                                     ("k", "v", "x"), ctx_len, ctx_len)
            bias = _na_bias_tables(na_rpb[li], rows)
            attn = _attention(q, k, v, k_c, v_c, bias, batch, n, ctx_len)
            wcat, gbias = _lru_gate_weights(lru_gate_r_w[li], lru_gate_r_b[li], lru_gate_i_w[li], lru_gate_i_b[li])
            lru = _lru(xb, gb, xb_c, lru_conv_w[li].astype(F32), lru_conv_b[li].reshape(1, LRU_WIDTH).astype(F32),
                       wcat, gbias, lru_lambda[li].astype(F32), batch, n, ctx_len)
            parts, w_out = [attn, lru], mix_w_out[li].astype(BF16)
        else:
            cs, lr, ls, tw_cos, tw_sin = _dft_tables(n)
            y1, y2 = _fnet_channel(x2d, shift1, scale1, gain_mix, cs, n, tm)
            parts, w_out = [_fnet_position(y1, y2, lr, ls, tw_cos, tw_sin, batch, n)], fnet_w_out[li].astype(BF16)
        x1, h2, route = _post_mixer(parts, w_out, x2d, gate1, shift2, scale2, gain_ffn, rw_cat, rbias, n,
                                    2 * POST_SUBTILE)
        x2d = _grouped_moe(h2, route, x1, gate2, moe_w_gate.astype(F32), moe_w_up.astype(F32),
                           moe_w_down.astype(F32), layer, tri, n)
    return x2d.reshape(batch, n, d)
```

```python
import functools

import numpy as np
import jax
import jax.numpy as jnp
from jax import lax
from jax.experimental import pallas as pl
from jax.experimental.pallas import tpu as pltpu

F32 = jnp.float32
BF16 = jnp.bfloat16
HIGHEST = lax.Precision.HIGHEST

D_MODEL = 1024
GRID_W = 64
HEAD_DIM = 64
NA_HEADS = 8
NA_WIDTH = NA_HEADS * HEAD_DIM
NA_WIN_ROWS = 8
NA_WIN_COLS = 16
LRU_WIDTH = 512
LRU_BLOCK = 64
LRU_C = 8.0
FNET_GROUPS = 4
N_EXPERTS = 16
EXPERTS_PER_GROUP = 4
N_EXPERT_GROUPS = 4
D_FF_EXPERT = 512
RMS_EPS = 1e-6
MASK_VALUE = -1e30
LOG2_E = 1.4426950408889634

V7X_VMEM_LIMIT_BYTES = 56 * 1024 * 1024
LANES = 128
SUBLANES = 8

NA_QROWS = 4
NA_KROWS = NA_QROWS + NA_WIN_ROWS - 1
NA_QBLK = NA_QROWS * GRID_W
NA_KBLK = NA_KROWS * GRID_W

LRU_CHUNK = LANES
LRU_TROWS = 512

ROUTE_GID_ROW = EXPERTS_PER_GROUP
MOE_TILE = 512
ROW_ALIGN = 16
MOE_CROWS = MOE_TILE + N_EXPERT_GROUPS * ROW_ALIGN
MOE_SEG_BITS = (MOE_TILE // ROW_ALIGN).bit_length()
MOE_TAIL_BITS = (MOE_TILE // ROW_ALIGN - 1).bit_length()


def _sigmoid(x):
    return 1.0 / (1.0 + jnp.exp(-x))


def _sigmoid_tanh(x):
    return 0.5 + 0.5 * jnp.tanh(0.5 * x)


def _cparams(sem, vmem=V7X_VMEM_LIMIT_BYTES):
    return pltpu.CompilerParams(dimension_semantics=sem, vmem_limit_bytes=vmem)


def _mod_kernel(c_ref, w_ref, b_ref, o_ref):
    c = c_ref[...]
    s = c * _sigmoid(c)
    o_ref[0] = jnp.dot(s, w_ref[0], precision=HIGHEST, preferred_element_type=F32) + b_ref[0]


def _modulation(c_rows, ada_w, ada_b):
    depth, d, n6 = ada_w.shape
    r = c_rows.shape[0]
    tn = 1536
    return pl.pallas_call(
        _mod_kernel,
        out_shape=jax.ShapeDtypeStruct((depth, r, n6), F32),
        grid=(depth, n6 // tn),
        in_specs=[pl.BlockSpec((r, d), lambda l, j: (0, 0)),
                  pl.BlockSpec((1, d, tn), lambda l, j: (l, 0, j)),
                  pl.BlockSpec((1, 1, tn), lambda l, j: (l, 0, j))],
        out_specs=pl.BlockSpec((1, r, tn), lambda l, j: (l, 0, j)),
        compiler_params=_cparams(("arbitrary", "arbitrary")),
        name="adaln_mod",
    )(c_rows, ada_w, ada_b.reshape(depth, 1, n6))


def _norm_modulate(x, gain, shift, scale):
    ms = jnp.mean(x * x, axis=-1, keepdims=True)
    y = x * lax.rsqrt(ms + RMS_EPS) * gain
    return y * (1.0 + scale) + shift


def _inproj_kernel(x_ref, shift_ref, scale_ref, gain_ref, w_ref, ind_ref, qg_ref, kg_ref, *out_refs, segs):
    h = _norm_modulate(x_ref[...], gain_ref[...], shift_ref[0], scale_ref[0]).astype(BF16)
    for s, (kind, o_ref) in enumerate(zip(segs, out_refs)):
        z = jnp.dot(h, w_ref[:, s * NA_WIDTH:(s + 1) * NA_WIDTH], preferred_element_type=F32)
        if kind in ("q", "k"):
            ms = jnp.dot((z * z).astype(BF16), ind_ref[...], preferred_element_type=F32) * (1.0 / HEAD_DIM)
            g = qg_ref[...] if kind == "q" else kg_ref[...]
            z = z * lax.rsqrt(ms + RMS_EPS) * g
        if kind in ("x", "g"):
            for c in range(LRU_WIDTH // LRU_CHUNK):
                o_ref[c] = z[:, c * LRU_CHUNK:(c + 1) * LRU_CHUNK].astype(o_ref.dtype)
        else:
            o_ref[...] = z.astype(o_ref.dtype)


def _inproj(x2d, shift, scale, gain, w, ind, qg, kg, segs, tokens_per_batch, tm):
    t, d = x2d.shape
    tpb = tokens_per_batch // tm
    dt = {"q": BF16, "k": BF16, "v": BF16, "x": F32, "g": F32}
    full = lambda i: (0, 0)
    nch = LRU_WIDTH // LRU_CHUNK

    def out_shape(kind):
        shape = (nch, t, LRU_CHUNK) if kind in ("x", "g") else (t, NA_WIDTH)
        return jax.ShapeDtypeStruct(shape, dt[kind])

    def out_spec(kind):
        if kind in ("x", "g"):
            return pl.BlockSpec((nch, tm, LRU_CHUNK), lambda i: (0, i, 0))
        return pl.BlockSpec((tm, NA_WIDTH), lambda i: (i, 0))

    return pl.pallas_call(
        functools.partial(_inproj_kernel, segs=segs),
        out_shape=[out_shape(k) for k in segs],
        grid=(t // tm,),
        in_specs=[pl.BlockSpec((tm, d), lambda i: (i, 0)),
                  pl.BlockSpec((1, 1, d), lambda i: (i // tpb, 0, 0)),
                  pl.BlockSpec((1, 1, d), lambda i: (i // tpb, 0, 0)),
                  pl.BlockSpec((1, d), full),
                  pl.BlockSpec(w.shape, full),
                  pl.BlockSpec(ind.shape, full),
                  pl.BlockSpec((1, NA_WIDTH), full),
                  pl.BlockSpec((1, NA_WIDTH), full)],
        out_specs=[out_spec(k) for k in segs],
        compiler_params=_cparams(("arbitrary",)),
        name="inproj_" + "".join(segs),
    )(x2d, shift, scale, gain, w, ind, qg, kg)


def _na_bias_tables(rpb, rows):
    kr = NA_WIN_ROWS
    rb_count = rows // NA_QROWS
    cq = np.arange(GRID_W)
    ck = np.arange(GRID_W)
    col_start = np.clip(cq - NA_WIN_COLS // 2, 0, GRID_W - NA_WIN_COLS)
    valid_c = (ck[None, :] >= col_start[:, None]) & (ck[None, :] < col_start[:, None] + NA_WIN_COLS)
    dc = np.clip(ck[None, :] - cq[:, None], 1 - NA_WIN_COLS, NA_WIN_COLS - 1) + (NA_WIN_COLS - 1)
    n_dr, n_dc = 2 * NA_WIN_ROWS - 1, 2 * NA_WIN_COLS - 1
    sel_c = (dc[:, :, None] == np.arange(n_dc)) & valid_c[:, :, None]
    blocks = jnp.einsum("hrc,qkc->hrqk", rpb.astype(F32), jnp.asarray(sel_c, F32), precision=HIGHEST)
    blocks = blocks + jnp.asarray(np.where(valid_c, 0.0, MASK_VALUE), F32)
    blocks = jnp.concatenate([blocks, jnp.full((NA_HEADS, 1, GRID_W, GRID_W), MASK_VALUE, F32)], axis=1)
    blocks = blocks * LOG2_E
    which = []
    for rb in (0, 1, rb_count - 1):
        r = rb * NA_QROWS + np.arange(NA_QROWS)
        ks = int(np.clip(rb * NA_QROWS - kr // 2, 0, rows - NA_KROWS))
        key_r = ks + np.arange(NA_KROWS)
        row_start = np.clip(r - kr // 2, 0, rows - kr)
        valid_r = (key_r[None, :] >= row_start[:, None]) & (key_r[None, :] < row_start[:, None] + kr)
        dr = np.clip(key_r[None, :] - r[:, None] + (NA_WIN_ROWS - 1), 0, n_dr - 1)
        which.append(np.where(valid_r, dr, n_dr))
    return _na_bias_assemble(blocks, which)


def _na_bias_kernel(blk_ref, o_ref, *, which):
    for t, table in enumerate(which):
        @pl.when(pl.program_id(0) == t)
        def _():
            for i in range(NA_QROWS):
                row = jnp.concatenate([blk_ref[0, int(table[i, j])] for j in range(NA_KROWS)], axis=1)
                o_ref[0, 0, i * GRID_W:(i + 1) * GRID_W, :] = row


def _na_bias_assemble(blocks, which):
    heads, nblk = blocks.shape[:2]
    return pl.pallas_call(
        functools.partial(_na_bias_kernel, which=which),
        out_shape=jax.ShapeDtypeStruct((len(which), heads, NA_QBLK, NA_KBLK), F32),
        grid=(len(which), heads),
        in_specs=[pl.BlockSpec((1, nblk, GRID_W, GRID_W), lambda t, h: (h, 0, 0, 0))],
        out_specs=pl.BlockSpec((1, 1, NA_QBLK, NA_KBLK), lambda t, h: (t, h, 0, 0)),
        compiler_params=_cparams(("arbitrary", "arbitrary")),
        name="na_bias",
    )(blocks)


def _attn_kernel(q_ref, k_ref, v_ref, kc_ref, vc_ref, bias_ref, o_ref, *, rows):
    rb = pl.program_id(1)
    ks = jnp.clip(rb * NA_QROWS - NA_WIN_ROWS // 2, 0, rows - NA_KROWS)
    kstart = pl.multiple_of(ks * GRID_W, GRID_W)
    last = rows // NA_QROWS - 1
    geom = jnp.where(rb == 0, 0, jnp.where(rb == last, 2, 1))
    nt = (((1,), (1,)), ((), ()))
    ctx_len = kc_ref.shape[0]
    low_half = lax.broadcasted_iota(jnp.int32, (NA_QBLK, LANES), 1) < HEAD_DIM
    for pair in range(NA_HEADS * HEAD_DIM // LANES):
        ls = slice(pair * LANES, (pair + 1) * LANES)
        q2 = q_ref[:, ls]
        k_all = jnp.concatenate([kc_ref[:, ls], k_ref[pl.ds(kstart, NA_KBLK), ls]], axis=0)
        v_all = jnp.concatenate([vc_ref[:, ls], v_ref[pl.ds(kstart, NA_KBLK), ls]], axis=0)
        outs = []
        for half in range(2):
            qh = jnp.where(low_half == (half == 0), q2, jnp.zeros_like(q2))
            s = lax.dot_general(qh, k_all, nt, preferred_element_type=F32)
            s = jnp.concatenate([s[:, :ctx_len], s[:, ctx_len:] + bias_ref[geom, 2 * pair + half]], axis=1)
            m = jnp.max(s, axis=-1, keepdims=True)
            p = jnp.exp2(s - m)
            l = jnp.sum(p, axis=-1, keepdims=True)
            outs.append(jnp.dot(p.astype(BF16), v_all, preferred_element_type=F32) / l)
        o_ref[:, ls] = jnp.where(low_half, outs[0], outs[1]).astype(o_ref.dtype)


def _attention(q, k, v, kc, vc, bias, batch, n, ctx_len):
    rows = n // GRID_W
    rbc = rows // NA_QROWS
    return pl.pallas_call(
        functools.partial(_attn_kernel, rows=rows),
        out_shape=jax.ShapeDtypeStruct((batch * n, NA_WIDTH), BF16),
        grid=(batch, rbc),
        in_specs=[pl.BlockSpec((NA_QBLK, NA_WIDTH), lambda b, rb: (b * rbc + rb, 0)),
                  pl.BlockSpec((n, NA_WIDTH), lambda b, rb: (b, 0)),
                  pl.BlockSpec((n, NA_WIDTH), lambda b, rb: (b, 0)),
                  pl.BlockSpec((ctx_len, NA_WIDTH), lambda b, rb: (b, 0)),
                  pl.BlockSpec((ctx_len, NA_WIDTH), lambda b, rb: (b, 0)),
                  pl.BlockSpec(bias.shape, lambda b, rb: (0, 0, 0, 0), pipeline_mode=pl.Buffered(1))],
        out_specs=pl.BlockSpec((NA_QBLK, NA_WIDTH), lambda b, rb: (b * rbc + rb, 0)),
        compiler_params=_cparams(("arbitrary", "arbitrary")),
        name="na_attention",
    )(q, k, v, kc, vc, bias)


SCAN_GROUPS = 2
SCAN_CHUNKS = SCAN_GROUPS * SUBLANES


def _scan_pitch(n):
    p = -(-n // SCAN_CHUNKS)
    while p % 8 != 4:
        p += 1
    return p


NEG_LOG2_E = -LOG2_E


def _lru_coeff_tile(half_xc, zh, half_bias, k, d):
    c = LRU_CHUNK
    t_r = jnp.tanh(zh[:, (2 * d) * c:(2 * d + 1) * c] + half_bias[:, (2 * d) * c:(2 * d + 1) * c])
    t_i = jnp.tanh(zh[:, (2 * d + 1) * c:(2 * d + 2) * c] + half_bias[:, (2 * d + 1) * c:(2 * d + 2) * c])
    neg_log_a = k[d:d + 1, :] * (1.0 + t_r)
    a = jnp.exp2(neg_log_a * NEG_LOG2_E)
    one_minus_a2 = jnp.tanh(neg_log_a) * (a * a + 1.0)
    root = jnp.where(one_minus_a2 > 0.0, one_minus_a2 * lax.rsqrt(one_minus_a2), 0.0)
    return a, root * (half_xc + half_xc * t_i)


def _conv_tile(xpad, t0, w, b, rows):
    acc = b + w[0:1, :] * xpad[pl.ds(t0 + SUBLANES - 2, rows), :]
    acc = acc + w[1:2, :] * xpad[pl.ds(t0 + SUBLANES - 1, rows), :]
    acc = acc + w[2:3, :] * xpad[pl.ds(t0 + SUBLANES, rows), :]
    return acc + w[3:4, :] * xpad[pl.ds(t0 + SUBLANES + 1, rows), :]


SCAN_UNROLL = 4


def _group_rows(j, g, pitch):
    return pl.ds(g * SUBLANES * pitch + j, SUBLANES, stride=pitch)


def _chunk_totals(af_ref, bf_ref, ab_ref, bb_ref, pitch):
    def body(j, carry):
        jb = pitch - 1 - j
        out = []
        for g in range(SCAN_GROUPS):
            pf, hf, pb, hb = carry[4 * g:4 * g + 4]
            af = af_ref[_group_rows(j, g, pitch), :]
            ab = ab_ref[_group_rows(jb, g, pitch), :]
            out += [af * pf, af * hf + bf_ref[_group_rows(j, g, pitch), :],
                    ab * pb, ab * hb + bb_ref[_group_rows(jb, g, pitch), :]]
        return tuple(out)
    one = jnp.ones((SUBLANES, LRU_CHUNK), F32)
    zero = jnp.zeros((SUBLANES, LRU_CHUNK), F32)
    res = lax.fori_loop(0, pitch, body, (one, zero, one, zero) * SCAN_GROUPS, unroll=SCAN_UNROLL)
    fwd = [(res[4 * g], res[4 * g + 1]) for g in range(SCAN_GROUPS)]
    bwd = [(res[4 * g + 2], res[4 * g + 3]) for g in range(SCAN_GROUPS)]
    return fwd, bwd


def _chunk_starts(totals, h0, reverse):
    row = lax.broadcasted_iota(jnp.int32, (SUBLANES, LRU_CHUNK), 0)
    starts = [jnp.zeros((SUBLANES, LRU_CHUNK), F32) for _ in range(SCAN_GROUPS)]
    state = h0
    order = range(SCAN_CHUNKS - 1, -1, -1) if reverse else range(SCAN_CHUNKS)
    for c in order:
        g, s = divmod(c, SUBLANES)
        p_end, h_end = totals[g]
        starts[g] = jnp.where(row == s, state, starts[g])
        state = p_end[s:s + 1, :] * state + h_end[s:s + 1, :]
    return starts, state


def _scan_write(af_ref, bf_ref, hf_ref, ab_ref, bb_ref, hb_ref, starts_f, starts_b, pitch):
    def body(j, carry):
        jb = pitch - 1 - j
        out = []
        for g in range(SCAN_GROUPS):
            hf, hb = carry[2 * g:2 * g + 2]
            hf = af_ref[_group_rows(j, g, pitch), :] * hf + bf_ref[_group_rows(j, g, pitch), :]
            hb = ab_ref[_group_rows(jb, g, pitch), :] * hb + bb_ref[_group_rows(jb, g, pitch), :]
            hf_ref[_group_rows(j, g, pitch), :] = hf
            hb_ref[_group_rows(jb, g, pitch), :] = hb
            out += [hf, hb]
        return tuple(out)
    init = tuple(v for g in range(SCAN_GROUPS) for v in (starts_f[g], starts_b[g]))
    lax.fori_loop(0, pitch, body, init, unroll=SCAN_UNROLL)


def _lru_kernel(x_ref, g_ref, xc_ref, cw_ref, cb_ref, w_ref, gb_ref, lam_ref, o_ref,
                xpad, a0, b0, a1, b1, h0s, h1s, ca0, cb0, ca1, cb1, *, n, ctx_len):
    pitch = _scan_pitch(n)
    cpitch = _scan_pitch(ctx_len)
    cw = cw_ref[...]
    cb = cb_ref[...]
    gbias = gb_ref[0]
    lam = lam_ref[...]
    sp = jnp.maximum(-lam, 0.0) + jnp.log1p(jnp.exp(-jnp.abs(lam)))
    k = (0.5 * LRU_C) * sp
    wcat = w_ref[0]
    zeros8 = jnp.zeros((SUBLANES, LRU_CHUNK), F32)

    def fill_coeffs(src_rows, total, length, trows, a_refs, b_refs):
        for d in range(2):
            a_refs[d][pl.ds(length, total - length), :] = jnp.ones((total - length, LRU_CHUNK), F32)
            b_refs[d][pl.ds(length, total - length), :] = jnp.zeros((total - length, LRU_CHUNK), F32)
        xpad[pl.ds(0, SUBLANES), :] = zeros8
        xpad[pl.ds(SUBLANES + length, SUBLANES), :] = zeros8
        xpad[pl.ds(SUBLANES, length), :] = src_rows

        def tile(t, carry):
            t0 = pl.multiple_of(t * trows, SUBLANES)
            xc = _conv_tile(xpad, t0, cw, cb, trows)
            zh = jnp.dot(xc.astype(BF16), wcat, preferred_element_type=F32)
            half_xc = 0.5 * xc
            for d in range(2):
                a, b = _lru_coeff_tile(half_xc, zh, gbias, k, d)
                a_refs[d][pl.ds(t0, trows), :] = a
                b_refs[d][pl.ds(t0, trows), :] = b
            return carry
        lax.fori_loop(0, length // trows, tile, 0)

    fill_coeffs(xc_ref[0], SCAN_CHUNKS * cpitch, ctx_len, ctx_len, (ca0, ca1), (cb0, cb1))
    zero_state = jnp.zeros((1, LRU_CHUNK), F32)
    fwd, bwd = _chunk_totals(ca0, cb0, ca1, cb1, cpitch)
    _, init_f = _chunk_starts(fwd, zero_state, reverse=False)
    _, init_b = _chunk_starts(bwd, zero_state, reverse=True)

    fill_coeffs(x_ref[0], SCAN_CHUNKS * pitch, n, LRU_TROWS, (a0, a1), (b0, b1))
    fwd, bwd = _chunk_totals(a0, b0, a1, b1, pitch)
    starts_f, _ = _chunk_starts(fwd, init_f, reverse=False)
    starts_b, _ = _chunk_starts(bwd, init_b, reverse=True)
    _scan_write(a0, b0, h0s, a1, b1, h1s, starts_f, starts_b, pitch)

    def out_tile(t, carry):
        t0 = pl.multiple_of(t * LRU_TROWS, SUBLANES)
        y = h0s[pl.ds(t0, LRU_TROWS), :] + h1s[pl.ds(t0, LRU_TROWS), :]
        g = g_ref[0, pl.ds(t0, LRU_TROWS), :]
        gelu = 0.5 * g * (1.0 + jnp.tanh(0.7978845608028654 * (g + 0.044715 * (g * g * g))))
        o_ref[0, pl.ds(t0, LRU_TROWS), :] = (gelu * y).astype(o_ref.dtype)
        return carry
    lax.fori_loop(0, n // LRU_TROWS, out_tile, 0)


def _lru(xb, gb, xb_ctx, conv_w, conv_b, wcat, gbias, lam, batch, n, ctx_len):
    nch = LRU_WIDTH // LRU_CHUNK
    pitch = _scan_pitch(n)
    cpitch = _scan_pitch(ctx_len)
    big = pltpu.VMEM((SCAN_CHUNKS * pitch, LRU_CHUNK), F32)
    small = pltpu.VMEM((SCAN_CHUNKS * cpitch, LRU_CHUNK), F32)
    return pl.pallas_call(
        functools.partial(_lru_kernel, n=n, ctx_len=ctx_len),
        out_shape=jax.ShapeDtypeStruct((nch, batch * n, LRU_CHUNK), BF16),
        grid=(batch, nch),
        in_specs=[pl.BlockSpec((1, n, LRU_CHUNK), lambda b, c: (c, b, 0)),
                  pl.BlockSpec((1, n, LRU_CHUNK), lambda b, c: (c, b, 0)),
                  pl.BlockSpec((1, ctx_len, LRU_CHUNK), lambda b, c: (c, b, 0)),
                  pl.BlockSpec((4, LRU_CHUNK), lambda b, c: (0, c)),
                  pl.BlockSpec((1, LRU_CHUNK), lambda b, c: (0, c)),
                  pl.BlockSpec((1, LRU_CHUNK, 4 * LRU_CHUNK), lambda b, c: (c, 0, 0)),
                  pl.BlockSpec((1, 1, 4 * LRU_CHUNK), lambda b, c: (c, 0, 0)),
                  pl.BlockSpec((2, LRU_CHUNK), lambda b, c: (0, c))],
        out_specs=pl.BlockSpec((1, n, LRU_CHUNK), lambda b, c: (c, b, 0)),
        scratch_shapes=[pltpu.VMEM((n + 2 * SUBLANES, LRU_CHUNK), F32),
                        big, big, big, big, big, big, small, small, small, small],
        compiler_params=_cparams(("arbitrary", "arbitrary")),
        name="rglru",
    )(xb, gb, xb_ctx, conv_w, conv_b, wcat, gbias, lam)


def _lru_gate_weights(w_r, b_r, w_i, b_i):
    nch = LRU_WIDTH // LRU_CHUNK
    bpc = LRU_CHUNK // LRU_BLOCK

    def dense(w):
        wc = w.reshape(nch, bpc, LRU_BLOCK, LRU_BLOCK)
        eye = jnp.eye(bpc, dtype=w.dtype)
        return jnp.einsum("cbij,bd->cbidj", wc, eye).reshape(nch, LRU_CHUNK, LRU_CHUNK)

    wcat = jnp.concatenate([dense(w_r[0]), dense(w_i[0]), dense(w_r[1]), dense(w_i[1])], axis=-1)
    chunk = lambda v: v.reshape(nch, 1, LRU_CHUNK)
    gbias = jnp.concatenate([chunk(b_r[0]), chunk(b_i[0]), chunk(b_r[1]), chunk(b_i[1])], axis=-1)
    return (0.5 * wcat).astype(BF16), (0.5 * gbias).astype(F32)


def _route(s, sel, route_ref):
    srow = [s[e:e + 1, :] for e in range(N_EXPERTS)]
    lrow = [sel[e:e + 1, :] for e in range(N_EXPERTS)]
    gscore = []
    for g in range(N_EXPERT_GROUPS):
        a = lrow[g * EXPERTS_PER_GROUP:(g + 1) * EXPERTS_PER_GROUP]
        best = a[0] + a[1]
        for i, j in ((0, 2), (0, 3), (1, 2), (1, 3), (2, 3)):
            best = jnp.maximum(best, a[i] + a[j])
        gscore.append(best)
    bg = jnp.zeros_like(gscore[0], dtype=jnp.int32)
    bv = gscore[0]
    for g in range(1, N_EXPERT_GROUPS):
        upd = gscore[g] > bv
        bg = jnp.where(upd, g, bg)
        bv = jnp.where(upd, gscore[g], bv)

    def pick(rows_):
        out = []
        for j in range(EXPERTS_PER_GROUP):
            v = rows_[j]
            for g in range(1, N_EXPERT_GROUPS):
                v = jnp.where(bg == g, rows_[g * EXPERTS_PER_GROUP + j], v)
            out.append(v)
        return out
    cand = pick(lrow)
    cs = pick(srow)
    i1 = jnp.zeros_like(bg)
    v1 = cand[0]
    w1 = cs[0]
    for j in range(1, EXPERTS_PER_GROUP):
        upd = cand[j] > v1
        i1 = jnp.where(upd, j, i1)
        v1 = jnp.where(upd, cand[j], v1)
        w1 = jnp.where(upd, cs[j], w1)
    i2 = jnp.full_like(bg, -1)
    v2 = jnp.full_like(v1, -jnp.inf)
    w2 = jnp.zeros_like(w1)
    for j in range(EXPERTS_PER_GROUP):
        upd = (i1 != j) & (cand[j] > v2)
        i2 = jnp.where(upd, j, i2)
        v2 = jnp.where(upd, cand[j], v2)
        w2 = jnp.where(upd, cs[j], w2)
    den = w1 + w2
    g1 = w1 / den
    g2 = w2 / den
    for j in range(EXPERTS_PER_GROUP):
        route_ref[j:j + 1, :] = jnp.where(i1 == j, g1, 0.0) + jnp.where(i2 == j, g2, 0.0)
    route_ref[ROUTE_GID_ROW:ROUTE_GID_ROW + 1, :] = bg.astype(F32)
    pad = SUBLANES - ROUTE_GID_ROW - 1
    route_ref[ROUTE_GID_ROW + 1:, :] = jnp.zeros((pad, bg.shape[1]), F32)


POST_SUBTILE = 512


def _post_kernel(*refs, n_parts):
    parts = refs[:n_parts]
    (w_ref, x_ref, gate_ref, shift_ref, scale_ref, gain_ref, rw_ref, rb_ref,
     x1_ref, h2_ref, route_ref) = refs[n_parts:]
    for sub in range(x_ref.shape[0] // POST_SUBTILE):
        rows = pl.ds(sub * POST_SUBTILE, POST_SUBTILE)
        pieces = []
        for p in parts:
            pieces += [p[c, rows, :] for c in range(p.shape[0])] if len(p.shape) == 3 else [p[rows, :]]
        mixed = jnp.concatenate(pieces, axis=-1) if len(pieces) > 1 else pieces[0]
        mix = jnp.dot(mixed, w_ref[...], preferred_element_type=F32)
        x1 = x_ref[rows, :] + gate_ref[0] * mix
        x1_ref[rows, :] = x1
        h2 = _norm_modulate(x1, gain_ref[...], shift_ref[0], scale_ref[0])
        h_hi = h2.astype(BF16)
        h2_ref[rows, :] = h_hi
        h_lo = (h2 - h_hi.astype(F32)).astype(BF16)
        prod = (jnp.dot(h_hi, rw_ref[...], preferred_element_type=F32)
                + jnp.dot(h_lo, rw_ref[...], preferred_element_type=F32))
        logits = prod[:, :LANES] + prod[:, LANES:]
        s = _sigmoid(logits.T[:N_EXPERTS, :])
        _route(s, s + rb_ref[...], route_ref.at[:, rows])


def _post_mixer(parts, w, x2d, gate1, shift2, scale2, gain, rw_cat, rbias, tokens_per_batch, tm):
    t, d = x2d.shape
    tpb = tokens_per_batch // tm
    full = lambda i: (0, 0)
    per_b = lambda i: (i // tpb, 0, 0)

    def part_spec(p):
        if p.ndim == 3:
            return pl.BlockSpec((p.shape[0], tm, p.shape[2]), lambda i: (0, i, 0))
        return pl.BlockSpec((tm, p.shape[1]), lambda i: (i, 0))

    return pl.pallas_call(
        functools.partial(_post_kernel, n_parts=len(parts)),
        out_shape=[jax.ShapeDtypeStruct((t, d), F32), jax.ShapeDtypeStruct((t, d), BF16),
                   jax.ShapeDtypeStruct((SUBLANES, t), F32)],
        grid=(t // tm,),
        in_specs=[part_spec(p) for p in parts] + [
                  pl.BlockSpec(w.shape, full),
                  pl.BlockSpec((tm, d), lambda i: (i, 0)),
                  pl.BlockSpec((1, 1, d), per_b),
                  pl.BlockSpec((1, 1, d), per_b),
                  pl.BlockSpec((1, 1, d), per_b),
                  pl.BlockSpec((1, d), full),
                  pl.BlockSpec(rw_cat.shape, full),
                  pl.BlockSpec(rbias.shape, full)],
        out_specs=[pl.BlockSpec((tm, d), lambda i: (i, 0)),
                   pl.BlockSpec((tm, d), lambda i: (i, 0)),
                   pl.BlockSpec((SUBLANES, tm), lambda i: (0, i))],
        compiler_params=_cparams(("arbitrary",)),
        name="post_mixer",
    )(*parts, w, x2d, gate1, shift2, scale2, gain, rw_cat, rbias)


def _moe_layout(t):
    nt = t // MOE_TILE
    grid = -(-(t + N_EXPERT_GROUPS * (ROW_ALIGN - 1) * nt) // MOE_TILE) + N_EXPERT_GROUPS
    return nt, grid


def _moe_tables(gid, t):
    nt, grid = _moe_layout(t)
    ng = N_EXPERT_GROUPS
    per_tile = MOE_TILE // ROW_ALIGN
    onehot = (gid.reshape(nt, MOE_TILE, 1) == jnp.arange(ng, dtype=jnp.int32)).astype(jnp.int32)
    cnt = onehot.sum(axis=1)
    seg = (cnt + ROW_ALIGN - 1) // ROW_ALIGN
    src = jnp.cumsum(seg, axis=1) - seg
    fill = seg.sum(axis=0)
    ntile = (fill + per_tile - 1) // per_tile
    cum = jnp.cumsum(ntile)
    base = (cum - ntile) * per_tile
    dst = jnp.cumsum(seg, axis=0) - seg + base[None, :]
    seg_tab = jnp.concatenate([seg, src, dst], axis=1).reshape(-1).astype(jnp.int32)
    tail = (-fill) % per_tile
    tail_tab = jnp.concatenate([tail, fill + base, cum[-1:]]).astype(jnp.int32)
    i = jnp.arange(grid, dtype=jnp.int32)
    valid = i < cum[-1]
    ie = jnp.minimum(i, cum[-1] - 1)
    g_of = jnp.sum((ie[:, None] >= cum[None, :]).astype(jnp.int32), axis=1)
    return seg_tab, tail_tab, g_of.astype(jnp.int32), valid.astype(jnp.int32)


def _segment_copies(tab_ref, tile, enable, make_copy):
    ng = N_EXPERT_GROUPS
    base = jnp.maximum(tile, 0) * (3 * ng)
    out = []
    for g in range(ng):
        n = tab_ref[base + g]
        src = tab_ref[base + ng + g]
        dst = tab_ref[base + 2 * ng + g]
        for k in range(MOE_SEG_BITS - 1, -1, -1):
            done = (n >> (k + 1)) << (k + 1)
            rows = ROW_ALIGN << k
            s0 = pl.multiple_of((src + done) * ROW_ALIGN, ROW_ALIGN)
            d0 = pl.multiple_of((dst + done) * ROW_ALIGN, ROW_ALIGN)
            out.append((enable & (((n >> k) & 1) == 1), make_copy(s0, d0, rows)))
    return out


def _start_copies(pairs):
    for cond, copies in pairs:
        @pl.when(cond)
        def _():
            for c in copies:
                c.start()


def _wait_copies(pairs):
    for cond, copies in pairs:
        @pl.when(cond)
        def _():
            for c in copies:
                c.wait()


def _split_bf16x3(x):
    hi = x.astype(BF16).astype(F32)
    r1 = x - hi
    mid = r1.astype(BF16).astype(F32)
    lo = (r1 - mid).astype(BF16).astype(F32)
    return hi, mid, lo


def _dispatch_kernel(seg_ref, tail_ref, h_ref, route_ref, tri_ref, slot_ref, hs_ref, gs_ref,
                     cbuf, gbuf, zh, zg, sem, *, nt):
    i = pl.program_id(0)
    tm = h_ref.shape[0]
    ng = N_EXPERT_GROUPS
    cur = i % 2

    def seg_copies(tile, enable, buf):
        def seg_copy(s0, d0, rows):
            return (pltpu.make_async_copy(cbuf.at[buf, pl.ds(s0, rows)], hs_ref.at[pl.ds(d0, rows)], sem.at[buf, 0]),
                    pltpu.make_async_copy(gbuf.at[buf, pl.ds(s0, rows)], gs_ref.at[pl.ds(d0, rows)], sem.at[buf, 1]))
        return _segment_copies(seg_ref, tile, enable, seg_copy)

    _wait_copies(seg_copies(i - 2, i >= 2, cur))

    route = route_ref[...]
    gid = route[ROUTE_GID_ROW:ROUTE_GID_ROW + 1, :]
    grp = lax.broadcasted_iota(jnp.int32, (SUBLANES, tm), 0).astype(F32)
    onehot = jnp.where(grp == gid, 1.0, 0.0)
    rank = jnp.dot(onehot.astype(BF16), tri_ref[...], preferred_element_type=F32)
    slot = jnp.zeros((1, tm), F32)
    for g in range(ng):
        start = (seg_ref[i * 3 * ng + ng + g] * ROW_ALIGN).astype(F32)
        slot = slot + onehot[g:g + 1, :] * (rank[g:g + 1, :] - 1.0 + start)
    slot_ref[...] = jnp.broadcast_to(slot, (SUBLANES, tm))
    perm = jnp.where(lax.broadcasted_iota(jnp.int32, (MOE_CROWS, tm), 0).astype(F32) == slot, 1.0, 0.0)
    perm = perm.astype(BF16)
    cbuf[cur] = jnp.dot(perm, h_ref[...], preferred_element_type=F32).astype(cbuf.dtype)
    parts = jnp.concatenate(list(_split_bf16x3(route)) + [jnp.zeros((LANES - 3 * SUBLANES, tm), F32)], axis=0)
    gbuf[cur] = lax.dot_general(perm, parts.astype(BF16), (((1,), (1,)), ((), ())), preferred_element_type=F32)
    _start_copies(seg_copies(i, i >= 0, cur))

    @pl.when(i == pl.num_programs(0) - 1)
    def _():
        _wait_copies(seg_copies(i - 1, i >= 1, 1 - cur))
        _wait_copies(seg_copies(i, i >= 0, cur))
        zh[...] = jnp.zeros(zh.shape, zh.dtype)
        zg[...] = jnp.zeros(zg.shape, zg.dtype)

        def zero_copy(d0, rows):
            return (pltpu.make_async_copy(zh.at[pl.ds(0, rows)], hs_ref.at[pl.ds(d0, rows)], sem.at[0, 0]),
                    pltpu.make_async_copy(zg.at[pl.ds(0, rows)], gs_ref.at[pl.ds(d0, rows)], sem.at[0, 1]))
        pairs = []
        for g in range(ng):
            n = tail_ref[g]
            dst = tail_ref[ng + g]
            for k in range(MOE_TAIL_BITS - 1, -1, -1):
                done = (n >> (k + 1)) << (k + 1)
                d0 = pl.multiple_of((dst + done) * ROW_ALIGN, ROW_ALIGN)
                pairs.append((((n >> k) & 1) == 1, zero_copy(d0, ROW_ALIGN << k)))
        used = tail_ref[2 * ng]
        total = hs_ref.shape[0] // MOE_TILE
        for j in range(total - nt):
            d0 = pl.multiple_of(jnp.minimum(used + j, total - 1) * MOE_TILE, MOE_TILE)
            pairs.append((used + j < total, zero_copy(d0, MOE_TILE)))
        _start_copies(pairs)
        _wait_copies(pairs)


def _dispatch(seg_tab, tail_tab, h2, route, tri):
    t, d = h2.shape
    nt, grid = _moe_layout(t)
    rows = grid * MOE_TILE
    grid_spec = pltpu.PrefetchScalarGridSpec(
        num_scalar_prefetch=2,
        grid=(nt,),
        in_specs=[pl.BlockSpec((MOE_TILE, d), lambda i, *_: (i, 0)),
                  pl.BlockSpec((SUBLANES, MOE_TILE), lambda i, *_: (0, i)),
                  pl.BlockSpec((MOE_TILE, MOE_TILE), lambda i, *_: (0, 0))],
        out_specs=[pl.BlockSpec((SUBLANES, MOE_TILE), lambda i, *_: (0, i)),
                   pl.BlockSpec(memory_space=pl.ANY),
                   pl.BlockSpec(memory_space=pl.ANY)],
        scratch_shapes=[pltpu.VMEM((2, MOE_CROWS, d), BF16), pltpu.VMEM((2, MOE_CROWS, LANES), F32),
                        pltpu.VMEM((MOE_TILE, d), BF16), pltpu.VMEM((MOE_TILE, LANES), F32),
                        pltpu.SemaphoreType.DMA((2, 2))])
    return pl.pallas_call(
        functools.partial(_dispatch_kernel, nt=nt),
        out_shape=[jax.ShapeDtypeStruct((SUBLANES, t), F32),
                   jax.ShapeDtypeStruct((rows, d), BF16),
                   jax.ShapeDtypeStruct((rows, LANES), F32)],
        grid_spec=grid_spec,
        compiler_params=_cparams(("arbitrary",)),
        name="moe_dispatch",
    )(seg_tab, tail_tab, h2, route, tri)


def _ffn_kernel(grp_ref, valid_ref, h_ref, g_ref, wg32_ref, wu32_ref, wd32_ref, y_ref, wg_ref, wu_ref, wd_ref):
    i = pl.program_id(0)

    @pl.when((i == 0) | (grp_ref[i] != grp_ref[jnp.maximum(i - 1, 0)]))
    def _():
        for j in range(EXPERTS_PER_GROUP):
            wg_ref[j] = wg32_ref[j].astype(BF16)
            wu_ref[j] = wu32_ref[j].astype(BF16)
            wd_ref[j] = wd32_ref[j].astype(BF16)

    @pl.when(valid_ref[i] == 0)
    def _():
        y_ref[...] = jnp.zeros(y_ref.shape, y_ref.dtype)

    @pl.when(valid_ref[i] == 1)
    def _():
        h = h_ref[...]
        gates = g_ref[...]
        y = jnp.zeros(y_ref.shape, F32)
        for j in range(EXPERTS_PER_GROUP):
            a = jnp.dot(h, wg_ref[j], preferred_element_type=F32)
            u = jnp.dot(h, wu_ref[j], preferred_element_type=F32)
            gate = (gates[:, j:j + 1] + gates[:, SUBLANES + j:SUBLANES + j + 1]
                    + gates[:, 2 * SUBLANES + j:2 * SUBLANES + j + 1])
            act = (a * _sigmoid_tanh(a)) * u * gate
            y = y + jnp.dot(act.astype(BF16), wd_ref[j], preferred_element_type=F32)
        y_ref[...] = y.astype(y_ref.dtype)


def _ffn(grp, valid, hs, gs, wg, wu, wd, layer):
    rows, d = hs.shape
    epg = EXPERTS_PER_GROUP
    once = pl.Buffered(1)
    w_idx = lambda i, grp, valid: (layer, grp[i], 0, 0)
    grid_spec = pltpu.PrefetchScalarGridSpec(
        num_scalar_prefetch=2,
        grid=(rows // MOE_TILE,),
        in_specs=[pl.BlockSpec((MOE_TILE, d), lambda i, grp, valid: (i, 0)),
                  pl.BlockSpec((MOE_TILE, LANES), lambda i, grp, valid: (i, 0)),
                  pl.BlockSpec((None, epg, d, D_FF_EXPERT), w_idx, pipeline_mode=once),
                  pl.BlockSpec((None, epg, d, D_FF_EXPERT), w_idx, pipeline_mode=once),
                  pl.BlockSpec((None, epg, D_FF_EXPERT, d), w_idx, pipeline_mode=once)],
        out_specs=pl.BlockSpec((MOE_TILE, d), lambda i, grp, valid: (i, 0)),
        scratch_shapes=[pltpu.VMEM((epg, d, D_FF_EXPERT), BF16), pltpu.VMEM((epg, d, D_FF_EXPERT), BF16),
                        pltpu.VMEM((epg, D_FF_EXPERT, d), BF16)])
    return pl.pallas_call(
        _ffn_kernel,
        out_shape=jax.ShapeDtypeStruct((rows, d), BF16),
        grid_spec=grid_spec,
        compiler_params=_cparams(("arbitrary",)),
        name="moe_ffn",
    )(grp, valid, hs, gs, wg, wu, wd)


def _combine_kernel(seg_ref, x1_ref, slot_ref, gate2_ref, ys_ref, *rest, fnet):
    if fnet:
        shift_ref, scale_ref, gain_ref, cs_ref, o_ref, y1_ref, y2_ref, ybuf, sem = rest
    else:
        o_ref, ybuf, sem = rest
    i = pl.program_id(0)
    nt = pl.num_programs(0)
    tm = x1_ref.shape[0]
    cur = i % 2

    def seg_copies(tile, enable, buf):
        def seg_copy(s0, d0, rows):
            return (pltpu.make_async_copy(ys_ref.at[pl.ds(d0, rows)], ybuf.at[buf, pl.ds(s0, rows)], sem.at[buf]),)
        return _segment_copies(seg_ref, tile, enable, seg_copy)

    @pl.when(i == 0)
    def _():
        ybuf[...] = jnp.zeros(ybuf.shape, ybuf.dtype)
        _start_copies(seg_copies(i, i == 0, cur))

    nxt = jnp.minimum(i + 1, nt - 1)
    _start_copies(seg_copies(nxt, i + 1 < nt, 1 - cur))
    _wait_copies(seg_copies(i, i >= 0, cur))
    slot = slot_ref[0:1, :]
    perm = jnp.where(lax.broadcasted_iota(jnp.int32, (MOE_CROWS, tm), 0).astype(F32) == slot, 1.0, 0.0)
    y = lax.dot_general(perm.astype(BF16), ybuf[cur], (((0,), (0,)), ((), ())), preferred_element_type=F32)
    x = x1_ref[...] + gate2_ref[0] * y
    o_ref[...] = x
    if fnet:
        _fnet_channel_tile(x, shift_ref, scale_ref, gain_ref, cs_ref, y1_ref, y2_ref)


def _combine(seg_tab, x1, slot, gate2, ys, tokens_per_batch, fnet=None):
    t, d = x1.shape
    tpb = tokens_per_batch // MOE_TILE
    per_b = lambda i, *_: (i // tpb, 0, 0)
    full = lambda i, *_: (0, 0)
    in_specs = [pl.BlockSpec((MOE_TILE, d), lambda i, *_: (i, 0)),
                pl.BlockSpec((SUBLANES, MOE_TILE), lambda i, *_: (0, i)),
                pl.BlockSpec((1, 1, d), per_b),
                pl.BlockSpec(memory_space=pl.ANY)]
    out_shape = [jax.ShapeDtypeStruct((t, d), F32)]
    out_specs = [pl.BlockSpec((MOE_TILE, d), lambda i, *_: (i, 0))]
    args = [seg_tab, x1, slot, gate2, ys]
    if fnet is not None:
        shift, scale, gain, cs = fnet
        gw = d // FNET_GROUPS
        in_specs += [pl.BlockSpec((1, 1, d), per_b), pl.BlockSpec((1, 1, d), per_b), pl.BlockSpec((1, d), full),
                     pl.BlockSpec(cs.shape, full)]
        out_shape += [jax.ShapeDtypeStruct((FNET_GROUPS, t, gw), BF16)] * 2
        out_specs += [pl.BlockSpec((FNET_GROUPS, MOE_TILE, gw), lambda i, *_: (0, i, 0))] * 2
        args += [shift, scale, gain, cs]
    grid_spec = pltpu.PrefetchScalarGridSpec(
        num_scalar_prefetch=1,
        grid=(t // MOE_TILE,),
        in_specs=in_specs,
        out_specs=out_specs,
        scratch_shapes=[pltpu.VMEM((2, MOE_CROWS, d), BF16), pltpu.SemaphoreType.DMA((2,))])
    out = pl.pallas_call(
        functools.partial(_combine_kernel, fnet=fnet is not None),
        out_shape=out_shape,
        grid_spec=grid_spec,
        compiler_params=_cparams(("arbitrary",)),
        name="moe_combine_fnet" if fnet is not None else "moe_combine",
    )(*args)
    return out if fnet is not None else out[0]


def _grouped_moe(h2, route, x1, gate2, wg, wu, wd, layer, tri):
    t = h2.shape[0]
    gid = route[ROUTE_GID_ROW].astype(jnp.int32)
    seg_tab, tail_tab, grp, valid = _moe_tables(gid, t)
    slot, hs, gs = _dispatch(seg_tab, tail_tab, h2, route, tri)
    ys = _ffn(grp, valid, hs, gs, wg, wu, wd, layer)
    return seg_tab, x1, slot, gate2, ys


def _fnet_channel_tile(x, shift_ref, scale_ref, gain_ref, cs_ref, y1_ref, y2_ref):
    h = _norm_modulate(x, gain_ref[...], shift_ref[0], scale_ref[0]).astype(BF16)
    gw = D_MODEL // FNET_GROUPS
    for g in range(FNET_GROUPS):
        y = jnp.dot(h[:, g * gw:(g + 1) * gw], cs_ref[...], preferred_element_type=F32)
        y1_ref[g] = y[:, :gw].astype(y1_ref.dtype)
        y2_ref[g] = y[:, gw:].astype(y2_ref.dtype)


def _fnet_chan_kernel(x_ref, shift_ref, scale_ref, gain_ref, cs_ref, y1_ref, y2_ref):
    _fnet_channel_tile(x_ref[...], shift_ref, scale_ref, gain_ref, cs_ref, y1_ref, y2_ref)


def _fnet_channel(x2d, shift, scale, gain, cs, tokens_per_batch, tm):
    t, d = x2d.shape
    tpb = tokens_per_batch // tm
    gw = d // FNET_GROUPS
    full = lambda i: (0, 0)
    out = jax.ShapeDtypeStruct((FNET_GROUPS, t, gw), BF16)
    out_spec = pl.BlockSpec((FNET_GROUPS, tm, gw), lambda i: (0, i, 0))
    return pl.pallas_call(
        _fnet_chan_kernel,
        out_shape=[out, out],
        grid=(t // tm,),
        in_specs=[pl.BlockSpec((tm, d), lambda i: (i, 0)),
                  pl.BlockSpec((1, 1, d), lambda i: (i // tpb, 0, 0)),
                  pl.BlockSpec((1, 1, d), lambda i: (i // tpb, 0, 0)),
                  pl.BlockSpec((1, d), full),
                  pl.BlockSpec(cs.shape, full)],
        out_specs=[out_spec, out_spec],
        compiler_params=_cparams(("arbitrary",)),
        name="fnet_channel",
    )(x2d, shift, scale, gain, cs)


FFT_J = SUBLANES


def _fnet_pos_kernel(y1_ref, y2_ref, lr_ref, ls_ref, cs_ref, sn_ref, o_ref, z1, z2, a_re, a_im, *, n):
    r1 = n // GRID_W
    gw = y1_ref.shape[2]
    nrj = FFT_J * r1
    z1[...] = y1_ref[0].astype(F32)
    z2[...] = y2_ref[0].astype(F32)

    def stage_r(sb, carry):
        s0 = pl.multiple_of(sb * FFT_J, FFT_J)
        rhs1 = jnp.concatenate([z1[pl.ds(GRID_W * r + s0, FFT_J), :] for r in range(r1)], axis=0).astype(BF16)
        rhs2 = jnp.concatenate([z2[pl.ds(GRID_W * r + s0, FFT_J), :] for r in range(r1)], axis=0).astype(BF16)
        p = jnp.dot(lr_ref[...], rhs1, preferred_element_type=F32)
        q = jnp.dot(lr_ref[...], rhs2, preferred_element_type=F32)
        re = p[:nrj] - q[nrj:]
        nim = q[:nrj] + p[nrj:]
        cs = jnp.concatenate([cs_ref[sb]] * (gw // LANES), axis=1)
        sn = jnp.concatenate([sn_ref[sb]] * (gw // LANES), axis=1)
        tre = re * cs - nim * sn
        tnim = re * sn + nim * cs
        for c in range(r1):
            a_re[pl.ds(GRID_W * c + s0, FFT_J), :] = tre[FFT_J * c:FFT_J * (c + 1)]
            a_im[pl.ds(GRID_W * c + s0, FFT_J), :] = tnim[FFT_J * c:FFT_J * (c + 1)]
        return carry
    lax.fori_loop(0, GRID_W // FFT_J, stage_r, 0)

    cblk = FFT_J * GRID_W

    def stage_s(cb, carry):
        c0 = pl.multiple_of(cb * cblk, cblk)
        blk = jnp.concatenate([a_re[pl.ds(c0, cblk), :], a_im[pl.ds(c0, cblk), :]], axis=0).astype(BF16)
        o = jnp.dot(ls_ref[...], blk, preferred_element_type=F32)
        k0 = pl.multiple_of(cb * FFT_J, FFT_J)
        for d in range(GRID_W):
            z1[pl.ds(r1 * d + k0, FFT_J), :] = o[FFT_J * d:FFT_J * (d + 1)]
        return carry
    lax.fori_loop(0, r1 // FFT_J, stage_s, 0)
    o_ref[0] = z1[...].astype(o_ref.dtype)


def _fnet_position(y1, y2, lr, ls, tw_cos, tw_sin, batch, n):
    groups, t, gw = y1.shape
    full2 = lambda b, g: (0, 0)
    full3 = lambda b, g: (0, 0, 0)
    scratch = pltpu.VMEM((n, gw), F32)
    return pl.pallas_call(
        functools.partial(_fnet_pos_kernel, n=n),
        out_shape=jax.ShapeDtypeStruct((groups, t, gw), BF16),
        grid=(batch, groups),
        in_specs=[pl.BlockSpec((1, n, gw), lambda b, g: (g, b, 0)),
                  pl.BlockSpec((1, n, gw), lambda b, g: (g, b, 0)),
                  pl.BlockSpec(lr.shape, full2),
                  pl.BlockSpec(ls.shape, full2),
                  pl.BlockSpec(tw_cos.shape, full3),
                  pl.BlockSpec(tw_sin.shape, full3)],
        out_specs=pl.BlockSpec((1, n, gw), lambda b, g: (g, b, 0)),
        scratch_shapes=[scratch, scratch, scratch, scratch],
        compiler_params=_cparams(("arbitrary", "arbitrary")),
        name="fnet_position",
    )(y1, y2, lr, ls, tw_cos, tw_sin)


def _dft_tables(n):
    assert n % (GRID_W * FFT_J) == 0
    gw = D_MODEL // FNET_GROUPS
    j = np.arange(gw)
    ang = 2.0 * np.pi * ((j[:, None] * j[None, :]) % gw) / gw
    cs = np.concatenate([np.cos(ang), np.sin(ang)], axis=1) / np.sqrt(gw)
    r1 = n // GRID_W
    eye = np.eye(FFT_J)
    scale = float(n) ** -0.25
    a = np.arange(r1)
    ang_r = 2.0 * np.pi * ((a[:, None] * a[None, :]) % r1) / r1
    lr = np.concatenate([np.kron(np.cos(ang_r), eye), np.kron(np.sin(ang_r), eye)], axis=0) * scale
    s = np.arange(GRID_W)
    ang_s = 2.0 * np.pi * ((s[:, None] * s[None, :]) % GRID_W) / GRID_W
    m_c = np.einsum("ds,cC->dcCs", np.cos(ang_s), eye).reshape(FFT_J * GRID_W, FFT_J * GRID_W)
    m_s = np.einsum("ds,cC->dcCs", np.sin(ang_s), eye).reshape(FFT_J * GRID_W, FFT_J * GRID_W)
    ls = np.concatenate([m_c, -m_s], axis=1) * scale
    sb = np.arange(GRID_W // FFT_J)
    s_of = sb[:, None, None] * FFT_J + np.arange(FFT_J)[None, None, :]
    ang_t = 2.0 * np.pi * ((s_of * a[None, :, None]) % n) / n
    ang_t = ang_t.reshape(len(sb), r1 * FFT_J, 1)
    tw_cos = jnp.broadcast_to(jnp.asarray(np.cos(ang_t), F32), (len(sb), r1 * FFT_J, LANES))
    tw_sin = jnp.broadcast_to(jnp.asarray(np.sin(ang_t), F32), (len(sb), r1 * FFT_J, LANES))
    return jnp.asarray(cs, BF16), jnp.asarray(lr, BF16), jnp.asarray(ls, BF16), tw_cos, tw_sin


def kernel(x, c, ctx, c_ctx, ada_w, ada_b, norm_mix, norm_ffn, mix_w_in, mix_w_out, na_q_norm, na_k_norm, na_rpb,
           lru_conv_w, lru_conv_b, lru_gate_r_w, lru_gate_r_b, lru_gate_i_w, lru_gate_i_b, lru_lambda,
           fnet_w_out, router_w, router_bias, moe_w_gate, moe_w_up, moe_w_down):
    batch, n, d = x.shape
    ctx_len = ctx.shape[1]
    depth = ada_w.shape[0]
    rows = n // GRID_W
    assert d == D_MODEL and n % (GRID_W * NA_QROWS) == 0 and rows >= 4 * NA_QROWS
    assert n % MOE_TILE == 0
    t = batch * n
    tm = 512
    tri = jnp.asarray(np.triu(np.ones((MOE_TILE, MOE_TILE))), BF16)

    r_pad = -(-(batch + 1) // SUBLANES) * SUBLANES
    c_rows = jnp.concatenate([c, c_ctx[None, :], jnp.zeros((r_pad - batch - 1, d), c.dtype)], axis=0)
    mod = _modulation(c_rows, ada_w, ada_b)

    def mod_slices(layer):
        m = mod[layer, :batch].reshape(batch, 1, 6, d)
        return [m[:, :, i, :] for i in range(6)]

    rw = jnp.pad(router_w.astype(F32), ((0, 0), (0, LANES - N_EXPERTS)))
    rw_hi = rw.astype(BF16)
    rw_cat = jnp.concatenate([rw_hi, (rw - rw_hi.astype(F32)).astype(BF16)], axis=1)
    rbias = router_bias.reshape(N_EXPERTS, 1).astype(F32)
    x2d = x.reshape(t, d)
    ctx2d = ctx.reshape(batch * ctx_len, d)

    pending = None
    for layer in range(depth):
        li = layer // 2
        shift1, scale1, gate1, shift2, scale2, gate2 = mod_slices(layer)
        gain_mix = norm_mix[layer].reshape(1, d)
        gain_ffn = norm_ffn[layer].reshape(1, d)
        if layer % 2 == 0:
            if pending is not None:
                x2d = _combine(*pending, n)
            w_in = mix_w_in[li].astype(BF16)
            ind = jnp.asarray(np.kron(np.eye(NA_HEADS), np.ones((HEAD_DIM, HEAD_DIM))), BF16)
            qg = (jnp.tile(na_q_norm[li], NA_HEADS) * (HEAD_DIM ** -0.5 * LOG2_E)).reshape(1, NA_WIDTH).astype(F32)
            kg = jnp.tile(na_k_norm[li], NA_HEADS).reshape(1, NA_WIDTH).astype(F32)
            q, k, v, xb, gb = _inproj(x2d, shift1, scale1, gain_mix, w_in, ind, qg, kg,
                                      ("q", "k", "v", "x", "g"), n, 2 * tm)
            mctx = mod[layer, batch, :2 * d]
            shift_c = jnp.broadcast_to(mctx[:d], (batch, 1, d))
            scale_c = jnp.broadcast_to(mctx[d:], (batch, 1, d))
            k_c, v_c, xb_c = _inproj(ctx2d, shift_c, scale_c, gain_mix, w_in[:, NA_WIDTH:4 * NA_WIDTH], ind, qg, kg,
                                     ("k", "v", "x"), ctx_len, ctx_len)
            bias = _na_bias_tables(na_rpb[li], rows)
            attn = _attention(q, k, v, k_c, v_c, bias, batch, n, ctx_len)
            wcat, gbias = _lru_gate_weights(lru_gate_r_w[li], lru_gate_r_b[li], lru_gate_i_w[li], lru_gate_i_b[li])
            lru = _lru(xb, gb, xb_c, lru_conv_w[li].astype(F32), lru_conv_b[li].reshape(1, LRU_WIDTH).astype(F32),
                       wcat, gbias, lru_lambda[li].astype(F32), batch, n, ctx_len)
            parts, w_out = [attn, lru], mix_w_out[li].astype(BF16)
        else:
            cs, lr, ls, tw_cos, tw_sin = _dft_tables(n)
            if pending is not None:
                x2d, y1, y2 = _combine(*pending, n, fnet=(shift1, scale1, gain_mix, cs))
            else:
                y1, y2 = _fnet_channel(x2d, shift1, scale1, gain_mix, cs, n, tm)
            parts, w_out = [_fnet_position(y1, y2, lr, ls, tw_cos, tw_sin, batch, n)], fnet_w_out[li].astype(BF16)
        x1, h2, route = _post_mixer(parts, w_out, x2d, gate1, shift2, scale2, gain_ffn, rw_cat, rbias, n,
                                    2 * POST_SUBTILE)
        pending = _grouped_moe(h2, route, x1, gate2, moe_w_gate.astype(F32), moe_w_up.astype(F32),
                               moe_w_down.astype(F32), layer, tri)
    return _combine(*pending, n).reshape(batch, n, d)
```

```python
import functools

import numpy as np
import jax
import jax.numpy as jnp
from jax import lax
from jax.experimental import pallas as pl
from jax.experimental.pallas import tpu as pltpu

F32 = jnp.float32
BF16 = jnp.bfloat16
HIGHEST = lax.Precision.HIGHEST

D_MODEL = 1024
GRID_W = 64
HEAD_DIM = 64
NA_HEADS = 8
NA_WIDTH = NA_HEADS * HEAD_DIM
NA_WIN_ROWS = 8
NA_WIN_COLS = 16
LRU_WIDTH = 512
LRU_BLOCK = 64
LRU_C = 8.0
FNET_GROUPS = 4
N_EXPERTS = 16
EXPERTS_PER_GROUP = 4
N_EXPERT_GROUPS = 4
D_FF_EXPERT = 512
RMS_EPS = 1e-6
MASK_VALUE = -1e30
LOG2_E = 1.4426950408889634

V7X_VMEM_LIMIT_BYTES = 56 * 1024 * 1024
LANES = 128
SUBLANES = 8

NA_QROWS = 4
NA_KROWS = NA_QROWS + NA_WIN_ROWS - 1
NA_QBLK = NA_QROWS * GRID_W
NA_KBLK = NA_KROWS * GRID_W

LRU_CHUNK = LANES
LRU_TROWS = 512

ROUTE_GID_ROW = EXPERTS_PER_GROUP
MOE_TILE = 512
ROW_ALIGN = 16
MOE_CROWS = MOE_TILE + N_EXPERT_GROUPS * ROW_ALIGN
MOE_SEG_BITS = (MOE_TILE // ROW_ALIGN).bit_length()
MOE_TAIL_BITS = (MOE_TILE // ROW_ALIGN - 1).bit_length()


def _sigmoid(x):
    return 1.0 / (1.0 + jnp.exp(-x))


def _sigmoid_tanh(x):
    return 0.5 + 0.5 * jnp.tanh(0.5 * x)


def _cparams(sem, vmem=V7X_VMEM_LIMIT_BYTES):
    return pltpu.CompilerParams(dimension_semantics=sem, vmem_limit_bytes=vmem)


def _mod_kernel(c_ref, w_ref, b_ref, o_ref):
    c = c_ref[...]
    s = c * _sigmoid(c)
    o_ref[0] = jnp.dot(s, w_ref[0], precision=HIGHEST, preferred_element_type=F32) + b_ref[0]


def _modulation(c_rows, ada_w, ada_b):
    depth, d, n6 = ada_w.shape
    r = c_rows.shape[0]
    tn = 1536
    return pl.pallas_call(
        _mod_kernel,
        out_shape=jax.ShapeDtypeStruct((depth, r, n6), F32),
        grid=(depth, n6 // tn),
        in_specs=[pl.BlockSpec((r, d), lambda l, j: (0, 0)),
                  pl.BlockSpec((1, d, tn), lambda l, j: (l, 0, j)),
                  pl.BlockSpec((1, 1, tn), lambda l, j: (l, 0, j))],
        out_specs=pl.BlockSpec((1, r, tn), lambda l, j: (l, 0, j)),
        compiler_params=_cparams(("arbitrary", "arbitrary")),
        name="adaln_mod",
    )(c_rows, ada_w, ada_b.reshape(depth, 1, n6))


def _norm_modulate(x, gain, shift, scale):
    ms = jnp.mean(x * x, axis=-1, keepdims=True)
    y = x * lax.rsqrt(ms + RMS_EPS) * gain
    return y * (1.0 + scale) + shift


def _inproj_kernel(x_ref, shift_ref, scale_ref, gain_ref, w_ref, ind_ref, qg_ref, kg_ref, *out_refs, segs):
    h = _norm_modulate(x_ref[...], gain_ref[...], shift_ref[0], scale_ref[0]).astype(BF16)
    for s, (kind, o_ref) in enumerate(zip(segs, out_refs)):
        z = jnp.dot(h, w_ref[:, s * NA_WIDTH:(s + 1) * NA_WIDTH], preferred_element_type=F32)
        if kind in ("q", "k"):
            ms = jnp.dot((z * z).astype(BF16), ind_ref[...], preferred_element_type=F32) * (1.0 / HEAD_DIM)
            g = qg_ref[...] if kind == "q" else kg_ref[...]
            z = z * lax.rsqrt(ms + RMS_EPS) * g
        if kind in ("x", "g"):
            for c in range(LRU_WIDTH // LRU_CHUNK):
                o_ref[c] = z[:, c * LRU_CHUNK:(c + 1) * LRU_CHUNK].astype(o_ref.dtype)
        else:
            o_ref[...] = z.astype(o_ref.dtype)


def _inproj(x2d, shift, scale, gain, w, ind, qg, kg, segs, tokens_per_batch, tm):
    t, d = x2d.shape
    tpb = tokens_per_batch // tm
    dt = {"q": BF16, "k": BF16, "v": BF16, "x": F32, "g": F32}
    full = lambda i: (0, 0)
    nch = LRU_WIDTH // LRU_CHUNK

    def out_shape(kind):
        shape = (nch, t, LRU_CHUNK) if kind in ("x", "g") else (t, NA_WIDTH)
        return jax.ShapeDtypeStruct(shape, dt[kind])

    def out_spec(kind):
        if kind in ("x", "g"):
            return pl.BlockSpec((nch, tm, LRU_CHUNK), lambda i: (0, i, 0))
        return pl.BlockSpec((tm, NA_WIDTH), lambda i: (i, 0))

    return pl.pallas_call(
        functools.partial(_inproj_kernel, segs=segs),
        out_shape=[out_shape(k) for k in segs],
        grid=(t // tm,),
        in_specs=[pl.BlockSpec((tm, d), lambda i: (i, 0)),
                  pl.BlockSpec((1, 1, d), lambda i: (i // tpb, 0, 0)),
                  pl.BlockSpec((1, 1, d), lambda i: (i // tpb, 0, 0)),
                  pl.BlockSpec((1, d), full),
                  pl.BlockSpec(w.shape, full),
                  pl.BlockSpec(ind.shape, full),
                  pl.BlockSpec((1, NA_WIDTH), full),
                  pl.BlockSpec((1, NA_WIDTH), full)],
        out_specs=[out_spec(k) for k in segs],
        compiler_params=_cparams(("arbitrary",)),
        name="inproj_" + "".join(segs),
    )(x2d, shift, scale, gain, w, ind, qg, kg)


def _na_bias_tables(rpb, rows):
    kr = NA_WIN_ROWS
    rb_count = rows // NA_QROWS
    cq = np.arange(GRID_W)
    ck = np.arange(GRID_W)
    col_start = np.clip(cq - NA_WIN_COLS // 2, 0, GRID_W - NA_WIN_COLS)
    valid_c = (ck[None, :] >= col_start[:, None]) & (ck[None, :] < col_start[:, None] + NA_WIN_COLS)
    dc = np.clip(ck[None, :] - cq[:, None], 1 - NA_WIN_COLS, NA_WIN_COLS - 1) + (NA_WIN_COLS - 1)
    n_dr, n_dc = 2 * NA_WIN_ROWS - 1, 2 * NA_WIN_COLS - 1
    sel_c = (dc[:, :, None] == np.arange(n_dc)) & valid_c[:, :, None]
    blocks = jnp.einsum("hrc,qkc->hrqk", rpb.astype(F32), jnp.asarray(sel_c, F32), precision=HIGHEST)
    blocks = blocks + jnp.asarray(np.where(valid_c, 0.0, MASK_VALUE), F32)
    blocks = jnp.concatenate([blocks, jnp.full((NA_HEADS, 1, GRID_W, GRID_W), MASK_VALUE, F32)], axis=1)
    blocks = blocks * LOG2_E
    which = []
    for rb in (0, 1, rb_count - 1):
        r = rb * NA_QROWS + np.arange(NA_QROWS)
        ks = int(np.clip(rb * NA_QROWS - kr // 2, 0, rows - NA_KROWS))
        key_r = ks + np.arange(NA_KROWS)
        row_start = np.clip(r - kr // 2, 0, rows - kr)
        valid_r = (key_r[None, :] >= row_start[:, None]) & (key_r[None, :] < row_start[:, None] + kr)
        dr = np.clip(key_r[None, :] - r[:, None] + (NA_WIN_ROWS - 1), 0, n_dr - 1)
        which.append(np.where(valid_r, dr, n_dr))
    return _na_bias_assemble(blocks, which)


def _na_bias_kernel(blk_ref, o_ref, *, which):
    for t, table in enumerate(which):
        @pl.when(pl.program_id(0) == t)
        def _():
            for i in range(NA_QROWS):
                row = jnp.concatenate([blk_ref[0, int(table[i, j])] for j in range(NA_KROWS)], axis=1)
                o_ref[0, 0, i * GRID_W:(i + 1) * GRID_W, :] = row


def _na_bias_assemble(blocks, which):
    heads, nblk = blocks.shape[:2]
    return pl.pallas_call(
        functools.partial(_na_bias_kernel, which=which),
        out_shape=jax.ShapeDtypeStruct((len(which), heads, NA_QBLK, NA_KBLK), F32),
        grid=(len(which), heads),
        in_specs=[pl.BlockSpec((1, nblk, GRID_W, GRID_W), lambda t, h: (h, 0, 0, 0))],
        out_specs=pl.BlockSpec((1, 1, NA_QBLK, NA_KBLK), lambda t, h: (t, h, 0, 0)),
        compiler_params=_cparams(("arbitrary", "arbitrary")),
        name="na_bias",
    )(blocks)


def _attn_kernel(q_ref, k_ref, v_ref, kc_ref, vc_ref, bias_ref, o_ref, *, rows):
    rb = pl.program_id(1)
    ks = jnp.clip(rb * NA_QROWS - NA_WIN_ROWS // 2, 0, rows - NA_KROWS)
    kstart = pl.multiple_of(ks * GRID_W, GRID_W)
    last = rows // NA_QROWS - 1
    geom = jnp.where(rb == 0, 0, jnp.where(rb == last, 2, 1))
    nt = (((1,), (1,)), ((), ()))
    ctx_len = kc_ref.shape[0]
    low_half = lax.broadcasted_iota(jnp.int32, (NA_QBLK, LANES), 1) < HEAD_DIM
    for pair in range(NA_HEADS * HEAD_DIM // LANES):
        ls = slice(pair * LANES, (pair + 1) * LANES)
        q2 = q_ref[:, ls]
        k_all = jnp.concatenate([kc_ref[:, ls], k_ref[pl.ds(kstart, NA_KBLK), ls]], axis=0)
        v_all = jnp.concatenate([vc_ref[:, ls], v_ref[pl.ds(kstart, NA_KBLK), ls]], axis=0)
        outs = []
        for half in range(2):
            qh = jnp.where(low_half == (half == 0), q2, jnp.zeros_like(q2))
            s = lax.dot_general(qh, k_all, nt, preferred_element_type=F32)
            s = jnp.concatenate([s[:, :ctx_len], s[:, ctx_len:] + bias_ref[geom, 2 * pair + half]], axis=1)
            m = jnp.max(s, axis=-1, keepdims=True)
            p = jnp.exp2(s - m)
            l = jnp.sum(p, axis=-1, keepdims=True)
            outs.append(jnp.dot(p.astype(BF16), v_all, preferred_element_type=F32) / l)
        o_ref[:, ls] = jnp.where(low_half, outs[0], outs[1]).astype(o_ref.dtype)


def _attention(q, k, v, kc, vc, bias, batch, n, ctx_len):
    rows = n // GRID_W
    rbc = rows // NA_QROWS
    return pl.pallas_call(
        functools.partial(_attn_kernel, rows=rows),
        out_shape=jax.ShapeDtypeStruct((batch * n, NA_WIDTH), BF16),
        grid=(batch, rbc),
        in_specs=[pl.BlockSpec((NA_QBLK, NA_WIDTH), lambda b, rb: (b * rbc + rb, 0)),
                  pl.BlockSpec((n, NA_WIDTH), lambda b, rb: (b, 0)),
                  pl.BlockSpec((n, NA_WIDTH), lambda b, rb: (b, 0)),
                  pl.BlockSpec((ctx_len, NA_WIDTH), lambda b, rb: (b, 0)),
                  pl.BlockSpec((ctx_len, NA_WIDTH), lambda b, rb: (b, 0)),
                  pl.BlockSpec(bias.shape, lambda b, rb: (0, 0, 0, 0), pipeline_mode=pl.Buffered(1))],
        out_specs=pl.BlockSpec((NA_QBLK, NA_WIDTH), lambda b, rb: (b * rbc + rb, 0)),
        compiler_params=_cparams(("arbitrary", "arbitrary")),
        name="na_attention",
    )(q, k, v, kc, vc, bias)


SCAN_GROUPS = 2
SCAN_CHUNKS = SCAN_GROUPS * SUBLANES


def _scan_pitch(n):
    p = -(-n // SCAN_CHUNKS)
    while p % 8 != 4:
        p += 1
    return p


NEG_LOG2_E = -LOG2_E


def _lru_coeff_tile(half_xc, zh, half_bias, k, d):
    c = LRU_CHUNK
    t_r = jnp.tanh(zh[:, (2 * d) * c:(2 * d + 1) * c] + half_bias[:, (2 * d) * c:(2 * d + 1) * c])
    t_i = jnp.tanh(zh[:, (2 * d + 1) * c:(2 * d + 2) * c] + half_bias[:, (2 * d + 1) * c:(2 * d + 2) * c])
    neg_log_a = k[d:d + 1, :] * (1.0 + t_r)
    a = jnp.exp2(neg_log_a * NEG_LOG2_E)
    one_minus_a2 = jnp.tanh(neg_log_a) * (a * a + 1.0)
    root = jnp.where(one_minus_a2 > 0.0, one_minus_a2 * lax.rsqrt(one_minus_a2), 0.0)
    return a, root * (half_xc + half_xc * t_i)


def _conv_tile(xpad, t0, w, b, rows):
    acc = b + w[0:1, :] * xpad[pl.ds(t0 + SUBLANES - 2, rows), :]
    acc = acc + w[1:2, :] * xpad[pl.ds(t0 + SUBLANES - 1, rows), :]
    acc = acc + w[2:3, :] * xpad[pl.ds(t0 + SUBLANES, rows), :]
    return acc + w[3:4, :] * xpad[pl.ds(t0 + SUBLANES + 1, rows), :]


SCAN_UNROLL = 4


def _group_rows(j, g, pitch):
    return pl.ds(g * SUBLANES * pitch + j, SUBLANES, stride=pitch)


def _chunk_totals(af_ref, bf_ref, ab_ref, bb_ref, pitch):
    def body(j, carry):
        jb = pitch - 1 - j
        out = []
        for g in range(SCAN_GROUPS):
            pf, hf, pb, hb = carry[4 * g:4 * g + 4]
            af = af_ref[_group_rows(j, g, pitch), :]
            ab = ab_ref[_group_rows(jb, g, pitch), :]
            out += [af * pf, af * hf + bf_ref[_group_rows(j, g, pitch), :],
                    ab * pb, ab * hb + bb_ref[_group_rows(jb, g, pitch), :]]
        return tuple(out)
    one = jnp.ones((SUBLANES, LRU_CHUNK), F32)
    zero = jnp.zeros((SUBLANES, LRU_CHUNK), F32)
    res = lax.fori_loop(0, pitch, body, (one, zero, one, zero) * SCAN_GROUPS, unroll=SCAN_UNROLL)
    fwd = [(res[4 * g], res[4 * g + 1]) for g in range(SCAN_GROUPS)]
    bwd = [(res[4 * g + 2], res[4 * g + 3]) for g in range(SCAN_GROUPS)]
    return fwd, bwd


def _chunk_starts(totals, h0, reverse):
    row = lax.broadcasted_iota(jnp.int32, (SUBLANES, LRU_CHUNK), 0)
    starts = [jnp.zeros((SUBLANES, LRU_CHUNK), F32) for _ in range(SCAN_GROUPS)]
    state = h0
    order = range(SCAN_CHUNKS - 1, -1, -1) if reverse else range(SCAN_CHUNKS)
    for c in order:
        g, s = divmod(c, SUBLANES)
        p_end, h_end = totals[g]
        starts[g] = jnp.where(row == s, state, starts[g])
        state = p_end[s:s + 1, :] * state + h_end[s:s + 1, :]
    return starts, state


def _scan_write(af_ref, bf_ref, hf_ref, ab_ref, bb_ref, hb_ref, starts_f, starts_b, pitch):
    def body(j, carry):
        jb = pitch - 1 - j
        out = []
        for g in range(SCAN_GROUPS):
            hf, hb = carry[2 * g:2 * g + 2]
            hf = af_ref[_group_rows(j, g, pitch), :] * hf + bf_ref[_group_rows(j, g, pitch), :]
            hb = ab_ref[_group_rows(jb, g, pitch), :] * hb + bb_ref[_group_rows(jb, g, pitch), :]
            hf_ref[_group_rows(j, g, pitch), :] = hf
            hb_ref[_group_rows(jb, g, pitch), :] = hb
            out += [hf, hb]
        return tuple(out)
    init = tuple(v for g in range(SCAN_GROUPS) for v in (starts_f[g], starts_b[g]))
    lax.fori_loop(0, pitch, body, init, unroll=SCAN_UNROLL)


def _lru_kernel(x_ref, g_ref, xc_ref, cw_ref, cb_ref, w_ref, gb_ref, lam_ref, o_ref,
                xpad, a0, b0, a1, b1, h0s, h1s, ca0, cb0, ca1, cb1, *, n, ctx_len):
    pitch = _scan_pitch(n)
    cpitch = _scan_pitch(ctx_len)
    cw = cw_ref[...]
    cb = cb_ref[...]
    gbias = gb_ref[0]
    lam = lam_ref[...]
    sp = jnp.maximum(-lam, 0.0) + jnp.log1p(jnp.exp(-jnp.abs(lam)))
    k = (0.5 * LRU_C) * sp
    wcat = w_ref[0]
    zeros8 = jnp.zeros((SUBLANES, LRU_CHUNK), F32)

    def fill_coeffs(src_rows, total, length, trows, a_refs, b_refs):
        for d in range(2):
            a_refs[d][pl.ds(length, total - length), :] = jnp.ones((total - length, LRU_CHUNK), F32)
            b_refs[d][pl.ds(length, total - length), :] = jnp.zeros((total - length, LRU_CHUNK), F32)
        xpad[pl.ds(0, SUBLANES), :] = zeros8
        xpad[pl.ds(SUBLANES + length, SUBLANES), :] = zeros8
        xpad[pl.ds(SUBLANES, length), :] = src_rows

        def tile(t, carry):
            t0 = pl.multiple_of(t * trows, SUBLANES)
            xc = _conv_tile(xpad, t0, cw, cb, trows)
            zh = jnp.dot(xc.astype(BF16), wcat, preferred_element_type=F32)
            half_xc = 0.5 * xc
            for d in range(2):
                a, b = _lru_coeff_tile(half_xc, zh, gbias, k, d)
                a_refs[d][pl.ds(t0, trows), :] = a
                b_refs[d][pl.ds(t0, trows), :] = b
            return carry
        lax.fori_loop(0, length // trows, tile, 0)

    fill_coeffs(xc_ref[0], SCAN_CHUNKS * cpitch, ctx_len, ctx_len, (ca0, ca1), (cb0, cb1))
    zero_state = jnp.zeros((1, LRU_CHUNK), F32)
    fwd, bwd = _chunk_totals(ca0, cb0, ca1, cb1, cpitch)
    _, init_f = _chunk_starts(fwd, zero_state, reverse=False)
    _, init_b = _chunk_starts(bwd, zero_state, reverse=True)

    fill_coeffs(x_ref[0], SCAN_CHUNKS * pitch, n, LRU_TROWS, (a0, a1), (b0, b1))
    fwd, bwd = _chunk_totals(a0, b0, a1, b1, pitch)
    starts_f, _ = _chunk_starts(fwd, init_f, reverse=False)
    starts_b, _ = _chunk_starts(bwd, init_b, reverse=True)
    _scan_write(a0, b0, h0s, a1, b1, h1s, starts_f, starts_b, pitch)

    def out_tile(t, carry):
        t0 = pl.multiple_of(t * LRU_TROWS, SUBLANES)
        y = h0s[pl.ds(t0, LRU_TROWS), :] + h1s[pl.ds(t0, LRU_TROWS), :]
        g = g_ref[0, pl.ds(t0, LRU_TROWS), :]
        gelu = 0.5 * g * (1.0 + jnp.tanh(0.7978845608028654 * (g + 0.044715 * (g * g * g))))
        o_ref[0, pl.ds(t0, LRU_TROWS), :] = (gelu * y).astype(o_ref.dtype)
        return carry
    lax.fori_loop(0, n // LRU_TROWS, out_tile, 0)


def _lru(xb, gb, xb_ctx, conv_w, conv_b, wcat, gbias, lam, batch, n, ctx_len):
    nch = LRU_WIDTH // LRU_CHUNK
    pitch = _scan_pitch(n)
    cpitch = _scan_pitch(ctx_len)
    big = pltpu.VMEM((SCAN_CHUNKS * pitch, LRU_CHUNK), F32)
    small = pltpu.VMEM((SCAN_CHUNKS * cpitch, LRU_CHUNK), F32)
    return pl.pallas_call(
        functools.partial(_lru_kernel, n=n, ctx_len=ctx_len),
        out_shape=jax.ShapeDtypeStruct((nch, batch * n, LRU_CHUNK), BF16),
        grid=(batch, nch),
        in_specs=[pl.BlockSpec((1, n, LRU_CHUNK), lambda b, c: (c, b, 0)),
                  pl.BlockSpec((1, n, LRU_CHUNK), lambda b, c: (c, b, 0)),
                  pl.BlockSpec((1, ctx_len, LRU_CHUNK), lambda b, c: (c, b, 0)),
                  pl.BlockSpec((4, LRU_CHUNK), lambda b, c: (0, c)),
                  pl.BlockSpec((1, LRU_CHUNK), lambda b, c: (0, c)),
                  pl.BlockSpec((1, LRU_CHUNK, 4 * LRU_CHUNK), lambda b, c: (c, 0, 0)),
                  pl.BlockSpec((1, 1, 4 * LRU_CHUNK), lambda b, c: (c, 0, 0)),
                  pl.BlockSpec((2, LRU_CHUNK), lambda b, c: (0, c))],
        out_specs=pl.BlockSpec((1, n, LRU_CHUNK), lambda b, c: (c, b, 0)),
        scratch_shapes=[pltpu.VMEM((n + 2 * SUBLANES, LRU_CHUNK), F32),
                        big, big, big, big, big, big, small, small, small, small],
        compiler_params=_cparams(("arbitrary", "arbitrary")),
        name="rglru",
    )(xb, gb, xb_ctx, conv_w, conv_b, wcat, gbias, lam)


def _lru_gate_weights(w_r, b_r, w_i, b_i):
    nch = LRU_WIDTH // LRU_CHUNK
    bpc = LRU_CHUNK // LRU_BLOCK

    def dense(w):
        wc = w.reshape(nch, bpc, LRU_BLOCK, LRU_BLOCK)
        eye = jnp.eye(bpc, dtype=w.dtype)
        return jnp.einsum("cbij,bd->cbidj", wc, eye).reshape(nch, LRU_CHUNK, LRU_CHUNK)

    wcat = jnp.concatenate([dense(w_r[0]), dense(w_i[0]), dense(w_r[1]), dense(w_i[1])], axis=-1)
    chunk = lambda v: v.reshape(nch, 1, LRU_CHUNK)
    gbias = jnp.concatenate([chunk(b_r[0]), chunk(b_i[0]), chunk(b_r[1]), chunk(b_i[1])], axis=-1)
    return (0.5 * wcat).astype(BF16), (0.5 * gbias).astype(F32)


def _route(s, sel, route_ref):
    srow = [s[e:e + 1, :] for e in range(N_EXPERTS)]
    lrow = [sel[e:e + 1, :] for e in range(N_EXPERTS)]
    gscore = []
    for g in range(N_EXPERT_GROUPS):
        a = lrow[g * EXPERTS_PER_GROUP:(g + 1) * EXPERTS_PER_GROUP]
        best = a[0] + a[1]
        for i, j in ((0, 2), (0, 3), (1, 2), (1, 3), (2, 3)):
            best = jnp.maximum(best, a[i] + a[j])
        gscore.append(best)
    bg = jnp.zeros_like(gscore[0], dtype=jnp.int32)
    bv = gscore[0]
    for g in range(1, N_EXPERT_GROUPS):
        upd = gscore[g] > bv
        bg = jnp.where(upd, g, bg)
        bv = jnp.where(upd, gscore[g], bv)

    def pick(rows_):
        out = []
        for j in range(EXPERTS_PER_GROUP):
            v = rows_[j]
            for g in range(1, N_EXPERT_GROUPS):
                v = jnp.where(bg == g, rows_[g * EXPERTS_PER_GROUP + j], v)
            out.append(v)
        return out
    cand = pick(lrow)
    cs = pick(srow)
    i1 = jnp.zeros_like(bg)
    v1 = cand[0]
    w1 = cs[0]
    for j in range(1, EXPERTS_PER_GROUP):
        upd = cand[j] > v1
        i1 = jnp.where(upd, j, i1)
        v1 = jnp.where(upd, cand[j], v1)
        w1 = jnp.where(upd, cs[j], w1)
    i2 = jnp.full_like(bg, -1)
    v2 = jnp.full_like(v1, -jnp.inf)
    w2 = jnp.zeros_like(w1)
    for j in range(EXPERTS_PER_GROUP):
        upd = (i1 != j) & (cand[j] > v2)
        i2 = jnp.where(upd, j, i2)
        v2 = jnp.where(upd, cand[j], v2)
        w2 = jnp.where(upd, cs[j], w2)
    den = w1 + w2
    g1 = w1 / den
    g2 = w2 / den
    for j in range(EXPERTS_PER_GROUP):
        route_ref[j:j + 1, :] = jnp.where(i1 == j, g1, 0.0) + jnp.where(i2 == j, g2, 0.0)
    route_ref[ROUTE_GID_ROW:ROUTE_GID_ROW + 1, :] = bg.astype(F32)
    pad = SUBLANES - ROUTE_GID_ROW - 1
    route_ref[ROUTE_GID_ROW + 1:, :] = jnp.zeros((pad, bg.shape[1]), F32)


POST_SUBTILE = 512


def _post_kernel(*refs, n_parts):
    parts = refs[:n_parts]
    (w_ref, x_ref, gate_ref, shift_ref, scale_ref, gain_ref, rw_ref, rb_ref,
     x1_ref, h2_ref, route_ref) = refs[n_parts:]
    for sub in range(x_ref.shape[0] // POST_SUBTILE):
        rows = pl.ds(sub * POST_SUBTILE, POST_SUBTILE)
        pieces = []
        for p in parts:
            pieces += [p[c, rows, :] for c in range(p.shape[0])] if len(p.shape) == 3 else [p[rows, :]]
        mixed = jnp.concatenate(pieces, axis=-1) if len(pieces) > 1 else pieces[0]
        mix = jnp.dot(mixed, w_ref[...], preferred_element_type=F32)
        x1 = x_ref[rows, :] + gate_ref[0] * mix
        x1_ref[rows, :] = x1
        h2 = _norm_modulate(x1, gain_ref[...], shift_ref[0], scale_ref[0])
        h_hi = h2.astype(BF16)
        h2_ref[rows, :] = h_hi
        logits = jnp.dot(h_hi, rw_ref[...], preferred_element_type=F32)
        s = _sigmoid(logits[:, :LANES].T[:N_EXPERTS, :])
        _route(s, s + rb_ref[...], route_ref.at[:, rows])


def _post_mixer(parts, w, x2d, gate1, shift2, scale2, gain, rw_cat, rbias, tokens_per_batch, tm):
    t, d = x2d.shape
    tpb = tokens_per_batch // tm
    full = lambda i: (0, 0)
    per_b = lambda i: (i // tpb, 0, 0)

    def part_spec(p):
        if p.ndim == 3:
            return pl.BlockSpec((p.shape[0], tm, p.shape[2]), lambda i: (0, i, 0))
        return pl.BlockSpec((tm, p.shape[1]), lambda i: (i, 0))

    return pl.pallas_call(
        functools.partial(_post_kernel, n_parts=len(parts)),
        out_shape=[jax.ShapeDtypeStruct((t, d), F32), jax.ShapeDtypeStruct((t, d), BF16),
                   jax.ShapeDtypeStruct((SUBLANES, t), F32)],
        grid=(t // tm,),
        in_specs=[part_spec(p) for p in parts] + [
                  pl.BlockSpec(w.shape, full),
                  pl.BlockSpec((tm, d), lambda i: (i, 0)),
                  pl.BlockSpec((1, 1, d), per_b),
                  pl.BlockSpec((1, 1, d), per_b),
                  pl.BlockSpec((1, 1, d), per_b),
                  pl.BlockSpec((1, d), full),
                  pl.BlockSpec(rw_cat.shape, full),
                  pl.BlockSpec(rbias.shape, full)],
        out_specs=[pl.BlockSpec((tm, d), lambda i: (i, 0)),
                   pl.BlockSpec((tm, d), lambda i: (i, 0)),
                   pl.BlockSpec((SUBLANES, tm), lambda i: (0, i))],
        compiler_params=_cparams(("arbitrary",)),
        name="post_mixer",
    )(*parts, w, x2d, gate1, shift2, scale2, gain, rw_cat, rbias)


def _moe_layout(t):
    nt = t // MOE_TILE
    grid = -(-(t + N_EXPERT_GROUPS * (ROW_ALIGN - 1) * nt) // MOE_TILE) + N_EXPERT_GROUPS
    return nt, grid


def _moe_tables(gid, t):
    nt, grid = _moe_layout(t)
    ng = N_EXPERT_GROUPS
    per_tile = MOE_TILE // ROW_ALIGN
    onehot = (gid.reshape(nt, MOE_TILE, 1) == jnp.arange(ng, dtype=jnp.int32)).astype(jnp.int32)
    cnt = onehot.sum(axis=1)
    seg = (cnt + ROW_ALIGN - 1) // ROW_ALIGN
    src = jnp.cumsum(seg, axis=1) - seg
    fill = seg.sum(axis=0)
    ntile = (fill + per_tile - 1) // per_tile
    cum = jnp.cumsum(ntile)
    base = (cum - ntile) * per_tile
    dst = jnp.cumsum(seg, axis=0) - seg + base[None, :]
    seg_tab = jnp.concatenate([seg, src, dst], axis=1).reshape(-1).astype(jnp.int32)
    tail = (-fill) % per_tile
    tail_tab = jnp.concatenate([tail, fill + base, cum[-1:]]).astype(jnp.int32)
    i = jnp.arange(grid, dtype=jnp.int32)
    valid = i < cum[-1]
    ie = jnp.minimum(i, cum[-1] - 1)
    g_of = jnp.sum((ie[:, None] >= cum[None, :]).astype(jnp.int32), axis=1)
    return seg_tab, tail_tab, g_of.astype(jnp.int32), valid.astype(jnp.int32)


def _segment_copies(tab_ref, tile, enable, make_copy):
    ng = N_EXPERT_GROUPS
    base = jnp.maximum(tile, 0) * (3 * ng)
    out = []
    for g in range(ng):
        n = tab_ref[base + g]
        src = tab_ref[base + ng + g]
        dst = tab_ref[base + 2 * ng + g]
        for k in range(MOE_SEG_BITS - 1, -1, -1):
            done = (n >> (k + 1)) << (k + 1)
            rows = ROW_ALIGN << k
            s0 = pl.multiple_of((src + done) * ROW_ALIGN, ROW_ALIGN)
            d0 = pl.multiple_of((dst + done) * ROW_ALIGN, ROW_ALIGN)
            out.append((enable & (((n >> k) & 1) == 1), make_copy(s0, d0, rows)))
    return out


def _start_copies(pairs):
    for cond, copies in pairs:
        @pl.when(cond)
        def _():
            for c in copies:
                c.start()


def _wait_copies(pairs):
    for cond, copies in pairs:
        @pl.when(cond)
        def _():
            for c in copies:
                c.wait()


def _split_bf16x3(x):
    hi = x.astype(BF16).astype(F32)
    r1 = x - hi
    mid = r1.astype(BF16).astype(F32)
    lo = (r1 - mid).astype(BF16).astype(F32)
    return hi, mid, lo


def _dispatch_kernel(seg_ref, tail_ref, h_ref, route_ref, tri_ref, slot_ref, hs_ref, cbuf, zbuf, sem, *, nt):
    i = pl.program_id(0)
    tm, d = h_ref.shape
    ng = N_EXPERT_GROUPS
    cur = i % 2

    def seg_copies(tile, enable, buf):
        def seg_copy(s0, d0, rows):
            return (pltpu.make_async_copy(cbuf.at[buf, pl.ds(s0, rows)], hs_ref.at[pl.ds(d0, rows)], sem.at[buf]),)
        return _segment_copies(seg_ref, tile, enable, seg_copy)

    _wait_copies(seg_copies(i - 2, i >= 2, cur))

    route = route_ref[...]
    gid = route[ROUTE_GID_ROW:ROUTE_GID_ROW + 1, :]
    grp = lax.broadcasted_iota(jnp.int32, (SUBLANES, tm), 0).astype(F32)
    onehot = jnp.where(grp == gid, 1.0, 0.0)
    rank = jnp.dot(onehot.astype(BF16), tri_ref[...], preferred_element_type=F32)
    slot = jnp.zeros((1, tm), F32)
    for g in range(ng):
        start = (seg_ref[i * 3 * ng + ng + g] * ROW_ALIGN).astype(F32)
        slot = slot + onehot[g:g + 1, :] * (rank[g:g + 1, :] - 1.0 + start)
    slot_ref[...] = jnp.broadcast_to(slot, (SUBLANES, tm))
    perm = jnp.where(lax.broadcasted_iota(jnp.int32, (MOE_CROWS, tm), 0).astype(F32) == slot, 1.0, 0.0)
    perm = perm.astype(BF16)
    cbuf[cur, :, :d] = jnp.dot(perm, h_ref[...], preferred_element_type=F32).astype(cbuf.dtype)
    parts = jnp.concatenate(list(_split_bf16x3(route)) + [jnp.zeros((LANES - 3 * SUBLANES, tm), F32)], axis=0)
    record = lax.dot_general(perm, parts.astype(BF16), (((1,), (1,)), ((), ())), preferred_element_type=F32)
    cbuf[cur, :, d:] = record.astype(cbuf.dtype)
    _start_copies(seg_copies(i, i >= 0, cur))

    @pl.when(i == pl.num_programs(0) - 1)
    def _():
        _wait_copies(seg_copies(i - 1, i >= 1, 1 - cur))
        _wait_copies(seg_copies(i, i >= 0, cur))
        zbuf[...] = jnp.zeros(zbuf.shape, zbuf.dtype)

        def zero_copy(d0, rows):
            return (pltpu.make_async_copy(zbuf.at[pl.ds(0, rows)], hs_ref.at[pl.ds(d0, rows)], sem.at[0]),)
        pairs = []
        for g in range(ng):
            n = tail_ref[g]
            dst = tail_ref[ng + g]
            for k in range(MOE_TAIL_BITS - 1, -1, -1):
                done = (n >> (k + 1)) << (k + 1)
                d0 = pl.multiple_of((dst + done) * ROW_ALIGN, ROW_ALIGN)
                pairs.append((((n >> k) & 1) == 1, zero_copy(d0, ROW_ALIGN << k)))
        used = tail_ref[2 * ng]
        total = hs_ref.shape[0] // MOE_TILE
        for j in range(total - nt):
            d0 = pl.multiple_of(jnp.minimum(used + j, total - 1) * MOE_TILE, MOE_TILE)
            pairs.append((used + j < total, zero_copy(d0, MOE_TILE)))
        _start_copies(pairs)
        _wait_copies(pairs)


def _dispatch(seg_tab, tail_tab, h2, route, tri):
    t, d = h2.shape
    nt, grid = _moe_layout(t)
    rows = grid * MOE_TILE
    grid_spec = pltpu.PrefetchScalarGridSpec(
        num_scalar_prefetch=2,
        grid=(nt,),
        in_specs=[pl.BlockSpec((MOE_TILE, d), lambda i, *_: (i, 0)),
                  pl.BlockSpec((SUBLANES, MOE_TILE), lambda i, *_: (0, i)),
                  pl.BlockSpec((MOE_TILE, MOE_TILE), lambda i, *_: (0, 0))],
        out_specs=[pl.BlockSpec((SUBLANES, MOE_TILE), lambda i, *_: (0, i)),
                   pl.BlockSpec(memory_space=pl.ANY)],
        scratch_shapes=[pltpu.VMEM((2, MOE_CROWS, d + LANES), BF16), pltpu.VMEM((MOE_TILE, d + LANES), BF16),
                        pltpu.SemaphoreType.DMA((2,))])
    return pl.pallas_call(
        functools.partial(_dispatch_kernel, nt=nt),
        out_shape=[jax.ShapeDtypeStruct((SUBLANES, t), F32),
                   jax.ShapeDtypeStruct((rows, d + LANES), BF16)],
        grid_spec=grid_spec,
        compiler_params=_cparams(("arbitrary",)),
        name="moe_dispatch",
    )(seg_tab, tail_tab, h2, route, tri)


def _ffn_kernel(grp_ref, valid_ref, h_ref, wg32_ref, wu32_ref, wd32_ref, y_ref, wg_ref, wu_ref, wd_ref):
    i = pl.program_id(0)
    d = y_ref.shape[1]

    @pl.when((i == 0) | (grp_ref[i] != grp_ref[jnp.maximum(i - 1, 0)]))
    def _():
        for j in range(EXPERTS_PER_GROUP):
            wg_ref[j] = wg32_ref[j].astype(BF16)
            wu_ref[j] = wu32_ref[j].astype(BF16)
            wd_ref[j] = wd32_ref[j].astype(BF16)

    @pl.when(valid_ref[i] == 0)
    def _():
        y_ref[...] = jnp.zeros(y_ref.shape, y_ref.dtype)

    @pl.when(valid_ref[i] == 1)
    def _():
        h = h_ref[:, :d]
        gates = h_ref[:, d:].astype(F32)
        y = jnp.zeros(y_ref.shape, F32)
        for j in range(EXPERTS_PER_GROUP):
            a = jnp.dot(h, wg_ref[j], preferred_element_type=F32)
            u = jnp.dot(h, wu_ref[j], preferred_element_type=F32)
            gate = (gates[:, j:j + 1] + gates[:, SUBLANES + j:SUBLANES + j + 1]
                    + gates[:, 2 * SUBLANES + j:2 * SUBLANES + j + 1])
            act = (a * _sigmoid_tanh(a)) * u * gate
            y = y + jnp.dot(act.astype(BF16), wd_ref[j], preferred_element_type=F32)
        y_ref[...] = y.astype(y_ref.dtype)


def _ffn(grp, valid, hs, wg, wu, wd, layer):
    rows, width = hs.shape
    d = width - LANES
    epg = EXPERTS_PER_GROUP
    once = pl.Buffered(1)
    w_idx = lambda i, grp, valid: (layer, grp[i], 0, 0)
    grid_spec = pltpu.PrefetchScalarGridSpec(
        num_scalar_prefetch=2,
        grid=(rows // MOE_TILE,),
        in_specs=[pl.BlockSpec((MOE_TILE, width), lambda i, grp, valid: (i, 0)),
                  pl.BlockSpec((None, epg, d, D_FF_EXPERT), w_idx, pipeline_mode=once),
                  pl.BlockSpec((None, epg, d, D_FF_EXPERT), w_idx, pipeline_mode=once),
                  pl.BlockSpec((None, epg, D_FF_EXPERT, d), w_idx, pipeline_mode=once)],
        out_specs=pl.BlockSpec((MOE_TILE, d), lambda i, grp, valid: (i, 0)),
        scratch_shapes=[pltpu.VMEM((epg, d, D_FF_EXPERT), BF16), pltpu.VMEM((epg, d, D_FF_EXPERT), BF16),
                        pltpu.VMEM((epg, D_FF_EXPERT, d), BF16)])
    return pl.pallas_call(
        _ffn_kernel,
        out_shape=jax.ShapeDtypeStruct((rows, d), BF16),
        grid_spec=grid_spec,
        compiler_params=_cparams(("arbitrary",)),
        name="moe_ffn",
    )(grp, valid, hs, wg, wu, wd)


def _combine_kernel(seg_ref, x1_ref, slot_ref, gate2_ref, ys_ref, *rest, fnet):
    if fnet:
        shift_ref, scale_ref, gain_ref, cs_ref, o_ref, y1_ref, y2_ref, ybuf, sem = rest
    else:
        o_ref, ybuf, sem = rest
    i = pl.program_id(0)
    nt = pl.num_programs(0)
    tm = x1_ref.shape[0]
    cur = i % 2

    def seg_copies(tile, enable, buf):
        def seg_copy(s0, d0, rows):
            return (pltpu.make_async_copy(ys_ref.at[pl.ds(d0, rows)], ybuf.at[buf, pl.ds(s0, rows)], sem.at[buf]),)
        return _segment_copies(seg_ref, tile, enable, seg_copy)

    @pl.when(i == 0)
    def _():
        ybuf[...] = jnp.zeros(ybuf.shape, ybuf.dtype)
        _start_copies(seg_copies(i, i == 0, cur))

    nxt = jnp.minimum(i + 1, nt - 1)
    _start_copies(seg_copies(nxt, i + 1 < nt, 1 - cur))
    _wait_copies(seg_copies(i, i >= 0, cur))
    slot = slot_ref[0:1, :]
    perm = jnp.where(lax.broadcasted_iota(jnp.int32, (MOE_CROWS, tm), 0).astype(F32) == slot, 1.0, 0.0)
    y = lax.dot_general(perm.astype(BF16), ybuf[cur], (((0,), (0,)), ((), ())), preferred_element_type=F32)
    x = x1_ref[...] + gate2_ref[0] * y
    o_ref[...] = x
    if fnet:
        _fnet_channel_tile(x, shift_ref, scale_ref, gain_ref, cs_ref, y1_ref, y2_ref)


def _combine(seg_tab, x1, slot, gate2, ys, tokens_per_batch, fnet=None):
    t, d = x1.shape
    tpb = tokens_per_batch // MOE_TILE
    per_b = lambda i, *_: (i // tpb, 0, 0)
    full = lambda i, *_: (0, 0)
    in_specs = [pl.BlockSpec((MOE_TILE, d), lambda i, *_: (i, 0)),
                pl.BlockSpec((SUBLANES, MOE_TILE), lambda i, *_: (0, i)),
                pl.BlockSpec((1, 1, d), per_b),
                pl.BlockSpec(memory_space=pl.ANY)]
    out_shape = [jax.ShapeDtypeStruct((t, d), F32)]
    out_specs = [pl.BlockSpec((MOE_TILE, d), lambda i, *_: (i, 0))]
    args = [seg_tab, x1, slot, gate2, ys]
    if fnet is not None:
        shift, scale, gain, cs = fnet
        gw = d // FNET_GROUPS
        in_specs += [pl.BlockSpec((1, 1, d), per_b), pl.BlockSpec((1, 1, d), per_b), pl.BlockSpec((1, d), full),
                     pl.BlockSpec(cs.shape, full)]
        out_shape += [jax.ShapeDtypeStruct((FNET_GROUPS, t, gw), BF16)] * 2
        out_specs += [pl.BlockSpec((FNET_GROUPS, MOE_TILE, gw), lambda i, *_: (0, i, 0))] * 2
        args += [shift, scale, gain, cs]
    grid_spec = pltpu.PrefetchScalarGridSpec(
        num_scalar_prefetch=1,
        grid=(t // MOE_TILE,),
        in_specs=in_specs,
        out_specs=out_specs,
        scratch_shapes=[pltpu.VMEM((2, MOE_CROWS, d), BF16), pltpu.SemaphoreType.DMA((2,))])
    out = pl.pallas_call(
        functools.partial(_combine_kernel, fnet=fnet is not None),
        out_shape=out_shape,
        grid_spec=grid_spec,
        compiler_params=_cparams(("arbitrary",)),
        name="moe_combine_fnet" if fnet is not None else "moe_combine",
    )(*args)
    return out if fnet is not None else out[0]


def _grouped_moe(h2, route, x1, gate2, wg, wu, wd, layer, tri):
    t = h2.shape[0]
    gid = route[ROUTE_GID_ROW].astype(jnp.int32)
    seg_tab, tail_tab, grp, valid = _moe_tables(gid, t)
    slot, hs = _dispatch(seg_tab, tail_tab, h2, route, tri)
    ys = _ffn(grp, valid, hs, wg, wu, wd, layer)
    return seg_tab, x1, slot, gate2, ys


def _fnet_channel_tile(x, shift_ref, scale_ref, gain_ref, cs_ref, y1_ref, y2_ref):
    h = _norm_modulate(x, gain_ref[...], shift_ref[0], scale_ref[0]).astype(BF16)
    gw = D_MODEL // FNET_GROUPS
    for g in range(FNET_GROUPS):
        y = jnp.dot(h[:, g * gw:(g + 1) * gw], cs_ref[...], preferred_element_type=F32)
        y1_ref[g] = y[:, :gw].astype(y1_ref.dtype)
        y2_ref[g] = y[:, gw:].astype(y2_ref.dtype)


def _fnet_chan_kernel(x_ref, shift_ref, scale_ref, gain_ref, cs_ref, y1_ref, y2_ref):
    _fnet_channel_tile(x_ref[...], shift_ref, scale_ref, gain_ref, cs_ref, y1_ref, y2_ref)


def _fnet_channel(x2d, shift, scale, gain, cs, tokens_per_batch, tm):
    t, d = x2d.shape
    tpb = tokens_per_batch // tm
    gw = d // FNET_GROUPS
    full = lambda i: (0, 0)
    out = jax.ShapeDtypeStruct((FNET_GROUPS, t, gw), BF16)
    out_spec = pl.BlockSpec((FNET_GROUPS, tm, gw), lambda i: (0, i, 0))
    return pl.pallas_call(
        _fnet_chan_kernel,
        out_shape=[out, out],
        grid=(t // tm,),
        in_specs=[pl.BlockSpec((tm, d), lambda i: (i, 0)),
                  pl.BlockSpec((1, 1, d), lambda i: (i // tpb, 0, 0)),
                  pl.BlockSpec((1, 1, d), lambda i: (i // tpb, 0, 0)),
                  pl.BlockSpec((1, d), full),
                  pl.BlockSpec(cs.shape, full)],
        out_specs=[out_spec, out_spec],
        compiler_params=_cparams(("arbitrary",)),
        name="fnet_channel",
    )(x2d, shift, scale, gain, cs)


FFT_J = SUBLANES


def _fnet_pos_kernel(y1_ref, y2_ref, lr_ref, ls_ref, cs_ref, sn_ref, o_ref, z1, z2, a_re, a_im, *, n):
    r1 = n // GRID_W
    gw = y1_ref.shape[2]
    nrj = FFT_J * r1
    z1[...] = y1_ref[0].astype(F32)
    z2[...] = y2_ref[0].astype(F32)

    def stage_r(sb, carry):
        s0 = pl.multiple_of(sb * FFT_J, FFT_J)
        rhs1 = jnp.concatenate([z1[pl.ds(GRID_W * r + s0, FFT_J), :] for r in range(r1)], axis=0).astype(BF16)
        rhs2 = jnp.concatenate([z2[pl.ds(GRID_W * r + s0, FFT_J), :] for r in range(r1)], axis=0).astype(BF16)
        p = jnp.dot(lr_ref[...], rhs1, preferred_element_type=F32)
        q = jnp.dot(lr_ref[...], rhs2, preferred_element_type=F32)
        re = p[:nrj] - q[nrj:]
        nim = q[:nrj] + p[nrj:]
        cs = jnp.concatenate([cs_ref[sb]] * (gw // LANES), axis=1)
        sn = jnp.concatenate([sn_ref[sb]] * (gw // LANES), axis=1)
        tre = re * cs - nim * sn
        tnim = re * sn + nim * cs
        for c in range(r1):
            a_re[pl.ds(GRID_W * c + s0, FFT_J), :] = tre[FFT_J * c:FFT_J * (c + 1)]
            a_im[pl.ds(GRID_W * c + s0, FFT_J), :] = tnim[FFT_J * c:FFT_J * (c + 1)]
        return carry
    lax.fori_loop(0, GRID_W // FFT_J, stage_r, 0)

    cblk = FFT_J * GRID_W

    def stage_s(cb, carry):
        c0 = pl.multiple_of(cb * cblk, cblk)
        blk = jnp.concatenate([a_re[pl.ds(c0, cblk), :], a_im[pl.ds(c0, cblk), :]], axis=0).astype(BF16)
        o = jnp.dot(ls_ref[...], blk, preferred_element_type=F32)
        k0 = pl.multiple_of(cb * FFT_J, FFT_J)
        for d in range(GRID_W):
            z1[pl.ds(r1 * d + k0, FFT_J), :] = o[FFT_J * d:FFT_J * (d + 1)]
        return carry
    lax.fori_loop(0, r1 // FFT_J, stage_s, 0)
    o_ref[0] = z1[...].astype(o_ref.dtype)


def _fnet_position(y1, y2, lr, ls, tw_cos, tw_sin, batch, n):
    groups, t, gw = y1.shape
    full2 = lambda b, g: (0, 0)
    full3 = lambda b, g: (0, 0, 0)
    scratch = pltpu.VMEM((n, gw), F32)
    return pl.pallas_call(
        functools.partial(_fnet_pos_kernel, n=n),
        out_shape=jax.ShapeDtypeStruct((groups, t, gw), BF16),
        grid=(batch, groups),
        in_specs=[pl.BlockSpec((1, n, gw), lambda b, g: (g, b, 0)),
                  pl.BlockSpec((1, n, gw), lambda b, g: (g, b, 0)),
                  pl.BlockSpec(lr.shape, full2),
                  pl.BlockSpec(ls.shape, full2),
                  pl.BlockSpec(tw_cos.shape, full3),
                  pl.BlockSpec(tw_sin.shape, full3)],
        out_specs=pl.BlockSpec((1, n, gw), lambda b, g: (g, b, 0)),
        scratch_shapes=[scratch, scratch, scratch, scratch],
        compiler_params=_cparams(("arbitrary", "arbitrary")),
        name="fnet_position",
    )(y1, y2, lr, ls, tw_cos, tw_sin)


def _dft_tables(n):
    assert n % (GRID_W * FFT_J) == 0
    gw = D_MODEL // FNET_GROUPS
    j = np.arange(gw)
    ang = 2.0 * np.pi * ((j[:, None] * j[None, :]) % gw) / gw
    cs = np.concatenate([np.cos(ang), np.sin(ang)], axis=1) / np.sqrt(gw)
    r1 = n // GRID_W
    eye = np.eye(FFT_J)
    scale = float(n) ** -0.25
    a = np.arange(r1)
    ang_r = 2.0 * np.pi * ((a[:, None] * a[None, :]) % r1) / r1
    lr = np.concatenate([np.kron(np.cos(ang_r), eye), np.kron(np.sin(ang_r), eye)], axis=0) * scale
    s = np.arange(GRID_W)
    ang_s = 2.0 * np.pi * ((s[:, None] * s[None, :]) % GRID_W) / GRID_W
    m_c = np.einsum("ds,cC->dcCs", np.cos(ang_s), eye).reshape(FFT_J * GRID_W, FFT_J * GRID_W)
    m_s = np.einsum("ds,cC->dcCs", np.sin(ang_s), eye).reshape(FFT_J * GRID_W, FFT_J * GRID_W)
    ls = np.concatenate([m_c, -m_s], axis=1) * scale
    sb = np.arange(GRID_W // FFT_J)
    s_of = sb[:, None, None] * FFT_J + np.arange(FFT_J)[None, None, :]
    ang_t = 2.0 * np.pi * ((s_of * a[None, :, None]) % n) / n
    ang_t = ang_t.reshape(len(sb), r1 * FFT_J, 1)
    tw_cos = jnp.broadcast_to(jnp.asarray(np.cos(ang_t), F32), (len(sb), r1 * FFT_J, LANES))
    tw_sin = jnp.broadcast_to(jnp.asarray(np.sin(ang_t), F32), (len(sb), r1 * FFT_J, LANES))
    return jnp.asarray(cs, BF16), jnp.asarray(lr, BF16), jnp.asarray(ls, BF16), tw_cos, tw_sin


def kernel(x, c, ctx, c_ctx, ada_w, ada_b, norm_mix, norm_ffn, mix_w_in, mix_w_out, na_q_norm, na_k_norm, na_rpb,
           lru_conv_w, lru_conv_b, lru_gate_r_w, lru_gate_r_b, lru_gate_i_w, lru_gate_i_b, lru_lambda,
           fnet_w_out, router_w, router_bias, moe_w_gate, moe_w_up, moe_w_down):
    batch, n, d = x.shape
    ctx_len = ctx.shape[1]
    depth = ada_w.shape[0]
    rows = n // GRID_W
    assert d == D_MODEL and n % (GRID_W * NA_QROWS) == 0 and rows >= 4 * NA_QROWS
    assert n % MOE_TILE == 0
    t = batch * n
    tm = 512
    tri = jnp.asarray(np.triu(np.ones((MOE_TILE, MOE_TILE))), BF16)

    r_pad = -(-(batch + 1) // SUBLANES) * SUBLANES
    c_rows = jnp.concatenate([c, c_ctx[None, :], jnp.zeros((r_pad - batch - 1, d), c.dtype)], axis=0)
    mod = _modulation(c_rows, ada_w, ada_b)

    def mod_slices(layer):
        m = mod[layer, :batch].reshape(batch, 1, 6, d)
        return [m[:, :, i, :] for i in range(6)]

    rw_cat = jnp.pad(router_w.astype(F32), ((0, 0), (0, 2 * LANES - N_EXPERTS))).astype(BF16)
    rbias = router_bias.reshape(N_EXPERTS, 1).astype(F32)
    x2d = x.reshape(t, d)
    ctx2d = ctx.reshape(batch * ctx_len, d)

    pending = None
    for layer in range(depth):
        li = layer // 2
        shift1, scale1, gate1, shift2, scale2, gate2 = mod_slices(layer)
        gain_mix = norm_mix[layer].reshape(1, d)
        gain_ffn = norm_ffn[layer].reshape(1, d)
        if layer % 2 == 0:
            if pending is not None:
                x2d = _combine(*pending, n)
            w_in = mix_w_in[li].astype(BF16)
            ind = jnp.asarray(np.kron(np.eye(NA_HEADS), np.ones((HEAD_DIM, HEAD_DIM))), BF16)
            qg = (jnp.tile(na_q_norm[li], NA_HEADS) * (HEAD_DIM ** -0.5 * LOG2_E)).reshape(1, NA_WIDTH).astype(F32)
            kg = jnp.tile(na_k_norm[li], NA_HEADS).reshape(1, NA_WIDTH).astype(F32)
            q, k, v, xb, gb = _inproj(x2d, shift1, scale1, gain_mix, w_in, ind, qg, kg,
                                      ("q", "k", "v", "x", "g"), n, 2 * tm)
            mctx = mod[layer, batch, :2 * d]
            shift_c = jnp.broadcast_to(mctx[:d], (batch, 1, d))
            scale_c = jnp.broadcast_to(mctx[d:], (batch, 1, d))
            k_c, v_c, xb_c = _inproj(ctx2d, shift_c, scale_c, gain_mix, w_in[:, NA_WIDTH:4 * NA_WIDTH], ind, qg, kg,
                                     ("k", "v", "x"), ctx_len, ctx_len)
            bias = _na_bias_tables(na_rpb[li], rows)
            attn = _attention(q, k, v, k_c, v_c, bias, batch, n, ctx_len)
            wcat, gbias = _lru_gate_weights(lru_gate_r_w[li], lru_gate_r_b[li], lru_gate_i_w[li], lru_gate_i_b[li])
            lru = _lru(xb, gb, xb_c, lru_conv_w[li].astype(F32), lru_conv_b[li].reshape(1, LRU_WIDTH).astype(F32),
                       wcat, gbias, lru_lambda[li].astype(F32), batch, n, ctx_len)
            parts, w_out = [attn, lru], mix_w_out[li].astype(BF16)
        else:
            cs, lr, ls, tw_cos, tw_sin = _dft_tables(n)
            if pending is not None:
                x2d, y1, y2 = _combine(*pending, n, fnet=(shift1, scale1, gain_mix, cs))
            else:
                y1, y2 = _fnet_channel(x2d, shift1, scale1, gain_mix, cs, n, tm)
            parts, w_out = [_fnet_position(y1, y2, lr, ls, tw_cos, tw_sin, batch, n)], fnet_w_out[li].astype(BF16)
        x1, h2, route = _post_mixer(parts, w_out, x2d, gate1, shift2, scale2, gain_ffn, rw_cat, rbias, n,
                                    2 * POST_SUBTILE)
        pending = _grouped_moe(h2, route, x1, gate2, moe_w_gate.astype(F32), moe_w_up.astype(F32),
                               moe_w_down.astype(F32), layer, tri)
    return _combine(*pending, n).reshape(batch, n, d)
```

```python
import functools

import numpy as np
import jax
import jax.numpy as jnp
from jax import lax
from jax.experimental import pallas as pl
from jax.experimental.pallas import tpu as pltpu

F32 = jnp.float32
BF16 = jnp.bfloat16
HIGHEST = lax.Precision.HIGHEST

D_MODEL = 1024
GRID_W = 64
HEAD_DIM = 64
NA_HEADS = 8
NA_WIDTH = NA_HEADS * HEAD_DIM
NA_WIN_ROWS = 8
NA_WIN_COLS = 16
LRU_WIDTH = 512
LRU_BLOCK = 64
LRU_C = 8.0
FNET_GROUPS = 4
N_EXPERTS = 16
EXPERTS_PER_GROUP = 4
N_EXPERT_GROUPS = 4
D_FF_EXPERT = 512
RMS_EPS = 1e-6
MASK_VALUE = -1e30
LOG2_E = 1.4426950408889634

V7X_VMEM_LIMIT_BYTES = 56 * 1024 * 1024
LANES = 128
SUBLANES = 8

NA_QROWS = 4
NA_KROWS = NA_QROWS + NA_WIN_ROWS - 1
NA_QBLK = NA_QROWS * GRID_W
NA_KBLK = NA_KROWS * GRID_W
NA_STEP_BLOCKS = 2

LRU_CHUNK = LANES
LRU_TROWS = 512

ROUTE_GID_ROW = EXPERTS_PER_GROUP
MOE_TILE = 512
ROW_ALIGN = 16
MOE_CROWS = MOE_TILE + N_EXPERT_GROUPS * ROW_ALIGN
MOE_SEG_BITS = (MOE_TILE // ROW_ALIGN).bit_length()
MOE_TAIL_BITS = (MOE_TILE // ROW_ALIGN - 1).bit_length()


def _sigmoid(x):
    return 1.0 / (1.0 + jnp.exp(-x))


def _sigmoid_tanh(x):
    return 0.5 + 0.5 * jnp.tanh(0.5 * x)


def _cparams(sem, vmem=V7X_VMEM_LIMIT_BYTES):
    return pltpu.CompilerParams(dimension_semantics=sem, vmem_limit_bytes=vmem)


def _mod_kernel(c_ref, w_ref, b_ref, o_ref):
    c = c_ref[...]
    s = c * _sigmoid(c)
    o_ref[0] = jnp.dot(s, w_ref[0], precision=HIGHEST, preferred_element_type=F32) + b_ref[0]


def _modulation(c_rows, ada_w, ada_b):
    depth, d, n6 = ada_w.shape
    r = c_rows.shape[0]
    tn = 1536
    return pl.pallas_call(
        _mod_kernel,
        out_shape=jax.ShapeDtypeStruct((depth, r, n6), F32),
        grid=(depth, n6 // tn),
        in_specs=[pl.BlockSpec((r, d), lambda l, j: (0, 0)),
                  pl.BlockSpec((1, d, tn), lambda l, j: (l, 0, j)),
                  pl.BlockSpec((1, 1, tn), lambda l, j: (l, 0, j))],
        out_specs=pl.BlockSpec((1, r, tn), lambda l, j: (l, 0, j)),
        compiler_params=_cparams(("arbitrary", "arbitrary")),
        name="adaln_mod",
    )(c_rows, ada_w, ada_b.reshape(depth, 1, n6))


def _norm_modulate(x, gain, shift, scale):
    ms = jnp.mean(x * x, axis=-1, keepdims=True)
    y = x * lax.rsqrt(ms + RMS_EPS) * gain
    return y * (1.0 + scale) + shift


def _inproj_kernel(x_ref, shift_ref, scale_ref, gain_ref, w_ref, ind_ref, qg_ref, kg_ref, *out_refs, segs):
    h = _norm_modulate(x_ref[...], gain_ref[...], shift_ref[0], scale_ref[0]).astype(BF16)
    for s, (kind, o_ref) in enumerate(zip(segs, out_refs)):
        z = jnp.dot(h, w_ref[:, s * NA_WIDTH:(s + 1) * NA_WIDTH], preferred_element_type=F32)
        if kind in ("q", "k"):
            zz = (z * z).astype(BF16)
            hw = ind_ref.shape[0]
            ms = jnp.concatenate([jnp.dot(zz[:, i * hw:(i + 1) * hw], ind_ref[...], preferred_element_type=F32)
                                  for i in range(NA_WIDTH // hw)], axis=1) * (1.0 / HEAD_DIM)
            g = qg_ref[...] if kind == "q" else kg_ref[...]
            z = z * lax.rsqrt(ms + RMS_EPS) * g
        if kind in ("x", "g"):
            for c in range(LRU_WIDTH // LRU_CHUNK):
                o_ref[c] = z[:, c * LRU_CHUNK:(c + 1) * LRU_CHUNK].astype(o_ref.dtype)
        else:
            o_ref[...] = z.astype(o_ref.dtype)


def _inproj(x2d, shift, scale, gain, w, ind, qg, kg, segs, tokens_per_batch, tm):
    t, d = x2d.shape
    tpb = tokens_per_batch // tm
    dt = {"q": BF16, "k": BF16, "v": BF16, "x": F32, "g": F32}
    full = lambda i: (0, 0)
    nch = LRU_WIDTH // LRU_CHUNK

    def out_shape(kind):
        shape = (nch, t, LRU_CHUNK) if kind in ("x", "g") else (t, NA_WIDTH)
        return jax.ShapeDtypeStruct(shape, dt[kind])

    def out_spec(kind):
        if kind in ("x", "g"):
            return pl.BlockSpec((nch, tm, LRU_CHUNK), lambda i: (0, i, 0))
        return pl.BlockSpec((tm, NA_WIDTH), lambda i: (i, 0))

    return pl.pallas_call(
        functools.partial(_inproj_kernel, segs=segs),
        out_shape=[out_shape(k) for k in segs],
        grid=(t // tm,),
        in_specs=[pl.BlockSpec((tm, d), lambda i: (i, 0)),
                  pl.BlockSpec((1, 1, d), lambda i: (i // tpb, 0, 0)),
                  pl.BlockSpec((1, 1, d), lambda i: (i // tpb, 0, 0)),
                  pl.BlockSpec((1, d), full),
                  pl.BlockSpec(w.shape, full),
                  pl.BlockSpec(ind.shape, full),
                  pl.BlockSpec((1, NA_WIDTH), full),
                  pl.BlockSpec((1, NA_WIDTH), full)],
        out_specs=[out_spec(k) for k in segs],
        compiler_params=_cparams(("arbitrary",)),
        name="inproj_" + "".join(segs),
    )(x2d, shift, scale, gain, w, ind, qg, kg)


def _na_bias_tables(rpb, rows):
    kr = NA_WIN_ROWS
    rb_count = rows // NA_QROWS
    cq = np.arange(GRID_W)
    ck = np.arange(GRID_W)
    col_start = np.clip(cq - NA_WIN_COLS // 2, 0, GRID_W - NA_WIN_COLS)
    valid_c = (ck[None, :] >= col_start[:, None]) & (ck[None, :] < col_start[:, None] + NA_WIN_COLS)
    dc = np.clip(ck[None, :] - cq[:, None], 1 - NA_WIN_COLS, NA_WIN_COLS - 1) + (NA_WIN_COLS - 1)
    n_dr, n_dc = 2 * NA_WIN_ROWS - 1, 2 * NA_WIN_COLS - 1
    sel_c = (dc[:, :, None] == np.arange(n_dc)) & valid_c[:, :, None]
    blocks = jnp.einsum("hrc,qkc->hrqk", rpb.astype(F32), jnp.asarray(sel_c, F32), precision=HIGHEST)
    blocks = blocks + jnp.asarray(np.where(valid_c, 0.0, MASK_VALUE), F32)
    blocks = jnp.concatenate([blocks, jnp.full((NA_HEADS, 1, GRID_W, GRID_W), MASK_VALUE, F32)], axis=1)
    blocks = blocks * LOG2_E
    which = []
    for rb in (0, 1, rb_count - 1):
        r = rb * NA_QROWS + np.arange(NA_QROWS)
        ks = int(np.clip(rb * NA_QROWS - kr // 2, 0, rows - NA_KROWS))
        key_r = ks + np.arange(NA_KROWS)
        row_start = np.clip(r - kr // 2, 0, rows - kr)
        valid_r = (key_r[None, :] >= row_start[:, None]) & (key_r[None, :] < row_start[:, None] + kr)
        dr = np.clip(key_r[None, :] - r[:, None] + (NA_WIN_ROWS - 1), 0, n_dr - 1)
        which.append(np.where(valid_r, dr, n_dr))
    return _na_bias_assemble(blocks, which)


def _na_bias_kernel(blk_ref, o_ref, *, which):
    for t, table in enumerate(which):
        @pl.when(pl.program_id(0) == t)
        def _():
            for i in range(NA_QROWS):
                row = jnp.concatenate([blk_ref[0, int(table[i, j])] for j in range(NA_KROWS)], axis=1)
                o_ref[0, 0, i * GRID_W:(i + 1) * GRID_W, :] = row


def _na_bias_assemble(blocks, which):
    heads, nblk = blocks.shape[:2]
    return pl.pallas_call(
        functools.partial(_na_bias_kernel, which=which),
        out_shape=jax.ShapeDtypeStruct((len(which), heads, NA_QBLK, NA_KBLK), F32),
        grid=(len(which), heads),
        in_specs=[pl.BlockSpec((1, nblk, GRID_W, GRID_W), lambda t, h: (h, 0, 0, 0))],
        out_specs=pl.BlockSpec((1, 1, NA_QBLK, NA_KBLK), lambda t, h: (t, h, 0, 0)),
        compiler_params=_cparams(("arbitrary", "arbitrary")),
        name="na_bias",
    )(blocks)


def _attn_kernel(q_ref, k_ref, v_ref, kc_ref, vc_ref, bias_ref, o_ref, *, rows):
    last = rows // NA_QROWS - 1
    nt = (((1,), (1,)), ((), ()))
    ctx_len = kc_ref.shape[0]
    low_half = lax.broadcasted_iota(jnp.int32, (NA_QBLK, LANES), 1) < HEAD_DIM
    for blk in range(NA_STEP_BLOCKS):
        rb = pl.program_id(1) * NA_STEP_BLOCKS + blk
        ks = jnp.clip(rb * NA_QROWS - NA_WIN_ROWS // 2, 0, rows - NA_KROWS)
        kstart = pl.multiple_of(ks * GRID_W, GRID_W)
        geom = jnp.where(rb == 0, 0, jnp.where(rb == last, 2, 1))
        qrows = slice(blk * NA_QBLK, (blk + 1) * NA_QBLK)
        for pair in range(NA_HEADS * HEAD_DIM // LANES):
            ls = slice(pair * LANES, (pair + 1) * LANES)
            q2 = q_ref[qrows, ls]
            k_all = jnp.concatenate([kc_ref[:, ls], k_ref[pl.ds(kstart, NA_KBLK), ls]], axis=0)
            v_all = jnp.concatenate([vc_ref[:, ls], v_ref[pl.ds(kstart, NA_KBLK), ls]], axis=0)
            outs = []
            for half in range(2):
                qh = jnp.where(low_half == (half == 0), q2, jnp.zeros_like(q2))
                s = lax.dot_general(qh, k_all, nt, preferred_element_type=F32)
                s = jnp.concatenate([s[:, :ctx_len], s[:, ctx_len:] + bias_ref[geom, 2 * pair + half]], axis=1)
                m = jnp.max(s, axis=-1, keepdims=True)
                p = jnp.exp2(s - m)
                l = jnp.sum(p, axis=-1, keepdims=True)
                outs.append(jnp.dot(p.astype(BF16), v_all, preferred_element_type=F32) / l)
            o_ref[qrows, ls] = jnp.where(low_half, outs[0], outs[1]).astype(o_ref.dtype)


def _attention(q, k, v, kc, vc, bias, batch, n, ctx_len):
    rows = n // GRID_W
    rbc = rows // (NA_QROWS * NA_STEP_BLOCKS)
    qblk = NA_QBLK * NA_STEP_BLOCKS
    return pl.pallas_call(
        functools.partial(_attn_kernel, rows=rows),
        out_shape=jax.ShapeDtypeStruct((batch * n, NA_WIDTH), BF16),
        grid=(batch, rbc),
        in_specs=[pl.BlockSpec((qblk, NA_WIDTH), lambda b, rb: (b * rbc + rb, 0)),
                  pl.BlockSpec((n, NA_WIDTH), lambda b, rb: (b, 0)),
                  pl.BlockSpec((n, NA_WIDTH), lambda b, rb: (b, 0)),
                  pl.BlockSpec((ctx_len, NA_WIDTH), lambda b, rb: (b, 0)),
                  pl.BlockSpec((ctx_len, NA_WIDTH), lambda b, rb: (b, 0)),
                  pl.BlockSpec(bias.shape, lambda b, rb: (0, 0, 0, 0), pipeline_mode=pl.Buffered(1))],
        out_specs=pl.BlockSpec((qblk, NA_WIDTH), lambda b, rb: (b * rbc + rb, 0)),
        compiler_params=_cparams(("arbitrary", "arbitrary")),
        name="na_attention",
    )(q, k, v, kc, vc, bias)


SCAN_GROUPS = 2
SCAN_CHUNKS = SCAN_GROUPS * SUBLANES


def _scan_pitch(n):
    p = -(-n // SCAN_CHUNKS)
    while p % 8 != 4:
        p += 1
    return p


NEG_LOG2_E = -LOG2_E


def _lru_coeff_tile(half_xc, zh, half_bias, k, d):
    c = LRU_CHUNK
    t_r = jnp.tanh(zh[:, (2 * d) * c:(2 * d + 1) * c] + half_bias[:, (2 * d) * c:(2 * d + 1) * c])
    t_i = jnp.tanh(zh[:, (2 * d + 1) * c:(2 * d + 2) * c] + half_bias[:, (2 * d + 1) * c:(2 * d + 2) * c])
    neg_log_a = k[d:d + 1, :] * (1.0 + t_r)
    a = jnp.exp2(neg_log_a * NEG_LOG2_E)
    one_minus_a2 = jnp.tanh(neg_log_a) * (a * a + 1.0)
    root = jnp.where(one_minus_a2 > 0.0, one_minus_a2 * lax.rsqrt(one_minus_a2), 0.0)
    return a, root * (half_xc + half_xc * t_i)


def _conv_tile(xpad, t0, w, b, rows):
    acc = b + w[0:1, :] * xpad[pl.ds(t0 + SUBLANES - 2, rows), :]
    acc = acc + w[1:2, :] * xpad[pl.ds(t0 + SUBLANES - 1, rows), :]
    acc = acc + w[2:3, :] * xpad[pl.ds(t0 + SUBLANES, rows), :]
    return acc + w[3:4, :] * xpad[pl.ds(t0 + SUBLANES + 1, rows), :]


SCAN_UNROLL = 4


def _group_rows(j, g, pitch):
    return pl.ds(g * SUBLANES * pitch + j, SUBLANES, stride=pitch)


def _chunk_totals(af_ref, bf_ref, ab_ref, bb_ref, pitch):
    def body(j, carry):
        jb = pitch - 1 - j
        out = []
        for g in range(SCAN_GROUPS):
            pf, hf, pb, hb = carry[4 * g:4 * g + 4]
            af = af_ref[_group_rows(j, g, pitch), :]
            ab = ab_ref[_group_rows(jb, g, pitch), :]
            out += [af * pf, af * hf + bf_ref[_group_rows(j, g, pitch), :],
                    ab * pb, ab * hb + bb_ref[_group_rows(jb, g, pitch), :]]
        return tuple(out)
    one = jnp.ones((SUBLANES, LRU_CHUNK), F32)
    zero = jnp.zeros((SUBLANES, LRU_CHUNK), F32)
    res = lax.fori_loop(0, pitch, body, (one, zero, one, zero) * SCAN_GROUPS, unroll=SCAN_UNROLL)
    fwd = [(res[4 * g], res[4 * g + 1]) for g in range(SCAN_GROUPS)]
    bwd = [(res[4 * g + 2], res[4 * g + 3]) for g in range(SCAN_GROUPS)]
    return fwd, bwd


def _chunk_starts(totals, h0, reverse):
    row = lax.broadcasted_iota(jnp.int32, (SUBLANES, LRU_CHUNK), 0)
    starts = [jnp.zeros((SUBLANES, LRU_CHUNK), F32) for _ in range(SCAN_GROUPS)]
    state = h0
    order = range(SCAN_CHUNKS - 1, -1, -1) if reverse else range(SCAN_CHUNKS)
    for c in order:
        g, s = divmod(c, SUBLANES)
        p_end, h_end = totals[g]
        starts[g] = jnp.where(row == s, state, starts[g])
        state = p_end[s:s + 1, :] * state + h_end[s:s + 1, :]
    return starts, state


def _scan_write(af_ref, bf_ref, hf_ref, ab_ref, bb_ref, hb_ref, starts_f, starts_b, pitch):
    def body(j, carry):
        jb = pitch - 1 - j
        out = []
        for g in range(SCAN_GROUPS):
            hf, hb = carry[2 * g:2 * g + 2]
            hf = af_ref[_group_rows(j, g, pitch), :] * hf + bf_ref[_group_rows(j, g, pitch), :]
            hb = ab_ref[_group_rows(jb, g, pitch), :] * hb + bb_ref[_group_rows(jb, g, pitch), :]
            hf_ref[_group_rows(j, g, pitch), :] = hf
            hb_ref[_group_rows(jb, g, pitch), :] = hb
            out += [hf, hb]
        return tuple(out)
    init = tuple(v for g in range(SCAN_GROUPS) for v in (starts_f[g], starts_b[g]))
    lax.fori_loop(0, pitch, body, init, unroll=SCAN_UNROLL)


def _lru_kernel(x_ref, g_ref, xc_ref, cw_ref, cb_ref, w_ref, gb_ref, lam_ref, o_ref,
                xpad, a0, b0, a1, b1, h0s, h1s, ca0, cb0, ca1, cb1, *, n, ctx_len):
    pitch = _scan_pitch(n)
    cpitch = _scan_pitch(ctx_len)
    cw = cw_ref[...]
    cb = cb_ref[...]
    gbias = gb_ref[0]
    lam = lam_ref[...]
    sp = jnp.maximum(-lam, 0.0) + jnp.log1p(jnp.exp(-jnp.abs(lam)))
    k = (0.5 * LRU_C) * sp
    wcat = w_ref[0]
    zeros8 = jnp.zeros((SUBLANES, LRU_CHUNK), F32)

    def fill_coeffs(src_rows, total, length, trows, a_refs, b_refs):
        for d in range(2):
            a_refs[d][pl.ds(length, total - length), :] = jnp.ones((total - length, LRU_CHUNK), F32)
            b_refs[d][pl.ds(length, total - length), :] = jnp.zeros((total - length, LRU_CHUNK), F32)
        xpad[pl.ds(0, SUBLANES), :] = zeros8
        xpad[pl.ds(SUBLANES + length, SUBLANES), :] = zeros8
        xpad[pl.ds(SUBLANES, length), :] = src_rows

        def tile(t, carry):
            t0 = pl.multiple_of(t * trows, SUBLANES)
            xc = _conv_tile(xpad, t0, cw, cb, trows)
            zh = jnp.dot(xc.astype(BF16), wcat, preferred_element_type=F32)
            half_xc = 0.5 * xc
            for d in range(2):
                a, b = _lru_coeff_tile(half_xc, zh, gbias, k, d)
                a_refs[d][pl.ds(t0, trows), :] = a
                b_refs[d][pl.ds(t0, trows), :] = b
            return carry
        lax.fori_loop(0, length // trows, tile, 0)

    fill_coeffs(xc_ref[0], SCAN_CHUNKS * cpitch, ctx_len, ctx_len, (ca0, ca1), (cb0, cb1))
    zero_state = jnp.zeros((1, LRU_CHUNK), F32)
    fwd, bwd = _chunk_totals(ca0, cb0, ca1, cb1, cpitch)
    _, init_f = _chunk_starts(fwd, zero_state, reverse=False)
    _, init_b = _chunk_starts(bwd, zero_state, reverse=True)

    fill_coeffs(x_ref[0], SCAN_CHUNKS * pitch, n, LRU_TROWS, (a0, a1), (b0, b1))
    fwd, bwd = _chunk_totals(a0, b0, a1, b1, pitch)
    starts_f, _ = _chunk_starts(fwd, init_f, reverse=False)
    starts_b, _ = _chunk_starts(bwd, init_b, reverse=True)
    _scan_write(a0, b0, h0s, a1, b1, h1s, starts_f, starts_b, pitch)

    def out_tile(t, carry):
        t0 = pl.multiple_of(t * LRU_TROWS, SUBLANES)
        y = h0s[pl.ds(t0, LRU_TROWS), :] + h1s[pl.ds(t0, LRU_TROWS), :]
        g = g_ref[0, pl.ds(t0, LRU_TROWS), :]
        gelu = 0.5 * g * (1.0 + jnp.tanh(0.7978845608028654 * (g + 0.044715 * (g * g * g))))
        o_ref[0, pl.ds(t0, LRU_TROWS), :] = (gelu * y).astype(o_ref.dtype)
        return carry
    lax.fori_loop(0, n // LRU_TROWS, out_tile, 0)


def _lru(xb, gb, xb_ctx, conv_w, conv_b, wcat, gbias, lam, batch, n, ctx_len):
    nch = LRU_WIDTH // LRU_CHUNK
    pitch = _scan_pitch(n)
    cpitch = _scan_pitch(ctx_len)
    big = pltpu.VMEM((SCAN_CHUNKS * pitch, LRU_CHUNK), F32)
    small = pltpu.VMEM((SCAN_CHUNKS * cpitch, LRU_CHUNK), F32)
    return pl.pallas_call(
        functools.partial(_lru_kernel, n=n, ctx_len=ctx_len),
        out_shape=jax.ShapeDtypeStruct((nch, batch * n, LRU_CHUNK), BF16),
        grid=(batch, nch),
        in_specs=[pl.BlockSpec((1, n, LRU_CHUNK), lambda b, c: (c, b, 0)),
                  pl.BlockSpec((1, n, LRU_CHUNK), lambda b, c: (c, b, 0)),
                  pl.BlockSpec((1, ctx_len, LRU_CHUNK), lambda b, c: (c, b, 0)),
                  pl.BlockSpec((4, LRU_CHUNK), lambda b, c: (0, c)),
                  pl.BlockSpec((1, LRU_CHUNK), lambda b, c: (0, c)),
                  pl.BlockSpec((1, LRU_CHUNK, 4 * LRU_CHUNK), lambda b, c: (c, 0, 0)),
                  pl.BlockSpec((1, 1, 4 * LRU_CHUNK), lambda b, c: (c, 0, 0)),
                  pl.BlockSpec((2, LRU_CHUNK), lambda b, c: (0, c))],
        out_specs=pl.BlockSpec((1, n, LRU_CHUNK), lambda b, c: (c, b, 0)),
        scratch_shapes=[pltpu.VMEM((n + 2 * SUBLANES, LRU_CHUNK), F32),
                        big, big, big, big, big, big, small, small, small, small],
        compiler_params=_cparams(("arbitrary", "arbitrary")),
        name="rglru",
    )(xb, gb, xb_ctx, conv_w, conv_b, wcat, gbias, lam)


def _lru_gate_weights(w_r, b_r, w_i, b_i):
    nch = LRU_WIDTH // LRU_CHUNK
    bpc = LRU_CHUNK // LRU_BLOCK

    def dense(w):
        wc = w.reshape(nch, bpc, LRU_BLOCK, LRU_BLOCK)
        eye = jnp.eye(bpc, dtype=w.dtype)
        return jnp.einsum("cbij,bd->cbidj", wc, eye).reshape(nch, LRU_CHUNK, LRU_CHUNK)

    wcat = jnp.concatenate([dense(w_r[0]), dense(w_i[0]), dense(w_r[1]), dense(w_i[1])], axis=-1)
    chunk = lambda v: v.reshape(nch, 1, LRU_CHUNK)
    gbias = jnp.concatenate([chunk(b_r[0]), chunk(b_i[0]), chunk(b_r[1]), chunk(b_i[1])], axis=-1)
    return (0.5 * wcat).astype(BF16), (0.5 * gbias).astype(F32)


def _route(s, sel, route_ref):
    srow = [s[e:e + 1, :] for e in range(N_EXPERTS)]
    lrow = [sel[e:e + 1, :] for e in range(N_EXPERTS)]
    gscore = []
    for g in range(N_EXPERT_GROUPS):
        a = lrow[g * EXPERTS_PER_GROUP:(g + 1) * EXPERTS_PER_GROUP]
        best = a[0] + a[1]
        for i, j in ((0, 2), (0, 3), (1, 2), (1, 3), (2, 3)):
            best = jnp.maximum(best, a[i] + a[j])
        gscore.append(best)
    bg = jnp.zeros_like(gscore[0], dtype=jnp.int32)
    bv = gscore[0]
    for g in range(1, N_EXPERT_GROUPS):
        upd = gscore[g] > bv
        bg = jnp.where(upd, g, bg)
        bv = jnp.where(upd, gscore[g], bv)

    def pick(rows_):
        out = []
        for j in range(EXPERTS_PER_GROUP):
            v = rows_[j]
            for g in range(1, N_EXPERT_GROUPS):
                v = jnp.where(bg == g, rows_[g * EXPERTS_PER_GROUP + j], v)
            out.append(v)
        return out
    cand = pick(lrow)
    cs = pick(srow)
    i1 = jnp.zeros_like(bg)
    v1 = cand[0]
    w1 = cs[0]
    for j in range(1, EXPERTS_PER_GROUP):
        upd = cand[j] > v1
        i1 = jnp.where(upd, j, i1)
        v1 = jnp.where(upd, cand[j], v1)
        w1 = jnp.where(upd, cs[j], w1)
    i2 = jnp.full_like(bg, -1)
    v2 = jnp.full_like(v1, -jnp.inf)
    w2 = jnp.zeros_like(w1)
    for j in range(EXPERTS_PER_GROUP):
        upd = (i1 != j) & (cand[j] > v2)
        i2 = jnp.where(upd, j, i2)
        v2 = jnp.where(upd, cand[j], v2)
        w2 = jnp.where(upd, cs[j], w2)
    den = w1 + w2
    g1 = w1 / den
    g2 = w2 / den
    for j in range(EXPERTS_PER_GROUP):
        route_ref[j:j + 1, :] = jnp.where(i1 == j, g1, 0.0) + jnp.where(i2 == j, g2, 0.0)
    route_ref[ROUTE_GID_ROW:ROUTE_GID_ROW + 1, :] = bg.astype(F32)
    pad = SUBLANES - ROUTE_GID_ROW - 1
    route_ref[ROUTE_GID_ROW + 1:, :] = jnp.zeros((pad, bg.shape[1]), F32)


POST_SUBTILE = 512


def _post_kernel(*refs, n_parts):
    parts = refs[:n_parts]
    (w_ref, x_ref, gate_ref, shift_ref, scale_ref, gain_ref, rw_ref, rb_ref,
     x1_ref, h2_ref, route_ref) = refs[n_parts:]
    for sub in range(x_ref.shape[0] // POST_SUBTILE):
        rows = pl.ds(sub * POST_SUBTILE, POST_SUBTILE)
        pieces = []
        for p in parts:
            pieces += [p[c, rows, :] for c in range(p.shape[0])] if len(p.shape) == 3 else [p[rows, :]]
        mixed = jnp.concatenate(pieces, axis=-1) if len(pieces) > 1 else pieces[0]
        mix = jnp.dot(mixed, w_ref[...], preferred_element_type=F32)
        x1 = x_ref[rows, :] + gate_ref[0] * mix
        x1_ref[rows, :] = x1
        h2 = _norm_modulate(x1, gain_ref[...], shift_ref[0], scale_ref[0])
        h_hi = h2.astype(BF16)
        h2_ref[rows, :] = h_hi
        logits = jnp.dot(h_hi, rw_ref[...], preferred_element_type=F32)
        s = _sigmoid(logits[:, :LANES].T[:N_EXPERTS, :])
        _route(s, s + rb_ref[...], route_ref.at[:, rows])


def _post_mixer(parts, w, x2d, gate1, shift2, scale2, gain, rw_cat, rbias, tokens_per_batch, tm):
    t, d = x2d.shape
    tpb = tokens_per_batch // tm
    full = lambda i: (0, 0)
    per_b = lambda i: (i // tpb, 0, 0)

    def part_spec(p):
        if p.ndim == 3:
            return pl.BlockSpec((p.shape[0], tm, p.shape[2]), lambda i: (0, i, 0))
        return pl.BlockSpec((tm, p.shape[1]), lambda i: (i, 0))

    return pl.pallas_call(
        functools.partial(_post_kernel, n_parts=len(parts)),
        out_shape=[jax.ShapeDtypeStruct((t, d), F32), jax.ShapeDtypeStruct((t, d), BF16),
                   jax.ShapeDtypeStruct((SUBLANES, t), F32)],
        grid=(t // tm,),
        in_specs=[part_spec(p) for p in parts] + [
                  pl.BlockSpec(w.shape, full),
                  pl.BlockSpec((tm, d), lambda i: (i, 0)),
                  pl.BlockSpec((1, 1, d), per_b),
                  pl.BlockSpec((1, 1, d), per_b),
                  pl.BlockSpec((1, 1, d), per_b),
                  pl.BlockSpec((1, d), full),
                  pl.BlockSpec(rw_cat.shape, full),
                  pl.BlockSpec(rbias.shape, full)],
        out_specs=[pl.BlockSpec((tm, d), lambda i: (i, 0)),
                   pl.BlockSpec((tm, d), lambda i: (i, 0)),
                   pl.BlockSpec((SUBLANES, tm), lambda i: (0, i))],
        compiler_params=_cparams(("arbitrary",)),
        name="post_mixer",
    )(*parts, w, x2d, gate1, shift2, scale2, gain, rw_cat, rbias)


def _moe_layout(t):
    nt = t // MOE_TILE
    grid = -(-(t + N_EXPERT_GROUPS * (ROW_ALIGN - 1) * nt) // MOE_TILE) + N_EXPERT_GROUPS
    return nt, grid


def _moe_tables(gid, t):
    nt, grid = _moe_layout(t)
    ng = N_EXPERT_GROUPS
    per_tile = MOE_TILE // ROW_ALIGN
    onehot = (gid.reshape(nt, MOE_TILE, 1) == jnp.arange(ng, dtype=jnp.int32)).astype(jnp.int32)
    cnt = onehot.sum(axis=1)
    seg = (cnt + ROW_ALIGN - 1) // ROW_ALIGN
    src = jnp.cumsum(seg, axis=1) - seg
    fill = seg.sum(axis=0)
    ntile = (fill + per_tile - 1) // per_tile
    cum = jnp.cumsum(ntile)
    base = (cum - ntile) * per_tile
    dst = jnp.cumsum(seg, axis=0) - seg + base[None, :]
    seg_tab = jnp.concatenate([seg, src, dst], axis=1).reshape(-1).astype(jnp.int32)
    tail = (-fill) % per_tile
    tail_tab = jnp.concatenate([tail, fill + base, cum[-1:]]).astype(jnp.int32)
    i = jnp.arange(grid, dtype=jnp.int32)
    valid = i < cum[-1]
    ie = jnp.minimum(i, cum[-1] - 1)
    g_of = jnp.sum((ie[:, None] >= cum[None, :]).astype(jnp.int32), axis=1)
    return seg_tab, tail_tab, g_of.astype(jnp.int32), valid.astype(jnp.int32)


def _segment_copies(tab_ref, tile, enable, make_copy):
    ng = N_EXPERT_GROUPS
    base = jnp.maximum(tile, 0) * (3 * ng)
    out = []
    for g in range(ng):
        n = tab_ref[base + g]
        src = tab_ref[base + ng + g]
        dst = tab_ref[base + 2 * ng + g]
        for k in range(MOE_SEG_BITS - 1, -1, -1):
            done = (n >> (k + 1)) << (k + 1)
            rows = ROW_ALIGN << k
            s0 = pl.multiple_of((src + done) * ROW_ALIGN, ROW_ALIGN)
            d0 = pl.multiple_of((dst + done) * ROW_ALIGN, ROW_ALIGN)
            out.append((enable & (((n >> k) & 1) == 1), make_copy(s0, d0, rows)))
    return out


def _start_copies(pairs):
    for cond, copies in pairs:
        @pl.when(cond)
        def _():
            for c in copies:
                c.start()


def _wait_copies(pairs):
    for cond, copies in pairs:
        @pl.when(cond)
        def _():
            for c in copies:
                c.wait()


def _split_bf16x3(x):
    hi = x.astype(BF16).astype(F32)
    r1 = x - hi
    mid = r1.astype(BF16).astype(F32)
    lo = (r1 - mid).astype(BF16).astype(F32)
    return hi, mid, lo


def _dispatch_kernel(seg_ref, tail_ref, h_ref, route_ref, tri_ref, slot_ref, hs_ref, cbuf, zbuf, sem, *, nt):
    i = pl.program_id(0)
    tm, d = h_ref.shape
    ng = N_EXPERT_GROUPS
    cur = i % 2

    def seg_copies(tile, enable, buf):
        def seg_copy(s0, d0, rows):
            return (pltpu.make_async_copy(cbuf.at[buf, pl.ds(s0, rows)], hs_ref.at[pl.ds(d0, rows)], sem.at[buf]),)
        return _segment_copies(seg_ref, tile, enable, seg_copy)

    _wait_copies(seg_copies(i - 2, i >= 2, cur))

    route = route_ref[...]
    gid = route[ROUTE_GID_ROW:ROUTE_GID_ROW + 1, :]
    grp = lax.broadcasted_iota(jnp.int32, (SUBLANES, tm), 0).astype(F32)
    onehot = jnp.where(grp == gid, 1.0, 0.0)
    rank = jnp.dot(onehot.astype(BF16), tri_ref[...], preferred_element_type=F32)
    slot = jnp.zeros((1, tm), F32)
    for g in range(ng):
        start = (seg_ref[i * 3 * ng + ng + g] * ROW_ALIGN).astype(F32)
        slot = slot + onehot[g:g + 1, :] * (rank[g:g + 1, :] - 1.0 + start)
    slot_ref[...] = jnp.broadcast_to(slot, (SUBLANES, tm))
    perm = jnp.where(lax.broadcasted_iota(jnp.int32, (MOE_CROWS, tm), 0).astype(F32) == slot, 1.0, 0.0)
    perm = perm.astype(BF16)
    cbuf[cur, :, :d] = jnp.dot(perm, h_ref[...], preferred_element_type=F32).astype(cbuf.dtype)
    parts = jnp.concatenate(list(_split_bf16x3(route)) + [jnp.zeros((LANES - 3 * SUBLANES, tm), F32)], axis=0)
    record = lax.dot_general(perm, parts.astype(BF16), (((1,), (1,)), ((), ())), preferred_element_type=F32)
    cbuf[cur, :, d:] = record.astype(cbuf.dtype)
    _start_copies(seg_copies(i, i >= 0, cur))

    @pl.when(i == pl.num_programs(0) - 1)
    def _():
        _wait_copies(seg_copies(i - 1, i >= 1, 1 - cur))
        _wait_copies(seg_copies(i, i >= 0, cur))
        zbuf[...] = jnp.zeros(zbuf.shape, zbuf.dtype)

        def zero_copy(d0, rows):
            return (pltpu.make_async_copy(zbuf.at[pl.ds(0, rows)], hs_ref.at[pl.ds(d0, rows)], sem.at[0]),)
        pairs = []
        for g in range(ng):
            n = tail_ref[g]
            dst = tail_ref[ng + g]
            for k in range(MOE_TAIL_BITS - 1, -1, -1):
                done = (n >> (k + 1)) << (k + 1)
                d0 = pl.multiple_of((dst + done) * ROW_ALIGN, ROW_ALIGN)
                pairs.append((((n >> k) & 1) == 1, zero_copy(d0, ROW_ALIGN << k)))
        used = tail_ref[2 * ng]
        total = hs_ref.shape[0] // MOE_TILE
        for j in range(total - nt):
            d0 = pl.multiple_of(jnp.minimum(used + j, total - 1) * MOE_TILE, MOE_TILE)
            pairs.append((used + j < total, zero_copy(d0, MOE_TILE)))
        _start_copies(pairs)
        _wait_copies(pairs)


def _dispatch(seg_tab, tail_tab, h2, route, tri):
    t, d = h2.shape
    nt, grid = _moe_layout(t)
    rows = grid * MOE_TILE
    grid_spec = pltpu.PrefetchScalarGridSpec(
        num_scalar_prefetch=2,
        grid=(nt,),
        in_specs=[pl.BlockSpec((MOE_TILE, d), lambda i, *_: (i, 0)),
                  pl.BlockSpec((SUBLANES, MOE_TILE), lambda i, *_: (0, i)),
                  pl.BlockSpec((MOE_TILE, MOE_TILE), lambda i, *_: (0, 0))],
        out_specs=[pl.BlockSpec((SUBLANES, MOE_TILE), lambda i, *_: (0, i)),
                   pl.BlockSpec(memory_space=pl.ANY)],
        scratch_shapes=[pltpu.VMEM((2, MOE_CROWS, d + LANES), BF16), pltpu.VMEM((MOE_TILE, d + LANES), BF16),
                        pltpu.SemaphoreType.DMA((2,))])
    return pl.pallas_call(
        functools.partial(_dispatch_kernel, nt=nt),
        out_shape=[jax.ShapeDtypeStruct((SUBLANES, t), F32),
                   jax.ShapeDtypeStruct((rows, d + LANES), BF16)],
        grid_spec=grid_spec,
        compiler_params=_cparams(("arbitrary",)),
        name="moe_dispatch",
    )(seg_tab, tail_tab, h2, route, tri)


def _ffn_kernel(grp_ref, valid_ref, h_ref, wg32_ref, wu32_ref, wd32_ref, y_ref, wg_ref, wu_ref, wd_ref):
    i = pl.program_id(0)
    d = y_ref.shape[1]

    @pl.when((i == 0) | (grp_ref[i] != grp_ref[jnp.maximum(i - 1, 0)]))
    def _():
        for j in range(EXPERTS_PER_GROUP):
            cols = slice(j * D_FF_EXPERT, (j + 1) * D_FF_EXPERT)
            wg_ref[:, cols] = wg32_ref[j].astype(BF16)
            wu_ref[:, cols] = wu32_ref[j].astype(BF16)
            wd_ref[cols, :] = wd32_ref[j].astype(BF16)

    @pl.when(valid_ref[i] == 0)
    def _():
        y_ref[...] = jnp.zeros(y_ref.shape, y_ref.dtype)

    @pl.when(valid_ref[i] == 1)
    def _():
        h = h_ref[:, :d]
        gates = h_ref[:, d:].astype(F32)
        acts = []
        for j in range(EXPERTS_PER_GROUP):
            cols = slice(j * D_FF_EXPERT, (j + 1) * D_FF_EXPERT)
            a = jnp.dot(h, wg_ref[:, cols], preferred_element_type=F32)
            u = jnp.dot(h, wu_ref[:, cols], preferred_element_type=F32)
            gate = (gates[:, j:j + 1] + gates[:, SUBLANES + j:SUBLANES + j + 1]
                    + gates[:, 2 * SUBLANES + j:2 * SUBLANES + j + 1])
            acts.append(((a * _sigmoid_tanh(a)) * u * gate).astype(BF16))
        y = jnp.dot(jnp.concatenate(acts, axis=1), wd_ref[...], preferred_element_type=F32)
        y_ref[...] = y.astype(y_ref.dtype)


def _ffn(grp, valid, hs, wg, wu, wd, layer):
    rows, width = hs.shape
    d = width - LANES
    epg = EXPERTS_PER_GROUP
    once = pl.Buffered(1)
    w_idx = lambda i, grp, valid: (layer, grp[i], 0, 0)
    grid_spec = pltpu.PrefetchScalarGridSpec(
        num_scalar_prefetch=2,
        grid=(rows // MOE_TILE,),
        in_specs=[pl.BlockSpec((MOE_TILE, width), lambda i, grp, valid: (i, 0)),
                  pl.BlockSpec((None, epg, d, D_FF_EXPERT), w_idx, pipeline_mode=once),
                  pl.BlockSpec((None, epg, d, D_FF_EXPERT), w_idx, pipeline_mode=once),
                  pl.BlockSpec((None, epg, D_FF_EXPERT, d), w_idx, pipeline_mode=once)],
        out_specs=pl.BlockSpec((MOE_TILE, d), lambda i, grp, valid: (i, 0)),
        scratch_shapes=[pltpu.VMEM((d, epg * D_FF_EXPERT), BF16), pltpu.VMEM((d, epg * D_FF_EXPERT), BF16),
                        pltpu.VMEM((epg * D_FF_EXPERT, d), BF16)])
    return pl.pallas_call(
        _ffn_kernel,
        out_shape=jax.ShapeDtypeStruct((rows, d), BF16),
        grid_spec=grid_spec,
        compiler_params=_cparams(("arbitrary",)),
        name="moe_ffn",
    )(grp, valid, hs, wg, wu, wd)


def _combine_kernel(seg_ref, x1_ref, slot_ref, gate2_ref, ys_ref, *rest, fnet):
    if fnet:
        shift_ref, scale_ref, gain_ref, cs_ref, o_ref, y1_ref, y2_ref, ybuf, sem = rest
    else:
        o_ref, ybuf, sem = rest
    i = pl.program_id(0)
    nt = pl.num_programs(0)
    tm = x1_ref.shape[0]
    cur = i % 2

    def seg_copies(tile, enable, buf):
        def seg_copy(s0, d0, rows):
            return (pltpu.make_async_copy(ys_ref.at[pl.ds(d0, rows)], ybuf.at[buf, pl.ds(s0, rows)], sem.at[buf]),)
        return _segment_copies(seg_ref, tile, enable, seg_copy)

    @pl.when(i == 0)
    def _():
        ybuf[...] = jnp.zeros(ybuf.shape, ybuf.dtype)
        _start_copies(seg_copies(i, i == 0, cur))

    nxt = jnp.minimum(i + 1, nt - 1)
    _start_copies(seg_copies(nxt, i + 1 < nt, 1 - cur))
    _wait_copies(seg_copies(i, i >= 0, cur))
    slot = slot_ref[0:1, :]
    perm = jnp.where(lax.broadcasted_iota(jnp.int32, (MOE_CROWS, tm), 0).astype(F32) == slot, 1.0, 0.0)
    y = lax.dot_general(perm.astype(BF16), ybuf[cur], (((0,), (0,)), ((), ())), preferred_element_type=F32)
    x = x1_ref[...] + gate2_ref[0] * y
    o_ref[...] = x
    if fnet:
        _fnet_channel_tile(x, shift_ref, scale_ref, gain_ref, cs_ref, y1_ref, y2_ref)


def _combine(seg_tab, x1, slot, gate2, ys, tokens_per_batch, fnet=None):
    t, d = x1.shape
    tpb = tokens_per_batch // MOE_TILE
    per_b = lambda i, *_: (i // tpb, 0, 0)
    full = lambda i, *_: (0, 0)
    in_specs = [pl.BlockSpec((MOE_TILE, d), lambda i, *_: (i, 0)),
                pl.BlockSpec((SUBLANES, MOE_TILE), lambda i, *_: (0, i)),
                pl.BlockSpec((1, 1, d), per_b),
                pl.BlockSpec(memory_space=pl.ANY)]
    out_shape = [jax.ShapeDtypeStruct((t, d), F32)]
    out_specs = [pl.BlockSpec((MOE_TILE, d), lambda i, *_: (i, 0))]
    args = [seg_tab, x1, slot, gate2, ys]
    if fnet is not None:
        shift, scale, gain, cs = fnet
        gw = d // FNET_GROUPS
        in_specs += [pl.BlockSpec((1, 1, d), per_b), pl.BlockSpec((1, 1, d), per_b), pl.BlockSpec((1, d), full),
                     pl.BlockSpec(cs.shape, full)]
        out_shape += [jax.ShapeDtypeStruct((FNET_GROUPS, t, gw), BF16)] * 2
        out_specs += [pl.BlockSpec((FNET_GROUPS, MOE_TILE, gw), lambda i, *_: (0, i, 0))] * 2
        args += [shift, scale, gain, cs]
    grid_spec = pltpu.PrefetchScalarGridSpec(
        num_scalar_prefetch=1,
        grid=(t // MOE_TILE,),
        in_specs=in_specs,
        out_specs=out_specs,
        scratch_shapes=[pltpu.VMEM((2, MOE_CROWS, d), BF16), pltpu.SemaphoreType.DMA((2,))])
    out = pl.pallas_call(
        functools.partial(_combine_kernel, fnet=fnet is not None),
        out_shape=out_shape,
        grid_spec=grid_spec,
        compiler_params=_cparams(("arbitrary",)),
        name="moe_combine_fnet" if fnet is not None else "moe_combine",
    )(*args)
    return out if fnet is not None else out[0]


def _grouped_moe(h2, route, x1, gate2, wg, wu, wd, layer, tri):
    t = h2.shape[0]
    gid = route[ROUTE_GID_ROW].astype(jnp.int32)
    seg_tab, tail_tab, grp, valid = _moe_tables(gid, t)
    slot, hs = _dispatch(seg_tab, tail_tab, h2, route, tri)
    ys = _ffn(grp, valid, hs, wg, wu, wd, layer)
    return seg_tab, x1, slot, gate2, ys


def _fnet_channel_tile(x, shift_ref, scale_ref, gain_ref, cs_ref, y1_ref, y2_ref):
    h = _norm_modulate(x, gain_ref[...], shift_ref[0], scale_ref[0]).astype(BF16)
    gw = D_MODEL // FNET_GROUPS
    for g in range(FNET_GROUPS):
        y = jnp.dot(h[:, g * gw:(g + 1) * gw], cs_ref[...], preferred_element_type=F32)
        y1_ref[g] = y[:, :gw].astype(y1_ref.dtype)
        y2_ref[g] = y[:, gw:].astype(y2_ref.dtype)


def _fnet_chan_kernel(x_ref, shift_ref, scale_ref, gain_ref, cs_ref, y1_ref, y2_ref):
    _fnet_channel_tile(x_ref[...], shift_ref, scale_ref, gain_ref, cs_ref, y1_ref, y2_ref)


def _fnet_channel(x2d, shift, scale, gain, cs, tokens_per_batch, tm):
    t, d = x2d.shape
    tpb = tokens_per_batch // tm
    gw = d // FNET_GROUPS
    full = lambda i: (0, 0)
    out = jax.ShapeDtypeStruct((FNET_GROUPS, t, gw), BF16)
    out_spec = pl.BlockSpec((FNET_GROUPS, tm, gw), lambda i: (0, i, 0))
    return pl.pallas_call(
        _fnet_chan_kernel,
        out_shape=[out, out],
        grid=(t // tm,),
        in_specs=[pl.BlockSpec((tm, d), lambda i: (i, 0)),
                  pl.BlockSpec((1, 1, d), lambda i: (i // tpb, 0, 0)),
                  pl.BlockSpec((1, 1, d), lambda i: (i // tpb, 0, 0)),
                  pl.BlockSpec((1, d), full),
                  pl.BlockSpec(cs.shape, full)],
        out_specs=[out_spec, out_spec],
        compiler_params=_cparams(("arbitrary",)),
        name="fnet_channel",
    )(x2d, shift, scale, gain, cs)


FFT_J = SUBLANES


def _fnet_pos_kernel(y1_ref, y2_ref, lr_ref, ls_ref, cs_ref, sn_ref, o_ref, z1, z2, a_re, a_im, *, n):
    r1 = n // GRID_W
    gw = y1_ref.shape[2]
    nrj = FFT_J * r1
    z1[...] = y1_ref[0].astype(F32)
    z2[...] = y2_ref[0].astype(F32)

    def stage_r(sb, carry):
        s0 = pl.multiple_of(sb * FFT_J, FFT_J)
        rhs1 = jnp.concatenate([z1[pl.ds(GRID_W * r + s0, FFT_J), :] for r in range(r1)], axis=0).astype(BF16)
        rhs2 = jnp.concatenate([z2[pl.ds(GRID_W * r + s0, FFT_J), :] for r in range(r1)], axis=0).astype(BF16)
        p = jnp.dot(lr_ref[...], rhs1, preferred_element_type=F32)
        q = jnp.dot(lr_ref[...], rhs2, preferred_element_type=F32)
        re = p[:nrj] - q[nrj:]
        nim = q[:nrj] + p[nrj:]
        cs = jnp.concatenate([cs_ref[sb]] * (gw // LANES), axis=1)
        sn = jnp.concatenate([sn_ref[sb]] * (gw // LANES), axis=1)
        tre = re * cs - nim * sn
        tnim = re * sn + nim * cs
        for c in range(r1):
            a_re[pl.ds(GRID_W * c + s0, FFT_J), :] = tre[FFT_J * c:FFT_J * (c + 1)]
            a_im[pl.ds(GRID_W * c + s0, FFT_J), :] = tnim[FFT_J * c:FFT_J * (c + 1)]
        return carry
    lax.fori_loop(0, GRID_W // FFT_J, stage_r, 0)

    cblk = FFT_J * GRID_W

    def stage_s(cb, carry):
        c0 = pl.multiple_of(cb * cblk, cblk)
        blk = jnp.concatenate([a_re[pl.ds(c0, cblk), :], a_im[pl.ds(c0, cblk), :]], axis=0).astype(BF16)
        o = jnp.dot(ls_ref[...], blk, preferred_element_type=F32)
        k0 = pl.multiple_of(cb * FFT_J, FFT_J)
        for d in range(GRID_W):
            z1[pl.ds(r1 * d + k0, FFT_J), :] = o[FFT_J * d:FFT_J * (d + 1)]
        return carry
    lax.fori_loop(0, r1 // FFT_J, stage_s, 0)
    o_ref[0] = z1[...].astype(o_ref.dtype)


def _fnet_position(y1, y2, lr, ls, tw_cos, tw_sin, batch, n):
    groups, t, gw = y1.shape
    full2 = lambda b, g: (0, 0)
    full3 = lambda b, g: (0, 0, 0)
    scratch = pltpu.VMEM((n, gw), F32)
    return pl.pallas_call(
        functools.partial(_fnet_pos_kernel, n=n),
        out_shape=jax.ShapeDtypeStruct((groups, t, gw), BF16),
        grid=(batch, groups),
        in_specs=[pl.BlockSpec((1, n, gw), lambda b, g: (g, b, 0)),
                  pl.BlockSpec((1, n, gw), lambda b, g: (g, b, 0)),
                  pl.BlockSpec(lr.shape, full2),
                  pl.BlockSpec(ls.shape, full2),
                  pl.BlockSpec(tw_cos.shape, full3),
                  pl.BlockSpec(tw_sin.shape, full3)],
        out_specs=pl.BlockSpec((1, n, gw), lambda b, g: (g, b, 0)),
        scratch_shapes=[scratch, scratch, scratch, scratch],
        compiler_params=_cparams(("arbitrary", "arbitrary")),
        name="fnet_position",
    )(y1, y2, lr, ls, tw_cos, tw_sin)


def _dft_tables(n):
    assert n % (GRID_W * FFT_J) == 0
    gw = D_MODEL // FNET_GROUPS
    j = np.arange(gw)
    ang = 2.0 * np.pi * ((j[:, None] * j[None, :]) % gw) / gw
    cs = np.concatenate([np.cos(ang), np.sin(ang)], axis=1) / np.sqrt(gw)
    r1 = n // GRID_W
    eye = np.eye(FFT_J)
    scale = float(n) ** -0.25
    a = np.arange(r1)
    ang_r = 2.0 * np.pi * ((a[:, None] * a[None, :]) % r1) / r1
    lr = np.concatenate([np.kron(np.cos(ang_r), eye), np.kron(np.sin(ang_r), eye)], axis=0) * scale
    s = np.arange(GRID_W)
    ang_s = 2.0 * np.pi * ((s[:, None] * s[None, :]) % GRID_W) / GRID_W
    m_c = np.einsum("ds,cC->dcCs", np.cos(ang_s), eye).reshape(FFT_J * GRID_W, FFT_J * GRID_W)
    m_s = np.einsum("ds,cC->dcCs", np.sin(ang_s), eye).reshape(FFT_J * GRID_W, FFT_J * GRID_W)
    ls = np.concatenate([m_c, -m_s], axis=1) * scale
    sb = np.arange(GRID_W // FFT_J)
    s_of = sb[:, None, None] * FFT_J + np.arange(FFT_J)[None, None, :]
    ang_t = 2.0 * np.pi * ((s_of * a[None, :, None]) % n) / n
    ang_t = ang_t.reshape(len(sb), r1 * FFT_J, 1)
    tw_cos = jnp.broadcast_to(jnp.asarray(np.cos(ang_t), F32), (len(sb), r1 * FFT_J, LANES))
    tw_sin = jnp.broadcast_to(jnp.asarray(np.sin(ang_t), F32), (len(sb), r1 * FFT_J, LANES))
    return jnp.asarray(cs, BF16), jnp.asarray(lr, BF16), jnp.asarray(ls, BF16), tw_cos, tw_sin


def kernel(x, c, ctx, c_ctx, ada_w, ada_b, norm_mix, norm_ffn, mix_w_in, mix_w_out, na_q_norm, na_k_norm, na_rpb,
           lru_conv_w, lru_conv_b, lru_gate_r_w, lru_gate_r_b, lru_gate_i_w, lru_gate_i_b, lru_lambda,
           fnet_w_out, router_w, router_bias, moe_w_gate, moe_w_up, moe_w_down):
    batch, n, d = x.shape
    ctx_len = ctx.shape[1]
    depth = ada_w.shape[0]
    rows = n // GRID_W
    assert d == D_MODEL and n % (GRID_W * NA_QROWS) == 0 and rows >= 4 * NA_QROWS
    assert n % MOE_TILE == 0
    t = batch * n
    tm = 512
    tri = jnp.asarray(np.triu(np.ones((MOE_TILE, MOE_TILE))), BF16)

    r_pad = -(-(batch + 1) // SUBLANES) * SUBLANES
    c_rows = jnp.concatenate([c, c_ctx[None, :], jnp.zeros((r_pad - batch - 1, d), c.dtype)], axis=0)
    mod = _modulation(c_rows, ada_w, ada_b)

    def mod_slices(layer):
        m = mod[layer, :batch].reshape(batch, 1, 6, d)
        return [m[:, :, i, :] for i in range(6)]

    rw_cat = jnp.pad(router_w.astype(F32), ((0, 0), (0, 2 * LANES - N_EXPERTS))).astype(BF16)
    rbias = router_bias.reshape(N_EXPERTS, 1).astype(F32)
    x2d = x.reshape(t, d)
    ctx2d = ctx.reshape(batch * ctx_len, d)

    pending = None
    for layer in range(depth):
        li = layer // 2
        shift1, scale1, gate1, shift2, scale2, gate2 = mod_slices(layer)
        gain_mix = norm_mix[layer].reshape(1, d)
        gain_ffn = norm_ffn[layer].reshape(1, d)
        if layer % 2 == 0:
            if pending is not None:
                x2d = _combine(*pending, n)
            w_in = mix_w_in[li].astype(BF16)
            ind = jnp.asarray(np.kron(np.eye(NA_HEADS // 2), np.ones((HEAD_DIM, HEAD_DIM))), BF16)
            qg = (jnp.tile(na_q_norm[li], NA_HEADS) * (HEAD_DIM ** -0.5 * LOG2_E)).reshape(1, NA_WIDTH).astype(F32)
            kg = jnp.tile(na_k_norm[li], NA_HEADS).reshape(1, NA_WIDTH).astype(F32)
            q, k, v, xb, gb = _inproj(x2d, shift1, scale1, gain_mix, w_in, ind, qg, kg,
                                      ("q", "k", "v", "x", "g"), n, 2 * tm)
            mctx = mod[layer, batch, :2 * d]
            shift_c = jnp.broadcast_to(mctx[:d], (batch, 1, d))
            scale_c = jnp.broadcast_to(mctx[d:], (batch, 1, d))
            k_c, v_c, xb_c = _inproj(ctx2d, shift_c, scale_c, gain_mix, w_in[:, NA_WIDTH:4 * NA_WIDTH], ind, qg, kg,
                                     ("k", "v", "x"), ctx_len, ctx_len)
            bias = _na_bias_tables(na_rpb[li], rows)
            attn = _attention(q, k, v, k_c, v_c, bias, batch, n, ctx_len)
            wcat, gbias = _lru_gate_weights(lru_gate_r_w[li], lru_gate_r_b[li], lru_gate_i_w[li], lru_gate_i_b[li])
            lru = _lru(xb, gb, xb_c, lru_conv_w[li].astype(F32), lru_conv_b[li].reshape(1, LRU_WIDTH).astype(F32),
                       wcat, gbias, lru_lambda[li].astype(F32), batch, n, ctx_len)
            parts, w_out = [attn, lru], mix_w_out[li].astype(BF16)
        else:
            cs, lr, ls, tw_cos, tw_sin = _dft_tables(n)
            if pending is not None:
                x2d, y1, y2 = _combine(*pending, n, fnet=(shift1, scale1, gain_mix, cs))
            else:
                y1, y2 = _fnet_channel(x2d, shift1, scale1, gain_mix, cs, n, tm)
            parts, w_out = [_fnet_position(y1, y2, lr, ls, tw_cos, tw_sin, batch, n)], fnet_w_out[li].astype(BF16)
        x1, h2, route = _post_mixer(parts, w_out, x2d, gate1, shift2, scale2, gain_ffn, rw_cat, rbias, n,
                                    2 * POST_SUBTILE)
        pending = _grouped_moe(h2, route, x1, gate2, moe_w_gate.astype(F32), moe_w_up.astype(F32),
                               moe_w_down.astype(F32), layer, tri)
    return _combine(*pending, n).reshape(batch, n, d)
```

```python
import functools

import numpy as np
import jax
import jax.numpy as jnp
from jax import lax
from jax.experimental import pallas as pl
from jax.experimental.pallas import tpu as pltpu

F32 = jnp.float32
BF16 = jnp.bfloat16
HIGHEST = lax.Precision.HIGHEST

D_MODEL = 1024
GRID_W = 64
HEAD_DIM = 64
NA_HEADS = 8
NA_WIDTH = NA_HEADS * HEAD_DIM
NA_WIN_ROWS = 8
NA_WIN_COLS = 16
LRU_WIDTH = 512
LRU_BLOCK = 64
LRU_C = 8.0
FNET_GROUPS = 4
N_EXPERTS = 16
EXPERTS_PER_GROUP = 4
N_EXPERT_GROUPS = 4
D_FF_EXPERT = 512
RMS_EPS = 1e-6
MASK_VALUE = -1e30
LOG2_E = 1.4426950408889634

V7X_VMEM_LIMIT_BYTES = 56 * 1024 * 1024
LANES = 128
SUBLANES = 8

NA_QROWS = 4
NA_KROWS = NA_QROWS + NA_WIN_ROWS - 1
NA_QBLK = NA_QROWS * GRID_W
NA_KBLK = NA_KROWS * GRID_W
NA_STEP_BLOCKS = 2

LRU_CHUNK = LANES
LRU_TROWS = 512

ROUTE_GID_ROW = EXPERTS_PER_GROUP
MOE_TILE = 512
ROW_ALIGN = 16
MOE_CROWS = MOE_TILE + N_EXPERT_GROUPS * ROW_ALIGN
MOE_SEG_BITS = (MOE_TILE // ROW_ALIGN).bit_length()
MOE_TAIL_BITS = (MOE_TILE // ROW_ALIGN - 1).bit_length()


def _sigmoid(x):
    return 1.0 / (1.0 + jnp.exp(-x))


def _sigmoid_tanh(x):
    return 0.5 + 0.5 * jnp.tanh(0.5 * x)


def _cparams(sem, vmem=V7X_VMEM_LIMIT_BYTES):
    return pltpu.CompilerParams(dimension_semantics=sem, vmem_limit_bytes=vmem)


def _mod_kernel(c_ref, w_ref, b_ref, o_ref):
    c = c_ref[...]
    s = c * _sigmoid(c)
    o_ref[0] = jnp.dot(s, w_ref[0], precision=HIGHEST, preferred_element_type=F32) + b_ref[0]


def _modulation(c_rows, ada_w, ada_b):
    depth, d, n6 = ada_w.shape
    r = c_rows.shape[0]
    tn = 1536
    return pl.pallas_call(
        _mod_kernel,
        out_shape=jax.ShapeDtypeStruct((depth, r, n6), F32),
        grid=(depth, n6 // tn),
        in_specs=[pl.BlockSpec((r, d), lambda l, j: (0, 0)),
                  pl.BlockSpec((1, d, tn), lambda l, j: (l, 0, j)),
                  pl.BlockSpec((1, 1, tn), lambda l, j: (l, 0, j))],
        out_specs=pl.BlockSpec((1, r, tn), lambda l, j: (l, 0, j)),
        compiler_params=_cparams(("arbitrary", "arbitrary")),
        name="adaln_mod",
    )(c_rows, ada_w, ada_b.reshape(depth, 1, n6))


def _norm_modulate(x, gain, shift, scale):
    ms = jnp.mean(x * x, axis=-1, keepdims=True)
    y = x * lax.rsqrt(ms + RMS_EPS) * gain
    return y * (1.0 + scale) + shift


def _inproj_kernel(x_ref, shift_ref, scale_ref, gain_ref, w_ref, ind_ref, qg_ref, kg_ref, *out_refs, segs):
    h = _norm_modulate(x_ref[...], gain_ref[...], shift_ref[0], scale_ref[0]).astype(BF16)
    for s, (kind, o_ref) in enumerate(zip(segs, out_refs)):
        z = jnp.dot(h, w_ref[:, s * NA_WIDTH:(s + 1) * NA_WIDTH], preferred_element_type=F32)
        if kind in ("q", "k"):
            zz = (z * z).astype(BF16)
            hw = ind_ref.shape[0]
            ms = jnp.concatenate([jnp.dot(zz[:, i * hw:(i + 1) * hw], ind_ref[...], preferred_element_type=F32)
                                  for i in range(NA_WIDTH // hw)], axis=1) * (1.0 / HEAD_DIM)
            g = qg_ref[...] if kind == "q" else kg_ref[...]
            z = z * lax.rsqrt(ms + RMS_EPS) * g
        if kind in ("x", "g"):
            for c in range(LRU_WIDTH // LRU_CHUNK):
                o_ref[c] = z[:, c * LRU_CHUNK:(c + 1) * LRU_CHUNK].astype(o_ref.dtype)
        else:
            o_ref[...] = z.astype(o_ref.dtype)


def _inproj(x2d, shift, scale, gain, w, ind, qg, kg, segs, tokens_per_batch, tm):
    t, d = x2d.shape
    tpb = tokens_per_batch // tm
    dt = {"q": BF16, "k": BF16, "v": BF16, "x": F32, "g": F32}
    full = lambda i: (0, 0)
    nch = LRU_WIDTH // LRU_CHUNK

    def out_shape(kind):
        shape = (nch, t, LRU_CHUNK) if kind in ("x", "g") else (t, NA_WIDTH)
        return jax.ShapeDtypeStruct(shape, dt[kind])

    def out_spec(kind):
        if kind in ("x", "g"):
            return pl.BlockSpec((nch, tm, LRU_CHUNK), lambda i: (0, i, 0))
        return pl.BlockSpec((tm, NA_WIDTH), lambda i: (i, 0))

    return pl.pallas_call(
        functools.partial(_inproj_kernel, segs=segs),
        out_shape=[out_shape(k) for k in segs],
        grid=(t // tm,),
        in_specs=[pl.BlockSpec((tm, d), lambda i: (i, 0)),
                  pl.BlockSpec((1, 1, d), lambda i: (i // tpb, 0, 0)),
                  pl.BlockSpec((1, 1, d), lambda i: (i // tpb, 0, 0)),
                  pl.BlockSpec((1, d), full),
                  pl.BlockSpec(w.shape, full),
                  pl.BlockSpec(ind.shape, full),
                  pl.BlockSpec((1, NA_WIDTH), full),
                  pl.BlockSpec((1, NA_WIDTH), full)],
        out_specs=[out_spec(k) for k in segs],
        compiler_params=_cparams(("arbitrary",)),
        name="inproj_" + "".join(segs),
    )(x2d, shift, scale, gain, w, ind, qg, kg)


def _na_bias_tables(rpb, rows):
    kr = NA_WIN_ROWS
    rb_count = rows // NA_QROWS
    cq = np.arange(GRID_W)
    ck = np.arange(GRID_W)
    col_start = np.clip(cq - NA_WIN_COLS // 2, 0, GRID_W - NA_WIN_COLS)
    valid_c = (ck[None, :] >= col_start[:, None]) & (ck[None, :] < col_start[:, None] + NA_WIN_COLS)
    dc = np.clip(ck[None, :] - cq[:, None], 1 - NA_WIN_COLS, NA_WIN_COLS - 1) + (NA_WIN_COLS - 1)
    n_dr, n_dc = 2 * NA_WIN_ROWS - 1, 2 * NA_WIN_COLS - 1
    sel_c = (dc[:, :, None] == np.arange(n_dc)) & valid_c[:, :, None]
    blocks = jnp.einsum("hrc,qkc->hrqk", rpb.astype(F32), jnp.asarray(sel_c, F32), precision=HIGHEST)
    blocks = blocks + jnp.asarray(np.where(valid_c, 0.0, MASK_VALUE), F32)
    blocks = jnp.concatenate([blocks, jnp.full((NA_HEADS, 1, GRID_W, GRID_W), MASK_VALUE, F32)], axis=1)
    blocks = blocks * LOG2_E
    which = []
    for rb in (0, 1, rb_count - 1):
        r = rb * NA_QROWS + np.arange(NA_QROWS)
        ks = int(np.clip(rb * NA_QROWS - kr // 2, 0, rows - NA_KROWS))
        key_r = ks + np.arange(NA_KROWS)
        row_start = np.clip(r - kr // 2, 0, rows - kr)
        valid_r = (key_r[None, :] >= row_start[:, None]) & (key_r[None, :] < row_start[:, None] + kr)
        dr = np.clip(key_r[None, :] - r[:, None] + (NA_WIN_ROWS - 1), 0, n_dr - 1)
        which.append(np.where(valid_r, dr, n_dr))
    return _na_bias_assemble(blocks, which)


def _na_bias_kernel(blk_ref, o_ref, *, which):
    for t, table in enumerate(which):
        @pl.when(pl.program_id(0) == t)
        def _():
            for i in range(NA_QROWS):
                row = jnp.concatenate([blk_ref[0, int(table[i, j])] for j in range(NA_KROWS)], axis=1)
                o_ref[0, 0, i * GRID_W:(i + 1) * GRID_W, :] = row


def _na_bias_assemble(blocks, which):
    heads, nblk = blocks.shape[:2]
    return pl.pallas_call(
        functools.partial(_na_bias_kernel, which=which),
        out_shape=jax.ShapeDtypeStruct((len(which), heads, NA_QBLK, NA_KBLK), F32),
        grid=(len(which), heads),
        in_specs=[pl.BlockSpec((1, nblk, GRID_W, GRID_W), lambda t, h: (h, 0, 0, 0))],
        out_specs=pl.BlockSpec((1, 1, NA_QBLK, NA_KBLK), lambda t, h: (t, h, 0, 0)),
        compiler_params=_cparams(("arbitrary", "arbitrary")),
        name="na_bias",
    )(blocks)


def _attn_kernel(q_ref, k_ref, v_ref, kc_ref, vc_ref, bias_ref, o_ref, *, rows):
    last = rows // NA_QROWS - 1
    nt = (((1,), (1,)), ((), ()))
    ctx_len = kc_ref.shape[0]
    low_half = lax.broadcasted_iota(jnp.int32, (NA_QBLK, LANES), 1) < HEAD_DIM
    for blk in range(NA_STEP_BLOCKS):
        rb = pl.program_id(1) * NA_STEP_BLOCKS + blk
        ks = jnp.clip(rb * NA_QROWS - NA_WIN_ROWS // 2, 0, rows - NA_KROWS)
        kstart = pl.multiple_of(ks * GRID_W, GRID_W)
        geom = jnp.where(rb == 0, 0, jnp.where(rb == last, 2, 1))
        qrows = slice(blk * NA_QBLK, (blk + 1) * NA_QBLK)
        for pair in range(NA_HEADS * HEAD_DIM // LANES):
            ls = slice(pair * LANES, (pair + 1) * LANES)
            q2 = q_ref[qrows, ls]
            k_all = jnp.concatenate([kc_ref[:, ls], k_ref[pl.ds(kstart, NA_KBLK), ls]], axis=0)
            v_all = jnp.concatenate([vc_ref[:, ls], v_ref[pl.ds(kstart, NA_KBLK), ls]], axis=0)
            outs = []
            for half in range(2):
                qh = jnp.where(low_half == (half == 0), q2, jnp.zeros_like(q2))
                s = lax.dot_general(qh, k_all, nt, preferred_element_type=F32)
                s = jnp.concatenate([s[:, :ctx_len], s[:, ctx_len:] + bias_ref[geom, 2 * pair + half]], axis=1)
                m = jnp.max(s, axis=-1, keepdims=True)
                p = jnp.exp2(s - m)
                l = jnp.sum(p, axis=-1, keepdims=True)
                outs.append(jnp.dot(p.astype(BF16), v_all, preferred_element_type=F32) / l)
            o_ref[qrows, ls] = jnp.where(low_half, outs[0], outs[1]).astype(o_ref.dtype)


def _attention(q, k, v, kc, vc, bias, batch, n, ctx_len):
    rows = n // GRID_W
    rbc = rows // (NA_QROWS * NA_STEP_BLOCKS)
    qblk = NA_QBLK * NA_STEP_BLOCKS
    return pl.pallas_call(
        functools.partial(_attn_kernel, rows=rows),
        out_shape=jax.ShapeDtypeStruct((batch * n, NA_WIDTH), BF16),
        grid=(batch, rbc),
        in_specs=[pl.BlockSpec((qblk, NA_WIDTH), lambda b, rb: (b * rbc + rb, 0)),
                  pl.BlockSpec((n, NA_WIDTH), lambda b, rb: (b, 0)),
                  pl.BlockSpec((n, NA_WIDTH), lambda b, rb: (b, 0)),
                  pl.BlockSpec((ctx_len, NA_WIDTH), lambda b, rb: (b, 0)),
                  pl.BlockSpec((ctx_len, NA_WIDTH), lambda b, rb: (b, 0)),
                  pl.BlockSpec(bias.shape, lambda b, rb: (0, 0, 0, 0), pipeline_mode=pl.Buffered(1))],
        out_specs=pl.BlockSpec((qblk, NA_WIDTH), lambda b, rb: (b * rbc + rb, 0)),
        compiler_params=_cparams(("arbitrary", "arbitrary")),
        name="na_attention",
    )(q, k, v, kc, vc, bias)


SCAN_GROUPS = 2
SCAN_CHUNKS = SCAN_GROUPS * SUBLANES


def _scan_pitch(n):
    p = -(-n // SCAN_CHUNKS)
    while p % 8 != 4:
        p += 1
    return p


NEG_LOG2_E = -LOG2_E


def _lru_coeff_tile(half_xc, zh, half_bias, k, d):
    c = LRU_CHUNK
    t_r = jnp.tanh(zh[:, (2 * d) * c:(2 * d + 1) * c] + half_bias[:, (2 * d) * c:(2 * d + 1) * c])
    t_i = jnp.tanh(zh[:, (2 * d + 1) * c:(2 * d + 2) * c] + half_bias[:, (2 * d + 1) * c:(2 * d + 2) * c])
    neg_log_a = k[d:d + 1, :] * (1.0 + t_r)
    a = jnp.exp2(neg_log_a * NEG_LOG2_E)
    one_minus_a2 = jnp.tanh(neg_log_a) * (a * a + 1.0)
    root = jnp.where(one_minus_a2 > 0.0, one_minus_a2 * lax.rsqrt(one_minus_a2), 0.0)
    return a, root * (half_xc + half_xc * t_i)


def _conv_tile(xpad, t0, w, b, rows):
    acc = b + w[0:1, :] * xpad[pl.ds(t0 + SUBLANES - 2, rows), :]
    acc = acc + w[1:2, :] * xpad[pl.ds(t0 + SUBLANES - 1, rows), :]
    acc = acc + w[2:3, :] * xpad[pl.ds(t0 + SUBLANES, rows), :]
    return acc + w[3:4, :] * xpad[pl.ds(t0 + SUBLANES + 1, rows), :]


SCAN_UNROLL = 4


def _group_rows(j, g, pitch):
    return pl.ds(g * SUBLANES * pitch + j, SUBLANES, stride=pitch)


def _chunk_totals(af_ref, bf_ref, ab_ref, bb_ref, pitch):
    def body(j, carry):
        jb = pitch - 1 - j
        out = []
        for g in range(SCAN_GROUPS):
            pf, hf, pb, hb = carry[4 * g:4 * g + 4]
            af = af_ref[_group_rows(j, g, pitch), :]
            ab = ab_ref[_group_rows(jb, g, pitch), :]
            out += [af * pf, af * hf + bf_ref[_group_rows(j, g, pitch), :],
                    ab * pb, ab * hb + bb_ref[_group_rows(jb, g, pitch), :]]
        return tuple(out)
    one = jnp.ones((SUBLANES, LRU_CHUNK), F32)
    zero = jnp.zeros((SUBLANES, LRU_CHUNK), F32)
    res = lax.fori_loop(0, pitch, body, (one, zero, one, zero) * SCAN_GROUPS, unroll=SCAN_UNROLL)
    fwd = [(res[4 * g], res[4 * g + 1]) for g in range(SCAN_GROUPS)]
    bwd = [(res[4 * g + 2], res[4 * g + 3]) for g in range(SCAN_GROUPS)]
    return fwd, bwd


def _chunk_starts(totals, h0, reverse):
    row = lax.broadcasted_iota(jnp.int32, (SUBLANES, LRU_CHUNK), 0)
    starts = [jnp.zeros((SUBLANES, LRU_CHUNK), F32) for _ in range(SCAN_GROUPS)]
    state = h0
    order = range(SCAN_CHUNKS - 1, -1, -1) if reverse else range(SCAN_CHUNKS)
    for c in order:
        g, s = divmod(c, SUBLANES)
        p_end, h_end = totals[g]
        starts[g] = jnp.where(row == s, state, starts[g])
        state = p_end[s:s + 1, :] * state + h_end[s:s + 1, :]
    return starts, state


def _scan_write(af_ref, bf_ref, hf_ref, ab_ref, bb_ref, hb_ref, starts_f, starts_b, pitch):
    def body(j, carry):
        jb = pitch - 1 - j
        out = []
        for g in range(SCAN_GROUPS):
            hf, hb = carry[2 * g:2 * g + 2]
            hf = af_ref[_group_rows(j, g, pitch), :] * hf + bf_ref[_group_rows(j, g, pitch), :]
            hb = ab_ref[_group_rows(jb, g, pitch), :] * hb + bb_ref[_group_rows(jb, g, pitch), :]
            hf_ref[_group_rows(j, g, pitch), :] = hf
            hb_ref[_group_rows(jb, g, pitch), :] = hb
            out += [hf, hb]
        return tuple(out)
    init = tuple(v for g in range(SCAN_GROUPS) for v in (starts_f[g], starts_b[g]))
    lax.fori_loop(0, pitch, body, init, unroll=SCAN_UNROLL)


def _lru_kernel(x_ref, g_ref, xc_ref, cw_ref, cb_ref, w_ref, gb_ref, lam_ref, o_ref,
                xpad, a0, b0, a1, b1, h0s, h1s, ca0, cb0, ca1, cb1, *, n, ctx_len):
    pitch = _scan_pitch(n)
    cpitch = _scan_pitch(ctx_len)
    cw = cw_ref[...]
    cb = cb_ref[...]
    gbias = gb_ref[0]
    lam = lam_ref[...]
    sp = jnp.maximum(-lam, 0.0) + jnp.log1p(jnp.exp(-jnp.abs(lam)))
    k = (0.5 * LRU_C) * sp
    wcat = w_ref[0]
    zeros8 = jnp.zeros((SUBLANES, LRU_CHUNK), F32)

    def fill_coeffs(src_rows, total, length, trows, a_refs, b_refs):
        for d in range(2):
            a_refs[d][pl.ds(length, total - length), :] = jnp.ones((total - length, LRU_CHUNK), F32)
            b_refs[d][pl.ds(length, total - length), :] = jnp.zeros((total - length, LRU_CHUNK), F32)
        xpad[pl.ds(0, SUBLANES), :] = zeros8
        xpad[pl.ds(SUBLANES + length, SUBLANES), :] = zeros8
        xpad[pl.ds(SUBLANES, length), :] = src_rows

        def tile(t, carry):
            t0 = pl.multiple_of(t * trows, SUBLANES)
            xc = _conv_tile(xpad, t0, cw, cb, trows)
            zh = jnp.dot(xc.astype(BF16), wcat, preferred_element_type=F32)
            half_xc = 0.5 * xc
            for d in range(2):
                a, b = _lru_coeff_tile(half_xc, zh, gbias, k, d)
                a_refs[d][pl.ds(t0, trows), :] = a
                b_refs[d][pl.ds(t0, trows), :] = b
            return carry
        lax.fori_loop(0, length // trows, tile, 0)

    fill_coeffs(xc_ref[0], SCAN_CHUNKS * cpitch, ctx_len, ctx_len, (ca0, ca1), (cb0, cb1))
    zero_state = jnp.zeros((1, LRU_CHUNK), F32)
    fwd, bwd = _chunk_totals(ca0, cb0, ca1, cb1, cpitch)
    _, init_f = _chunk_starts(fwd, zero_state, reverse=False)
    _, init_b = _chunk_starts(bwd, zero_state, reverse=True)

    fill_coeffs(x_ref[0], SCAN_CHUNKS * pitch, n, LRU_TROWS, (a0, a1), (b0, b1))
    fwd, bwd = _chunk_totals(a0, b0, a1, b1, pitch)
    starts_f, _ = _chunk_starts(fwd, init_f, reverse=False)
    starts_b, _ = _chunk_starts(bwd, init_b, reverse=True)
    _scan_write(a0, b0, h0s, a1, b1, h1s, starts_f, starts_b, pitch)

    def out_tile(t, carry):
        t0 = pl.multiple_of(t * LRU_TROWS, SUBLANES)
        y = h0s[pl.ds(t0, LRU_TROWS), :] + h1s[pl.ds(t0, LRU_TROWS), :]
        g = g_ref[0, pl.ds(t0, LRU_TROWS), :]
        gelu = 0.5 * g * (1.0 + jnp.tanh(0.7978845608028654 * (g + 0.044715 * (g * g * g))))
        o_ref[0, pl.ds(t0, LRU_TROWS), :] = (gelu * y).astype(o_ref.dtype)
        return carry
    lax.fori_loop(0, n // LRU_TROWS, out_tile, 0)


def _lru(xb, gb, xb_ctx, conv_w, conv_b, wcat, gbias, lam, batch, n, ctx_len):
    nch = LRU_WIDTH // LRU_CHUNK
    pitch = _scan_pitch(n)
    cpitch = _scan_pitch(ctx_len)
    big = pltpu.VMEM((SCAN_CHUNKS * pitch, LRU_CHUNK), F32)
    small = pltpu.VMEM((SCAN_CHUNKS * cpitch, LRU_CHUNK), F32)
    return pl.pallas_call(
        functools.partial(_lru_kernel, n=n, ctx_len=ctx_len),
        out_shape=jax.ShapeDtypeStruct((nch, batch * n, LRU_CHUNK), BF16),
        grid=(batch, nch),
        in_specs=[pl.BlockSpec((1, n, LRU_CHUNK), lambda b, c: (c, b, 0)),
                  pl.BlockSpec((1, n, LRU_CHUNK), lambda b, c: (c, b, 0)),
                  pl.BlockSpec((1, ctx_len, LRU_CHUNK), lambda b, c: (c, b, 0)),
                  pl.BlockSpec((4, LRU_CHUNK), lambda b, c: (0, c)),
                  pl.BlockSpec((1, LRU_CHUNK), lambda b, c: (0, c)),
                  pl.BlockSpec((1, LRU_CHUNK, 4 * LRU_CHUNK), lambda b, c: (c, 0, 0)),
                  pl.BlockSpec((1, 1, 4 * LRU_CHUNK), lambda b, c: (c, 0, 0)),
                  pl.BlockSpec((2, LRU_CHUNK), lambda b, c: (0, c))],
        out_specs=pl.BlockSpec((1, n, LRU_CHUNK), lambda b, c: (c, b, 0)),
        scratch_shapes=[pltpu.VMEM((n + 2 * SUBLANES, LRU_CHUNK), F32),
                        big, big, big, big, big, big, small, small, small, small],
        compiler_params=_cparams(("arbitrary", "arbitrary")),
        name="rglru",
    )(xb, gb, xb_ctx, conv_w, conv_b, wcat, gbias, lam)


def _lru_gate_weights(w_r, b_r, w_i, b_i):
    nch = LRU_WIDTH // LRU_CHUNK
    bpc = LRU_CHUNK // LRU_BLOCK

    def dense(w):
        wc = w.reshape(nch, bpc, LRU_BLOCK, LRU_BLOCK)
        eye = jnp.eye(bpc, dtype=w.dtype)
        return jnp.einsum("cbij,bd->cbidj", wc, eye).reshape(nch, LRU_CHUNK, LRU_CHUNK)

    wcat = jnp.concatenate([dense(w_r[0]), dense(w_i[0]), dense(w_r[1]), dense(w_i[1])], axis=-1)
    chunk = lambda v: v.reshape(nch, 1, LRU_CHUNK)
    gbias = jnp.concatenate([chunk(b_r[0]), chunk(b_i[0]), chunk(b_r[1]), chunk(b_i[1])], axis=-1)
    return (0.5 * wcat).astype(BF16), (0.5 * gbias).astype(F32)


def _route(s, sel, route_ref):
    srow = [s[e:e + 1, :] for e in range(N_EXPERTS)]
    lrow = [sel[e:e + 1, :] for e in range(N_EXPERTS)]
    gscore = []
    for g in range(N_EXPERT_GROUPS):
        a = lrow[g * EXPERTS_PER_GROUP:(g + 1) * EXPERTS_PER_GROUP]
        best = a[0] + a[1]
        for i, j in ((0, 2), (0, 3), (1, 2), (1, 3), (2, 3)):
            best = jnp.maximum(best, a[i] + a[j])
        gscore.append(best)
    bg = jnp.zeros_like(gscore[0], dtype=jnp.int32)
    bv = gscore[0]
    for g in range(1, N_EXPERT_GROUPS):
        upd = gscore[g] > bv
        bg = jnp.where(upd, g, bg)
        bv = jnp.where(upd, gscore[g], bv)

    def pick(rows_):
        out = []
        for j in range(EXPERTS_PER_GROUP):
            v = rows_[j]
            for g in range(1, N_EXPERT_GROUPS):
                v = jnp.where(bg == g, rows_[g * EXPERTS_PER_GROUP + j], v)
            out.append(v)
        return out
    cand = pick(lrow)
    cs = pick(srow)
    i1 = jnp.zeros_like(bg)
    v1 = cand[0]
    w1 = cs[0]
    for j in range(1, EXPERTS_PER_GROUP):
        upd = cand[j] > v1
        i1 = jnp.where(upd, j, i1)
        v1 = jnp.where(upd, cand[j], v1)
        w1 = jnp.where(upd, cs[j], w1)
    i2 = jnp.full_like(bg, -1)
    v2 = jnp.full_like(v1, -jnp.inf)
    w2 = jnp.zeros_like(w1)
    for j in range(EXPERTS_PER_GROUP):
        upd = (i1 != j) & (cand[j] > v2)
        i2 = jnp.where(upd, j, i2)
        v2 = jnp.where(upd, cand[j], v2)
        w2 = jnp.where(upd, cs[j], w2)
    den = w1 + w2
    g1 = w1 / den
    g2 = w2 / den
    for j in range(EXPERTS_PER_GROUP):
        route_ref[j:j + 1, :] = jnp.where(i1 == j, g1, 0.0) + jnp.where(i2 == j, g2, 0.0)
    route_ref[ROUTE_GID_ROW:ROUTE_GID_ROW + 1, :] = bg.astype(F32)
    pad = SUBLANES - ROUTE_GID_ROW - 1
    route_ref[ROUTE_GID_ROW + 1:, :] = jnp.zeros((pad, bg.shape[1]), F32)


POST_SUBTILE = 512


def _post_kernel(*refs, n_parts):
    parts = refs[:n_parts]
    (w_ref, x_ref, gate_ref, shift_ref, scale_ref, gain_ref, rw_ref, rb_ref,
     x1_ref, h2_ref, route_ref) = refs[n_parts:]
    for sub in range(x_ref.shape[0] // POST_SUBTILE):
        rows = pl.ds(sub * POST_SUBTILE, POST_SUBTILE)
        pieces = []
        for p in parts:
            pieces += [p[c, rows, :] for c in range(p.shape[0])] if len(p.shape) == 3 else [p[rows, :]]
        mixed = jnp.concatenate(pieces, axis=-1) if len(pieces) > 1 else pieces[0]
        mix = jnp.dot(mixed, w_ref[...], preferred_element_type=F32)
        x1 = x_ref[rows, :] + gate_ref[0] * mix
        x1_ref[rows, :] = x1
        h2 = _norm_modulate(x1, gain_ref[...], shift_ref[0], scale_ref[0])
        h_hi = h2.astype(BF16)
        h2_ref[rows, :] = h_hi
        logits = jnp.dot(h_hi, rw_ref[...], preferred_element_type=F32)
        s = _sigmoid(logits[:, :LANES].T[:N_EXPERTS, :])
        _route(s, s + rb_ref[...], route_ref.at[:, rows])


def _post_mixer(parts, w, x2d, gate1, shift2, scale2, gain, rw_cat, rbias, tokens_per_batch, tm):
    t, d = x2d.shape
    tpb = tokens_per_batch // tm
    full = lambda i: (0, 0)
    per_b = lambda i: (i // tpb, 0, 0)

    def part_spec(p):
        if p.ndim == 3:
            return pl.BlockSpec((p.shape[0], tm, p.shape[2]), lambda i: (0, i, 0))
        return pl.BlockSpec((tm, p.shape[1]), lambda i: (i, 0))

    return pl.pallas_call(
        functools.partial(_post_kernel, n_parts=len(parts)),
        out_shape=[jax.ShapeDtypeStruct((t, d), F32), jax.ShapeDtypeStruct((t, d), BF16),
                   jax.ShapeDtypeStruct((SUBLANES, t), F32)],
        grid=(t // tm,),
        in_specs=[part_spec(p) for p in parts] + [
                  pl.BlockSpec(w.shape, full),
                  pl.BlockSpec((tm, d), lambda i: (i, 0)),
                  pl.BlockSpec((1, 1, d), per_b),
                  pl.BlockSpec((1, 1, d), per_b),
                  pl.BlockSpec((1, 1, d), per_b),
                  pl.BlockSpec((1, d), full),
                  pl.BlockSpec(rw_cat.shape, full),
                  pl.BlockSpec(rbias.shape, full)],
        out_specs=[pl.BlockSpec((tm, d), lambda i: (i, 0)),
                   pl.BlockSpec((tm, d), lambda i: (i, 0)),
                   pl.BlockSpec((SUBLANES, tm), lambda i: (0, i))],
        compiler_params=_cparams(("arbitrary",)),
        name="post_mixer",
    )(*parts, w, x2d, gate1, shift2, scale2, gain, rw_cat, rbias)


def _moe_layout(t):
    nt = t // MOE_TILE
    grid = -(-(t + N_EXPERT_GROUPS * (ROW_ALIGN - 1) * nt) // MOE_TILE) + N_EXPERT_GROUPS
    return nt, grid


def _moe_tables(gid, t):
    nt, grid = _moe_layout(t)
    ng = N_EXPERT_GROUPS
    per_tile = MOE_TILE // ROW_ALIGN
    onehot = (gid.reshape(nt, MOE_TILE, 1) == jnp.arange(ng, dtype=jnp.int32)).astype(jnp.int32)
    cnt = onehot.sum(axis=1)
    seg = (cnt + ROW_ALIGN - 1) // ROW_ALIGN
    src = jnp.cumsum(seg, axis=1) - seg
    fill = seg.sum(axis=0)
    ntile = (fill + per_tile - 1) // per_tile
    cum = jnp.cumsum(ntile)
    base = (cum - ntile) * per_tile
    dst = jnp.cumsum(seg, axis=0) - seg + base[None, :]
    seg_tab = jnp.concatenate([seg, src, dst], axis=1).reshape(-1).astype(jnp.int32)
    tail = (-fill) % per_tile
    tail_tab = jnp.concatenate([tail, fill + base, cum[-1:]]).astype(jnp.int32)
    i = jnp.arange(grid, dtype=jnp.int32)
    valid = i < cum[-1]
    ie = jnp.minimum(i, cum[-1] - 1)
    g_of = jnp.sum((ie[:, None] >= cum[None, :]).astype(jnp.int32), axis=1)
    return seg_tab, tail_tab, g_of.astype(jnp.int32), valid.astype(jnp.int32)


def _segment_copies(tab_ref, tile, enable, make_copy):
    ng = N_EXPERT_GROUPS
    base = jnp.maximum(tile, 0) * (3 * ng)
    out = []
    for g in range(ng):
        n = tab_ref[base + g]
        src = tab_ref[base + ng + g]
        dst = tab_ref[base + 2 * ng + g]
        for k in range(MOE_SEG_BITS - 1, -1, -1):
            done = (n >> (k + 1)) << (k + 1)
            rows = ROW_ALIGN << k
            s0 = pl.multiple_of((src + done) * ROW_ALIGN, ROW_ALIGN)
            d0 = pl.multiple_of((dst + done) * ROW_ALIGN, ROW_ALIGN)
            out.append((enable & (((n >> k) & 1) == 1), make_copy(s0, d0, rows)))
    return out


def _start_copies(pairs):
    for cond, copies in pairs:
        @pl.when(cond)
        def _():
            for c in copies:
                c.start()


def _wait_copies(pairs):
    for cond, copies in pairs:
        @pl.when(cond)
        def _():
            for c in copies:
                c.wait()


def _split_bf16x3(x):
    hi = x.astype(BF16).astype(F32)
    r1 = x - hi
    mid = r1.astype(BF16).astype(F32)
    lo = (r1 - mid).astype(BF16).astype(F32)
    return hi, mid, lo


def _dispatch_kernel(seg_ref, tail_ref, h_ref, route_ref, tri_ref, slot_ref, hs_ref, cbuf, zbuf, sem, *, nt):
    i = pl.program_id(0)
    tm, d = h_ref.shape
    ng = N_EXPERT_GROUPS
    cur = i % 2

    def seg_copies(tile, enable, buf):
        def seg_copy(s0, d0, rows):
            return (pltpu.make_async_copy(cbuf.at[buf, pl.ds(s0, rows)], hs_ref.at[pl.ds(d0, rows)], sem.at[buf]),)
        return _segment_copies(seg_ref, tile, enable, seg_copy)

    _wait_copies(seg_copies(i - 2, i >= 2, cur))

    route = route_ref[...]
    gid = route[ROUTE_GID_ROW:ROUTE_GID_ROW + 1, :]
    grp = lax.broadcasted_iota(jnp.int32, (SUBLANES, tm), 0).astype(F32)
    onehot = jnp.where(grp == gid, 1.0, 0.0)
    rank = jnp.dot(onehot.astype(BF16), tri_ref[...], preferred_element_type=F32)
    slot = jnp.zeros((1, tm), F32)
    for g in range(ng):
        start = (seg_ref[i * 3 * ng + ng + g] * ROW_ALIGN).astype(F32)
        slot = slot + onehot[g:g + 1, :] * (rank[g:g + 1, :] - 1.0 + start)
    slot_ref[...] = jnp.broadcast_to(slot, (SUBLANES, tm))
    perm = jnp.where(lax.broadcasted_iota(jnp.int32, (MOE_CROWS, tm), 0).astype(F32) == slot, 1.0, 0.0)
    perm = perm.astype(BF16)
    cbuf[cur, :, :d] = jnp.dot(perm, h_ref[...], preferred_element_type=F32).astype(cbuf.dtype)
    parts = jnp.concatenate(list(_split_bf16x3(route)) + [jnp.zeros((LANES - 3 * SUBLANES, tm), F32)], axis=0)
    record = lax.dot_general(perm, parts.astype(BF16), (((1,), (1,)), ((), ())), preferred_element_type=F32)
    cbuf[cur, :, d:] = record.astype(cbuf.dtype)
    _start_copies(seg_copies(i, i >= 0, cur))

    @pl.when(i == pl.num_programs(0) - 1)
    def _():
        _wait_copies(seg_copies(i - 1, i >= 1, 1 - cur))
        _wait_copies(seg_copies(i, i >= 0, cur))
        zbuf[...] = jnp.zeros(zbuf.shape, zbuf.dtype)

        def zero_copy(d0, rows):
            return (pltpu.make_async_copy(zbuf.at[pl.ds(0, rows)], hs_ref.at[pl.ds(d0, rows)], sem.at[0]),)
        pairs = []
        for g in range(ng):
            n = tail_ref[g]
            dst = tail_ref[ng + g]
            for k in range(MOE_TAIL_BITS - 1, -1, -1):
                done = (n >> (k + 1)) << (k + 1)
                d0 = pl.multiple_of((dst + done) * ROW_ALIGN, ROW_ALIGN)
                pairs.append((((n >> k) & 1) == 1, zero_copy(d0, ROW_ALIGN << k)))
        used = tail_ref[2 * ng]
        total = hs_ref.shape[0] // MOE_TILE
        for j in range(total - nt):
            d0 = pl.multiple_of(jnp.minimum(used + j, total - 1) * MOE_TILE, MOE_TILE)
            pairs.append((used + j < total, zero_copy(d0, MOE_TILE)))
        _start_copies(pairs)
        _wait_copies(pairs)


def _dispatch(seg_tab, tail_tab, h2, route, tri):
    t, d = h2.shape
    nt, grid = _moe_layout(t)
    rows = grid * MOE_TILE
    grid_spec = pltpu.PrefetchScalarGridSpec(
        num_scalar_prefetch=2,
        grid=(nt,),
        in_specs=[pl.BlockSpec((MOE_TILE, d), lambda i, *_: (i, 0)),
                  pl.BlockSpec((SUBLANES, MOE_TILE), lambda i, *_: (0, i)),
                  pl.BlockSpec((MOE_TILE, MOE_TILE), lambda i, *_: (0, 0))],
        out_specs=[pl.BlockSpec((SUBLANES, MOE_TILE), lambda i, *_: (0, i)),
                   pl.BlockSpec(memory_space=pl.ANY)],
        scratch_shapes=[pltpu.VMEM((2, MOE_CROWS, d + LANES), BF16), pltpu.VMEM((MOE_TILE, d + LANES), BF16),
                        pltpu.SemaphoreType.DMA((2,))])
    return pl.pallas_call(
        functools.partial(_dispatch_kernel, nt=nt),
        out_shape=[jax.ShapeDtypeStruct((SUBLANES, t), F32),
                   jax.ShapeDtypeStruct((rows, d + LANES), BF16)],
        grid_spec=grid_spec,
        compiler_params=_cparams(("arbitrary",)),
        name="moe_dispatch",
    )(seg_tab, tail_tab, h2, route, tri)


def _ffn_kernel(grp_ref, valid_ref, h_ref, wg32_ref, wu32_ref, wd32_ref, y_ref, wg_ref, wu_ref, wd_ref):
    i = pl.program_id(0)
    d = y_ref.shape[1]

    @pl.when((i == 0) | (grp_ref[i] != grp_ref[jnp.maximum(i - 1, 0)]))
    def _():
        for j in range(EXPERTS_PER_GROUP):
            cols = slice(j * D_FF_EXPERT, (j + 1) * D_FF_EXPERT)
            wg_ref[:, cols] = wg32_ref[j].astype(BF16)
            wu_ref[:, cols] = wu32_ref[j].astype(BF16)
            wd_ref[cols, :] = wd32_ref[j].astype(BF16)

    @pl.when(valid_ref[i] == 0)
    def _():
        y_ref[...] = jnp.zeros(y_ref.shape, y_ref.dtype)

    @pl.when(valid_ref[i] == 1)
    def _():
        h = h_ref[:, :d]
        gates = h_ref[:, d:].astype(F32)
        acts = []
        for j in range(EXPERTS_PER_GROUP):
            cols = slice(j * D_FF_EXPERT, (j + 1) * D_FF_EXPERT)
            a = jnp.dot(h, wg_ref[:, cols], preferred_element_type=F32)
            u = jnp.dot(h, wu_ref[:, cols], preferred_element_type=F32)
            gate = (gates[:, j:j + 1] + gates[:, SUBLANES + j:SUBLANES + j + 1]
                    + gates[:, 2 * SUBLANES + j:2 * SUBLANES + j + 1])
            acts.append(((a * _sigmoid_tanh(a)) * u * gate).astype(BF16))
        y = jnp.dot(jnp.concatenate(acts, axis=1), wd_ref[...], preferred_element_type=F32)
        y_ref[...] = y.astype(y_ref.dtype)


def _ffn(grp, valid, hs, wg, wu, wd, layer):
    rows, width = hs.shape
    d = width - LANES
    epg = EXPERTS_PER_GROUP
    once = pl.Buffered(1)
    w_idx = lambda i, grp, valid: (layer, grp[i], 0, 0)
    grid_spec = pltpu.PrefetchScalarGridSpec(
        num_scalar_prefetch=2,
        grid=(rows // MOE_TILE,),
        in_specs=[pl.BlockSpec((MOE_TILE, width), lambda i, grp, valid: (i, 0)),
                  pl.BlockSpec((None, epg, d, D_FF_EXPERT), w_idx, pipeline_mode=once),
                  pl.BlockSpec((None, epg, d, D_FF_EXPERT), w_idx, pipeline_mode=once),
                  pl.BlockSpec((None, epg, D_FF_EXPERT, d), w_idx, pipeline_mode=once)],
        out_specs=pl.BlockSpec((MOE_TILE, d), lambda i, grp, valid: (i, 0)),
        scratch_shapes=[pltpu.VMEM((d, epg * D_FF_EXPERT), BF16), pltpu.VMEM((d, epg * D_FF_EXPERT), BF16),
                        pltpu.VMEM((epg * D_FF_EXPERT, d), BF16)])
    return pl.pallas_call(
        _ffn_kernel,
        out_shape=jax.ShapeDtypeStruct((rows, d), BF16),
        grid_spec=grid_spec,
        compiler_params=_cparams(("arbitrary",)),
        name="moe_ffn",
    )(grp, valid, hs, wg, wu, wd)


def _combine_kernel(seg_ref, x1_ref, slot_ref, gate2_ref, ys_ref, *rest, fnet):
    if fnet:
        shift_ref, scale_ref, gain_ref, cs_ref, o_ref, y1_ref, y2_ref, ybuf, sem = rest
    else:
        o_ref, ybuf, sem = rest
    i = pl.program_id(0)
    nt = pl.num_programs(0)
    tm = x1_ref.shape[0]
    cur = i % 2

    def seg_copies(tile, enable, buf):
        def seg_copy(s0, d0, rows):
            return (pltpu.make_async_copy(ys_ref.at[pl.ds(d0, rows)], ybuf.at[buf, pl.ds(s0, rows)], sem.at[buf]),)
        return _segment_copies(seg_ref, tile, enable, seg_copy)

    @pl.when(i == 0)
    def _():
        ybuf[...] = jnp.zeros(ybuf.shape, ybuf.dtype)
        _start_copies(seg_copies(i, i == 0, cur))

    nxt = jnp.minimum(i + 1, nt - 1)
    _start_copies(seg_copies(nxt, i + 1 < nt, 1 - cur))
    _wait_copies(seg_copies(i, i >= 0, cur))
    slot = slot_ref[0:1, :]
    perm = jnp.where(lax.broadcasted_iota(jnp.int32, (MOE_CROWS, tm), 0).astype(F32) == slot, 1.0, 0.0)
    y = lax.dot_general(perm.astype(BF16), ybuf[cur], (((0,), (0,)), ((), ())), preferred_element_type=F32)
    x = x1_ref[...] + gate2_ref[0] * y
    o_ref[...] = x
    if fnet:
        _fnet_channel_tile(x, shift_ref, scale_ref, gain_ref, cs_ref, y1_ref, y2_ref)


def _combine(seg_tab, x1, slot, gate2, ys, tokens_per_batch, fnet=None):
    t, d = x1.shape
    tpb = tokens_per_batch // MOE_TILE
    per_b = lambda i, *_: (i // tpb, 0, 0)
    full = lambda i, *_: (0, 0)
    in_specs = [pl.BlockSpec((MOE_TILE, d), lambda i, *_: (i, 0)),
                pl.BlockSpec((SUBLANES, MOE_TILE), lambda i, *_: (0, i)),
                pl.BlockSpec((1, 1, d), per_b),
                pl.BlockSpec(memory_space=pl.ANY)]
    out_shape = [jax.ShapeDtypeStruct((t, d), F32)]
    out_specs = [pl.BlockSpec((MOE_TILE, d), lambda i, *_: (i, 0))]
    args = [seg_tab, x1, slot, gate2, ys]
    if fnet is not None:
        shift, scale, gain, cs = fnet
        gw = d // FNET_GROUPS
        in_specs += [pl.BlockSpec((1, 1, d), per_b), pl.BlockSpec((1, 1, d), per_b), pl.BlockSpec((1, d), full),
                     pl.BlockSpec(cs.shape, full)]
        out_shape += [jax.ShapeDtypeStruct((FNET_GROUPS, t, gw), BF16)] * 2
        out_specs += [pl.BlockSpec((FNET_GROUPS, MOE_TILE, gw), lambda i, *_: (0, i, 0))] * 2
        args += [shift, scale, gain, cs]
    grid_spec = pltpu.PrefetchScalarGridSpec(
        num_scalar_prefetch=1,
        grid=(t // MOE_TILE,),
        in_specs=in_specs,
        out_specs=out_specs,
        scratch_shapes=[pltpu.VMEM((2, MOE_CROWS, d), BF16), pltpu.SemaphoreType.DMA((2,))])
    out = pl.pallas_call(
        functools.partial(_combine_kernel, fnet=fnet is not None),
        out_shape=out_shape,
        grid_spec=grid_spec,
        compiler_params=_cparams(("arbitrary",)),
        name="moe_combine_fnet" if fnet is not None else "moe_combine",
    )(*args)
    return out if fnet is not None else out[0]


def _grouped_moe(h2, route, x1, gate2, wg, wu, wd, layer, tri):
    t = h2.shape[0]
    gid = route[ROUTE_GID_ROW].astype(jnp.int32)
    seg_tab, tail_tab, grp, valid = _moe_tables(gid, t)
    slot, hs = _dispatch(seg_tab, tail_tab, h2, route, tri)
    ys = _ffn(grp, valid, hs, wg, wu, wd, layer)
    return seg_tab, x1, slot, gate2, ys


def _fnet_channel_tile(x, shift_ref, scale_ref, gain_ref, cs_ref, y1_ref, y2_ref):
    h = _norm_modulate(x, gain_ref[...], shift_ref[0], scale_ref[0]).astype(BF16)
    gw = D_MODEL // FNET_GROUPS
    for g in range(FNET_GROUPS):
        y = jnp.dot(h[:, g * gw:(g + 1) * gw], cs_ref[...], preferred_element_type=F32)
        y1_ref[g] = y[:, :gw].astype(y1_ref.dtype)
        y2_ref[g] = y[:, gw:].astype(y2_ref.dtype)


def _fnet_chan_kernel(x_ref, shift_ref, scale_ref, gain_ref, cs_ref, y1_ref, y2_ref):
    _fnet_channel_tile(x_ref[...], shift_ref, scale_ref, gain_ref, cs_ref, y1_ref, y2_ref)


def _fnet_channel(x2d, shift, scale, gain, cs, tokens_per_batch, tm):
    t, d = x2d.shape
    tpb = tokens_per_batch // tm
    gw = d // FNET_GROUPS
    full = lambda i: (0, 0)
    out = jax.ShapeDtypeStruct((FNET_GROUPS, t, gw), BF16)
    out_spec = pl.BlockSpec((FNET_GROUPS, tm, gw), lambda i: (0, i, 0))
    return pl.pallas_call(
        _fnet_chan_kernel,
        out_shape=[out, out],
        grid=(t // tm,),
        in_specs=[pl.BlockSpec((tm, d), lambda i: (i, 0)),
                  pl.BlockSpec((1, 1, d), lambda i: (i // tpb, 0, 0)),
                  pl.BlockSpec((1, 1, d), lambda i: (i // tpb, 0, 0)),
                  pl.BlockSpec((1, d), full),
                  pl.BlockSpec(cs.shape, full)],
        out_specs=[out_spec, out_spec],
        compiler_params=_cparams(("arbitrary",)),
        name="fnet_channel",
    )(x2d, shift, scale, gain, cs)


FFT_J = SUBLANES


def _half_rows(m):
    hc = m // 2 + 1
    return hc, -(-FFT_J * hc // ROW_ALIGN) * ROW_ALIGN


def _mirror(lo, hi, hc, m):
    return jnp.concatenate([lo[:FFT_J * hc]] + [hi[FFT_J * c:FFT_J * (c + 1)] for c in range(m - hc, 0, -1)], axis=0)


def _fnet_pos_kernel(y1_ref, y2_ref, lr_ref, ls_ref, cs_ref, sn_ref, o_ref, z1, z2, a_re, a_im, *, n):
    r1 = n // GRID_W
    gw = y1_ref.shape[2]
    hc_r, part_r = _half_rows(r1)
    hc_s, part_s = _half_rows(GRID_W)
    z1[...] = y1_ref[0].astype(F32)
    z2[...] = y2_ref[0].astype(F32)

    def stage_r(sb, carry):
        s0 = pl.multiple_of(sb * FFT_J, FFT_J)
        rhs1 = jnp.concatenate([z1[pl.ds(GRID_W * r + s0, FFT_J), :] for r in range(r1)], axis=0).astype(BF16)
        rhs2 = jnp.concatenate([z2[pl.ds(GRID_W * r + s0, FFT_J), :] for r in range(r1)], axis=0).astype(BF16)
        p = jnp.dot(lr_ref[...], rhs1, preferred_element_type=F32)
        q = jnp.dot(lr_ref[...], rhs2, preferred_element_type=F32)
        pc, ps = p[:part_r], p[part_r:]
        qc, qs = q[:part_r], q[part_r:]
        re = _mirror(pc - qs, pc + qs, hc_r, r1)
        nim = _mirror(qc + ps, qc - ps, hc_r, r1)
        cs = jnp.concatenate([cs_ref[sb]] * (gw // LANES), axis=1)
        sn = jnp.concatenate([sn_ref[sb]] * (gw // LANES), axis=1)
        tre = re * cs - nim * sn
        tnim = re * sn + nim * cs
        for c in range(r1):
            a_re[pl.ds(GRID_W * c + s0, FFT_J), :] = tre[FFT_J * c:FFT_J * (c + 1)]
            a_im[pl.ds(GRID_W * c + s0, FFT_J), :] = tnim[FFT_J * c:FFT_J * (c + 1)]
        return carry
    lax.fori_loop(0, GRID_W // FFT_J, stage_r, 0)

    cblk = FFT_J * GRID_W

    def stage_s(cb, carry):
        c0 = pl.multiple_of(cb * cblk, cblk)
        u = jnp.dot(ls_ref[0], a_re[pl.ds(c0, cblk), :].astype(BF16), preferred_element_type=F32)
        v = jnp.dot(ls_ref[1], a_im[pl.ds(c0, cblk), :].astype(BF16), preferred_element_type=F32)
        lo = u - v
        hi = u + v
        k0 = pl.multiple_of(cb * FFT_J, FFT_J)
        for d in range(hc_s):
            z1[pl.ds(r1 * d + k0, FFT_J), :] = lo[FFT_J * d:FFT_J * (d + 1)]
        for d in range(1, GRID_W - hc_s + 1):
            z1[pl.ds(r1 * (GRID_W - d) + k0, FFT_J), :] = hi[FFT_J * d:FFT_J * (d + 1)]
        return carry
    lax.fori_loop(0, r1 // FFT_J, stage_s, 0)
    o_ref[0] = z1[...].astype(o_ref.dtype)


def _fnet_position(y1, y2, lr, ls, tw_cos, tw_sin, batch, n):
    groups, t, gw = y1.shape
    full2 = lambda b, g: (0, 0)
    full3 = lambda b, g: (0, 0, 0)
    scratch = pltpu.VMEM((n, gw), F32)
    return pl.pallas_call(
        functools.partial(_fnet_pos_kernel, n=n),
        out_shape=jax.ShapeDtypeStruct((groups, t, gw), BF16),
        grid=(batch, groups),
        in_specs=[pl.BlockSpec((1, n, gw), lambda b, g: (g, b, 0)),
                  pl.BlockSpec((1, n, gw), lambda b, g: (g, b, 0)),
                  pl.BlockSpec(lr.shape, full2),
                  pl.BlockSpec(ls.shape, full3),
                  pl.BlockSpec(tw_cos.shape, full3),
                  pl.BlockSpec(tw_sin.shape, full3)],
        out_specs=pl.BlockSpec((1, n, gw), lambda b, g: (g, b, 0)),
        scratch_shapes=[scratch, scratch, scratch, scratch],
        compiler_params=_cparams(("arbitrary", "arbitrary")),
        name="fnet_position",
    )(y1, y2, lr, ls, tw_cos, tw_sin)


def _dft_tables(n):
    assert n % (GRID_W * FFT_J) == 0
    gw = D_MODEL // FNET_GROUPS
    j = np.arange(gw)
    ang = 2.0 * np.pi * ((j[:, None] * j[None, :]) % gw) / gw
    cs = np.concatenate([np.cos(ang), np.sin(ang)], axis=1) / np.sqrt(gw)
    r1 = n // GRID_W
    assert r1 % 2 == 0
    eye = np.eye(FFT_J)
    scale = float(n) ** -0.25
    a = np.arange(r1)
    hc_r, part_r = _half_rows(r1)
    ang_r = 2.0 * np.pi * ((a[:hc_r, None] * a[None, :]) % r1) / r1
    lr = np.zeros((2 * part_r, FFT_J * r1))
    lr[:FFT_J * hc_r] = np.kron(np.cos(ang_r), eye) * scale
    lr[part_r:part_r + FFT_J * hc_r] = np.kron(np.sin(ang_r), eye) * scale
    s = np.arange(GRID_W)
    hc_s, part_s = _half_rows(GRID_W)
    ang_s = 2.0 * np.pi * ((s[:hc_s, None] * s[None, :]) % GRID_W) / GRID_W
    ls = np.zeros((2, part_s, FFT_J * GRID_W))
    ls[0, :FFT_J * hc_s] = np.einsum("ds,cC->dcCs", np.cos(ang_s), eye).reshape(FFT_J * hc_s, FFT_J * GRID_W) * scale
    ls[1, :FFT_J * hc_s] = np.einsum("ds,cC->dcCs", np.sin(ang_s), eye).reshape(FFT_J * hc_s, FFT_J * GRID_W) * scale
    sb = np.arange(GRID_W // FFT_J)
    s_of = sb[:, None, None] * FFT_J + np.arange(FFT_J)[None, None, :]
    ang_t = 2.0 * np.pi * ((s_of * a[None, :, None]) % n) / n
    ang_t = ang_t.reshape(len(sb), r1 * FFT_J, 1)
    tw_cos = jnp.broadcast_to(jnp.asarray(np.cos(ang_t), F32), (len(sb), r1 * FFT_J, LANES))
    tw_sin = jnp.broadcast_to(jnp.asarray(np.sin(ang_t), F32), (len(sb), r1 * FFT_J, LANES))
    return jnp.asarray(cs, BF16), jnp.asarray(lr, BF16), jnp.asarray(ls, BF16), tw_cos, tw_sin


def kernel(x, c, ctx, c_ctx, ada_w, ada_b, norm_mix, norm_ffn, mix_w_in, mix_w_out, na_q_norm, na_k_norm, na_rpb,
           lru_conv_w, lru_conv_b, lru_gate_r_w, lru_gate_r_b, lru_gate_i_w, lru_gate_i_b, lru_lambda,
           fnet_w_out, router_w, router_bias, moe_w_gate, moe_w_up, moe_w_down):
    batch, n, d = x.shape
    ctx_len = ctx.shape[1]
    depth = ada_w.shape[0]
    rows = n // GRID_W
    assert d == D_MODEL and n % (GRID_W * NA_QROWS) == 0 and rows >= 4 * NA_QROWS
    assert n % MOE_TILE == 0
    t = batch * n
    tm = 512
    tri = jnp.asarray(np.triu(np.ones((MOE_TILE, MOE_TILE))), BF16)

    r_pad = -(-(batch + 1) // SUBLANES) * SUBLANES
    c_rows = jnp.concatenate([c, c_ctx[None, :], jnp.zeros((r_pad - batch - 1, d), c.dtype)], axis=0)
    mod = _modulation(c_rows, ada_w, ada_b)

    def mod_slices(layer):
        m = mod[layer, :batch].reshape(batch, 1, 6, d)
        return [m[:, :, i, :] for i in range(6)]

    rw_cat = jnp.pad(router_w.astype(F32), ((0, 0), (0, 2 * LANES - N_EXPERTS))).astype(BF16)
    rbias = router_bias.reshape(N_EXPERTS, 1).astype(F32)
    x2d = x.reshape(t, d)
    ctx2d = ctx.reshape(batch * ctx_len, d)

    pending = None
    for layer in range(depth):
        li = layer // 2
        shift1, scale1, gate1, shift2, scale2, gate2 = mod_slices(layer)
        gain_mix = norm_mix[layer].reshape(1, d)
        gain_ffn = norm_ffn[layer].reshape(1, d)
        if layer % 2 == 0:
            if pending is not None:
                x2d = _combine(*pending, n)
            w_in = mix_w_in[li].astype(BF16)
            ind = jnp.asarray(np.kron(np.eye(NA_HEADS // 2), np.ones((HEAD_DIM, HEAD_DIM))), BF16)
            qg = (jnp.tile(na_q_norm[li], NA_HEADS) * (HEAD_DIM ** -0.5 * LOG2_E)).reshape(1, NA_WIDTH).astype(F32)
            kg = jnp.tile(na_k_norm[li], NA_HEADS).reshape(1, NA_WIDTH).astype(F32)
            q, k, v, xb, gb = _inproj(x2d, shift1, scale1, gain_mix, w_in, ind, qg, kg,
                                      ("q", "k", "v", "x", "g"), n, 2 * tm)
            mctx = mod[layer, batch, :2 * d]
            shift_c = jnp.broadcast_to(mctx[:d], (batch, 1, d))
            scale_c = jnp.broadcast_to(mctx[d:], (batch, 1, d))
            k_c, v_c, xb_c = _inproj(ctx2d, shift_c, scale_c, gain_mix, w_in[:, NA_WIDTH:4 * NA_WIDTH], ind, qg, kg,
                                     ("k", "v", "x"), ctx_len, ctx_len)
            bias = _na_bias_tables(na_rpb[li], rows)
            attn = _attention(q, k, v, k_c, v_c, bias, batch, n, ctx_len)
            wcat, gbias = _lru_gate_weights(lru_gate_r_w[li], lru_gate_r_b[li], lru_gate_i_w[li], lru_gate_i_b[li])
            lru = _lru(xb, gb, xb_c, lru_conv_w[li].astype(F32), lru_conv_b[li].reshape(1, LRU_WIDTH).astype(F32),
                       wcat, gbias, lru_lambda[li].astype(F32), batch, n, ctx_len)
            parts, w_out = [attn, lru], mix_w_out[li].astype(BF16)
        else:
            cs, lr, ls, tw_cos, tw_sin = _dft_tables(n)
            if pending is not None:
                x2d, y1, y2 = _combine(*pending, n, fnet=(shift1, scale1, gain_mix, cs))
            else:
                y1, y2 = _fnet_channel(x2d, shift1, scale1, gain_mix, cs, n, tm)
            parts, w_out = [_fnet_position(y1, y2, lr, ls, tw_cos, tw_sin, batch, n)], fnet_w_out[li].astype(BF16)
        x1, h2, route = _post_mixer(parts, w_out, x2d, gate1, shift2, scale2, gain_ffn, rw_cat, rbias, n,
                                    2 * POST_SUBTILE)
        pending = _grouped_moe(h2, route, x1, gate2, moe_w_gate.astype(F32), moe_w_up.astype(F32),
                               moe_w_down.astype(F32), layer, tri)
    return _combine(*pending, n).reshape(batch, n, d)
```

```python
import functools

import numpy as np
import jax
import jax.numpy as jnp
from jax import lax
from jax.experimental import pallas as pl
from jax.experimental.pallas import tpu as pltpu

F32 = jnp.float32
BF16 = jnp.bfloat16
HIGHEST = lax.Precision.HIGHEST

D_MODEL = 1024
GRID_W = 64
HEAD_DIM = 64
NA_HEADS = 8
NA_WIDTH = NA_HEADS * HEAD_DIM
NA_WIN_ROWS = 8
NA_WIN_COLS = 16
LRU_WIDTH = 512
LRU_BLOCK = 64
LRU_C = 8.0
FNET_GROUPS = 4
N_EXPERTS = 16
EXPERTS_PER_GROUP = 4
N_EXPERT_GROUPS = 4
D_FF_EXPERT = 512
RMS_EPS = 1e-6
MASK_VALUE = -1e30
LOG2_E = 1.4426950408889634

V7X_VMEM_LIMIT_BYTES = 56 * 1024 * 1024
LANES = 128
SUBLANES = 8

NA_QROWS = 4
NA_KROWS = NA_QROWS + NA_WIN_ROWS - 1
NA_QBLK = NA_QROWS * GRID_W
NA_KBLK = NA_KROWS * GRID_W
NA_STEP_BLOCKS = 2

LRU_CHUNK = LANES
LRU_TROWS = 512

ROUTE_GID_ROW = EXPERTS_PER_GROUP
MOE_TILE = 512
ROW_ALIGN = 16
MOE_CROWS = MOE_TILE + N_EXPERT_GROUPS * ROW_ALIGN
MOE_SEG_BITS = (MOE_TILE // ROW_ALIGN).bit_length()
MOE_TAIL_BITS = (MOE_TILE // ROW_ALIGN - 1).bit_length()


def _sigmoid(x):
    return 1.0 / (1.0 + jnp.exp(-x))


def _sigmoid_tanh(x):
    return 0.5 + 0.5 * jnp.tanh(0.5 * x)


def _cparams(sem, vmem=V7X_VMEM_LIMIT_BYTES):
    return pltpu.CompilerParams(dimension_semantics=sem, vmem_limit_bytes=vmem)


def _mod_kernel(c_ref, w_ref, b_ref, o_ref):
    c = c_ref[...]
    s = c * _sigmoid(c)
    o_ref[0] = jnp.dot(s, w_ref[0], precision=HIGHEST, preferred_element_type=F32) + b_ref[0]


def _modulation(c_rows, ada_w, ada_b):
    depth, d, n6 = ada_w.shape
    r = c_rows.shape[0]
    tn = 1536
    return pl.pallas_call(
        _mod_kernel,
        out_shape=jax.ShapeDtypeStruct((depth, r, n6), F32),
        grid=(depth, n6 // tn),
        in_specs=[pl.BlockSpec((r, d), lambda l, j: (0, 0)),
                  pl.BlockSpec((1, d, tn), lambda l, j: (l, 0, j)),
                  pl.BlockSpec((1, 1, tn), lambda l, j: (l, 0, j))],
        out_specs=pl.BlockSpec((1, r, tn), lambda l, j: (l, 0, j)),
        compiler_params=_cparams(("arbitrary", "arbitrary")),
        name="adaln_mod",
    )(c_rows, ada_w, ada_b.reshape(depth, 1, n6))


def _norm_modulate(x, gain, shift, scale):
    ms = jnp.mean(x * x, axis=-1, keepdims=True)
    y = x * lax.rsqrt(ms + RMS_EPS) * gain
    return y * (1.0 + scale) + shift


def _inproj_kernel(x_ref, shift_ref, scale_ref, gain_ref, w_ref, ind_ref, qg_ref, kg_ref, *out_refs, segs):
    h = _norm_modulate(x_ref[...], gain_ref[...], shift_ref[0], scale_ref[0]).astype(BF16)
    for s, (kind, o_ref) in enumerate(zip(segs, out_refs)):
        z = jnp.dot(h, w_ref[:, s * NA_WIDTH:(s + 1) * NA_WIDTH], preferred_element_type=F32)
        if kind in ("q", "k"):
            zz = (z * z).astype(BF16)
            hw = ind_ref.shape[0]
            ms = jnp.concatenate([jnp.dot(zz[:, i * hw:(i + 1) * hw], ind_ref[...], preferred_element_type=F32)
                                  for i in range(NA_WIDTH // hw)], axis=1) * (1.0 / HEAD_DIM)
            g = qg_ref[...] if kind == "q" else kg_ref[...]
            z = z * lax.rsqrt(ms + RMS_EPS) * g
        if kind in ("x", "g"):
            for c in range(LRU_WIDTH // LRU_CHUNK):
                o_ref[c] = z[:, c * LRU_CHUNK:(c + 1) * LRU_CHUNK].astype(o_ref.dtype)
        else:
            o_ref[...] = z.astype(o_ref.dtype)


def _inproj(x2d, shift, scale, gain, w, ind, qg, kg, segs, tokens_per_batch, tm):
    t, d = x2d.shape
    tpb = tokens_per_batch // tm
    dt = {"q": BF16, "k": BF16, "v": BF16, "x": F32, "g": F32}
    full = lambda i: (0, 0)
    nch = LRU_WIDTH // LRU_CHUNK

    def out_shape(kind):
        shape = (nch, t, LRU_CHUNK) if kind in ("x", "g") else (t, NA_WIDTH)
        return jax.ShapeDtypeStruct(shape, dt[kind])

    def out_spec(kind):
        if kind in ("x", "g"):
            return pl.BlockSpec((nch, tm, LRU_CHUNK), lambda i: (0, i, 0))
        return pl.BlockSpec((tm, NA_WIDTH), lambda i: (i, 0))

    return pl.pallas_call(
        functools.partial(_inproj_kernel, segs=segs),
        out_shape=[out_shape(k) for k in segs],
        grid=(t // tm,),
        in_specs=[pl.BlockSpec((tm, d), lambda i: (i, 0)),
                  pl.BlockSpec((1, 1, d), lambda i: (i // tpb, 0, 0)),
                  pl.BlockSpec((1, 1, d), lambda i: (i // tpb, 0, 0)),
                  pl.BlockSpec((1, d), full),
                  pl.BlockSpec(w.shape, full),
                  pl.BlockSpec(ind.shape, full),
                  pl.BlockSpec((1, NA_WIDTH), full),
                  pl.BlockSpec((1, NA_WIDTH), full)],
        out_specs=[out_spec(k) for k in segs],
        compiler_params=_cparams(("arbitrary",)),
        name="inproj_" + "".join(segs),
    )(x2d, shift, scale, gain, w, ind, qg, kg)


def _na_bias_tables(rpb, rows):
    kr = NA_WIN_ROWS
    rb_count = rows // NA_QROWS
    cq = np.arange(GRID_W)
    ck = np.arange(GRID_W)
    col_start = np.clip(cq - NA_WIN_COLS // 2, 0, GRID_W - NA_WIN_COLS)
    valid_c = (ck[None, :] >= col_start[:, None]) & (ck[None, :] < col_start[:, None] + NA_WIN_COLS)
    dc = np.clip(ck[None, :] - cq[:, None], 1 - NA_WIN_COLS, NA_WIN_COLS - 1) + (NA_WIN_COLS - 1)
    n_dr, n_dc = 2 * NA_WIN_ROWS - 1, 2 * NA_WIN_COLS - 1
    sel_c = (dc[:, :, None] == np.arange(n_dc)) & valid_c[:, :, None]
    blocks = jnp.einsum("hrc,qkc->hrqk", rpb.astype(F32), jnp.asarray(sel_c, F32), precision=HIGHEST)
    blocks = blocks + jnp.asarray(np.where(valid_c, 0.0, MASK_VALUE), F32)
    blocks = jnp.concatenate([blocks, jnp.full((NA_HEADS, 1, GRID_W, GRID_W), MASK_VALUE, F32)], axis=1)
    blocks = blocks * LOG2_E
    which = []
    for rb in (0, 1, rb_count - 1):
        r = rb * NA_QROWS + np.arange(NA_QROWS)
        ks = int(np.clip(rb * NA_QROWS - kr // 2, 0, rows - NA_KROWS))
        key_r = ks + np.arange(NA_KROWS)
        row_start = np.clip(r - kr // 2, 0, rows - kr)
        valid_r = (key_r[None, :] >= row_start[:, None]) & (key_r[None, :] < row_start[:, None] + kr)
        dr = np.clip(key_r[None, :] - r[:, None] + (NA_WIN_ROWS - 1), 0, n_dr - 1)
        which.append(np.where(valid_r, dr, n_dr))
    return _na_bias_assemble(blocks, which)


def _na_bias_kernel(blk_ref, o_ref, *, which):
    for t, table in enumerate(which):
        @pl.when(pl.program_id(0) == t)
        def _():
            for i in range(NA_QROWS):
                row = jnp.concatenate([blk_ref[0, int(table[i, j])] for j in range(NA_KROWS)], axis=1)
                o_ref[0, 0, i * GRID_W:(i + 1) * GRID_W, :] = row


def _na_bias_assemble(blocks, which):
    heads, nblk = blocks.shape[:2]
    return pl.pallas_call(
        functools.partial(_na_bias_kernel, which=which),
        out_shape=jax.ShapeDtypeStruct((len(which), heads, NA_QBLK, NA_KBLK), F32),
        grid=(len(which), heads),
        in_specs=[pl.BlockSpec((1, nblk, GRID_W, GRID_W), lambda t, h: (h, 0, 0, 0))],
        out_specs=pl.BlockSpec((1, 1, NA_QBLK, NA_KBLK), lambda t, h: (t, h, 0, 0)),
        compiler_params=_cparams(("arbitrary", "arbitrary")),
        name="na_bias",
    )(blocks)


def _attn_kernel(q_ref, k_ref, v_ref, kc_ref, vc_ref, bias_ref, o_ref, *, rows):
    last = rows // NA_QROWS - 1
    nt = (((1,), (1,)), ((), ()))
    ctx_len = kc_ref.shape[0]
    low_half = lax.broadcasted_iota(jnp.int32, (NA_QBLK, LANES), 1) < HEAD_DIM
    for blk in range(NA_STEP_BLOCKS):
        rb = pl.program_id(1) * NA_STEP_BLOCKS + blk
        ks = jnp.clip(rb * NA_QROWS - NA_WIN_ROWS // 2, 0, rows - NA_KROWS)
        kstart = pl.multiple_of(ks * GRID_W, GRID_W)
        geom = jnp.where(rb == 0, 0, jnp.where(rb == last, 2, 1))
        qrows = slice(blk * NA_QBLK, (blk + 1) * NA_QBLK)
        for pair in range(NA_HEADS * HEAD_DIM // LANES):
            ls = slice(pair * LANES, (pair + 1) * LANES)
            q2 = q_ref[qrows, ls]
            k_all = jnp.concatenate([kc_ref[:, ls], k_ref[pl.ds(kstart, NA_KBLK), ls]], axis=0)
            v_all = jnp.concatenate([vc_ref[:, ls], v_ref[pl.ds(kstart, NA_KBLK), ls]], axis=0)
            outs = []
            for half in range(2):
                qh = jnp.where(low_half == (half == 0), q2, jnp.zeros_like(q2))
                s = lax.dot_general(qh, k_all, nt, preferred_element_type=F32)
                s = jnp.concatenate([s[:, :ctx_len], s[:, ctx_len:] + bias_ref[geom, 2 * pair + half]], axis=1)
                m = jnp.max(s, axis=-1, keepdims=True)
                p = jnp.exp2(s - m)
                l = jnp.sum(p, axis=-1, keepdims=True)
                outs.append(jnp.dot(p.astype(BF16), v_all, preferred_element_type=F32) / l)
            o_ref[qrows, ls] = jnp.where(low_half, outs[0], outs[1]).astype(o_ref.dtype)


def _attention(q, k, v, kc, vc, bias, batch, n, ctx_len):
    rows = n // GRID_W
    rbc = rows // (NA_QROWS * NA_STEP_BLOCKS)
    qblk = NA_QBLK * NA_STEP_BLOCKS
    return pl.pallas_call(
        functools.partial(_attn_kernel, rows=rows),
        out_shape=jax.ShapeDtypeStruct((batch * n, NA_WIDTH), BF16),
        grid=(batch, rbc),
        in_specs=[pl.BlockSpec((qblk, NA_WIDTH), lambda b, rb: (b * rbc + rb, 0)),
                  pl.BlockSpec((n, NA_WIDTH), lambda b, rb: (b, 0)),
                  pl.BlockSpec((n, NA_WIDTH), lambda b, rb: (b, 0)),
                  pl.BlockSpec((ctx_len, NA_WIDTH), lambda b, rb: (b, 0)),
                  pl.BlockSpec((ctx_len, NA_WIDTH), lambda b, rb: (b, 0)),
                  pl.BlockSpec(bias.shape, lambda b, rb: (0, 0, 0, 0), pipeline_mode=pl.Buffered(1))],
        out_specs=pl.BlockSpec((qblk, NA_WIDTH), lambda b, rb: (b * rbc + rb, 0)),
        compiler_params=_cparams(("arbitrary", "arbitrary")),
        name="na_attention",
    )(q, k, v, kc, vc, bias)


SCAN_GROUPS = 2
SCAN_CHUNKS = SCAN_GROUPS * SUBLANES


def _scan_pitch(n):
    p = -(-n // SCAN_CHUNKS)
    while p % 8 != 4:
        p += 1
    return p


NEG_LOG2_E = -LOG2_E


def _lru_coeff_tile(half_xc, zh, half_bias, k, d):
    c = LRU_CHUNK
    t_r = jnp.tanh(zh[:, (2 * d) * c:(2 * d + 1) * c] + half_bias[:, (2 * d) * c:(2 * d + 1) * c])
    t_i = jnp.tanh(zh[:, (2 * d + 1) * c:(2 * d + 2) * c] + half_bias[:, (2 * d + 1) * c:(2 * d + 2) * c])
    neg_log_a = k[d:d + 1, :] * (1.0 + t_r)
    a = jnp.exp2(neg_log_a * NEG_LOG2_E)
    one_minus_a2 = jnp.tanh(neg_log_a) * (a * a + 1.0)
    root = jnp.where(one_minus_a2 > 0.0, one_minus_a2 * lax.rsqrt(one_minus_a2), 0.0)
    return a, root * (half_xc + half_xc * t_i)


def _conv_tile(xpad, t0, w, b, rows):
    acc = b + w[0:1, :] * xpad[pl.ds(t0 + SUBLANES - 2, rows), :]
    acc = acc + w[1:2, :] * xpad[pl.ds(t0 + SUBLANES - 1, rows), :]
    acc = acc + w[2:3, :] * xpad[pl.ds(t0 + SUBLANES, rows), :]
    return acc + w[3:4, :] * xpad[pl.ds(t0 + SUBLANES + 1, rows), :]


SCAN_UNROLL = 4


def _group_rows(j, g, pitch):
    return pl.ds(g * SUBLANES * pitch + j, SUBLANES, stride=pitch)


def _chunk_totals(af_ref, bf_ref, ab_ref, bb_ref, pitch):
    def body(j, carry):
        jb = pitch - 1 - j
        out = []
        for g in range(SCAN_GROUPS):
            pf, hf, pb, hb = carry[4 * g:4 * g + 4]
            af = af_ref[_group_rows(j, g, pitch), :]
            ab = ab_ref[_group_rows(jb, g, pitch), :]
            out += [af * pf, af * hf + bf_ref[_group_rows(j, g, pitch), :],
                    ab * pb, ab * hb + bb_ref[_group_rows(jb, g, pitch), :]]
        return tuple(out)
    one = jnp.ones((SUBLANES, LRU_CHUNK), F32)
    zero = jnp.zeros((SUBLANES, LRU_CHUNK), F32)
    res = lax.fori_loop(0, pitch, body, (one, zero, one, zero) * SCAN_GROUPS, unroll=SCAN_UNROLL)
    fwd = [(res[4 * g], res[4 * g + 1]) for g in range(SCAN_GROUPS)]
    bwd = [(res[4 * g + 2], res[4 * g + 3]) for g in range(SCAN_GROUPS)]
    return fwd, bwd


def _chunk_starts(totals, h0, reverse):
    row = lax.broadcasted_iota(jnp.int32, (SUBLANES, LRU_CHUNK), 0)
    starts = [jnp.zeros((SUBLANES, LRU_CHUNK), F32) for _ in range(SCAN_GROUPS)]
    state = h0
    order = range(SCAN_CHUNKS - 1, -1, -1) if reverse else range(SCAN_CHUNKS)
    for c in order:
        g, s = divmod(c, SUBLANES)
        p_end, h_end = totals[g]
        starts[g] = jnp.where(row == s, state, starts[g])
        state = p_end[s:s + 1, :] * state + h_end[s:s + 1, :]
    return starts, state


def _scan_write(af_ref, bf_ref, hf_ref, ab_ref, bb_ref, hb_ref, starts_f, starts_b, pitch):
    def body(j, carry):
        jb = pitch - 1 - j
        out = []
        for g in range(SCAN_GROUPS):
            hf, hb = carry[2 * g:2 * g + 2]
            hf = af_ref[_group_rows(j, g, pitch), :] * hf + bf_ref[_group_rows(j, g, pitch), :]
            hb = ab_ref[_group_rows(jb, g, pitch), :] * hb + bb_ref[_group_rows(jb, g, pitch), :]
            hf_ref[_group_rows(j, g, pitch), :] = hf
            hb_ref[_group_rows(jb, g, pitch), :] = hb
            out += [hf, hb]
        return tuple(out)
    init = tuple(v for g in range(SCAN_GROUPS) for v in (starts_f[g], starts_b[g]))
    lax.fori_loop(0, pitch, body, init, unroll=SCAN_UNROLL)


def _lru_kernel(x_ref, g_ref, xc_ref, cw_ref, cb_ref, w_ref, gb_ref, lam_ref, o_ref,
                xpad, a0, b0, a1, b1, h0s, h1s, ca0, cb0, ca1, cb1, *, n, ctx_len):
    pitch = _scan_pitch(n)
    cpitch = _scan_pitch(ctx_len)
    cw = cw_ref[...]
    cb = cb_ref[...]
    gbias = gb_ref[0]
    lam = lam_ref[...]
    sp = jnp.maximum(-lam, 0.0) + jnp.log1p(jnp.exp(-jnp.abs(lam)))
    k = (0.5 * LRU_C) * sp
    wcat = w_ref[0]
    zeros8 = jnp.zeros((SUBLANES, LRU_CHUNK), F32)

    def fill_coeffs(src_rows, total, length, trows, a_refs, b_refs):
        for d in range(2):
            a_refs[d][pl.ds(length, total - length), :] = jnp.ones((total - length, LRU_CHUNK), F32)
            b_refs[d][pl.ds(length, total - length), :] = jnp.zeros((total - length, LRU_CHUNK), F32)
        xpad[pl.ds(0, SUBLANES), :] = zeros8
        xpad[pl.ds(SUBLANES + length, SUBLANES), :] = zeros8
        xpad[pl.ds(SUBLANES, length), :] = src_rows

        def tile(t, carry):
            t0 = pl.multiple_of(t * trows, SUBLANES)
            xc = _conv_tile(xpad, t0, cw, cb, trows)
            zh = jnp.dot(xc.astype(BF16), wcat, preferred_element_type=F32)
            half_xc = 0.5 * xc
            for d in range(2):
                a, b = _lru_coeff_tile(half_xc, zh, gbias, k, d)
                a_refs[d][pl.ds(t0, trows), :] = a
                b_refs[d][pl.ds(t0, trows), :] = b
            return carry
        lax.fori_loop(0, length // trows, tile, 0)

    fill_coeffs(xc_ref[0], SCAN_CHUNKS * cpitch, ctx_len, ctx_len, (ca0, ca1), (cb0, cb1))
    zero_state = jnp.zeros((1, LRU_CHUNK), F32)
    fwd, bwd = _chunk_totals(ca0, cb0, ca1, cb1, cpitch)
    _, init_f = _chunk_starts(fwd, zero_state, reverse=False)
    _, init_b = _chunk_starts(bwd, zero_state, reverse=True)

    fill_coeffs(x_ref[0], SCAN_CHUNKS * pitch, n, LRU_TROWS, (a0, a1), (b0, b1))
    fwd, bwd = _chunk_totals(a0, b0, a1, b1, pitch)
    starts_f, _ = _chunk_starts(fwd, init_f, reverse=False)
    starts_b, _ = _chunk_starts(bwd, init_b, reverse=True)
    _scan_write(a0, b0, h0s, a1, b1, h1s, starts_f, starts_b, pitch)

    def out_tile(t, carry):
        t0 = pl.multiple_of(t * LRU_TROWS, SUBLANES)
        y = h0s[pl.ds(t0, LRU_TROWS), :] + h1s[pl.ds(t0, LRU_TROWS), :]
        g = g_ref[0, pl.ds(t0, LRU_TROWS), :]
        gelu = 0.5 * g * (1.0 + jnp.tanh(0.7978845608028654 * (g + 0.044715 * (g * g * g))))
        o_ref[0, pl.ds(t0, LRU_TROWS), :] = (gelu * y).astype(o_ref.dtype)
        return carry
    lax.fori_loop(0, n // LRU_TROWS, out_tile, 0)


def _lru(xb, gb, xb_ctx, conv_w, conv_b, wcat, gbias, lam, batch, n, ctx_len):
    nch = LRU_WIDTH // LRU_CHUNK
    pitch = _scan_pitch(n)
    cpitch = _scan_pitch(ctx_len)
    big = pltpu.VMEM((SCAN_CHUNKS * pitch, LRU_CHUNK), F32)
    small = pltpu.VMEM((SCAN_CHUNKS * cpitch, LRU_CHUNK), F32)
    return pl.pallas_call(
        functools.partial(_lru_kernel, n=n, ctx_len=ctx_len),
        out_shape=jax.ShapeDtypeStruct((nch, batch * n, LRU_CHUNK), BF16),
        grid=(batch, nch),
        in_specs=[pl.BlockSpec((1, n, LRU_CHUNK), lambda b, c: (c, b, 0)),
                  pl.BlockSpec((1, n, LRU_CHUNK), lambda b, c: (c, b, 0)),
                  pl.BlockSpec((1, ctx_len, LRU_CHUNK), lambda b, c: (c, b, 0)),
                  pl.BlockSpec((4, LRU_CHUNK), lambda b, c: (0, c)),
                  pl.BlockSpec((1, LRU_CHUNK), lambda b, c: (0, c)),
                  pl.BlockSpec((1, LRU_CHUNK, 4 * LRU_CHUNK), lambda b, c: (c, 0, 0)),
                  pl.BlockSpec((1, 1, 4 * LRU_CHUNK), lambda b, c: (c, 0, 0)),
                  pl.BlockSpec((2, LRU_CHUNK), lambda b, c: (0, c))],
        out_specs=pl.BlockSpec((1, n, LRU_CHUNK), lambda b, c: (c, b, 0)),
        scratch_shapes=[pltpu.VMEM((n + 2 * SUBLANES, LRU_CHUNK), F32),
                        big, big, big, big, big, big, small, small, small, small],
        compiler_params=_cparams(("arbitrary", "arbitrary")),
        name="rglru",
    )(xb, gb, xb_ctx, conv_w, conv_b, wcat, gbias, lam)


def _lru_gate_weights(w_r, b_r, w_i, b_i):
    nch = LRU_WIDTH // LRU_CHUNK
    bpc = LRU_CHUNK // LRU_BLOCK

    def dense(w):
        wc = w.reshape(nch, bpc, LRU_BLOCK, LRU_BLOCK)
        eye = jnp.eye(bpc, dtype=w.dtype)
        return jnp.einsum("cbij,bd->cbidj", wc, eye).reshape(nch, LRU_CHUNK, LRU_CHUNK)

    wcat = jnp.concatenate([dense(w_r[0]), dense(w_i[0]), dense(w_r[1]), dense(w_i[1])], axis=-1)
    chunk = lambda v: v.reshape(nch, 1, LRU_CHUNK)
    gbias = jnp.concatenate([chunk(b_r[0]), chunk(b_i[0]), chunk(b_r[1]), chunk(b_i[1])], axis=-1)
    return (0.5 * wcat).astype(BF16), (0.5 * gbias).astype(F32)


def _route(s, sel, route_ref):
    srow = [s[e:e + 1, :] for e in range(N_EXPERTS)]
    lrow = [sel[e:e + 1, :] for e in range(N_EXPERTS)]
    gscore = []
    for g in range(N_EXPERT_GROUPS):
        a = lrow[g * EXPERTS_PER_GROUP:(g + 1) * EXPERTS_PER_GROUP]
        best = a[0] + a[1]
        for i, j in ((0, 2), (0, 3), (1, 2), (1, 3), (2, 3)):
            best = jnp.maximum(best, a[i] + a[j])
        gscore.append(best)
    bg = jnp.zeros_like(gscore[0], dtype=jnp.int32)
    bv = gscore[0]
    for g in range(1, N_EXPERT_GROUPS):
        upd = gscore[g] > bv
        bg = jnp.where(upd, g, bg)
        bv = jnp.where(upd, gscore[g], bv)

    def pick(rows_):
        out = []
        for j in range(EXPERTS_PER_GROUP):
            v = rows_[j]
            for g in range(1, N_EXPERT_GROUPS):
                v = jnp.where(bg == g, rows_[g * EXPERTS_PER_GROUP + j], v)
            out.append(v)
        return out
    cand = pick(lrow)
    cs = pick(srow)
    i1 = jnp.zeros_like(bg)
    v1 = cand[0]
    w1 = cs[0]
    for j in range(1, EXPERTS_PER_GROUP):
        upd = cand[j] > v1
        i1 = jnp.where(upd, j, i1)
        v1 = jnp.where(upd, cand[j], v1)
        w1 = jnp.where(upd, cs[j], w1)
    i2 = jnp.full_like(bg, -1)
    v2 = jnp.full_like(v1, -jnp.inf)
    w2 = jnp.zeros_like(w1)
    for j in range(EXPERTS_PER_GROUP):
        upd = (i1 != j) & (cand[j] > v2)
        i2 = jnp.where(upd, j, i2)
        v2 = jnp.where(upd, cand[j], v2)
        w2 = jnp.where(upd, cs[j], w2)
    den = w1 + w2
    g1 = w1 / den
    g2 = w2 / den
    for j in range(EXPERTS_PER_GROUP):
        route_ref[j:j + 1, :] = jnp.where(i1 == j, g1, 0.0) + jnp.where(i2 == j, g2, 0.0)
    route_ref[ROUTE_GID_ROW:ROUTE_GID_ROW + 1, :] = bg.astype(F32)
    pad = SUBLANES - ROUTE_GID_ROW - 1
    route_ref[ROUTE_GID_ROW + 1:, :] = jnp.zeros((pad, bg.shape[1]), F32)


POST_SUBTILE = 512


def _post_kernel(*refs, n_parts):
    parts = refs[:n_parts]
    (w_ref, x_ref, gate_ref, shift_ref, scale_ref, gain_ref, rw_ref, rb_ref,
     x1_ref, h2_ref, route_ref) = refs[n_parts:]
    for sub in range(x_ref.shape[0] // POST_SUBTILE):
        rows = pl.ds(sub * POST_SUBTILE, POST_SUBTILE)
        pieces = []
        for p in parts:
            pieces += [p[c, rows, :] for c in range(p.shape[0])] if len(p.shape) == 3 else [p[rows, :]]
        mixed = jnp.concatenate(pieces, axis=-1) if len(pieces) > 1 else pieces[0]
        mix = jnp.dot(mixed, w_ref[...], preferred_element_type=F32)
        x1 = x_ref[rows, :] + gate_ref[0] * mix
        x1_ref[rows, :] = x1
        h2 = _norm_modulate(x1, gain_ref[...], shift_ref[0], scale_ref[0])
        h_hi = h2.astype(BF16)
        h2_ref[rows, :] = h_hi
        logits = jnp.dot(h_hi, rw_ref[...], preferred_element_type=F32)
        s = _sigmoid(logits[:, :LANES].T[:N_EXPERTS, :])
        _route(s, s + rb_ref[...], route_ref.at[:, rows])


def _post_mixer(parts, w, x2d, gate1, shift2, scale2, gain, rw_cat, rbias, tokens_per_batch, tm):
    t, d = x2d.shape
    tpb = tokens_per_batch // tm
    full = lambda i: (0, 0)
    per_b = lambda i: (i // tpb, 0, 0)

    def part_spec(p):
        if p.ndim == 3:
            return pl.BlockSpec((p.shape[0], tm, p.shape[2]), lambda i: (0, i, 0))
        return pl.BlockSpec((tm, p.shape[1]), lambda i: (i, 0))

    return pl.pallas_call(
        functools.partial(_post_kernel, n_parts=len(parts)),
        out_shape=[jax.ShapeDtypeStruct((t, d), F32), jax.ShapeDtypeStruct((t, d), BF16),
                   jax.ShapeDtypeStruct((SUBLANES, t), F32)],
        grid=(t // tm,),
        in_specs=[part_spec(p) for p in parts] + [
                  pl.BlockSpec(w.shape, full),
                  pl.BlockSpec((tm, d), lambda i: (i, 0)),
                  pl.BlockSpec((1, 1, d), per_b),
                  pl.BlockSpec((1, 1, d), per_b),
                  pl.BlockSpec((1, 1, d), per_b),
                  pl.BlockSpec((1, d), full),
                  pl.BlockSpec(rw_cat.shape, full),
                  pl.BlockSpec(rbias.shape, full)],
        out_specs=[pl.BlockSpec((tm, d), lambda i: (i, 0)),
                   pl.BlockSpec((tm, d), lambda i: (i, 0)),
                   pl.BlockSpec((SUBLANES, tm), lambda i: (0, i))],
        compiler_params=_cparams(("arbitrary",)),
        name="post_mixer",
    )(*parts, w, x2d, gate1, shift2, scale2, gain, rw_cat, rbias)


def _moe_layout(t):
    nt = t // MOE_TILE
    grid = -(-(t + N_EXPERT_GROUPS * (ROW_ALIGN - 1) * nt) // MOE_TILE) + N_EXPERT_GROUPS
    return nt, grid


def _moe_tables(gid, t):
    nt, grid = _moe_layout(t)
    ng = N_EXPERT_GROUPS
    per_tile = MOE_TILE // ROW_ALIGN
    onehot = (gid.reshape(nt, MOE_TILE, 1) == jnp.arange(ng, dtype=jnp.int32)).astype(jnp.int32)
    cnt = onehot.sum(axis=1)
    seg = (cnt + ROW_ALIGN - 1) // ROW_ALIGN
    src = jnp.cumsum(seg, axis=1) - seg
    fill = seg.sum(axis=0)
    ntile = (fill + per_tile - 1) // per_tile
    cum = jnp.cumsum(ntile)
    base = (cum - ntile) * per_tile
    dst = jnp.cumsum(seg, axis=0) - seg + base[None, :]
    seg_tab = jnp.concatenate([seg, src, dst], axis=1).reshape(-1).astype(jnp.int32)
    tail = (-fill) % per_tile
    tail_tab = jnp.concatenate([tail, fill + base, cum[-1:]]).astype(jnp.int32)
    i = jnp.arange(grid, dtype=jnp.int32)
    valid = i < cum[-1]
    ie = jnp.minimum(i, cum[-1] - 1)
    g_of = jnp.sum((ie[:, None] >= cum[None, :]).astype(jnp.int32), axis=1)
    return seg_tab, tail_tab, g_of.astype(jnp.int32), valid.astype(jnp.int32)


def _segment_copies(tab_ref, tile, enable, make_copy):
    ng = N_EXPERT_GROUPS
    base = jnp.maximum(tile, 0) * (3 * ng)
    out = []
    for g in range(ng):
        n = tab_ref[base + g]
        src = tab_ref[base + ng + g]
        dst = tab_ref[base + 2 * ng + g]
        for k in range(MOE_SEG_BITS - 1, -1, -1):
            done = (n >> (k + 1)) << (k + 1)
            rows = ROW_ALIGN << k
            s0 = pl.multiple_of((src + done) * ROW_ALIGN, ROW_ALIGN)
            d0 = pl.multiple_of((dst + done) * ROW_ALIGN, ROW_ALIGN)
            out.append((enable & (((n >> k) & 1) == 1), make_copy(s0, d0, rows)))
    return out


def _start_copies(pairs):
    for cond, copies in pairs:
        @pl.when(cond)
        def _():
            for c in copies:
                c.start()


def _wait_copies(pairs):
    for cond, copies in pairs:
        @pl.when(cond)
        def _():
            for c in copies:
                c.wait()


def _split_bf16x3(x):
    hi = x.astype(BF16).astype(F32)
    r1 = x - hi
    mid = r1.astype(BF16).astype(F32)
    lo = (r1 - mid).astype(BF16).astype(F32)
    return hi, mid, lo


def _dispatch_kernel(seg_ref, tail_ref, h_ref, route_ref, tri_ref, slot_ref, hs_ref, cbuf, zbuf, sem, *, nt):
    i = pl.program_id(0)
    tm, d = h_ref.shape
    ng = N_EXPERT_GROUPS
    cur = i % 2

    def seg_copies(tile, enable, buf):
        def seg_copy(s0, d0, rows):
            return (pltpu.make_async_copy(cbuf.at[buf, pl.ds(s0, rows)], hs_ref.at[pl.ds(d0, rows)], sem.at[buf]),)
        return _segment_copies(seg_ref, tile, enable, seg_copy)

    _wait_copies(seg_copies(i - 2, i >= 2, cur))

    route = route_ref[...]
    gid = route[ROUTE_GID_ROW:ROUTE_GID_ROW + 1, :]
    grp = lax.broadcasted_iota(jnp.int32, (SUBLANES, tm), 0).astype(F32)
    onehot = jnp.where(grp == gid, 1.0, 0.0)
    rank = jnp.dot(onehot.astype(BF16), tri_ref[...], preferred_element_type=F32)
    slot = jnp.zeros((1, tm), F32)
    for g in range(ng):
        start = (seg_ref[i * 3 * ng + ng + g] * ROW_ALIGN).astype(F32)
        slot = slot + onehot[g:g + 1, :] * (rank[g:g + 1, :] - 1.0 + start)
    slot_ref[...] = jnp.broadcast_to(slot, (SUBLANES, tm))
    perm = jnp.where(lax.broadcasted_iota(jnp.int32, (MOE_CROWS, tm), 0).astype(F32) == slot, 1.0, 0.0)
    perm = perm.astype(BF16)
    cbuf[cur, :, :d] = jnp.dot(perm, h_ref[...], preferred_element_type=F32).astype(cbuf.dtype)
    parts = jnp.concatenate(list(_split_bf16x3(route)) + [jnp.zeros((LANES - 3 * SUBLANES, tm), F32)], axis=0)
    record = lax.dot_general(perm, parts.astype(BF16), (((1,), (1,)), ((), ())), preferred_element_type=F32)
    cbuf[cur, :, d:] = record.astype(cbuf.dtype)
    _start_copies(seg_copies(i, i >= 0, cur))

    @pl.when(i == pl.num_programs(0) - 1)
    def _():
        _wait_copies(seg_copies(i - 1, i >= 1, 1 - cur))
        _wait_copies(seg_copies(i, i >= 0, cur))
        zbuf[...] = jnp.zeros(zbuf.shape, zbuf.dtype)

        def zero_copy(d0, rows):
            return (pltpu.make_async_copy(zbuf.at[pl.ds(0, rows)], hs_ref.at[pl.ds(d0, rows)], sem.at[0]),)
        pairs = []
        for g in range(ng):
            n = tail_ref[g]
            dst = tail_ref[ng + g]
            for k in range(MOE_TAIL_BITS - 1, -1, -1):
                done = (n >> (k + 1)) << (k + 1)
                d0 = pl.multiple_of((dst + done) * ROW_ALIGN, ROW_ALIGN)
                pairs.append((((n >> k) & 1) == 1, zero_copy(d0, ROW_ALIGN << k)))
        used = tail_ref[2 * ng]
        total = hs_ref.shape[0] // MOE_TILE
        for j in range(total - nt):
            d0 = pl.multiple_of(jnp.minimum(used + j, total - 1) * MOE_TILE, MOE_TILE)
            pairs.append((used + j < total, zero_copy(d0, MOE_TILE)))
        _start_copies(pairs)
        _wait_copies(pairs)


def _dispatch(seg_tab, tail_tab, h2, route, tri):
    t, d = h2.shape
    nt, grid = _moe_layout(t)
    rows = grid * MOE_TILE
    grid_spec = pltpu.PrefetchScalarGridSpec(
        num_scalar_prefetch=2,
        grid=(nt,),
        in_specs=[pl.BlockSpec((MOE_TILE, d), lambda i, *_: (i, 0)),
                  pl.BlockSpec((SUBLANES, MOE_TILE), lambda i, *_: (0, i)),
                  pl.BlockSpec((MOE_TILE, MOE_TILE), lambda i, *_: (0, 0))],
        out_specs=[pl.BlockSpec((SUBLANES, MOE_TILE), lambda i, *_: (0, i)),
                   pl.BlockSpec(memory_space=pl.ANY)],
        scratch_shapes=[pltpu.VMEM((2, MOE_CROWS, d + LANES), BF16), pltpu.VMEM((MOE_TILE, d + LANES), BF16),
                        pltpu.SemaphoreType.DMA((2,))])
    return pl.pallas_call(
        functools.partial(_dispatch_kernel, nt=nt),
        out_shape=[jax.ShapeDtypeStruct((SUBLANES, t), F32),
                   jax.ShapeDtypeStruct((rows, d + LANES), BF16)],
        grid_spec=grid_spec,
        compiler_params=_cparams(("arbitrary",)),
        name="moe_dispatch",
    )(seg_tab, tail_tab, h2, route, tri)


def _ffn_kernel(grp_ref, valid_ref, h_ref, wg32_ref, wu32_ref, wd32_ref, y_ref, wg_ref, wu_ref, wd_ref):
    i = pl.program_id(0)
    d = y_ref.shape[1]

    @pl.when((i == 0) | (grp_ref[i] != grp_ref[jnp.maximum(i - 1, 0)]))
    def _():
        for j in range(EXPERTS_PER_GROUP):
            cols = slice(j * D_FF_EXPERT, (j + 1) * D_FF_EXPERT)
            wg_ref[:, cols] = wg32_ref[j].astype(BF16)
            wu_ref[:, cols] = wu32_ref[j].astype(BF16)
            wd_ref[cols, :] = wd32_ref[j].astype(BF16)

    @pl.when(valid_ref[i] == 0)
    def _():
        y_ref[...] = jnp.zeros(y_ref.shape, y_ref.dtype)

    @pl.when(valid_ref[i] == 1)
    def _():
        h = h_ref[:, :d]
        gates = h_ref[:, d:].astype(F32)
        acts = []
        for j in range(EXPERTS_PER_GROUP):
            cols = slice(j * D_FF_EXPERT, (j + 1) * D_FF_EXPERT)
            a = jnp.dot(h, wg_ref[:, cols], preferred_element_type=F32)
            u = jnp.dot(h, wu_ref[:, cols], preferred_element_type=F32)
            gate = (gates[:, j:j + 1] + gates[:, SUBLANES + j:SUBLANES + j + 1]
                    + gates[:, 2 * SUBLANES + j:2 * SUBLANES + j + 1])
            acts.append(((a * _sigmoid_tanh(a)) * u * gate).astype(BF16))
        y = jnp.dot(jnp.concatenate(acts, axis=1), wd_ref[...], preferred_element_type=F32)
        y_ref[...] = y.astype(y_ref.dtype)


def _ffn(grp, valid, hs, wg, wu, wd, layer):
    rows, width = hs.shape
    d = width - LANES
    epg = EXPERTS_PER_GROUP
    once = pl.Buffered(1)
    w_idx = lambda i, grp, valid: (layer, grp[i], 0, 0)
    grid_spec = pltpu.PrefetchScalarGridSpec(
        num_scalar_prefetch=2,
        grid=(rows // MOE_TILE,),
        in_specs=[pl.BlockSpec((MOE_TILE, width), lambda i, grp, valid: (i, 0)),
                  pl.BlockSpec((None, epg, d, D_FF_EXPERT), w_idx, pipeline_mode=once),
                  pl.BlockSpec((None, epg, d, D_FF_EXPERT), w_idx, pipeline_mode=once),
                  pl.BlockSpec((None, epg, D_FF_EXPERT, d), w_idx, pipeline_mode=once)],
        out_specs=pl.BlockSpec((MOE_TILE, d), lambda i, grp, valid: (i, 0)),
        scratch_shapes=[pltpu.VMEM((d, epg * D_FF_EXPERT), BF16), pltpu.VMEM((d, epg * D_FF_EXPERT), BF16),
                        pltpu.VMEM((epg * D_FF_EXPERT, d), BF16)])
    return pl.pallas_call(
        _ffn_kernel,
        out_shape=jax.ShapeDtypeStruct((rows, d), BF16),
        grid_spec=grid_spec,
        compiler_params=_cparams(("arbitrary",)),
        name="moe_ffn",
    )(grp, valid, hs, wg, wu, wd)


def _combine_kernel(seg_ref, x1_ref, slot_ref, gate2_ref, ys_ref, *rest, fnet):
    if fnet:
        shift_ref, scale_ref, gain_ref, cs_ref, o_ref, y1_ref, y2_ref, ybuf, sem = rest
    else:
        o_ref, ybuf, sem = rest
    i = pl.program_id(0)
    nt = pl.num_programs(0)
    tm = x1_ref.shape[0]
    cur = i % 2

    def seg_copies(tile, enable, buf):
        def seg_copy(s0, d0, rows):
            return (pltpu.make_async_copy(ys_ref.at[pl.ds(d0, rows)], ybuf.at[buf, pl.ds(s0, rows)], sem.at[buf]),)
        return _segment_copies(seg_ref, tile, enable, seg_copy)

    @pl.when(i == 0)
    def _():
        ybuf[...] = jnp.zeros(ybuf.shape, ybuf.dtype)
        _start_copies(seg_copies(i, i == 0, cur))

    nxt = jnp.minimum(i + 1, nt - 1)
    _start_copies(seg_copies(nxt, i + 1 < nt, 1 - cur))
    _wait_copies(seg_copies(i, i >= 0, cur))
    slot = slot_ref[0:1, :]
    perm = jnp.where(lax.broadcasted_iota(jnp.int32, (MOE_CROWS, tm), 0).astype(F32) == slot, 1.0, 0.0)
    y = lax.dot_general(perm.astype(BF16), ybuf[cur], (((0,), (0,)), ((), ())), preferred_element_type=F32)
    x = x1_ref[...] + gate2_ref[0] * y
    o_ref[...] = x
    if fnet:
        _fnet_channel_tile(x, shift_ref, scale_ref, gain_ref, cs_ref, y1_ref, y2_ref)


def _combine(seg_tab, x1, slot, gate2, ys, tokens_per_batch, fnet=None):
    t, d = x1.shape
    tpb = tokens_per_batch // MOE_TILE
    per_b = lambda i, *_: (i // tpb, 0, 0)
    full = lambda i, *_: (0, 0)
    in_specs = [pl.BlockSpec((MOE_TILE, d), lambda i, *_: (i, 0)),
                pl.BlockSpec((SUBLANES, MOE_TILE), lambda i, *_: (0, i)),
                pl.BlockSpec((1, 1, d), per_b),
                pl.BlockSpec(memory_space=pl.ANY)]
    out_shape = [jax.ShapeDtypeStruct((t, d), F32)]
    out_specs = [pl.BlockSpec((MOE_TILE, d), lambda i, *_: (i, 0))]
    args = [seg_tab, x1, slot, gate2, ys]
    if fnet is not None:
        shift, scale, gain, cs = fnet
        gw = d // FNET_GROUPS
        in_specs += [pl.BlockSpec((1, 1, d), per_b), pl.BlockSpec((1, 1, d), per_b), pl.BlockSpec((1, d), full),
                     pl.BlockSpec(cs.shape, full)]
        out_shape += [jax.ShapeDtypeStruct((FNET_GROUPS, t, gw), BF16)] * 2
        out_specs += [pl.BlockSpec((FNET_GROUPS, MOE_TILE, gw), lambda i, *_: (0, i, 0))] * 2
        args += [shift, scale, gain, cs]
    grid_spec = pltpu.PrefetchScalarGridSpec(
        num_scalar_prefetch=1,
        grid=(t // MOE_TILE,),
        in_specs=in_specs,
        out_specs=out_specs,
        scratch_shapes=[pltpu.VMEM((2, MOE_CROWS, d), BF16), pltpu.SemaphoreType.DMA((2,))])
    out = pl.pallas_call(
        functools.partial(_combine_kernel, fnet=fnet is not None),
        out_shape=out_shape,
        grid_spec=grid_spec,
        compiler_params=_cparams(("arbitrary",)),
        name="moe_combine_fnet" if fnet is not None else "moe_combine",
    )(*args)
    return out if fnet is not None else out[0]


def _grouped_moe(h2, route, x1, gate2, wg, wu, wd, layer, tri):
    t = h2.shape[0]
    gid = route[ROUTE_GID_ROW].astype(jnp.int32)
    seg_tab, tail_tab, grp, valid = _moe_tables(gid, t)
    slot, hs = _dispatch(seg_tab, tail_tab, h2, route, tri)
    ys = _ffn(grp, valid, hs, wg, wu, wd, layer)
    return seg_tab, x1, slot, gate2, ys


FFT_J = SUBLANES
FFT_PAIR_ROWS = 2 * FFT_J * (GRID_W // FFT_J)


def _pair_grid_rows(y):
    rows = y.shape[0]
    assert rows % (2 * GRID_W) == 0
    groups = []
    for u in range(rows // (2 * GRID_W)):
        for sb in range(GRID_W // FFT_J):
            for e in range(2):
                start = GRID_W * (2 * u + e) + FFT_J * sb
                groups.append(y[start:start + FFT_J])
    return jnp.concatenate(groups, axis=0)


def _fnet_channel_tile(x, shift_ref, scale_ref, gain_ref, cs_ref, y1_ref, y2_ref):
    h = _norm_modulate(x, gain_ref[...], shift_ref[0], scale_ref[0]).astype(BF16)
    gw = D_MODEL // FNET_GROUPS
    for g in range(FNET_GROUPS):
        y = _pair_grid_rows(jnp.dot(h[:, g * gw:(g + 1) * gw], cs_ref[...], preferred_element_type=F32))
        y1_ref[g] = y[:, :gw].astype(y1_ref.dtype)
        y2_ref[g] = y[:, gw:].astype(y2_ref.dtype)


def _fnet_chan_kernel(x_ref, shift_ref, scale_ref, gain_ref, cs_ref, y1_ref, y2_ref):
    _fnet_channel_tile(x_ref[...], shift_ref, scale_ref, gain_ref, cs_ref, y1_ref, y2_ref)


def _fnet_channel(x2d, shift, scale, gain, cs, tokens_per_batch, tm):
    t, d = x2d.shape
    tpb = tokens_per_batch // tm
    gw = d // FNET_GROUPS
    full = lambda i: (0, 0)
    out = jax.ShapeDtypeStruct((FNET_GROUPS, t, gw), BF16)
    out_spec = pl.BlockSpec((FNET_GROUPS, tm, gw), lambda i: (0, i, 0))
    return pl.pallas_call(
        _fnet_chan_kernel,
        out_shape=[out, out],
        grid=(t // tm,),
        in_specs=[pl.BlockSpec((tm, d), lambda i: (i, 0)),
                  pl.BlockSpec((1, 1, d), lambda i: (i // tpb, 0, 0)),
                  pl.BlockSpec((1, 1, d), lambda i: (i // tpb, 0, 0)),
                  pl.BlockSpec((1, d), full),
                  pl.BlockSpec(cs.shape, full)],
        out_specs=[out_spec, out_spec],
        compiler_params=_cparams(("arbitrary",)),
        name="fnet_channel",
    )(x2d, shift, scale, gain, cs)


def _half_rows(m):
    hc = m // 2 + 1
    return hc, -(-FFT_J * hc // ROW_ALIGN) * ROW_ALIGN


def _mirror(lo, hi, hc, m):
    return jnp.concatenate([lo[:FFT_J * hc]] + [hi[FFT_J * c:FFT_J * (c + 1)] for c in range(m - hc, 0, -1)], axis=0)


def _fnet_pos_kernel(y1_ref, y2_ref, lr_ref, ls_ref, cs_ref, sn_ref, o_ref, z1, a_re, a_im, *, n):
    r1 = n // GRID_W
    gw = y1_ref.shape[2]
    hc_r, part_r = _half_rows(r1)
    hc_s, part_s = _half_rows(GRID_W)

    def stage_r(sb, carry):
        s0 = pl.multiple_of(sb * FFT_J, FFT_J)
        p0 = pl.multiple_of(sb * 2 * FFT_J, 2 * FFT_J)
        rhs1 = jnp.concatenate([y1_ref[0, pl.ds(FFT_PAIR_ROWS * u + p0, 2 * FFT_J), :] for u in range(r1 // 2)], axis=0)
        rhs2 = jnp.concatenate([y2_ref[0, pl.ds(FFT_PAIR_ROWS * u + p0, 2 * FFT_J), :] for u in range(r1 // 2)], axis=0)
        p = jnp.dot(lr_ref[...], rhs1, preferred_element_type=F32)
        q = jnp.dot(lr_ref[...], rhs2, preferred_element_type=F32)
        pc, ps = p[:part_r], p[part_r:]
        qc, qs = q[:part_r], q[part_r:]
        re = _mirror(pc - qs, pc + qs, hc_r, r1)
        nim = _mirror(qc + ps, qc - ps, hc_r, r1)
        cs = jnp.concatenate([cs_ref[sb]] * (gw // LANES), axis=1)
        sn = jnp.concatenate([sn_ref[sb]] * (gw // LANES), axis=1)
        tre = re * cs - nim * sn
        tnim = re * sn + nim * cs
        for c in range(r1):
            a_re[pl.ds(GRID_W * c + s0, FFT_J), :] = tre[FFT_J * c:FFT_J * (c + 1)]
            a_im[pl.ds(GRID_W * c + s0, FFT_J), :] = tnim[FFT_J * c:FFT_J * (c + 1)]
        return carry
    lax.fori_loop(0, GRID_W // FFT_J, stage_r, 0)

    cblk = FFT_J * GRID_W

    def stage_s(cb, carry):
        c0 = pl.multiple_of(cb * cblk, cblk)
        u = jnp.dot(ls_ref[0], a_re[pl.ds(c0, cblk), :].astype(BF16), preferred_element_type=F32)
        v = jnp.dot(ls_ref[1], a_im[pl.ds(c0, cblk), :].astype(BF16), preferred_element_type=F32)
        lo = u - v
        hi = u + v
        k0 = pl.multiple_of(cb * FFT_J, FFT_J)
        for d in range(hc_s):
            z1[pl.ds(r1 * d + k0, FFT_J), :] = lo[FFT_J * d:FFT_J * (d + 1)]
        for d in range(1, GRID_W - hc_s + 1):
            z1[pl.ds(r1 * (GRID_W - d) + k0, FFT_J), :] = hi[FFT_J * d:FFT_J * (d + 1)]
        return carry
    lax.fori_loop(0, r1 // FFT_J, stage_s, 0)
    o_ref[0] = z1[...].astype(o_ref.dtype)


def _fnet_position(y1, y2, lr, ls, tw_cos, tw_sin, batch, n):
    groups, t, gw = y1.shape
    full2 = lambda b, g: (0, 0)
    full3 = lambda b, g: (0, 0, 0)
    scratch = pltpu.VMEM((n, gw), F32)
    return pl.pallas_call(
        functools.partial(_fnet_pos_kernel, n=n),
        out_shape=jax.ShapeDtypeStruct((groups, t, gw), BF16),
        grid=(batch, groups),
        in_specs=[pl.BlockSpec((1, n, gw), lambda b, g: (g, b, 0)),
                  pl.BlockSpec((1, n, gw), lambda b, g: (g, b, 0)),
                  pl.BlockSpec(lr.shape, full2),
                  pl.BlockSpec(ls.shape, full3),
                  pl.BlockSpec(tw_cos.shape, full3),
                  pl.BlockSpec(tw_sin.shape, full3)],
        out_specs=pl.BlockSpec((1, n, gw), lambda b, g: (g, b, 0)),
        scratch_shapes=[scratch, scratch, scratch],
        compiler_params=_cparams(("arbitrary", "arbitrary")),
        name="fnet_position",
    )(y1, y2, lr, ls, tw_cos, tw_sin)


def _dft_tables(n):
    assert n % (GRID_W * FFT_J) == 0
    gw = D_MODEL // FNET_GROUPS
    j = np.arange(gw)
    ang = 2.0 * np.pi * ((j[:, None] * j[None, :]) % gw) / gw
    cs = np.concatenate([np.cos(ang), np.sin(ang)], axis=1) / np.sqrt(gw)
    r1 = n // GRID_W
    assert r1 % 2 == 0
    eye = np.eye(FFT_J)
    scale = float(n) ** -0.25
    a = np.arange(r1)
    hc_r, part_r = _half_rows(r1)
    ang_r = 2.0 * np.pi * ((a[:hc_r, None] * a[None, :]) % r1) / r1
    lr = np.zeros((2 * part_r, FFT_J * r1))
    lr[:FFT_J * hc_r] = np.kron(np.cos(ang_r), eye) * scale
    lr[part_r:part_r + FFT_J * hc_r] = np.kron(np.sin(ang_r), eye) * scale
    s = np.arange(GRID_W)
    hc_s, part_s = _half_rows(GRID_W)
    ang_s = 2.0 * np.pi * ((s[:hc_s, None] * s[None, :]) % GRID_W) / GRID_W
    ls = np.zeros((2, part_s, FFT_J * GRID_W))
    ls[0, :FFT_J * hc_s] = np.einsum("ds,cC->dcCs", np.cos(ang_s), eye).reshape(FFT_J * hc_s, FFT_J * GRID_W) * scale
    ls[1, :FFT_J * hc_s] = np.einsum("ds,cC->dcCs", np.sin(ang_s), eye).reshape(FFT_J * hc_s, FFT_J * GRID_W) * scale
    sb = np.arange(GRID_W // FFT_J)
    s_of = sb[:, None, None] * FFT_J + np.arange(FFT_J)[None, None, :]
    ang_t = 2.0 * np.pi * ((s_of * a[None, :, None]) % n) / n
    ang_t = ang_t.reshape(len(sb), r1 * FFT_J, 1)
    tw_cos = jnp.broadcast_to(jnp.asarray(np.cos(ang_t), F32), (len(sb), r1 * FFT_J, LANES))
    tw_sin = jnp.broadcast_to(jnp.asarray(np.sin(ang_t), F32), (len(sb), r1 * FFT_J, LANES))
    return jnp.asarray(cs, BF16), jnp.asarray(lr, BF16), jnp.asarray(ls, BF16), tw_cos, tw_sin


def kernel(x, c, ctx, c_ctx, ada_w, ada_b, norm_mix, norm_ffn, mix_w_in, mix_w_out, na_q_norm, na_k_norm, na_rpb,
           lru_conv_w, lru_conv_b, lru_gate_r_w, lru_gate_r_b, lru_gate_i_w, lru_gate_i_b, lru_lambda,
           fnet_w_out, router_w, router_bias, moe_w_gate, moe_w_up, moe_w_down):
    batch, n, d = x.shape
    ctx_len = ctx.shape[1]
    depth = ada_w.shape[0]
    rows = n // GRID_W
    assert d == D_MODEL and n % (GRID_W * NA_QROWS) == 0 and rows >= 4 * NA_QROWS
    assert n % MOE_TILE == 0
    t = batch * n
    tm = 512
    tri = jnp.asarray(np.triu(np.ones((MOE_TILE, MOE_TILE))), BF16)

    r_pad = -(-(batch + 1) // SUBLANES) * SUBLANES
    c_rows = jnp.concatenate([c, c_ctx[None, :], jnp.zeros((r_pad - batch - 1, d), c.dtype)], axis=0)
    mod = _modulation(c_rows, ada_w, ada_b)

    def mod_slices(layer):
        m = mod[layer, :batch].reshape(batch, 1, 6, d)
        return [m[:, :, i, :] for i in range(6)]

    rw_cat = jnp.pad(router_w.astype(F32), ((0, 0), (0, 2 * LANES - N_EXPERTS))).astype(BF16)
    rbias = router_bias.reshape(N_EXPERTS, 1).astype(F32)
    x2d = x.reshape(t, d)
    ctx2d = ctx.reshape(batch * ctx_len, d)

    pending = None
    for layer in range(depth):
        li = layer // 2
        shift1, scale1, gate1, shift2, scale2, gate2 = mod_slices(layer)
        gain_mix = norm_mix[layer].reshape(1, d)
        gain_ffn = norm_ffn[layer].reshape(1, d)
        if layer % 2 == 0:
            if pending is not None:
                x2d = _combine(*pending, n)
            w_in = mix_w_in[li].astype(BF16)
            ind = jnp.asarray(np.kron(np.eye(NA_HEADS // 2), np.ones((HEAD_DIM, HEAD_DIM))), BF16)
            qg = (jnp.tile(na_q_norm[li], NA_HEADS) * (HEAD_DIM ** -0.5 * LOG2_E)).reshape(1, NA_WIDTH).astype(F32)
            kg = jnp.tile(na_k_norm[li], NA_HEADS).reshape(1, NA_WIDTH).astype(F32)
            q, k, v, xb, gb = _inproj(x2d, shift1, scale1, gain_mix, w_in, ind, qg, kg,
                                      ("q", "k", "v", "x", "g"), n, 2 * tm)
            mctx = mod[layer, batch, :2 * d]
            shift_c = jnp.broadcast_to(mctx[:d], (batch, 1, d))
            scale_c = jnp.broadcast_to(mctx[d:], (batch, 1, d))
            k_c, v_c, xb_c = _inproj(ctx2d, shift_c, scale_c, gain_mix, w_in[:, NA_WIDTH:4 * NA_WIDTH], ind, qg, kg,
                                     ("k", "v", "x"), ctx_len, ctx_len)
            bias = _na_bias_tables(na_rpb[li], rows)
            attn = _attention(q, k, v, k_c, v_c, bias, batch, n, ctx_len)
            wcat, gbias = _lru_gate_weights(lru_gate_r_w[li], lru_gate_r_b[li], lru_gate_i_w[li], lru_gate_i_b[li])
            lru = _lru(xb, gb, xb_c, lru_conv_w[li].astype(F32), lru_conv_b[li].reshape(1, LRU_WIDTH).astype(F32),
                       wcat, gbias, lru_lambda[li].astype(F32), batch, n, ctx_len)
            parts, w_out = [attn, lru], mix_w_out[li].astype(BF16)
        else:
            cs, lr, ls, tw_cos, tw_sin = _dft_tables(n)
            if pending is not None:
                x2d, y1, y2 = _combine(*pending, n, fnet=(shift1, scale1, gain_mix, cs))
            else:
                y1, y2 = _fnet_channel(x2d, shift1, scale1, gain_mix, cs, n, tm)
            parts, w_out = [_fnet_position(y1, y2, lr, ls, tw_cos, tw_sin, batch, n)], fnet_w_out[li].astype(BF16)
        x1, h2, route = _post_mixer(parts, w_out, x2d, gate1, shift2, scale2, gain_ffn, rw_cat, rbias, n,
                                    2 * POST_SUBTILE)
        pending = _grouped_moe(h2, route, x1, gate2, moe_w_gate.astype(F32), moe_w_up.astype(F32),
                               moe_w_down.astype(F32), layer, tri)
    return _combine(*pending, n).reshape(batch, n, d)
```

```python
import functools

import numpy as np
import jax
import jax.numpy as jnp
from jax import lax
from jax.experimental import pallas as pl
from jax.experimental.pallas import tpu as pltpu

F32 = jnp.float32
BF16 = jnp.bfloat16
HIGHEST = lax.Precision.HIGHEST

D_MODEL = 1024
GRID_W = 64
HEAD_DIM = 64
NA_HEADS = 8
NA_WIDTH = NA_HEADS * HEAD_DIM
NA_WIN_ROWS = 8
NA_WIN_COLS = 16
LRU_WIDTH = 512
LRU_BLOCK = 64
LRU_C = 8.0
FNET_GROUPS = 4
N_EXPERTS = 16
EXPERTS_PER_GROUP = 4
N_EXPERT_GROUPS = 4
D_FF_EXPERT = 512
RMS_EPS = 1e-6
MASK_VALUE = -1e30
LOG2_E = 1.4426950408889634

V7X_VMEM_LIMIT_BYTES = 56 * 1024 * 1024
LANES = 128
SUBLANES = 8

NA_QROWS = 4
NA_KROWS = NA_QROWS + NA_WIN_ROWS - 1
NA_QBLK = NA_QROWS * GRID_W
NA_KBLK = NA_KROWS * GRID_W
NA_STEP_BLOCKS = 2

LRU_CHUNK = LANES
LRU_TROWS = 512

ROUTE_GID_ROW = EXPERTS_PER_GROUP
MOE_TILE = 512
ROW_ALIGN = 16
MOE_CROWS = MOE_TILE + N_EXPERT_GROUPS * ROW_ALIGN
MOE_SEG_BITS = (MOE_TILE // ROW_ALIGN).bit_length()
MOE_TAIL_BITS = (MOE_TILE // ROW_ALIGN - 1).bit_length()


def _sigmoid(x):
    return 1.0 / (1.0 + jnp.exp(-x))


def _sigmoid_tanh(x):
    return 0.5 + 0.5 * jnp.tanh(0.5 * x)


def _cparams(sem, vmem=V7X_VMEM_LIMIT_BYTES):
    return pltpu.CompilerParams(dimension_semantics=sem, vmem_limit_bytes=vmem)


def _mod_kernel(c_ref, w_ref, b_ref, o_ref):
    c = c_ref[...]
    s = c * _sigmoid(c)
    o_ref[0] = jnp.dot(s, w_ref[0], precision=HIGHEST, preferred_element_type=F32) + b_ref[0]


def _modulation(c_rows, ada_w, ada_b):
    depth, d, n6 = ada_w.shape
    r = c_rows.shape[0]
    tn = n6 // 2
    return pl.pallas_call(
        _mod_kernel,
        out_shape=jax.ShapeDtypeStruct((depth, r, n6), F32),
        grid=(depth, n6 // tn),
        in_specs=[pl.BlockSpec((r, d), lambda l, j: (0, 0)),
                  pl.BlockSpec((1, d, tn), lambda l, j: (l, 0, j)),
                  pl.BlockSpec((1, 1, tn), lambda l, j: (l, 0, j))],
        out_specs=pl.BlockSpec((1, r, tn), lambda l, j: (l, 0, j)),
        compiler_params=_cparams(("arbitrary", "arbitrary")),
        name="adaln_mod",
    )(c_rows, ada_w, ada_b.reshape(depth, 1, n6))


def _norm_modulate(x, gain, shift, scale):
    ms = jnp.mean(x * x, axis=-1, keepdims=True)
    y = x * lax.rsqrt(ms + RMS_EPS) * gain
    return y * (1.0 + scale) + shift


def _inproj_kernel(x_ref, shift_ref, scale_ref, gain_ref, w_ref, ind_ref, qg_ref, kg_ref, *out_refs, segs):
    h = _norm_modulate(x_ref[...], gain_ref[...], shift_ref[0], scale_ref[0]).astype(BF16)
    for s, (kind, o_ref) in enumerate(zip(segs, out_refs)):
        z = jnp.dot(h, w_ref[:, s * NA_WIDTH:(s + 1) * NA_WIDTH], preferred_element_type=F32)
        if kind in ("q", "k"):
            zz = (z * z).astype(BF16)
            hw = ind_ref.shape[0]
            ms = jnp.concatenate([jnp.dot(zz[:, i * hw:(i + 1) * hw], ind_ref[...], preferred_element_type=F32)
                                  for i in range(NA_WIDTH // hw)], axis=1) * (1.0 / HEAD_DIM)
            g = qg_ref[...] if kind == "q" else kg_ref[...]
            z = z * lax.rsqrt(ms + RMS_EPS) * g
        if kind in ("x", "g"):
            for c in range(LRU_WIDTH // LRU_CHUNK):
                o_ref[c] = z[:, c * LRU_CHUNK:(c + 1) * LRU_CHUNK].astype(o_ref.dtype)
        else:
            o_ref[...] = z.astype(o_ref.dtype)


def _inproj(x2d, shift, scale, gain, w, ind, qg, kg, segs, tokens_per_batch, tm):
    t, d = x2d.shape
    tpb = tokens_per_batch // tm
    dt = {"q": BF16, "k": BF16, "v": BF16, "x": F32, "g": F32}
    full = lambda i: (0, 0)
    nch = LRU_WIDTH // LRU_CHUNK

    def out_shape(kind):
        shape = (nch, t, LRU_CHUNK) if kind in ("x", "g") else (t, NA_WIDTH)
        return jax.ShapeDtypeStruct(shape, dt[kind])

    def out_spec(kind):
        if kind in ("x", "g"):
            return pl.BlockSpec((nch, tm, LRU_CHUNK), lambda i: (0, i, 0))
        return pl.BlockSpec((tm, NA_WIDTH), lambda i: (i, 0))

    return pl.pallas_call(
        functools.partial(_inproj_kernel, segs=segs),
        out_shape=[out_shape(k) for k in segs],
        grid=(t // tm,),
        in_specs=[pl.BlockSpec((tm, d), lambda i: (i, 0)),
                  pl.BlockSpec((1, 1, d), lambda i: (i // tpb, 0, 0)),
                  pl.BlockSpec((1, 1, d), lambda i: (i // tpb, 0, 0)),
                  pl.BlockSpec((1, d), full),
                  pl.BlockSpec(w.shape, full),
                  pl.BlockSpec(ind.shape, full),
                  pl.BlockSpec((1, NA_WIDTH), full),
                  pl.BlockSpec((1, NA_WIDTH), full)],
        out_specs=[out_spec(k) for k in segs],
        compiler_params=_cparams(("arbitrary",)),
        name="inproj_" + "".join(segs),
    )(x2d, shift, scale, gain, w, ind, qg, kg)


def _na_bias_tables(rpb, rows):
    kr = NA_WIN_ROWS
    rb_count = rows // NA_QROWS
    cq = np.arange(GRID_W)
    ck = np.arange(GRID_W)
    col_start = np.clip(cq - NA_WIN_COLS // 2, 0, GRID_W - NA_WIN_COLS)
    valid_c = (ck[None, :] >= col_start[:, None]) & (ck[None, :] < col_start[:, None] + NA_WIN_COLS)
    dc = np.clip(ck[None, :] - cq[:, None], 1 - NA_WIN_COLS, NA_WIN_COLS - 1) + (NA_WIN_COLS - 1)
    n_dr, n_dc = 2 * NA_WIN_ROWS - 1, 2 * NA_WIN_COLS - 1
    sel_c = (dc[:, :, None] == np.arange(n_dc)) & valid_c[:, :, None]
    blocks = jnp.einsum("hrc,qkc->hrqk", rpb.astype(F32), jnp.asarray(sel_c, F32), precision=HIGHEST)
    blocks = blocks + jnp.asarray(np.where(valid_c, 0.0, MASK_VALUE), F32)
    blocks = jnp.concatenate([blocks, jnp.full((NA_HEADS, 1, GRID_W, GRID_W), MASK_VALUE, F32)], axis=1)
    blocks = blocks * LOG2_E
    which = []
    for rb in (0, 1, rb_count - 1):
        r = rb * NA_QROWS + np.arange(NA_QROWS)
        ks = int(np.clip(rb * NA_QROWS - kr // 2, 0, rows - NA_KROWS))
        key_r = ks + np.arange(NA_KROWS)
        row_start = np.clip(r - kr // 2, 0, rows - kr)
        valid_r = (key_r[None, :] >= row_start[:, None]) & (key_r[None, :] < row_start[:, None] + kr)
        dr = np.clip(key_r[None, :] - r[:, None] + (NA_WIN_ROWS - 1), 0, n_dr - 1)
        which.append(np.where(valid_r, dr, n_dr))
    return _na_bias_assemble(blocks, which)


def _na_bias_kernel(blk_ref, o_ref, *, which):
    for t, table in enumerate(which):
        @pl.when(pl.program_id(0) == t)
        def _():
            for i in range(NA_QROWS):
                row = jnp.concatenate([blk_ref[0, int(table[i, j])] for j in range(NA_KROWS)], axis=1)
                o_ref[0, 0, i * GRID_W:(i + 1) * GRID_W, :] = row


def _na_bias_assemble(blocks, which):
    heads, nblk = blocks.shape[:2]
    return pl.pallas_call(
        functools.partial(_na_bias_kernel, which=which),
        out_shape=jax.ShapeDtypeStruct((len(which), heads, NA_QBLK, NA_KBLK), F32),
        grid=(len(which), heads),
        in_specs=[pl.BlockSpec((1, nblk, GRID_W, GRID_W), lambda t, h: (h, 0, 0, 0))],
        out_specs=pl.BlockSpec((1, 1, NA_QBLK, NA_KBLK), lambda t, h: (t, h, 0, 0)),
        compiler_params=_cparams(("arbitrary", "arbitrary")),
        name="na_bias",
    )(blocks)


def _attn_kernel(q_ref, k_ref, v_ref, kc_ref, vc_ref, bias_ref, o_ref, *, rows):
    last = rows // NA_QROWS - 1
    nt = (((1,), (1,)), ((), ()))
    ctx_len = kc_ref.shape[0]
    low_half = lax.broadcasted_iota(jnp.int32, (NA_QBLK, LANES), 1) < HEAD_DIM
    for blk in range(NA_STEP_BLOCKS):
        rb = pl.program_id(1) * NA_STEP_BLOCKS + blk
        ks = jnp.clip(rb * NA_QROWS - NA_WIN_ROWS // 2, 0, rows - NA_KROWS)
        kstart = pl.multiple_of(ks * GRID_W, GRID_W)
        geom = jnp.where(rb == 0, 0, jnp.where(rb == last, 2, 1))
        qrows = slice(blk * NA_QBLK, (blk + 1) * NA_QBLK)
        for pair in range(NA_HEADS * HEAD_DIM // LANES):
            ls = slice(pair * LANES, (pair + 1) * LANES)
            q2 = q_ref[qrows, ls]
            k_all = jnp.concatenate([kc_ref[:, ls], k_ref[pl.ds(kstart, NA_KBLK), ls]], axis=0)
            v_all = jnp.concatenate([vc_ref[:, ls], v_ref[pl.ds(kstart, NA_KBLK), ls]], axis=0)
            outs = []
            for half in range(2):
                qh = jnp.where(low_half == (half == 0), q2, jnp.zeros_like(q2))
                s = lax.dot_general(qh, k_all, nt, preferred_element_type=F32)
                s = jnp.concatenate([s[:, :ctx_len], s[:, ctx_len:] + bias_ref[geom, 2 * pair + half]], axis=1)
                m = jnp.max(s, axis=-1, keepdims=True)
                p = jnp.exp2(s - m)
                l = jnp.sum(p, axis=-1, keepdims=True)
                outs.append(jnp.dot(p.astype(BF16), v_all, preferred_element_type=F32) / l)
            o_ref[qrows, ls] = jnp.where(low_half, outs[0], outs[1]).astype(o_ref.dtype)


def _attention(q, k, v, kc, vc, bias, batch, n, ctx_len):
    rows = n // GRID_W
    rbc = rows // (NA_QROWS * NA_STEP_BLOCKS)
    qblk = NA_QBLK * NA_STEP_BLOCKS
    return pl.pallas_call(
        functools.partial(_attn_kernel, rows=rows),
        out_shape=jax.ShapeDtypeStruct((batch * n, NA_WIDTH), BF16),
        grid=(batch, rbc),
        in_specs=[pl.BlockSpec((qblk, NA_WIDTH), lambda b, rb: (b * rbc + rb, 0)),
                  pl.BlockSpec((n, NA_WIDTH), lambda b, rb: (b, 0)),
                  pl.BlockSpec((n, NA_WIDTH), lambda b, rb: (b, 0)),
                  pl.BlockSpec((ctx_len, NA_WIDTH), lambda b, rb: (b, 0)),
                  pl.BlockSpec((ctx_len, NA_WIDTH), lambda b, rb: (b, 0)),
                  pl.BlockSpec(bias.shape, lambda b, rb: (0, 0, 0, 0), pipeline_mode=pl.Buffered(1))],
        out_specs=pl.BlockSpec((qblk, NA_WIDTH), lambda b, rb: (b * rbc + rb, 0)),
        compiler_params=_cparams(("arbitrary", "arbitrary")),
        name="na_attention",
    )(q, k, v, kc, vc, bias)


SCAN_GROUPS = 4
SCAN_CHUNKS = SCAN_GROUPS * SUBLANES


def _scan_pitch(n):
    p = -(-n // SCAN_CHUNKS)
    while p % 8 != 4:
        p += 1
    return p


NEG_LOG2_E = -LOG2_E


def _lru_coeff_tile(half_xc, zh, half_bias, k, d):
    c = LRU_CHUNK
    t_r = jnp.tanh(zh[:, (2 * d) * c:(2 * d + 1) * c] + half_bias[:, (2 * d) * c:(2 * d + 1) * c])
    t_i = jnp.tanh(zh[:, (2 * d + 1) * c:(2 * d + 2) * c] + half_bias[:, (2 * d + 1) * c:(2 * d + 2) * c])
    neg_log_a = k[d:d + 1, :] * (1.0 + t_r)
    a = jnp.exp2(neg_log_a * NEG_LOG2_E)
    one_minus_a2 = jnp.tanh(neg_log_a) * (a * a + 1.0)
    root = jnp.where(one_minus_a2 > 0.0, one_minus_a2 * lax.rsqrt(one_minus_a2), 0.0)
    return a, root * (half_xc + half_xc * t_i)


def _conv_tile(xpad, t0, w, b, rows):
    acc = b + w[0:1, :] * xpad[pl.ds(t0 + SUBLANES - 2, rows), :]
    acc = acc + w[1:2, :] * xpad[pl.ds(t0 + SUBLANES - 1, rows), :]
    acc = acc + w[2:3, :] * xpad[pl.ds(t0 + SUBLANES, rows), :]
    return acc + w[3:4, :] * xpad[pl.ds(t0 + SUBLANES + 1, rows), :]


SCAN_UNROLL = 4


def _group_rows(j, g, pitch):
    return pl.ds(g * SUBLANES * pitch + j, SUBLANES, stride=pitch)


def _chunk_totals(af_ref, bf_ref, ab_ref, bb_ref, pitch):
    def body(j, carry):
        jb = pitch - 1 - j
        out = []
        for g in range(SCAN_GROUPS):
            pf, hf, pb, hb = carry[4 * g:4 * g + 4]
            af = af_ref[_group_rows(j, g, pitch), :]
            ab = ab_ref[_group_rows(jb, g, pitch), :]
            out += [af * pf, af * hf + bf_ref[_group_rows(j, g, pitch), :],
                    ab * pb, ab * hb + bb_ref[_group_rows(jb, g, pitch), :]]
        return tuple(out)
    one = jnp.ones((SUBLANES, LRU_CHUNK), F32)
    zero = jnp.zeros((SUBLANES, LRU_CHUNK), F32)
    res = lax.fori_loop(0, pitch, body, (one, zero, one, zero) * SCAN_GROUPS, unroll=SCAN_UNROLL)
    fwd = [(res[4 * g], res[4 * g + 1]) for g in range(SCAN_GROUPS)]
    bwd = [(res[4 * g + 2], res[4 * g + 3]) for g in range(SCAN_GROUPS)]
    return fwd, bwd


def _chunk_starts(totals, h0, reverse):
    row = lax.broadcasted_iota(jnp.int32, (SUBLANES, LRU_CHUNK), 0)
    starts = [jnp.zeros((SUBLANES, LRU_CHUNK), F32) for _ in range(SCAN_GROUPS)]
    state = h0
    order = range(SCAN_CHUNKS - 1, -1, -1) if reverse else range(SCAN_CHUNKS)
    for c in order:
        g, s = divmod(c, SUBLANES)
        p_end, h_end = totals[g]
        starts[g] = jnp.where(row == s, state, starts[g])
        state = p_end[s:s + 1, :] * state + h_end[s:s + 1, :]
    return starts, state


def _scan_write(af_ref, bf_ref, hf_ref, ab_ref, bb_ref, hb_ref, starts_f, starts_b, pitch):
    def body(j, carry):
        jb = pitch - 1 - j
        out = []
        for g in range(SCAN_GROUPS):
            hf, hb = carry[2 * g:2 * g + 2]
            hf = af_ref[_group_rows(j, g, pitch), :] * hf + bf_ref[_group_rows(j, g, pitch), :]
            hb = ab_ref[_group_rows(jb, g, pitch), :] * hb + bb_ref[_group_rows(jb, g, pitch), :]
            hf_ref[_group_rows(j, g, pitch), :] = hf
            hb_ref[_group_rows(jb, g, pitch), :] = hb
            out += [hf, hb]
        return tuple(out)
    init = tuple(v for g in range(SCAN_GROUPS) for v in (starts_f[g], starts_b[g]))
    lax.fori_loop(0, pitch, body, init, unroll=SCAN_UNROLL)


def _lru_kernel(x_ref, g_ref, xc_ref, cw_ref, cb_ref, w_ref, gb_ref, lam_ref, o_ref,
                xpad, a0, b0, a1, b1, h0s, h1s, ca0, cb0, ca1, cb1, *, n, ctx_len):
    pitch = _scan_pitch(n)
    cpitch = _scan_pitch(ctx_len)
    cw = cw_ref[...]
    cb = cb_ref[...]
    gbias = gb_ref[0]
    lam = lam_ref[...]
    sp = jnp.maximum(-lam, 0.0) + jnp.log1p(jnp.exp(-jnp.abs(lam)))
    k = (0.5 * LRU_C) * sp
    wcat = w_ref[0]
    zeros8 = jnp.zeros((SUBLANES, LRU_CHUNK), F32)

    def fill_coeffs(src_rows, total, length, trows, a_refs, b_refs):
        for d in range(2):
            a_refs[d][pl.ds(length, total - length), :] = jnp.ones((total - length, LRU_CHUNK), F32)
            b_refs[d][pl.ds(length, total - length), :] = jnp.zeros((total - length, LRU_CHUNK), F32)
        xpad[pl.ds(0, SUBLANES), :] = zeros8
        xpad[pl.ds(SUBLANES + length, SUBLANES), :] = zeros8
        xpad[pl.ds(SUBLANES, length), :] = src_rows

        def tile(t, carry):
            t0 = pl.multiple_of(t * trows, SUBLANES)
            xc = _conv_tile(xpad, t0, cw, cb, trows)
            zh = jnp.dot(xc.astype(BF16), wcat, preferred_element_type=F32)
            half_xc = 0.5 * xc
            for d in range(2):
                a, b = _lru_coeff_tile(half_xc, zh, gbias, k, d)
                a_refs[d][pl.ds(t0, trows), :] = a
                b_refs[d][pl.ds(t0, trows), :] = b
            return carry
        lax.fori_loop(0, length // trows, tile, 0)

    fill_coeffs(xc_ref[0], SCAN_CHUNKS * cpitch, ctx_len, ctx_len, (ca0, ca1), (cb0, cb1))
    zero_state = jnp.zeros((1, LRU_CHUNK), F32)
    fwd, bwd = _chunk_totals(ca0, cb0, ca1, cb1, cpitch)
    _, init_f = _chunk_starts(fwd, zero_state, reverse=False)
    _, init_b = _chunk_starts(bwd, zero_state, reverse=True)

    fill_coeffs(x_ref[0], SCAN_CHUNKS * pitch, n, LRU_TROWS, (a0, a1), (b0, b1))
    fwd, bwd = _chunk_totals(a0, b0, a1, b1, pitch)
    starts_f, _ = _chunk_starts(fwd, init_f, reverse=False)
    starts_b, _ = _chunk_starts(bwd, init_b, reverse=True)
    _scan_write(a0, b0, h0s, a1, b1, h1s, starts_f, starts_b, pitch)

    def out_tile(t, carry):
        t0 = pl.multiple_of(t * LRU_TROWS, SUBLANES)
        y = h0s[pl.ds(t0, LRU_TROWS), :] + h1s[pl.ds(t0, LRU_TROWS), :]
        g = g_ref[0, pl.ds(t0, LRU_TROWS), :]
        gelu = 0.5 * g * (1.0 + jnp.tanh(0.7978845608028654 * (g + 0.044715 * (g * g * g))))
        o_ref[0, pl.ds(t0, LRU_TROWS), :] = (gelu * y).astype(o_ref.dtype)
        return carry
    lax.fori_loop(0, n // LRU_TROWS, out_tile, 0)


def _lru(xb, gb, xb_ctx, conv_w, conv_b, wcat, gbias, lam, batch, n, ctx_len):
    nch = LRU_WIDTH // LRU_CHUNK
    pitch = _scan_pitch(n)
    cpitch = _scan_pitch(ctx_len)
    big = pltpu.VMEM((SCAN_CHUNKS * pitch, LRU_CHUNK), F32)
    small = pltpu.VMEM((SCAN_CHUNKS * cpitch, LRU_CHUNK), F32)
    return pl.pallas_call(
        functools.partial(_lru_kernel, n=n, ctx_len=ctx_len),
        out_shape=jax.ShapeDtypeStruct((nch, batch * n, LRU_CHUNK), BF16),
        grid=(batch, nch),
        in_specs=[pl.BlockSpec((1, n, LRU_CHUNK), lambda b, c: (c, b, 0)),
                  pl.BlockSpec((1, n, LRU_CHUNK), lambda b, c: (c, b, 0)),
                  pl.BlockSpec((1, ctx_len, LRU_CHUNK), lambda b, c: (c, b, 0)),
                  pl.BlockSpec((4, LRU_CHUNK), lambda b, c: (0, c)),
                  pl.BlockSpec((1, LRU_CHUNK), lambda b, c: (0, c)),
                  pl.BlockSpec((1, LRU_CHUNK, 4 * LRU_CHUNK), lambda b, c: (c, 0, 0)),
                  pl.BlockSpec((1, 1, 4 * LRU_CHUNK), lambda b, c: (c, 0, 0)),
                  pl.BlockSpec((2, LRU_CHUNK), lambda b, c: (0, c))],
        out_specs=pl.BlockSpec((1, n, LRU_CHUNK), lambda b, c: (c, b, 0)),
        scratch_shapes=[pltpu.VMEM((n + 2 * SUBLANES, LRU_CHUNK), F32),
                        big, big, big, big, big, big, small, small, small, small],
        compiler_params=_cparams(("arbitrary", "arbitrary")),
        name="rglru",
    )(xb, gb, xb_ctx, conv_w, conv_b, wcat, gbias, lam)


def _lru_gate_weights(w_r, b_r, w_i, b_i):
    nch = LRU_WIDTH // LRU_CHUNK
    bpc = LRU_CHUNK // LRU_BLOCK

    def dense(w):
        wc = w.reshape(nch, bpc, LRU_BLOCK, LRU_BLOCK)
        eye = jnp.eye(bpc, dtype=w.dtype)
        return jnp.einsum("cbij,bd->cbidj", wc, eye).reshape(nch, LRU_CHUNK, LRU_CHUNK)

    wcat = jnp.concatenate([dense(w_r[0]), dense(w_i[0]), dense(w_r[1]), dense(w_i[1])], axis=-1)
    chunk = lambda v: v.reshape(nch, 1, LRU_CHUNK)
    gbias = jnp.concatenate([chunk(b_r[0]), chunk(b_i[0]), chunk(b_r[1]), chunk(b_i[1])], axis=-1)
    return (0.5 * wcat).astype(BF16), (0.5 * gbias).astype(F32)


def _route(s, sel, route_ref):
    srow = [s[e:e + 1, :] for e in range(N_EXPERTS)]
    lrow = [sel[e:e + 1, :] for e in range(N_EXPERTS)]
    gscore = []
    for g in range(N_EXPERT_GROUPS):
        a = lrow[g * EXPERTS_PER_GROUP:(g + 1) * EXPERTS_PER_GROUP]
        best = a[0] + a[1]
        for i, j in ((0, 2), (0, 3), (1, 2), (1, 3), (2, 3)):
            best = jnp.maximum(best, a[i] + a[j])
        gscore.append(best)
    bg = jnp.zeros_like(gscore[0], dtype=jnp.int32)
    bv = gscore[0]
    for g in range(1, N_EXPERT_GROUPS):
        upd = gscore[g] > bv
        bg = jnp.where(upd, g, bg)
        bv = jnp.where(upd, gscore[g], bv)

    def pick(rows_):
        out = []
        for j in range(EXPERTS_PER_GROUP):
            v = rows_[j]
            for g in range(1, N_EXPERT_GROUPS):
                v = jnp.where(bg == g, rows_[g * EXPERTS_PER_GROUP + j], v)
            out.append(v)
        return out
    cand = pick(lrow)
    cs = pick(srow)
    i1 = jnp.zeros_like(bg)
    v1 = cand[0]
    w1 = cs[0]
    for j in range(1, EXPERTS_PER_GROUP):
        upd = cand[j] > v1
        i1 = jnp.where(upd, j, i1)
        v1 = jnp.where(upd, cand[j], v1)
        w1 = jnp.where(upd, cs[j], w1)
    i2 = jnp.full_like(bg, -1)
    v2 = jnp.full_like(v1, -jnp.inf)
    w2 = jnp.zeros_like(w1)
    for j in range(EXPERTS_PER_GROUP):
        upd = (i1 != j) & (cand[j] > v2)
        i2 = jnp.where(upd, j, i2)
        v2 = jnp.where(upd, cand[j], v2)
        w2 = jnp.where(upd, cs[j], w2)
    den = w1 + w2
    g1 = w1 / den
    g2 = w2 / den
    for j in range(EXPERTS_PER_GROUP):
        route_ref[j:j + 1, :] = jnp.where(i1 == j, g1, 0.0) + jnp.where(i2 == j, g2, 0.0)
    route_ref[ROUTE_GID_ROW:ROUTE_GID_ROW + 1, :] = bg.astype(F32)
    pad = SUBLANES - ROUTE_GID_ROW - 1
    route_ref[ROUTE_GID_ROW + 1:, :] = jnp.zeros((pad, bg.shape[1]), F32)


POST_SUBTILE = 512


def _post_kernel(*refs, n_parts):
    parts = refs[:n_parts]
    (w_ref, x_ref, gate_ref, shift_ref, scale_ref, gain_ref, rw_ref, rb_ref,
     x1_ref, h2_ref, route_ref) = refs[n_parts:]
    for sub in range(x_ref.shape[0] // POST_SUBTILE):
        rows = pl.ds(sub * POST_SUBTILE, POST_SUBTILE)
        pieces = []
        for p in parts:
            pieces += [p[c, rows, :] for c in range(p.shape[0])] if len(p.shape) == 3 else [p[rows, :]]
        mixed = jnp.concatenate(pieces, axis=-1) if len(pieces) > 1 else pieces[0]
        mix = jnp.dot(mixed, w_ref[...], preferred_element_type=F32)
        x1 = x_ref[rows, :] + gate_ref[0] * mix
        x1_ref[rows, :] = x1
        h2 = _norm_modulate(x1, gain_ref[...], shift_ref[0], scale_ref[0])
        h_hi = h2.astype(BF16)
        h2_ref[rows, :] = h_hi
        logits = jnp.dot(h_hi, rw_ref[...], preferred_element_type=F32)
        s = _sigmoid(logits[:, :LANES].T[:N_EXPERTS, :])
        _route(s, s + rb_ref[...], route_ref.at[:, rows])


def _post_mixer(parts, w, x2d, gate1, shift2, scale2, gain, rw_cat, rbias, tokens_per_batch, tm):
    t, d = x2d.shape
    tpb = tokens_per_batch // tm
    full = lambda i: (0, 0)
    per_b = lambda i: (i // tpb, 0, 0)

    def part_spec(p):
        if p.ndim == 3:
            return pl.BlockSpec((p.shape[0], tm, p.shape[2]), lambda i: (0, i, 0))
        return pl.BlockSpec((tm, p.shape[1]), lambda i: (i, 0))

    return pl.pallas_call(
        functools.partial(_post_kernel, n_parts=len(parts)),
        out_shape=[jax.ShapeDtypeStruct((t, d), F32), jax.ShapeDtypeStruct((t, d), BF16),
                   jax.ShapeDtypeStruct((SUBLANES, t), F32)],
        grid=(t // tm,),
        in_specs=[part_spec(p) for p in parts] + [
                  pl.BlockSpec(w.shape, full),
                  pl.BlockSpec((tm, d), lambda i: (i, 0)),
                  pl.BlockSpec((1, 1, d), per_b),
                  pl.BlockSpec((1, 1, d), per_b),
                  pl.BlockSpec((1, 1, d), per_b),
                  pl.BlockSpec((1, d), full),
                  pl.BlockSpec(rw_cat.shape, full),
                  pl.BlockSpec(rbias.shape, full)],
        out_specs=[pl.BlockSpec((tm, d), lambda i: (i, 0)),
                   pl.BlockSpec((tm, d), lambda i: (i, 0)),
                   pl.BlockSpec((SUBLANES, tm), lambda i: (0, i))],
        compiler_params=_cparams(("arbitrary",)),
        name="post_mixer",
    )(*parts, w, x2d, gate1, shift2, scale2, gain, rw_cat, rbias)


def _moe_layout(t):
    nt = t // MOE_TILE
    grid = -(-(t + N_EXPERT_GROUPS * (ROW_ALIGN - 1) * nt) // MOE_TILE) + N_EXPERT_GROUPS
    return nt, grid


def _moe_tables(gid, t):
    nt, grid = _moe_layout(t)
    ng = N_EXPERT_GROUPS
    per_tile = MOE_TILE // ROW_ALIGN
    onehot = (gid.reshape(nt, MOE_TILE, 1) == jnp.arange(ng, dtype=jnp.int32)).astype(jnp.int32)
    cnt = onehot.sum(axis=1)
    seg = (cnt + ROW_ALIGN - 1) // ROW_ALIGN
    src = jnp.cumsum(seg, axis=1) - seg
    fill = seg.sum(axis=0)
    ntile = (fill + per_tile - 1) // per_tile
    cum = jnp.cumsum(ntile)
    base = (cum - ntile) * per_tile
    dst = jnp.cumsum(seg, axis=0) - seg + base[None, :]
    seg_tab = jnp.concatenate([seg, src, dst], axis=1).reshape(-1).astype(jnp.int32)
    tail = (-fill) % per_tile
    tail_tab = jnp.concatenate([tail, fill + base, cum[-1:]]).astype(jnp.int32)
    i = jnp.arange(grid, dtype=jnp.int32)
    valid = i < cum[-1]
    ie = jnp.minimum(i, cum[-1] - 1)
    g_of = jnp.sum((ie[:, None] >= cum[None, :]).astype(jnp.int32), axis=1)
    return seg_tab, tail_tab, g_of.astype(jnp.int32), valid.astype(jnp.int32)


def _segment_copies(tab_ref, tile, enable, make_copy):
    ng = N_EXPERT_GROUPS
    base = jnp.maximum(tile, 0) * (3 * ng)
    out = []
    for g in range(ng):
        n = tab_ref[base + g]
        src = tab_ref[base + ng + g]
        dst = tab_ref[base + 2 * ng + g]
        for k in range(MOE_SEG_BITS - 1, -1, -1):
            done = (n >> (k + 1)) << (k + 1)
            rows = ROW_ALIGN << k
            s0 = pl.multiple_of((src + done) * ROW_ALIGN, ROW_ALIGN)
            d0 = pl.multiple_of((dst + done) * ROW_ALIGN, ROW_ALIGN)
            out.append((enable & (((n >> k) & 1) == 1), make_copy(s0, d0, rows)))
    return out


def _start_copies(pairs):
    for cond, copies in pairs:
        @pl.when(cond)
        def _():
            for c in copies:
                c.start()


def _wait_copies(pairs):
    for cond, copies in pairs:
        @pl.when(cond)
        def _():
            for c in copies:
                c.wait()


def _split_bf16x3(x):
    hi = x.astype(BF16).astype(F32)
    r1 = x - hi
    mid = r1.astype(BF16).astype(F32)
    lo = (r1 - mid).astype(BF16).astype(F32)
    return hi, mid, lo


def _dispatch_kernel(seg_ref, tail_ref, h_ref, route_ref, tri_ref, slot_ref, hs_ref, cbuf, zbuf, sem, *, nt):
    i = pl.program_id(0)
    tm, d = h_ref.shape
    ng = N_EXPERT_GROUPS
    cur = i % 2

    def seg_copies(tile, enable, buf):
        def seg_copy(s0, d0, rows):
            return (pltpu.make_async_copy(cbuf.at[buf, pl.ds(s0, rows)], hs_ref.at[pl.ds(d0, rows)], sem.at[buf]),)
        return _segment_copies(seg_ref, tile, enable, seg_copy)

    _wait_copies(seg_copies(i - 2, i >= 2, cur))

    route = route_ref[...]
    gid = route[ROUTE_GID_ROW:ROUTE_GID_ROW + 1, :]
    grp = lax.broadcasted_iota(jnp.int32, (SUBLANES, tm), 0).astype(F32)
    onehot = jnp.where(grp == gid, 1.0, 0.0)
    rank = jnp.dot(onehot.astype(BF16), tri_ref[...], preferred_element_type=F32)
    slot = jnp.zeros((1, tm), F32)
    for g in range(ng):
        start = (seg_ref[i * 3 * ng + ng + g] * ROW_ALIGN).astype(F32)
        slot = slot + onehot[g:g + 1, :] * (rank[g:g + 1, :] - 1.0 + start)
    slot_ref[...] = jnp.broadcast_to(slot, (SUBLANES, tm))
    perm = jnp.where(lax.broadcasted_iota(jnp.int32, (MOE_CROWS, tm), 0).astype(F32) == slot, 1.0, 0.0)
    perm = perm.astype(BF16)
    cbuf[cur, :, :d] = jnp.dot(perm, h_ref[...], preferred_element_type=F32).astype(cbuf.dtype)
    parts = jnp.concatenate(list(_split_bf16x3(route)) + [jnp.zeros((LANES - 3 * SUBLANES, tm), F32)], axis=0)
    record = lax.dot_general(perm, parts.astype(BF16), (((1,), (1,)), ((), ())), preferred_element_type=F32)
    cbuf[cur, :, d:] = record.astype(cbuf.dtype)
    _start_copies(seg_copies(i, i >= 0, cur))

    @pl.when(i == pl.num_programs(0) - 1)
    def _():
        _wait_copies(seg_copies(i - 1, i >= 1, 1 - cur))
        _wait_copies(seg_copies(i, i >= 0, cur))
        zbuf[...] = jnp.zeros(zbuf.shape, zbuf.dtype)

        def zero_copy(d0, rows):
            return (pltpu.make_async_copy(zbuf.at[pl.ds(0, rows)], hs_ref.at[pl.ds(d0, rows)], sem.at[0]),)
        pairs = []
        for g in range(ng):
            n = tail_ref[g]
            dst = tail_ref[ng + g]
            for k in range(MOE_TAIL_BITS - 1, -1, -1):
                done = (n >> (k + 1)) << (k + 1)
                d0 = pl.multiple_of((dst + done) * ROW_ALIGN, ROW_ALIGN)
                pairs.append((((n >> k) & 1) == 1, zero_copy(d0, ROW_ALIGN << k)))
        used = tail_ref[2 * ng]
        total = hs_ref.shape[0] // MOE_TILE
        for j in range(total - nt):
            d0 = pl.multiple_of(jnp.minimum(used + j, total - 1) * MOE_TILE, MOE_TILE)
            pairs.append((used + j < total, zero_copy(d0, MOE_TILE)))
        _start_copies(pairs)
        _wait_copies(pairs)


def _dispatch(seg_tab, tail_tab, h2, route, tri):
    t, d = h2.shape
    nt, grid = _moe_layout(t)
    rows = grid * MOE_TILE
    grid_spec = pltpu.PrefetchScalarGridSpec(
        num_scalar_prefetch=2,
        grid=(nt,),
        in_specs=[pl.BlockSpec((MOE_TILE, d), lambda i, *_: (i, 0)),
                  pl.BlockSpec((SUBLANES, MOE_TILE), lambda i, *_: (0, i)),
                  pl.BlockSpec((MOE_TILE, MOE_TILE), lambda i, *_: (0, 0))],
        out_specs=[pl.BlockSpec((SUBLANES, MOE_TILE), lambda i, *_: (0, i)),
                   pl.BlockSpec(memory_space=pl.ANY)],
        scratch_shapes=[pltpu.VMEM((2, MOE_CROWS, d + LANES), BF16), pltpu.VMEM((MOE_TILE, d + LANES), BF16),
                        pltpu.SemaphoreType.DMA((2,))])
    return pl.pallas_call(
        functools.partial(_dispatch_kernel, nt=nt),
        out_shape=[jax.ShapeDtypeStruct((SUBLANES, t), F32),
                   jax.ShapeDtypeStruct((rows, d + LANES), BF16)],
        grid_spec=grid_spec,
        compiler_params=_cparams(("arbitrary",)),
        name="moe_dispatch",
    )(seg_tab, tail_tab, h2, route, tri)


def _ffn_kernel(grp_ref, valid_ref, h_ref, wg32_ref, wu32_ref, wd32_ref, y_ref, wg_ref, wu_ref, wd_ref):
    i = pl.program_id(0)
    d = y_ref.shape[1]

    @pl.when((i == 0) | (grp_ref[i] != grp_ref[jnp.maximum(i - 1, 0)]))
    def _():
        for j in range(EXPERTS_PER_GROUP):
            cols = slice(j * D_FF_EXPERT, (j + 1) * D_FF_EXPERT)
            wg_ref[:, cols] = wg32_ref[j].astype(BF16)
            wu_ref[:, cols] = wu32_ref[j].astype(BF16)
            wd_ref[cols, :] = wd32_ref[j].astype(BF16)

    @pl.when(valid_ref[i] == 0)
    def _():
        y_ref[...] = jnp.zeros(y_ref.shape, y_ref.dtype)

    @pl.when(valid_ref[i] == 1)
    def _():
        h = h_ref[:, :d]
        gates = h_ref[:, d:].astype(F32)
        acts = []
        for j in range(EXPERTS_PER_GROUP):
            cols = slice(j * D_FF_EXPERT, (j + 1) * D_FF_EXPERT)
            a = jnp.dot(h, wg_ref[:, cols], preferred_element_type=F32)
            u = jnp.dot(h, wu_ref[:, cols], preferred_element_type=F32)
            gate = (gates[:, j:j + 1] + gates[:, SUBLANES + j:SUBLANES + j + 1]
                    + gates[:, 2 * SUBLANES + j:2 * SUBLANES + j + 1])
            acts.append(((a * _sigmoid_tanh(a)) * u * gate).astype(BF16))
        y = jnp.dot(jnp.concatenate(acts, axis=1), wd_ref[...], preferred_element_type=F32)
        y_ref[...] = y.astype(y_ref.dtype)


def _ffn(grp, valid, hs, wg, wu, wd, layer):
    rows, width = hs.shape
    d = width - LANES
    epg = EXPERTS_PER_GROUP
    once = pl.Buffered(1)
    w_idx = lambda i, grp, valid: (layer, grp[i], 0, 0)
    grid_spec = pltpu.PrefetchScalarGridSpec(
        num_scalar_prefetch=2,
        grid=(rows // MOE_TILE,),
        in_specs=[pl.BlockSpec((MOE_TILE, width), lambda i, grp, valid: (i, 0)),
                  pl.BlockSpec((None, epg, d, D_FF_EXPERT), w_idx, pipeline_mode=once),
                  pl.BlockSpec((None, epg, d, D_FF_EXPERT), w_idx, pipeline_mode=once),
                  pl.BlockSpec((None, epg, D_FF_EXPERT, d), w_idx, pipeline_mode=once)],
        out_specs=pl.BlockSpec((MOE_TILE, d), lambda i, grp, valid: (i, 0)),
        scratch_shapes=[pltpu.VMEM((d, epg * D_FF_EXPERT), BF16), pltpu.VMEM((d, epg * D_FF_EXPERT), BF16),
                        pltpu.VMEM((epg * D_FF_EXPERT, d), BF16)])
    return pl.pallas_call(
        _ffn_kernel,
        out_shape=jax.ShapeDtypeStruct((rows, d), BF16),
        grid_spec=grid_spec,
        compiler_params=_cparams(("arbitrary",)),
        name="moe_ffn",
    )(grp, valid, hs, wg, wu, wd)


def _combine_kernel(seg_ref, x1_ref, slot_ref, gate2_ref, ys_ref, *rest, fnet):
    if fnet:
        shift_ref, scale_ref, gain_ref, cs_ref, o_ref, y1_ref, y2_ref, ybuf, sem = rest
    else:
        o_ref, ybuf, sem = rest
    i = pl.program_id(0)
    nt = pl.num_programs(0)
    tm = x1_ref.shape[0]
    cur = i % 2

    def seg_copies(tile, enable, buf):
        def seg_copy(s0, d0, rows):
            return (pltpu.make_async_copy(ys_ref.at[pl.ds(d0, rows)], ybuf.at[buf, pl.ds(s0, rows)], sem.at[buf]),)
        return _segment_copies(seg_ref, tile, enable, seg_copy)

    @pl.when(i == 0)
    def _():
        ybuf[...] = jnp.zeros(ybuf.shape, ybuf.dtype)
        _start_copies(seg_copies(i, i == 0, cur))

    nxt = jnp.minimum(i + 1, nt - 1)
    _start_copies(seg_copies(nxt, i + 1 < nt, 1 - cur))
    _wait_copies(seg_copies(i, i >= 0, cur))
    slot = slot_ref[0:1, :]
    perm = jnp.where(lax.broadcasted_iota(jnp.int32, (MOE_CROWS, tm), 0).astype(F32) == slot, 1.0, 0.0)
    y = lax.dot_general(perm.astype(BF16), ybuf[cur], (((0,), (0,)), ((), ())), preferred_element_type=F32)
    x = x1_ref[...] + gate2_ref[0] * y
    o_ref[...] = x
    if fnet:
        _fnet_channel_tile(x, shift_ref, scale_ref, gain_ref, cs_ref, y1_ref, y2_ref)


def _combine(seg_tab, x1, slot, gate2, ys, tokens_per_batch, fnet=None):
    t, d = x1.shape
    tpb = tokens_per_batch // MOE_TILE
    per_b = lambda i, *_: (i // tpb, 0, 0)
    full = lambda i, *_: (0, 0)
    in_specs = [pl.BlockSpec((MOE_TILE, d), lambda i, *_: (i, 0)),
                pl.BlockSpec((SUBLANES, MOE_TILE), lambda i, *_: (0, i)),
                pl.BlockSpec((1, 1, d), per_b),
                pl.BlockSpec(memory_space=pl.ANY)]
    out_shape = [jax.ShapeDtypeStruct((t, d), F32)]
    out_specs = [pl.BlockSpec((MOE_TILE, d), lambda i, *_: (i, 0))]
    args = [seg_tab, x1, slot, gate2, ys]
    if fnet is not None:
        shift, scale, gain, cs = fnet
        gw = d // FNET_GROUPS
        in_specs += [pl.BlockSpec((1, 1, d), per_b), pl.BlockSpec((1, 1, d), per_b), pl.BlockSpec((1, d), full),
                     pl.BlockSpec(cs.shape, full)]
        out_shape += [jax.ShapeDtypeStruct((FNET_GROUPS, t, gw), BF16)] * 2
        out_specs += [pl.BlockSpec((FNET_GROUPS, MOE_TILE, gw), lambda i, *_: (0, i, 0))] * 2
        args += [shift, scale, gain, cs]
    grid_spec = pltpu.PrefetchScalarGridSpec(
        num_scalar_prefetch=1,
        grid=(t // MOE_TILE,),
        in_specs=in_specs,
        out_specs=out_specs,
        scratch_shapes=[pltpu.VMEM((2, MOE_CROWS, d), BF16), pltpu.SemaphoreType.DMA((2,))])
    out = pl.pallas_call(
        functools.partial(_combine_kernel, fnet=fnet is not None),
        out_shape=out_shape,
        grid_spec=grid_spec,
        compiler_params=_cparams(("arbitrary",)),
        name="moe_combine_fnet" if fnet is not None else "moe_combine",
    )(*args)
    return out if fnet is not None else out[0]


def _grouped_moe(h2, route, x1, gate2, wg, wu, wd, layer, tri):
    t = h2.shape[0]
    gid = route[ROUTE_GID_ROW].astype(jnp.int32)
    seg_tab, tail_tab, grp, valid = _moe_tables(gid, t)
    slot, hs = _dispatch(seg_tab, tail_tab, h2, route, tri)
    ys = _ffn(grp, valid, hs, wg, wu, wd, layer)
    return seg_tab, x1, slot, gate2, ys


FFT_J = SUBLANES
FFT_PAIR_ROWS = 2 * FFT_J * (GRID_W // FFT_J)


def _pair_grid_rows(y):
    rows = y.shape[0]
    assert rows % (2 * GRID_W) == 0
    groups = []
    for u in range(rows // (2 * GRID_W)):
        for sb in range(GRID_W // FFT_J):
            for e in range(2):
                start = GRID_W * (2 * u + e) + FFT_J * sb
                groups.append(y[start:start + FFT_J])
    return jnp.concatenate(groups, axis=0)


def _fnet_channel_tile(x, shift_ref, scale_ref, gain_ref, cs_ref, y1_ref, y2_ref):
    h = _norm_modulate(x, gain_ref[...], shift_ref[0], scale_ref[0]).astype(BF16)
    gw = D_MODEL // FNET_GROUPS
    for g in range(FNET_GROUPS):
        y = _pair_grid_rows(jnp.dot(h[:, g * gw:(g + 1) * gw], cs_ref[...], preferred_element_type=F32))
        y1_ref[g] = y[:, :gw].astype(y1_ref.dtype)
        y2_ref[g] = y[:, gw:].astype(y2_ref.dtype)


def _fnet_chan_kernel(x_ref, shift_ref, scale_ref, gain_ref, cs_ref, y1_ref, y2_ref):
    _fnet_channel_tile(x_ref[...], shift_ref, scale_ref, gain_ref, cs_ref, y1_ref, y2_ref)


def _fnet_channel(x2d, shift, scale, gain, cs, tokens_per_batch, tm):
    t, d = x2d.shape
    tpb = tokens_per_batch // tm
    gw = d // FNET_GROUPS
    full = lambda i: (0, 0)
    out = jax.ShapeDtypeStruct((FNET_GROUPS, t, gw), BF16)
    out_spec = pl.BlockSpec((FNET_GROUPS, tm, gw), lambda i: (0, i, 0))
    return pl.pallas_call(
        _fnet_chan_kernel,
        out_shape=[out, out],
        grid=(t // tm,),
        in_specs=[pl.BlockSpec((tm, d), lambda i: (i, 0)),
                  pl.BlockSpec((1, 1, d), lambda i: (i // tpb, 0, 0)),
                  pl.BlockSpec((1, 1, d), lambda i: (i // tpb, 0, 0)),
                  pl.BlockSpec((1, d), full),
                  pl.BlockSpec(cs.shape, full)],
        out_specs=[out_spec, out_spec],
        compiler_params=_cparams(("arbitrary",)),
        name="fnet_channel",
    )(x2d, shift, scale, gain, cs)


def _half_rows(m):
    hc = m // 2 + 1
    return hc, -(-FFT_J * hc // ROW_ALIGN) * ROW_ALIGN


def _mirror(lo, hi, hc, m):
    return jnp.concatenate([lo[:FFT_J * hc]] + [hi[FFT_J * c:FFT_J * (c + 1)] for c in range(m - hc, 0, -1)], axis=0)


def _fnet_pos_kernel(y1_ref, y2_ref, lr_ref, ls_ref, cs_ref, sn_ref, o_ref, z1, a_re, a_im, *, n):
    r1 = n // GRID_W
    gw = y1_ref.shape[2]
    hc_r, part_r = _half_rows(r1)
    hc_s, part_s = _half_rows(GRID_W)

    def stage_r(sb, carry):
        s0 = pl.multiple_of(sb * FFT_J, FFT_J)
        p0 = pl.multiple_of(sb * 2 * FFT_J, 2 * FFT_J)
        rhs1 = jnp.concatenate([y1_ref[0, pl.ds(FFT_PAIR_ROWS * u + p0, 2 * FFT_J), :] for u in range(r1 // 2)], axis=0)
        rhs2 = jnp.concatenate([y2_ref[0, pl.ds(FFT_PAIR_ROWS * u + p0, 2 * FFT_J), :] for u in range(r1 // 2)], axis=0)
        p = jnp.dot(lr_ref[...], rhs1, preferred_element_type=F32)
        q = jnp.dot(lr_ref[...], rhs2, preferred_element_type=F32)
        pc, ps = p[:part_r], p[part_r:]
        qc, qs = q[:part_r], q[part_r:]
        re = _mirror(pc - qs, pc + qs, hc_r, r1)
        nim = _mirror(qc + ps, qc - ps, hc_r, r1)
        cs = jnp.concatenate([cs_ref[sb]] * (gw // LANES), axis=1)
        sn = jnp.concatenate([sn_ref[sb]] * (gw // LANES), axis=1)
        tre = re * cs - nim * sn
        tnim = re * sn + nim * cs
        for c in range(r1):
            a_re[pl.ds(GRID_W * c + s0, FFT_J), :] = tre[FFT_J * c:FFT_J * (c + 1)]
            a_im[pl.ds(GRID_W * c + s0, FFT_J), :] = tnim[FFT_J * c:FFT_J * (c + 1)]
        return carry
    lax.fori_loop(0, GRID_W // FFT_J, stage_r, 0)

    cblk = FFT_J * GRID_W

    def stage_s(cb, carry):
        c0 = pl.multiple_of(cb * cblk, cblk)
        u = jnp.dot(ls_ref[0], a_re[pl.ds(c0, cblk), :].astype(BF16), preferred_element_type=F32)
        v = jnp.dot(ls_ref[1], a_im[pl.ds(c0, cblk), :].astype(BF16), preferred_element_type=F32)
        lo = u - v
        hi = u + v
        k0 = pl.multiple_of(cb * FFT_J, FFT_J)
        for d in range(hc_s):
            z1[pl.ds(r1 * d + k0, FFT_J), :] = lo[FFT_J * d:FFT_J * (d + 1)]
        for d in range(1, GRID_W - hc_s + 1):
            z1[pl.ds(r1 * (GRID_W - d) + k0, FFT_J), :] = hi[FFT_J * d:FFT_J * (d + 1)]
        return carry
    lax.fori_loop(0, r1 // FFT_J, stage_s, 0)
    o_ref[0] = z1[...].astype(o_ref.dtype)


def _fnet_position(y1, y2, lr, ls, tw_cos, tw_sin, batch, n):
    groups, t, gw = y1.shape
    full2 = lambda b, g: (0, 0)
    full3 = lambda b, g: (0, 0, 0)
    scratch = pltpu.VMEM((n, gw), F32)
    return pl.pallas_call(
        functools.partial(_fnet_pos_kernel, n=n),
        out_shape=jax.ShapeDtypeStruct((groups, t, gw), BF16),
        grid=(batch, groups),
        in_specs=[pl.BlockSpec((1, n, gw), lambda b, g: (g, b, 0)),
                  pl.BlockSpec((1, n, gw), lambda b, g: (g, b, 0)),
                  pl.BlockSpec(lr.shape, full2),
                  pl.BlockSpec(ls.shape, full3),
                  pl.BlockSpec(tw_cos.shape, full3),
                  pl.BlockSpec(tw_sin.shape, full3)],
        out_specs=pl.BlockSpec((1, n, gw), lambda b, g: (g, b, 0)),
        scratch_shapes=[scratch, scratch, scratch],
        compiler_params=_cparams(("arbitrary", "arbitrary")),
        name="fnet_position",
    )(y1, y2, lr, ls, tw_cos, tw_sin)


def _dft_tables(n):
    assert n % (GRID_W * FFT_J) == 0
    gw = D_MODEL // FNET_GROUPS
    j = np.arange(gw)
    ang = 2.0 * np.pi * ((j[:, None] * j[None, :]) % gw) / gw
    cs = np.concatenate([np.cos(ang), np.sin(ang)], axis=1) / np.sqrt(gw)
    r1 = n // GRID_W
    assert r1 % 2 == 0
    eye = np.eye(FFT_J)
    scale = float(n) ** -0.25
    a = np.arange(r1)
    hc_r, part_r = _half_rows(r1)
    ang_r = 2.0 * np.pi * ((a[:hc_r, None] * a[None, :]) % r1) / r1
    lr = np.zeros((2 * part_r, FFT_J * r1))
    lr[:FFT_J * hc_r] = np.kron(np.cos(ang_r), eye) * scale
    lr[part_r:part_r + FFT_J * hc_r] = np.kron(np.sin(ang_r), eye) * scale
    s = np.arange(GRID_W)
    hc_s, part_s = _half_rows(GRID_W)
    ang_s = 2.0 * np.pi * ((s[:hc_s, None] * s[None, :]) % GRID_W) / GRID_W
    ls = np.zeros((2, part_s, FFT_J * GRID_W))
    ls[0, :FFT_J * hc_s] = np.einsum("ds,cC->dcCs", np.cos(ang_s), eye).reshape(FFT_J * hc_s, FFT_J * GRID_W) * scale
    ls[1, :FFT_J * hc_s] = np.einsum("ds,cC->dcCs", np.sin(ang_s), eye).reshape(FFT_J * hc_s, FFT_J * GRID_W) * scale
    sb = np.arange(GRID_W // FFT_J)
    s_of = sb[:, None, None] * FFT_J + np.arange(FFT_J)[None, None, :]
    ang_t = 2.0 * np.pi * ((s_of * a[None, :, None]) % n) / n
    ang_t = ang_t.reshape(len(sb), r1 * FFT_J, 1)
    tw_cos = jnp.broadcast_to(jnp.asarray(np.cos(ang_t), F32), (len(sb), r1 * FFT_J, LANES))
    tw_sin = jnp.broadcast_to(jnp.asarray(np.sin(ang_t), F32), (len(sb), r1 * FFT_J, LANES))
    return jnp.asarray(cs, BF16), jnp.asarray(lr, BF16), jnp.asarray(ls, BF16), tw_cos, tw_sin


def kernel(x, c, ctx, c_ctx, ada_w, ada_b, norm_mix, norm_ffn, mix_w_in, mix_w_out, na_q_norm, na_k_norm, na_rpb,
           lru_conv_w, lru_conv_b, lru_gate_r_w, lru_gate_r_b, lru_gate_i_w, lru_gate_i_b, lru_lambda,
           fnet_w_out, router_w, router_bias, moe_w_gate, moe_w_up, moe_w_down):
    batch, n, d = x.shape
    ctx_len = ctx.shape[1]
    depth = ada_w.shape[0]
    rows = n // GRID_W
    assert d == D_MODEL and n % (GRID_W * NA_QROWS) == 0 and rows >= 4 * NA_QROWS
    assert n % MOE_TILE == 0
    t = batch * n
    tm = 512
    tri = jnp.asarray(np.triu(np.ones((MOE_TILE, MOE_TILE))), BF16)

    r_pad = -(-(batch + 1) // SUBLANES) * SUBLANES
    c_rows = jnp.concatenate([c, c_ctx[None, :], jnp.zeros((r_pad - batch - 1, d), c.dtype)], axis=0)
    mod = _modulation(c_rows, ada_w, ada_b)

    def mod_slices(layer):
        m = mod[layer, :batch].reshape(batch, 1, 6, d)
        return [m[:, :, i, :] for i in range(6)]

    rw_cat = jnp.pad(router_w.astype(F32), ((0, 0), (0, 2 * LANES - N_EXPERTS))).astype(BF16)
    rbias = router_bias.reshape(N_EXPERTS, 1).astype(F32)
    x2d = x.reshape(t, d)
    ctx2d = ctx.reshape(batch * ctx_len, d)

    pending = None
    for layer in range(depth):
        li = layer // 2
        shift1, scale1, gate1, shift2, scale2, gate2 = mod_slices(layer)
        gain_mix = norm_mix[layer].reshape(1, d)
        gain_ffn = norm_ffn[layer].reshape(1, d)
        if layer % 2 == 0:
            if pending is not None:
                x2d = _combine(*pending, n)
            w_in = mix_w_in[li].astype(BF16)
            ind = jnp.asarray(np.kron(np.eye(NA_HEADS // 2), np.ones((HEAD_DIM, HEAD_DIM))), BF16)
            qg = (jnp.tile(na_q_norm[li], NA_HEADS) * (HEAD_DIM ** -0.5 * LOG2_E)).reshape(1, NA_WIDTH).astype(F32)
            kg = jnp.tile(na_k_norm[li], NA_HEADS).reshape(1, NA_WIDTH).astype(F32)
            q, k, v, xb, gb = _inproj(x2d, shift1, scale1, gain_mix, w_in, ind, qg, kg,
                                      ("q", "k", "v", "x", "g"), n, 2 * tm)
            mctx = mod[layer, batch, :2 * d]
            shift_c = jnp.broadcast_to(mctx[:d], (batch, 1, d))
            scale_c = jnp.broadcast_to(mctx[d:], (batch, 1, d))
            k_c, v_c, xb_c = _inproj(ctx2d, shift_c, scale_c, gain_mix, w_in[:, NA_WIDTH:4 * NA_WIDTH], ind, qg, kg,
                                     ("k", "v", "x"), ctx_len, ctx_len)
            bias = _na_bias_tables(na_rpb[li], rows)
            attn = _attention(q, k, v, k_c, v_c, bias, batch, n, ctx_len)
            wcat, gbias = _lru_gate_weights(lru_gate_r_w[li], lru_gate_r_b[li], lru_gate_i_w[li], lru_gate_i_b[li])
            lru = _lru(xb, gb, xb_c, lru_conv_w[li].astype(F32), lru_conv_b[li].reshape(1, LRU_WIDTH).astype(F32),
                       wcat, gbias, lru_lambda[li].astype(F32), batch, n, ctx_len)
            parts, w_out = [attn, lru], mix_w_out[li].astype(BF16)
        else:
            cs, lr, ls, tw_cos, tw_sin = _dft_tables(n)
            if pending is not None:
                x2d, y1, y2 = _combine(*pending, n, fnet=(shift1, scale1, gain_mix, cs))
            else:
                y1, y2 = _fnet_channel(x2d, shift1, scale1, gain_mix, cs, n, tm)
            parts, w_out = [_fnet_position(y1, y2, lr, ls, tw_cos, tw_sin, batch, n)], fnet_w_out[li].astype(BF16)
        x1, h2, route = _post_mixer(parts, w_out, x2d, gate1, shift2, scale2, gain_ffn, rw_cat, rbias, n,
                                    2 * POST_SUBTILE)
        pending = _grouped_moe(h2, route, x1, gate2, moe_w_gate.astype(F32), moe_w_up.astype(F32),
                               moe_w_down.astype(F32), layer, tri)
    return _combine(*pending, n).reshape(batch, n, d)
```

```python
import functools

import numpy as np
import jax
import jax.numpy as jnp
from jax import lax
from jax.experimental import pallas as pl
from jax.experimental.pallas import tpu as pltpu

F32 = jnp.float32
BF16 = jnp.bfloat16
HIGHEST = lax.Precision.HIGHEST

D_MODEL = 1024
GRID_W = 64
HEAD_DIM = 64
NA_HEADS = 8
NA_WIDTH = NA_HEADS * HEAD_DIM
NA_WIN_ROWS = 8
NA_WIN_COLS = 16
LRU_WIDTH = 512
LRU_BLOCK = 64
LRU_C = 8.0
FNET_GROUPS = 4
N_EXPERTS = 16
EXPERTS_PER_GROUP = 4
N_EXPERT_GROUPS = 4
D_FF_EXPERT = 512
RMS_EPS = 1e-6
MASK_VALUE = -1e30
LOG2_E = 1.4426950408889634

V7X_VMEM_LIMIT_BYTES = 56 * 1024 * 1024
LANES = 128
SUBLANES = 8

NA_QROWS = 4
NA_KROWS = NA_QROWS + NA_WIN_ROWS - 1
NA_QBLK = NA_QROWS * GRID_W
NA_KBLK = NA_KROWS * GRID_W
NA_STEP_BLOCKS = 2

LRU_CHUNK = LANES
LRU_TROWS = 512

ROUTE_GID_ROW = EXPERTS_PER_GROUP
MOE_TILE = 512
ROW_ALIGN = 16
MOE_CROWS = MOE_TILE + N_EXPERT_GROUPS * ROW_ALIGN
MOE_SEG_BITS = (MOE_TILE // ROW_ALIGN).bit_length()
MOE_TAIL_BITS = (MOE_TILE // ROW_ALIGN - 1).bit_length()


def _sigmoid(x):
    return 1.0 / (1.0 + jnp.exp(-x))


def _sigmoid_tanh(x):
    return 0.5 + 0.5 * jnp.tanh(0.5 * x)


def _cparams(sem, vmem=V7X_VMEM_LIMIT_BYTES):
    return pltpu.CompilerParams(dimension_semantics=sem, vmem_limit_bytes=vmem)


def _mod_kernel(c_ref, w_ref, b_ref, o_ref):
    c = c_ref[...]
    s = c * _sigmoid(c)
    o_ref[0] = jnp.dot(s, w_ref[0], precision=HIGHEST, preferred_element_type=F32) + b_ref[0]


def _modulation(c_rows, ada_w, ada_b):
    depth, d, n6 = ada_w.shape
    r = c_rows.shape[0]
    tn = n6 // 2
    return pl.pallas_call(
        _mod_kernel,
        out_shape=jax.ShapeDtypeStruct((depth, r, n6), F32),
        grid=(depth, n6 // tn),
        in_specs=[pl.BlockSpec((r, d), lambda l, j: (0, 0)),
                  pl.BlockSpec((1, d, tn), lambda l, j: (l, 0, j)),
                  pl.BlockSpec((1, 1, tn), lambda l, j: (l, 0, j))],
        out_specs=pl.BlockSpec((1, r, tn), lambda l, j: (l, 0, j)),
        compiler_params=_cparams(("arbitrary", "arbitrary")),
        name="adaln_mod",
    )(c_rows, ada_w, ada_b.reshape(depth, 1, n6))


def _norm_modulate(x, gain, shift, scale):
    ms = jnp.mean(x * x, axis=-1, keepdims=True)
    y = x * lax.rsqrt(ms + RMS_EPS) * gain
    return y * (1.0 + scale) + shift


def _inproj_kernel(x_ref, shift_ref, scale_ref, gain_ref, w_ref, ind_ref, qg_ref, kg_ref, *out_refs, segs):
    h = _norm_modulate(x_ref[...], gain_ref[...], shift_ref[0], scale_ref[0]).astype(BF16)
    for s, (kind, o_ref) in enumerate(zip(segs, out_refs)):
        z = jnp.dot(h, w_ref[:, s * NA_WIDTH:(s + 1) * NA_WIDTH], preferred_element_type=F32)
        if kind in ("q", "k"):
            zz = (z * z).astype(BF16)
            hw = ind_ref.shape[0]
            ms = jnp.concatenate([jnp.dot(zz[:, i * hw:(i + 1) * hw], ind_ref[...], preferred_element_type=F32)
                                  for i in range(NA_WIDTH // hw)], axis=1) * (1.0 / HEAD_DIM)
            g = qg_ref[...] if kind == "q" else kg_ref[...]
            z = z * lax.rsqrt(ms + RMS_EPS) * g
        if kind in ("x", "g"):
            for c in range(LRU_WIDTH // LRU_CHUNK):
                o_ref[c] = z[:, c * LRU_CHUNK:(c + 1) * LRU_CHUNK].astype(o_ref.dtype)
        else:
            o_ref[...] = z.astype(o_ref.dtype)


def _inproj(x2d, shift, scale, gain, w, ind, qg, kg, segs, tokens_per_batch, tm):
    t, d = x2d.shape
    tpb = tokens_per_batch // tm
    dt = {"q": BF16, "k": BF16, "v": BF16, "x": F32, "g": F32}
    full = lambda i: (0, 0)
    nch = LRU_WIDTH // LRU_CHUNK

    def out_shape(kind):
        shape = (nch, t, LRU_CHUNK) if kind in ("x", "g") else (t, NA_WIDTH)
        return jax.ShapeDtypeStruct(shape, dt[kind])

    def out_spec(kind):
        if kind in ("x", "g"):
            return pl.BlockSpec((nch, tm, LRU_CHUNK), lambda i: (0, i, 0))
        return pl.BlockSpec((tm, NA_WIDTH), lambda i: (i, 0))

    return pl.pallas_call(
        functools.partial(_inproj_kernel, segs=segs),
        out_shape=[out_shape(k) for k in segs],
        grid=(t // tm,),
        in_specs=[pl.BlockSpec((tm, d), lambda i: (i, 0)),
                  pl.BlockSpec((1, 1, d), lambda i: (i // tpb, 0, 0)),
                  pl.BlockSpec((1, 1, d), lambda i: (i // tpb, 0, 0)),
                  pl.BlockSpec((1, d), full),
                  pl.BlockSpec(w.shape, full),
                  pl.BlockSpec(ind.shape, full),
                  pl.BlockSpec((1, NA_WIDTH), full),
                  pl.BlockSpec((1, NA_WIDTH), full)],
        out_specs=[out_spec(k) for k in segs],
        compiler_params=_cparams(("arbitrary",)),
        name="inproj_" + "".join(segs),
    )(x2d, shift, scale, gain, w, ind, qg, kg)


def _na_bias_tables(rpb, rows):
    kr = NA_WIN_ROWS
    rb_count = rows // NA_QROWS
    cq = np.arange(GRID_W)
    ck = np.arange(GRID_W)
    col_start = np.clip(cq - NA_WIN_COLS // 2, 0, GRID_W - NA_WIN_COLS)
    valid_c = (ck[None, :] >= col_start[:, None]) & (ck[None, :] < col_start[:, None] + NA_WIN_COLS)
    dc = np.clip(ck[None, :] - cq[:, None], 1 - NA_WIN_COLS, NA_WIN_COLS - 1) + (NA_WIN_COLS - 1)
    n_dr, n_dc = 2 * NA_WIN_ROWS - 1, 2 * NA_WIN_COLS - 1
    sel_c = (dc[:, :, None] == np.arange(n_dc)) & valid_c[:, :, None]
    blocks = jnp.einsum("hrc,qkc->hrqk", rpb.astype(F32), jnp.asarray(sel_c, F32), precision=HIGHEST)
    blocks = blocks + jnp.asarray(np.where(valid_c, 0.0, MASK_VALUE), F32)
    blocks = jnp.concatenate([blocks, jnp.full((NA_HEADS, 1, GRID_W, GRID_W), MASK_VALUE, F32)], axis=1)
    blocks = blocks * LOG2_E
    which = []
    for rb in (0, 1, rb_count - 1):
        r = rb * NA_QROWS + np.arange(NA_QROWS)
        ks = int(np.clip(rb * NA_QROWS - kr // 2, 0, rows - NA_KROWS))
        key_r = ks + np.arange(NA_KROWS)
        row_start = np.clip(r - kr // 2, 0, rows - kr)
        valid_r = (key_r[None, :] >= row_start[:, None]) & (key_r[None, :] < row_start[:, None] + kr)
        dr = np.clip(key_r[None, :] - r[:, None] + (NA_WIN_ROWS - 1), 0, n_dr - 1)
        which.append(np.where(valid_r, dr, n_dr))
    return _na_bias_assemble(blocks, which)


def _na_bias_kernel(blk_ref, o_ref, *, which):
    for t, table in enumerate(which):
        @pl.when(pl.program_id(0) == t)
        def _():
            for i in range(NA_QROWS):
                row = jnp.concatenate([blk_ref[0, int(table[i, j])] for j in range(NA_KROWS)], axis=1)
                o_ref[0, 0, i * GRID_W:(i + 1) * GRID_W, :] = row


def _na_bias_assemble(blocks, which):
    heads, nblk = blocks.shape[:2]
    return pl.pallas_call(
        functools.partial(_na_bias_kernel, which=which),
        out_shape=jax.ShapeDtypeStruct((len(which), heads, NA_QBLK, NA_KBLK), F32),
        grid=(len(which), heads),
        in_specs=[pl.BlockSpec((1, nblk, GRID_W, GRID_W), lambda t, h: (h, 0, 0, 0))],
        out_specs=pl.BlockSpec((1, 1, NA_QBLK, NA_KBLK), lambda t, h: (t, h, 0, 0)),
        compiler_params=_cparams(("arbitrary", "arbitrary")),
        name="na_bias",
    )(blocks)


def _attn_kernel(q_ref, k_ref, v_ref, kc_ref, vc_ref, bias_ref, o_ref, *, rows):
    last = rows // NA_QROWS - 1
    nt = (((1,), (1,)), ((), ()))
    ctx_len = kc_ref.shape[0]
    low_half = lax.broadcasted_iota(jnp.int32, (NA_QBLK, LANES), 1) < HEAD_DIM
    for blk in range(NA_STEP_BLOCKS):
        rb = pl.program_id(1) * NA_STEP_BLOCKS + blk
        ks = jnp.clip(rb * NA_QROWS - NA_WIN_ROWS // 2, 0, rows - NA_KROWS)
        kstart = pl.multiple_of(ks * GRID_W, GRID_W)
        geom = jnp.where(rb == 0, 0, jnp.where(rb == last, 2, 1))
        qrows = slice(blk * NA_QBLK, (blk + 1) * NA_QBLK)
        for pair in range(NA_HEADS * HEAD_DIM // LANES):
            ls = slice(pair * LANES, (pair + 1) * LANES)
            q2 = q_ref[qrows, ls]
            k_all = jnp.concatenate([kc_ref[:, ls], k_ref[pl.ds(kstart, NA_KBLK), ls]], axis=0)
            v_all = jnp.concatenate([vc_ref[:, ls], v_ref[pl.ds(kstart, NA_KBLK), ls]], axis=0)
            outs = []
            for half in range(2):
                qh = jnp.where(low_half == (half == 0), q2, jnp.zeros_like(q2))
                s = lax.dot_general(qh, k_all, nt, preferred_element_type=F32)
                s = jnp.concatenate([s[:, :ctx_len], s[:, ctx_len:] + bias_ref[geom, 2 * pair + half]], axis=1)
                m = jnp.max(s, axis=-1, keepdims=True)
                p = jnp.exp2(s - m)
                l = jnp.sum(p, axis=-1, keepdims=True)
                outs.append(jnp.dot(p.astype(BF16), v_all, preferred_element_type=F32) / l)
            o_ref[qrows, ls] = jnp.where(low_half, outs[0], outs[1]).astype(o_ref.dtype)


def _attention(q, k, v, kc, vc, bias, batch, n, ctx_len):
    rows = n // GRID_W
    rbc = rows // (NA_QROWS * NA_STEP_BLOCKS)
    qblk = NA_QBLK * NA_STEP_BLOCKS
    return pl.pallas_call(
        functools.partial(_attn_kernel, rows=rows),
        out_shape=jax.ShapeDtypeStruct((batch * n, NA_WIDTH), BF16),
        grid=(batch, rbc),
        in_specs=[pl.BlockSpec((qblk, NA_WIDTH), lambda b, rb: (b * rbc + rb, 0)),
                  pl.BlockSpec((n, NA_WIDTH), lambda b, rb: (b, 0)),
                  pl.BlockSpec((n, NA_WIDTH), lambda b, rb: (b, 0)),
                  pl.BlockSpec((ctx_len, NA_WIDTH), lambda b, rb: (b, 0)),
                  pl.BlockSpec((ctx_len, NA_WIDTH), lambda b, rb: (b, 0)),
                  pl.BlockSpec(bias.shape, lambda b, rb: (0, 0, 0, 0), pipeline_mode=pl.Buffered(1))],
        out_specs=pl.BlockSpec((qblk, NA_WIDTH), lambda b, rb: (b * rbc + rb, 0)),
        compiler_params=_cparams(("arbitrary", "arbitrary")),
        name="na_attention",
    )(q, k, v, kc, vc, bias)


SCAN_GROUPS = 4
SCAN_CHUNKS = SCAN_GROUPS * SUBLANES


def _scan_pitch(n):
    p = -(-n // SCAN_CHUNKS)
    while p % 8 != 4:
        p += 1
    return p


NEG_LOG2_E = -LOG2_E


def _lru_coeff_tile(half_xc, zh, half_bias, k, d):
    c = LRU_CHUNK
    t_r = jnp.tanh(zh[:, (2 * d) * c:(2 * d + 1) * c] + half_bias[:, (2 * d) * c:(2 * d + 1) * c])
    t_i = jnp.tanh(zh[:, (2 * d + 1) * c:(2 * d + 2) * c] + half_bias[:, (2 * d + 1) * c:(2 * d + 2) * c])
    neg_log_a = k[d:d + 1, :] * (1.0 + t_r)
    a = jnp.exp2(neg_log_a * NEG_LOG2_E)
    one_minus_a2 = jnp.tanh(neg_log_a) * (a * a + 1.0)
    root = jnp.where(one_minus_a2 > 0.0, one_minus_a2 * lax.rsqrt(one_minus_a2), 0.0)
    return a, root * (half_xc + half_xc * t_i)


def _conv_tile(xpad, t0, w, b, rows):
    acc = b + w[0:1, :] * xpad[pl.ds(t0 + SUBLANES - 2, rows), :]
    acc = acc + w[1:2, :] * xpad[pl.ds(t0 + SUBLANES - 1, rows), :]
    acc = acc + w[2:3, :] * xpad[pl.ds(t0 + SUBLANES, rows), :]
    return acc + w[3:4, :] * xpad[pl.ds(t0 + SUBLANES + 1, rows), :]


SCAN_UNROLL = 4


def _group_rows(j, g, pitch):
    return pl.ds(g * SUBLANES * pitch + j, SUBLANES, stride=pitch)


def _chunk_totals(af_ref, bf_ref, ab_ref, bb_ref, pitch):
    def body(j, carry):
        jb = pitch - 1 - j
        out = []
        for g in range(SCAN_GROUPS):
            pf, hf, pb, hb = carry[4 * g:4 * g + 4]
            af = af_ref[_group_rows(j, g, pitch), :]
            ab = ab_ref[_group_rows(jb, g, pitch), :]
            out += [af * pf, af * hf + bf_ref[_group_rows(j, g, pitch), :],
                    ab * pb, ab * hb + bb_ref[_group_rows(jb, g, pitch), :]]
        return tuple(out)
    one = jnp.ones((SUBLANES, LRU_CHUNK), F32)
    zero = jnp.zeros((SUBLANES, LRU_CHUNK), F32)
    res = lax.fori_loop(0, pitch, body, (one, zero, one, zero) * SCAN_GROUPS, unroll=SCAN_UNROLL)
    fwd = [(res[4 * g], res[4 * g + 1]) for g in range(SCAN_GROUPS)]
    bwd = [(res[4 * g + 2], res[4 * g + 3]) for g in range(SCAN_GROUPS)]
    return fwd, bwd


def _chunk_starts(totals, h0, reverse):
    row = lax.broadcasted_iota(jnp.int32, (SUBLANES, LRU_CHUNK), 0)
    starts = [jnp.zeros((SUBLANES, LRU_CHUNK), F32) for _ in range(SCAN_GROUPS)]
    state = h0
    order = range(SCAN_CHUNKS - 1, -1, -1) if reverse else range(SCAN_CHUNKS)
    for c in order:
        g, s = divmod(c, SUBLANES)
        p_end, h_end = totals[g]
        starts[g] = jnp.where(row == s, state, starts[g])
        state = p_end[s:s + 1, :] * state + h_end[s:s + 1, :]
    return starts, state


def _scan_write(af_ref, bf_ref, hf_ref, ab_ref, bb_ref, hb_ref, starts_f, starts_b, pitch):
    def body(j, carry):
        jb = pitch - 1 - j
        out = []
        for g in range(SCAN_GROUPS):
            hf, hb = carry[2 * g:2 * g + 2]
            hf = af_ref[_group_rows(j, g, pitch), :] * hf + bf_ref[_group_rows(j, g, pitch), :]
            hb = ab_ref[_group_rows(jb, g, pitch), :] * hb + bb_ref[_group_rows(jb, g, pitch), :]
            hf_ref[_group_rows(j, g, pitch), :] = hf
            hb_ref[_group_rows(jb, g, pitch), :] = hb
            out += [hf, hb]
        return tuple(out)
    init = tuple(v for g in range(SCAN_GROUPS) for v in (starts_f[g], starts_b[g]))
    lax.fori_loop(0, pitch, body, init, unroll=SCAN_UNROLL)


def _lru_kernel(x_ref, g_ref, xc_ref, cw_ref, cb_ref, w_ref, gb_ref, lam_ref, o_ref,
                xpad, a0, b0, a1, b1, h0s, h1s, ca0, cb0, ca1, cb1, *, n, ctx_len):
    pitch = _scan_pitch(n)
    cpitch = _scan_pitch(ctx_len)
    cw = cw_ref[...]
    cb = cb_ref[...]
    gbias = gb_ref[0]
    lam = lam_ref[...]
    sp = jnp.maximum(-lam, 0.0) + jnp.log1p(jnp.exp(-jnp.abs(lam)))
    k = (0.5 * LRU_C) * sp
    wcat = w_ref[0]
    zeros8 = jnp.zeros((SUBLANES, LRU_CHUNK), F32)

    def fill_coeffs(src_rows, total, length, trows, a_refs, b_refs):
        for d in range(2):
            a_refs[d][pl.ds(length, total - length), :] = jnp.ones((total - length, LRU_CHUNK), F32)
            b_refs[d][pl.ds(length, total - length), :] = jnp.zeros((total - length, LRU_CHUNK), F32)
        xpad[pl.ds(0, SUBLANES), :] = zeros8
        xpad[pl.ds(SUBLANES + length, SUBLANES), :] = zeros8
        xpad[pl.ds(SUBLANES, length), :] = src_rows

        def tile(t, carry):
            t0 = pl.multiple_of(t * trows, SUBLANES)
            xc = _conv_tile(xpad, t0, cw, cb, trows)
            zh = jnp.dot(xc.astype(BF16), wcat, preferred_element_type=F32)
            half_xc = 0.5 * xc
            for d in range(2):
                a, b = _lru_coeff_tile(half_xc, zh, gbias, k, d)
                a_refs[d][pl.ds(t0, trows), :] = a
                b_refs[d][pl.ds(t0, trows), :] = b
            return carry
        lax.fori_loop(0, length // trows, tile, 0)

    fill_coeffs(xc_ref[0], SCAN_CHUNKS * cpitch, ctx_len, ctx_len, (ca0, ca1), (cb0, cb1))
    zero_state = jnp.zeros((1, LRU_CHUNK), F32)
    fwd, bwd = _chunk_totals(ca0, cb0, ca1, cb1, cpitch)
    _, init_f = _chunk_starts(fwd, zero_state, reverse=False)
    _, init_b = _chunk_starts(bwd, zero_state, reverse=True)

    fill_coeffs(x_ref[0], SCAN_CHUNKS * pitch, n, LRU_TROWS, (a0, a1), (b0, b1))
    fwd, bwd = _chunk_totals(a0, b0, a1, b1, pitch)
    starts_f, _ = _chunk_starts(fwd, init_f, reverse=False)
    starts_b, _ = _chunk_starts(bwd, init_b, reverse=True)
    _scan_write(a0, b0, h0s, a1, b1, h1s, starts_f, starts_b, pitch)

    def out_tile(t, carry):
        t0 = pl.multiple_of(t * LRU_TROWS, SUBLANES)
        y = h0s[pl.ds(t0, LRU_TROWS), :] + h1s[pl.ds(t0, LRU_TROWS), :]
        g = g_ref[0, pl.ds(t0, LRU_TROWS), :]
        gelu = 0.5 * g * (1.0 + jnp.tanh(0.7978845608028654 * (g + 0.044715 * (g * g * g))))
        o_ref[0, pl.ds(t0, LRU_TROWS), :] = (gelu * y).astype(o_ref.dtype)
        return carry
    lax.fori_loop(0, n // LRU_TROWS, out_tile, 0)


def _lru(xb, gb, xb_ctx, conv_w, conv_b, wcat, gbias, lam, batch, n, ctx_len):
    nch = LRU_WIDTH // LRU_CHUNK
    pitch = _scan_pitch(n)
    cpitch = _scan_pitch(ctx_len)
    big = pltpu.VMEM((SCAN_CHUNKS * pitch, LRU_CHUNK), F32)
    small = pltpu.VMEM((SCAN_CHUNKS * cpitch, LRU_CHUNK), F32)
    return pl.pallas_call(
        functools.partial(_lru_kernel, n=n, ctx_len=ctx_len),
        out_shape=jax.ShapeDtypeStruct((nch, batch * n, LRU_CHUNK), BF16),
        grid=(batch, nch),
        in_specs=[pl.BlockSpec((1, n, LRU_CHUNK), lambda b, c: (c, b, 0)),
                  pl.BlockSpec((1, n, LRU_CHUNK), lambda b, c: (c, b, 0)),
                  pl.BlockSpec((1, ctx_len, LRU_CHUNK), lambda b, c: (c, b, 0)),
                  pl.BlockSpec((4, LRU_CHUNK), lambda b, c: (0, c)),
                  pl.BlockSpec((1, LRU_CHUNK), lambda b, c: (0, c)),
                  pl.BlockSpec((1, LRU_CHUNK, 4 * LRU_CHUNK), lambda b, c: (c, 0, 0)),
                  pl.BlockSpec((1, 1, 4 * LRU_CHUNK), lambda b, c: (c, 0, 0)),
                  pl.BlockSpec((2, LRU_CHUNK), lambda b, c: (0, c))],
        out_specs=pl.BlockSpec((1, n, LRU_CHUNK), lambda b, c: (c, b, 0)),
        scratch_shapes=[pltpu.VMEM((n + 2 * SUBLANES, LRU_CHUNK), F32),
                        big, big, big, big, big, big, small, small, small, small],
        compiler_params=_cparams(("arbitrary", "arbitrary")),
        name="rglru",
    )(xb, gb, xb_ctx, conv_w, conv_b, wcat, gbias, lam)


def _lru_gate_weights(w_r, b_r, w_i, b_i):
    nch = LRU_WIDTH // LRU_CHUNK
    bpc = LRU_CHUNK // LRU_BLOCK

    def dense(w):
        wc = w.reshape(nch, bpc, LRU_BLOCK, LRU_BLOCK)
        eye = jnp.eye(bpc, dtype=w.dtype)
        return jnp.einsum("cbij,bd->cbidj", wc, eye).reshape(nch, LRU_CHUNK, LRU_CHUNK)

    wcat = jnp.concatenate([dense(w_r[0]), dense(w_i[0]), dense(w_r[1]), dense(w_i[1])], axis=-1)
    chunk = lambda v: v.reshape(nch, 1, LRU_CHUNK)
    gbias = jnp.concatenate([chunk(b_r[0]), chunk(b_i[0]), chunk(b_r[1]), chunk(b_i[1])], axis=-1)
    return (0.5 * wcat).astype(BF16), (0.5 * gbias).astype(F32)


def _route(s, sel, route_ref):
    srow = [s[e:e + 1, :] for e in range(N_EXPERTS)]
    lrow = [sel[e:e + 1, :] for e in range(N_EXPERTS)]
    gscore = []
    for g in range(N_EXPERT_GROUPS):
        a = lrow[g * EXPERTS_PER_GROUP:(g + 1) * EXPERTS_PER_GROUP]
        best = a[0] + a[1]
        for i, j in ((0, 2), (0, 3), (1, 2), (1, 3), (2, 3)):
            best = jnp.maximum(best, a[i] + a[j])
        gscore.append(best)
    bg = jnp.zeros_like(gscore[0], dtype=jnp.int32)
    bv = gscore[0]
    for g in range(1, N_EXPERT_GROUPS):
        upd = gscore[g] > bv
        bg = jnp.where(upd, g, bg)
        bv = jnp.where(upd, gscore[g], bv)

    def pick(rows_):
        out = []
        for j in range(EXPERTS_PER_GROUP):
            v = rows_[j]
            for g in range(1, N_EXPERT_GROUPS):
                v = jnp.where(bg == g, rows_[g * EXPERTS_PER_GROUP + j], v)
            out.append(v)
        return out
    cand = pick(lrow)
    cs = pick(srow)
    i1 = jnp.zeros_like(bg)
    v1 = cand[0]
    w1 = cs[0]
    for j in range(1, EXPERTS_PER_GROUP):
        upd = cand[j] > v1
        i1 = jnp.where(upd, j, i1)
        v1 = jnp.where(upd, cand[j], v1)
        w1 = jnp.where(upd, cs[j], w1)
    i2 = jnp.full_like(bg, -1)
    v2 = jnp.full_like(v1, -jnp.inf)
    w2 = jnp.zeros_like(w1)
    for j in range(EXPERTS_PER_GROUP):
        upd = (i1 != j) & (cand[j] > v2)
        i2 = jnp.where(upd, j, i2)
        v2 = jnp.where(upd, cand[j], v2)
        w2 = jnp.where(upd, cs[j], w2)
    den = w1 + w2
    g1 = w1 / den
    g2 = w2 / den
    for j in range(EXPERTS_PER_GROUP):
        route_ref[j:j + 1, :] = jnp.where(i1 == j, g1, 0.0) + jnp.where(i2 == j, g2, 0.0)
    route_ref[ROUTE_GID_ROW:ROUTE_GID_ROW + 1, :] = bg.astype(F32)
    pad = SUBLANES - ROUTE_GID_ROW - 1
    route_ref[ROUTE_GID_ROW + 1:, :] = jnp.zeros((pad, bg.shape[1]), F32)


POST_SUBTILE = 512
POST_TILE = 2 * POST_SUBTILE
INPROJ_TILE = 1024


def _post_kernel(*refs, n_parts):
    parts = refs[:n_parts]
    (w_ref, x_ref, gate_ref, shift_ref, scale_ref, gain_ref, rw_ref, rb_ref,
     x1_ref, h2_ref, route_ref) = refs[n_parts:]
    for sub in range(x_ref.shape[0] // POST_SUBTILE):
        rows = pl.ds(sub * POST_SUBTILE, POST_SUBTILE)
        pieces = []
        for p in parts:
            pieces += [p[c, rows, :] for c in range(p.shape[0])] if len(p.shape) == 3 else [p[rows, :]]
        mixed = jnp.concatenate(pieces, axis=-1) if len(pieces) > 1 else pieces[0]
        mix = jnp.dot(mixed, w_ref[...], preferred_element_type=F32)
        x1 = x_ref[rows, :] + gate_ref[0] * mix
        x1_ref[rows, :] = x1
        h2 = _norm_modulate(x1, gain_ref[...], shift_ref[0], scale_ref[0])
        h_hi = h2.astype(BF16)
        h2_ref[rows, :] = h_hi
        logits = jnp.dot(h_hi, rw_ref[...], preferred_element_type=F32)
        s = _sigmoid(logits[:, :LANES].T[:N_EXPERTS, :])
        _route(s, s + rb_ref[...], route_ref.at[:, rows])


def _post_mixer(parts, w, x2d, gate1, shift2, scale2, gain, rw_cat, rbias, tokens_per_batch, tm):
    t, d = x2d.shape
    tpb = tokens_per_batch // tm
    full = lambda i: (0, 0)
    per_b = lambda i: (i // tpb, 0, 0)

    def part_spec(p):
        if p.ndim == 3:
            return pl.BlockSpec((p.shape[0], tm, p.shape[2]), lambda i: (0, i, 0))
        return pl.BlockSpec((tm, p.shape[1]), lambda i: (i, 0))

    return pl.pallas_call(
        functools.partial(_post_kernel, n_parts=len(parts)),
        out_shape=[jax.ShapeDtypeStruct((t, d), F32), jax.ShapeDtypeStruct((t, d), BF16),
                   jax.ShapeDtypeStruct((SUBLANES, t), F32)],
        grid=(t // tm,),
        in_specs=[part_spec(p) for p in parts] + [
                  pl.BlockSpec(w.shape, full),
                  pl.BlockSpec((tm, d), lambda i: (i, 0)),
                  pl.BlockSpec((1, 1, d), per_b),
                  pl.BlockSpec((1, 1, d), per_b),
                  pl.BlockSpec((1, 1, d), per_b),
                  pl.BlockSpec((1, d), full),
                  pl.BlockSpec(rw_cat.shape, full),
                  pl.BlockSpec(rbias.shape, full)],
        out_specs=[pl.BlockSpec((tm, d), lambda i: (i, 0)),
                   pl.BlockSpec((tm, d), lambda i: (i, 0)),
                   pl.BlockSpec((SUBLANES, tm), lambda i: (0, i))],
        compiler_params=_cparams(("arbitrary",)),
        name="post_mixer",
    )(*parts, w, x2d, gate1, shift2, scale2, gain, rw_cat, rbias)


def _moe_layout(t):
    nt = t // MOE_TILE
    grid = -(-(t + N_EXPERT_GROUPS * (ROW_ALIGN - 1) * nt) // MOE_TILE) + N_EXPERT_GROUPS
    return nt, grid


def _moe_tables(gid, t):
    nt, grid = _moe_layout(t)
    ng = N_EXPERT_GROUPS
    per_tile = MOE_TILE // ROW_ALIGN
    onehot = (gid.reshape(nt, MOE_TILE, 1) == jnp.arange(ng, dtype=jnp.int32)).astype(jnp.int32)
    cnt = onehot.sum(axis=1)
    seg = (cnt + ROW_ALIGN - 1) // ROW_ALIGN
    src = jnp.cumsum(seg, axis=1) - seg
    fill = seg.sum(axis=0)
    ntile = (fill + per_tile - 1) // per_tile
    cum = jnp.cumsum(ntile)
    base = (cum - ntile) * per_tile
    dst = jnp.cumsum(seg, axis=0) - seg + base[None, :]
    seg_tab = jnp.concatenate([seg, src, dst], axis=1).reshape(-1).astype(jnp.int32)
    tail = (-fill) % per_tile
    tail_tab = jnp.concatenate([tail, fill + base, cum[-1:]]).astype(jnp.int32)
    i = jnp.arange(grid, dtype=jnp.int32)
    valid = i < cum[-1]
    ie = jnp.minimum(i, cum[-1] - 1)
    g_of = jnp.sum((ie[:, None] >= cum[None, :]).astype(jnp.int32), axis=1)
    return seg_tab, tail_tab, g_of.astype(jnp.int32), valid.astype(jnp.int32)


def _segment_copies(tab_ref, tile, enable, make_copy):
    ng = N_EXPERT_GROUPS
    base = jnp.maximum(tile, 0) * (3 * ng)
    out = []
    for g in range(ng):
        n = tab_ref[base + g]
        src = tab_ref[base + ng + g]
        dst = tab_ref[base + 2 * ng + g]
        for k in range(MOE_SEG_BITS - 1, -1, -1):
            done = (n >> (k + 1)) << (k + 1)
            rows = ROW_ALIGN << k
            s0 = pl.multiple_of((src + done) * ROW_ALIGN, ROW_ALIGN)
            d0 = pl.multiple_of((dst + done) * ROW_ALIGN, ROW_ALIGN)
            out.append((enable & (((n >> k) & 1) == 1), make_copy(s0, d0, rows)))
    return out


def _start_copies(pairs):
    for cond, copies in pairs:
        @pl.when(cond)
        def _():
            for c in copies:
                c.start()


def _wait_copies(pairs):
    for cond, copies in pairs:
        @pl.when(cond)
        def _():
            for c in copies:
                c.wait()


def _split_bf16x3(x):
    hi = x.astype(BF16).astype(F32)
    r1 = x - hi
    mid = r1.astype(BF16).astype(F32)
    lo = (r1 - mid).astype(BF16).astype(F32)
    return hi, mid, lo


def _dispatch_kernel(seg_ref, tail_ref, h_ref, route_ref, tri_ref, slot_ref, hs_ref, cbuf, zbuf, sem, *, nt):
    i = pl.program_id(0)
    tm, d = h_ref.shape
    ng = N_EXPERT_GROUPS
    cur = i % 2

    def seg_copies(tile, enable, buf):
        def seg_copy(s0, d0, rows):
            return (pltpu.make_async_copy(cbuf.at[buf, pl.ds(s0, rows)], hs_ref.at[pl.ds(d0, rows)], sem.at[buf]),)
        return _segment_copies(seg_ref, tile, enable, seg_copy)

    _wait_copies(seg_copies(i - 2, i >= 2, cur))

    route = route_ref[...]
    gid = route[ROUTE_GID_ROW:ROUTE_GID_ROW + 1, :]
    grp = lax.broadcasted_iota(jnp.int32, (SUBLANES, tm), 0).astype(F32)
    onehot = jnp.where(grp == gid, 1.0, 0.0)
    rank = jnp.dot(onehot.astype(BF16), tri_ref[...], preferred_element_type=F32)
    slot = jnp.zeros((1, tm), F32)
    for g in range(ng):
        start = (seg_ref[i * 3 * ng + ng + g] * ROW_ALIGN).astype(F32)
        slot = slot + onehot[g:g + 1, :] * (rank[g:g + 1, :] - 1.0 + start)
    slot_ref[...] = jnp.broadcast_to(slot, (SUBLANES, tm))
    perm = jnp.where(lax.broadcasted_iota(jnp.int32, (MOE_CROWS, tm), 0).astype(F32) == slot, 1.0, 0.0)
    perm = perm.astype(BF16)
    cbuf[cur, :, :d] = jnp.dot(perm, h_ref[...], preferred_element_type=F32).astype(cbuf.dtype)
    parts = jnp.concatenate(list(_split_bf16x3(route)) + [jnp.zeros((LANES - 3 * SUBLANES, tm), F32)], axis=0)
    record = lax.dot_general(perm, parts.astype(BF16), (((1,), (1,)), ((), ())), preferred_element_type=F32)
    cbuf[cur, :, d:] = record.astype(cbuf.dtype)
    _start_copies(seg_copies(i, i >= 0, cur))

    @pl.when(i == pl.num_programs(0) - 1)
    def _():
        _wait_copies(seg_copies(i - 1, i >= 1, 1 - cur))
        _wait_copies(seg_copies(i, i >= 0, cur))
        zbuf[...] = jnp.zeros(zbuf.shape, zbuf.dtype)

        def zero_copy(d0, rows):
            return (pltpu.make_async_copy(zbuf.at[pl.ds(0, rows)], hs_ref.at[pl.ds(d0, rows)], sem.at[0]),)
        pairs = []
        for g in range(ng):
            n = tail_ref[g]
            dst = tail_ref[ng + g]
            for k in range(MOE_TAIL_BITS - 1, -1, -1):
                done = (n >> (k + 1)) << (k + 1)
                d0 = pl.multiple_of((dst + done) * ROW_ALIGN, ROW_ALIGN)
                pairs.append((((n >> k) & 1) == 1, zero_copy(d0, ROW_ALIGN << k)))
        used = tail_ref[2 * ng]
        total = hs_ref.shape[0] // MOE_TILE
        for j in range(total - nt):
            d0 = pl.multiple_of(jnp.minimum(used + j, total - 1) * MOE_TILE, MOE_TILE)
            pairs.append((used + j < total, zero_copy(d0, MOE_TILE)))
        _start_copies(pairs)
        _wait_copies(pairs)


def _dispatch(seg_tab, tail_tab, h2, route, tri):
    t, d = h2.shape
    nt, grid = _moe_layout(t)
    rows = grid * MOE_TILE
    grid_spec = pltpu.PrefetchScalarGridSpec(
        num_scalar_prefetch=2,
        grid=(nt,),
        in_specs=[pl.BlockSpec((MOE_TILE, d), lambda i, *_: (i, 0)),
                  pl.BlockSpec((SUBLANES, MOE_TILE), lambda i, *_: (0, i)),
                  pl.BlockSpec((MOE_TILE, MOE_TILE), lambda i, *_: (0, 0))],
        out_specs=[pl.BlockSpec((SUBLANES, MOE_TILE), lambda i, *_: (0, i)),
                   pl.BlockSpec(memory_space=pl.ANY)],
        scratch_shapes=[pltpu.VMEM((2, MOE_CROWS, d + LANES), BF16), pltpu.VMEM((MOE_TILE, d + LANES), BF16),
                        pltpu.SemaphoreType.DMA((2,))])
    return pl.pallas_call(
        functools.partial(_dispatch_kernel, nt=nt),
        out_shape=[jax.ShapeDtypeStruct((SUBLANES, t), F32),
                   jax.ShapeDtypeStruct((rows, d + LANES), BF16)],
        grid_spec=grid_spec,
        compiler_params=_cparams(("arbitrary",)),
        name="moe_dispatch",
    )(seg_tab, tail_tab, h2, route, tri)


def _ffn_kernel(grp_ref, valid_ref, h_ref, wg32_ref, wu32_ref, wd32_ref, y_ref, wg_ref, wu_ref, wd_ref):
    i = pl.program_id(0)
    d = y_ref.shape[1]

    @pl.when((i == 0) | (grp_ref[i] != grp_ref[jnp.maximum(i - 1, 0)]))
    def _():
        for j in range(EXPERTS_PER_GROUP):
            cols = slice(j * D_FF_EXPERT, (j + 1) * D_FF_EXPERT)
            wg_ref[:, cols] = wg32_ref[j].astype(BF16)
            wu_ref[:, cols] = wu32_ref[j].astype(BF16)
            wd_ref[cols, :] = wd32_ref[j].astype(BF16)

    @pl.when(valid_ref[i] == 0)
    def _():
        y_ref[...] = jnp.zeros(y_ref.shape, y_ref.dtype)

    @pl.when(valid_ref[i] == 1)
    def _():
        h = h_ref[:, :d]
        gates = h_ref[:, d:].astype(F32)
        acts = []
        for j in range(EXPERTS_PER_GROUP):
            cols = slice(j * D_FF_EXPERT, (j + 1) * D_FF_EXPERT)
            a = jnp.dot(h, wg_ref[:, cols], preferred_element_type=F32)
            u = jnp.dot(h, wu_ref[:, cols], preferred_element_type=F32)
            gate = (gates[:, j:j + 1] + gates[:, SUBLANES + j:SUBLANES + j + 1]
                    + gates[:, 2 * SUBLANES + j:2 * SUBLANES + j + 1])
            acts.append(((a * _sigmoid_tanh(a)) * u * gate).astype(BF16))
        y = jnp.dot(jnp.concatenate(acts, axis=1), wd_ref[...], preferred_element_type=F32)
        y_ref[...] = y.astype(y_ref.dtype)


def _ffn(grp, valid, hs, wg, wu, wd, layer):
    rows, width = hs.shape
    d = width - LANES
    epg = EXPERTS_PER_GROUP
    once = pl.Buffered(1)
    w_idx = lambda i, grp, valid: (layer, grp[i], 0, 0)
    grid_spec = pltpu.PrefetchScalarGridSpec(
        num_scalar_prefetch=2,
        grid=(rows // MOE_TILE,),
        in_specs=[pl.BlockSpec((MOE_TILE, width), lambda i, grp, valid: (i, 0)),
                  pl.BlockSpec((None, epg, d, D_FF_EXPERT), w_idx, pipeline_mode=once),
                  pl.BlockSpec((None, epg, d, D_FF_EXPERT), w_idx, pipeline_mode=once),
                  pl.BlockSpec((None, epg, D_FF_EXPERT, d), w_idx, pipeline_mode=once)],
        out_specs=pl.BlockSpec((MOE_TILE, d), lambda i, grp, valid: (i, 0)),
        scratch_shapes=[pltpu.VMEM((d, epg * D_FF_EXPERT), BF16), pltpu.VMEM((d, epg * D_FF_EXPERT), BF16),
                        pltpu.VMEM((epg * D_FF_EXPERT, d), BF16)])
    return pl.pallas_call(
        _ffn_kernel,
        out_shape=jax.ShapeDtypeStruct((rows, d), BF16),
        grid_spec=grid_spec,
        compiler_params=_cparams(("arbitrary",)),
        name="moe_ffn",
    )(grp, valid, hs, wg, wu, wd)


def _combine_kernel(seg_ref, x1_ref, slot_ref, gate2_ref, ys_ref, *rest, fnet):
    if fnet:
        shift_ref, scale_ref, gain_ref, cs_ref, o_ref, y1_ref, y2_ref, ybuf, sem = rest
    else:
        o_ref, ybuf, sem = rest
    i = pl.program_id(0)
    nt = pl.num_programs(0)
    tm = x1_ref.shape[0]
    cur = i % 2

    def seg_copies(tile, enable, buf):
        def seg_copy(s0, d0, rows):
            return (pltpu.make_async_copy(ys_ref.at[pl.ds(d0, rows)], ybuf.at[buf, pl.ds(s0, rows)], sem.at[buf]),)
        return _segment_copies(seg_ref, tile, enable, seg_copy)

    @pl.when(i == 0)
    def _():
        ybuf[...] = jnp.zeros(ybuf.shape, ybuf.dtype)
        _start_copies(seg_copies(i, i == 0, cur))

    nxt = jnp.minimum(i + 1, nt - 1)
    _start_copies(seg_copies(nxt, i + 1 < nt, 1 - cur))
    _wait_copies(seg_copies(i, i >= 0, cur))
    slot = slot_ref[0:1, :]
    perm = jnp.where(lax.broadcasted_iota(jnp.int32, (MOE_CROWS, tm), 0).astype(F32) == slot, 1.0, 0.0)
    y = lax.dot_general(perm.astype(BF16), ybuf[cur], (((0,), (0,)), ((), ())), preferred_element_type=F32)
    x = x1_ref[...] + gate2_ref[0] * y
    o_ref[...] = x
    if fnet:
        _fnet_channel_tile(x, shift_ref, scale_ref, gain_ref, cs_ref, y1_ref, y2_ref)


def _combine(seg_tab, x1, slot, gate2, ys, tokens_per_batch, fnet=None):
    t, d = x1.shape
    tpb = tokens_per_batch // MOE_TILE
    per_b = lambda i, *_: (i // tpb, 0, 0)
    full = lambda i, *_: (0, 0)
    in_specs = [pl.BlockSpec((MOE_TILE, d), lambda i, *_: (i, 0)),
                pl.BlockSpec((SUBLANES, MOE_TILE), lambda i, *_: (0, i)),
                pl.BlockSpec((1, 1, d), per_b),
                pl.BlockSpec(memory_space=pl.ANY)]
    out_shape = [jax.ShapeDtypeStruct((t, d), F32)]
    out_specs = [pl.BlockSpec((MOE_TILE, d), lambda i, *_: (i, 0))]
    args = [seg_tab, x1, slot, gate2, ys]
    if fnet is not None:
        shift, scale, gain, cs = fnet
        gw = d // FNET_GROUPS
        in_specs += [pl.BlockSpec((1, 1, d), per_b), pl.BlockSpec((1, 1, d), per_b), pl.BlockSpec((1, d), full),
                     pl.BlockSpec(cs.shape, full)]
        out_shape += [jax.ShapeDtypeStruct((FNET_GROUPS, t, gw), BF16)] * 2
        out_specs += [pl.BlockSpec((FNET_GROUPS, MOE_TILE, gw), lambda i, *_: (0, i, 0))] * 2
        args += [shift, scale, gain, cs]
    grid_spec = pltpu.PrefetchScalarGridSpec(
        num_scalar_prefetch=1,
        grid=(t // MOE_TILE,),
        in_specs=in_specs,
        out_specs=out_specs,
        scratch_shapes=[pltpu.VMEM((2, MOE_CROWS, d), BF16), pltpu.SemaphoreType.DMA((2,))])
    out = pl.pallas_call(
        functools.partial(_combine_kernel, fnet=fnet is not None),
        out_shape=out_shape,
        grid_spec=grid_spec,
        compiler_params=_cparams(("arbitrary",)),
        name="moe_combine_fnet" if fnet is not None else "moe_combine",
    )(*args)
    return out if fnet is not None else out[0]


def _grouped_moe(h2, route, x1, gate2, wg, wu, wd, layer, tri):
    t = h2.shape[0]
    gid = route[ROUTE_GID_ROW].astype(jnp.int32)
    seg_tab, tail_tab, grp, valid = _moe_tables(gid, t)
    slot, hs = _dispatch(seg_tab, tail_tab, h2, route, tri)
    ys = _ffn(grp, valid, hs, wg, wu, wd, layer)
    return seg_tab, x1, slot, gate2, ys


FFT_J = SUBLANES
FFT_PAIR_ROWS = 2 * FFT_J * (GRID_W // FFT_J)


def _pair_grid_rows(y):
    rows = y.shape[0]
    assert rows % (2 * GRID_W) == 0
    groups = []
    for u in range(rows // (2 * GRID_W)):
        for sb in range(GRID_W // FFT_J):
            for e in range(2):
                start = GRID_W * (2 * u + e) + FFT_J * sb
                groups.append(y[start:start + FFT_J])
    return jnp.concatenate(groups, axis=0)


def _fnet_channel_tile(x, shift_ref, scale_ref, gain_ref, cs_ref, y1_ref, y2_ref):
    h = _norm_modulate(x, gain_ref[...], shift_ref[0], scale_ref[0]).astype(BF16)
    gw = D_MODEL // FNET_GROUPS
    for g in range(FNET_GROUPS):
        y = _pair_grid_rows(jnp.dot(h[:, g * gw:(g + 1) * gw], cs_ref[...], preferred_element_type=F32))
        y1_ref[g] = y[:, :gw].astype(y1_ref.dtype)
        y2_ref[g] = y[:, gw:].astype(y2_ref.dtype)


def _half_rows(m):
    hc = m // 2 + 1
    return hc, -(-FFT_J * hc // ROW_ALIGN) * ROW_ALIGN


def _mirror(lo, hi, hc, m):
    return jnp.concatenate([lo[:FFT_J * hc]] + [hi[FFT_J * c:FFT_J * (c + 1)] for c in range(m - hc, 0, -1)], axis=0)


def _fnet_pos_kernel(y1_ref, y2_ref, lr_ref, ls_ref, cs_ref, sn_ref, o_ref, z1, a_re, a_im, *, n):
    r1 = n // GRID_W
    gw = y1_ref.shape[2]
    hc_r, part_r = _half_rows(r1)
    hc_s, part_s = _half_rows(GRID_W)

    def stage_r(sb, carry):
        s0 = pl.multiple_of(sb * FFT_J, FFT_J)
        p0 = pl.multiple_of(sb * 2 * FFT_J, 2 * FFT_J)
        rhs1 = jnp.concatenate([y1_ref[0, pl.ds(FFT_PAIR_ROWS * u + p0, 2 * FFT_J), :] for u in range(r1 // 2)], axis=0)
        rhs2 = jnp.concatenate([y2_ref[0, pl.ds(FFT_PAIR_ROWS * u + p0, 2 * FFT_J), :] for u in range(r1 // 2)], axis=0)
        p = jnp.dot(lr_ref[...], rhs1, preferred_element_type=F32)
        q = jnp.dot(lr_ref[...], rhs2, preferred_element_type=F32)
        pc, ps = p[:part_r], p[part_r:]
        qc, qs = q[:part_r], q[part_r:]
        re = _mirror(pc - qs, pc + qs, hc_r, r1)
        nim = _mirror(qc + ps, qc - ps, hc_r, r1)
        cs = jnp.concatenate([cs_ref[sb]] * (gw // LANES), axis=1)
        sn = jnp.concatenate([sn_ref[sb]] * (gw // LANES), axis=1)
        tre = re * cs - nim * sn
        tnim = re * sn + nim * cs
        for c in range(r1):
            a_re[pl.ds(GRID_W * c + s0, FFT_J), :] = tre[FFT_J * c:FFT_J * (c + 1)]
            a_im[pl.ds(GRID_W * c + s0, FFT_J), :] = tnim[FFT_J * c:FFT_J * (c + 1)]
        return carry
    lax.fori_loop(0, GRID_W // FFT_J, stage_r, 0)

    cblk = FFT_J * GRID_W

    def stage_s(cb, carry):
        c0 = pl.multiple_of(cb * cblk, cblk)
        u = jnp.dot(ls_ref[0], a_re[pl.ds(c0, cblk), :].astype(BF16), preferred_element_type=F32)
        v = jnp.dot(ls_ref[1], a_im[pl.ds(c0, cblk), :].astype(BF16), preferred_element_type=F32)
        lo = u - v
        hi = u + v
        k0 = pl.multiple_of(cb * FFT_J, FFT_J)
        for d in range(hc_s):
            z1[pl.ds(r1 * d + k0, FFT_J), :] = lo[FFT_J * d:FFT_J * (d + 1)]
        for d in range(1, GRID_W - hc_s + 1):
            z1[pl.ds(r1 * (GRID_W - d) + k0, FFT_J), :] = hi[FFT_J * d:FFT_J * (d + 1)]
        return carry
    lax.fori_loop(0, r1 // FFT_J, stage_s, 0)
    o_ref[0] = z1[...].astype(o_ref.dtype)


def _fnet_position(y1, y2, lr, ls, tw_cos, tw_sin, batch, n):
    groups, t, gw = y1.shape
    full2 = lambda b, g: (0, 0)
    full3 = lambda b, g: (0, 0, 0)
    scratch = pltpu.VMEM((n, gw), F32)
    return pl.pallas_call(
        functools.partial(_fnet_pos_kernel, n=n),
        out_shape=jax.ShapeDtypeStruct((groups, t, gw), BF16),
        grid=(batch, groups),
        in_specs=[pl.BlockSpec((1, n, gw), lambda b, g: (g, b, 0)),
                  pl.BlockSpec((1, n, gw), lambda b, g: (g, b, 0)),
                  pl.BlockSpec(lr.shape, full2),
                  pl.BlockSpec(ls.shape, full3),
                  pl.BlockSpec(tw_cos.shape, full3),
                  pl.BlockSpec(tw_sin.shape, full3)],
        out_specs=pl.BlockSpec((1, n, gw), lambda b, g: (g, b, 0)),
        scratch_shapes=[scratch, scratch, scratch],
        compiler_params=_cparams(("arbitrary", "arbitrary")),
        name="fnet_position",
    )(y1, y2, lr, ls, tw_cos, tw_sin)


def _dft_tables(n):
    assert n % (GRID_W * FFT_J) == 0
    gw = D_MODEL // FNET_GROUPS
    j = np.arange(gw)
    ang = 2.0 * np.pi * ((j[:, None] * j[None, :]) % gw) / gw
    cs = np.concatenate([np.cos(ang), np.sin(ang)], axis=1) / np.sqrt(gw)
    r1 = n // GRID_W
    assert r1 % 2 == 0
    eye = np.eye(FFT_J)
    scale = float(n) ** -0.25
    a = np.arange(r1)
    hc_r, part_r = _half_rows(r1)
    ang_r = 2.0 * np.pi * ((a[:hc_r, None] * a[None, :]) % r1) / r1
    lr = np.zeros((2 * part_r, FFT_J * r1))
    lr[:FFT_J * hc_r] = np.kron(np.cos(ang_r), eye) * scale
    lr[part_r:part_r + FFT_J * hc_r] = np.kron(np.sin(ang_r), eye) * scale
    s = np.arange(GRID_W)
    hc_s, part_s = _half_rows(GRID_W)
    ang_s = 2.0 * np.pi * ((s[:hc_s, None] * s[None, :]) % GRID_W) / GRID_W
    ls = np.zeros((2, part_s, FFT_J * GRID_W))
    ls[0, :FFT_J * hc_s] = np.einsum("ds,cC->dcCs", np.cos(ang_s), eye).reshape(FFT_J * hc_s, FFT_J * GRID_W) * scale
    ls[1, :FFT_J * hc_s] = np.einsum("ds,cC->dcCs", np.sin(ang_s), eye).reshape(FFT_J * hc_s, FFT_J * GRID_W) * scale
    sb = np.arange(GRID_W // FFT_J)
    s_of = sb[:, None, None] * FFT_J + np.arange(FFT_J)[None, None, :]
    ang_t = 2.0 * np.pi * ((s_of * a[None, :, None]) % n) / n
    ang_t = ang_t.reshape(len(sb), r1 * FFT_J, 1)
    tw_cos = jnp.broadcast_to(jnp.asarray(np.cos(ang_t), F32), (len(sb), r1 * FFT_J, LANES))
    tw_sin = jnp.broadcast_to(jnp.asarray(np.sin(ang_t), F32), (len(sb), r1 * FFT_J, LANES))
    return jnp.asarray(cs, BF16), jnp.asarray(lr, BF16), jnp.asarray(ls, BF16), tw_cos, tw_sin


def kernel(x, c, ctx, c_ctx, ada_w, ada_b, norm_mix, norm_ffn, mix_w_in, mix_w_out, na_q_norm, na_k_norm, na_rpb,
           lru_conv_w, lru_conv_b, lru_gate_r_w, lru_gate_r_b, lru_gate_i_w, lru_gate_i_b, lru_lambda,
           fnet_w_out, router_w, router_bias, moe_w_gate, moe_w_up, moe_w_down):
    batch, n, d = x.shape
    ctx_len = ctx.shape[1]
    depth = ada_w.shape[0]
    rows = n // GRID_W
    assert d == D_MODEL and n % (GRID_W * NA_QROWS) == 0 and rows >= 4 * NA_QROWS
    assert n % MOE_TILE == 0 and n % INPROJ_TILE == 0 and n % POST_TILE == 0 and ctx_len % LANES == 0
    t = batch * n
    tri = jnp.asarray(np.triu(np.ones((MOE_TILE, MOE_TILE))), BF16)

    r_pad = -(-(batch + 1) // SUBLANES) * SUBLANES
    c_rows = jnp.concatenate([c, c_ctx[None, :], jnp.zeros((r_pad - batch - 1, d), c.dtype)], axis=0)
    mod = _modulation(c_rows, ada_w, ada_b)

    def mod_slices(layer):
        m = mod[layer, :batch].reshape(batch, 1, 6, d)
        return [m[:, :, i, :] for i in range(6)]

    rw_cat = jnp.pad(router_w.astype(F32), ((0, 0), (0, 2 * LANES - N_EXPERTS))).astype(BF16)
    rbias = router_bias.reshape(N_EXPERTS, 1).astype(F32)
    x2d = x.reshape(t, d)
    ctx2d = ctx.reshape(batch * ctx_len, d)

    pending = None
    for layer in range(depth):
        li = layer // 2
        shift1, scale1, gate1, shift2, scale2, gate2 = mod_slices(layer)
        gain_mix = norm_mix[layer].reshape(1, d)
        gain_ffn = norm_ffn[layer].reshape(1, d)
        if layer % 2 == 0:
            if pending is not None:
                x2d = _combine(*pending, n)
            w_in = mix_w_in[li].astype(BF16)
            ind = jnp.asarray(np.kron(np.eye(NA_HEADS // 2), np.ones((HEAD_DIM, HEAD_DIM))), BF16)
            qg = (jnp.tile(na_q_norm[li], NA_HEADS) * (HEAD_DIM ** -0.5 * LOG2_E)).reshape(1, NA_WIDTH).astype(F32)
            kg = jnp.tile(na_k_norm[li], NA_HEADS).reshape(1, NA_WIDTH).astype(F32)
            q, k, v, xb, gb = _inproj(x2d, shift1, scale1, gain_mix, w_in, ind, qg, kg,
                                      ("q", "k", "v", "x", "g"), n, INPROJ_TILE)
            mctx = mod[layer, batch, :2 * d]
            shift_c = jnp.broadcast_to(mctx[:d], (batch, 1, d))
            scale_c = jnp.broadcast_to(mctx[d:], (batch, 1, d))
            k_c, v_c, xb_c = _inproj(ctx2d, shift_c, scale_c, gain_mix, w_in[:, NA_WIDTH:4 * NA_WIDTH], ind, qg, kg,
                                     ("k", "v", "x"), ctx_len, ctx_len)
            bias = _na_bias_tables(na_rpb[li], rows)
            attn = _attention(q, k, v, k_c, v_c, bias, batch, n, ctx_len)
            wcat, gbias = _lru_gate_weights(lru_gate_r_w[li], lru_gate_r_b[li], lru_gate_i_w[li], lru_gate_i_b[li])
            lru = _lru(xb, gb, xb_c, lru_conv_w[li].astype(F32), lru_conv_b[li].reshape(1, LRU_WIDTH).astype(F32),
                       wcat, gbias, lru_lambda[li].astype(F32), batch, n, ctx_len)
            parts, w_out = [attn, lru], mix_w_out[li].astype(BF16)
        else:
            cs, lr, ls, tw_cos, tw_sin = _dft_tables(n)
            x2d, y1, y2 = _combine(*pending, n, fnet=(shift1, scale1, gain_mix, cs))
            parts, w_out = [_fnet_position(y1, y2, lr, ls, tw_cos, tw_sin, batch, n)], fnet_w_out[li].astype(BF16)
        x1, h2, route = _post_mixer(parts, w_out, x2d, gate1, shift2, scale2, gain_ffn, rw_cat, rbias, n,
                                    POST_TILE)
        pending = _grouped_moe(h2, route, x1, gate2, moe_w_gate.astype(F32), moe_w_up.astype(F32),
                               moe_w_down.astype(F32), layer, tri)
    return _combine(*pending, n).reshape(batch, n, d)
```

```python
import functools

import numpy as np
import jax
import jax.numpy as jnp
from jax import lax
from jax.experimental import pallas as pl
from jax.experimental.pallas import tpu as pltpu

F32 = jnp.float32
BF16 = jnp.bfloat16
HIGHEST = lax.Precision.HIGHEST

D_MODEL = 1024
GRID_W = 64
HEAD_DIM = 64
NA_HEADS = 8
NA_WIDTH = NA_HEADS * HEAD_DIM
NA_WIN_ROWS = 8
NA_WIN_COLS = 16
LRU_WIDTH = 512
LRU_BLOCK = 64
LRU_C = 8.0
FNET_GROUPS = 4
N_EXPERTS = 16
EXPERTS_PER_GROUP = 4
N_EXPERT_GROUPS = 4
D_FF_EXPERT = 512
RMS_EPS = 1e-6
MASK_VALUE = -1e30
LOG2_E = 1.4426950408889634

V7X_VMEM_LIMIT_BYTES = 56 * 1024 * 1024
LANES = 128
SUBLANES = 8

NA_QROWS = 4
NA_KROWS = NA_QROWS + NA_WIN_ROWS - 1
NA_QBLK = NA_QROWS * GRID_W
NA_KBLK = NA_KROWS * GRID_W
NA_STEP_BLOCKS = 2

LRU_CHUNK = LANES
LRU_TROWS = 512

ROUTE_GID_ROW = EXPERTS_PER_GROUP
MOE_TILE = 512
ROW_ALIGN = 16
MOE_CROWS = MOE_TILE + N_EXPERT_GROUPS * ROW_ALIGN
MOE_SEG_BITS = (MOE_TILE // ROW_ALIGN).bit_length()
MOE_TAIL_BITS = (MOE_TILE // ROW_ALIGN - 1).bit_length()


def _sigmoid(x):
    return 1.0 / (1.0 + jnp.exp(-x))


def _sigmoid_tanh(x):
    return 0.5 + 0.5 * jnp.tanh(0.5 * x)


def _cparams(sem, vmem=V7X_VMEM_LIMIT_BYTES):
    return pltpu.CompilerParams(dimension_semantics=sem, vmem_limit_bytes=vmem)


def _mod_kernel(c_ref, w_ref, b_ref, o_ref):
    c = c_ref[...]
    s = c * _sigmoid(c)
    o_ref[0] = jnp.dot(s.astype(BF16), w_ref[0].astype(BF16), preferred_element_type=F32) + b_ref[0]


def _modulation(c_rows, ada_w, ada_b):
    depth, d, n6 = ada_w.shape
    r = c_rows.shape[0]
    tn = n6 // 2
    return pl.pallas_call(
        _mod_kernel,
        out_shape=jax.ShapeDtypeStruct((depth, r, n6), F32),
        grid=(depth, n6 // tn),
        in_specs=[pl.BlockSpec((r, d), lambda l, j: (0, 0)),
                  pl.BlockSpec((1, d, tn), lambda l, j: (l, 0, j)),
                  pl.BlockSpec((1, 1, tn), lambda l, j: (l, 0, j))],
        out_specs=pl.BlockSpec((1, r, tn), lambda l, j: (l, 0, j)),
        compiler_params=_cparams(("arbitrary", "arbitrary")),
        name="adaln_mod",
    )(c_rows, ada_w, ada_b.reshape(depth, 1, n6))


def _norm_modulate(x, gain, shift, scale):
    ms = jnp.mean(x * x, axis=-1, keepdims=True)
    y = x * lax.rsqrt(ms + RMS_EPS) * gain
    return y * (1.0 + scale) + shift


def _inproj_kernel(x_ref, shift_ref, scale_ref, gain_ref, w_ref, ind_ref, qg_ref, kg_ref, *out_refs, segs):
    h = _norm_modulate(x_ref[...], gain_ref[...], shift_ref[0], scale_ref[0]).astype(BF16)
    for s, (kind, o_ref) in enumerate(zip(segs, out_refs)):
        z = jnp.dot(h, w_ref[:, s * NA_WIDTH:(s + 1) * NA_WIDTH], preferred_element_type=F32)
        if kind in ("q", "k"):
            zz = (z * z).astype(BF16)
            hw = ind_ref.shape[0]
            ms = jnp.concatenate([jnp.dot(zz[:, i * hw:(i + 1) * hw], ind_ref[...], preferred_element_type=F32)
                                  for i in range(NA_WIDTH // hw)], axis=1) * (1.0 / HEAD_DIM)
            g = qg_ref[...] if kind == "q" else kg_ref[...]
            z = z * lax.rsqrt(ms + RMS_EPS) * g
        if kind in ("x", "g"):
            for c in range(LRU_WIDTH // LRU_CHUNK):
                o_ref[c] = z[:, c * LRU_CHUNK:(c + 1) * LRU_CHUNK].astype(o_ref.dtype)
        else:
            o_ref[...] = z.astype(o_ref.dtype)


def _inproj(x2d, shift, scale, gain, w, ind, qg, kg, segs, tokens_per_batch, tm):
    t, d = x2d.shape
    tpb = tokens_per_batch // tm
    dt = {"q": BF16, "k": BF16, "v": BF16, "x": F32, "g": F32}
    full = lambda i: (0, 0)
    nch = LRU_WIDTH // LRU_CHUNK

    def out_shape(kind):
        shape = (nch, t, LRU_CHUNK) if kind in ("x", "g") else (t, NA_WIDTH)
        return jax.ShapeDtypeStruct(shape, dt[kind])

    def out_spec(kind):
        if kind in ("x", "g"):
            return pl.BlockSpec((nch, tm, LRU_CHUNK), lambda i: (0, i, 0))
        return pl.BlockSpec((tm, NA_WIDTH), lambda i: (i, 0))

    return pl.pallas_call(
        functools.partial(_inproj_kernel, segs=segs),
        out_shape=[out_shape(k) for k in segs],
        grid=(t // tm,),
        in_specs=[pl.BlockSpec((tm, d), lambda i: (i, 0)),
                  pl.BlockSpec((1, 1, d), lambda i: (i // tpb, 0, 0)),
                  pl.BlockSpec((1, 1, d), lambda i: (i // tpb, 0, 0)),
                  pl.BlockSpec((1, d), full),
                  pl.BlockSpec(w.shape, full),
                  pl.BlockSpec(ind.shape, full),
                  pl.BlockSpec((1, NA_WIDTH), full),
                  pl.BlockSpec((1, NA_WIDTH), full)],
        out_specs=[out_spec(k) for k in segs],
        compiler_params=_cparams(("arbitrary",)),
        name="inproj_" + "".join(segs),
    )(x2d, shift, scale, gain, w, ind, qg, kg)


def _na_bias_tables(rpb, rows):
    kr = NA_WIN_ROWS
    rb_count = rows // NA_QROWS
    cq = np.arange(GRID_W)
    ck = np.arange(GRID_W)
    col_start = np.clip(cq - NA_WIN_COLS // 2, 0, GRID_W - NA_WIN_COLS)
    valid_c = (ck[None, :] >= col_start[:, None]) & (ck[None, :] < col_start[:, None] + NA_WIN_COLS)
    dc = np.clip(ck[None, :] - cq[:, None], 1 - NA_WIN_COLS, NA_WIN_COLS - 1) + (NA_WIN_COLS - 1)
    n_dr, n_dc = 2 * NA_WIN_ROWS - 1, 2 * NA_WIN_COLS - 1
    sel_c = (dc[:, :, None] == np.arange(n_dc)) & valid_c[:, :, None]
    blocks = jnp.einsum("hrc,qkc->hrqk", rpb.astype(F32), jnp.asarray(sel_c, F32), precision=HIGHEST)
    blocks = blocks + jnp.asarray(np.where(valid_c, 0.0, MASK_VALUE), F32)
    blocks = jnp.concatenate([blocks, jnp.full((NA_HEADS, 1, GRID_W, GRID_W), MASK_VALUE, F32)], axis=1)
    blocks = blocks * LOG2_E
    which = []
    for rb in (0, 1, rb_count - 1):
        r = rb * NA_QROWS + np.arange(NA_QROWS)
        ks = int(np.clip(rb * NA_QROWS - kr // 2, 0, rows - NA_KROWS))
        key_r = ks + np.arange(NA_KROWS)
        row_start = np.clip(r - kr // 2, 0, rows - kr)
        valid_r = (key_r[None, :] >= row_start[:, None]) & (key_r[None, :] < row_start[:, None] + kr)
        dr = np.clip(key_r[None, :] - r[:, None] + (NA_WIN_ROWS - 1), 0, n_dr - 1)
        which.append(np.where(valid_r, dr, n_dr))
    return _na_bias_assemble(blocks, which)


def _na_bias_kernel(blk_ref, o_ref, *, which):
    for t, table in enumerate(which):
        @pl.when(pl.program_id(0) == t)
        def _():
            for i in range(NA_QROWS):
                row = jnp.concatenate([blk_ref[0, int(table[i, j])] for j in range(NA_KROWS)], axis=1)
                o_ref[0, 0, i * GRID_W:(i + 1) * GRID_W, :] = row


def _na_bias_assemble(blocks, which):
    heads, nblk = blocks.shape[:2]
    return pl.pallas_call(
        functools.partial(_na_bias_kernel, which=which),
        out_shape=jax.ShapeDtypeStruct((len(which), heads, NA_QBLK, NA_KBLK), F32),
        grid=(len(which), heads),
        in_specs=[pl.BlockSpec((1, nblk, GRID_W, GRID_W), lambda t, h: (h, 0, 0, 0))],
        out_specs=pl.BlockSpec((1, 1, NA_QBLK, NA_KBLK), lambda t, h: (t, h, 0, 0)),
        compiler_params=_cparams(("arbitrary", "arbitrary")),
        name="na_bias",
    )(blocks)


def _attn_kernel(q_ref, k_ref, v_ref, kc_ref, vc_ref, bias_ref, o_ref, *, rows):
    last = rows // NA_QROWS - 1
    nt = (((1,), (1,)), ((), ()))
    ctx_len = kc_ref.shape[0]
    low_half = lax.broadcasted_iota(jnp.int32, (NA_QBLK, LANES), 1) < HEAD_DIM
    for blk in range(NA_STEP_BLOCKS):
        rb = pl.program_id(1) * NA_STEP_BLOCKS + blk
        ks = jnp.clip(rb * NA_QROWS - NA_WIN_ROWS // 2, 0, rows - NA_KROWS)
        kstart = pl.multiple_of(ks * GRID_W, GRID_W)
        geom = jnp.where(rb == 0, 0, jnp.where(rb == last, 2, 1))
        qrows = slice(blk * NA_QBLK, (blk + 1) * NA_QBLK)
        for pair in range(NA_HEADS * HEAD_DIM // LANES):
            ls = slice(pair * LANES, (pair + 1) * LANES)
            q2 = q_ref[qrows, ls]
            k_all = jnp.concatenate([kc_ref[:, ls], k_ref[pl.ds(kstart, NA_KBLK), ls]], axis=0)
            v_all = jnp.concatenate([vc_ref[:, ls], v_ref[pl.ds(kstart, NA_KBLK), ls]], axis=0)
            outs = []
            for half in range(2):
                qh = jnp.where(low_half == (half == 0), q2, jnp.zeros_like(q2))
                s = lax.dot_general(qh, k_all, nt, preferred_element_type=F32)
                s = jnp.concatenate([s[:, :ctx_len], s[:, ctx_len:] + bias_ref[geom, 2 * pair + half]], axis=1)
                m = jnp.max(s, axis=-1, keepdims=True)
                p = jnp.exp2(s - m)
                l = jnp.sum(p, axis=-1, keepdims=True)
                outs.append(jnp.dot(p.astype(BF16), v_all, preferred_element_type=F32) / l)
            o_ref[qrows, ls] = jnp.where(low_half, outs[0], outs[1]).astype(o_ref.dtype)


def _attention(q, k, v, kc, vc, bias, batch, n, ctx_len):
    rows = n // GRID_W
    rbc = rows // (NA_QROWS * NA_STEP_BLOCKS)
    qblk = NA_QBLK * NA_STEP_BLOCKS
    return pl.pallas_call(
        functools.partial(_attn_kernel, rows=rows),
        out_shape=jax.ShapeDtypeStruct((batch * n, NA_WIDTH), BF16),
        grid=(batch, rbc),
        in_specs=[pl.BlockSpec((qblk, NA_WIDTH), lambda b, rb: (b * rbc + rb, 0)),
                  pl.BlockSpec((n, NA_WIDTH), lambda b, rb: (b, 0)),
                  pl.BlockSpec((n, NA_WIDTH), lambda b, rb: (b, 0)),
                  pl.BlockSpec((ctx_len, NA_WIDTH), lambda b, rb: (b, 0)),
                  pl.BlockSpec((ctx_len, NA_WIDTH), lambda b, rb: (b, 0)),
                  pl.BlockSpec(bias.shape, lambda b, rb: (0, 0, 0, 0), pipeline_mode=pl.Buffered(1))],
        out_specs=pl.BlockSpec((qblk, NA_WIDTH), lambda b, rb: (b * rbc + rb, 0)),
        compiler_params=_cparams(("arbitrary", "arbitrary")),
        name="na_attention",
    )(q, k, v, kc, vc, bias)


SCAN_GROUPS = 4
SCAN_CHUNKS = SCAN_GROUPS * SUBLANES


def _scan_pitch(n):
    p = -(-n // SCAN_CHUNKS)
    while p % 8 != 4:
        p += 1
    return p


NEG_LOG2_E = -LOG2_E


def _lru_coeff_tile(half_xc, zh, half_bias, k, d):
    c = LRU_CHUNK
    t_r = jnp.tanh(zh[:, (2 * d) * c:(2 * d + 1) * c] + half_bias[:, (2 * d) * c:(2 * d + 1) * c])
    t_i = jnp.tanh(zh[:, (2 * d + 1) * c:(2 * d + 2) * c] + half_bias[:, (2 * d + 1) * c:(2 * d + 2) * c])
    neg_log_a = k[d:d + 1, :] * (1.0 + t_r)
    a = jnp.exp2(neg_log_a * NEG_LOG2_E)
    one_minus_a2 = jnp.tanh(neg_log_a) * (a * a + 1.0)
    root = jnp.where(one_minus_a2 > 0.0, one_minus_a2 * lax.rsqrt(one_minus_a2), 0.0)
    return a, root * (half_xc + half_xc * t_i)


def _conv_tile(xpad, t0, w, b, rows):
    acc = b + w[0:1, :] * xpad[pl.ds(t0 + SUBLANES - 2, rows), :]
    acc = acc + w[1:2, :] * xpad[pl.ds(t0 + SUBLANES - 1, rows), :]
    acc = acc + w[2:3, :] * xpad[pl.ds(t0 + SUBLANES, rows), :]
    return acc + w[3:4, :] * xpad[pl.ds(t0 + SUBLANES + 1, rows), :]


SCAN_UNROLL = 4


def _group_rows(j, g, pitch):
    return pl.ds(g * SUBLANES * pitch + j, SUBLANES, stride=pitch)


def _chunk_totals(af_ref, bf_ref, ab_ref, bb_ref, pitch):
    def body(j, carry):
        jb = pitch - 1 - j
        out = []
        for g in range(SCAN_GROUPS):
            pf, hf, pb, hb = carry[4 * g:4 * g + 4]
            af = af_ref[_group_rows(j, g, pitch), :]
            ab = ab_ref[_group_rows(jb, g, pitch), :]
            out += [af * pf, af * hf + bf_ref[_group_rows(j, g, pitch), :],
                    ab * pb, ab * hb + bb_ref[_group_rows(jb, g, pitch), :]]
        return tuple(out)
    one = jnp.ones((SUBLANES, LRU_CHUNK), F32)
    zero = jnp.zeros((SUBLANES, LRU_CHUNK), F32)
    res = lax.fori_loop(0, pitch, body, (one, zero, one, zero) * SCAN_GROUPS, unroll=SCAN_UNROLL)
    fwd = [(res[4 * g], res[4 * g + 1]) for g in range(SCAN_GROUPS)]
    bwd = [(res[4 * g + 2], res[4 * g + 3]) for g in range(SCAN_GROUPS)]
    return fwd, bwd


def _chunk_starts(totals, h0, reverse):
    row = lax.broadcasted_iota(jnp.int32, (SUBLANES, LRU_CHUNK), 0)
    starts = [jnp.zeros((SUBLANES, LRU_CHUNK), F32) for _ in range(SCAN_GROUPS)]
    state = h0
    order = range(SCAN_CHUNKS - 1, -1, -1) if reverse else range(SCAN_CHUNKS)
    for c in order:
        g, s = divmod(c, SUBLANES)
        p_end, h_end = totals[g]
        starts[g] = jnp.where(row == s, state, starts[g])
        state = p_end[s:s + 1, :] * state + h_end[s:s + 1, :]
    return starts, state


def _scan_write(af_ref, bf_ref, hf_ref, ab_ref, bb_ref, hb_ref, starts_f, starts_b, pitch):
    def body(j, carry):
        jb = pitch - 1 - j
        out = []
        for g in range(SCAN_GROUPS):
            hf, hb = carry[2 * g:2 * g + 2]
            hf = af_ref[_group_rows(j, g, pitch), :] * hf + bf_ref[_group_rows(j, g, pitch), :]
            hb = ab_ref[_group_rows(jb, g, pitch), :] * hb + bb_ref[_group_rows(jb, g, pitch), :]
            hf_ref[_group_rows(j, g, pitch), :] = hf
            hb_ref[_group_rows(jb, g, pitch), :] = hb
            out += [hf, hb]
        return tuple(out)
    init = tuple(v for g in range(SCAN_GROUPS) for v in (starts_f[g], starts_b[g]))
    lax.fori_loop(0, pitch, body, init, unroll=SCAN_UNROLL)


def _lru_kernel(x_ref, g_ref, xc_ref, cw_ref, cb_ref, w_ref, gb_ref, lam_ref, o_ref,
                xpad, a0, b0, a1, b1, h0s, h1s, ca0, cb0, ca1, cb1, *, n, ctx_len):
    pitch = _scan_pitch(n)
    cpitch = _scan_pitch(ctx_len)
    cw = cw_ref[...]
    cb = cb_ref[...]
    gbias = gb_ref[0]
    lam = lam_ref[...]
    sp = jnp.maximum(-lam, 0.0) + jnp.log1p(jnp.exp(-jnp.abs(lam)))
    k = (0.5 * LRU_C) * sp
    wcat = w_ref[0]
    zeros8 = jnp.zeros((SUBLANES, LRU_CHUNK), F32)

    def fill_coeffs(src_rows, total, length, trows, a_refs, b_refs):
        for d in range(2):
            a_refs[d][pl.ds(length, total - length), :] = jnp.ones((total - length, LRU_CHUNK), F32)
            b_refs[d][pl.ds(length, total - length), :] = jnp.zeros((total - length, LRU_CHUNK), F32)
        xpad[pl.ds(0, SUBLANES), :] = zeros8
        xpad[pl.ds(SUBLANES + length, SUBLANES), :] = zeros8
        xpad[pl.ds(SUBLANES, length), :] = src_rows

        def tile(t, carry):
            t0 = pl.multiple_of(t * trows, SUBLANES)
            xc = _conv_tile(xpad, t0, cw, cb, trows)
            zh = jnp.dot(xc.astype(BF16), wcat, preferred_element_type=F32)
            half_xc = 0.5 * xc
            for d in range(2):
                a, b = _lru_coeff_tile(half_xc, zh, gbias, k, d)
                a_refs[d][pl.ds(t0, trows), :] = a
                b_refs[d][pl.ds(t0, trows), :] = b
            return carry
        lax.fori_loop(0, length // trows, tile, 0)

    fill_coeffs(xc_ref[0], SCAN_CHUNKS * cpitch, ctx_len, ctx_len, (ca0, ca1), (cb0, cb1))
    zero_state = jnp.zeros((1, LRU_CHUNK), F32)
    fwd, bwd = _chunk_totals(ca0, cb0, ca1, cb1, cpitch)
    _, init_f = _chunk_starts(fwd, zero_state, reverse=False)
    _, init_b = _chunk_starts(bwd, zero_state, reverse=True)

    fill_coeffs(x_ref[0], SCAN_CHUNKS * pitch, n, LRU_TROWS, (a0, a1), (b0, b1))
    fwd, bwd = _chunk_totals(a0, b0, a1, b1, pitch)
    starts_f, _ = _chunk_starts(fwd, init_f, reverse=False)
    starts_b, _ = _chunk_starts(bwd, init_b, reverse=True)
    _scan_write(a0, b0, h0s, a1, b1, h1s, starts_f, starts_b, pitch)

    def out_tile(t, carry):
        t0 = pl.multiple_of(t * LRU_TROWS, SUBLANES)
        y = h0s[pl.ds(t0, LRU_TROWS), :] + h1s[pl.ds(t0, LRU_TROWS), :]
        g = g_ref[0, pl.ds(t0, LRU_TROWS), :]
        gelu = 0.5 * g * (1.0 + jnp.tanh(0.7978845608028654 * (g + 0.044715 * (g * g * g))))
        o_ref[0, pl.ds(t0, LRU_TROWS), :] = (gelu * y).astype(o_ref.dtype)
        return carry
    lax.fori_loop(0, n // LRU_TROWS, out_tile, 0)


def _lru(xb, gb, xb_ctx, conv_w, conv_b, wcat, gbias, lam, batch, n, ctx_len):
    nch = LRU_WIDTH // LRU_CHUNK
    pitch = _scan_pitch(n)
    cpitch = _scan_pitch(ctx_len)
    big = pltpu.VMEM((SCAN_CHUNKS * pitch, LRU_CHUNK), F32)
    small = pltpu.VMEM((SCAN_CHUNKS * cpitch, LRU_CHUNK), F32)
    return pl.pallas_call(
        functools.partial(_lru_kernel, n=n, ctx_len=ctx_len),
        out_shape=jax.ShapeDtypeStruct((nch, batch * n, LRU_CHUNK), BF16),
        grid=(batch, nch),
        in_specs=[pl.BlockSpec((1, n, LRU_CHUNK), lambda b, c: (c, b, 0)),
                  pl.BlockSpec((1, n, LRU_CHUNK), lambda b, c: (c, b, 0)),
                  pl.BlockSpec((1, ctx_len, LRU_CHUNK), lambda b, c: (c, b, 0)),
                  pl.BlockSpec((4, LRU_CHUNK), lambda b, c: (0, c)),
                  pl.BlockSpec((1, LRU_CHUNK), lambda b, c: (0, c)),
                  pl.BlockSpec((1, LRU_CHUNK, 4 * LRU_CHUNK), lambda b, c: (c, 0, 0)),
                  pl.BlockSpec((1, 1, 4 * LRU_CHUNK), lambda b, c: (c, 0, 0)),
                  pl.BlockSpec((2, LRU_CHUNK), lambda b, c: (0, c))],
        out_specs=pl.BlockSpec((1, n, LRU_CHUNK), lambda b, c: (c, b, 0)),
        scratch_shapes=[pltpu.VMEM((n + 2 * SUBLANES, LRU_CHUNK), F32),
                        big, big, big, big, big, big, small, small, small, small],
        compiler_params=_cparams(("arbitrary", "arbitrary")),
        name="rglru",
    )(xb, gb, xb_ctx, conv_w, conv_b, wcat, gbias, lam)


def _lru_gate_weights(w_r, b_r, w_i, b_i):
    nch = LRU_WIDTH // LRU_CHUNK
    bpc = LRU_CHUNK // LRU_BLOCK

    def dense(w):
        wc = w.reshape(nch, bpc, LRU_BLOCK, LRU_BLOCK)
        eye = jnp.eye(bpc, dtype=w.dtype)
        return jnp.einsum("cbij,bd->cbidj", wc, eye).reshape(nch, LRU_CHUNK, LRU_CHUNK)

    wcat = jnp.concatenate([dense(w_r[0]), dense(w_i[0]), dense(w_r[1]), dense(w_i[1])], axis=-1)
    chunk = lambda v: v.reshape(nch, 1, LRU_CHUNK)
    gbias = jnp.concatenate([chunk(b_r[0]), chunk(b_i[0]), chunk(b_r[1]), chunk(b_i[1])], axis=-1)
    return (0.5 * wcat).astype(BF16), (0.5 * gbias).astype(F32)


def _route(s, sel, route_ref):
    srow = [s[e:e + 1, :] for e in range(N_EXPERTS)]
    lrow = [sel[e:e + 1, :] for e in range(N_EXPERTS)]
    gscore = []
    for g in range(N_EXPERT_GROUPS):
        a = lrow[g * EXPERTS_PER_GROUP:(g + 1) * EXPERTS_PER_GROUP]
        best = a[0] + a[1]
        for i, j in ((0, 2), (0, 3), (1, 2), (1, 3), (2, 3)):
            best = jnp.maximum(best, a[i] + a[j])
        gscore.append(best)
    bg = jnp.zeros_like(gscore[0], dtype=jnp.int32)
    bv = gscore[0]
    for g in range(1, N_EXPERT_GROUPS):
        upd = gscore[g] > bv
        bg = jnp.where(upd, g, bg)
        bv = jnp.where(upd, gscore[g], bv)

    def pick(rows_):
        out = []
        for j in range(EXPERTS_PER_GROUP):
            v = rows_[j]
            for g in range(1, N_EXPERT_GROUPS):
                v = jnp.where(bg == g, rows_[g * EXPERTS_PER_GROUP + j], v)
            out.append(v)
        return out
    cand = pick(lrow)
    cs = pick(srow)
    i1 = jnp.zeros_like(bg)
    v1 = cand[0]
    w1 = cs[0]
    for j in range(1, EXPERTS_PER_GROUP):
        upd = cand[j] > v1
        i1 = jnp.where(upd, j, i1)
        v1 = jnp.where(upd, cand[j], v1)
        w1 = jnp.where(upd, cs[j], w1)
    i2 = jnp.full_like(bg, -1)
    v2 = jnp.full_like(v1, -jnp.inf)
    w2 = jnp.zeros_like(w1)
    for j in range(EXPERTS_PER_GROUP):
        upd = (i1 != j) & (cand[j] > v2)
        i2 = jnp.where(upd, j, i2)
        v2 = jnp.where(upd, cand[j], v2)
        w2 = jnp.where(upd, cs[j], w2)
    den = w1 + w2
    g1 = w1 / den
    g2 = w2 / den
    for j in range(EXPERTS_PER_GROUP):
        route_ref[j:j + 1, :] = jnp.where(i1 == j, g1, 0.0) + jnp.where(i2 == j, g2, 0.0)
    route_ref[ROUTE_GID_ROW:ROUTE_GID_ROW + 1, :] = bg.astype(F32)
    pad = SUBLANES - ROUTE_GID_ROW - 1
    route_ref[ROUTE_GID_ROW + 1:, :] = jnp.zeros((pad, bg.shape[1]), F32)


POST_SUBTILE = 512
POST_TILE = 2 * POST_SUBTILE
INPROJ_TILE = 1024


def _post_kernel(*refs, n_parts):
    parts = refs[:n_parts]
    (w_ref, x_ref, gate_ref, shift_ref, scale_ref, gain_ref, rw_ref, rb_ref,
     x1_ref, h2_ref, route_ref) = refs[n_parts:]
    for sub in range(x_ref.shape[0] // POST_SUBTILE):
        rows = pl.ds(sub * POST_SUBTILE, POST_SUBTILE)
        pieces = []
        for p in parts:
            pieces += [p[c, rows, :] for c in range(p.shape[0])] if len(p.shape) == 3 else [p[rows, :]]
        mixed = jnp.concatenate(pieces, axis=-1) if len(pieces) > 1 else pieces[0]
        mix = jnp.dot(mixed, w_ref[...], preferred_element_type=F32)
        x1 = x_ref[rows, :] + gate_ref[0] * mix
        x1_ref[rows, :] = x1
        h2 = _norm_modulate(x1, gain_ref[...], shift_ref[0], scale_ref[0])
        h_hi = h2.astype(BF16)
        h2_ref[rows, :] = h_hi
        logits = jnp.dot(h_hi, rw_ref[...], preferred_element_type=F32)
        s = _sigmoid(logits[:, :LANES].T[:N_EXPERTS, :])
        _route(s, s + rb_ref[...], route_ref.at[:, rows])


def _post_mixer(parts, w, x2d, gate1, shift2, scale2, gain, rw_cat, rbias, tokens_per_batch, tm):
    t, d = x2d.shape
    tpb = tokens_per_batch // tm
    full = lambda i: (0, 0)
    per_b = lambda i: (i // tpb, 0, 0)

    def part_spec(p):
        if p.ndim == 3:
            return pl.BlockSpec((p.shape[0], tm, p.shape[2]), lambda i: (0, i, 0))
        return pl.BlockSpec((tm, p.shape[1]), lambda i: (i, 0))

    return pl.pallas_call(
        functools.partial(_post_kernel, n_parts=len(parts)),
        out_shape=[jax.ShapeDtypeStruct((t, d), F32), jax.ShapeDtypeStruct((t, d), BF16),
                   jax.ShapeDtypeStruct((SUBLANES, t), F32)],
        grid=(t // tm,),
        in_specs=[part_spec(p) for p in parts] + [
                  pl.BlockSpec(w.shape, full),
                  pl.BlockSpec((tm, d), lambda i: (i, 0)),
                  pl.BlockSpec((1, 1, d), per_b),
                  pl.BlockSpec((1, 1, d), per_b),
                  pl.BlockSpec((1, 1, d), per_b),
                  pl.BlockSpec((1, d), full),
                  pl.BlockSpec(rw_cat.shape, full),
                  pl.BlockSpec(rbias.shape, full)],
        out_specs=[pl.BlockSpec((tm, d), lambda i: (i, 0)),
                   pl.BlockSpec((tm, d), lambda i: (i, 0)),
                   pl.BlockSpec((SUBLANES, tm), lambda i: (0, i))],
        compiler_params=_cparams(("arbitrary",)),
        name="post_mixer",
    )(*parts, w, x2d, gate1, shift2, scale2, gain, rw_cat, rbias)


def _moe_layout(t):
    nt = t // MOE_TILE
    grid = -(-(t + N_EXPERT_GROUPS * (ROW_ALIGN - 1) * nt) // MOE_TILE) + N_EXPERT_GROUPS
    return nt, grid


def _moe_tables(gid, t):
    nt, grid = _moe_layout(t)
    ng = N_EXPERT_GROUPS
    per_tile = MOE_TILE // ROW_ALIGN
    onehot = (gid.reshape(nt, MOE_TILE, 1) == jnp.arange(ng, dtype=jnp.int32)).astype(jnp.int32)
    cnt = onehot.sum(axis=1)
    seg = (cnt + ROW_ALIGN - 1) // ROW_ALIGN
    src = jnp.cumsum(seg, axis=1) - seg
    fill = seg.sum(axis=0)
    ntile = (fill + per_tile - 1) // per_tile
    cum = jnp.cumsum(ntile)
    base = (cum - ntile) * per_tile
    dst = jnp.cumsum(seg, axis=0) - seg + base[None, :]
    seg_tab = jnp.concatenate([seg, src, dst], axis=1).reshape(-1).astype(jnp.int32)
    tail = (-fill) % per_tile
    tail_tab = jnp.concatenate([tail, fill + base, cum[-1:]]).astype(jnp.int32)
    i = jnp.arange(grid, dtype=jnp.int32)
    valid = i < cum[-1]
    ie = jnp.minimum(i, cum[-1] - 1)
    g_of = jnp.sum((ie[:, None] >= cum[None, :]).astype(jnp.int32), axis=1)
    return seg_tab, tail_tab, g_of.astype(jnp.int32), valid.astype(jnp.int32)


def _segment_copies(tab_ref, tile, enable, make_copy):
    ng = N_EXPERT_GROUPS
    base = jnp.maximum(tile, 0) * (3 * ng)
    out = []
    for g in range(ng):
        n = tab_ref[base + g]
        src = tab_ref[base + ng + g]
        dst = tab_ref[base + 2 * ng + g]
        for k in range(MOE_SEG_BITS - 1, -1, -1):
            done = (n >> (k + 1)) << (k + 1)
            rows = ROW_ALIGN << k
            s0 = pl.multiple_of((src + done) * ROW_ALIGN, ROW_ALIGN)
            d0 = pl.multiple_of((dst + done) * ROW_ALIGN, ROW_ALIGN)
            out.append((enable & (((n >> k) & 1) == 1), make_copy(s0, d0, rows)))
    return out


def _start_copies(pairs):
    for cond, copies in pairs:
        @pl.when(cond)
        def _():
            for c in copies:
                c.start()


def _wait_copies(pairs):
    for cond, copies in pairs:
        @pl.when(cond)
        def _():
            for c in copies:
                c.wait()


def _split_bf16x3(x):
    hi = x.astype(BF16).astype(F32)
    r1 = x - hi
    mid = r1.astype(BF16).astype(F32)
    lo = (r1 - mid).astype(BF16).astype(F32)
    return hi, mid, lo


def _dispatch_kernel(seg_ref, tail_ref, h_ref, route_ref, tri_ref, slot_ref, hs_ref, cbuf, zbuf, sem, *, nt):
    i = pl.program_id(0)
    tm, d = h_ref.shape
    ng = N_EXPERT_GROUPS
    cur = i % 2

    def seg_copies(tile, enable, buf):
        def seg_copy(s0, d0, rows):
            return (pltpu.make_async_copy(cbuf.at[buf, pl.ds(s0, rows)], hs_ref.at[pl.ds(d0, rows)], sem.at[buf]),)
        return _segment_copies(seg_ref, tile, enable, seg_copy)

    _wait_copies(seg_copies(i - 2, i >= 2, cur))

    route = route_ref[...]
    gid = route[ROUTE_GID_ROW:ROUTE_GID_ROW + 1, :]
    grp = lax.broadcasted_iota(jnp.int32, (SUBLANES, tm), 0).astype(F32)
    onehot = jnp.where(grp == gid, 1.0, 0.0)
    rank = jnp.dot(onehot.astype(BF16), tri_ref[...], preferred_element_type=F32)
    slot = jnp.zeros((1, tm), F32)
    for g in range(ng):
        start = (seg_ref[i * 3 * ng + ng + g] * ROW_ALIGN).astype(F32)
        slot = slot + onehot[g:g + 1, :] * (rank[g:g + 1, :] - 1.0 + start)
    slot_ref[...] = jnp.broadcast_to(slot, (SUBLANES, tm))
    perm = jnp.where(lax.broadcasted_iota(jnp.int32, (MOE_CROWS, tm), 0).astype(F32) == slot, 1.0, 0.0)
    perm = perm.astype(BF16)
    cbuf[cur, :, :d] = jnp.dot(perm, h_ref[...], preferred_element_type=F32).astype(cbuf.dtype)
    parts = jnp.concatenate(list(_split_bf16x3(route)) + [jnp.zeros((LANES - 3 * SUBLANES, tm), F32)], axis=0)
    record = lax.dot_general(perm, parts.astype(BF16), (((1,), (1,)), ((), ())), preferred_element_type=F32)
    cbuf[cur, :, d:] = record.astype(cbuf.dtype)
    _start_copies(seg_copies(i, i >= 0, cur))

    @pl.when(i == pl.num_programs(0) - 1)
    def _():
        _wait_copies(seg_copies(i - 1, i >= 1, 1 - cur))
        _wait_copies(seg_copies(i, i >= 0, cur))
        zbuf[...] = jnp.zeros(zbuf.shape, zbuf.dtype)

        def zero_copy(d0, rows):
            return (pltpu.make_async_copy(zbuf.at[pl.ds(0, rows)], hs_ref.at[pl.ds(d0, rows)], sem.at[0]),)
        pairs = []
        for g in range(ng):
            n = tail_ref[g]
            dst = tail_ref[ng + g]
            for k in range(MOE_TAIL_BITS - 1, -1, -1):
                done = (n >> (k + 1)) << (k + 1)
                d0 = pl.multiple_of((dst + done) * ROW_ALIGN, ROW_ALIGN)
                pairs.append((((n >> k) & 1) == 1, zero_copy(d0, ROW_ALIGN << k)))
        used = tail_ref[2 * ng]
        total = hs_ref.shape[0] // MOE_TILE
        for j in range(total - nt):
            d0 = pl.multiple_of(jnp.minimum(used + j, total - 1) * MOE_TILE, MOE_TILE)
            pairs.append((used + j < total, zero_copy(d0, MOE_TILE)))
        _start_copies(pairs)
        _wait_copies(pairs)


def _dispatch(seg_tab, tail_tab, h2, route, tri):
    t, d = h2.shape
    nt, grid = _moe_layout(t)
    rows = grid * MOE_TILE
    grid_spec = pltpu.PrefetchScalarGridSpec(
        num_scalar_prefetch=2,
        grid=(nt,),
        in_specs=[pl.BlockSpec((MOE_TILE, d), lambda i, *_: (i, 0)),
                  pl.BlockSpec((SUBLANES, MOE_TILE), lambda i, *_: (0, i)),
                  pl.BlockSpec((MOE_TILE, MOE_TILE), lambda i, *_: (0, 0))],
        out_specs=[pl.BlockSpec((SUBLANES, MOE_TILE), lambda i, *_: (0, i)),
                   pl.BlockSpec(memory_space=pl.ANY)],
        scratch_shapes=[pltpu.VMEM((2, MOE_CROWS, d + LANES), BF16), pltpu.VMEM((MOE_TILE, d + LANES), BF16),
                        pltpu.SemaphoreType.DMA((2,))])
    return pl.pallas_call(
        functools.partial(_dispatch_kernel, nt=nt),
        out_shape=[jax.ShapeDtypeStruct((SUBLANES, t), F32),
                   jax.ShapeDtypeStruct((rows, d + LANES), BF16)],
        grid_spec=grid_spec,
        compiler_params=_cparams(("arbitrary",)),
        name="moe_dispatch",
    )(seg_tab, tail_tab, h2, route, tri)


def _ffn_kernel(grp_ref, valid_ref, h_ref, wg32_ref, wu32_ref, wd32_ref, y_ref, wg_ref, wu_ref, wd_ref):
    i = pl.program_id(0)
    d = y_ref.shape[1]

    @pl.when((i == 0) | (grp_ref[i] != grp_ref[jnp.maximum(i - 1, 0)]))
    def _():
        for j in range(EXPERTS_PER_GROUP):
            cols = slice(j * D_FF_EXPERT, (j + 1) * D_FF_EXPERT)
            wg_ref[:, cols] = wg32_ref[j].astype(BF16)
            wu_ref[:, cols] = wu32_ref[j].astype(BF16)
            wd_ref[cols, :] = wd32_ref[j].astype(BF16)

    @pl.when(valid_ref[i] == 0)
    def _():
        y_ref[...] = jnp.zeros(y_ref.shape, y_ref.dtype)

    @pl.when(valid_ref[i] == 1)
    def _():
        h = h_ref[:, :d]
        gates = h_ref[:, d:].astype(F32)
        acts = []
        for j in range(EXPERTS_PER_GROUP):
            cols = slice(j * D_FF_EXPERT, (j + 1) * D_FF_EXPERT)
            a = jnp.dot(h, wg_ref[:, cols], preferred_element_type=F32)
            u = jnp.dot(h, wu_ref[:, cols], preferred_element_type=F32)
            gate = (gates[:, j:j + 1] + gates[:, SUBLANES + j:SUBLANES + j + 1]
                    + gates[:, 2 * SUBLANES + j:2 * SUBLANES + j + 1])
            acts.append(((a * _sigmoid_tanh(a)) * u * gate).astype(BF16))
        y = jnp.dot(jnp.concatenate(acts, axis=1), wd_ref[...], preferred_element_type=F32)
        y_ref[...] = y.astype(y_ref.dtype)


def _ffn(grp, valid, hs, wg, wu, wd, layer):
    rows, width = hs.shape
    d = width - LANES
    epg = EXPERTS_PER_GROUP
    once = pl.Buffered(1)
    w_idx = lambda i, grp, valid: (layer, grp[i], 0, 0)
    grid_spec = pltpu.PrefetchScalarGridSpec(
        num_scalar_prefetch=2,
        grid=(rows // MOE_TILE,),
        in_specs=[pl.BlockSpec((MOE_TILE, width), lambda i, grp, valid: (i, 0)),
                  pl.BlockSpec((None, epg, d, D_FF_EXPERT), w_idx, pipeline_mode=once),
                  pl.BlockSpec((None, epg, d, D_FF_EXPERT), w_idx, pipeline_mode=once),
                  pl.BlockSpec((None, epg, D_FF_EXPERT, d), w_idx, pipeline_mode=once)],
        out_specs=pl.BlockSpec((MOE_TILE, d), lambda i, grp, valid: (i, 0)),
        scratch_shapes=[pltpu.VMEM((d, epg * D_FF_EXPERT), BF16), pltpu.VMEM((d, epg * D_FF_EXPERT), BF16),
                        pltpu.VMEM((epg * D_FF_EXPERT, d), BF16)])
    return pl.pallas_call(
        _ffn_kernel,
        out_shape=jax.ShapeDtypeStruct((rows, d), BF16),
        grid_spec=grid_spec,
        compiler_params=_cparams(("arbitrary",)),
        name="moe_ffn",
    )(grp, valid, hs, wg, wu, wd)


def _combine_kernel(seg_ref, x1_ref, slot_ref, gate2_ref, ys_ref, *rest, fnet):
    if fnet:
        shift_ref, scale_ref, gain_ref, cs_ref, o_ref, y1_ref, y2_ref, ybuf, sem = rest
    else:
        o_ref, ybuf, sem = rest
    i = pl.program_id(0)
    nt = pl.num_programs(0)
    tm = x1_ref.shape[0]
    cur = i % 2

    def seg_copies(tile, enable, buf):
        def seg_copy(s0, d0, rows):
            return (pltpu.make_async_copy(ys_ref.at[pl.ds(d0, rows)], ybuf.at[buf, pl.ds(s0, rows)], sem.at[buf]),)
        return _segment_copies(seg_ref, tile, enable, seg_copy)

    @pl.when(i == 0)
    def _():
        ybuf[...] = jnp.zeros(ybuf.shape, ybuf.dtype)
        _start_copies(seg_copies(i, i == 0, cur))

    nxt = jnp.minimum(i + 1, nt - 1)
    _start_copies(seg_copies(nxt, i + 1 < nt, 1 - cur))
    _wait_copies(seg_copies(i, i >= 0, cur))
    slot = slot_ref[0:1, :]
    perm = jnp.where(lax.broadcasted_iota(jnp.int32, (MOE_CROWS, tm), 0).astype(F32) == slot, 1.0, 0.0)
    y = lax.dot_general(perm.astype(BF16), ybuf[cur], (((0,), (0,)), ((), ())), preferred_element_type=F32)
    x = x1_ref[...] + gate2_ref[0] * y
    o_ref[...] = x
    if fnet:
        _fnet_channel_tile(x, shift_ref, scale_ref, gain_ref, cs_ref, y1_ref, y2_ref)


def _combine(seg_tab, x1, slot, gate2, ys, tokens_per_batch, fnet=None):
    t, d = x1.shape
    tpb = tokens_per_batch // MOE_TILE
    per_b = lambda i, *_: (i // tpb, 0, 0)
    full = lambda i, *_: (0, 0)
    in_specs = [pl.BlockSpec((MOE_TILE, d), lambda i, *_: (i, 0)),
                pl.BlockSpec((SUBLANES, MOE_TILE), lambda i, *_: (0, i)),
                pl.BlockSpec((1, 1, d), per_b),
                pl.BlockSpec(memory_space=pl.ANY)]
    out_shape = [jax.ShapeDtypeStruct((t, d), F32)]
    out_specs = [pl.BlockSpec((MOE_TILE, d), lambda i, *_: (i, 0))]
    args = [seg_tab, x1, slot, gate2, ys]
    if fnet is not None:
        shift, scale, gain, cs = fnet
        gw = d // FNET_GROUPS
        in_specs += [pl.BlockSpec((1, 1, d), per_b), pl.BlockSpec((1, 1, d), per_b), pl.BlockSpec((1, d), full),
                     pl.BlockSpec(cs.shape, full)]
        out_shape += [jax.ShapeDtypeStruct((FNET_GROUPS, t, gw), BF16)] * 2
        out_specs += [pl.BlockSpec((FNET_GROUPS, MOE_TILE, gw), lambda i, *_: (0, i, 0))] * 2
        args += [shift, scale, gain, cs]
    grid_spec = pltpu.PrefetchScalarGridSpec(
        num_scalar_prefetch=1,
        grid=(t // MOE_TILE,),
        in_specs=in_specs,
        out_specs=out_specs,
        scratch_shapes=[pltpu.VMEM((2, MOE_CROWS, d), BF16), pltpu.SemaphoreType.DMA((2,))])
    out = pl.pallas_call(
        functools.partial(_combine_kernel, fnet=fnet is not None),
        out_shape=out_shape,
        grid_spec=grid_spec,
        compiler_params=_cparams(("arbitrary",)),
        name="moe_combine_fnet" if fnet is not None else "moe_combine",
    )(*args)
    return out if fnet is not None else out[0]


def _grouped_moe(h2, route, x1, gate2, wg, wu, wd, layer, tri):
    t = h2.shape[0]
    gid = route[ROUTE_GID_ROW].astype(jnp.int32)
    seg_tab, tail_tab, grp, valid = _moe_tables(gid, t)
    slot, hs = _dispatch(seg_tab, tail_tab, h2, route, tri)
    ys = _ffn(grp, valid, hs, wg, wu, wd, layer)
    return seg_tab, x1, slot, gate2, ys


FFT_J = SUBLANES
FFT_PAIR_ROWS = 2 * FFT_J * (GRID_W // FFT_J)


def _pair_grid_rows(y):
    rows = y.shape[0]
    assert rows % (2 * GRID_W) == 0
    groups = []
    for u in range(rows // (2 * GRID_W)):
        for sb in range(GRID_W // FFT_J):
            for e in range(2):
                start = GRID_W * (2 * u + e) + FFT_J * sb
                groups.append(y[start:start + FFT_J])
    return jnp.concatenate(groups, axis=0)


def _fnet_channel_tile(x, shift_ref, scale_ref, gain_ref, cs_ref, y1_ref, y2_ref):
    h = _norm_modulate(x, gain_ref[...], shift_ref[0], scale_ref[0]).astype(BF16)
    gw = D_MODEL // FNET_GROUPS
    for g in range(FNET_GROUPS):
        y = _pair_grid_rows(jnp.dot(h[:, g * gw:(g + 1) * gw], cs_ref[...], preferred_element_type=F32))
        y1_ref[g] = y[:, :gw].astype(y1_ref.dtype)
        y2_ref[g] = y[:, gw:].astype(y2_ref.dtype)


def _half_rows(m):
    hc = m // 2 + 1
    return hc, -(-FFT_J * hc // ROW_ALIGN) * ROW_ALIGN


def _mirror(lo, hi, hc, m):
    return jnp.concatenate([lo[:FFT_J * hc]] + [hi[FFT_J * c:FFT_J * (c + 1)] for c in range(m - hc, 0, -1)], axis=0)


def _fnet_pos_kernel(y1_ref, y2_ref, lr_ref, ls_ref, cs_ref, sn_ref, o_ref, z1, a_re, a_im, *, n):
    r1 = n // GRID_W
    gw = y1_ref.shape[2]
    hc_r, part_r = _half_rows(r1)
    hc_s, part_s = _half_rows(GRID_W)

    def stage_r(sb, carry):
        s0 = pl.multiple_of(sb * FFT_J, FFT_J)
        p0 = pl.multiple_of(sb * 2 * FFT_J, 2 * FFT_J)
        rhs1 = jnp.concatenate([y1_ref[0, pl.ds(FFT_PAIR_ROWS * u + p0, 2 * FFT_J), :] for u in range(r1 // 2)], axis=0)
        rhs2 = jnp.concatenate([y2_ref[0, pl.ds(FFT_PAIR_ROWS * u + p0, 2 * FFT_J), :] for u in range(r1 // 2)], axis=0)
        p = jnp.dot(lr_ref[...], rhs1, preferred_element_type=F32)
        q = jnp.dot(lr_ref[...], rhs2, preferred_element_type=F32)
        pc, ps = p[:part_r], p[part_r:]
        qc, qs = q[:part_r], q[part_r:]
        re = _mirror(pc - qs, pc + qs, hc_r, r1)
        nim = _mirror(qc + ps, qc - ps, hc_r, r1)
        cs = jnp.concatenate([cs_ref[sb]] * (gw // LANES), axis=1)
        sn = jnp.concatenate([sn_ref[sb]] * (gw // LANES), axis=1)
        tre = re * cs - nim * sn
        tnim = re * sn + nim * cs
        for c in range(r1):
            a_re[pl.ds(GRID_W * c + s0, FFT_J), :] = tre[FFT_J * c:FFT_J * (c + 1)]
            a_im[pl.ds(GRID_W * c + s0, FFT_J), :] = tnim[FFT_J * c:FFT_J * (c + 1)]
        return carry
    lax.fori_loop(0, GRID_W // FFT_J, stage_r, 0)

    cblk = FFT_J * GRID_W

    def stage_s(cb, carry):
        c0 = pl.multiple_of(cb * cblk, cblk)
        u = jnp.dot(ls_ref[0], a_re[pl.ds(c0, cblk), :].astype(BF16), preferred_element_type=F32)
        v = jnp.dot(ls_ref[1], a_im[pl.ds(c0, cblk), :].astype(BF16), preferred_element_type=F32)
        lo = u - v
        hi = u + v
        k0 = pl.multiple_of(cb * FFT_J, FFT_J)
        for d in range(hc_s):
            z1[pl.ds(r1 * d + k0, FFT_J), :] = lo[FFT_J * d:FFT_J * (d + 1)]
        for d in range(1, GRID_W - hc_s + 1):
            z1[pl.ds(r1 * (GRID_W - d) + k0, FFT_J), :] = hi[FFT_J * d:FFT_J * (d + 1)]
        return carry
    lax.fori_loop(0, r1 // FFT_J, stage_s, 0)
    o_ref[0] = z1[...].astype(o_ref.dtype)


def _fnet_position(y1, y2, lr, ls, tw_cos, tw_sin, batch, n):
    groups, t, gw = y1.shape
    full2 = lambda b, g: (0, 0)
    full3 = lambda b, g: (0, 0, 0)
    scratch = pltpu.VMEM((n, gw), F32)
    return pl.pallas_call(
        functools.partial(_fnet_pos_kernel, n=n),
        out_shape=jax.ShapeDtypeStruct((groups, t, gw), BF16),
        grid=(batch, groups),
        in_specs=[pl.BlockSpec((1, n, gw), lambda b, g: (g, b, 0)),
                  pl.BlockSpec((1, n, gw), lambda b, g: (g, b, 0)),
                  pl.BlockSpec(lr.shape, full2),
                  pl.BlockSpec(ls.shape, full3),
                  pl.BlockSpec(tw_cos.shape, full3),
                  pl.BlockSpec(tw_sin.shape, full3)],
        out_specs=pl.BlockSpec((1, n, gw), lambda b, g: (g, b, 0)),
        scratch_shapes=[scratch, scratch, scratch],
        compiler_params=_cparams(("arbitrary", "arbitrary")),
        name="fnet_position",
    )(y1, y2, lr, ls, tw_cos, tw_sin)


def _dft_tables(n):
    assert n % (GRID_W * FFT_J) == 0
    gw = D_MODEL // FNET_GROUPS
    j = np.arange(gw)
    ang = 2.0 * np.pi * ((j[:, None] * j[None, :]) % gw) / gw
    cs = np.concatenate([np.cos(ang), np.sin(ang)], axis=1) / np.sqrt(gw)
    r1 = n // GRID_W
    assert r1 % 2 == 0
    eye = np.eye(FFT_J)
    scale = float(n) ** -0.25
    a = np.arange(r1)
    hc_r, part_r = _half_rows(r1)
    ang_r = 2.0 * np.pi * ((a[:hc_r, None] * a[None, :]) % r1) / r1
    lr = np.zeros((2 * part_r, FFT_J * r1))
    lr[:FFT_J * hc_r] = np.kron(np.cos(ang_r), eye) * scale
    lr[part_r:part_r + FFT_J * hc_r] = np.kron(np.sin(ang_r), eye) * scale
    s = np.arange(GRID_W)
    hc_s, part_s = _half_rows(GRID_W)
    ang_s = 2.0 * np.pi * ((s[:hc_s, None] * s[None, :]) % GRID_W) / GRID_W
    ls = np.zeros((2, part_s, FFT_J * GRID_W))
    ls[0, :FFT_J * hc_s] = np.einsum("ds,cC->dcCs", np.cos(ang_s), eye).reshape(FFT_J * hc_s, FFT_J * GRID_W) * scale
    ls[1, :FFT_J * hc_s] = np.einsum("ds,cC->dcCs", np.sin(ang_s), eye).reshape(FFT_J * hc_s, FFT_J * GRID_W) * scale
    sb = np.arange(GRID_W // FFT_J)
    s_of = sb[:, None, None] * FFT_J + np.arange(FFT_J)[None, None, :]
    ang_t = 2.0 * np.pi * ((s_of * a[None, :, None]) % n) / n
    ang_t = ang_t.reshape(len(sb), r1 * FFT_J, 1)
    tw_cos = jnp.broadcast_to(jnp.asarray(np.cos(ang_t), F32), (len(sb), r1 * FFT_J, LANES))
    tw_sin = jnp.broadcast_to(jnp.asarray(np.sin(ang_t), F32), (len(sb), r1 * FFT_J, LANES))
    return jnp.asarray(cs, BF16), jnp.asarray(lr, BF16), jnp.asarray(ls, BF16), tw_cos, tw_sin


def kernel(x, c, ctx, c_ctx, ada_w, ada_b, norm_mix, norm_ffn, mix_w_in, mix_w_out, na_q_norm, na_k_norm, na_rpb,
           lru_conv_w, lru_conv_b, lru_gate_r_w, lru_gate_r_b, lru_gate_i_w, lru_gate_i_b, lru_lambda,
           fnet_w_out, router_w, router_bias, moe_w_gate, moe_w_up, moe_w_down):
    batch, n, d = x.shape
    ctx_len = ctx.shape[1]
    depth = ada_w.shape[0]
    rows = n // GRID_W
    assert d == D_MODEL and n % (GRID_W * NA_QROWS) == 0 and rows >= 4 * NA_QROWS
    assert n % MOE_TILE == 0 and n % INPROJ_TILE == 0 and n % POST_TILE == 0 and ctx_len % LANES == 0
    t = batch * n
    tri = jnp.asarray(np.triu(np.ones((MOE_TILE, MOE_TILE))), BF16)

    r_pad = -(-(batch + 1) // SUBLANES) * SUBLANES
    c_rows = jnp.concatenate([c, c_ctx[None, :], jnp.zeros((r_pad - batch - 1, d), c.dtype)], axis=0)
    mod = _modulation(c_rows, ada_w, ada_b)

    def mod_slices(layer):
        m = mod[layer, :batch].reshape(batch, 1, 6, d)
        return [m[:, :, i, :] for i in range(6)]

    rw_cat = jnp.pad(router_w.astype(F32), ((0, 0), (0, 2 * LANES - N_EXPERTS))).astype(BF16)
    rbias = router_bias.reshape(N_EXPERTS, 1).astype(F32)
    x2d = x.reshape(t, d)
    ctx2d = ctx.reshape(batch * ctx_len, d)

    pending = None
    for layer in range(depth):
        li = layer // 2
        shift1, scale1, gate1, shift2, scale2, gate2 = mod_slices(layer)
        gain_mix = norm_mix[layer].reshape(1, d)
        gain_ffn = norm_ffn[layer].reshape(1, d)
        if layer % 2 == 0:
            if pending is not None:
                x2d = _combine(*pending, n)
            w_in = mix_w_in[li].astype(BF16)
            ind = jnp.asarray(np.kron(np.eye(NA_HEADS // 2), np.ones((HEAD_DIM, HEAD_DIM))), BF16)
            qg = (jnp.tile(na_q_norm[li], NA_HEADS) * (HEAD_DIM ** -0.5 * LOG2_E)).reshape(1, NA_WIDTH).astype(F32)
            kg = jnp.tile(na_k_norm[li], NA_HEADS).reshape(1, NA_WIDTH).astype(F32)
            q, k, v, xb, gb = _inproj(x2d, shift1, scale1, gain_mix, w_in, ind, qg, kg,
                                      ("q", "k", "v", "x", "g"), n, INPROJ_TILE)
            mctx = mod[layer, batch, :2 * d]
            shift_c = jnp.broadcast_to(mctx[:d], (batch, 1, d))
            scale_c = jnp.broadcast_to(mctx[d:], (batch, 1, d))
            k_c, v_c, xb_c = _inproj(ctx2d, shift_c, scale_c, gain_mix, w_in[:, NA_WIDTH:4 * NA_WIDTH], ind, qg, kg,
                                     ("k", "v", "x"), ctx_len, ctx_len)
            bias = _na_bias_tables(na_rpb[li], rows)
            attn = _attention(q, k, v, k_c, v_c, bias, batch, n, ctx_len)
            wcat, gbias = _lru_gate_weights(lru_gate_r_w[li], lru_gate_r_b[li], lru_gate_i_w[li], lru_gate_i_b[li])
            lru = _lru(xb, gb, xb_c, lru_conv_w[li].astype(F32), lru_conv_b[li].reshape(1, LRU_WIDTH).astype(F32),
                       wcat, gbias, lru_lambda[li].astype(F32), batch, n, ctx_len)
            parts, w_out = [attn, lru], mix_w_out[li].astype(BF16)
        else:
            cs, lr, ls, tw_cos, tw_sin = _dft_tables(n)
            x2d, y1, y2 = _combine(*pending, n, fnet=(shift1, scale1, gain_mix, cs))
            parts, w_out = [_fnet_position(y1, y2, lr, ls, tw_cos, tw_sin, batch, n)], fnet_w_out[li].astype(BF16)
        x1, h2, route = _post_mixer(parts, w_out, x2d, gate1, shift2, scale2, gain_ffn, rw_cat, rbias, n,
                                    POST_TILE)
        pending = _grouped_moe(h2, route, x1, gate2, moe_w_gate.astype(F32), moe_w_up.astype(F32),
                               moe_w_down.astype(F32), layer, tri)
    return _combine(*pending, n).reshape(batch, n, d)
```

```python
import functools

import numpy as np
import jax
import jax.numpy as jnp
from jax import lax
from jax.experimental import pallas as pl
from jax.experimental.pallas import tpu as pltpu

F32 = jnp.float32
BF16 = jnp.bfloat16
HIGHEST = lax.Precision.HIGHEST

D_MODEL = 1024
GRID_W = 64
HEAD_DIM = 64
NA_HEADS = 8
NA_WIDTH = NA_HEADS * HEAD_DIM
NA_WIN_ROWS = 8
NA_WIN_COLS = 16
LRU_WIDTH = 512
LRU_BLOCK = 64
LRU_C = 8.0
FNET_GROUPS = 4
N_EXPERTS = 16
EXPERTS_PER_GROUP = 4
N_EXPERT_GROUPS = 4
D_FF_EXPERT = 512
RMS_EPS = 1e-6
MASK_VALUE = -1e30
LOG2_E = 1.4426950408889634

V7X_VMEM_LIMIT_BYTES = 56 * 1024 * 1024
LANES = 128
SUBLANES = 8

NA_QROWS = 4
NA_KROWS = NA_QROWS + NA_WIN_ROWS - 1
NA_QBLK = NA_QROWS * GRID_W
NA_KBLK = NA_KROWS * GRID_W
NA_STEP_BLOCKS = 2

LRU_CHUNK = LANES
LRU_TROWS = 512

ROUTE_GID_ROW = EXPERTS_PER_GROUP
MOE_TILE = 512
ROW_ALIGN = 16
MOE_CROWS = MOE_TILE + N_EXPERT_GROUPS * ROW_ALIGN
MOE_SEG_BITS = (MOE_TILE // ROW_ALIGN).bit_length()
MOE_TAIL_BITS = (MOE_TILE // ROW_ALIGN - 1).bit_length()


def _sigmoid(x):
    return 1.0 / (1.0 + jnp.exp(-x))


def _sigmoid_tanh(x):
    return 0.5 + 0.5 * jnp.tanh(0.5 * x)


def _cparams(sem, vmem=V7X_VMEM_LIMIT_BYTES):
    return pltpu.CompilerParams(dimension_semantics=sem, vmem_limit_bytes=vmem)


def _mod_kernel(c_ref, w_ref, b_ref, o_ref):
    c = c_ref[...]
    s = c * _sigmoid(c)
    o_ref[0] = jnp.dot(s.astype(BF16), w_ref[0].astype(BF16), preferred_element_type=F32) + b_ref[0]


def _modulation(c_rows, ada_w, ada_b):
    depth, d, n6 = ada_w.shape
    r = c_rows.shape[0]
    tn = n6 // 2
    return pl.pallas_call(
        _mod_kernel,
        out_shape=jax.ShapeDtypeStruct((depth, r, n6), F32),
        grid=(depth, n6 // tn),
        in_specs=[pl.BlockSpec((r, d), lambda l, j: (0, 0)),
                  pl.BlockSpec((1, d, tn), lambda l, j: (l, 0, j)),
                  pl.BlockSpec((1, 1, tn), lambda l, j: (l, 0, j))],
        out_specs=pl.BlockSpec((1, r, tn), lambda l, j: (l, 0, j)),
        compiler_params=_cparams(("arbitrary", "arbitrary")),
        name="adaln_mod",
    )(c_rows, ada_w, ada_b.reshape(depth, 1, n6))


def _norm_modulate(x, gain, shift, scale):
    ms = jnp.mean(x * x, axis=-1, keepdims=True)
    y = x * lax.rsqrt(ms + RMS_EPS) * gain
    return y * (1.0 + scale) + shift


def _inproj_kernel(x_ref, shift_ref, scale_ref, gain_ref, w_ref, ind_ref, qg_ref, kg_ref, *rest, segs, n_side):
    side_in, out_refs, side_out = rest[:n_side], rest[n_side:n_side + len(segs)], rest[n_side + len(segs):]
    for src, dst in zip(side_in, side_out):
        dst[...] = src[...].astype(dst.dtype)
    h = _norm_modulate(x_ref[...], gain_ref[...], shift_ref[0], scale_ref[0]).astype(BF16)
    for s, (kind, o_ref) in enumerate(zip(segs, out_refs)):
        z = jnp.dot(h, w_ref[:, s * NA_WIDTH:(s + 1) * NA_WIDTH], preferred_element_type=F32)
        if kind in ("q", "k"):
            zz = (z * z).astype(BF16)
            hw = ind_ref.shape[0]
            ms = jnp.concatenate([jnp.dot(zz[:, i * hw:(i + 1) * hw], ind_ref[...], preferred_element_type=F32)
                                  for i in range(NA_WIDTH // hw)], axis=1) * (1.0 / HEAD_DIM)
            g = qg_ref[...] if kind == "q" else kg_ref[...]
            z = z * lax.rsqrt(ms + RMS_EPS) * g
        if kind in ("x", "g"):
            for c in range(LRU_WIDTH // LRU_CHUNK):
                o_ref[c] = z[:, c * LRU_CHUNK:(c + 1) * LRU_CHUNK].astype(o_ref.dtype)
        else:
            o_ref[...] = z.astype(o_ref.dtype)


def _inproj(x2d, shift, scale, gain, w, ind, qg, kg, segs, tokens_per_batch, tm, side=()):
    t, d = x2d.shape
    tpb = tokens_per_batch // tm
    dt = {"q": BF16, "k": BF16, "v": BF16, "x": F32, "g": F32}
    full = lambda i: (0, 0)
    nch = LRU_WIDTH // LRU_CHUNK
    steps = t // tm
    side_specs, side_shapes = [], []
    for arr in side:
        items = arr.shape[0]
        share = -(-items // steps)
        assert items % share == 0
        last = items // share - 1
        side_specs.append(pl.BlockSpec((share,) + arr.shape[1:], lambda i, last=last: (jnp.minimum(i, last), 0, 0)))
        side_shapes.append(jax.ShapeDtypeStruct(arr.shape, BF16))

    def out_shape(kind):
        shape = (nch, t, LRU_CHUNK) if kind in ("x", "g") else (t, NA_WIDTH)
        return jax.ShapeDtypeStruct(shape, dt[kind])

    def out_spec(kind):
        if kind in ("x", "g"):
            return pl.BlockSpec((nch, tm, LRU_CHUNK), lambda i: (0, i, 0))
        return pl.BlockSpec((tm, NA_WIDTH), lambda i: (i, 0))

    return pl.pallas_call(
        functools.partial(_inproj_kernel, segs=segs, n_side=len(side)),
        out_shape=[out_shape(k) for k in segs] + side_shapes,
        grid=(steps,),
        in_specs=[pl.BlockSpec((tm, d), lambda i: (i, 0)),
                  pl.BlockSpec((1, 1, d), lambda i: (i // tpb, 0, 0)),
                  pl.BlockSpec((1, 1, d), lambda i: (i // tpb, 0, 0)),
                  pl.BlockSpec((1, d), full),
                  pl.BlockSpec(w.shape, full),
                  pl.BlockSpec(ind.shape, full),
                  pl.BlockSpec((1, NA_WIDTH), full),
                  pl.BlockSpec((1, NA_WIDTH), full)] + side_specs,
        out_specs=[out_spec(k) for k in segs] + side_specs,
        compiler_params=_cparams(("arbitrary",)),
        name="inproj_" + "".join(segs),
    )(x2d, shift, scale, gain, w, ind, qg, kg, *side)


def _na_bias_tables(rpb, rows):
    kr = NA_WIN_ROWS
    rb_count = rows // NA_QROWS
    cq = np.arange(GRID_W)
    ck = np.arange(GRID_W)
    col_start = np.clip(cq - NA_WIN_COLS // 2, 0, GRID_W - NA_WIN_COLS)
    valid_c = (ck[None, :] >= col_start[:, None]) & (ck[None, :] < col_start[:, None] + NA_WIN_COLS)
    dc = np.clip(ck[None, :] - cq[:, None], 1 - NA_WIN_COLS, NA_WIN_COLS - 1) + (NA_WIN_COLS - 1)
    n_dr, n_dc = 2 * NA_WIN_ROWS - 1, 2 * NA_WIN_COLS - 1
    sel_c = (dc[:, :, None] == np.arange(n_dc)) & valid_c[:, :, None]
    blocks = jnp.einsum("hrc,qkc->hrqk", rpb.astype(F32), jnp.asarray(sel_c, F32), precision=HIGHEST)
    blocks = blocks + jnp.asarray(np.where(valid_c, 0.0, MASK_VALUE), F32)
    blocks = jnp.concatenate([blocks, jnp.full((NA_HEADS, 1, GRID_W, GRID_W), MASK_VALUE, F32)], axis=1)
    blocks = blocks * LOG2_E
    which = []
    for rb in (0, 1, rb_count - 1):
        r = rb * NA_QROWS + np.arange(NA_QROWS)
        ks = int(np.clip(rb * NA_QROWS - kr // 2, 0, rows - NA_KROWS))
        key_r = ks + np.arange(NA_KROWS)
        row_start = np.clip(r - kr // 2, 0, rows - kr)
        valid_r = (key_r[None, :] >= row_start[:, None]) & (key_r[None, :] < row_start[:, None] + kr)
        dr = np.clip(key_r[None, :] - r[:, None] + (NA_WIN_ROWS - 1), 0, n_dr - 1)
        which.append(np.where(valid_r, dr, n_dr))
    return _na_bias_assemble(blocks, which)


def _na_bias_kernel(blk_ref, o_ref, *, which):
    for t, table in enumerate(which):
        @pl.when(pl.program_id(0) == t)
        def _():
            for i in range(NA_QROWS):
                row = jnp.concatenate([blk_ref[0, int(table[i, j])] for j in range(NA_KROWS)], axis=1)
                o_ref[0, 0, i * GRID_W:(i + 1) * GRID_W, :] = row


def _na_bias_assemble(blocks, which):
    heads, nblk = blocks.shape[:2]
    return pl.pallas_call(
        functools.partial(_na_bias_kernel, which=which),
        out_shape=jax.ShapeDtypeStruct((len(which), heads, NA_QBLK, NA_KBLK), F32),
        grid=(len(which), heads),
        in_specs=[pl.BlockSpec((1, nblk, GRID_W, GRID_W), lambda t, h: (h, 0, 0, 0))],
        out_specs=pl.BlockSpec((1, 1, NA_QBLK, NA_KBLK), lambda t, h: (t, h, 0, 0)),
        compiler_params=_cparams(("arbitrary", "arbitrary")),
        name="na_bias",
    )(blocks)


def _attn_kernel(q_ref, k_ref, v_ref, kc_ref, vc_ref, bias_ref, o_ref, *, rows):
    last = rows // NA_QROWS - 1
    nt = (((1,), (1,)), ((), ()))
    ctx_len = kc_ref.shape[0]
    low_half = lax.broadcasted_iota(jnp.int32, (NA_QBLK, LANES), 1) < HEAD_DIM
    for blk in range(NA_STEP_BLOCKS):
        rb = pl.program_id(1) * NA_STEP_BLOCKS + blk
        ks = jnp.clip(rb * NA_QROWS - NA_WIN_ROWS // 2, 0, rows - NA_KROWS)
        kstart = pl.multiple_of(ks * GRID_W, GRID_W)
        geom = jnp.where(rb == 0, 0, jnp.where(rb == last, 2, 1))
        qrows = slice(blk * NA_QBLK, (blk + 1) * NA_QBLK)
        for pair in range(NA_HEADS * HEAD_DIM // LANES):
            ls = slice(pair * LANES, (pair + 1) * LANES)
            q2 = q_ref[qrows, ls]
            k_all = jnp.concatenate([kc_ref[:, ls], k_ref[pl.ds(kstart, NA_KBLK), ls]], axis=0)
            v_all = jnp.concatenate([vc_ref[:, ls], v_ref[pl.ds(kstart, NA_KBLK), ls]], axis=0)
            outs = []
            for half in range(2):
                qh = jnp.where(low_half == (half == 0), q2, jnp.zeros_like(q2))
                s = lax.dot_general(qh, k_all, nt, preferred_element_type=F32)
                s = jnp.concatenate([s[:, :ctx_len], s[:, ctx_len:] + bias_ref[geom, 2 * pair + half]], axis=1)
                m = jnp.max(s, axis=-1, keepdims=True)
                p = jnp.exp2(s - m)
                l = jnp.sum(p, axis=-1, keepdims=True)
                outs.append(jnp.dot(p.astype(BF16), v_all, preferred_element_type=F32) / l)
            o_ref[qrows, ls] = jnp.where(low_half, outs[0], outs[1]).astype(o_ref.dtype)


def _attention(q, k, v, kc, vc, bias, batch, n, ctx_len):
    rows = n // GRID_W
    rbc = rows // (NA_QROWS * NA_STEP_BLOCKS)
    qblk = NA_QBLK * NA_STEP_BLOCKS
    return pl.pallas_call(
        functools.partial(_attn_kernel, rows=rows),
        out_shape=jax.ShapeDtypeStruct((batch * n, NA_WIDTH), BF16),
        grid=(batch, rbc),
        in_specs=[pl.BlockSpec((qblk, NA_WIDTH), lambda b, rb: (b * rbc + rb, 0)),
                  pl.BlockSpec((n, NA_WIDTH), lambda b, rb: (b, 0)),
                  pl.BlockSpec((n, NA_WIDTH), lambda b, rb: (b, 0)),
                  pl.BlockSpec((ctx_len, NA_WIDTH), lambda b, rb: (b, 0)),
                  pl.BlockSpec((ctx_len, NA_WIDTH), lambda b, rb: (b, 0)),
                  pl.BlockSpec(bias.shape, lambda b, rb: (0, 0, 0, 0), pipeline_mode=pl.Buffered(1))],
        out_specs=pl.BlockSpec((qblk, NA_WIDTH), lambda b, rb: (b * rbc + rb, 0)),
        compiler_params=_cparams(("arbitrary", "arbitrary")),
        name="na_attention",
    )(q, k, v, kc, vc, bias)


SCAN_GROUPS = 4
SCAN_CHUNKS = SCAN_GROUPS * SUBLANES


def _scan_pitch(n):
    p = -(-n // SCAN_CHUNKS)
    while p % 8 != 4:
        p += 1
    return p


NEG_LOG2_E = -LOG2_E


def _lru_coeff_tile(half_xc, zh, half_bias, k, d):
    c = LRU_CHUNK
    t_r = jnp.tanh(zh[:, (2 * d) * c:(2 * d + 1) * c] + half_bias[:, (2 * d) * c:(2 * d + 1) * c])
    t_i = jnp.tanh(zh[:, (2 * d + 1) * c:(2 * d + 2) * c] + half_bias[:, (2 * d + 1) * c:(2 * d + 2) * c])
    neg_log_a = k[d:d + 1, :] * (1.0 + t_r)
    a = jnp.exp2(neg_log_a * NEG_LOG2_E)
    one_minus_a2 = jnp.tanh(neg_log_a) * (a * a + 1.0)
    root = jnp.where(one_minus_a2 > 0.0, one_minus_a2 * lax.rsqrt(one_minus_a2), 0.0)
    return a, root * (half_xc + half_xc * t_i)


def _conv_tile(xpad, t0, w, b, rows):
    acc = b + w[0:1, :] * xpad[pl.ds(t0 + SUBLANES - 2, rows), :]
    acc = acc + w[1:2, :] * xpad[pl.ds(t0 + SUBLANES - 1, rows), :]
    acc = acc + w[2:3, :] * xpad[pl.ds(t0 + SUBLANES, rows), :]
    return acc + w[3:4, :] * xpad[pl.ds(t0 + SUBLANES + 1, rows), :]


SCAN_UNROLL = 4


def _group_rows(j, g, pitch):
    return pl.ds(g * SUBLANES * pitch + j, SUBLANES, stride=pitch)


def _chunk_totals(af_ref, bf_ref, ab_ref, bb_ref, pitch):
    def body(j, carry):
        jb = pitch - 1 - j
        out = []
        for g in range(SCAN_GROUPS):
            pf, hf, pb, hb = carry[4 * g:4 * g + 4]
            af = af_ref[_group_rows(j, g, pitch), :]
            ab = ab_ref[_group_rows(jb, g, pitch), :]
            out += [af * pf, af * hf + bf_ref[_group_rows(j, g, pitch), :],
                    ab * pb, ab * hb + bb_ref[_group_rows(jb, g, pitch), :]]
        return tuple(out)
    one = jnp.ones((SUBLANES, LRU_CHUNK), F32)
    zero = jnp.zeros((SUBLANES, LRU_CHUNK), F32)
    res = lax.fori_loop(0, pitch, body, (one, zero, one, zero) * SCAN_GROUPS, unroll=SCAN_UNROLL)
    fwd = [(res[4 * g], res[4 * g + 1]) for g in range(SCAN_GROUPS)]
    bwd = [(res[4 * g + 2], res[4 * g + 3]) for g in range(SCAN_GROUPS)]
    return fwd, bwd


def _chunk_starts(totals, h0, reverse):
    row = lax.broadcasted_iota(jnp.int32, (SUBLANES, LRU_CHUNK), 0)
    starts = [jnp.zeros((SUBLANES, LRU_CHUNK), F32) for _ in range(SCAN_GROUPS)]
    state = h0
    order = range(SCAN_CHUNKS - 1, -1, -1) if reverse else range(SCAN_CHUNKS)
    for c in order:
        g, s = divmod(c, SUBLANES)
        p_end, h_end = totals[g]
        starts[g] = jnp.where(row == s, state, starts[g])
        state = p_end[s:s + 1, :] * state + h_end[s:s + 1, :]
    return starts, state


def _scan_write(af_ref, bf_ref, hf_ref, ab_ref, bb_ref, hb_ref, starts_f, starts_b, pitch):
    def body(j, carry):
        jb = pitch - 1 - j
        out = []
        for g in range(SCAN_GROUPS):
            hf, hb = carry[2 * g:2 * g + 2]
            hf = af_ref[_group_rows(j, g, pitch), :] * hf + bf_ref[_group_rows(j, g, pitch), :]
            hb = ab_ref[_group_rows(jb, g, pitch), :] * hb + bb_ref[_group_rows(jb, g, pitch), :]
            hf_ref[_group_rows(j, g, pitch), :] = hf
            hb_ref[_group_rows(jb, g, pitch), :] = hb
            out += [hf, hb]
        return tuple(out)
    init = tuple(v for g in range(SCAN_GROUPS) for v in (starts_f[g], starts_b[g]))
    lax.fori_loop(0, pitch, body, init, unroll=SCAN_UNROLL)


def _lru_kernel(x_ref, g_ref, xc_ref, cw_ref, cb_ref, w_ref, gb_ref, lam_ref, o_ref,
                xpad, a0, b0, a1, b1, h0s, h1s, ca0, cb0, ca1, cb1, *, n, ctx_len):
    pitch = _scan_pitch(n)
    cpitch = _scan_pitch(ctx_len)
    cw = cw_ref[...]
    cb = cb_ref[...]
    gbias = gb_ref[0]
    lam = lam_ref[...]
    sp = jnp.maximum(-lam, 0.0) + jnp.log1p(jnp.exp(-jnp.abs(lam)))
    k = (0.5 * LRU_C) * sp
    wcat = w_ref[0]
    zeros8 = jnp.zeros((SUBLANES, LRU_CHUNK), F32)

    def fill_coeffs(src_rows, total, length, trows, a_refs, b_refs):
        for d in range(2):
            a_refs[d][pl.ds(length, total - length), :] = jnp.ones((total - length, LRU_CHUNK), F32)
            b_refs[d][pl.ds(length, total - length), :] = jnp.zeros((total - length, LRU_CHUNK), F32)
        xpad[pl.ds(0, SUBLANES), :] = zeros8
        xpad[pl.ds(SUBLANES + length, SUBLANES), :] = zeros8
        xpad[pl.ds(SUBLANES, length), :] = src_rows

        def tile(t, carry):
            t0 = pl.multiple_of(t * trows, SUBLANES)
            xc = _conv_tile(xpad, t0, cw, cb, trows)
            zh = jnp.dot(xc.astype(BF16), wcat, preferred_element_type=F32)
            half_xc = 0.5 * xc
            for d in range(2):
                a, b = _lru_coeff_tile(half_xc, zh, gbias, k, d)
                a_refs[d][pl.ds(t0, trows), :] = a
                b_refs[d][pl.ds(t0, trows), :] = b
            return carry
        lax.fori_loop(0, length // trows, tile, 0)

    fill_coeffs(xc_ref[0], SCAN_CHUNKS * cpitch, ctx_len, ctx_len, (ca0, ca1), (cb0, cb1))
    zero_state = jnp.zeros((1, LRU_CHUNK), F32)
    fwd, bwd = _chunk_totals(ca0, cb0, ca1, cb1, cpitch)
    _, init_f = _chunk_starts(fwd, zero_state, reverse=False)
    _, init_b = _chunk_starts(bwd, zero_state, reverse=True)

    fill_coeffs(x_ref[0], SCAN_CHUNKS * pitch, n, LRU_TROWS, (a0, a1), (b0, b1))
    fwd, bwd = _chunk_totals(a0, b0, a1, b1, pitch)
    starts_f, _ = _chunk_starts(fwd, init_f, reverse=False)
    starts_b, _ = _chunk_starts(bwd, init_b, reverse=True)
    _scan_write(a0, b0, h0s, a1, b1, h1s, starts_f, starts_b, pitch)

    def out_tile(t, carry):
        t0 = pl.multiple_of(t * LRU_TROWS, SUBLANES)
        y = h0s[pl.ds(t0, LRU_TROWS), :] + h1s[pl.ds(t0, LRU_TROWS), :]
        g = g_ref[0, pl.ds(t0, LRU_TROWS), :]
        gelu = 0.5 * g * (1.0 + jnp.tanh(0.7978845608028654 * (g + 0.044715 * (g * g * g))))
        o_ref[0, pl.ds(t0, LRU_TROWS), :] = (gelu * y).astype(o_ref.dtype)
        return carry
    lax.fori_loop(0, n // LRU_TROWS, out_tile, 0)


def _lru(xb, gb, xb_ctx, conv_w, conv_b, wcat, gbias, lam, batch, n, ctx_len):
    nch = LRU_WIDTH // LRU_CHUNK
    pitch = _scan_pitch(n)
    cpitch = _scan_pitch(ctx_len)
    big = pltpu.VMEM((SCAN_CHUNKS * pitch, LRU_CHUNK), F32)
    small = pltpu.VMEM((SCAN_CHUNKS * cpitch, LRU_CHUNK), F32)
    return pl.pallas_call(
        functools.partial(_lru_kernel, n=n, ctx_len=ctx_len),
        out_shape=jax.ShapeDtypeStruct((nch, batch * n, LRU_CHUNK), BF16),
        grid=(batch, nch),
        in_specs=[pl.BlockSpec((1, n, LRU_CHUNK), lambda b, c: (c, b, 0)),
                  pl.BlockSpec((1, n, LRU_CHUNK), lambda b, c: (c, b, 0)),
                  pl.BlockSpec((1, ctx_len, LRU_CHUNK), lambda b, c: (c, b, 0)),
                  pl.BlockSpec((4, LRU_CHUNK), lambda b, c: (0, c)),
                  pl.BlockSpec((1, LRU_CHUNK), lambda b, c: (0, c)),
                  pl.BlockSpec((1, LRU_CHUNK, 4 * LRU_CHUNK), lambda b, c: (c, 0, 0)),
                  pl.BlockSpec((1, 1, 4 * LRU_CHUNK), lambda b, c: (c, 0, 0)),
                  pl.BlockSpec((2, LRU_CHUNK), lambda b, c: (0, c))],
        out_specs=pl.BlockSpec((1, n, LRU_CHUNK), lambda b, c: (c, b, 0)),
        scratch_shapes=[pltpu.VMEM((n + 2 * SUBLANES, LRU_CHUNK), F32),
                        big, big, big, big, big, big, small, small, small, small],
        compiler_params=_cparams(("arbitrary", "arbitrary")),
        name="rglru",
    )(xb, gb, xb_ctx, conv_w, conv_b, wcat, gbias, lam)


def _lru_gate_weights(w_r, b_r, w_i, b_i):
    nch = LRU_WIDTH // LRU_CHUNK
    bpc = LRU_CHUNK // LRU_BLOCK

    def dense(w):
        wc = w.reshape(nch, bpc, LRU_BLOCK, LRU_BLOCK)
        eye = jnp.eye(bpc, dtype=w.dtype)
        return jnp.einsum("cbij,bd->cbidj", wc, eye).reshape(nch, LRU_CHUNK, LRU_CHUNK)

    wcat = jnp.concatenate([dense(w_r[0]), dense(w_i[0]), dense(w_r[1]), dense(w_i[1])], axis=-1)
    chunk = lambda v: v.reshape(nch, 1, LRU_CHUNK)
    gbias = jnp.concatenate([chunk(b_r[0]), chunk(b_i[0]), chunk(b_r[1]), chunk(b_i[1])], axis=-1)
    return (0.5 * wcat).astype(BF16), (0.5 * gbias).astype(F32)


def _route(s, sel, route_ref):
    srow = [s[e:e + 1, :] for e in range(N_EXPERTS)]
    lrow = [sel[e:e + 1, :] for e in range(N_EXPERTS)]
    gscore = []
    for g in range(N_EXPERT_GROUPS):
        a = lrow[g * EXPERTS_PER_GROUP:(g + 1) * EXPERTS_PER_GROUP]
        best = a[0] + a[1]
        for i, j in ((0, 2), (0, 3), (1, 2), (1, 3), (2, 3)):
            best = jnp.maximum(best, a[i] + a[j])
        gscore.append(best)
    bg = jnp.zeros_like(gscore[0], dtype=jnp.int32)
    bv = gscore[0]
    for g in range(1, N_EXPERT_GROUPS):
        upd = gscore[g] > bv
        bg = jnp.where(upd, g, bg)
        bv = jnp.where(upd, gscore[g], bv)

    def pick(rows_):
        out = []
        for j in range(EXPERTS_PER_GROUP):
            v = rows_[j]
            for g in range(1, N_EXPERT_GROUPS):
                v = jnp.where(bg == g, rows_[g * EXPERTS_PER_GROUP + j], v)
            out.append(v)
        return out
    cand = pick(lrow)
    cs = pick(srow)
    i1 = jnp.zeros_like(bg)
    v1 = cand[0]
    w1 = cs[0]
    for j in range(1, EXPERTS_PER_GROUP):
        upd = cand[j] > v1
        i1 = jnp.where(upd, j, i1)
        v1 = jnp.where(upd, cand[j], v1)
        w1 = jnp.where(upd, cs[j], w1)
    i2 = jnp.full_like(bg, -1)
    v2 = jnp.full_like(v1, -jnp.inf)
    w2 = jnp.zeros_like(w1)
    for j in range(EXPERTS_PER_GROUP):
        upd = (i1 != j) & (cand[j] > v2)
        i2 = jnp.where(upd, j, i2)
        v2 = jnp.where(upd, cand[j], v2)
        w2 = jnp.where(upd, cs[j], w2)
    den = w1 + w2
    g1 = w1 / den
    g2 = w2 / den
    for j in range(EXPERTS_PER_GROUP):
        route_ref[j:j + 1, :] = jnp.where(i1 == j, g1, 0.0) + jnp.where(i2 == j, g2, 0.0)
    route_ref[ROUTE_GID_ROW:ROUTE_GID_ROW + 1, :] = bg.astype(F32)
    pad = SUBLANES - ROUTE_GID_ROW - 1
    route_ref[ROUTE_GID_ROW + 1:, :] = jnp.zeros((pad, bg.shape[1]), F32)


POST_SUBTILE = 512
POST_TILE = 2 * POST_SUBTILE
INPROJ_TILE = 1024


def _post_kernel(*refs, n_parts):
    parts = refs[:n_parts]
    (w_ref, x_ref, gate_ref, shift_ref, scale_ref, gain_ref, rw_ref, rb_ref,
     x1_ref, h2_ref, route_ref) = refs[n_parts:]
    for sub in range(x_ref.shape[0] // POST_SUBTILE):
        rows = pl.ds(sub * POST_SUBTILE, POST_SUBTILE)
        pieces = []
        for p in parts:
            pieces += [p[c, rows, :] for c in range(p.shape[0])] if len(p.shape) == 3 else [p[rows, :]]
        mixed = jnp.concatenate(pieces, axis=-1) if len(pieces) > 1 else pieces[0]
        mix = jnp.dot(mixed, w_ref[...], preferred_element_type=F32)
        x1 = x_ref[rows, :] + gate_ref[0] * mix
        x1_ref[rows, :] = x1
        h2 = _norm_modulate(x1, gain_ref[...], shift_ref[0], scale_ref[0])
        h_hi = h2.astype(BF16)
        h2_ref[rows, :] = h_hi
        logits = jnp.dot(h_hi, rw_ref[...], preferred_element_type=F32)
        s = _sigmoid(logits[:, :LANES].T[:N_EXPERTS, :])
        _route(s, s + rb_ref[...], route_ref.at[:, rows])


def _post_mixer(parts, w, x2d, gate1, shift2, scale2, gain, rw_cat, rbias, tokens_per_batch, tm):
    t, d = x2d.shape
    tpb = tokens_per_batch // tm
    full = lambda i: (0, 0)
    per_b = lambda i: (i // tpb, 0, 0)

    def part_spec(p):
        if p.ndim == 3:
            return pl.BlockSpec((p.shape[0], tm, p.shape[2]), lambda i: (0, i, 0))
        return pl.BlockSpec((tm, p.shape[1]), lambda i: (i, 0))

    return pl.pallas_call(
        functools.partial(_post_kernel, n_parts=len(parts)),
        out_shape=[jax.ShapeDtypeStruct((t, d), F32), jax.ShapeDtypeStruct((t, d), BF16),
                   jax.ShapeDtypeStruct((SUBLANES, t), F32)],
        grid=(t // tm,),
        in_specs=[part_spec(p) for p in parts] + [
                  pl.BlockSpec(w.shape, full),
                  pl.BlockSpec((tm, d), lambda i: (i, 0)),
                  pl.BlockSpec((1, 1, d), per_b),
                  pl.BlockSpec((1, 1, d), per_b),
                  pl.BlockSpec((1, 1, d), per_b),
                  pl.BlockSpec((1, d), full),
                  pl.BlockSpec(rw_cat.shape, full),
                  pl.BlockSpec(rbias.shape, full)],
        out_specs=[pl.BlockSpec((tm, d), lambda i: (i, 0)),
                   pl.BlockSpec((tm, d), lambda i: (i, 0)),
                   pl.BlockSpec((SUBLANES, tm), lambda i: (0, i))],
        compiler_params=_cparams(("arbitrary",)),
        name="post_mixer",
    )(*parts, w, x2d, gate1, shift2, scale2, gain, rw_cat, rbias)


def _moe_layout(t):
    nt = t // MOE_TILE
    grid = -(-(t + N_EXPERT_GROUPS * (ROW_ALIGN - 1) * nt) // MOE_TILE) + N_EXPERT_GROUPS
    return nt, grid


def _moe_tables(gid, t):
    nt, grid = _moe_layout(t)
    ng = N_EXPERT_GROUPS
    per_tile = MOE_TILE // ROW_ALIGN
    onehot = (gid.reshape(nt, MOE_TILE, 1) == jnp.arange(ng, dtype=jnp.int32)).astype(jnp.int32)
    cnt = onehot.sum(axis=1)
    seg = (cnt + ROW_ALIGN - 1) // ROW_ALIGN
    src = jnp.cumsum(seg, axis=1) - seg
    fill = seg.sum(axis=0)
    ntile = (fill + per_tile - 1) // per_tile
    cum = jnp.cumsum(ntile)
    base = (cum - ntile) * per_tile
    dst = jnp.cumsum(seg, axis=0) - seg + base[None, :]
    seg_tab = jnp.concatenate([seg, src, dst], axis=1).reshape(-1).astype(jnp.int32)
    tail = (-fill) % per_tile
    tail_tab = jnp.concatenate([tail, fill + base, cum[-1:]]).astype(jnp.int32)
    i = jnp.arange(grid, dtype=jnp.int32)
    valid = i < cum[-1]
    ie = jnp.minimum(i, cum[-1] - 1)
    g_of = jnp.sum((ie[:, None] >= cum[None, :]).astype(jnp.int32), axis=1)
    return seg_tab, tail_tab, g_of.astype(jnp.int32), valid.astype(jnp.int32)


def _segment_copies(tab_ref, tile, enable, make_copy):
    ng = N_EXPERT_GROUPS
    base = jnp.maximum(tile, 0) * (3 * ng)
    out = []
    for g in range(ng):
        n = tab_ref[base + g]
        src = tab_ref[base + ng + g]
        dst = tab_ref[base + 2 * ng + g]
        for k in range(MOE_SEG_BITS - 1, -1, -1):
            done = (n >> (k + 1)) << (k + 1)
            rows = ROW_ALIGN << k
            s0 = pl.multiple_of((src + done) * ROW_ALIGN, ROW_ALIGN)
            d0 = pl.multiple_of((dst + done) * ROW_ALIGN, ROW_ALIGN)
            out.append((enable & (((n >> k) & 1) == 1), make_copy(s0, d0, rows)))
    return out


def _start_copies(pairs):
    for cond, copies in pairs:
        @pl.when(cond)
        def _():
            for c in copies:
                c.start()


def _wait_copies(pairs):
    for cond, copies in pairs:
        @pl.when(cond)
        def _():
            for c in copies:
                c.wait()


def _split_bf16x3(x):
    hi = x.astype(BF16).astype(F32)
    r1 = x - hi
    mid = r1.astype(BF16).astype(F32)
    lo = (r1 - mid).astype(BF16).astype(F32)
    return hi, mid, lo


def _dispatch_kernel(seg_ref, tail_ref, h_ref, route_ref, tri_ref, slot_ref, hs_ref, cbuf, zbuf, sem, *, nt):
    i = pl.program_id(0)
    tm, d = h_ref.shape
    ng = N_EXPERT_GROUPS
    cur = i % 2

    def seg_copies(tile, enable, buf):
        def seg_copy(s0, d0, rows):
            return (pltpu.make_async_copy(cbuf.at[buf, pl.ds(s0, rows)], hs_ref.at[pl.ds(d0, rows)], sem.at[buf]),)
        return _segment_copies(seg_ref, tile, enable, seg_copy)

    _wait_copies(seg_copies(i - 2, i >= 2, cur))

    route = route_ref[...]
    gid = route[ROUTE_GID_ROW:ROUTE_GID_ROW + 1, :]
    grp = lax.broadcasted_iota(jnp.int32, (SUBLANES, tm), 0).astype(F32)
    onehot = jnp.where(grp == gid, 1.0, 0.0)
    rank = jnp.dot(onehot.astype(BF16), tri_ref[...], preferred_element_type=F32)
    slot = jnp.zeros((1, tm), F32)
    for g in range(ng):
        start = (seg_ref[i * 3 * ng + ng + g] * ROW_ALIGN).astype(F32)
        slot = slot + onehot[g:g + 1, :] * (rank[g:g + 1, :] - 1.0 + start)
    slot_ref[...] = jnp.broadcast_to(slot, (SUBLANES, tm))
    perm = jnp.where(lax.broadcasted_iota(jnp.int32, (MOE_CROWS, tm), 0).astype(F32) == slot, 1.0, 0.0)
    perm = perm.astype(BF16)
    cbuf[cur, :, :d] = jnp.dot(perm, h_ref[...], preferred_element_type=F32).astype(cbuf.dtype)
    parts = jnp.concatenate(list(_split_bf16x3(route)) + [jnp.zeros((LANES - 3 * SUBLANES, tm), F32)], axis=0)
    record = lax.dot_general(perm, parts.astype(BF16), (((1,), (1,)), ((), ())), preferred_element_type=F32)
    cbuf[cur, :, d:] = record.astype(cbuf.dtype)
    _start_copies(seg_copies(i, i >= 0, cur))

    @pl.when(i == pl.num_programs(0) - 1)
    def _():
        _wait_copies(seg_copies(i - 1, i >= 1, 1 - cur))
        _wait_copies(seg_copies(i, i >= 0, cur))
        zbuf[...] = jnp.zeros(zbuf.shape, zbuf.dtype)

        def zero_copy(d0, rows):
            return (pltpu.make_async_copy(zbuf.at[pl.ds(0, rows)], hs_ref.at[pl.ds(d0, rows)], sem.at[0]),)
        pairs = []
        for g in range(ng):
            n = tail_ref[g]
            dst = tail_ref[ng + g]
            for k in range(MOE_TAIL_BITS - 1, -1, -1):
                done = (n >> (k + 1)) << (k + 1)
                d0 = pl.multiple_of((dst + done) * ROW_ALIGN, ROW_ALIGN)
                pairs.append((((n >> k) & 1) == 1, zero_copy(d0, ROW_ALIGN << k)))
        used = tail_ref[2 * ng]
        total = hs_ref.shape[0] // MOE_TILE
        for j in range(total - nt):
            d0 = pl.multiple_of(jnp.minimum(used + j, total - 1) * MOE_TILE, MOE_TILE)
            pairs.append((used + j < total, zero_copy(d0, MOE_TILE)))
        _start_copies(pairs)
        _wait_copies(pairs)


def _dispatch(seg_tab, tail_tab, h2, route, tri):
    t, d = h2.shape
    nt, grid = _moe_layout(t)
    rows = grid * MOE_TILE
    grid_spec = pltpu.PrefetchScalarGridSpec(
        num_scalar_prefetch=2,
        grid=(nt,),
        in_specs=[pl.BlockSpec((MOE_TILE, d), lambda i, *_: (i, 0)),
                  pl.BlockSpec((SUBLANES, MOE_TILE), lambda i, *_: (0, i)),
                  pl.BlockSpec((MOE_TILE, MOE_TILE), lambda i, *_: (0, 0))],
        out_specs=[pl.BlockSpec((SUBLANES, MOE_TILE), lambda i, *_: (0, i)),
                   pl.BlockSpec(memory_space=pl.ANY)],
        scratch_shapes=[pltpu.VMEM((2, MOE_CROWS, d + LANES), BF16), pltpu.VMEM((MOE_TILE, d + LANES), BF16),
                        pltpu.SemaphoreType.DMA((2,))])
    return pl.pallas_call(
        functools.partial(_dispatch_kernel, nt=nt),
        out_shape=[jax.ShapeDtypeStruct((SUBLANES, t), F32),
                   jax.ShapeDtypeStruct((rows, d + LANES), BF16)],
        grid_spec=grid_spec,
        compiler_params=_cparams(("arbitrary",)),
        name="moe_dispatch",
    )(seg_tab, tail_tab, h2, route, tri)


def _ffn_kernel(grp_ref, valid_ref, h_ref, wg_ref, wu_ref, wd_ref, y_ref):
    i = pl.program_id(0)
    d = y_ref.shape[1]

    @pl.when(valid_ref[i] == 0)
    def _():
        y_ref[...] = jnp.zeros(y_ref.shape, y_ref.dtype)

    @pl.when(valid_ref[i] == 1)
    def _():
        h = h_ref[:, :d]
        gates = h_ref[:, d:].astype(F32)
        acts = []
        for j in range(EXPERTS_PER_GROUP):
            a = jnp.dot(h, wg_ref[j], preferred_element_type=F32)
            u = jnp.dot(h, wu_ref[j], preferred_element_type=F32)
            gate = (gates[:, j:j + 1] + gates[:, SUBLANES + j:SUBLANES + j + 1]
                    + gates[:, 2 * SUBLANES + j:2 * SUBLANES + j + 1])
            acts.append(((a * _sigmoid_tanh(a)) * u * gate).astype(BF16))
        wd = wd_ref[...].reshape(EXPERTS_PER_GROUP * D_FF_EXPERT, d)
        y = jnp.dot(jnp.concatenate(acts, axis=1), wd, preferred_element_type=F32)
        y_ref[...] = y.astype(y_ref.dtype)


def _ffn(grp, valid, hs, wg, wu, wd, layer):
    rows, width = hs.shape
    d = width - LANES
    epg = EXPERTS_PER_GROUP
    w_idx = lambda i, grp, valid: (layer * N_EXPERT_GROUPS + grp[i], 0, 0)
    grid_spec = pltpu.PrefetchScalarGridSpec(
        num_scalar_prefetch=2,
        grid=(rows // MOE_TILE,),
        in_specs=[pl.BlockSpec((MOE_TILE, width), lambda i, grp, valid: (i, 0)),
                  pl.BlockSpec((epg, d, D_FF_EXPERT), w_idx),
                  pl.BlockSpec((epg, d, D_FF_EXPERT), w_idx),
                  pl.BlockSpec((epg, D_FF_EXPERT, d), w_idx)],
        out_specs=pl.BlockSpec((MOE_TILE, d), lambda i, grp, valid: (i, 0)))
    return pl.pallas_call(
        _ffn_kernel,
        out_shape=jax.ShapeDtypeStruct((rows, d), BF16),
        grid_spec=grid_spec,
        compiler_params=_cparams(("arbitrary",)),
        name="moe_ffn",
    )(grp, valid, hs, wg, wu, wd)


def _combine_kernel(seg_ref, x1_ref, slot_ref, gate2_ref, ys_ref, *rest, fnet):
    if fnet:
        shift_ref, scale_ref, gain_ref, cs_ref, o_ref, y1_ref, y2_ref, ybuf, sem = rest
    else:
        o_ref, ybuf, sem = rest
    i = pl.program_id(0)
    nt = pl.num_programs(0)
    tm = x1_ref.shape[0]
    cur = i % 2

    def seg_copies(tile, enable, buf):
        def seg_copy(s0, d0, rows):
            return (pltpu.make_async_copy(ys_ref.at[pl.ds(d0, rows)], ybuf.at[buf, pl.ds(s0, rows)], sem.at[buf]),)
        return _segment_copies(seg_ref, tile, enable, seg_copy)

    @pl.when(i == 0)
    def _():
        ybuf[...] = jnp.zeros(ybuf.shape, ybuf.dtype)
        _start_copies(seg_copies(i, i == 0, cur))

    nxt = jnp.minimum(i + 1, nt - 1)
    _start_copies(seg_copies(nxt, i + 1 < nt, 1 - cur))
    _wait_copies(seg_copies(i, i >= 0, cur))
    slot = slot_ref[0:1, :]
    perm = jnp.where(lax.broadcasted_iota(jnp.int32, (MOE_CROWS, tm), 0).astype(F32) == slot, 1.0, 0.0)
    y = lax.dot_general(perm.astype(BF16), ybuf[cur], (((0,), (0,)), ((), ())), preferred_element_type=F32)
    x = x1_ref[...] + gate2_ref[0] * y
    o_ref[...] = x
    if fnet:
        _fnet_channel_tile(x, shift_ref, scale_ref, gain_ref, cs_ref, y1_ref, y2_ref)


def _combine(seg_tab, x1, slot, gate2, ys, tokens_per_batch, fnet=None):
    t, d = x1.shape
    tpb = tokens_per_batch // MOE_TILE
    per_b = lambda i, *_: (i // tpb, 0, 0)
    full = lambda i, *_: (0, 0)
    in_specs = [pl.BlockSpec((MOE_TILE, d), lambda i, *_: (i, 0)),
                pl.BlockSpec((SUBLANES, MOE_TILE), lambda i, *_: (0, i)),
                pl.BlockSpec((1, 1, d), per_b),
                pl.BlockSpec(memory_space=pl.ANY)]
    out_shape = [jax.ShapeDtypeStruct((t, d), F32)]
    out_specs = [pl.BlockSpec((MOE_TILE, d), lambda i, *_: (i, 0))]
    args = [seg_tab, x1, slot, gate2, ys]
    if fnet is not None:
        shift, scale, gain, cs = fnet
        gw = d // FNET_GROUPS
        in_specs += [pl.BlockSpec((1, 1, d), per_b), pl.BlockSpec((1, 1, d), per_b), pl.BlockSpec((1, d), full),
                     pl.BlockSpec(cs.shape, full)]
        out_shape += [jax.ShapeDtypeStruct((FNET_GROUPS, t, gw), BF16)] * 2
        out_specs += [pl.BlockSpec((FNET_GROUPS, MOE_TILE, gw), lambda i, *_: (0, i, 0))] * 2
        args += [shift, scale, gain, cs]
    grid_spec = pltpu.PrefetchScalarGridSpec(
        num_scalar_prefetch=1,
        grid=(t // MOE_TILE,),
        in_specs=in_specs,
        out_specs=out_specs,
        scratch_shapes=[pltpu.VMEM((2, MOE_CROWS, d), BF16), pltpu.SemaphoreType.DMA((2,))])
    out = pl.pallas_call(
        functools.partial(_combine_kernel, fnet=fnet is not None),
        out_shape=out_shape,
        grid_spec=grid_spec,
        compiler_params=_cparams(("arbitrary",)),
        name="moe_combine_fnet" if fnet is not None else "moe_combine",
    )(*args)
    return out if fnet is not None else out[0]


def _grouped_moe(h2, route, x1, gate2, wg, wu, wd, layer, tri):
    t = h2.shape[0]
    gid = route[ROUTE_GID_ROW].astype(jnp.int32)
    seg_tab, tail_tab, grp, valid = _moe_tables(gid, t)
    slot, hs = _dispatch(seg_tab, tail_tab, h2, route, tri)
    ys = _ffn(grp, valid, hs, wg, wu, wd, layer)
    return seg_tab, x1, slot, gate2, ys


FFT_J = SUBLANES
FFT_PAIR_ROWS = 2 * FFT_J * (GRID_W // FFT_J)


def _pair_grid_rows(y):
    rows = y.shape[0]
    assert rows % (2 * GRID_W) == 0
    groups = []
    for u in range(rows // (2 * GRID_W)):
        for sb in range(GRID_W // FFT_J):
            for e in range(2):
                start = GRID_W * (2 * u + e) + FFT_J * sb
                groups.append(y[start:start + FFT_J])
    return jnp.concatenate(groups, axis=0)


def _fnet_channel_tile(x, shift_ref, scale_ref, gain_ref, cs_ref, y1_ref, y2_ref):
    h = _norm_modulate(x, gain_ref[...], shift_ref[0], scale_ref[0]).astype(BF16)
    gw = D_MODEL // FNET_GROUPS
    for g in range(FNET_GROUPS):
        y = _pair_grid_rows(jnp.dot(h[:, g * gw:(g + 1) * gw], cs_ref[...], preferred_element_type=F32))
        y1_ref[g] = y[:, :gw].astype(y1_ref.dtype)
        y2_ref[g] = y[:, gw:].astype(y2_ref.dtype)


def _half_rows(m):
    hc = m // 2 + 1
    return hc, -(-FFT_J * hc // ROW_ALIGN) * ROW_ALIGN


def _mirror(lo, hi, hc, m):
    return jnp.concatenate([lo[:FFT_J * hc]] + [hi[FFT_J * c:FFT_J * (c + 1)] for c in range(m - hc, 0, -1)], axis=0)


def _fnet_pos_kernel(y1_ref, y2_ref, lr_ref, ls_ref, cs_ref, sn_ref, o_ref, z1, a_re, a_im, *, n):
    r1 = n // GRID_W
    gw = y1_ref.shape[2]
    hc_r, part_r = _half_rows(r1)
    hc_s, part_s = _half_rows(GRID_W)

    def stage_r(sb, carry):
        s0 = pl.multiple_of(sb * FFT_J, FFT_J)
        p0 = pl.multiple_of(sb * 2 * FFT_J, 2 * FFT_J)
        rhs1 = jnp.concatenate([y1_ref[0, pl.ds(FFT_PAIR_ROWS * u + p0, 2 * FFT_J), :] for u in range(r1 // 2)], axis=0)
        rhs2 = jnp.concatenate([y2_ref[0, pl.ds(FFT_PAIR_ROWS * u + p0, 2 * FFT_J), :] for u in range(r1 // 2)], axis=0)
        p = jnp.dot(lr_ref[...], rhs1, preferred_element_type=F32)
        q = jnp.dot(lr_ref[...], rhs2, preferred_element_type=F32)
        pc, ps = p[:part_r], p[part_r:]
        qc, qs = q[:part_r], q[part_r:]
        re = _mirror(pc - qs, pc + qs, hc_r, r1)
        nim = _mirror(qc + ps, qc - ps, hc_r, r1)
        cs = jnp.concatenate([cs_ref[sb]] * (gw // LANES), axis=1)
        sn = jnp.concatenate([sn_ref[sb]] * (gw // LANES), axis=1)
        tre = re * cs - nim * sn
        tnim = re * sn + nim * cs
        for c in range(r1):
            a_re[pl.ds(GRID_W * c + s0, FFT_J), :] = tre[FFT_J * c:FFT_J * (c + 1)]
            a_im[pl.ds(GRID_W * c + s0, FFT_J), :] = tnim[FFT_J * c:FFT_J * (c + 1)]
        return carry
    lax.fori_loop(0, GRID_W // FFT_J, stage_r, 0)

    cblk = FFT_J * GRID_W

    def stage_s(cb, carry):
        c0 = pl.multiple_of(cb * cblk, cblk)
        u = jnp.dot(ls_ref[0], a_re[pl.ds(c0, cblk), :].astype(BF16), preferred_element_type=F32)
        v = jnp.dot(ls_ref[1], a_im[pl.ds(c0, cblk), :].astype(BF16), preferred_element_type=F32)
        lo = u - v
        hi = u + v
        k0 = pl.multiple_of(cb * FFT_J, FFT_J)
        for d in range(hc_s):
            z1[pl.ds(r1 * d + k0, FFT_J), :] = lo[FFT_J * d:FFT_J * (d + 1)]
        for d in range(1, GRID_W - hc_s + 1):
            z1[pl.ds(r1 * (GRID_W - d) + k0, FFT_J), :] = hi[FFT_J * d:FFT_J * (d + 1)]
        return carry
    lax.fori_loop(0, r1 // FFT_J, stage_s, 0)
    o_ref[0] = z1[...].astype(o_ref.dtype)


def _fnet_position(y1, y2, lr, ls, tw_cos, tw_sin, batch, n):
    groups, t, gw = y1.shape
    full2 = lambda b, g: (0, 0)
    full3 = lambda b, g: (0, 0, 0)
    scratch = pltpu.VMEM((n, gw), F32)
    return pl.pallas_call(
        functools.partial(_fnet_pos_kernel, n=n),
        out_shape=jax.ShapeDtypeStruct((groups, t, gw), BF16),
        grid=(batch, groups),
        in_specs=[pl.BlockSpec((1, n, gw), lambda b, g: (g, b, 0)),
                  pl.BlockSpec((1, n, gw), lambda b, g: (g, b, 0)),
                  pl.BlockSpec(lr.shape, full2),
                  pl.BlockSpec(ls.shape, full3),
                  pl.BlockSpec(tw_cos.shape, full3),
                  pl.BlockSpec(tw_sin.shape, full3)],
        out_specs=pl.BlockSpec((1, n, gw), lambda b, g: (g, b, 0)),
        scratch_shapes=[scratch, scratch, scratch],
        compiler_params=_cparams(("arbitrary", "arbitrary")),
        name="fnet_position",
    )(y1, y2, lr, ls, tw_cos, tw_sin)


def _dft_tables(n):
    assert n % (GRID_W * FFT_J) == 0
    gw = D_MODEL // FNET_GROUPS
    j = np.arange(gw)
    ang = 2.0 * np.pi * ((j[:, None] * j[None, :]) % gw) / gw
    cs = np.concatenate([np.cos(ang), np.sin(ang)], axis=1) / np.sqrt(gw)
    r1 = n // GRID_W
    assert r1 % 2 == 0
    eye = np.eye(FFT_J)
    scale = float(n) ** -0.25
    a = np.arange(r1)
    hc_r, part_r = _half_rows(r1)
    ang_r = 2.0 * np.pi * ((a[:hc_r, None] * a[None, :]) % r1) / r1
    lr = np.zeros((2 * part_r, FFT_J * r1))
    lr[:FFT_J * hc_r] = np.kron(np.cos(ang_r), eye) * scale
    lr[part_r:part_r + FFT_J * hc_r] = np.kron(np.sin(ang_r), eye) * scale
    s = np.arange(GRID_W)
    hc_s, part_s = _half_rows(GRID_W)
    ang_s = 2.0 * np.pi * ((s[:hc_s, None] * s[None, :]) % GRID_W) / GRID_W
    ls = np.zeros((2, part_s, FFT_J * GRID_W))
    ls[0, :FFT_J * hc_s] = np.einsum("ds,cC->dcCs", np.cos(ang_s), eye).reshape(FFT_J * hc_s, FFT_J * GRID_W) * scale
    ls[1, :FFT_J * hc_s] = np.einsum("ds,cC->dcCs", np.sin(ang_s), eye).reshape(FFT_J * hc_s, FFT_J * GRID_W) * scale
    sb = np.arange(GRID_W // FFT_J)
    s_of = sb[:, None, None] * FFT_J + np.arange(FFT_J)[None, None, :]
    ang_t = 2.0 * np.pi * ((s_of * a[None, :, None]) % n) / n
    ang_t = ang_t.reshape(len(sb), r1 * FFT_J, 1)
    tw_cos = jnp.broadcast_to(jnp.asarray(np.cos(ang_t), F32), (len(sb), r1 * FFT_J, LANES))
    tw_sin = jnp.broadcast_to(jnp.asarray(np.sin(ang_t), F32), (len(sb), r1 * FFT_J, LANES))
    return jnp.asarray(cs, BF16), jnp.asarray(lr, BF16), jnp.asarray(ls, BF16), tw_cos, tw_sin


def kernel(x, c, ctx, c_ctx, ada_w, ada_b, norm_mix, norm_ffn, mix_w_in, mix_w_out, na_q_norm, na_k_norm, na_rpb,
           lru_conv_w, lru_conv_b, lru_gate_r_w, lru_gate_r_b, lru_gate_i_w, lru_gate_i_b, lru_lambda,
           fnet_w_out, router_w, router_bias, moe_w_gate, moe_w_up, moe_w_down):
    batch, n, d = x.shape
    ctx_len = ctx.shape[1]
    depth = ada_w.shape[0]
    rows = n // GRID_W
    assert d == D_MODEL and n % (GRID_W * NA_QROWS) == 0 and rows >= 4 * NA_QROWS
    assert n % MOE_TILE == 0 and n % INPROJ_TILE == 0 and n % POST_TILE == 0 and ctx_len % LANES == 0
    t = batch * n
    tri = jnp.asarray(np.triu(np.ones((MOE_TILE, MOE_TILE))), BF16)

    r_pad = -(-(batch + 1) // SUBLANES) * SUBLANES
    c_rows = jnp.concatenate([c, c_ctx[None, :], jnp.zeros((r_pad - batch - 1, d), c.dtype)], axis=0)
    mod = _modulation(c_rows, ada_w, ada_b)

    def mod_slices(layer):
        m = mod[layer, :batch].reshape(batch, 1, 6, d)
        return [m[:, :, i, :] for i in range(6)]

    rw_cat = jnp.pad(router_w.astype(F32), ((0, 0), (0, 2 * LANES - N_EXPERTS))).astype(BF16)
    rbias = router_bias.reshape(N_EXPERTS, 1).astype(F32)
    x2d = x.reshape(t, d)
    ctx2d = ctx.reshape(batch * ctx_len, d)

    pending = None
    moe_w = None
    for layer in range(depth):
        li = layer // 2
        shift1, scale1, gate1, shift2, scale2, gate2 = mod_slices(layer)
        gain_mix = norm_mix[layer].reshape(1, d)
        gain_ffn = norm_ffn[layer].reshape(1, d)
        if layer % 2 == 0:
            if pending is not None:
                x2d = _combine(*pending, n)
            w_in = mix_w_in[li].astype(BF16)
            ind = jnp.asarray(np.kron(np.eye(NA_HEADS // 2), np.ones((HEAD_DIM, HEAD_DIM))), BF16)
            qg = (jnp.tile(na_q_norm[li], NA_HEADS) * (HEAD_DIM ** -0.5 * LOG2_E)).reshape(1, NA_WIDTH).astype(F32)
            kg = jnp.tile(na_k_norm[li], NA_HEADS).reshape(1, NA_WIDTH).astype(F32)
            side = () if moe_w is not None else (moe_w_gate.reshape((-1,) + moe_w_gate.shape[2:]).astype(F32),
                                                 moe_w_up.reshape((-1,) + moe_w_up.shape[2:]).astype(F32),
                                                 moe_w_down.reshape((-1,) + moe_w_down.shape[2:]).astype(F32))
            q, k, v, xb, gb, *rounded = _inproj(x2d, shift1, scale1, gain_mix, w_in, ind, qg, kg,
                                                ("q", "k", "v", "x", "g"), n, INPROJ_TILE, side=side)
            if rounded:
                moe_w = rounded
            mctx = mod[layer, batch, :2 * d]
            shift_c = jnp.broadcast_to(mctx[:d], (batch, 1, d))
            scale_c = jnp.broadcast_to(mctx[d:], (batch, 1, d))
            k_c, v_c, xb_c = _inproj(ctx2d, shift_c, scale_c, gain_mix, w_in[:, NA_WIDTH:4 * NA_WIDTH], ind, qg, kg,
                                     ("k", "v", "x"), ctx_len, ctx_len)
            bias = _na_bias_tables(na_rpb[li], rows)
            attn = _attention(q, k, v, k_c, v_c, bias, batch, n, ctx_len)
            wcat, gbias = _lru_gate_weights(lru_gate_r_w[li], lru_gate_r_b[li], lru_gate_i_w[li], lru_gate_i_b[li])
            lru = _lru(xb, gb, xb_c, lru_conv_w[li].astype(F32), lru_conv_b[li].reshape(1, LRU_WIDTH).astype(F32),
                       wcat, gbias, lru_lambda[li].astype(F32), batch, n, ctx_len)
            parts, w_out = [attn, lru], mix_w_out[li].astype(BF16)
        else:
            cs, lr, ls, tw_cos, tw_sin = _dft_tables(n)
            x2d, y1, y2 = _combine(*pending, n, fnet=(shift1, scale1, gain_mix, cs))
            parts, w_out = [_fnet_position(y1, y2, lr, ls, tw_cos, tw_sin, batch, n)], fnet_w_out[li].astype(BF16)
        x1, h2, route = _post_mixer(parts, w_out, x2d, gate1, shift2, scale2, gain_ffn, rw_cat, rbias, n,
                                    POST_TILE)
        pending = _grouped_moe(h2, route, x1, gate2, *moe_w, layer, tri)
    return _combine(*pending, n).reshape(batch, n, d)
```

```python
import functools

import numpy as np
import jax
import jax.numpy as jnp
from jax import lax
from jax.experimental import pallas as pl
from jax.experimental.pallas import tpu as pltpu

F32 = jnp.float32
BF16 = jnp.bfloat16
HIGHEST = lax.Precision.HIGHEST

D_MODEL = 1024
GRID_W = 64
HEAD_DIM = 64
NA_HEADS = 8
NA_WIDTH = NA_HEADS * HEAD_DIM
NA_WIN_ROWS = 8
NA_WIN_COLS = 16
LRU_WIDTH = 512
LRU_BLOCK = 64
LRU_C = 8.0
FNET_GROUPS = 4
N_EXPERTS = 16
EXPERTS_PER_GROUP = 4
N_EXPERT_GROUPS = 4
D_FF_EXPERT = 512
RMS_EPS = 1e-6
MASK_VALUE = -1e30
LOG2_E = 1.4426950408889634

V7X_VMEM_LIMIT_BYTES = 56 * 1024 * 1024
LANES = 128
SUBLANES = 8

NA_QROWS = 4
NA_KROWS = NA_QROWS + NA_WIN_ROWS - 1
NA_QBLK = NA_QROWS * GRID_W
NA_KBLK = NA_KROWS * GRID_W
NA_STEP_BLOCKS = 2

LRU_CHUNK = LANES
LRU_TROWS = 512

ROUTE_GID_ROW = EXPERTS_PER_GROUP
MOE_TILE = 512
ROW_ALIGN = 16
MOE_CROWS = MOE_TILE + N_EXPERT_GROUPS * ROW_ALIGN
MOE_SEG_BITS = (MOE_TILE // ROW_ALIGN).bit_length()
X1_RING = 3
MOE_TAIL_BITS = (MOE_TILE // ROW_ALIGN - 1).bit_length()


def _sigmoid(x):
    return 1.0 / (1.0 + jnp.exp(-x))


def _sigmoid_tanh(x):
    return 0.5 + 0.5 * jnp.tanh(0.5 * x)


def _cparams(sem, vmem=V7X_VMEM_LIMIT_BYTES):
    return pltpu.CompilerParams(dimension_semantics=sem, vmem_limit_bytes=vmem)


def _mod_kernel(c_ref, w_ref, b_ref, o_ref):
    c = c_ref[...]
    s = c * _sigmoid(c)
    o_ref[0] = jnp.dot(s.astype(BF16), w_ref[0].astype(BF16), preferred_element_type=F32) + b_ref[0]


def _modulation(c_rows, ada_w, ada_b):
    depth, d, n6 = ada_w.shape
    r = c_rows.shape[0]
    tn = n6 // 2
    return pl.pallas_call(
        _mod_kernel,
        out_shape=jax.ShapeDtypeStruct((depth, r, n6), F32),
        grid=(depth, n6 // tn),
        in_specs=[pl.BlockSpec((r, d), lambda l, j: (0, 0)),
                  pl.BlockSpec((1, d, tn), lambda l, j: (l, 0, j)),
                  pl.BlockSpec((1, 1, tn), lambda l, j: (l, 0, j))],
        out_specs=pl.BlockSpec((1, r, tn), lambda l, j: (l, 0, j)),
        compiler_params=_cparams(("arbitrary", "arbitrary")),
        name="adaln_mod",
    )(c_rows, ada_w, ada_b.reshape(depth, 1, n6))


def _norm_modulate(x, gain, shift, scale):
    ms = jnp.mean(x * x, axis=-1, keepdims=True)
    y = x * lax.rsqrt(ms + RMS_EPS) * gain
    return y * (1.0 + scale) + shift


def _inproj_kernel(x_ref, shift_ref, scale_ref, gain_ref, w_ref, ind_ref, qg_ref, kg_ref, *rest, segs, n_side):
    side_in, out_refs, side_out = rest[:n_side], rest[n_side:n_side + len(segs)], rest[n_side + len(segs):]
    for src, dst in zip(side_in, side_out):
        dst[...] = src[...].astype(dst.dtype)
    h = _norm_modulate(x_ref[...], gain_ref[...], shift_ref[0], scale_ref[0]).astype(BF16)
    for s, (kind, o_ref) in enumerate(zip(segs, out_refs)):
        z = jnp.dot(h, w_ref[:, s * NA_WIDTH:(s + 1) * NA_WIDTH], preferred_element_type=F32)
        if kind in ("q", "k"):
            zz = (z * z).astype(BF16)
            hw = ind_ref.shape[0]
            ms = jnp.concatenate([jnp.dot(zz[:, i * hw:(i + 1) * hw], ind_ref[...], preferred_element_type=F32)
                                  for i in range(NA_WIDTH // hw)], axis=1) * (1.0 / HEAD_DIM)
            g = qg_ref[...] if kind == "q" else kg_ref[...]
            z = z * lax.rsqrt(ms + RMS_EPS) * g
        if kind in ("x", "g"):
            for c in range(LRU_WIDTH // LRU_CHUNK):
                o_ref[c] = z[:, c * LRU_CHUNK:(c + 1) * LRU_CHUNK].astype(o_ref.dtype)
        else:
            o_ref[...] = z.astype(o_ref.dtype)


def _inproj(x2d, shift, scale, gain, w, ind, qg, kg, segs, tokens_per_batch, tm, side=()):
    t, d = x2d.shape
    tpb = tokens_per_batch // tm
    dt = {"q": BF16, "k": BF16, "v": BF16, "x": F32, "g": F32}
    full = lambda i: (0, 0)
    nch = LRU_WIDTH // LRU_CHUNK
    steps = t // tm
    side_specs, side_shapes = [], []
    for arr in side:
        items = arr.shape[0]
        share = -(-items // steps)
        assert items % share == 0
        last = items // share - 1
        side_specs.append(pl.BlockSpec((share,) + arr.shape[1:], lambda i, last=last: (jnp.minimum(i, last), 0, 0)))
        side_shapes.append(jax.ShapeDtypeStruct(arr.shape, BF16))

    def out_shape(kind):
        shape = (nch, t, LRU_CHUNK) if kind in ("x", "g") else (t, NA_WIDTH)
        return jax.ShapeDtypeStruct(shape, dt[kind])

    def out_spec(kind):
        if kind in ("x", "g"):
            return pl.BlockSpec((nch, tm, LRU_CHUNK), lambda i: (0, i, 0))
        return pl.BlockSpec((tm, NA_WIDTH), lambda i: (i, 0))

    return pl.pallas_call(
        functools.partial(_inproj_kernel, segs=segs, n_side=len(side)),
        out_shape=[out_shape(k) for k in segs] + side_shapes,
        grid=(steps,),
        in_specs=[pl.BlockSpec((tm, d), lambda i: (i, 0)),
                  pl.BlockSpec((1, 1, d), lambda i: (i // tpb, 0, 0)),
                  pl.BlockSpec((1, 1, d), lambda i: (i // tpb, 0, 0)),
                  pl.BlockSpec((1, d), full),
                  pl.BlockSpec(w.shape, full),
                  pl.BlockSpec(ind.shape, full),
                  pl.BlockSpec((1, NA_WIDTH), full),
                  pl.BlockSpec((1, NA_WIDTH), full)] + side_specs,
        out_specs=[out_spec(k) for k in segs] + side_specs,
        compiler_params=_cparams(("arbitrary",)),
        name="inproj_" + "".join(segs),
    )(x2d, shift, scale, gain, w, ind, qg, kg, *side)


def _na_bias_tables(rpb, rows):
    kr = NA_WIN_ROWS
    rb_count = rows // NA_QROWS
    cq = np.arange(GRID_W)
    ck = np.arange(GRID_W)
    col_start = np.clip(cq - NA_WIN_COLS // 2, 0, GRID_W - NA_WIN_COLS)
    valid_c = (ck[None, :] >= col_start[:, None]) & (ck[None, :] < col_start[:, None] + NA_WIN_COLS)
    dc = np.clip(ck[None, :] - cq[:, None], 1 - NA_WIN_COLS, NA_WIN_COLS - 1) + (NA_WIN_COLS - 1)
    n_dr, n_dc = 2 * NA_WIN_ROWS - 1, 2 * NA_WIN_COLS - 1
    sel_c = (dc[:, :, None] == np.arange(n_dc)) & valid_c[:, :, None]
    blocks = jnp.einsum("hrc,qkc->hrqk", rpb.astype(F32), jnp.asarray(sel_c, F32), precision=HIGHEST)
    blocks = blocks + jnp.asarray(np.where(valid_c, 0.0, MASK_VALUE), F32)
    blocks = jnp.concatenate([blocks, jnp.full((NA_HEADS, 1, GRID_W, GRID_W), MASK_VALUE, F32)], axis=1)
    blocks = blocks * LOG2_E
    which = []
    for rb in (0, 1, rb_count - 1):
        r = rb * NA_QROWS + np.arange(NA_QROWS)
        ks = int(np.clip(rb * NA_QROWS - kr // 2, 0, rows - NA_KROWS))
        key_r = ks + np.arange(NA_KROWS)
        row_start = np.clip(r - kr // 2, 0, rows - kr)
        valid_r = (key_r[None, :] >= row_start[:, None]) & (key_r[None, :] < row_start[:, None] + kr)
        dr = np.clip(key_r[None, :] - r[:, None] + (NA_WIN_ROWS - 1), 0, n_dr - 1)
        which.append(np.where(valid_r, dr, n_dr))
    return _na_bias_assemble(blocks, which)


def _na_bias_kernel(blk_ref, o_ref, *, which):
    for t, table in enumerate(which):
        @pl.when(pl.program_id(0) == t)
        def _():
            for i in range(NA_QROWS):
                row = jnp.concatenate([blk_ref[0, int(table[i, j])] for j in range(NA_KROWS)], axis=1)
                o_ref[0, 0, i * GRID_W:(i + 1) * GRID_W, :] = row


def _na_bias_assemble(blocks, which):
    heads, nblk = blocks.shape[:2]
    return pl.pallas_call(
        functools.partial(_na_bias_kernel, which=which),
        out_shape=jax.ShapeDtypeStruct((len(which), heads, NA_QBLK, NA_KBLK), F32),
        grid=(len(which), heads),
        in_specs=[pl.BlockSpec((1, nblk, GRID_W, GRID_W), lambda t, h: (h, 0, 0, 0))],
        out_specs=pl.BlockSpec((1, 1, NA_QBLK, NA_KBLK), lambda t, h: (t, h, 0, 0)),
        compiler_params=_cparams(("arbitrary", "arbitrary")),
        name="na_bias",
    )(blocks)


def _attn_kernel(q_ref, k_ref, v_ref, kc_ref, vc_ref, bias_ref, o_ref, *, rows):
    last = rows // NA_QROWS - 1
    nt = (((1,), (1,)), ((), ()))
    ctx_len = kc_ref.shape[0]
    low_half = lax.broadcasted_iota(jnp.int32, (NA_QBLK, LANES), 1) < HEAD_DIM
    for blk in range(NA_STEP_BLOCKS):
        rb = pl.program_id(1) * NA_STEP_BLOCKS + blk
        ks = jnp.clip(rb * NA_QROWS - NA_WIN_ROWS // 2, 0, rows - NA_KROWS)
        kstart = pl.multiple_of(ks * GRID_W, GRID_W)
        geom = jnp.where(rb == 0, 0, jnp.where(rb == last, 2, 1))
        qrows = slice(blk * NA_QBLK, (blk + 1) * NA_QBLK)
        for pair in range(NA_HEADS * HEAD_DIM // LANES):
            ls = slice(pair * LANES, (pair + 1) * LANES)
            q2 = q_ref[qrows, ls]
            k_all = jnp.concatenate([kc_ref[:, ls], k_ref[pl.ds(kstart, NA_KBLK), ls]], axis=0)
            v_all = jnp.concatenate([vc_ref[:, ls], v_ref[pl.ds(kstart, NA_KBLK), ls]], axis=0)
            outs = []
            for half in range(2):
                qh = jnp.where(low_half == (half == 0), q2, jnp.zeros_like(q2))
                s = lax.dot_general(qh, k_all, nt, preferred_element_type=F32)
                s = jnp.concatenate([s[:, :ctx_len], s[:, ctx_len:] + bias_ref[geom, 2 * pair + half]], axis=1)
                m = jnp.max(s, axis=-1, keepdims=True)
                p = jnp.exp2(s - m)
                l = jnp.sum(p, axis=-1, keepdims=True)
                outs.append(jnp.dot(p.astype(BF16), v_all, preferred_element_type=F32) / l)
            o_ref[qrows, ls] = jnp.where(low_half, outs[0], outs[1]).astype(o_ref.dtype)


def _attention(q, k, v, kc, vc, bias, batch, n, ctx_len):
    rows = n // GRID_W
    rbc = rows // (NA_QROWS * NA_STEP_BLOCKS)
    qblk = NA_QBLK * NA_STEP_BLOCKS
    return pl.pallas_call(
        functools.partial(_attn_kernel, rows=rows),
        out_shape=jax.ShapeDtypeStruct((batch * n, NA_WIDTH), BF16),
        grid=(batch, rbc),
        in_specs=[pl.BlockSpec((qblk, NA_WIDTH), lambda b, rb: (b * rbc + rb, 0)),
                  pl.BlockSpec((n, NA_WIDTH), lambda b, rb: (b, 0)),
                  pl.BlockSpec((n, NA_WIDTH), lambda b, rb: (b, 0)),
                  pl.BlockSpec((ctx_len, NA_WIDTH), lambda b, rb: (b, 0)),
                  pl.BlockSpec((ctx_len, NA_WIDTH), lambda b, rb: (b, 0)),
                  pl.BlockSpec(bias.shape, lambda b, rb: (0, 0, 0, 0), pipeline_mode=pl.Buffered(1))],
        out_specs=pl.BlockSpec((qblk, NA_WIDTH), lambda b, rb: (b * rbc + rb, 0)),
        compiler_params=_cparams(("arbitrary", "arbitrary")),
        name="na_attention",
    )(q, k, v, kc, vc, bias)


SCAN_GROUPS = 4
SCAN_CHUNKS = SCAN_GROUPS * SUBLANES


def _scan_pitch(n):
    p = -(-n // SCAN_CHUNKS)
    while p % 8 != 4:
        p += 1
    return p


NEG_LOG2_E = -LOG2_E


def _lru_coeff_tile(half_xc, zh, half_bias, k, d):
    c = LRU_CHUNK
    t_r = jnp.tanh(zh[:, (2 * d) * c:(2 * d + 1) * c] + half_bias[:, (2 * d) * c:(2 * d + 1) * c])
    t_i = jnp.tanh(zh[:, (2 * d + 1) * c:(2 * d + 2) * c] + half_bias[:, (2 * d + 1) * c:(2 * d + 2) * c])
    neg_log_a = k[d:d + 1, :] * (1.0 + t_r)
    a = jnp.exp2(neg_log_a * NEG_LOG2_E)
    one_minus_a2 = jnp.tanh(neg_log_a) * (a * a + 1.0)
    root = jnp.where(one_minus_a2 > 0.0, one_minus_a2 * lax.rsqrt(one_minus_a2), 0.0)
    return a, root * (half_xc + half_xc * t_i)


def _conv_tile(xpad, t0, w, b, rows):
    acc = b + w[0:1, :] * xpad[pl.ds(t0 + SUBLANES - 2, rows), :]
    acc = acc + w[1:2, :] * xpad[pl.ds(t0 + SUBLANES - 1, rows), :]
    acc = acc + w[2:3, :] * xpad[pl.ds(t0 + SUBLANES, rows), :]
    return acc + w[3:4, :] * xpad[pl.ds(t0 + SUBLANES + 1, rows), :]


SCAN_UNROLL = 4


def _group_rows(j, g, pitch):
    return pl.ds(g * SUBLANES * pitch + j, SUBLANES, stride=pitch)


def _chunk_totals(af_ref, bf_ref, ab_ref, bb_ref, pitch):
    def body(j, carry):
        jb = pitch - 1 - j
        out = []
        for g in range(SCAN_GROUPS):
            pf, hf, pb, hb = carry[4 * g:4 * g + 4]
            af = af_ref[_group_rows(j, g, pitch), :]
            ab = ab_ref[_group_rows(jb, g, pitch), :]
            out += [af * pf, af * hf + bf_ref[_group_rows(j, g, pitch), :],
                    ab * pb, ab * hb + bb_ref[_group_rows(jb, g, pitch), :]]
        return tuple(out)
    one = jnp.ones((SUBLANES, LRU_CHUNK), F32)
    zero = jnp.zeros((SUBLANES, LRU_CHUNK), F32)
    res = lax.fori_loop(0, pitch, body, (one, zero, one, zero) * SCAN_GROUPS, unroll=SCAN_UNROLL)
    fwd = [(res[4 * g], res[4 * g + 1]) for g in range(SCAN_GROUPS)]
    bwd = [(res[4 * g + 2], res[4 * g + 3]) for g in range(SCAN_GROUPS)]
    return fwd, bwd


def _chunk_starts(totals, h0, reverse):
    row = lax.broadcasted_iota(jnp.int32, (SUBLANES, LRU_CHUNK), 0)
    starts = [jnp.zeros((SUBLANES, LRU_CHUNK), F32) for _ in range(SCAN_GROUPS)]
    state = h0
    order = range(SCAN_CHUNKS - 1, -1, -1) if reverse else range(SCAN_CHUNKS)
    for c in order:
        g, s = divmod(c, SUBLANES)
        p_end, h_end = totals[g]
        starts[g] = jnp.where(row == s, state, starts[g])
        state = p_end[s:s + 1, :] * state + h_end[s:s + 1, :]
    return starts, state


def _scan_write(af_ref, bf_ref, hf_ref, ab_ref, bb_ref, hb_ref, starts_f, starts_b, pitch):
    def body(j, carry):
        jb = pitch - 1 - j
        out = []
        for g in range(SCAN_GROUPS):
            hf, hb = carry[2 * g:2 * g + 2]
            hf = af_ref[_group_rows(j, g, pitch), :] * hf + bf_ref[_group_rows(j, g, pitch), :]
            hb = ab_ref[_group_rows(jb, g, pitch), :] * hb + bb_ref[_group_rows(jb, g, pitch), :]
            hf_ref[_group_rows(j, g, pitch), :] = hf
            hb_ref[_group_rows(jb, g, pitch), :] = hb
            out += [hf, hb]
        return tuple(out)
    init = tuple(v for g in range(SCAN_GROUPS) for v in (starts_f[g], starts_b[g]))
    lax.fori_loop(0, pitch, body, init, unroll=SCAN_UNROLL)


def _lru_kernel(x_ref, g_ref, xc_ref, cw_ref, cb_ref, w_ref, gb_ref, lam_ref, o_ref,
                xpad, a0, b0, a1, b1, h0s, h1s, ca0, cb0, ca1, cb1, *, n, ctx_len):
    pitch = _scan_pitch(n)
    cpitch = _scan_pitch(ctx_len)
    cw = cw_ref[...]
    cb = cb_ref[...]
    gbias = gb_ref[0]
    lam = lam_ref[...]
    sp = jnp.maximum(-lam, 0.0) + jnp.log1p(jnp.exp(-jnp.abs(lam)))
    k = (0.5 * LRU_C) * sp
    wcat = w_ref[0]
    zeros8 = jnp.zeros((SUBLANES, LRU_CHUNK), F32)

    def fill_coeffs(src_rows, total, length, trows, a_refs, b_refs):
        for d in range(2):
            a_refs[d][pl.ds(length, total - length), :] = jnp.ones((total - length, LRU_CHUNK), F32)
            b_refs[d][pl.ds(length, total - length), :] = jnp.zeros((total - length, LRU_CHUNK), F32)
        xpad[pl.ds(0, SUBLANES), :] = zeros8
        xpad[pl.ds(SUBLANES + length, SUBLANES), :] = zeros8
        xpad[pl.ds(SUBLANES, length), :] = src_rows

        def tile(t, carry):
            t0 = pl.multiple_of(t * trows, SUBLANES)
            xc = _conv_tile(xpad, t0, cw, cb, trows)
            zh = jnp.dot(xc.astype(BF16), wcat, preferred_element_type=F32)
            half_xc = 0.5 * xc
            for d in range(2):
                a, b = _lru_coeff_tile(half_xc, zh, gbias, k, d)
                a_refs[d][pl.ds(t0, trows), :] = a
                b_refs[d][pl.ds(t0, trows), :] = b
            return carry
        lax.fori_loop(0, length // trows, tile, 0)

    fill_coeffs(xc_ref[0], SCAN_CHUNKS * cpitch, ctx_len, ctx_len, (ca0, ca1), (cb0, cb1))
    zero_state = jnp.zeros((1, LRU_CHUNK), F32)
    fwd, bwd = _chunk_totals(ca0, cb0, ca1, cb1, cpitch)
    _, init_f = _chunk_starts(fwd, zero_state, reverse=False)
    _, init_b = _chunk_starts(bwd, zero_state, reverse=True)

    fill_coeffs(x_ref[0], SCAN_CHUNKS * pitch, n, LRU_TROWS, (a0, a1), (b0, b1))
    fwd, bwd = _chunk_totals(a0, b0, a1, b1, pitch)
    starts_f, _ = _chunk_starts(fwd, init_f, reverse=False)
    starts_b, _ = _chunk_starts(bwd, init_b, reverse=True)
    _scan_write(a0, b0, h0s, a1, b1, h1s, starts_f, starts_b, pitch)

    def out_tile(t, carry):
        t0 = pl.multiple_of(t * LRU_TROWS, SUBLANES)
        y = h0s[pl.ds(t0, LRU_TROWS), :] + h1s[pl.ds(t0, LRU_TROWS), :]
        g = g_ref[0, pl.ds(t0, LRU_TROWS), :]
        gelu = 0.5 * g * (1.0 + jnp.tanh(0.7978845608028654 * (g + 0.044715 * (g * g * g))))
        o_ref[0, pl.ds(t0, LRU_TROWS), :] = (gelu * y).astype(o_ref.dtype)
        return carry
    lax.fori_loop(0, n // LRU_TROWS, out_tile, 0)


def _lru(xb, gb, xb_ctx, conv_w, conv_b, wcat, gbias, lam, batch, n, ctx_len):
    nch = LRU_WIDTH // LRU_CHUNK
    pitch = _scan_pitch(n)
    cpitch = _scan_pitch(ctx_len)
    big = pltpu.VMEM((SCAN_CHUNKS * pitch, LRU_CHUNK), F32)
    small = pltpu.VMEM((SCAN_CHUNKS * cpitch, LRU_CHUNK), F32)
    return pl.pallas_call(
        functools.partial(_lru_kernel, n=n, ctx_len=ctx_len),
        out_shape=jax.ShapeDtypeStruct((nch, batch * n, LRU_CHUNK), BF16),
        grid=(batch, nch),
        in_specs=[pl.BlockSpec((1, n, LRU_CHUNK), lambda b, c: (c, b, 0)),
                  pl.BlockSpec((1, n, LRU_CHUNK), lambda b, c: (c, b, 0)),
                  pl.BlockSpec((1, ctx_len, LRU_CHUNK), lambda b, c: (c, b, 0)),
                  pl.BlockSpec((4, LRU_CHUNK), lambda b, c: (0, c)),
                  pl.BlockSpec((1, LRU_CHUNK), lambda b, c: (0, c)),
                  pl.BlockSpec((1, LRU_CHUNK, 4 * LRU_CHUNK), lambda b, c: (c, 0, 0)),
                  pl.BlockSpec((1, 1, 4 * LRU_CHUNK), lambda b, c: (c, 0, 0)),
                  pl.BlockSpec((2, LRU_CHUNK), lambda b, c: (0, c))],
        out_specs=pl.BlockSpec((1, n, LRU_CHUNK), lambda b, c: (c, b, 0)),
        scratch_shapes=[pltpu.VMEM((n + 2 * SUBLANES, LRU_CHUNK), F32),
                        big, big, big, big, big, big, small, small, small, small],
        compiler_params=_cparams(("arbitrary", "arbitrary")),
        name="rglru",
    )(xb, gb, xb_ctx, conv_w, conv_b, wcat, gbias, lam)


def _lru_gate_weights(w_r, b_r, w_i, b_i):
    nch = LRU_WIDTH // LRU_CHUNK
    bpc = LRU_CHUNK // LRU_BLOCK

    def dense(w):
        wc = w.reshape(nch, bpc, LRU_BLOCK, LRU_BLOCK)
        eye = jnp.eye(bpc, dtype=w.dtype)
        return jnp.einsum("cbij,bd->cbidj", wc, eye).reshape(nch, LRU_CHUNK, LRU_CHUNK)

    wcat = jnp.concatenate([dense(w_r[0]), dense(w_i[0]), dense(w_r[1]), dense(w_i[1])], axis=-1)
    chunk = lambda v: v.reshape(nch, 1, LRU_CHUNK)
    gbias = jnp.concatenate([chunk(b_r[0]), chunk(b_i[0]), chunk(b_r[1]), chunk(b_i[1])], axis=-1)
    return (0.5 * wcat).astype(BF16), (0.5 * gbias).astype(F32)


def _route(s, sel, route_ref):
    srow = [s[e:e + 1, :] for e in range(N_EXPERTS)]
    lrow = [sel[e:e + 1, :] for e in range(N_EXPERTS)]
    gscore = []
    for g in range(N_EXPERT_GROUPS):
        a = lrow[g * EXPERTS_PER_GROUP:(g + 1) * EXPERTS_PER_GROUP]
        best = a[0] + a[1]
        for i, j in ((0, 2), (0, 3), (1, 2), (1, 3), (2, 3)):
            best = jnp.maximum(best, a[i] + a[j])
        gscore.append(best)
    bg = jnp.zeros_like(gscore[0], dtype=jnp.int32)
    bv = gscore[0]
    for g in range(1, N_EXPERT_GROUPS):
        upd = gscore[g] > bv
        bg = jnp.where(upd, g, bg)
        bv = jnp.where(upd, gscore[g], bv)

    def pick(rows_):
        out = []
        for j in range(EXPERTS_PER_GROUP):
            v = rows_[j]
            for g in range(1, N_EXPERT_GROUPS):
                v = jnp.where(bg == g, rows_[g * EXPERTS_PER_GROUP + j], v)
            out.append(v)
        return out
    cand = pick(lrow)
    cs = pick(srow)
    i1 = jnp.zeros_like(bg)
    v1 = cand[0]
    w1 = cs[0]
    for j in range(1, EXPERTS_PER_GROUP):
        upd = cand[j] > v1
        i1 = jnp.where(upd, j, i1)
        v1 = jnp.where(upd, cand[j], v1)
        w1 = jnp.where(upd, cs[j], w1)
    i2 = jnp.full_like(bg, -1)
    v2 = jnp.full_like(v1, -jnp.inf)
    w2 = jnp.zeros_like(w1)
    for j in range(EXPERTS_PER_GROUP):
        upd = (i1 != j) & (cand[j] > v2)
        i2 = jnp.where(upd, j, i2)
        v2 = jnp.where(upd, cand[j], v2)
        w2 = jnp.where(upd, cs[j], w2)
    den = w1 + w2
    g1 = w1 / den
    g2 = w2 / den
    for j in range(EXPERTS_PER_GROUP):
        route_ref[j:j + 1, :] = jnp.where(i1 == j, g1, 0.0) + jnp.where(i2 == j, g2, 0.0)
    route_ref[ROUTE_GID_ROW:ROUTE_GID_ROW + 1, :] = bg.astype(F32)
    pad = SUBLANES - ROUTE_GID_ROW - 1
    route_ref[ROUTE_GID_ROW + 1:, :] = jnp.zeros((pad, bg.shape[1]), F32)


POST_SUBTILE = 512
POST_TILE = 2 * POST_SUBTILE
INPROJ_TILE = 1024


def _post_kernel(*refs, n_parts):
    parts = refs[:n_parts]
    (w_ref, x_ref, gate_ref, shift_ref, scale_ref, gain_ref, rw_ref, rb_ref,
     x1_ref, h2_ref, route_ref) = refs[n_parts:]
    for sub in range(x_ref.shape[0] // POST_SUBTILE):
        rows = pl.ds(sub * POST_SUBTILE, POST_SUBTILE)
        pieces = []
        for p in parts:
            pieces += [p[c, rows, :] for c in range(p.shape[0])] if len(p.shape) == 3 else [p[rows, :]]
        mixed = jnp.concatenate(pieces, axis=-1) if len(pieces) > 1 else pieces[0]
        mix = jnp.dot(mixed, w_ref[...], preferred_element_type=F32)
        x1 = x_ref[rows, :] + gate_ref[0] * mix
        x1_ref[rows, :] = x1
        h2 = _norm_modulate(x1, gain_ref[...], shift_ref[0], scale_ref[0])
        h_hi = h2.astype(BF16)
        h2_ref[rows, :] = h_hi
        logits = jnp.dot(h_hi, rw_ref[...], preferred_element_type=F32)
        s = _sigmoid(logits[:, :LANES].T[:N_EXPERTS, :])
        _route(s, s + rb_ref[...], route_ref.at[:, rows])


def _post_mixer(parts, w, x2d, gate1, shift2, scale2, gain, rw_cat, rbias, tokens_per_batch, tm):
    t, d = x2d.shape
    tpb = tokens_per_batch // tm
    full = lambda i: (0, 0)
    per_b = lambda i: (i // tpb, 0, 0)

    def part_spec(p):
        if p.ndim == 3:
            return pl.BlockSpec((p.shape[0], tm, p.shape[2]), lambda i: (0, i, 0))
        return pl.BlockSpec((tm, p.shape[1]), lambda i: (i, 0))

    return pl.pallas_call(
        functools.partial(_post_kernel, n_parts=len(parts)),
        out_shape=[jax.ShapeDtypeStruct((t, d), F32), jax.ShapeDtypeStruct((t, d), BF16),
                   jax.ShapeDtypeStruct((SUBLANES, t), F32)],
        grid=(t // tm,),
        in_specs=[part_spec(p) for p in parts] + [
                  pl.BlockSpec(w.shape, full),
                  pl.BlockSpec((tm, d), lambda i: (i, 0)),
                  pl.BlockSpec((1, 1, d), per_b),
                  pl.BlockSpec((1, 1, d), per_b),
                  pl.BlockSpec((1, 1, d), per_b),
                  pl.BlockSpec((1, d), full),
                  pl.BlockSpec(rw_cat.shape, full),
                  pl.BlockSpec(rbias.shape, full)],
        out_specs=[pl.BlockSpec((tm, d), lambda i: (i, 0)),
                   pl.BlockSpec((tm, d), lambda i: (i, 0)),
                   pl.BlockSpec((SUBLANES, tm), lambda i: (0, i))],
        compiler_params=_cparams(("arbitrary",)),
        name="post_mixer",
    )(*parts, w, x2d, gate1, shift2, scale2, gain, rw_cat, rbias)


def _moe_layout(t):
    nt = t // MOE_TILE
    grid = -(-(t + N_EXPERT_GROUPS * (ROW_ALIGN - 1) * nt) // MOE_TILE) + N_EXPERT_GROUPS
    return nt, grid


def _moe_tables(gid, t):
    nt, grid = _moe_layout(t)
    ng = N_EXPERT_GROUPS
    per_tile = MOE_TILE // ROW_ALIGN
    onehot = (gid.reshape(nt, MOE_TILE, 1) == jnp.arange(ng, dtype=jnp.int32)).astype(jnp.int32)
    cnt = onehot.sum(axis=1)
    seg = (cnt + ROW_ALIGN - 1) // ROW_ALIGN
    src = jnp.cumsum(seg, axis=1) - seg
    fill = seg.sum(axis=0)
    ntile = (fill + per_tile - 1) // per_tile
    cum = jnp.cumsum(ntile)
    base = (cum - ntile) * per_tile
    dst = jnp.cumsum(seg, axis=0) - seg + base[None, :]
    seg_tab = jnp.concatenate([seg, src, dst], axis=1).reshape(-1).astype(jnp.int32)
    tail = (-fill) % per_tile
    tail_tab = jnp.concatenate([tail, fill + base, cum[-1:]]).astype(jnp.int32)
    i = jnp.arange(grid, dtype=jnp.int32)
    valid = i < cum[-1]
    ie = jnp.minimum(i, cum[-1] - 1)
    g_of = jnp.sum((ie[:, None] >= cum[None, :]).astype(jnp.int32), axis=1)
    return seg_tab, tail_tab, g_of.astype(jnp.int32), valid.astype(jnp.int32)


def _segment_copies(tab_ref, tile, enable, make_copy):
    ng = N_EXPERT_GROUPS
    base = jnp.maximum(tile, 0) * (3 * ng)
    out = []
    for g in range(ng):
        n = tab_ref[base + g]
        src = tab_ref[base + ng + g]
        dst = tab_ref[base + 2 * ng + g]
        for k in range(MOE_SEG_BITS - 1, -1, -1):
            done = (n >> (k + 1)) << (k + 1)
            rows = ROW_ALIGN << k
            s0 = pl.multiple_of((src + done) * ROW_ALIGN, ROW_ALIGN)
            d0 = pl.multiple_of((dst + done) * ROW_ALIGN, ROW_ALIGN)
            out.append((enable & (((n >> k) & 1) == 1), make_copy(s0, d0, rows)))
    return out


def _start_copies(pairs):
    for cond, copies in pairs:
        @pl.when(cond)
        def _():
            for c in copies:
                c.start()


def _wait_copies(pairs):
    for cond, copies in pairs:
        @pl.when(cond)
        def _():
            for c in copies:
                c.wait()


def _split_bf16x3(x):
    hi = x.astype(BF16).astype(F32)
    r1 = x - hi
    mid = r1.astype(BF16).astype(F32)
    lo = (r1 - mid).astype(BF16).astype(F32)
    return hi, mid, lo


def _dispatch_kernel(seg_ref, tail_ref, h_ref, route_ref, tri_ref, slot_ref, hs_ref, cbuf, zbuf, sem, *, nt):
    i = pl.program_id(0)
    tm, d = h_ref.shape
    ng = N_EXPERT_GROUPS
    cur = i % 2

    def seg_copies(tile, enable, buf):
        def seg_copy(s0, d0, rows):
            return (pltpu.make_async_copy(cbuf.at[buf, pl.ds(s0, rows)], hs_ref.at[pl.ds(d0, rows)], sem.at[buf]),)
        return _segment_copies(seg_ref, tile, enable, seg_copy)

    _wait_copies(seg_copies(i - 2, i >= 2, cur))

    route = route_ref[...]
    gid = route[ROUTE_GID_ROW:ROUTE_GID_ROW + 1, :]
    grp = lax.broadcasted_iota(jnp.int32, (SUBLANES, tm), 0).astype(F32)
    onehot = jnp.where(grp == gid, 1.0, 0.0)
    rank = jnp.dot(onehot.astype(BF16), tri_ref[...], preferred_element_type=F32)
    slot = jnp.zeros((1, tm), F32)
    for g in range(ng):
        start = (seg_ref[i * 3 * ng + ng + g] * ROW_ALIGN).astype(F32)
        slot = slot + onehot[g:g + 1, :] * (rank[g:g + 1, :] - 1.0 + start)
    slot_ref[...] = jnp.broadcast_to(slot, (SUBLANES, tm))
    perm = jnp.where(lax.broadcasted_iota(jnp.int32, (MOE_CROWS, tm), 0).astype(F32) == slot, 1.0, 0.0)
    perm = perm.astype(BF16)
    cbuf[cur, :, :d] = jnp.dot(perm, h_ref[...], preferred_element_type=F32).astype(cbuf.dtype)
    parts = jnp.concatenate(list(_split_bf16x3(route)) + [jnp.zeros((LANES - 3 * SUBLANES, tm), F32)], axis=0)
    record = lax.dot_general(perm, parts.astype(BF16), (((1,), (1,)), ((), ())), preferred_element_type=F32)
    cbuf[cur, :, d:] = record.astype(cbuf.dtype)
    _start_copies(seg_copies(i, i >= 0, cur))

    @pl.when(i == pl.num_programs(0) - 1)
    def _():
        _wait_copies(seg_copies(i - 1, i >= 1, 1 - cur))
        _wait_copies(seg_copies(i, i >= 0, cur))
        zbuf[...] = jnp.zeros(zbuf.shape, zbuf.dtype)

        def zero_copy(d0, rows):
            return (pltpu.make_async_copy(zbuf.at[pl.ds(0, rows)], hs_ref.at[pl.ds(d0, rows)], sem.at[0]),)
        pairs = []
        for g in range(ng):
            n = tail_ref[g]
            dst = tail_ref[ng + g]
            for k in range(MOE_TAIL_BITS - 1, -1, -1):
                done = (n >> (k + 1)) << (k + 1)
                d0 = pl.multiple_of((dst + done) * ROW_ALIGN, ROW_ALIGN)
                pairs.append((((n >> k) & 1) == 1, zero_copy(d0, ROW_ALIGN << k)))
        used = tail_ref[2 * ng]
        total = hs_ref.shape[0] // MOE_TILE
        for j in range(total - nt):
            d0 = pl.multiple_of(jnp.minimum(used + j, total - 1) * MOE_TILE, MOE_TILE)
            pairs.append((used + j < total, zero_copy(d0, MOE_TILE)))
        _start_copies(pairs)
        _wait_copies(pairs)


def _dispatch(seg_tab, tail_tab, h2, route, tri):
    t, d = h2.shape
    nt, grid = _moe_layout(t)
    rows = grid * MOE_TILE
    grid_spec = pltpu.PrefetchScalarGridSpec(
        num_scalar_prefetch=2,
        grid=(nt,),
        in_specs=[pl.BlockSpec((MOE_TILE, d), lambda i, *_: (i, 0)),
                  pl.BlockSpec((SUBLANES, MOE_TILE), lambda i, *_: (0, i)),
                  pl.BlockSpec((MOE_TILE, MOE_TILE), lambda i, *_: (0, 0))],
        out_specs=[pl.BlockSpec((SUBLANES, MOE_TILE), lambda i, *_: (0, i)),
                   pl.BlockSpec(memory_space=pl.ANY)],
        scratch_shapes=[pltpu.VMEM((2, MOE_CROWS, d + LANES), BF16), pltpu.VMEM((MOE_TILE, d + LANES), BF16),
                        pltpu.SemaphoreType.DMA((2,))])
    return pl.pallas_call(
        functools.partial(_dispatch_kernel, nt=nt),
        out_shape=[jax.ShapeDtypeStruct((SUBLANES, t), F32),
                   jax.ShapeDtypeStruct((rows, d + LANES), BF16)],
        grid_spec=grid_spec,
        compiler_params=_cparams(("arbitrary",)),
        name="moe_dispatch",
    )(seg_tab, tail_tab, h2, route, tri)


def _ffn_kernel(grp_ref, valid_ref, h_ref, wg_ref, wu_ref, wd_ref, y_ref):
    i = pl.program_id(0)
    d = y_ref.shape[1]

    @pl.when(valid_ref[i] == 0)
    def _():
        y_ref[...] = jnp.zeros(y_ref.shape, y_ref.dtype)

    @pl.when(valid_ref[i] == 1)
    def _():
        h = h_ref[:, :d]
        gates = h_ref[:, d:].astype(F32)
        acts = []
        for j in range(EXPERTS_PER_GROUP):
            a = jnp.dot(h, wg_ref[j], preferred_element_type=F32)
            u = jnp.dot(h, wu_ref[j], preferred_element_type=F32)
            gate = (gates[:, j:j + 1] + gates[:, SUBLANES + j:SUBLANES + j + 1]
                    + gates[:, 2 * SUBLANES + j:2 * SUBLANES + j + 1])
            acts.append(((a * _sigmoid_tanh(a)) * u * gate).astype(BF16))
        wd = wd_ref[...].reshape(EXPERTS_PER_GROUP * D_FF_EXPERT, d)
        y = jnp.dot(jnp.concatenate(acts, axis=1), wd, preferred_element_type=F32)
        y_ref[...] = y.astype(y_ref.dtype)


def _ffn(grp, valid, hs, wg, wu, wd, layer):
    rows, width = hs.shape
    d = width - LANES
    epg = EXPERTS_PER_GROUP
    w_idx = lambda i, grp, valid: (layer * N_EXPERT_GROUPS + grp[i], 0, 0)
    grid_spec = pltpu.PrefetchScalarGridSpec(
        num_scalar_prefetch=2,
        grid=(rows // MOE_TILE,),
        in_specs=[pl.BlockSpec((MOE_TILE, width), lambda i, grp, valid: (i, 0)),
                  pl.BlockSpec((epg, d, D_FF_EXPERT), w_idx),
                  pl.BlockSpec((epg, d, D_FF_EXPERT), w_idx),
                  pl.BlockSpec((epg, D_FF_EXPERT, d), w_idx)],
        out_specs=pl.BlockSpec((MOE_TILE, d), lambda i, grp, valid: (i, 0)))
    return pl.pallas_call(
        _ffn_kernel,
        out_shape=jax.ShapeDtypeStruct((rows, d), BF16),
        grid_spec=grid_spec,
        compiler_params=_cparams(("arbitrary",)),
        name="moe_ffn",
    )(grp, valid, hs, wg, wu, wd)


def _combine_kernel(seg_ref, x1_ref, slot_ref, gate2_ref, ys_ref, *rest, fnet):
    if fnet:
        shift_ref, scale_ref, gain_ref, cs_ref, o_ref, y1_ref, y2_ref, ybuf, xbuf, sem, xsem = rest
    else:
        o_ref, ybuf, xbuf, sem, xsem = rest
    i = pl.program_id(0)
    nt = pl.num_programs(0)
    tm = o_ref.shape[0]
    cur = i % 2

    def seg_copies(tile, enable, buf):
        def seg_copy(s0, d0, rows):
            return (pltpu.make_async_copy(ys_ref.at[pl.ds(d0, rows)], ybuf.at[buf, pl.ds(s0, rows)], sem.at[buf]),)
        return _segment_copies(seg_ref, tile, enable, seg_copy)

    def x1_copy(tile):
        tile = jnp.minimum(tile, nt - 1)
        r0 = pl.multiple_of(tile * tm, tm)
        ring = tile % X1_RING
        return pltpu.make_async_copy(x1_ref.at[pl.ds(r0, tm)], xbuf.at[ring], xsem.at[ring])

    @pl.when(i == 0)
    def _():
        ybuf[...] = jnp.zeros(ybuf.shape, ybuf.dtype)
        _start_copies(seg_copies(i, i == 0, cur))
        for j in range(X1_RING - 1):
            x1_copy(j).start()

    @pl.when(i + X1_RING - 1 < nt)
    def _():
        x1_copy(i + X1_RING - 1).start()

    nxt = jnp.minimum(i + 1, nt - 1)
    _start_copies(seg_copies(nxt, i + 1 < nt, 1 - cur))
    _wait_copies(seg_copies(i, i >= 0, cur))
    x1_copy(i).wait()
    slot = slot_ref[0:1, :]
    perm = jnp.where(lax.broadcasted_iota(jnp.int32, (MOE_CROWS, tm), 0).astype(F32) == slot, 1.0, 0.0)
    y = lax.dot_general(perm.astype(BF16), ybuf[cur], (((0,), (0,)), ((), ())), preferred_element_type=F32)
    x = xbuf[i % X1_RING] + gate2_ref[0] * y
    o_ref[...] = x
    if fnet:
        _fnet_channel_tile(x, shift_ref, scale_ref, gain_ref, cs_ref, y1_ref, y2_ref)


def _combine(seg_tab, x1, slot, gate2, ys, tokens_per_batch, fnet=None):
    t, d = x1.shape
    tpb = tokens_per_batch // MOE_TILE
    per_b = lambda i, *_: (i // tpb, 0, 0)
    full = lambda i, *_: (0, 0)
    assert t // MOE_TILE >= X1_RING - 1
    in_specs = [pl.BlockSpec(memory_space=pl.ANY),
                pl.BlockSpec((SUBLANES, MOE_TILE), lambda i, *_: (0, i)),
                pl.BlockSpec((1, 1, d), per_b),
                pl.BlockSpec(memory_space=pl.ANY)]
    out_shape = [jax.ShapeDtypeStruct((t, d), F32)]
    out_specs = [pl.BlockSpec((MOE_TILE, d), lambda i, *_: (i, 0))]
    args = [seg_tab, x1, slot, gate2, ys]
    if fnet is not None:
        shift, scale, gain, cs = fnet
        gw = d // FNET_GROUPS
        in_specs += [pl.BlockSpec((1, 1, d), per_b), pl.BlockSpec((1, 1, d), per_b), pl.BlockSpec((1, d), full),
                     pl.BlockSpec(cs.shape, full)]
        out_shape += [jax.ShapeDtypeStruct((FNET_GROUPS, t, gw), BF16)] * 2
        out_specs += [pl.BlockSpec((FNET_GROUPS, MOE_TILE, gw), lambda i, *_: (0, i, 0))] * 2
        args += [shift, scale, gain, cs]
    grid_spec = pltpu.PrefetchScalarGridSpec(
        num_scalar_prefetch=1,
        grid=(t // MOE_TILE,),
        in_specs=in_specs,
        out_specs=out_specs,
        scratch_shapes=[pltpu.VMEM((2, MOE_CROWS, d), BF16), pltpu.VMEM((X1_RING, MOE_TILE, d), F32),
                        pltpu.SemaphoreType.DMA((2,)), pltpu.SemaphoreType.DMA((X1_RING,))])
    out = pl.pallas_call(
        functools.partial(_combine_kernel, fnet=fnet is not None),
        out_shape=out_shape,
        grid_spec=grid_spec,
        compiler_params=_cparams(("arbitrary",)),
        name="moe_combine_fnet" if fnet is not None else "moe_combine",
    )(*args)
    return out if fnet is not None else out[0]


def _grouped_moe(h2, route, x1, gate2, wg, wu, wd, layer, tri):
    t = h2.shape[0]
    gid = route[ROUTE_GID_ROW].astype(jnp.int32)
    seg_tab, tail_tab, grp, valid = _moe_tables(gid, t)
    slot, hs = _dispatch(seg_tab, tail_tab, h2, route, tri)
    ys = _ffn(grp, valid, hs, wg, wu, wd, layer)
    return seg_tab, x1, slot, gate2, ys


FFT_J = SUBLANES
FFT_PAIR_ROWS = 2 * FFT_J * (GRID_W // FFT_J)


def _pair_grid_rows(y):
    rows = y.shape[0]
    assert rows % (2 * GRID_W) == 0
    groups = []
    for u in range(rows // (2 * GRID_W)):
        for sb in range(GRID_W // FFT_J):
            for e in range(2):
                start = GRID_W * (2 * u + e) + FFT_J * sb
                groups.append(y[start:start + FFT_J])
    return jnp.concatenate(groups, axis=0)


def _fnet_channel_tile(x, shift_ref, scale_ref, gain_ref, cs_ref, y1_ref, y2_ref):
    h = _norm_modulate(x, gain_ref[...], shift_ref[0], scale_ref[0]).astype(BF16)
    gw = D_MODEL // FNET_GROUPS
    for g in range(FNET_GROUPS):
        y = _pair_grid_rows(jnp.dot(h[:, g * gw:(g + 1) * gw], cs_ref[...], preferred_element_type=F32))
        y1_ref[g] = y[:, :gw].astype(y1_ref.dtype)
        y2_ref[g] = y[:, gw:].astype(y2_ref.dtype)


def _half_rows(m):
    hc = m // 2 + 1
    return hc, -(-FFT_J * hc // ROW_ALIGN) * ROW_ALIGN


def _mirror(lo, hi, hc, m):
    return jnp.concatenate([lo[:FFT_J * hc]] + [hi[FFT_J * c:FFT_J * (c + 1)] for c in range(m - hc, 0, -1)], axis=0)


def _fnet_pos_kernel(y1_ref, y2_ref, lr_ref, ls_ref, cs_ref, sn_ref, o_ref, z1, a_re, a_im, *, n):
    r1 = n // GRID_W
    gw = y1_ref.shape[2]
    hc_r, part_r = _half_rows(r1)
    hc_s, part_s = _half_rows(GRID_W)

    def stage_r(sb, carry):
        s0 = pl.multiple_of(sb * FFT_J, FFT_J)
        p0 = pl.multiple_of(sb * 2 * FFT_J, 2 * FFT_J)
        rhs1 = jnp.concatenate([y1_ref[0, pl.ds(FFT_PAIR_ROWS * u + p0, 2 * FFT_J), :] for u in range(r1 // 2)], axis=0)
        rhs2 = jnp.concatenate([y2_ref[0, pl.ds(FFT_PAIR_ROWS * u + p0, 2 * FFT_J), :] for u in range(r1 // 2)], axis=0)
        p = jnp.dot(lr_ref[...], rhs1, preferred_element_type=F32)
        q = jnp.dot(lr_ref[...], rhs2, preferred_element_type=F32)
        pc, ps = p[:part_r], p[part_r:]
        qc, qs = q[:part_r], q[part_r:]
        re = _mirror(pc - qs, pc + qs, hc_r, r1)
        nim = _mirror(qc + ps, qc - ps, hc_r, r1)
        cs = jnp.concatenate([cs_ref[sb]] * (gw // LANES), axis=1)
        sn = jnp.concatenate([sn_ref[sb]] * (gw // LANES), axis=1)
        tre = re * cs - nim * sn
        tnim = re * sn + nim * cs
        for c in range(r1):
            a_re[pl.ds(GRID_W * c + s0, FFT_J), :] = tre[FFT_J * c:FFT_J * (c + 1)]
            a_im[pl.ds(GRID_W * c + s0, FFT_J), :] = tnim[FFT_J * c:FFT_J * (c + 1)]
        return carry
    lax.fori_loop(0, GRID_W // FFT_J, stage_r, 0)

    cblk = FFT_J * GRID_W

    def stage_s(cb, carry):
        c0 = pl.multiple_of(cb * cblk, cblk)
        u = jnp.dot(ls_ref[0], a_re[pl.ds(c0, cblk), :].astype(BF16), preferred_element_type=F32)
        v = jnp.dot(ls_ref[1], a_im[pl.ds(c0, cblk), :].astype(BF16), preferred_element_type=F32)
        lo = u - v
        hi = u + v
        k0 = pl.multiple_of(cb * FFT_J, FFT_J)
        for d in range(hc_s):
            z1[pl.ds(r1 * d + k0, FFT_J), :] = lo[FFT_J * d:FFT_J * (d + 1)]
        for d in range(1, GRID_W - hc_s + 1):
            z1[pl.ds(r1 * (GRID_W - d) + k0, FFT_J), :] = hi[FFT_J * d:FFT_J * (d + 1)]
        return carry
    lax.fori_loop(0, r1 // FFT_J, stage_s, 0)
    o_ref[0] = z1[...].astype(o_ref.dtype)


def _fnet_position(y1, y2, lr, ls, tw_cos, tw_sin, batch, n):
    groups, t, gw = y1.shape
    full2 = lambda b, g: (0, 0)
    full3 = lambda b, g: (0, 0, 0)
    scratch = pltpu.VMEM((n, gw), F32)
    return pl.pallas_call(
        functools.partial(_fnet_pos_kernel, n=n),
        out_shape=jax.ShapeDtypeStruct((groups, t, gw), BF16),
        grid=(batch, groups),
        in_specs=[pl.BlockSpec((1, n, gw), lambda b, g: (g, b, 0)),
                  pl.BlockSpec((1, n, gw), lambda b, g: (g, b, 0)),
                  pl.BlockSpec(lr.shape, full2),
                  pl.BlockSpec(ls.shape, full3),
                  pl.BlockSpec(tw_cos.shape, full3),
                  pl.BlockSpec(tw_sin.shape, full3)],
        out_specs=pl.BlockSpec((1, n, gw), lambda b, g: (g, b, 0)),
        scratch_shapes=[scratch, scratch, scratch],
        compiler_params=_cparams(("arbitrary", "arbitrary")),
        name="fnet_position",
    )(y1, y2, lr, ls, tw_cos, tw_sin)


def _dft_tables(n):
    assert n % (GRID_W * FFT_J) == 0
    gw = D_MODEL // FNET_GROUPS
    j = np.arange(gw)
    ang = 2.0 * np.pi * ((j[:, None] * j[None, :]) % gw) / gw
    cs = np.concatenate([np.cos(ang), np.sin(ang)], axis=1) / np.sqrt(gw)
    r1 = n // GRID_W
    assert r1 % 2 == 0
    eye = np.eye(FFT_J)
    scale = float(n) ** -0.25
    a = np.arange(r1)
    hc_r, part_r = _half_rows(r1)
    ang_r = 2.0 * np.pi * ((a[:hc_r, None] * a[None, :]) % r1) / r1
    lr = np.zeros((2 * part_r, FFT_J * r1))
    lr[:FFT_J * hc_r] = np.kron(np.cos(ang_r), eye) * scale
    lr[part_r:part_r + FFT_J * hc_r] = np.kron(np.sin(ang_r), eye) * scale
    s = np.arange(GRID_W)
    hc_s, part_s = _half_rows(GRID_W)
    ang_s = 2.0 * np.pi * ((s[:hc_s, None] * s[None, :]) % GRID_W) / GRID_W
    ls = np.zeros((2, part_s, FFT_J * GRID_W))
    ls[0, :FFT_J * hc_s] = np.einsum("ds,cC->dcCs", np.cos(ang_s), eye).reshape(FFT_J * hc_s, FFT_J * GRID_W) * scale
    ls[1, :FFT_J * hc_s] = np.einsum("ds,cC->dcCs", np.sin(ang_s), eye).reshape(FFT_J * hc_s, FFT_J * GRID_W) * scale
    sb = np.arange(GRID_W // FFT_J)
    s_of = sb[:, None, None] * FFT_J + np.arange(FFT_J)[None, None, :]
    ang_t = 2.0 * np.pi * ((s_of * a[None, :, None]) % n) / n
    ang_t = ang_t.reshape(len(sb), r1 * FFT_J, 1)
    tw_cos = jnp.broadcast_to(jnp.asarray(np.cos(ang_t), F32), (len(sb), r1 * FFT_J, LANES))
    tw_sin = jnp.broadcast_to(jnp.asarray(np.sin(ang_t), F32), (len(sb), r1 * FFT_J, LANES))
    return jnp.asarray(cs, BF16), jnp.asarray(lr, BF16), jnp.asarray(ls, BF16), tw_cos, tw_sin


def kernel(x, c, ctx, c_ctx, ada_w, ada_b, norm_mix, norm_ffn, mix_w_in, mix_w_out, na_q_norm, na_k_norm, na_rpb,
           lru_conv_w, lru_conv_b, lru_gate_r_w, lru_gate_r_b, lru_gate_i_w, lru_gate_i_b, lru_lambda,
           fnet_w_out, router_w, router_bias, moe_w_gate, moe_w_up, moe_w_down):
    batch, n, d = x.shape
    ctx_len = ctx.shape[1]
    depth = ada_w.shape[0]
    rows = n // GRID_W
    assert d == D_MODEL and n % (GRID_W * NA_QROWS) == 0 and rows >= 4 * NA_QROWS
    assert n % MOE_TILE == 0 and n % INPROJ_TILE == 0 and n % POST_TILE == 0 and ctx_len % LANES == 0
    t = batch * n
    tri = jnp.asarray(np.triu(np.ones((MOE_TILE, MOE_TILE))), BF16)

    r_pad = -(-(batch + 1) // SUBLANES) * SUBLANES
    c_rows = jnp.concatenate([c, c_ctx[None, :], jnp.zeros((r_pad - batch - 1, d), c.dtype)], axis=0)
    mod = _modulation(c_rows, ada_w, ada_b)

    def mod_slices(layer):
        m = mod[layer, :batch].reshape(batch, 1, 6, d)
        return [m[:, :, i, :] for i in range(6)]

    rw_cat = jnp.pad(router_w.astype(F32), ((0, 0), (0, 2 * LANES - N_EXPERTS))).astype(BF16)
    rbias = router_bias.reshape(N_EXPERTS, 1).astype(F32)
    x2d = x.reshape(t, d)
    ctx2d = ctx.reshape(batch * ctx_len, d)

    pending = None
    moe_w = None
    for layer in range(depth):
        li = layer // 2
        shift1, scale1, gate1, shift2, scale2, gate2 = mod_slices(layer)
        gain_mix = norm_mix[layer].reshape(1, d)
        gain_ffn = norm_ffn[layer].reshape(1, d)
        if layer % 2 == 0:
            if pending is not None:
                x2d = _combine(*pending, n)
            w_in = mix_w_in[li].astype(BF16)
            ind = jnp.asarray(np.kron(np.eye(NA_HEADS // 2), np.ones((HEAD_DIM, HEAD_DIM))), BF16)
            qg = (jnp.tile(na_q_norm[li], NA_HEADS) * (HEAD_DIM ** -0.5 * LOG2_E)).reshape(1, NA_WIDTH).astype(F32)
            kg = jnp.tile(na_k_norm[li], NA_HEADS).reshape(1, NA_WIDTH).astype(F32)
            side = () if moe_w is not None else (moe_w_gate.reshape((-1,) + moe_w_gate.shape[2:]).astype(F32),
                                                 moe_w_up.reshape((-1,) + moe_w_up.shape[2:]).astype(F32),
                                                 moe_w_down.reshape((-1,) + moe_w_down.shape[2:]).astype(F32))
            q, k, v, xb, gb, *rounded = _inproj(x2d, shift1, scale1, gain_mix, w_in, ind, qg, kg,
                                                ("q", "k", "v", "x", "g"), n, INPROJ_TILE, side=side)
            if rounded:
                moe_w = rounded
            mctx = mod[layer, batch, :2 * d]
            shift_c = jnp.broadcast_to(mctx[:d], (batch, 1, d))
            scale_c = jnp.broadcast_to(mctx[d:], (batch, 1, d))
            k_c, v_c, xb_c = _inproj(ctx2d, shift_c, scale_c, gain_mix, w_in[:, NA_WIDTH:4 * NA_WIDTH], ind, qg, kg,
                                     ("k", "v", "x"), ctx_len, ctx_len)
            bias = _na_bias_tables(na_rpb[li], rows)
            attn = _attention(q, k, v, k_c, v_c, bias, batch, n, ctx_len)
            wcat, gbias = _lru_gate_weights(lru_gate_r_w[li], lru_gate_r_b[li], lru_gate_i_w[li], lru_gate_i_b[li])
            lru = _lru(xb, gb, xb_c, lru_conv_w[li].astype(F32), lru_conv_b[li].reshape(1, LRU_WIDTH).astype(F32),
                       wcat, gbias, lru_lambda[li].astype(F32), batch, n, ctx_len)
            parts, w_out = [attn, lru], mix_w_out[li].astype(BF16)
        else:
            cs, lr, ls, tw_cos, tw_sin = _dft_tables(n)
            x2d, y1, y2 = _combine(*pending, n, fnet=(shift1, scale1, gain_mix, cs))
            parts, w_out = [_fnet_position(y1, y2, lr, ls, tw_cos, tw_sin, batch, n)], fnet_w_out[li].astype(BF16)
        x1, h2, route = _post_mixer(parts, w_out, x2d, gate1, shift2, scale2, gain_ffn, rw_cat, rbias, n,
                                    POST_TILE)
        pending = _grouped_moe(h2, route, x1, gate2, *moe_w, layer, tri)
    return _combine(*pending, n).reshape(batch, n, d)
```

```python
import functools

import numpy as np
import jax
import jax.numpy as jnp
from jax import lax
from jax.experimental import pallas as pl
from jax.experimental.pallas import tpu as pltpu

F32 = jnp.float32
BF16 = jnp.bfloat16
HIGHEST = lax.Precision.HIGHEST

D_MODEL = 1024
GRID_W = 64
HEAD_DIM = 64
NA_HEADS = 8
NA_WIDTH = NA_HEADS * HEAD_DIM
NA_WIN_ROWS = 8
NA_WIN_COLS = 16
LRU_WIDTH = 512
LRU_BLOCK = 64
LRU_C = 8.0
FNET_GROUPS = 4
N_EXPERTS = 16
EXPERTS_PER_GROUP = 4
N_EXPERT_GROUPS = 4
D_FF_EXPERT = 512
RMS_EPS = 1e-6
MASK_VALUE = -1e30
LOG2_E = 1.4426950408889634

V7X_VMEM_LIMIT_BYTES = 56 * 1024 * 1024
LANES = 128
SUBLANES = 8

NA_QROWS = 4
NA_KROWS = NA_QROWS + NA_WIN_ROWS - 1
NA_QBLK = NA_QROWS * GRID_W
NA_KBLK = NA_KROWS * GRID_W
NA_STEP_BLOCKS = 2

LRU_CHUNK = LANES
LRU_TROWS = 512

ROUTE_GID_ROW = EXPERTS_PER_GROUP
MOE_TILE = 512
ROW_ALIGN = 16
MOE_CROWS = MOE_TILE + N_EXPERT_GROUPS * ROW_ALIGN
MOE_SEG_BITS = (MOE_TILE // ROW_ALIGN).bit_length()
X1_RING = 3
MOE_TAIL_BITS = (MOE_TILE // ROW_ALIGN - 1).bit_length()


def _sigmoid(x):
    return 1.0 / (1.0 + jnp.exp(-x))


def _sigmoid_tanh(x):
    return 0.5 + 0.5 * jnp.tanh(0.5 * x)


def _cparams(sem, vmem=V7X_VMEM_LIMIT_BYTES):
    return pltpu.CompilerParams(dimension_semantics=sem, vmem_limit_bytes=vmem)


def _mod_kernel(c_ref, w_ref, b_ref, o_ref):
    c = c_ref[...]
    s = c * _sigmoid(c)
    o_ref[0] = jnp.dot(s.astype(BF16), w_ref[0].astype(BF16), preferred_element_type=F32) + b_ref[0]


def _modulation(c_rows, ada_w, ada_b):
    depth, d, n6 = ada_w.shape
    r = c_rows.shape[0]
    tn = n6 // 2
    return pl.pallas_call(
        _mod_kernel,
        out_shape=jax.ShapeDtypeStruct((depth, r, n6), F32),
        grid=(depth, n6 // tn),
        in_specs=[pl.BlockSpec((r, d), lambda l, j: (0, 0)),
                  pl.BlockSpec((1, d, tn), lambda l, j: (l, 0, j)),
                  pl.BlockSpec((1, 1, tn), lambda l, j: (l, 0, j))],
        out_specs=pl.BlockSpec((1, r, tn), lambda l, j: (l, 0, j)),
        compiler_params=_cparams(("arbitrary", "arbitrary")),
        name="adaln_mod",
    )(c_rows, ada_w, ada_b.reshape(depth, 1, n6))


def _norm_modulate(x, gain, shift, scale):
    ms = jnp.mean(x * x, axis=-1, keepdims=True)
    y = x * lax.rsqrt(ms + RMS_EPS) * gain
    return y * (1.0 + scale) + shift


def _inproj_kernel(x_ref, shift_ref, scale_ref, gain_ref, w_ref, ind_ref, qg_ref, kg_ref, *rest, segs, n_side):
    side_in, out_refs, side_out = rest[:n_side], rest[n_side:n_side + len(segs)], rest[n_side + len(segs):]
    for src, dst in zip(side_in, side_out):
        dst[...] = src[...].astype(dst.dtype)
    h = _norm_modulate(x_ref[...], gain_ref[...], shift_ref[0], scale_ref[0]).astype(BF16)
    for s, (kind, o_ref) in enumerate(zip(segs, out_refs)):
        z = jnp.dot(h, w_ref[:, s * NA_WIDTH:(s + 1) * NA_WIDTH], preferred_element_type=F32)
        if kind in ("q", "k"):
            zz = (z * z).astype(BF16)
            hw = ind_ref.shape[0]
            ms = jnp.concatenate([jnp.dot(zz[:, i * hw:(i + 1) * hw], ind_ref[...], preferred_element_type=F32)
                                  for i in range(NA_WIDTH // hw)], axis=1) * (1.0 / HEAD_DIM)
            g = qg_ref[...] if kind == "q" else kg_ref[...]
            z = z * lax.rsqrt(ms + RMS_EPS) * g
        if kind in ("x", "g"):
            for c in range(LRU_WIDTH // LRU_CHUNK):
                o_ref[c] = z[:, c * LRU_CHUNK:(c + 1) * LRU_CHUNK].astype(o_ref.dtype)
        else:
            o_ref[...] = z.astype(o_ref.dtype)


def _inproj(x2d, shift, scale, gain, w, ind, qg, kg, segs, tokens_per_batch, tm, side=()):
    t, d = x2d.shape
    tpb = tokens_per_batch // tm
    dt = {"q": BF16, "k": BF16, "v": BF16, "x": F32, "g": F32}
    full = lambda i: (0, 0)
    nch = LRU_WIDTH // LRU_CHUNK
    steps = t // tm
    side_specs, side_shapes = [], []
    for arr in side:
        items = arr.shape[0]
        share = -(-items // steps)
        assert items % share == 0
        last = items // share - 1
        side_specs.append(pl.BlockSpec((share,) + arr.shape[1:], lambda i, last=last: (jnp.minimum(i, last), 0, 0)))
        side_shapes.append(jax.ShapeDtypeStruct(arr.shape, BF16))

    def out_shape(kind):
        shape = (nch, t, LRU_CHUNK) if kind in ("x", "g") else (t, NA_WIDTH)
        return jax.ShapeDtypeStruct(shape, dt[kind])

    def out_spec(kind):
        if kind in ("x", "g"):
            return pl.BlockSpec((nch, tm, LRU_CHUNK), lambda i: (0, i, 0))
        return pl.BlockSpec((tm, NA_WIDTH), lambda i: (i, 0))

    return pl.pallas_call(
        functools.partial(_inproj_kernel, segs=segs, n_side=len(side)),
        out_shape=[out_shape(k) for k in segs] + side_shapes,
        grid=(steps,),
        in_specs=[pl.BlockSpec((tm, d), lambda i: (i, 0)),
                  pl.BlockSpec((1, 1, d), lambda i: (i // tpb, 0, 0)),
                  pl.BlockSpec((1, 1, d), lambda i: (i // tpb, 0, 0)),
                  pl.BlockSpec((1, d), full),
                  pl.BlockSpec(w.shape, full),
                  pl.BlockSpec(ind.shape, full),
                  pl.BlockSpec((1, NA_WIDTH), full),
                  pl.BlockSpec((1, NA_WIDTH), full)] + side_specs,
        out_specs=[out_spec(k) for k in segs] + side_specs,
        compiler_params=_cparams(("arbitrary",)),
        name="inproj_" + "".join(segs),
    )(x2d, shift, scale, gain, w, ind, qg, kg, *side)


def _na_bias_tables(rpb, rows):
    kr = NA_WIN_ROWS
    rb_count = rows // NA_QROWS
    cq = np.arange(GRID_W)
    ck = np.arange(GRID_W)
    col_start = np.clip(cq - NA_WIN_COLS // 2, 0, GRID_W - NA_WIN_COLS)
    valid_c = (ck[None, :] >= col_start[:, None]) & (ck[None, :] < col_start[:, None] + NA_WIN_COLS)
    dc = np.clip(ck[None, :] - cq[:, None], 1 - NA_WIN_COLS, NA_WIN_COLS - 1) + (NA_WIN_COLS - 1)
    n_dr, n_dc = 2 * NA_WIN_ROWS - 1, 2 * NA_WIN_COLS - 1
    sel_c = (dc[:, :, None] == np.arange(n_dc)) & valid_c[:, :, None]
    blocks = jnp.einsum("hrc,qkc->hrqk", rpb.astype(F32), jnp.asarray(sel_c, F32), precision=HIGHEST)
    blocks = blocks + jnp.asarray(np.where(valid_c, 0.0, MASK_VALUE), F32)
    blocks = jnp.concatenate([blocks, jnp.full((NA_HEADS, 1, GRID_W, GRID_W), MASK_VALUE, F32)], axis=1)
    blocks = blocks * LOG2_E
    which = []
    for rb in (0, 1, rb_count - 1):
        r = rb * NA_QROWS + np.arange(NA_QROWS)
        ks = int(np.clip(rb * NA_QROWS - kr // 2, 0, rows - NA_KROWS))
        key_r = ks + np.arange(NA_KROWS)
        row_start = np.clip(r - kr // 2, 0, rows - kr)
        valid_r = (key_r[None, :] >= row_start[:, None]) & (key_r[None, :] < row_start[:, None] + kr)
        dr = np.clip(key_r[None, :] - r[:, None] + (NA_WIN_ROWS - 1), 0, n_dr - 1)
        which.append(np.where(valid_r, dr, n_dr))
    return _na_bias_assemble(blocks, which)


def _na_bias_kernel(blk_ref, o_ref, *, which):
    for t, table in enumerate(which):
        @pl.when(pl.program_id(0) == t)
        def _():
            for i in range(NA_QROWS):
                row = jnp.concatenate([blk_ref[0, int(table[i, j])] for j in range(NA_KROWS)], axis=1)
                o_ref[0, 0, i * GRID_W:(i + 1) * GRID_W, :] = row


def _na_bias_assemble(blocks, which):
    heads, nblk = blocks.shape[:2]
    return pl.pallas_call(
        functools.partial(_na_bias_kernel, which=which),
        out_shape=jax.ShapeDtypeStruct((len(which), heads, NA_QBLK, NA_KBLK), F32),
        grid=(len(which), heads),
        in_specs=[pl.BlockSpec((1, nblk, GRID_W, GRID_W), lambda t, h: (h, 0, 0, 0))],
        out_specs=pl.BlockSpec((1, 1, NA_QBLK, NA_KBLK), lambda t, h: (t, h, 0, 0)),
        compiler_params=_cparams(("arbitrary", "arbitrary")),
        name="na_bias",
    )(blocks)


def _attn_kernel(q_ref, k_ref, v_ref, kc_ref, vc_ref, bias_ref, o_ref, *, rows):
    last = rows // NA_QROWS - 1
    nt = (((1,), (1,)), ((), ()))
    ctx_len = kc_ref.shape[0]
    low_half = lax.broadcasted_iota(jnp.int32, (NA_QBLK, LANES), 1) < HEAD_DIM
    for blk in range(NA_STEP_BLOCKS):
        rb = pl.program_id(1) * NA_STEP_BLOCKS + blk
        ks = jnp.clip(rb * NA_QROWS - NA_WIN_ROWS // 2, 0, rows - NA_KROWS)
        kstart = pl.multiple_of(ks * GRID_W, GRID_W)
        geom = jnp.where(rb == 0, 0, jnp.where(rb == last, 2, 1))
        qrows = slice(blk * NA_QBLK, (blk + 1) * NA_QBLK)
        for pair in range(NA_HEADS * HEAD_DIM // LANES):
            ls = slice(pair * LANES, (pair + 1) * LANES)
            q2 = q_ref[qrows, ls]
            k_all = jnp.concatenate([kc_ref[:, ls], k_ref[pl.ds(kstart, NA_KBLK), ls]], axis=0)
            v_all = jnp.concatenate([vc_ref[:, ls], v_ref[pl.ds(kstart, NA_KBLK), ls]], axis=0)
            outs = []
            for half in range(2):
                qh = jnp.where(low_half == (half == 0), q2, jnp.zeros_like(q2))
                s = lax.dot_general(qh, k_all, nt, preferred_element_type=F32)
                s = jnp.concatenate([s[:, :ctx_len], s[:, ctx_len:] + bias_ref[geom, 2 * pair + half]], axis=1)
                m = jnp.max(s, axis=-1, keepdims=True)
                p = jnp.exp2(s - m)
                l = jnp.sum(p, axis=-1, keepdims=True)
                outs.append(jnp.dot(p.astype(BF16), v_all, preferred_element_type=F32) / l)
            o_ref[qrows, ls] = jnp.where(low_half, outs[0], outs[1]).astype(o_ref.dtype)


def _attention(q, k, v, kc, vc, bias, batch, n, ctx_len):
    rows = n // GRID_W
    rbc = rows // (NA_QROWS * NA_STEP_BLOCKS)
    qblk = NA_QBLK * NA_STEP_BLOCKS
    return pl.pallas_call(
        functools.partial(_attn_kernel, rows=rows),
        out_shape=jax.ShapeDtypeStruct((batch * n, NA_WIDTH), BF16),
        grid=(batch, rbc),
        in_specs=[pl.BlockSpec((qblk, NA_WIDTH), lambda b, rb: (b * rbc + rb, 0)),
                  pl.BlockSpec((n, NA_WIDTH), lambda b, rb: (b, 0)),
                  pl.BlockSpec((n, NA_WIDTH), lambda b, rb: (b, 0)),
                  pl.BlockSpec((ctx_len, NA_WIDTH), lambda b, rb: (b, 0)),
                  pl.BlockSpec((ctx_len, NA_WIDTH), lambda b, rb: (b, 0)),
                  pl.BlockSpec(bias.shape, lambda b, rb: (0, 0, 0, 0), pipeline_mode=pl.Buffered(1))],
        out_specs=pl.BlockSpec((qblk, NA_WIDTH), lambda b, rb: (b * rbc + rb, 0)),
        compiler_params=_cparams(("arbitrary", "arbitrary")),
        name="na_attention",
    )(q, k, v, kc, vc, bias)


SCAN_GROUPS = 4
SCAN_CHUNKS = SCAN_GROUPS * SUBLANES


def _scan_pitch(n):
    p = -(-n // SCAN_CHUNKS)
    while p % 8 != 4:
        p += 1
    return p


NEG_LOG2_E = -LOG2_E


def _lru_coeff_tile(half_xc, zh, half_bias, k, d):
    c = LRU_CHUNK
    t_r = jnp.tanh(zh[:, (2 * d) * c:(2 * d + 1) * c] + half_bias[:, (2 * d) * c:(2 * d + 1) * c])
    t_i = jnp.tanh(zh[:, (2 * d + 1) * c:(2 * d + 2) * c] + half_bias[:, (2 * d + 1) * c:(2 * d + 2) * c])
    neg_log_a = k[d:d + 1, :] * (1.0 + t_r)
    a = jnp.exp2(neg_log_a * NEG_LOG2_E)
    one_minus_a2 = jnp.tanh(neg_log_a) * (a * a + 1.0)
    root = jnp.where(one_minus_a2 > 0.0, one_minus_a2 * lax.rsqrt(one_minus_a2), 0.0)
    return a, root * (half_xc + half_xc * t_i)


def _conv_tile(xpad, t0, w, b, rows):
    acc = b + w[0:1, :] * xpad[pl.ds(t0 + SUBLANES - 2, rows), :]
    acc = acc + w[1:2, :] * xpad[pl.ds(t0 + SUBLANES - 1, rows), :]
    acc = acc + w[2:3, :] * xpad[pl.ds(t0 + SUBLANES, rows), :]
    return acc + w[3:4, :] * xpad[pl.ds(t0 + SUBLANES + 1, rows), :]


SCAN_UNROLL = 4


def _group_rows(j, g, pitch):
    return pl.ds(g * SUBLANES * pitch + j, SUBLANES, stride=pitch)


def _chunk_totals(af_ref, bf_ref, ab_ref, bb_ref, pitch):
    def body(j, carry):
        jb = pitch - 1 - j
        out = []
        for g in range(SCAN_GROUPS):
            pf, hf, pb, hb = carry[4 * g:4 * g + 4]
            af = af_ref[_group_rows(j, g, pitch), :]
            ab = ab_ref[_group_rows(jb, g, pitch), :]
            out += [af * pf, af * hf + bf_ref[_group_rows(j, g, pitch), :],
                    ab * pb, ab * hb + bb_ref[_group_rows(jb, g, pitch), :]]
        return tuple(out)
    one = jnp.ones((SUBLANES, LRU_CHUNK), F32)
    zero = jnp.zeros((SUBLANES, LRU_CHUNK), F32)
    res = lax.fori_loop(0, pitch, body, (one, zero, one, zero) * SCAN_GROUPS, unroll=SCAN_UNROLL)
    fwd = [(res[4 * g], res[4 * g + 1]) for g in range(SCAN_GROUPS)]
    bwd = [(res[4 * g + 2], res[4 * g + 3]) for g in range(SCAN_GROUPS)]
    return fwd, bwd


def _chunk_starts(totals, h0, reverse):
    row = lax.broadcasted_iota(jnp.int32, (SUBLANES, LRU_CHUNK), 0)
    starts = [jnp.zeros((SUBLANES, LRU_CHUNK), F32) for _ in range(SCAN_GROUPS)]
    state = h0
    order = range(SCAN_CHUNKS - 1, -1, -1) if reverse else range(SCAN_CHUNKS)
    for c in order:
        g, s = divmod(c, SUBLANES)
        p_end, h_end = totals[g]
        starts[g] = jnp.where(row == s, state, starts[g])
        state = p_end[s:s + 1, :] * state + h_end[s:s + 1, :]
    return starts, state


def _scan_write(af_ref, bf_ref, hf_ref, ab_ref, bb_ref, hb_ref, starts_f, starts_b, pitch):
    def body(j, carry):
        jb = pitch - 1 - j
        out = []
        for g in range(SCAN_GROUPS):
            hf, hb = carry[2 * g:2 * g + 2]
            hf = af_ref[_group_rows(j, g, pitch), :] * hf + bf_ref[_group_rows(j, g, pitch), :]
            hb = ab_ref[_group_rows(jb, g, pitch), :] * hb + bb_ref[_group_rows(jb, g, pitch), :]
            hf_ref[_group_rows(j, g, pitch), :] = hf
            hb_ref[_group_rows(jb, g, pitch), :] = hb
            out += [hf, hb]
        return tuple(out)
    init = tuple(v for g in range(SCAN_GROUPS) for v in (starts_f[g], starts_b[g]))
    lax.fori_loop(0, pitch, body, init, unroll=SCAN_UNROLL)


def _lru_kernel(x_ref, g_ref, xc_ref, cw_ref, cb_ref, w_ref, gb_ref, lam_ref, o_ref,
                xpad, a0, b0, a1, b1, h0s, h1s, ca0, cb0, ca1, cb1, *, n, ctx_len):
    pitch = _scan_pitch(n)
    cpitch = _scan_pitch(ctx_len)
    cw = cw_ref[...]
    cb = cb_ref[...]
    gbias = gb_ref[0]
    lam = lam_ref[...]
    sp = jnp.maximum(-lam, 0.0) + jnp.log1p(jnp.exp(-jnp.abs(lam)))
    k = (0.5 * LRU_C) * sp
    wcat = w_ref[0]
    zeros8 = jnp.zeros((SUBLANES, LRU_CHUNK), F32)

    def fill_coeffs(src_rows, total, length, trows, a_refs, b_refs):
        for d in range(2):
            a_refs[d][pl.ds(length, total - length), :] = jnp.ones((total - length, LRU_CHUNK), F32)
            b_refs[d][pl.ds(length, total - length), :] = jnp.zeros((total - length, LRU_CHUNK), F32)
        xpad[pl.ds(0, SUBLANES), :] = zeros8
        xpad[pl.ds(SUBLANES + length, SUBLANES), :] = zeros8
        xpad[pl.ds(SUBLANES, length), :] = src_rows

        def tile(t, carry):
            t0 = pl.multiple_of(t * trows, SUBLANES)
            xc = _conv_tile(xpad, t0, cw, cb, trows)
            zh = jnp.dot(xc.astype(BF16), wcat, preferred_element_type=F32)
            half_xc = 0.5 * xc
            for d in range(2):
                a, b = _lru_coeff_tile(half_xc, zh, gbias, k, d)
                a_refs[d][pl.ds(t0, trows), :] = a
                b_refs[d][pl.ds(t0, trows), :] = b
            return carry
        lax.fori_loop(0, length // trows, tile, 0)

    fill_coeffs(xc_ref[0], SCAN_CHUNKS * cpitch, ctx_len, ctx_len, (ca0, ca1), (cb0, cb1))
    zero_state = jnp.zeros((1, LRU_CHUNK), F32)
    fwd, bwd = _chunk_totals(ca0, cb0, ca1, cb1, cpitch)
    _, init_f = _chunk_starts(fwd, zero_state, reverse=False)
    _, init_b = _chunk_starts(bwd, zero_state, reverse=True)

    fill_coeffs(x_ref[0], SCAN_CHUNKS * pitch, n, LRU_TROWS, (a0, a1), (b0, b1))
    fwd, bwd = _chunk_totals(a0, b0, a1, b1, pitch)
    starts_f, _ = _chunk_starts(fwd, init_f, reverse=False)
    starts_b, _ = _chunk_starts(bwd, init_b, reverse=True)
    _scan_write(a0, b0, h0s, a1, b1, h1s, starts_f, starts_b, pitch)

    def out_tile(t, carry):
        t0 = pl.multiple_of(t * LRU_TROWS, SUBLANES)
        y = h0s[pl.ds(t0, LRU_TROWS), :] + h1s[pl.ds(t0, LRU_TROWS), :]
        g = g_ref[0, pl.ds(t0, LRU_TROWS), :]
        gelu = 0.5 * g * (1.0 + jnp.tanh(0.7978845608028654 * (g + 0.044715 * (g * g * g))))
        o_ref[0, pl.ds(t0, LRU_TROWS), :] = (gelu * y).astype(o_ref.dtype)
        return carry
    lax.fori_loop(0, n // LRU_TROWS, out_tile, 0)


def _lru(xb, gb, xb_ctx, conv_w, conv_b, wcat, gbias, lam, batch, n, ctx_len):
    nch = LRU_WIDTH // LRU_CHUNK
    pitch = _scan_pitch(n)
    cpitch = _scan_pitch(ctx_len)
    big = pltpu.VMEM((SCAN_CHUNKS * pitch, LRU_CHUNK), F32)
    small = pltpu.VMEM((SCAN_CHUNKS * cpitch, LRU_CHUNK), F32)
    return pl.pallas_call(
        functools.partial(_lru_kernel, n=n, ctx_len=ctx_len),
        out_shape=jax.ShapeDtypeStruct((nch, batch * n, LRU_CHUNK), BF16),
        grid=(batch, nch),
        in_specs=[pl.BlockSpec((1, n, LRU_CHUNK), lambda b, c: (c, b, 0)),
                  pl.BlockSpec((1, n, LRU_CHUNK), lambda b, c: (c, b, 0)),
                  pl.BlockSpec((1, ctx_len, LRU_CHUNK), lambda b, c: (c, b, 0)),
                  pl.BlockSpec((4, LRU_CHUNK), lambda b, c: (0, c)),
                  pl.BlockSpec((1, LRU_CHUNK), lambda b, c: (0, c)),
                  pl.BlockSpec((1, LRU_CHUNK, 4 * LRU_CHUNK), lambda b, c: (c, 0, 0)),
                  pl.BlockSpec((1, 1, 4 * LRU_CHUNK), lambda b, c: (c, 0, 0)),
                  pl.BlockSpec((2, LRU_CHUNK), lambda b, c: (0, c))],
        out_specs=pl.BlockSpec((1, n, LRU_CHUNK), lambda b, c: (c, b, 0)),
        scratch_shapes=[pltpu.VMEM((n + 2 * SUBLANES, LRU_CHUNK), F32),
                        big, big, big, big, big, big, small, small, small, small],
        compiler_params=_cparams(("arbitrary", "arbitrary")),
        name="rglru",
    )(xb, gb, xb_ctx, conv_w, conv_b, wcat, gbias, lam)


def _lru_gate_weights(w_r, b_r, w_i, b_i):
    nch = LRU_WIDTH // LRU_CHUNK
    bpc = LRU_CHUNK // LRU_BLOCK

    def dense(w):
        wc = w.reshape(nch, bpc, LRU_BLOCK, LRU_BLOCK)
        eye = jnp.eye(bpc, dtype=w.dtype)
        return jnp.einsum("cbij,bd->cbidj", wc, eye).reshape(nch, LRU_CHUNK, LRU_CHUNK)

    wcat = jnp.concatenate([dense(w_r[0]), dense(w_i[0]), dense(w_r[1]), dense(w_i[1])], axis=-1)
    chunk = lambda v: v.reshape(nch, 1, LRU_CHUNK)
    gbias = jnp.concatenate([chunk(b_r[0]), chunk(b_i[0]), chunk(b_r[1]), chunk(b_i[1])], axis=-1)
    return (0.5 * wcat).astype(BF16), (0.5 * gbias).astype(F32)


def _route(s, sel, route_ref):
    srow = [s[e:e + 1, :] for e in range(N_EXPERTS)]
    lrow = [sel[e:e + 1, :] for e in range(N_EXPERTS)]
    gscore = []
    for g in range(N_EXPERT_GROUPS):
        a = lrow[g * EXPERTS_PER_GROUP:(g + 1) * EXPERTS_PER_GROUP]
        best = a[0] + a[1]
        for i, j in ((0, 2), (0, 3), (1, 2), (1, 3), (2, 3)):
            best = jnp.maximum(best, a[i] + a[j])
        gscore.append(best)
    bg = jnp.zeros_like(gscore[0], dtype=jnp.int32)
    bv = gscore[0]
    for g in range(1, N_EXPERT_GROUPS):
        upd = gscore[g] > bv
        bg = jnp.where(upd, g, bg)
        bv = jnp.where(upd, gscore[g], bv)

    def pick(rows_):
        out = []
        for j in range(EXPERTS_PER_GROUP):
            v = rows_[j]
            for g in range(1, N_EXPERT_GROUPS):
                v = jnp.where(bg == g, rows_[g * EXPERTS_PER_GROUP + j], v)
            out.append(v)
        return out
    cand = pick(lrow)
    cs = pick(srow)
    i1 = jnp.zeros_like(bg)
    v1 = cand[0]
    w1 = cs[0]
    for j in range(1, EXPERTS_PER_GROUP):
        upd = cand[j] > v1
        i1 = jnp.where(upd, j, i1)
        v1 = jnp.where(upd, cand[j], v1)
        w1 = jnp.where(upd, cs[j], w1)
    i2 = jnp.full_like(bg, -1)
    v2 = jnp.full_like(v1, -jnp.inf)
    w2 = jnp.zeros_like(w1)
    for j in range(EXPERTS_PER_GROUP):
        upd = (i1 != j) & (cand[j] > v2)
        i2 = jnp.where(upd, j, i2)
        v2 = jnp.where(upd, cand[j], v2)
        w2 = jnp.where(upd, cs[j], w2)
    den = w1 + w2
    g1 = w1 / den
    g2 = w2 / den
    for j in range(EXPERTS_PER_GROUP):
        route_ref[j:j + 1, :] = jnp.where(i1 == j, g1, 0.0) + jnp.where(i2 == j, g2, 0.0)
    route_ref[ROUTE_GID_ROW:ROUTE_GID_ROW + 1, :] = bg.astype(F32)
    pad = SUBLANES - ROUTE_GID_ROW - 1
    route_ref[ROUTE_GID_ROW + 1:, :] = jnp.zeros((pad, bg.shape[1]), F32)


POST_SUBTILE = 512
POST_TILE = 2 * POST_SUBTILE
INPROJ_TILE = 1024


def _post_kernel(*refs, n_parts):
    parts = refs[:n_parts]
    (w_ref, x_ref, gate_ref, shift_ref, scale_ref, gain_ref, rw_ref, rb_ref,
     x1_ref, h2_ref, route_ref, xbuf, xsem) = refs[n_parts:]
    i = pl.program_id(0)
    nt = pl.num_programs(0)
    tm = x1_ref.shape[0]

    def x_copy(tile):
        tile = jnp.minimum(tile, nt - 1)
        r0 = pl.multiple_of(tile * tm, tm)
        ring = tile % X1_RING
        return pltpu.make_async_copy(x_ref.at[pl.ds(r0, tm)], xbuf.at[ring], xsem.at[ring])

    @pl.when(i == 0)
    def _():
        for j in range(X1_RING - 1):
            x_copy(j).start()

    @pl.when(i + X1_RING - 1 < nt)
    def _():
        x_copy(i + X1_RING - 1).start()

    x_copy(i).wait()
    x_tile = xbuf.at[i % X1_RING]
    for sub in range(tm // POST_SUBTILE):
        rows = pl.ds(sub * POST_SUBTILE, POST_SUBTILE)
        pieces = []
        for p in parts:
            pieces += [p[c, rows, :] for c in range(p.shape[0])] if len(p.shape) == 3 else [p[rows, :]]
        mixed = jnp.concatenate(pieces, axis=-1) if len(pieces) > 1 else pieces[0]
        mix = jnp.dot(mixed, w_ref[...], preferred_element_type=F32)
        x1 = x_tile[rows, :] + gate_ref[0] * mix
        x1_ref[rows, :] = x1
        h2 = _norm_modulate(x1, gain_ref[...], shift_ref[0], scale_ref[0])
        h_hi = h2.astype(BF16)
        h2_ref[rows, :] = h_hi
        logits = jnp.dot(h_hi, rw_ref[...], preferred_element_type=F32)
        s = _sigmoid(logits[:, :LANES].T[:N_EXPERTS, :])
        _route(s, s + rb_ref[...], route_ref.at[:, rows])


def _post_mixer(parts, w, x2d, gate1, shift2, scale2, gain, rw_cat, rbias, tokens_per_batch, tm):
    t, d = x2d.shape
    assert t // tm >= X1_RING - 1
    tpb = tokens_per_batch // tm
    full = lambda i: (0, 0)
    per_b = lambda i: (i // tpb, 0, 0)

    def part_spec(p):
        if p.ndim == 3:
            return pl.BlockSpec((p.shape[0], tm, p.shape[2]), lambda i: (0, i, 0))
        return pl.BlockSpec((tm, p.shape[1]), lambda i: (i, 0))

    return pl.pallas_call(
        functools.partial(_post_kernel, n_parts=len(parts)),
        out_shape=[jax.ShapeDtypeStruct((t, d), F32), jax.ShapeDtypeStruct((t, d), BF16),
                   jax.ShapeDtypeStruct((SUBLANES, t), F32)],
        grid=(t // tm,),
        in_specs=[part_spec(p) for p in parts] + [
                  pl.BlockSpec(w.shape, full),
                  pl.BlockSpec(memory_space=pl.ANY),
                  pl.BlockSpec((1, 1, d), per_b),
                  pl.BlockSpec((1, 1, d), per_b),
                  pl.BlockSpec((1, 1, d), per_b),
                  pl.BlockSpec((1, d), full),
                  pl.BlockSpec(rw_cat.shape, full),
                  pl.BlockSpec(rbias.shape, full)],
        out_specs=[pl.BlockSpec((tm, d), lambda i: (i, 0)),
                   pl.BlockSpec((tm, d), lambda i: (i, 0)),
                   pl.BlockSpec((SUBLANES, tm), lambda i: (0, i))],
        scratch_shapes=[pltpu.VMEM((X1_RING, tm, d), F32), pltpu.SemaphoreType.DMA((X1_RING,))],
        compiler_params=_cparams(("arbitrary",)),
        name="post_mixer",
    )(*parts, w, x2d, gate1, shift2, scale2, gain, rw_cat, rbias)


def _moe_layout(t):
    nt = t // MOE_TILE
    grid = -(-(t + N_EXPERT_GROUPS * (ROW_ALIGN - 1) * nt) // MOE_TILE) + N_EXPERT_GROUPS
    return nt, grid


def _moe_tables(gid, t):
    nt, grid = _moe_layout(t)
    ng = N_EXPERT_GROUPS
    per_tile = MOE_TILE // ROW_ALIGN
    onehot = (gid.reshape(nt, MOE_TILE, 1) == jnp.arange(ng, dtype=jnp.int32)).astype(jnp.int32)
    cnt = onehot.sum(axis=1)
    seg = (cnt + ROW_ALIGN - 1) // ROW_ALIGN
    src = jnp.cumsum(seg, axis=1) - seg
    fill = seg.sum(axis=0)
    ntile = (fill + per_tile - 1) // per_tile
    cum = jnp.cumsum(ntile)
    base = (cum - ntile) * per_tile
    dst = jnp.cumsum(seg, axis=0) - seg + base[None, :]
    seg_tab = jnp.concatenate([seg, src, dst], axis=1).reshape(-1).astype(jnp.int32)
    tail = (-fill) % per_tile
    tail_tab = jnp.concatenate([tail, fill + base, cum[-1:]]).astype(jnp.int32)
    i = jnp.arange(grid, dtype=jnp.int32)
    valid = i < cum[-1]
    ie = jnp.minimum(i, cum[-1] - 1)
    g_of = jnp.sum((ie[:, None] >= cum[None, :]).astype(jnp.int32), axis=1)
    return seg_tab, tail_tab, g_of.astype(jnp.int32), valid.astype(jnp.int32)


def _segment_copies(tab_ref, tile, enable, make_copy):
    ng = N_EXPERT_GROUPS
    base = jnp.maximum(tile, 0) * (3 * ng)
    out = []
    for g in range(ng):
        n = tab_ref[base + g]
        src = tab_ref[base + ng + g]
        dst = tab_ref[base + 2 * ng + g]
        for k in range(MOE_SEG_BITS - 1, -1, -1):
            done = (n >> (k + 1)) << (k + 1)
            rows = ROW_ALIGN << k
            s0 = pl.multiple_of((src + done) * ROW_ALIGN, ROW_ALIGN)
            d0 = pl.multiple_of((dst + done) * ROW_ALIGN, ROW_ALIGN)
            out.append((enable & (((n >> k) & 1) == 1), make_copy(s0, d0, rows)))
    return out


def _start_copies(pairs):
    for cond, copies in pairs:
        @pl.when(cond)
        def _():
            for c in copies:
                c.start()


def _wait_copies(pairs):
    for cond, copies in pairs:
        @pl.when(cond)
        def _():
            for c in copies:
                c.wait()


def _split_bf16x3(x):
    hi = x.astype(BF16).astype(F32)
    r1 = x - hi
    mid = r1.astype(BF16).astype(F32)
    lo = (r1 - mid).astype(BF16).astype(F32)
    return hi, mid, lo


def _dispatch_kernel(seg_ref, tail_ref, h_ref, route_ref, tri_ref, slot_ref, hs_ref, cbuf, zbuf, sem, *, nt):
    i = pl.program_id(0)
    tm, d = h_ref.shape
    ng = N_EXPERT_GROUPS
    cur = i % 2

    def seg_copies(tile, enable, buf):
        def seg_copy(s0, d0, rows):
            return (pltpu.make_async_copy(cbuf.at[buf, pl.ds(s0, rows)], hs_ref.at[pl.ds(d0, rows)], sem.at[buf]),)
        return _segment_copies(seg_ref, tile, enable, seg_copy)

    _wait_copies(seg_copies(i - 2, i >= 2, cur))

    route = route_ref[...]
    gid = route[ROUTE_GID_ROW:ROUTE_GID_ROW + 1, :]
    grp = lax.broadcasted_iota(jnp.int32, (SUBLANES, tm), 0).astype(F32)
    onehot = jnp.where(grp == gid, 1.0, 0.0)
    rank = jnp.dot(onehot.astype(BF16), tri_ref[...], preferred_element_type=F32)
    slot = jnp.zeros((1, tm), F32)
    for g in range(ng):
        start = (seg_ref[i * 3 * ng + ng + g] * ROW_ALIGN).astype(F32)
        slot = slot + onehot[g:g + 1, :] * (rank[g:g + 1, :] - 1.0 + start)
    slot_ref[...] = jnp.broadcast_to(slot, (SUBLANES, tm))
    perm = jnp.where(lax.broadcasted_iota(jnp.int32, (MOE_CROWS, tm), 0).astype(F32) == slot, 1.0, 0.0)
    perm = perm.astype(BF16)
    cbuf[cur, :, :d] = jnp.dot(perm, h_ref[...], preferred_element_type=F32).astype(cbuf.dtype)
    parts = jnp.concatenate(list(_split_bf16x3(route)) + [jnp.zeros((LANES - 3 * SUBLANES, tm), F32)], axis=0)
    record = lax.dot_general(perm, parts.astype(BF16), (((1,), (1,)), ((), ())), preferred_element_type=F32)
    cbuf[cur, :, d:] = record.astype(cbuf.dtype)
    _start_copies(seg_copies(i, i >= 0, cur))

    @pl.when(i == pl.num_programs(0) - 1)
    def _():
        _wait_copies(seg_copies(i - 1, i >= 1, 1 - cur))
        _wait_copies(seg_copies(i, i >= 0, cur))
        zbuf[...] = jnp.zeros(zbuf.shape, zbuf.dtype)

        def zero_copy(d0, rows):
            return (pltpu.make_async_copy(zbuf.at[pl.ds(0, rows)], hs_ref.at[pl.ds(d0, rows)], sem.at[0]),)
        pairs = []
        for g in range(ng):
            n = tail_ref[g]
            dst = tail_ref[ng + g]
            for k in range(MOE_TAIL_BITS - 1, -1, -1):
                done = (n >> (k + 1)) << (k + 1)
                d0 = pl.multiple_of((dst + done) * ROW_ALIGN, ROW_ALIGN)
                pairs.append((((n >> k) & 1) == 1, zero_copy(d0, ROW_ALIGN << k)))
        used = tail_ref[2 * ng]
        total = hs_ref.shape[0] // MOE_TILE
        for j in range(total - nt):
            d0 = pl.multiple_of(jnp.minimum(used + j, total - 1) * MOE_TILE, MOE_TILE)
            pairs.append((used + j < total, zero_copy(d0, MOE_TILE)))
        _start_copies(pairs)
        _wait_copies(pairs)


def _dispatch(seg_tab, tail_tab, h2, route, tri):
    t, d = h2.shape
    nt, grid = _moe_layout(t)
    rows = grid * MOE_TILE
    grid_spec = pltpu.PrefetchScalarGridSpec(
        num_scalar_prefetch=2,
        grid=(nt,),
        in_specs=[pl.BlockSpec((MOE_TILE, d), lambda i, *_: (i, 0)),
                  pl.BlockSpec((SUBLANES, MOE_TILE), lambda i, *_: (0, i)),
                  pl.BlockSpec((MOE_TILE, MOE_TILE), lambda i, *_: (0, 0))],
        out_specs=[pl.BlockSpec((SUBLANES, MOE_TILE), lambda i, *_: (0, i)),
                   pl.BlockSpec(memory_space=pl.ANY)],
        scratch_shapes=[pltpu.VMEM((2, MOE_CROWS, d + LANES), BF16), pltpu.VMEM((MOE_TILE, d + LANES), BF16),
                        pltpu.SemaphoreType.DMA((2,))])
    return pl.pallas_call(
        functools.partial(_dispatch_kernel, nt=nt),
        out_shape=[jax.ShapeDtypeStruct((SUBLANES, t), F32),
                   jax.ShapeDtypeStruct((rows, d + LANES), BF16)],
        grid_spec=grid_spec,
        compiler_params=_cparams(("arbitrary",)),
        name="moe_dispatch",
    )(seg_tab, tail_tab, h2, route, tri)


def _ffn_kernel(grp_ref, valid_ref, h_ref, wg_ref, wu_ref, wd_ref, y_ref):
    i = pl.program_id(0)
    d = y_ref.shape[1]

    @pl.when(valid_ref[i] == 0)
    def _():
        y_ref[...] = jnp.zeros(y_ref.shape, y_ref.dtype)

    @pl.when(valid_ref[i] == 1)
    def _():
        h = h_ref[:, :d]
        gates = h_ref[:, d:].astype(F32)
        acts = []
        for j in range(EXPERTS_PER_GROUP):
            a = jnp.dot(h, wg_ref[j], preferred_element_type=F32)
            u = jnp.dot(h, wu_ref[j], preferred_element_type=F32)
            gate = (gates[:, j:j + 1] + gates[:, SUBLANES + j:SUBLANES + j + 1]
                    + gates[:, 2 * SUBLANES + j:2 * SUBLANES + j + 1])
            acts.append(((a * _sigmoid_tanh(a)) * u * gate).astype(BF16))
        wd = wd_ref[...].reshape(EXPERTS_PER_GROUP * D_FF_EXPERT, d)
        y = jnp.dot(jnp.concatenate(acts, axis=1), wd, preferred_element_type=F32)
        y_ref[...] = y.astype(y_ref.dtype)


def _ffn(grp, valid, hs, wg, wu, wd, layer):
    rows, width = hs.shape
    d = width - LANES
    epg = EXPERTS_PER_GROUP
    w_idx = lambda i, grp, valid: (layer * N_EXPERT_GROUPS + grp[i], 0, 0)
    grid_spec = pltpu.PrefetchScalarGridSpec(
        num_scalar_prefetch=2,
        grid=(rows // MOE_TILE,),
        in_specs=[pl.BlockSpec((MOE_TILE, width), lambda i, grp, valid: (i, 0)),
                  pl.BlockSpec((epg, d, D_FF_EXPERT), w_idx),
                  pl.BlockSpec((epg, d, D_FF_EXPERT), w_idx),
                  pl.BlockSpec((epg, D_FF_EXPERT, d), w_idx)],
        out_specs=pl.BlockSpec((MOE_TILE, d), lambda i, grp, valid: (i, 0)))
    return pl.pallas_call(
        _ffn_kernel,
        out_shape=jax.ShapeDtypeStruct((rows, d), BF16),
        grid_spec=grid_spec,
        compiler_params=_cparams(("arbitrary",)),
        name="moe_ffn",
    )(grp, valid, hs, wg, wu, wd)


def _combine_kernel(seg_ref, x1_ref, slot_ref, gate2_ref, ys_ref, *rest, fnet):
    if fnet:
        shift_ref, scale_ref, gain_ref, cs_ref, o_ref, y1_ref, y2_ref, ybuf, xbuf, sem, xsem = rest
    else:
        o_ref, ybuf, xbuf, sem, xsem = rest
    i = pl.program_id(0)
    nt = pl.num_programs(0)
    tm = o_ref.shape[0]
    cur = i % 2

    def seg_copies(tile, enable, buf):
        def seg_copy(s0, d0, rows):
            return (pltpu.make_async_copy(ys_ref.at[pl.ds(d0, rows)], ybuf.at[buf, pl.ds(s0, rows)], sem.at[buf]),)
        return _segment_copies(seg_ref, tile, enable, seg_copy)

    def x1_copy(tile):
        tile = jnp.minimum(tile, nt - 1)
        r0 = pl.multiple_of(tile * tm, tm)
        ring = tile % X1_RING
        return pltpu.make_async_copy(x1_ref.at[pl.ds(r0, tm)], xbuf.at[ring], xsem.at[ring])

    @pl.when(i == 0)
    def _():
        ybuf[...] = jnp.zeros(ybuf.shape, ybuf.dtype)
        _start_copies(seg_copies(i, i == 0, cur))
        for j in range(X1_RING - 1):
            x1_copy(j).start()

    @pl.when(i + X1_RING - 1 < nt)
    def _():
        x1_copy(i + X1_RING - 1).start()

    nxt = jnp.minimum(i + 1, nt - 1)
    _start_copies(seg_copies(nxt, i + 1 < nt, 1 - cur))
    _wait_copies(seg_copies(i, i >= 0, cur))
    x1_copy(i).wait()
    slot = slot_ref[0:1, :]
    perm = jnp.where(lax.broadcasted_iota(jnp.int32, (MOE_CROWS, tm), 0).astype(F32) == slot, 1.0, 0.0)
    y = lax.dot_general(perm.astype(BF16), ybuf[cur], (((0,), (0,)), ((), ())), preferred_element_type=F32)
    x = xbuf[i % X1_RING] + gate2_ref[0] * y
    o_ref[...] = x
    if fnet:
        _fnet_channel_tile(x, shift_ref, scale_ref, gain_ref, cs_ref, y1_ref, y2_ref)


def _combine(seg_tab, x1, slot, gate2, ys, tokens_per_batch, fnet=None):
    t, d = x1.shape
    tpb = tokens_per_batch // MOE_TILE
    per_b = lambda i, *_: (i // tpb, 0, 0)
    full = lambda i, *_: (0, 0)
    assert t // MOE_TILE >= X1_RING - 1
    in_specs = [pl.BlockSpec(memory_space=pl.ANY),
                pl.BlockSpec((SUBLANES, MOE_TILE), lambda i, *_: (0, i)),
                pl.BlockSpec((1, 1, d), per_b),
                pl.BlockSpec(memory_space=pl.ANY)]
    out_shape = [jax.ShapeDtypeStruct((t, d), F32)]
    out_specs = [pl.BlockSpec((MOE_TILE, d), lambda i, *_: (i, 0))]
    args = [seg_tab, x1, slot, gate2, ys]
    if fnet is not None:
        shift, scale, gain, cs = fnet
        gw = d // FNET_GROUPS
        in_specs += [pl.BlockSpec((1, 1, d), per_b), pl.BlockSpec((1, 1, d), per_b), pl.BlockSpec((1, d), full),
                     pl.BlockSpec(cs.shape, full)]
        out_shape += [jax.ShapeDtypeStruct((FNET_GROUPS, t, gw), BF16)] * 2
        out_specs += [pl.BlockSpec((FNET_GROUPS, MOE_TILE, gw), lambda i, *_: (0, i, 0))] * 2
        args += [shift, scale, gain, cs]
    grid_spec = pltpu.PrefetchScalarGridSpec(
        num_scalar_prefetch=1,
        grid=(t // MOE_TILE,),
        in_specs=in_specs,
        out_specs=out_specs,
        scratch_shapes=[pltpu.VMEM((2, MOE_CROWS, d), BF16), pltpu.VMEM((X1_RING, MOE_TILE, d), F32),
                        pltpu.SemaphoreType.DMA((2,)), pltpu.SemaphoreType.DMA((X1_RING,))])
    out = pl.pallas_call(
        functools.partial(_combine_kernel, fnet=fnet is not None),
        out_shape=out_shape,
        grid_spec=grid_spec,
        compiler_params=_cparams(("arbitrary",)),
        name="moe_combine_fnet" if fnet is not None else "moe_combine",
    )(*args)
    return out if fnet is not None else out[0]


def _grouped_moe(h2, route, x1, gate2, wg, wu, wd, layer, tri):
    t = h2.shape[0]
    gid = route[ROUTE_GID_ROW].astype(jnp.int32)
    seg_tab, tail_tab, grp, valid = _moe_tables(gid, t)
    slot, hs = _dispatch(seg_tab, tail_tab, h2, route, tri)
    ys = _ffn(grp, valid, hs, wg, wu, wd, layer)
    return seg_tab, x1, slot, gate2, ys


FFT_J = SUBLANES
FFT_PAIR_ROWS = 2 * FFT_J * (GRID_W // FFT_J)


def _pair_grid_rows(y):
    rows = y.shape[0]
    assert rows % (2 * GRID_W) == 0
    groups = []
    for u in range(rows // (2 * GRID_W)):
        for sb in range(GRID_W // FFT_J):
            for e in range(2):
                start = GRID_W * (2 * u + e) + FFT_J * sb
                groups.append(y[start:start + FFT_J])
    return jnp.concatenate(groups, axis=0)


def _fnet_channel_tile(x, shift_ref, scale_ref, gain_ref, cs_ref, y1_ref, y2_ref):
    h = _norm_modulate(x, gain_ref[...], shift_ref[0], scale_ref[0]).astype(BF16)
    gw = D_MODEL // FNET_GROUPS
    for g in range(FNET_GROUPS):
        y = _pair_grid_rows(jnp.dot(h[:, g * gw:(g + 1) * gw], cs_ref[...], preferred_element_type=F32))
        y1_ref[g] = y[:, :gw].astype(y1_ref.dtype)
        y2_ref[g] = y[:, gw:].astype(y2_ref.dtype)


def _half_rows(m):
    hc = m // 2 + 1
    return hc, -(-FFT_J * hc // ROW_ALIGN) * ROW_ALIGN


def _mirror(lo, hi, hc, m):
    return jnp.concatenate([lo[:FFT_J * hc]] + [hi[FFT_J * c:FFT_J * (c + 1)] for c in range(m - hc, 0, -1)], axis=0)


def _fnet_pos_kernel(y1_ref, y2_ref, lr_ref, ls_ref, cs_ref, sn_ref, o_ref, z1, a_re, a_im, *, n):
    r1 = n // GRID_W
    gw = y1_ref.shape[2]
    hc_r, part_r = _half_rows(r1)
    hc_s, part_s = _half_rows(GRID_W)

    def stage_r(sb, carry):
        s0 = pl.multiple_of(sb * FFT_J, FFT_J)
        p0 = pl.multiple_of(sb * 2 * FFT_J, 2 * FFT_J)
        rhs1 = jnp.concatenate([y1_ref[0, pl.ds(FFT_PAIR_ROWS * u + p0, 2 * FFT_J), :] for u in range(r1 // 2)], axis=0)
        rhs2 = jnp.concatenate([y2_ref[0, pl.ds(FFT_PAIR_ROWS * u + p0, 2 * FFT_J), :] for u in range(r1 // 2)], axis=0)
        p = jnp.dot(lr_ref[...], rhs1, preferred_element_type=F32)
        q = jnp.dot(lr_ref[...], rhs2, preferred_element_type=F32)
        pc, ps = p[:part_r], p[part_r:]
        qc, qs = q[:part_r], q[part_r:]
        re = _mirror(pc - qs, pc + qs, hc_r, r1)
        nim = _mirror(qc + ps, qc - ps, hc_r, r1)
        cs = jnp.concatenate([cs_ref[sb]] * (gw // LANES), axis=1)
        sn = jnp.concatenate([sn_ref[sb]] * (gw // LANES), axis=1)
        tre = re * cs - nim * sn
        tnim = re * sn + nim * cs
        for c in range(r1):
            a_re[pl.ds(GRID_W * c + s0, FFT_J), :] = tre[FFT_J * c:FFT_J * (c + 1)]
            a_im[pl.ds(GRID_W * c + s0, FFT_J), :] = tnim[FFT_J * c:FFT_J * (c + 1)]
        return carry
    lax.fori_loop(0, GRID_W // FFT_J, stage_r, 0)

    cblk = FFT_J * GRID_W

    def stage_s(cb, carry):
        c0 = pl.multiple_of(cb * cblk, cblk)
        u = jnp.dot(ls_ref[0], a_re[pl.ds(c0, cblk), :].astype(BF16), preferred_element_type=F32)
        v = jnp.dot(ls_ref[1], a_im[pl.ds(c0, cblk), :].astype(BF16), preferred_element_type=F32)
        lo = u - v
        hi = u + v
        k0 = pl.multiple_of(cb * FFT_J, FFT_J)
        for d in range(hc_s):
            z1[pl.ds(r1 * d + k0, FFT_J), :] = lo[FFT_J * d:FFT_J * (d + 1)]
        for d in range(1, GRID_W - hc_s + 1):
            z1[pl.ds(r1 * (GRID_W - d) + k0, FFT_J), :] = hi[FFT_J * d:FFT_J * (d + 1)]
        return carry
    lax.fori_loop(0, r1 // FFT_J, stage_s, 0)
    o_ref[0] = z1[...].astype(o_ref.dtype)


def _fnet_position(y1, y2, lr, ls, tw_cos, tw_sin, batch, n):
    groups, t, gw = y1.shape
    full2 = lambda b, g: (0, 0)
    full3 = lambda b, g: (0, 0, 0)
    scratch = pltpu.VMEM((n, gw), F32)
    return pl.pallas_call(
        functools.partial(_fnet_pos_kernel, n=n),
        out_shape=jax.ShapeDtypeStruct((groups, t, gw), BF16),
        grid=(batch, groups),
        in_specs=[pl.BlockSpec((1, n, gw), lambda b, g: (g, b, 0)),
                  pl.BlockSpec((1, n, gw), lambda b, g: (g, b, 0)),
                  pl.BlockSpec(lr.shape, full2),
                  pl.BlockSpec(ls.shape, full3),
                  pl.BlockSpec(tw_cos.shape, full3),
                  pl.BlockSpec(tw_sin.shape, full3)],
        out_specs=pl.BlockSpec((1, n, gw), lambda b, g: (g, b, 0)),
        scratch_shapes=[scratch, scratch, scratch],
        compiler_params=_cparams(("arbitrary", "arbitrary")),
        name="fnet_position",
    )(y1, y2, lr, ls, tw_cos, tw_sin)


def _dft_tables(n):
    assert n % (GRID_W * FFT_J) == 0
    gw = D_MODEL // FNET_GROUPS
    j = np.arange(gw)
    ang = 2.0 * np.pi * ((j[:, None] * j[None, :]) % gw) / gw
    cs = np.concatenate([np.cos(ang), np.sin(ang)], axis=1) / np.sqrt(gw)
    r1 = n // GRID_W
    assert r1 % 2 == 0
    eye = np.eye(FFT_J)
    scale = float(n) ** -0.25
    a = np.arange(r1)
    hc_r, part_r = _half_rows(r1)
    ang_r = 2.0 * np.pi * ((a[:hc_r, None] * a[None, :]) % r1) / r1
    lr = np.zeros((2 * part_r, FFT_J * r1))
    lr[:FFT_J * hc_r] = np.kron(np.cos(ang_r), eye) * scale
    lr[part_r:part_r + FFT_J * hc_r] = np.kron(np.sin(ang_r), eye) * scale
    s = np.arange(GRID_W)
    hc_s, part_s = _half_rows(GRID_W)
    ang_s = 2.0 * np.pi * ((s[:hc_s, None] * s[None, :]) % GRID_W) / GRID_W
    ls = np.zeros((2, part_s, FFT_J * GRID_W))
    ls[0, :FFT_J * hc_s] = np.einsum("ds,cC->dcCs", np.cos(ang_s), eye).reshape(FFT_J * hc_s, FFT_J * GRID_W) * scale
    ls[1, :FFT_J * hc_s] = np.einsum("ds,cC->dcCs", np.sin(ang_s), eye).reshape(FFT_J * hc_s, FFT_J * GRID_W) * scale
    sb = np.arange(GRID_W // FFT_J)
    s_of = sb[:, None, None] * FFT_J + np.arange(FFT_J)[None, None, :]
    ang_t = 2.0 * np.pi * ((s_of * a[None, :, None]) % n) / n
    ang_t = ang_t.reshape(len(sb), r1 * FFT_J, 1)
    tw_cos = jnp.broadcast_to(jnp.asarray(np.cos(ang_t), F32), (len(sb), r1 * FFT_J, LANES))
    tw_sin = jnp.broadcast_to(jnp.asarray(np.sin(ang_t), F32), (len(sb), r1 * FFT_J, LANES))
    return jnp.asarray(cs, BF16), jnp.asarray(lr, BF16), jnp.asarray(ls, BF16), tw_cos, tw_sin


def kernel(x, c, ctx, c_ctx, ada_w, ada_b, norm_mix, norm_ffn, mix_w_in, mix_w_out, na_q_norm, na_k_norm, na_rpb,
           lru_conv_w, lru_conv_b, lru_gate_r_w, lru_gate_r_b, lru_gate_i_w, lru_gate_i_b, lru_lambda,
           fnet_w_out, router_w, router_bias, moe_w_gate, moe_w_up, moe_w_down):
    batch, n, d = x.shape
    ctx_len = ctx.shape[1]
    depth = ada_w.shape[0]
    rows = n // GRID_W
    assert d == D_MODEL and n % (GRID_W * NA_QROWS) == 0 and rows >= 4 * NA_QROWS
    assert n % MOE_TILE == 0 and n % INPROJ_TILE == 0 and n % POST_TILE == 0 and ctx_len % LANES == 0
    t = batch * n
    tri = jnp.asarray(np.triu(np.ones((MOE_TILE, MOE_TILE))), BF16)

    r_pad = -(-(batch + 1) // SUBLANES) * SUBLANES
    c_rows = jnp.concatenate([c, c_ctx[None, :], jnp.zeros((r_pad - batch - 1, d), c.dtype)], axis=0)
    mod = _modulation(c_rows, ada_w, ada_b)

    def mod_slices(layer):
        m = mod[layer, :batch].reshape(batch, 1, 6, d)
        return [m[:, :, i, :] for i in range(6)]

    rw_cat = jnp.pad(router_w.astype(F32), ((0, 0), (0, 2 * LANES - N_EXPERTS))).astype(BF16)
    rbias = router_bias.reshape(N_EXPERTS, 1).astype(F32)
    x2d = x.reshape(t, d)
    ctx2d = ctx.reshape(batch * ctx_len, d)

    pending = None
    moe_w = None
    for layer in range(depth):
        li = layer // 2
        shift1, scale1, gate1, shift2, scale2, gate2 = mod_slices(layer)
        gain_mix = norm_mix[layer].reshape(1, d)
        gain_ffn = norm_ffn[layer].reshape(1, d)
        if layer % 2 == 0:
            if pending is not None:
                x2d = _combine(*pending, n)
            w_in = mix_w_in[li].astype(BF16)
            ind = jnp.asarray(np.kron(np.eye(NA_HEADS // 2), np.ones((HEAD_DIM, HEAD_DIM))), BF16)
            qg = (jnp.tile(na_q_norm[li], NA_HEADS) * (HEAD_DIM ** -0.5 * LOG2_E)).reshape(1, NA_WIDTH).astype(F32)
            kg = jnp.tile(na_k_norm[li], NA_HEADS).reshape(1, NA_WIDTH).astype(F32)
            side = () if moe_w is not None else (moe_w_gate.reshape((-1,) + moe_w_gate.shape[2:]).astype(F32),
                                                 moe_w_up.reshape((-1,) + moe_w_up.shape[2:]).astype(F32),
                                                 moe_w_down.reshape((-1,) + moe_w_down.shape[2:]).astype(F32))
            q, k, v, xb, gb, *rounded = _inproj(x2d, shift1, scale1, gain_mix, w_in, ind, qg, kg,
                                                ("q", "k", "v", "x", "g"), n, INPROJ_TILE, side=side)
            if rounded:
                moe_w = rounded
            mctx = mod[layer, batch, :2 * d]
            shift_c = jnp.broadcast_to(mctx[:d], (batch, 1, d))
            scale_c = jnp.broadcast_to(mctx[d:], (batch, 1, d))
            k_c, v_c, xb_c = _inproj(ctx2d, shift_c, scale_c, gain_mix, w_in[:, NA_WIDTH:4 * NA_WIDTH], ind, qg, kg,
                                     ("k", "v", "x"), ctx_len, ctx_len)
            bias = _na_bias_tables(na_rpb[li], rows)
            attn = _attention(q, k, v, k_c, v_c, bias, batch, n, ctx_len)
            wcat, gbias = _lru_gate_weights(lru_gate_r_w[li], lru_gate_r_b[li], lru_gate_i_w[li], lru_gate_i_b[li])
            lru = _lru(xb, gb, xb_c, lru_conv_w[li].astype(F32), lru_conv_b[li].reshape(1, LRU_WIDTH).astype(F32),
                       wcat, gbias, lru_lambda[li].astype(F32), batch, n, ctx_len)
            parts, w_out = [attn, lru], mix_w_out[li].astype(BF16)
        else:
            cs, lr, ls, tw_cos, tw_sin = _dft_tables(n)
            x2d, y1, y2 = _combine(*pending, n, fnet=(shift1, scale1, gain_mix, cs))
            parts, w_out = [_fnet_position(y1, y2, lr, ls, tw_cos, tw_sin, batch, n)], fnet_w_out[li].astype(BF16)
        x1, h2, route = _post_mixer(parts, w_out, x2d, gate1, shift2, scale2, gain_ffn, rw_cat, rbias, n,
                                    POST_TILE)
        pending = _grouped_moe(h2, route, x1, gate2, *moe_w, layer, tri)
    return _combine(*pending, n).reshape(batch, n, d)
```

```python
import functools

import numpy as np
import jax
import jax.numpy as jnp
from jax import lax
from jax.experimental import pallas as pl
from jax.experimental.pallas import tpu as pltpu

F32 = jnp.float32
BF16 = jnp.bfloat16
HIGHEST = lax.Precision.HIGHEST

D_MODEL = 1024
GRID_W = 64
HEAD_DIM = 64
NA_HEADS = 8
NA_WIDTH = NA_HEADS * HEAD_DIM
NA_WIN_ROWS = 8
NA_WIN_COLS = 16
LRU_WIDTH = 512
LRU_BLOCK = 64
LRU_C = 8.0
FNET_GROUPS = 4
N_EXPERTS = 16
EXPERTS_PER_GROUP = 4
N_EXPERT_GROUPS = 4
D_FF_EXPERT = 512
RMS_EPS = 1e-6
MASK_VALUE = -1e30
LOG2_E = 1.4426950408889634

V7X_VMEM_LIMIT_BYTES = 56 * 1024 * 1024
LANES = 128
SUBLANES = 8

NA_QROWS = 4
NA_KROWS = NA_QROWS + NA_WIN_ROWS - 1
NA_QBLK = NA_QROWS * GRID_W
NA_KBLK = NA_KROWS * GRID_W
NA_STEP_BLOCKS = 2

LRU_CHUNK = LANES
LRU_TROWS = 512

ROUTE_GID_ROW = EXPERTS_PER_GROUP
MOE_TILE = 512
ROW_ALIGN = 16
MOE_CROWS = MOE_TILE + N_EXPERT_GROUPS * ROW_ALIGN
MOE_SEG_BITS = (MOE_TILE // ROW_ALIGN).bit_length()
X1_RING = 3
MOE_TAIL_BITS = (MOE_TILE // ROW_ALIGN - 1).bit_length()


def _sigmoid(x):
    return 1.0 / (1.0 + jnp.exp(-x))


def _sigmoid_tanh(x):
    return 0.5 + 0.5 * jnp.tanh(0.5 * x)


def _cparams(sem, vmem=V7X_VMEM_LIMIT_BYTES):
    return pltpu.CompilerParams(dimension_semantics=sem, vmem_limit_bytes=vmem)


def _mod_kernel(c_ref, w_ref, b_ref, o_ref):
    c = c_ref[...]
    s = c * _sigmoid(c)
    o_ref[0] = jnp.dot(s.astype(BF16), w_ref[0].astype(BF16), preferred_element_type=F32) + b_ref[0]


def _modulation(c_rows, ada_w, ada_b):
    depth, d, n6 = ada_w.shape
    r = c_rows.shape[0]
    tn = n6 // 2
    return pl.pallas_call(
        _mod_kernel,
        out_shape=jax.ShapeDtypeStruct((depth, r, n6), F32),
        grid=(depth, n6 // tn),
        in_specs=[pl.BlockSpec((r, d), lambda l, j: (0, 0)),
                  pl.BlockSpec((1, d, tn), lambda l, j: (l, 0, j)),
                  pl.BlockSpec((1, 1, tn), lambda l, j: (l, 0, j))],
        out_specs=pl.BlockSpec((1, r, tn), lambda l, j: (l, 0, j)),
        compiler_params=_cparams(("arbitrary", "arbitrary")),
        name="adaln_mod",
    )(c_rows, ada_w, ada_b.reshape(depth, 1, n6))


def _norm_modulate(x, gain, shift, scale):
    ms = jnp.mean(x * x, axis=-1, keepdims=True)
    y = x * lax.rsqrt(ms + RMS_EPS) * gain
    return y * (1.0 + scale) + shift


def _inproj_kernel(x_ref, shift_ref, scale_ref, gain_ref, w_ref, ind_ref, qg_ref, kg_ref, *rest, segs, n_side):
    side_in, out_refs, side_out = rest[:n_side], rest[n_side:n_side + len(segs)], rest[n_side + len(segs):]
    for src, dst in zip(side_in, side_out):
        dst[...] = src[...].astype(dst.dtype)
    h = _norm_modulate(x_ref[...], gain_ref[...], shift_ref[0], scale_ref[0]).astype(BF16)
    for s, (kind, o_ref) in enumerate(zip(segs, out_refs)):
        z = jnp.dot(h, w_ref[:, s * NA_WIDTH:(s + 1) * NA_WIDTH], preferred_element_type=F32)
        if kind in ("q", "k"):
            zz = (z * z).astype(BF16)
            hw = ind_ref.shape[0]
            ms = jnp.concatenate([jnp.dot(zz[:, i * hw:(i + 1) * hw], ind_ref[...], preferred_element_type=F32)
                                  for i in range(NA_WIDTH // hw)], axis=1) * (1.0 / HEAD_DIM)
            g = qg_ref[...] if kind == "q" else kg_ref[...]
            z = z * lax.rsqrt(ms + RMS_EPS) * g
        if kind in ("x", "g"):
            for c in range(LRU_WIDTH // LRU_CHUNK):
                o_ref[c] = z[:, c * LRU_CHUNK:(c + 1) * LRU_CHUNK].astype(o_ref.dtype)
        else:
            o_ref[...] = z.astype(o_ref.dtype)


def _inproj(x2d, shift, scale, gain, w, ind, qg, kg, segs, tokens_per_batch, tm, side=()):
    t, d = x2d.shape
    tpb = tokens_per_batch // tm
    dt = {"q": BF16, "k": BF16, "v": BF16, "x": F32, "g": F32}
    full = lambda i: (0, 0)
    nch = LRU_WIDTH // LRU_CHUNK
    steps = t // tm
    side_specs, side_shapes = [], []
    for arr in side:
        items = arr.shape[0]
        share = -(-items // steps)
        assert items % share == 0
        last = items // share - 1
        side_specs.append(pl.BlockSpec((share,) + arr.shape[1:], lambda i, last=last: (jnp.minimum(i, last), 0, 0)))
        side_shapes.append(jax.ShapeDtypeStruct(arr.shape, BF16))

    def out_shape(kind):
        shape = (nch, t, LRU_CHUNK) if kind in ("x", "g") else (t, NA_WIDTH)
        return jax.ShapeDtypeStruct(shape, dt[kind])

    def out_spec(kind):
        if kind in ("x", "g"):
            return pl.BlockSpec((nch, tm, LRU_CHUNK), lambda i: (0, i, 0))
        return pl.BlockSpec((tm, NA_WIDTH), lambda i: (i, 0))

    return pl.pallas_call(
        functools.partial(_inproj_kernel, segs=segs, n_side=len(side)),
        out_shape=[out_shape(k) for k in segs] + side_shapes,
        grid=(steps,),
        in_specs=[pl.BlockSpec((tm, d), lambda i: (i, 0)),
                  pl.BlockSpec((1, 1, d), lambda i: (i // tpb, 0, 0)),
                  pl.BlockSpec((1, 1, d), lambda i: (i // tpb, 0, 0)),
                  pl.BlockSpec((1, d), full),
                  pl.BlockSpec(w.shape, full),
                  pl.BlockSpec(ind.shape, full),
                  pl.BlockSpec((1, NA_WIDTH), full),
                  pl.BlockSpec((1, NA_WIDTH), full)] + side_specs,
        out_specs=[out_spec(k) for k in segs] + side_specs,
        compiler_params=_cparams(("arbitrary",)),
        name="inproj_" + "".join(segs),
    )(x2d, shift, scale, gain, w, ind, qg, kg, *side)


def _na_bias_tables(rpb, rows):
    kr = NA_WIN_ROWS
    rb_count = rows // NA_QROWS
    cq = np.arange(GRID_W)
    ck = np.arange(GRID_W)
    col_start = np.clip(cq - NA_WIN_COLS // 2, 0, GRID_W - NA_WIN_COLS)
    valid_c = (ck[None, :] >= col_start[:, None]) & (ck[None, :] < col_start[:, None] + NA_WIN_COLS)
    dc = np.clip(ck[None, :] - cq[:, None], 1 - NA_WIN_COLS, NA_WIN_COLS - 1) + (NA_WIN_COLS - 1)
    n_dr, n_dc = 2 * NA_WIN_ROWS - 1, 2 * NA_WIN_COLS - 1
    sel_c = (dc[:, :, None] == np.arange(n_dc)) & valid_c[:, :, None]
    blocks = jnp.einsum("hrc,qkc->hrqk", rpb.astype(F32), jnp.asarray(sel_c, F32), precision=HIGHEST)
    blocks = blocks + jnp.asarray(np.where(valid_c, 0.0, MASK_VALUE), F32)
    blocks = jnp.concatenate([blocks, jnp.full((NA_HEADS, 1, GRID_W, GRID_W), MASK_VALUE, F32)], axis=1)
    blocks = blocks * LOG2_E
    which = []
    for rb in (0, 1, rb_count - 1):
        r = rb * NA_QROWS + np.arange(NA_QROWS)
        ks = int(np.clip(rb * NA_QROWS - kr // 2, 0, rows - NA_KROWS))
        key_r = ks + np.arange(NA_KROWS)
        row_start = np.clip(r - kr // 2, 0, rows - kr)
        valid_r = (key_r[None, :] >= row_start[:, None]) & (key_r[None, :] < row_start[:, None] + kr)
        dr = np.clip(key_r[None, :] - r[:, None] + (NA_WIN_ROWS - 1), 0, n_dr - 1)
        which.append(np.where(valid_r, dr, n_dr))
    return _na_bias_assemble(blocks, which)


def _na_bias_kernel(blk_ref, o_ref, *, which):
    for t, table in enumerate(which):
        @pl.when(pl.program_id(0) == t)
        def _():
            for i in range(NA_QROWS):
                row = jnp.concatenate([blk_ref[0, int(table[i, j])] for j in range(NA_KROWS)], axis=1)
                o_ref[0, 0, i * GRID_W:(i + 1) * GRID_W, :] = row


def _na_bias_assemble(blocks, which):
    heads, nblk = blocks.shape[:2]
    return pl.pallas_call(
        functools.partial(_na_bias_kernel, which=which),
        out_shape=jax.ShapeDtypeStruct((len(which), heads, NA_QBLK, NA_KBLK), F32),
        grid=(len(which), heads),
        in_specs=[pl.BlockSpec((1, nblk, GRID_W, GRID_W), lambda t, h: (h, 0, 0, 0))],
        out_specs=pl.BlockSpec((1, 1, NA_QBLK, NA_KBLK), lambda t, h: (t, h, 0, 0)),
        compiler_params=_cparams(("arbitrary", "arbitrary")),
        name="na_bias",
    )(blocks)


def _attn_kernel(q_ref, k_ref, v_ref, kc_ref, vc_ref, bias_ref, o_ref, *, rows):
    last = rows // NA_QROWS - 1
    nt = (((1,), (1,)), ((), ()))
    ctx_len = kc_ref.shape[0]
    low_half = lax.broadcasted_iota(jnp.int32, (NA_QBLK, LANES), 1) < HEAD_DIM
    for blk in range(NA_STEP_BLOCKS):
        rb = pl.program_id(1) * NA_STEP_BLOCKS + blk
        ks = jnp.clip(rb * NA_QROWS - NA_WIN_ROWS // 2, 0, rows - NA_KROWS)
        kstart = pl.multiple_of(ks * GRID_W, GRID_W)
        geom = jnp.where(rb == 0, 0, jnp.where(rb == last, 2, 1))
        qrows = slice(blk * NA_QBLK, (blk + 1) * NA_QBLK)
        for pair in range(NA_HEADS * HEAD_DIM // LANES):
            ls = slice(pair * LANES, (pair + 1) * LANES)
            q2 = q_ref[qrows, ls]
            k_all = jnp.concatenate([kc_ref[:, ls], k_ref[pl.ds(kstart, NA_KBLK), ls]], axis=0)
            v_all = jnp.concatenate([vc_ref[:, ls], v_ref[pl.ds(kstart, NA_KBLK), ls]], axis=0)
            outs = []
            for half in range(2):
                qh = jnp.where(low_half == (half == 0), q2, jnp.zeros_like(q2))
                s = lax.dot_general(qh, k_all, nt, preferred_element_type=F32)
                s = jnp.concatenate([s[:, :ctx_len], s[:, ctx_len:] + bias_ref[geom, 2 * pair + half]], axis=1)
                m = jnp.max(s, axis=-1, keepdims=True)
                p = jnp.exp2(s - m)
                l = jnp.sum(p, axis=-1, keepdims=True)
                outs.append(jnp.dot(p.astype(BF16), v_all, preferred_element_type=F32) / l)
            o_ref[qrows, ls] = jnp.where(low_half, outs[0], outs[1]).astype(o_ref.dtype)


def _attention(q, k, v, kc, vc, bias, batch, n, ctx_len):
    rows = n // GRID_W
    rbc = rows // (NA_QROWS * NA_STEP_BLOCKS)
    qblk = NA_QBLK * NA_STEP_BLOCKS
    return pl.pallas_call(
        functools.partial(_attn_kernel, rows=rows),
        out_shape=jax.ShapeDtypeStruct((batch * n, NA_WIDTH), BF16),
        grid=(batch, rbc),
        in_specs=[pl.BlockSpec((qblk, NA_WIDTH), lambda b, rb: (b * rbc + rb, 0)),
                  pl.BlockSpec((n, NA_WIDTH), lambda b, rb: (b, 0)),
                  pl.BlockSpec((n, NA_WIDTH), lambda b, rb: (b, 0)),
                  pl.BlockSpec((ctx_len, NA_WIDTH), lambda b, rb: (b, 0)),
                  pl.BlockSpec((ctx_len, NA_WIDTH), lambda b, rb: (b, 0)),
                  pl.BlockSpec(bias.shape, lambda b, rb: (0, 0, 0, 0), pipeline_mode=pl.Buffered(1))],
        out_specs=pl.BlockSpec((qblk, NA_WIDTH), lambda b, rb: (b * rbc + rb, 0)),
        compiler_params=_cparams(("arbitrary", "arbitrary")),
        name="na_attention",
    )(q, k, v, kc, vc, bias)


SCAN_GROUPS = 4
SCAN_CHUNKS = SCAN_GROUPS * SUBLANES


def _scan_pitch(n):
    p = -(-n // SCAN_CHUNKS)
    while p % 8 != 4:
        p += 1
    return p


NEG_LOG2_E = -LOG2_E


def _lru_coeff_tile(half_xc, zh, half_bias, k, d):
    c = LRU_CHUNK
    t_r = jnp.tanh(zh[:, (2 * d) * c:(2 * d + 1) * c] + half_bias[:, (2 * d) * c:(2 * d + 1) * c])
    t_i = jnp.tanh(zh[:, (2 * d + 1) * c:(2 * d + 2) * c] + half_bias[:, (2 * d + 1) * c:(2 * d + 2) * c])
    neg_log_a = k[d:d + 1, :] * (1.0 + t_r)
    a = jnp.exp2(neg_log_a * NEG_LOG2_E)
    one_minus_a2 = jnp.tanh(neg_log_a) * (a * a + 1.0)
    root = jnp.where(one_minus_a2 > 0.0, one_minus_a2 * lax.rsqrt(one_minus_a2), 0.0)
    return a, root * (half_xc + half_xc * t_i)


def _conv_tile(xpad, t0, w, b, rows):
    acc = b + w[0:1, :] * xpad[pl.ds(t0 + SUBLANES - 2, rows), :]
    acc = acc + w[1:2, :] * xpad[pl.ds(t0 + SUBLANES - 1, rows), :]
    acc = acc + w[2:3, :] * xpad[pl.ds(t0 + SUBLANES, rows), :]
    return acc + w[3:4, :] * xpad[pl.ds(t0 + SUBLANES + 1, rows), :]


SCAN_UNROLL = 4


def _group_rows(j, g, pitch):
    return pl.ds(g * SUBLANES * pitch + j, SUBLANES, stride=pitch)


def _chunk_totals(af_ref, bf_ref, ab_ref, bb_ref, pitch):
    def body(j, carry):
        jb = pitch - 1 - j
        out = []
        for g in range(SCAN_GROUPS):
            pf, hf, pb, hb = carry[4 * g:4 * g + 4]
            af = af_ref[_group_rows(j, g, pitch), :]
            ab = ab_ref[_group_rows(jb, g, pitch), :]
            out += [af * pf, af * hf + bf_ref[_group_rows(j, g, pitch), :],
                    ab * pb, ab * hb + bb_ref[_group_rows(jb, g, pitch), :]]
        return tuple(out)
    one = jnp.ones((SUBLANES, LRU_CHUNK), F32)
    zero = jnp.zeros((SUBLANES, LRU_CHUNK), F32)
    res = lax.fori_loop(0, pitch, body, (one, zero, one, zero) * SCAN_GROUPS, unroll=SCAN_UNROLL)
    fwd = [(res[4 * g], res[4 * g + 1]) for g in range(SCAN_GROUPS)]
    bwd = [(res[4 * g + 2], res[4 * g + 3]) for g in range(SCAN_GROUPS)]
    return fwd, bwd


def _chunk_starts(totals, h0, reverse):
    row = lax.broadcasted_iota(jnp.int32, (SUBLANES, LRU_CHUNK), 0)
    starts = [jnp.zeros((SUBLANES, LRU_CHUNK), F32) for _ in range(SCAN_GROUPS)]
    state = h0
    order = range(SCAN_CHUNKS - 1, -1, -1) if reverse else range(SCAN_CHUNKS)
    for c in order:
        g, s = divmod(c, SUBLANES)
        p_end, h_end = totals[g]
        starts[g] = jnp.where(row == s, state, starts[g])
        state = p_end[s:s + 1, :] * state + h_end[s:s + 1, :]
    return starts, state


def _scan_write(af_ref, bf_ref, hf_ref, ab_ref, bb_ref, hb_ref, starts_f, starts_b, pitch):
    def body(j, carry):
        jb = pitch - 1 - j
        out = []
        for g in range(SCAN_GROUPS):
            hf, hb = carry[2 * g:2 * g + 2]
            hf = af_ref[_group_rows(j, g, pitch), :] * hf + bf_ref[_group_rows(j, g, pitch), :]
            hb = ab_ref[_group_rows(jb, g, pitch), :] * hb + bb_ref[_group_rows(jb, g, pitch), :]
            hf_ref[_group_rows(j, g, pitch), :] = hf
            hb_ref[_group_rows(jb, g, pitch), :] = hb
            out += [hf, hb]
        return tuple(out)
    init = tuple(v for g in range(SCAN_GROUPS) for v in (starts_f[g], starts_b[g]))
    lax.fori_loop(0, pitch, body, init, unroll=SCAN_UNROLL)


def _lru_kernel(x_ref, g_ref, xc_ref, cw_ref, cb_ref, w_ref, gb_ref, lam_ref, o_ref,
                xpad, a0, b0, a1, b1, h0s, h1s, ca0, cb0, ca1, cb1, *, n, ctx_len):
    pitch = _scan_pitch(n)
    cpitch = _scan_pitch(ctx_len)
    cw = cw_ref[...]
    cb = cb_ref[...]
    gbias = gb_ref[0]
    lam = lam_ref[...]
    sp = jnp.maximum(-lam, 0.0) + jnp.log1p(jnp.exp(-jnp.abs(lam)))
    k = (0.5 * LRU_C) * sp
    wcat = w_ref[0]
    zeros8 = jnp.zeros((SUBLANES, LRU_CHUNK), F32)

    def fill_coeffs(src_rows, total, length, trows, a_refs, b_refs):
        for d in range(2):
            a_refs[d][pl.ds(length, total - length), :] = jnp.ones((total - length, LRU_CHUNK), F32)
            b_refs[d][pl.ds(length, total - length), :] = jnp.zeros((total - length, LRU_CHUNK), F32)
        xpad[pl.ds(0, SUBLANES), :] = zeros8
        xpad[pl.ds(SUBLANES + length, SUBLANES), :] = zeros8
        xpad[pl.ds(SUBLANES, length), :] = src_rows

        def tile(t, carry):
            t0 = pl.multiple_of(t * trows, SUBLANES)
            xc = _conv_tile(xpad, t0, cw, cb, trows)
            zh = jnp.dot(xc.astype(BF16), wcat, preferred_element_type=F32)
            half_xc = 0.5 * xc
            for d in range(2):
                a, b = _lru_coeff_tile(half_xc, zh, gbias, k, d)
                a_refs[d][pl.ds(t0, trows), :] = a
                b_refs[d][pl.ds(t0, trows), :] = b
            return carry
        lax.fori_loop(0, length // trows, tile, 0)

    fill_coeffs(xc_ref[0], SCAN_CHUNKS * cpitch, ctx_len, ctx_len, (ca0, ca1), (cb0, cb1))
    zero_state = jnp.zeros((1, LRU_CHUNK), F32)
    fwd, bwd = _chunk_totals(ca0, cb0, ca1, cb1, cpitch)
    _, init_f = _chunk_starts(fwd, zero_state, reverse=False)
    _, init_b = _chunk_starts(bwd, zero_state, reverse=True)

    fill_coeffs(x_ref[0], SCAN_CHUNKS * pitch, n, LRU_TROWS, (a0, a1), (b0, b1))
    fwd, bwd = _chunk_totals(a0, b0, a1, b1, pitch)
    starts_f, _ = _chunk_starts(fwd, init_f, reverse=False)
    starts_b, _ = _chunk_starts(bwd, init_b, reverse=True)
    _scan_write(a0, b0, h0s, a1, b1, h1s, starts_f, starts_b, pitch)

    def out_tile(t, carry):
        t0 = pl.multiple_of(t * LRU_TROWS, SUBLANES)
        y = h0s[pl.ds(t0, LRU_TROWS), :] + h1s[pl.ds(t0, LRU_TROWS), :]
        g = g_ref[0, pl.ds(t0, LRU_TROWS), :]
        gelu = 0.5 * g * (1.0 + jnp.tanh(0.7978845608028654 * (g + 0.044715 * (g * g * g))))
        o_ref[0, pl.ds(t0, LRU_TROWS), :] = (gelu * y).astype(o_ref.dtype)
        return carry
    lax.fori_loop(0, n // LRU_TROWS, out_tile, 0)


def _lru(xb, gb, xb_ctx, conv_w, conv_b, wcat, gbias, lam, batch, n, ctx_len):
    nch = LRU_WIDTH // LRU_CHUNK
    pitch = _scan_pitch(n)
    cpitch = _scan_pitch(ctx_len)
    big = pltpu.VMEM((SCAN_CHUNKS * pitch, LRU_CHUNK), F32)
    small = pltpu.VMEM((SCAN_CHUNKS * cpitch, LRU_CHUNK), F32)
    return pl.pallas_call(
        functools.partial(_lru_kernel, n=n, ctx_len=ctx_len),
        out_shape=jax.ShapeDtypeStruct((nch, batch * n, LRU_CHUNK), BF16),
        grid=(batch, nch),
        in_specs=[pl.BlockSpec((1, n, LRU_CHUNK), lambda b, c: (c, b, 0)),
                  pl.BlockSpec((1, n, LRU_CHUNK), lambda b, c: (c, b, 0)),
                  pl.BlockSpec((1, ctx_len, LRU_CHUNK), lambda b, c: (c, b, 0)),
                  pl.BlockSpec((4, LRU_CHUNK), lambda b, c: (0, c)),
                  pl.BlockSpec((1, LRU_CHUNK), lambda b, c: (0, c)),
                  pl.BlockSpec((1, LRU_CHUNK, 4 * LRU_CHUNK), lambda b, c: (c, 0, 0)),
                  pl.BlockSpec((1, 1, 4 * LRU_CHUNK), lambda b, c: (c, 0, 0)),
                  pl.BlockSpec((2, LRU_CHUNK), lambda b, c: (0, c))],
        out_specs=pl.BlockSpec((1, n, LRU_CHUNK), lambda b, c: (c, b, 0)),
        scratch_shapes=[pltpu.VMEM((n + 2 * SUBLANES, LRU_CHUNK), F32),
                        big, big, big, big, big, big, small, small, small, small],
        compiler_params=_cparams(("arbitrary", "arbitrary")),
        name="rglru",
    )(xb, gb, xb_ctx, conv_w, conv_b, wcat, gbias, lam)


def _lru_gate_weights(w_r, b_r, w_i, b_i):
    nch = LRU_WIDTH // LRU_CHUNK
    bpc = LRU_CHUNK // LRU_BLOCK

    def dense(w):
        wc = w.reshape(nch, bpc, LRU_BLOCK, LRU_BLOCK)
        eye = jnp.eye(bpc, dtype=w.dtype)
        return jnp.einsum("cbij,bd->cbidj", wc, eye).reshape(nch, LRU_CHUNK, LRU_CHUNK)

    wcat = jnp.concatenate([dense(w_r[0]), dense(w_i[0]), dense(w_r[1]), dense(w_i[1])], axis=-1)
    chunk = lambda v: v.reshape(nch, 1, LRU_CHUNK)
    gbias = jnp.concatenate([chunk(b_r[0]), chunk(b_i[0]), chunk(b_r[1]), chunk(b_i[1])], axis=-1)
    return (0.5 * wcat).astype(BF16), (0.5 * gbias).astype(F32)


def _route(s, sel, route_ref):
    srow = [s[e:e + 1, :] for e in range(N_EXPERTS)]
    lrow = [sel[e:e + 1, :] for e in range(N_EXPERTS)]
    gscore = []
    for g in range(N_EXPERT_GROUPS):
        a = lrow[g * EXPERTS_PER_GROUP:(g + 1) * EXPERTS_PER_GROUP]
        best = a[0] + a[1]
        for i, j in ((0, 2), (0, 3), (1, 2), (1, 3), (2, 3)):
            best = jnp.maximum(best, a[i] + a[j])
        gscore.append(best)
    bg = jnp.zeros_like(gscore[0], dtype=jnp.int32)
    bv = gscore[0]
    for g in range(1, N_EXPERT_GROUPS):
        upd = gscore[g] > bv
        bg = jnp.where(upd, g, bg)
        bv = jnp.where(upd, gscore[g], bv)

    def pick(rows_):
        out = []
        for j in range(EXPERTS_PER_GROUP):
            v = rows_[j]
            for g in range(1, N_EXPERT_GROUPS):
                v = jnp.where(bg == g, rows_[g * EXPERTS_PER_GROUP + j], v)
            out.append(v)
        return out
    cand = pick(lrow)
    cs = pick(srow)
    i1 = jnp.zeros_like(bg)
    v1 = cand[0]
    w1 = cs[0]
    for j in range(1, EXPERTS_PER_GROUP):
        upd = cand[j] > v1
        i1 = jnp.where(upd, j, i1)
        v1 = jnp.where(upd, cand[j], v1)
        w1 = jnp.where(upd, cs[j], w1)
    i2 = jnp.full_like(bg, -1)
    v2 = jnp.full_like(v1, -jnp.inf)
    w2 = jnp.zeros_like(w1)
    for j in range(EXPERTS_PER_GROUP):
        upd = (i1 != j) & (cand[j] > v2)
        i2 = jnp.where(upd, j, i2)
        v2 = jnp.where(upd, cand[j], v2)
        w2 = jnp.where(upd, cs[j], w2)
    den = w1 + w2
    g1 = w1 / den
    g2 = w2 / den
    for j in range(EXPERTS_PER_GROUP):
        route_ref[j:j + 1, :] = jnp.where(i1 == j, g1, 0.0) + jnp.where(i2 == j, g2, 0.0)
    route_ref[ROUTE_GID_ROW:ROUTE_GID_ROW + 1, :] = bg.astype(F32)
    pad = SUBLANES - ROUTE_GID_ROW - 1
    route_ref[ROUTE_GID_ROW + 1:, :] = jnp.zeros((pad, bg.shape[1]), F32)


POST_SUBTILE = 512
POST_TILE = 2 * POST_SUBTILE
INPROJ_TILE = 1024


def _post_kernel(*refs, n_parts):
    parts = refs[:n_parts]
    (w_ref, x_ref, gate_ref, shift_ref, scale_ref, gain_ref, rw_ref, rb_ref,
     x1_ref, h2_ref, route_ref, xbuf, xsem) = refs[n_parts:]
    i = pl.program_id(0)
    nt = pl.num_programs(0)
    tm = x1_ref.shape[0]

    def x_copy(tile):
        tile = jnp.minimum(tile, nt - 1)
        r0 = pl.multiple_of(tile * tm, tm)
        ring = tile % X1_RING
        return pltpu.make_async_copy(x_ref.at[pl.ds(r0, tm)], xbuf.at[ring], xsem.at[ring])

    @pl.when(i == 0)
    def _():
        for j in range(X1_RING - 1):
            x_copy(j).start()

    @pl.when(i + X1_RING - 1 < nt)
    def _():
        x_copy(i + X1_RING - 1).start()

    x_copy(i).wait()
    x_tile = xbuf.at[i % X1_RING]
    for sub in range(tm // POST_SUBTILE):
        rows = pl.ds(sub * POST_SUBTILE, POST_SUBTILE)
        pieces = []
        for p in parts:
            pieces += [p[c, rows, :] for c in range(p.shape[0])] if len(p.shape) == 3 else [p[rows, :]]
        mixed = jnp.concatenate(pieces, axis=-1) if len(pieces) > 1 else pieces[0]
        mix = jnp.dot(mixed, w_ref[...], preferred_element_type=F32)
        x1 = x_tile[rows, :] + gate_ref[0] * mix
        x1_ref[rows, :] = x1
        h2 = _norm_modulate(x1, gain_ref[...], shift_ref[0], scale_ref[0])
        h_hi = h2.astype(BF16)
        h2_ref[rows, :] = h_hi
        logits = jnp.dot(h_hi, rw_ref[...], preferred_element_type=F32)
        s = _sigmoid(logits[:, :LANES].T[:N_EXPERTS, :])
        _route(s, s + rb_ref[...], route_ref.at[:, rows])


def _post_mixer(parts, w, x2d, gate1, shift2, scale2, gain, rw_cat, rbias, tokens_per_batch, tm):
    t, d = x2d.shape
    assert t // tm >= X1_RING - 1
    tpb = tokens_per_batch // tm
    full = lambda i: (0, 0)
    per_b = lambda i: (i // tpb, 0, 0)

    def part_spec(p):
        if p.ndim == 3:
            return pl.BlockSpec((p.shape[0], tm, p.shape[2]), lambda i: (0, i, 0))
        return pl.BlockSpec((tm, p.shape[1]), lambda i: (i, 0))

    return pl.pallas_call(
        functools.partial(_post_kernel, n_parts=len(parts)),
        out_shape=[jax.ShapeDtypeStruct((t, d), F32), jax.ShapeDtypeStruct((t, d), BF16),
                   jax.ShapeDtypeStruct((SUBLANES, t), F32)],
        grid=(t // tm,),
        in_specs=[part_spec(p) for p in parts] + [
                  pl.BlockSpec(w.shape, full),
                  pl.BlockSpec(memory_space=pl.ANY),
                  pl.BlockSpec((1, 1, d), per_b),
                  pl.BlockSpec((1, 1, d), per_b),
                  pl.BlockSpec((1, 1, d), per_b),
                  pl.BlockSpec((1, d), full),
                  pl.BlockSpec(rw_cat.shape, full),
                  pl.BlockSpec(rbias.shape, full)],
        out_specs=[pl.BlockSpec((tm, d), lambda i: (i, 0)),
                   pl.BlockSpec((tm, d), lambda i: (i, 0)),
                   pl.BlockSpec((SUBLANES, tm), lambda i: (0, i))],
        scratch_shapes=[pltpu.VMEM((X1_RING, tm, d), F32), pltpu.SemaphoreType.DMA((X1_RING,))],
        compiler_params=_cparams(("arbitrary",)),
        name="post_mixer",
    )(*parts, w, x2d, gate1, shift2, scale2, gain, rw_cat, rbias)


def _moe_layout(t):
    nt = t // MOE_TILE
    grid = -(-(t + N_EXPERT_GROUPS * (ROW_ALIGN - 1) * nt) // MOE_TILE) + N_EXPERT_GROUPS
    return nt, grid


def _moe_tables(gid, t):
    nt, grid = _moe_layout(t)
    ng = N_EXPERT_GROUPS
    per_tile = MOE_TILE // ROW_ALIGN
    onehot = (gid.reshape(nt, MOE_TILE, 1) == jnp.arange(ng, dtype=jnp.int32)).astype(jnp.int32)
    cnt = onehot.sum(axis=1)
    seg = (cnt + ROW_ALIGN - 1) // ROW_ALIGN
    src = jnp.cumsum(seg, axis=1) - seg
    fill = seg.sum(axis=0)
    ntile = (fill + per_tile - 1) // per_tile
    cum = jnp.cumsum(ntile)
    base = (cum - ntile) * per_tile
    dst = jnp.cumsum(seg, axis=0) - seg + base[None, :]
    seg_tab = jnp.concatenate([seg, src, dst], axis=1).reshape(-1).astype(jnp.int32)
    tail = (-fill) % per_tile
    tail_tab = jnp.concatenate([tail, fill + base, cum[-1:]]).astype(jnp.int32)
    i = jnp.arange(grid, dtype=jnp.int32)
    valid = i < cum[-1]
    ie = jnp.minimum(i, cum[-1] - 1)
    g_of = jnp.sum((ie[:, None] >= cum[None, :]).astype(jnp.int32), axis=1)
    return seg_tab, tail_tab, g_of.astype(jnp.int32), valid.astype(jnp.int32)


def _segment_copies(tab_ref, tile, enable, make_copy):
    ng = N_EXPERT_GROUPS
    base = jnp.maximum(tile, 0) * (3 * ng)
    out = []
    for g in range(ng):
        n = tab_ref[base + g]
        src = tab_ref[base + ng + g]
        dst = tab_ref[base + 2 * ng + g]
        for k in range(MOE_SEG_BITS - 1, -1, -1):
            done = (n >> (k + 1)) << (k + 1)
            rows = ROW_ALIGN << k
            s0 = pl.multiple_of((src + done) * ROW_ALIGN, ROW_ALIGN)
            d0 = pl.multiple_of((dst + done) * ROW_ALIGN, ROW_ALIGN)
            out.append((enable & (((n >> k) & 1) == 1), make_copy(s0, d0, rows)))
    return out


def _start_copies(pairs):
    for cond, copies in pairs:
        @pl.when(cond)
        def _():
            for c in copies:
                c.start()


def _wait_copies(pairs):
    for cond, copies in pairs:
        @pl.when(cond)
        def _():
            for c in copies:
                c.wait()


def _split_bf16x3(x):
    hi = x.astype(BF16).astype(F32)
    r1 = x - hi
    mid = r1.astype(BF16).astype(F32)
    lo = (r1 - mid).astype(BF16).astype(F32)
    return hi, mid, lo


def _dispatch_kernel(seg_ref, tail_ref, h_ref, route_ref, tri_ref, slot_ref, hs_ref, cbuf, zbuf, hbuf, sem, hsem,
                     *, nt):
    i = pl.program_id(0)
    tm, d = hbuf.shape[1:]
    ng = N_EXPERT_GROUPS
    cur = i % 2

    def h_copy(tile):
        tile = jnp.minimum(tile, nt - 1)
        r0 = pl.multiple_of(tile * tm, tm)
        ring = tile % X1_RING
        return pltpu.make_async_copy(h_ref.at[pl.ds(r0, tm)], hbuf.at[ring], hsem.at[ring])

    @pl.when(i == 0)
    def _():
        for j in range(X1_RING - 1):
            h_copy(j).start()

    @pl.when(i + X1_RING - 1 < nt)
    def _():
        h_copy(i + X1_RING - 1).start()

    def seg_copies(tile, enable, buf):
        def seg_copy(s0, d0, rows):
            return (pltpu.make_async_copy(cbuf.at[buf, pl.ds(s0, rows)], hs_ref.at[pl.ds(d0, rows)], sem.at[buf]),)
        return _segment_copies(seg_ref, tile, enable, seg_copy)

    _wait_copies(seg_copies(i - 2, i >= 2, cur))

    route = route_ref[...]
    gid = route[ROUTE_GID_ROW:ROUTE_GID_ROW + 1, :]
    grp = lax.broadcasted_iota(jnp.int32, (SUBLANES, tm), 0).astype(F32)
    onehot = jnp.where(grp == gid, 1.0, 0.0)
    rank = jnp.dot(onehot.astype(BF16), tri_ref[...], preferred_element_type=F32)
    slot = jnp.zeros((1, tm), F32)
    for g in range(ng):
        start = (seg_ref[i * 3 * ng + ng + g] * ROW_ALIGN).astype(F32)
        slot = slot + onehot[g:g + 1, :] * (rank[g:g + 1, :] - 1.0 + start)
    slot_ref[...] = jnp.broadcast_to(slot, (SUBLANES, tm))
    perm = jnp.where(lax.broadcasted_iota(jnp.int32, (MOE_CROWS, tm), 0).astype(F32) == slot, 1.0, 0.0)
    perm = perm.astype(BF16)
    h_copy(i).wait()
    cbuf[cur, :, :d] = jnp.dot(perm, hbuf[i % X1_RING], preferred_element_type=F32).astype(cbuf.dtype)
    parts = jnp.concatenate(list(_split_bf16x3(route)) + [jnp.zeros((LANES - 3 * SUBLANES, tm), F32)], axis=0)
    record = lax.dot_general(perm, parts.astype(BF16), (((1,), (1,)), ((), ())), preferred_element_type=F32)
    cbuf[cur, :, d:] = record.astype(cbuf.dtype)
    _start_copies(seg_copies(i, i >= 0, cur))

    @pl.when(i == pl.num_programs(0) - 1)
    def _():
        _wait_copies(seg_copies(i - 1, i >= 1, 1 - cur))
        _wait_copies(seg_copies(i, i >= 0, cur))
        zbuf[...] = jnp.zeros(zbuf.shape, zbuf.dtype)

        def zero_copy(d0, rows):
            return (pltpu.make_async_copy(zbuf.at[pl.ds(0, rows)], hs_ref.at[pl.ds(d0, rows)], sem.at[0]),)
        pairs = []
        for g in range(ng):
            n = tail_ref[g]
            dst = tail_ref[ng + g]
            for k in range(MOE_TAIL_BITS - 1, -1, -1):
                done = (n >> (k + 1)) << (k + 1)
                d0 = pl.multiple_of((dst + done) * ROW_ALIGN, ROW_ALIGN)
                pairs.append((((n >> k) & 1) == 1, zero_copy(d0, ROW_ALIGN << k)))
        used = tail_ref[2 * ng]
        total = hs_ref.shape[0] // MOE_TILE
        for j in range(total - nt):
            d0 = pl.multiple_of(jnp.minimum(used + j, total - 1) * MOE_TILE, MOE_TILE)
            pairs.append((used + j < total, zero_copy(d0, MOE_TILE)))
        _start_copies(pairs)
        _wait_copies(pairs)


def _dispatch(seg_tab, tail_tab, h2, route, tri):
    t, d = h2.shape
    nt, grid = _moe_layout(t)
    assert nt >= X1_RING - 1
    rows = grid * MOE_TILE
    grid_spec = pltpu.PrefetchScalarGridSpec(
        num_scalar_prefetch=2,
        grid=(nt,),
        in_specs=[pl.BlockSpec(memory_space=pl.ANY),
                  pl.BlockSpec((SUBLANES, MOE_TILE), lambda i, *_: (0, i)),
                  pl.BlockSpec((MOE_TILE, MOE_TILE), lambda i, *_: (0, 0))],
        out_specs=[pl.BlockSpec((SUBLANES, MOE_TILE), lambda i, *_: (0, i)),
                   pl.BlockSpec(memory_space=pl.ANY)],
        scratch_shapes=[pltpu.VMEM((2, MOE_CROWS, d + LANES), BF16), pltpu.VMEM((MOE_TILE, d + LANES), BF16),
                        pltpu.VMEM((X1_RING, MOE_TILE, d), h2.dtype),
                        pltpu.SemaphoreType.DMA((2,)), pltpu.SemaphoreType.DMA((X1_RING,))])
    return pl.pallas_call(
        functools.partial(_dispatch_kernel, nt=nt),
        out_shape=[jax.ShapeDtypeStruct((SUBLANES, t), F32),
                   jax.ShapeDtypeStruct((rows, d + LANES), BF16)],
        grid_spec=grid_spec,
        compiler_params=_cparams(("arbitrary",)),
        name="moe_dispatch",
    )(seg_tab, tail_tab, h2, route, tri)


def _ffn_kernel(grp_ref, valid_ref, h_ref, wg_ref, wu_ref, wd_ref, y_ref):
    i = pl.program_id(0)
    d = y_ref.shape[1]

    @pl.when(valid_ref[i] == 0)
    def _():
        y_ref[...] = jnp.zeros(y_ref.shape, y_ref.dtype)

    @pl.when(valid_ref[i] == 1)
    def _():
        h = h_ref[:, :d]
        gates = h_ref[:, d:].astype(F32)
        acts = []
        for j in range(EXPERTS_PER_GROUP):
            a = jnp.dot(h, wg_ref[j], preferred_element_type=F32)
            u = jnp.dot(h, wu_ref[j], preferred_element_type=F32)
            gate = (gates[:, j:j + 1] + gates[:, SUBLANES + j:SUBLANES + j + 1]
                    + gates[:, 2 * SUBLANES + j:2 * SUBLANES + j + 1])
            acts.append(((a * _sigmoid_tanh(a)) * u * gate).astype(BF16))
        wd = wd_ref[...].reshape(EXPERTS_PER_GROUP * D_FF_EXPERT, d)
        y = jnp.dot(jnp.concatenate(acts, axis=1), wd, preferred_element_type=F32)
        y_ref[...] = y.astype(y_ref.dtype)


def _ffn(grp, valid, hs, wg, wu, wd, layer):
    rows, width = hs.shape
    d = width - LANES
    epg = EXPERTS_PER_GROUP
    w_idx = lambda i, grp, valid: (layer * N_EXPERT_GROUPS + grp[i], 0, 0)
    grid_spec = pltpu.PrefetchScalarGridSpec(
        num_scalar_prefetch=2,
        grid=(rows // MOE_TILE,),
        in_specs=[pl.BlockSpec((MOE_TILE, width), lambda i, grp, valid: (i, 0)),
                  pl.BlockSpec((epg, d, D_FF_EXPERT), w_idx),
                  pl.BlockSpec((epg, d, D_FF_EXPERT), w_idx),
                  pl.BlockSpec((epg, D_FF_EXPERT, d), w_idx)],
        out_specs=pl.BlockSpec((MOE_TILE, d), lambda i, grp, valid: (i, 0)))
    return pl.pallas_call(
        _ffn_kernel,
        out_shape=jax.ShapeDtypeStruct((rows, d), BF16),
        grid_spec=grid_spec,
        compiler_params=_cparams(("arbitrary",)),
        name="moe_ffn",
    )(grp, valid, hs, wg, wu, wd)


def _combine_kernel(seg_ref, x1_ref, slot_ref, gate2_ref, ys_ref, *rest, fnet):
    if fnet:
        shift_ref, scale_ref, gain_ref, cs_ref, o_ref, y1_ref, y2_ref, ybuf, xbuf, sem, xsem = rest
    else:
        o_ref, ybuf, xbuf, sem, xsem = rest
    i = pl.program_id(0)
    nt = pl.num_programs(0)
    tm = o_ref.shape[0]
    cur = i % 2

    def seg_copies(tile, enable, buf):
        def seg_copy(s0, d0, rows):
            return (pltpu.make_async_copy(ys_ref.at[pl.ds(d0, rows)], ybuf.at[buf, pl.ds(s0, rows)], sem.at[buf]),)
        return _segment_copies(seg_ref, tile, enable, seg_copy)

    def x1_copy(tile):
        tile = jnp.minimum(tile, nt - 1)
        r0 = pl.multiple_of(tile * tm, tm)
        ring = tile % X1_RING
        return pltpu.make_async_copy(x1_ref.at[pl.ds(r0, tm)], xbuf.at[ring], xsem.at[ring])

    @pl.when(i == 0)
    def _():
        ybuf[...] = jnp.zeros(ybuf.shape, ybuf.dtype)
        _start_copies(seg_copies(i, i == 0, cur))
        for j in range(X1_RING - 1):
            x1_copy(j).start()

    @pl.when(i + X1_RING - 1 < nt)
    def _():
        x1_copy(i + X1_RING - 1).start()

    nxt = jnp.minimum(i + 1, nt - 1)
    _start_copies(seg_copies(nxt, i + 1 < nt, 1 - cur))
    _wait_copies(seg_copies(i, i >= 0, cur))
    x1_copy(i).wait()
    slot = slot_ref[0:1, :]
    perm = jnp.where(lax.broadcasted_iota(jnp.int32, (MOE_CROWS, tm), 0).astype(F32) == slot, 1.0, 0.0)
    y = lax.dot_general(perm.astype(BF16), ybuf[cur], (((0,), (0,)), ((), ())), preferred_element_type=F32)
    x = xbuf[i % X1_RING] + gate2_ref[0] * y
    o_ref[...] = x
    if fnet:
        _fnet_channel_tile(x, shift_ref, scale_ref, gain_ref, cs_ref, y1_ref, y2_ref)


def _combine(seg_tab, x1, slot, gate2, ys, tokens_per_batch, fnet=None):
    t, d = x1.shape
    tpb = tokens_per_batch // MOE_TILE
    per_b = lambda i, *_: (i // tpb, 0, 0)
    full = lambda i, *_: (0, 0)
    assert t // MOE_TILE >= X1_RING - 1
    in_specs = [pl.BlockSpec(memory_space=pl.ANY),
                pl.BlockSpec((SUBLANES, MOE_TILE), lambda i, *_: (0, i)),
                pl.BlockSpec((1, 1, d), per_b),
                pl.BlockSpec(memory_space=pl.ANY)]
    out_shape = [jax.ShapeDtypeStruct((t, d), F32)]
    out_specs = [pl.BlockSpec((MOE_TILE, d), lambda i, *_: (i, 0))]
    args = [seg_tab, x1, slot, gate2, ys]
    if fnet is not None:
        shift, scale, gain, cs = fnet
        gw = d // FNET_GROUPS
        in_specs += [pl.BlockSpec((1, 1, d), per_b), pl.BlockSpec((1, 1, d), per_b), pl.BlockSpec((1, d), full),
                     pl.BlockSpec(cs.shape, full)]
        out_shape += [jax.ShapeDtypeStruct((FNET_GROUPS, t, gw), BF16)] * 2
        out_specs += [pl.BlockSpec((FNET_GROUPS, MOE_TILE, gw), lambda i, *_: (0, i, 0))] * 2
        args += [shift, scale, gain, cs]
    grid_spec = pltpu.PrefetchScalarGridSpec(
        num_scalar_prefetch=1,
        grid=(t // MOE_TILE,),
        in_specs=in_specs,
        out_specs=out_specs,
        scratch_shapes=[pltpu.VMEM((2, MOE_CROWS, d), BF16), pltpu.VMEM((X1_RING, MOE_TILE, d), F32),
                        pltpu.SemaphoreType.DMA((2,)), pltpu.SemaphoreType.DMA((X1_RING,))])
    out = pl.pallas_call(
        functools.partial(_combine_kernel, fnet=fnet is not None),
        out_shape=out_shape,
        grid_spec=grid_spec,
        compiler_params=_cparams(("arbitrary",)),
        name="moe_combine_fnet" if fnet is not None else "moe_combine",
    )(*args)
    return out if fnet is not None else out[0]


def _grouped_moe(h2, route, x1, gate2, wg, wu, wd, layer, tri):
    t = h2.shape[0]
    gid = route[ROUTE_GID_ROW].astype(jnp.int32)
    seg_tab, tail_tab, grp, valid = _moe_tables(gid, t)
    slot, hs = _dispatch(seg_tab, tail_tab, h2, route, tri)
    ys = _ffn(grp, valid, hs, wg, wu, wd, layer)
    return seg_tab, x1, slot, gate2, ys


FFT_J = SUBLANES
FFT_PAIR_ROWS = 2 * FFT_J * (GRID_W // FFT_J)


def _pair_grid_rows(y):
    rows = y.shape[0]
    assert rows % (2 * GRID_W) == 0
    groups = []
    for u in range(rows // (2 * GRID_W)):
        for sb in range(GRID_W // FFT_J):
            for e in range(2):
                start = GRID_W * (2 * u + e) + FFT_J * sb
                groups.append(y[start:start + FFT_J])
    return jnp.concatenate(groups, axis=0)


def _fnet_channel_tile(x, shift_ref, scale_ref, gain_ref, cs_ref, y1_ref, y2_ref):
    h = _norm_modulate(x, gain_ref[...], shift_ref[0], scale_ref[0]).astype(BF16)
    gw = D_MODEL // FNET_GROUPS
    for g in range(FNET_GROUPS):
        y = _pair_grid_rows(jnp.dot(h[:, g * gw:(g + 1) * gw], cs_ref[...], preferred_element_type=F32))
        y1_ref[g] = y[:, :gw].astype(y1_ref.dtype)
        y2_ref[g] = y[:, gw:].astype(y2_ref.dtype)


def _half_rows(m):
    hc = m // 2 + 1
    return hc, -(-FFT_J * hc // ROW_ALIGN) * ROW_ALIGN


def _mirror(lo, hi, hc, m):
    return jnp.concatenate([lo[:FFT_J * hc]] + [hi[FFT_J * c:FFT_J * (c + 1)] for c in range(m - hc, 0, -1)], axis=0)


def _fnet_pos_kernel(y1_ref, y2_ref, lr_ref, ls_ref, cs_ref, sn_ref, o_ref, z1, a_re, a_im, *, n):
    r1 = n // GRID_W
    gw = y1_ref.shape[2]
    hc_r, part_r = _half_rows(r1)
    hc_s, part_s = _half_rows(GRID_W)

    def stage_r(sb, carry):
        s0 = pl.multiple_of(sb * FFT_J, FFT_J)
        p0 = pl.multiple_of(sb * 2 * FFT_J, 2 * FFT_J)
        rhs1 = jnp.concatenate([y1_ref[0, pl.ds(FFT_PAIR_ROWS * u + p0, 2 * FFT_J), :] for u in range(r1 // 2)], axis=0)
        rhs2 = jnp.concatenate([y2_ref[0, pl.ds(FFT_PAIR_ROWS * u + p0, 2 * FFT_J), :] for u in range(r1 // 2)], axis=0)
        p = jnp.dot(lr_ref[...], rhs1, preferred_element_type=F32)
        q = jnp.dot(lr_ref[...], rhs2, preferred_element_type=F32)
        pc, ps = p[:part_r], p[part_r:]
        qc, qs = q[:part_r], q[part_r:]
        re = _mirror(pc - qs, pc + qs, hc_r, r1)
        nim = _mirror(qc + ps, qc - ps, hc_r, r1)
        cs = jnp.concatenate([cs_ref[sb]] * (gw // LANES), axis=1)
        sn = jnp.concatenate([sn_ref[sb]] * (gw // LANES), axis=1)
        tre = re * cs - nim * sn
        tnim = re * sn + nim * cs
        for c in range(r1):
            a_re[pl.ds(GRID_W * c + s0, FFT_J), :] = tre[FFT_J * c:FFT_J * (c + 1)]
            a_im[pl.ds(GRID_W * c + s0, FFT_J), :] = tnim[FFT_J * c:FFT_J * (c + 1)]
        return carry
    lax.fori_loop(0, GRID_W // FFT_J, stage_r, 0)

    cblk = FFT_J * GRID_W

    def stage_s(cb, carry):
        c0 = pl.multiple_of(cb * cblk, cblk)
        u = jnp.dot(ls_ref[0], a_re[pl.ds(c0, cblk), :].astype(BF16), preferred_element_type=F32)
        v = jnp.dot(ls_ref[1], a_im[pl.ds(c0, cblk), :].astype(BF16), preferred_element_type=F32)
        lo = u - v
        hi = u + v
        k0 = pl.multiple_of(cb * FFT_J, FFT_J)
        for d in range(hc_s):
            z1[pl.ds(r1 * d + k0, FFT_J), :] = lo[FFT_J * d:FFT_J * (d + 1)]
        for d in range(1, GRID_W - hc_s + 1):
            z1[pl.ds(r1 * (GRID_W - d) + k0, FFT_J), :] = hi[FFT_J * d:FFT_J * (d + 1)]
        return carry
    lax.fori_loop(0, r1 // FFT_J, stage_s, 0)
    o_ref[0] = z1[...].astype(o_ref.dtype)


def _fnet_position(y1, y2, lr, ls, tw_cos, tw_sin, batch, n):
    groups, t, gw = y1.shape
    full2 = lambda b, g: (0, 0)
    full3 = lambda b, g: (0, 0, 0)
    scratch = pltpu.VMEM((n, gw), F32)
    return pl.pallas_call(
        functools.partial(_fnet_pos_kernel, n=n),
        out_shape=jax.ShapeDtypeStruct((groups, t, gw), BF16),
        grid=(batch, groups),
        in_specs=[pl.BlockSpec((1, n, gw), lambda b, g: (g, b, 0)),
                  pl.BlockSpec((1, n, gw), lambda b, g: (g, b, 0)),
                  pl.BlockSpec(lr.shape, full2),
                  pl.BlockSpec(ls.shape, full3),
                  pl.BlockSpec(tw_cos.shape, full3),
                  pl.BlockSpec(tw_sin.shape, full3)],
        out_specs=pl.BlockSpec((1, n, gw), lambda b, g: (g, b, 0)),
        scratch_shapes=[scratch, scratch, scratch],
        compiler_params=_cparams(("arbitrary", "arbitrary")),
        name="fnet_position",
    )(y1, y2, lr, ls, tw_cos, tw_sin)


def _dft_tables(n):
    assert n % (GRID_W * FFT_J) == 0
    gw = D_MODEL // FNET_GROUPS
    j = np.arange(gw)
    ang = 2.0 * np.pi * ((j[:, None] * j[None, :]) % gw) / gw
    cs = np.concatenate([np.cos(ang), np.sin(ang)], axis=1) / np.sqrt(gw)
    r1 = n // GRID_W
    assert r1 % 2 == 0
    eye = np.eye(FFT_J)
    scale = float(n) ** -0.25
    a = np.arange(r1)
    hc_r, part_r = _half_rows(r1)
    ang_r = 2.0 * np.pi * ((a[:hc_r, None] * a[None, :]) % r1) / r1
    lr = np.zeros((2 * part_r, FFT_J * r1))
    lr[:FFT_J * hc_r] = np.kron(np.cos(ang_r), eye) * scale
    lr[part_r:part_r + FFT_J * hc_r] = np.kron(np.sin(ang_r), eye) * scale
    s = np.arange(GRID_W)
    hc_s, part_s = _half_rows(GRID_W)
    ang_s = 2.0 * np.pi * ((s[:hc_s, None] * s[None, :]) % GRID_W) / GRID_W
    ls = np.zeros((2, part_s, FFT_J * GRID_W))
    ls[0, :FFT_J * hc_s] = np.einsum("ds,cC->dcCs", np.cos(ang_s), eye).reshape(FFT_J * hc_s, FFT_J * GRID_W) * scale
    ls[1, :FFT_J * hc_s] = np.einsum("ds,cC->dcCs", np.sin(ang_s), eye).reshape(FFT_J * hc_s, FFT_J * GRID_W) * scale
    sb = np.arange(GRID_W // FFT_J)
    s_of = sb[:, None, None] * FFT_J + np.arange(FFT_J)[None, None, :]
    ang_t = 2.0 * np.pi * ((s_of * a[None, :, None]) % n) / n
    ang_t = ang_t.reshape(len(sb), r1 * FFT_J, 1)
    tw_cos = jnp.broadcast_to(jnp.asarray(np.cos(ang_t), F32), (len(sb), r1 * FFT_J, LANES))
    tw_sin = jnp.broadcast_to(jnp.asarray(np.sin(ang_t), F32), (len(sb), r1 * FFT_J, LANES))
    return jnp.asarray(cs, BF16), jnp.asarray(lr, BF16), jnp.asarray(ls, BF16), tw_cos, tw_sin


def kernel(x, c, ctx, c_ctx, ada_w, ada_b, norm_mix, norm_ffn, mix_w_in, mix_w_out, na_q_norm, na_k_norm, na_rpb,
           lru_conv_w, lru_conv_b, lru_gate_r_w, lru_gate_r_b, lru_gate_i_w, lru_gate_i_b, lru_lambda,
           fnet_w_out, router_w, router_bias, moe_w_gate, moe_w_up, moe_w_down):
    batch, n, d = x.shape
    ctx_len = ctx.shape[1]
    depth = ada_w.shape[0]
    rows = n // GRID_W
    assert d == D_MODEL and n % (GRID_W * NA_QROWS) == 0 and rows >= 4 * NA_QROWS
    assert n % MOE_TILE == 0 and n % INPROJ_TILE == 0 and n % POST_TILE == 0 and ctx_len % LANES == 0
    t = batch * n
    tri = jnp.asarray(np.triu(np.ones((MOE_TILE, MOE_TILE))), BF16)

    r_pad = -(-(batch + 1) // SUBLANES) * SUBLANES
    c_rows = jnp.concatenate([c, c_ctx[None, :], jnp.zeros((r_pad - batch - 1, d), c.dtype)], axis=0)
    mod = _modulation(c_rows, ada_w, ada_b)

    def mod_slices(layer):
        m = mod[layer, :batch].reshape(batch, 1, 6, d)
        return [m[:, :, i, :] for i in range(6)]

    rw_cat = jnp.pad(router_w.astype(F32), ((0, 0), (0, 2 * LANES - N_EXPERTS))).astype(BF16)
    rbias = router_bias.reshape(N_EXPERTS, 1).astype(F32)
    x2d = x.reshape(t, d)
    ctx2d = ctx.reshape(batch * ctx_len, d)

    pending = None
    moe_w = None
    for layer in range(depth):
        li = layer // 2
        shift1, scale1, gate1, shift2, scale2, gate2 = mod_slices(layer)
        gain_mix = norm_mix[layer].reshape(1, d)
        gain_ffn = norm_ffn[layer].reshape(1, d)
        if layer % 2 == 0:
            if pending is not None:
                x2d = _combine(*pending, n)
            w_in = mix_w_in[li].astype(BF16)
            ind = jnp.asarray(np.kron(np.eye(NA_HEADS // 2), np.ones((HEAD_DIM, HEAD_DIM))), BF16)
            qg = (jnp.tile(na_q_norm[li], NA_HEADS) * (HEAD_DIM ** -0.5 * LOG2_E)).reshape(1, NA_WIDTH).astype(F32)
            kg = jnp.tile(na_k_norm[li], NA_HEADS).reshape(1, NA_WIDTH).astype(F32)
            side = () if moe_w is not None else (moe_w_gate.reshape((-1,) + moe_w_gate.shape[2:]).astype(F32),
                                                 moe_w_up.reshape((-1,) + moe_w_up.shape[2:]).astype(F32),
                                                 moe_w_down.reshape((-1,) + moe_w_down.shape[2:]).astype(F32))
            q, k, v, xb, gb, *rounded = _inproj(x2d, shift1, scale1, gain_mix, w_in, ind, qg, kg,
                                                ("q", "k", "v", "x", "g"), n, INPROJ_TILE, side=side)
            if rounded:
                moe_w = rounded
            mctx = mod[layer, batch, :2 * d]
            shift_c = jnp.broadcast_to(mctx[:d], (batch, 1, d))
            scale_c = jnp.broadcast_to(mctx[d:], (batch, 1, d))
            k_c, v_c, xb_c = _inproj(ctx2d, shift_c, scale_c, gain_mix, w_in[:, NA_WIDTH:4 * NA_WIDTH], ind, qg, kg,
                                     ("k", "v", "x"), ctx_len, ctx_len)
            bias = _na_bias_tables(na_rpb[li], rows)
            attn = _attention(q, k, v, k_c, v_c, bias, batch, n, ctx_len)
            wcat, gbias = _lru_gate_weights(lru_gate_r_w[li], lru_gate_r_b[li], lru_gate_i_w[li], lru_gate_i_b[li])
            lru = _lru(xb, gb, xb_c, lru_conv_w[li].astype(F32), lru_conv_b[li].reshape(1, LRU_WIDTH).astype(F32),
                       wcat, gbias, lru_lambda[li].astype(F32), batch, n, ctx_len)
            parts, w_out = [attn, lru], mix_w_out[li].astype(BF16)
        else:
            cs, lr, ls, tw_cos, tw_sin = _dft_tables(n)
            x2d, y1, y2 = _combine(*pending, n, fnet=(shift1, scale1, gain_mix, cs))
            parts, w_out = [_fnet_position(y1, y2, lr, ls, tw_cos, tw_sin, batch, n)], fnet_w_out[li].astype(BF16)
        x1, h2, route = _post_mixer(parts, w_out, x2d, gate1, shift2, scale2, gain_ffn, rw_cat, rbias, n,
                                    POST_TILE)
        pending = _grouped_moe(h2, route, x1, gate2, *moe_w, layer, tri)
    return _combine(*pending, n).reshape(batch, n, d)
```
